```python
import jax, jax.numpy as jnp
from jax import lax
import numpy as np

D_MODEL = 1024
BATCH = 16
SEQ = 256
DEPTH = 4
DEC_BATCH = 2
DEC_SEQ = 2048
PAST_LEN = 512

GRID_W = 64
N_MIXERS = 3
ATTN_HEADS = 16
ATTN_KV_HEADS = 4
ATTN_HEAD_DIM = D_MODEL // ATTN_HEADS
ATTN_GROUP = ATTN_HEADS // ATTN_KV_HEADS
ATTN_WIDTH = ATTN_HEADS * ATTN_HEAD_DIM
ATTN_KV_WIDTH = ATTN_KV_HEADS * ATTN_HEAD_DIM
WINDOW = 128
ATTN_BLOCK = 128
BAND_SIDE = WINDOW // ATTN_BLOCK
ROPE_BASE = 10000.0
ROPE_AXIS_DIM = ATTN_HEAD_DIM // 2
POOL_WIDTH = D_MODEL
POOL_WINDOWS = (2, 4, 8, 16)
POOL_GROUPS = len(POOL_WINDOWS)
POOL_GROUP_DIM = POOL_WIDTH // POOL_GROUPS
RET_HEADS = 4
RET_KEY_DIM = D_MODEL // RET_HEADS
RET_VAL_DIM = 2 * RET_KEY_DIM
RET_QK_WIDTH = RET_HEADS * RET_KEY_DIM
RET_V_WIDTH = RET_HEADS * RET_VAL_DIM
RET_CHUNK = 128
N_ATTN = (DEPTH + N_MIXERS - 1) // N_MIXERS
N_POOL = (DEPTH + N_MIXERS - 2) // N_MIXERS
N_RET = DEPTH // N_MIXERS
EPS = 1e-6
NEG_INF = -1e30

kernel_name = "hybrid_flow_ctx_prefix_step"


def rms_norm(x, g):
    xf = x.astype(jnp.float32)
    y = xf * lax.rsqrt(jnp.mean(xf * xf, axis=-1, keepdims=True) + EPS) * g.astype(jnp.float32)
    return y.astype(x.dtype)


def ada_params(cond, w, b):
    m = jax.nn.silu(cond) @ w + b
    return jnp.split(m, 3, axis=-1)


def axial_rope_tables(n_tok):
    n_rows = n_tok // GRID_W
    rows = jnp.repeat(jnp.arange(n_rows), GRID_W).astype(jnp.float32)
    cols = jnp.tile(jnp.arange(GRID_W), n_rows).astype(jnp.float32)
    half = ROPE_AXIS_DIM // 2
    inv = ROPE_BASE ** (-jnp.arange(half, dtype=jnp.float32) / half)
    ang_r = rows[:, None] * inv[None, :]
    ang_c = cols[:, None] * inv[None, :]
    cos = jnp.concatenate([jnp.cos(ang_r), jnp.cos(ang_r), jnp.cos(ang_c), jnp.cos(ang_c)], axis=-1)
    sin = jnp.concatenate([jnp.sin(ang_r), jnp.sin(ang_r), jnp.sin(ang_c), jnp.sin(ang_c)], axis=-1)
    return cos, sin


def apply_rope(x, cos, sin):
    shape = (x.shape[1],) + (1,) * (x.ndim - 3) + (x.shape[-1],)
    c = cos.reshape(shape)
    s = sin.reshape(shape)
    xf = x.astype(jnp.float32)
    x1r, x2r, x1c, x2c = jnp.split(xf, 4, axis=-1)
    rot = jnp.concatenate([-x2r, x1r, -x2c, x1c], axis=-1)
    return (xf * c + rot * s).astype(x.dtype)


def softmax_with_sink(s, sink):
    sk = sink.astype(jnp.float32)[None, :, :, None, None]
    m = jnp.maximum(jnp.max(s, axis=-1, keepdims=True), sk)
    e = jnp.exp(s - m)
    return e / (jnp.sum(e, axis=-1, keepdims=True) + jnp.exp(sk - m))


def attn_project(h, w_in):
    B, L, _ = h.shape
    q, k, v, z = jnp.split(h @ w_in, [ATTN_WIDTH, ATTN_WIDTH + ATTN_KV_WIDTH,
                                     ATTN_WIDTH + 2 * ATTN_KV_WIDTH], axis=-1)
    q = q.reshape(B, L, ATTN_KV_HEADS, ATTN_GROUP, ATTN_HEAD_DIM)
    k = k.reshape(B, L, ATTN_KV_HEADS, ATTN_HEAD_DIM)
    v = v.reshape(B, L, ATTN_KV_HEADS, ATTN_HEAD_DIM)
    return q, k, v, z


def context_attention(q, k, v, sink):
    B, L = q.shape[:2]
    nb = L // ATTN_BLOCK
    scale = ATTN_HEAD_DIM ** -0.5
    qb = q.reshape(B, nb, ATTN_BLOCK, ATTN_KV_HEADS, ATTN_GROUP, ATTN_HEAD_DIM).swapaxes(0, 1)

    def block(qi):
        s = jnp.einsum('bqhgd,bkhd->bhgqk', qi, k).astype(jnp.float32) * scale
        p = softmax_with_sink(s, sink).astype(v.dtype)
        return jnp.einsum('bhgqk,bkhd->bqhgd', p, v)

    o = lax.map(block, qb)
    return o.swapaxes(0, 1).reshape(B, L, ATTN_WIDTH)


def latent_attention(q, k, v, k_ctx, v_ctx, sink):
    B, S = q.shape[:2]
    nb = S // ATTN_BLOCK
    n_ctx = k_ctx.shape[1]
    pad = BAND_SIDE * ATTN_BLOCK
    band = (2 * BAND_SIDE + 1) * ATTN_BLOCK
    scale = ATTN_HEAD_DIM ** -0.5
    kp = jnp.pad(k, ((0, 0), (pad, pad), (0, 0), (0, 0)))
    vp = jnp.pad(v, ((0, 0), (pad, pad), (0, 0), (0, 0)))

    def block(n):
        start = n * ATTN_BLOCK
        qi = lax.dynamic_slice_in_dim(q, start, ATTN_BLOCK, axis=1)
        kb = lax.dynamic_slice_in_dim(kp, start, band, axis=1)
        vb = lax.dynamic_slice_in_dim(vp, start, band, axis=1)
        qpos = start + jnp.arange(ATTN_BLOCK)
        kpos = start - pad + jnp.arange(band)
        mask = ((jnp.abs(qpos[:, None] - kpos[None, :]) <= WINDOW)
                & (kpos >= 0)[None, :] & (kpos < S)[None, :])
        s_ctx = jnp.einsum('bqhgd,bkhd->bhgqk', qi, k_ctx).astype(jnp.float32) * scale
        s_lat = jnp.einsum('bqhgd,bkhd->bhgqk', qi, kb).astype(jnp.float32) * scale
        s_lat = jnp.where(mask, s_lat, NEG_INF)
        p = softmax_with_sink(jnp.concatenate([s_ctx, s_lat], axis=-1), sink).astype(v.dtype)
        return (jnp.einsum('bhgqk,bkhd->bqhgd', p[..., :n_ctx], v_ctx)
                + jnp.einsum('bhgqk,bkhd->bqhgd', p[..., n_ctx:], vb))

    o = lax.map(block, jnp.arange(nb))
    return o.swapaxes(0, 1).reshape(B, S, ATTN_WIDTH)


def attn_mixer_context(h, w_in, w_out, sink):
    q, k, v, z = attn_project(h, w_in)
    o = context_attention(q, k, v, sink.reshape(ATTN_KV_HEADS, ATTN_GROUP))
    return (o * jax.nn.silu(z)) @ w_out, k, v


def attn_mixer_latent(h, k_ctx, v_ctx, w_in, w_out, sink, cos, sin):
    q, k, v, z = attn_project(h, w_in)
    q = apply_rope(q, cos, sin)
    k = apply_rope(k, cos, sin)
    o = latent_attention(q, k, v, k_ctx.astype(k.dtype), v_ctx.astype(v.dtype),
                         sink.reshape(ATTN_KV_HEADS, ATTN_GROUP))
    return (o * jax.nn.silu(z)) @ w_out


def pool_mixer(h, w_in, w_grp, scale, w_out):
    B, L, _ = h.shape
    u, z = jnp.split(h @ w_in, 2, axis=-1)
    uf = u.astype(jnp.float32)
    cs = jnp.concatenate([jnp.zeros_like(uf[:, :1]), jnp.cumsum(uf, axis=1)], axis=1)
    t = jnp.arange(L)
    parts = []
    for g, w in enumerate(POOL_WINDOWS):
        left = w // 2
        right = w - 1 - left
        lo = jnp.clip(t - left, 0, L)
        hi = jnp.clip(t + right + 1, 0, L)
        sl = slice(g * POOL_GROUP_DIM, (g + 1) * POOL_GROUP_DIM)
        csg = cs[..., sl]
        mean = (csg[:, hi] - csg[:, lo]) / (hi - lo).astype(jnp.float32)[None, :, None]
        parts.append(mean - uf[..., sl])
    d = jnp.stack(parts, axis=2).astype(h.dtype)
    y = jnp.einsum('blgc,gcd->blgd', d, w_grp).reshape(B, L, POOL_WIDTH) * scale
    return (y * jax.nn.silu(z)) @ w_out


def retention_scan(q, k, v, log_gamma, s0):
    B, L = q.shape[:2]
    C = RET_CHUNK
    n = L // C
    idx = jnp.arange(C, dtype=jnp.float32)
    diff = idx[:, None] - idx[None, :]
    decay_in = jnp.where(diff[None] >= 0,
                         jnp.exp(jnp.maximum(diff, 0.0)[None] * log_gamma[:, None, None]), 0.0)
    q_dec = jnp.exp((idx + 1.0)[:, None] * log_gamma[None, :])
    k_dec = jnp.exp((C - 1.0 - idx)[:, None] * log_gamma[None, :])
    c_dec = jnp.exp(C * log_gamma)

    def to_chunks(a):
        return a.reshape((B, n, C) + a.shape[2:]).swapaxes(0, 1)

    def step(S, inp):
        qc, kc, vc = inp
        att = jnp.einsum('bihd,bjhd->bhij', qc, kc) * decay_in[None]
        o = (jnp.einsum('bhij,bjhe->bihe', att, vc)
             + jnp.einsum('bihd,bhde->bihe', qc * q_dec[None, :, :, None], S))
        S = S * c_dec[None, :, None, None] + jnp.einsum('bjhd,bjhe->bhde', kc * k_dec[None, :, :, None], vc)
        return S, o

    s_fin, o = lax.scan(step, s0, (to_chunks(q), to_chunks(k), to_chunks(v)))
    return o.swapaxes(0, 1).reshape(B, L, RET_HEADS, RET_VAL_DIM), s_fin


def retention_project(h, w_in):
    B, L, _ = h.shape
    q, k, v, z = jnp.split(h @ w_in, [RET_QK_WIDTH, 2 * RET_QK_WIDTH,
                                     2 * RET_QK_WIDTH + RET_V_WIDTH], axis=-1)
    q = q.reshape(B, L, RET_HEADS, RET_KEY_DIM).astype(jnp.float32)
    k = k.reshape(B, L, RET_HEADS, RET_KEY_DIM).astype(jnp.float32) * (RET_KEY_DIM ** -0.5)
    v = v.reshape(B, L, RET_HEADS, RET_VAL_DIM).astype(jnp.float32)
    return q, k, v, z


def bidir_retention(q, k, v, lg_f, lg_b, s_f, s_b):
    o_f, sf = retention_scan(q, k, v, lg_f, s_f)
    o_b, sb = retention_scan(q[:, ::-1], k[:, ::-1], v[:, ::-1], lg_b, s_b)
    return o_f + o_b[:, ::-1], sf, sb


def retention_output(o, z, gn_g, w_out):
    B, L = o.shape[:2]
    mu = jnp.mean(o, axis=-1, keepdims=True)
    var = jnp.mean(jnp.square(o - mu), axis=-1, keepdims=True)
    on = (o - mu) * lax.rsqrt(var + EPS)
    y = (on.reshape(B, L, RET_V_WIDTH) * gn_g.astype(jnp.float32)).astype(z.dtype)
    return (y * jax.nn.silu(z)) @ w_out


def setup_inputs(seed: int = 0) -> dict:
    key = jax.random.key(seed)
    ks = jax.random.split(key, 24)

    def nrm(k, shape, s):
        return jax.random.normal(k, shape, jnp.float32) * s

    D = D_MODEL
    base_decay = jnp.log(2.0 ** (5.0 + jnp.arange(RET_HEADS, dtype=jnp.float32)) - 1.0)
    return {
        "x_prompt": nrm(ks[0], (BATCH, SEQ, D), 1.0),
        "x_sample": nrm(ks[1], (DEC_BATCH, DEC_SEQ, D), 1.0),
        "cache_k": nrm(ks[2], (DEC_BATCH, N_ATTN, PAST_LEN, ATTN_KV_HEADS, ATTN_HEAD_DIM), 1.0),
        "cache_v": nrm(ks[3], (DEC_BATCH, N_ATTN, PAST_LEN, ATTN_KV_HEADS, ATTN_HEAD_DIM), 1.0),
        "state_fwd": nrm(ks[4], (DEC_BATCH, N_RET, RET_HEADS, RET_KEY_DIM, RET_VAL_DIM), 0.5),
        "state_bwd": nrm(ks[5], (DEC_BATCH, N_RET, RET_HEADS, RET_KEY_DIM, RET_VAL_DIM), 0.5),
        "c": nrm(ks[6], (DEC_BATCH, D), 1.0),
        "c_ctx": nrm(ks[7], (D,), 1.0),
        "norm_g": 1.0 + nrm(ks[8], (DEPTH, D), 0.05),
        "ada_w": nrm(ks[9], (DEPTH, D, 3 * D), D ** -0.5),
        "ada_b": nrm(ks[10], (DEPTH, 3 * D), 0.02),
        "attn_w_in": nrm(ks[11], (N_ATTN, D, 2 * ATTN_WIDTH + 2 * ATTN_KV_WIDTH), D ** -0.5),
        "attn_w_out": nrm(ks[12], (N_ATTN, ATTN_WIDTH, D), ATTN_WIDTH ** -0.5),
        "attn_sink": nrm(ks[13], (N_ATTN, ATTN_HEADS), 0.5),
        "pool_w_in": nrm(ks[14], (N_POOL, D, 2 * POOL_WIDTH), D ** -0.5),
        "pool_w_grp": nrm(ks[15], (N_POOL, POOL_GROUPS, POOL_GROUP_DIM, POOL_GROUP_DIM), POOL_GROUP_DIM ** -0.5),
        "pool_scale": 1.0 + nrm(ks[16], (N_POOL, POOL_WIDTH), 0.1),
        "pool_w_out": nrm(ks[17], (N_POOL, POOL_WIDTH, D), POOL_WIDTH ** -0.5),
        "ret_w_in": nrm(ks[18], (N_RET, D, 2 * RET_QK_WIDTH + 2 * RET_V_WIDTH), D ** -0.5),
        "ret_decay_fwd": base_decay + nrm(ks[19], (N_RET, RET_HEADS), 0.1),
        "ret_decay_bwd": base_decay + nrm(ks[20], (N_RET, RET_HEADS), 0.1),
        "ret_gn_g": 1.0 + nrm(ks[21], (N_RET, RET_V_WIDTH), 0.05),
        "ret_w_out": nrm(ks[22], (N_RET, RET_V_WIDTH, D), RET_V_WIDTH ** -0.5),
        "final_g": 1.0 + nrm(ks[23], (D,), 0.05),
    }


def reference(x_prompt, x_sample, cache_k, cache_v, state_fwd, state_bwd, c, c_ctx,
              norm_g, ada_w, ada_b, attn_w_in, attn_w_out, attn_sink,
              pool_w_in, pool_w_grp, pool_scale, pool_w_out,
              ret_w_in, ret_decay_fwd, ret_decay_bwd, ret_gn_g, ret_w_out, final_g):
    cos, sin = axial_rope_tables(x_sample.shape[1])
    xp, xs = x_prompt, x_sample
    new_k, new_v, new_sf, new_sb = [], [], [], []
    for i in range(DEPTH):
        kind, j = i % N_MIXERS, i // N_MIXERS
        sh_p, sc_p, g_p = ada_params(c_ctx, ada_w[i], ada_b[i])
        sh_s, sc_s, g_s = ada_params(c[:, None, :], ada_w[i], ada_b[i])
        hp = rms_norm(xp, norm_g[i]) * (1.0 + sc_p) + sh_p
        hs = rms_norm(xs, norm_g[i]) * (1.0 + sc_s) + sh_s
        if kind == 0:
            op, kc, vc = attn_mixer_context(hp, attn_w_in[j], attn_w_out[j], attn_sink[j])
            os_ = attn_mixer_latent(hs, cache_k[:, j], cache_v[:, j], attn_w_in[j], attn_w_out[j],
                                    attn_sink[j], cos, sin)
            new_k.append(kc)
            new_v.append(vc)
        elif kind == 1:
            op = pool_mixer(hp, pool_w_in[j], pool_w_grp[j], pool_scale[j], pool_w_out[j])
            os_ = pool_mixer(hs, pool_w_in[j], pool_w_grp[j], pool_scale[j], pool_w_out[j])
        else:
            lg_f = jax.nn.log_sigmoid(ret_decay_fwd[j].astype(jnp.float32))
            lg_b = jax.nn.log_sigmoid(ret_decay_bwd[j].astype(jnp.float32))
            q, k, v, z = retention_project(hp, ret_w_in[j])
            zeros = jnp.zeros((hp.shape[0], RET_HEADS, RET_KEY_DIM, RET_VAL_DIM), jnp.float32)
            o, sf, sb = bidir_retention(q, k, v, lg_f, lg_b, zeros, zeros)
            op = retention_output(o, z, ret_gn_g[j], ret_w_out[j])
            new_sf.append(sf.astype(xp.dtype))
            new_sb.append(sb.astype(xp.dtype))
            q, k, v, z = retention_project(hs, ret_w_in[j])
            o, _, _ = bidir_retention(q, k, v, lg_f, lg_b,
                                      state_fwd[:, j].astype(jnp.float32),
                                      state_bwd[:, j].astype(jnp.float32))
            os_ = retention_output(o, z, ret_gn_g[j], ret_w_out[j])
        xp = xp + g_p * op
        xs = xs + g_s * os_
    y_prompt = rms_norm(xp, final_g)
    y_sample = rms_norm(xs, final_g)
    new_cache_k = jnp.stack(new_k, axis=1)
    new_cache_v = jnp.stack(new_v, axis=1)
    new_state_fwd = jnp.stack(new_sf, axis=1)
    new_state_bwd = jnp.stack(new_sb, axis=1)
    return (y_prompt, y_sample, new_cache_k, new_cache_v, new_state_fwd, new_state_bwd)
```

```python
import functools

import jax
import jax.numpy as jnp
from jax import lax
from jax.experimental import pallas as pl
from jax.experimental.pallas import tpu as pltpu

F32 = jnp.float32
BF16 = jnp.bfloat16

D_MODEL = 1024
BATCH = 16
SEQ = 256
DEPTH = 4
DEC_BATCH = 2
DEC_SEQ = 2048
PAST_LEN = 512
GRID_W = 64
N_MIXERS = 3
ATTN_HEADS = 16
ATTN_KV_HEADS = 4
ATTN_HEAD_DIM = 64
ATTN_GROUP = 4
ATTN_WIDTH = 1024
ATTN_KV_WIDTH = 256
WINDOW = 128
ROPE_BASE = 10000.0
POOL_WINDOWS = (2, 4, 8, 16)
POOL_GROUP_DIM = 256
RET_HEADS = 4
RET_KEY_DIM = 256
RET_VAL_DIM = 512
RET_QK_WIDTH = 1024
RET_V_WIDTH = 2048
EPS = 1e-6
NEG_INF = -1e30

TM = 256
N_PROMPT_TOK = BATCH * SEQ
N_SAMPLE_TOK = DEC_BATCH * DEC_SEQ
N_TOK = N_PROMPT_TOK + N_SAMPLE_TOK
N_TILES = N_TOK // TM
N_PROMPT_TILES = N_PROMPT_TOK // TM
TILES_PER_DEC_SEQ = DEC_SEQ // TM
N_COND = 8
LANES = 128
Q_BLOCK = 128
BAND = 3 * Q_BLOCK
RET_CHUNK = 256
POOL_HALO = 8
VMEM_LIMIT = 56 * 1024 * 1024


def _cond_of_tile(i):
    return jnp.where(i < N_PROMPT_TILES, 0, 1 + (i - N_PROMPT_TILES) // TILES_PER_DEC_SEQ)


def _seq_tile(i):
    return jnp.where(i < N_PROMPT_TILES, 0, (i - N_PROMPT_TILES) % TILES_PER_DEC_SEQ)


def _silu(z):
    return z * (1.0 / (1.0 + jnp.exp(-z)))


def _dot(a, b):
    return jnp.dot(a, b, preferred_element_type=F32)


def _dot_nt(a, b):
    return lax.dot_general(a, b, (((1,), (1,)), ((), ())), preferred_element_type=F32)


def _params(*sem, vmem=VMEM_LIMIT):
    return pltpu.CompilerParams(dimension_semantics=sem, vmem_limit_bytes=vmem)


def _const_spec(shape):
    nd = len(shape)
    return pl.BlockSpec(shape, lambda *_: (0,) * nd, pipeline_mode=pl.Buffered(1))


def _mod_spec(part):
    return pl.BlockSpec((None, 1, D_MODEL), lambda i, *_: (_cond_of_tile(i), 0, part))


def _row_spec(width):
    return pl.BlockSpec((TM, width), lambda i: (i, 0))


def _ada_kernel(cond_ref, w_ref, b_ref, o_ref):
    s = _silu(cond_ref[...])
    o_ref[...] = jnp.dot(s, w_ref[...], preferred_element_type=F32,
                         precision=lax.Precision.HIGHEST) + b_ref[...]


def _ada_table(cond, ada_w, ada_b):
    tn = 768
    return pl.pallas_call(
        _ada_kernel,
        out_shape=jax.ShapeDtypeStruct((DEPTH, N_COND, 3 * D_MODEL), F32),
        grid=(DEPTH, 3 * D_MODEL // tn),
        in_specs=[
            pl.BlockSpec((N_COND, D_MODEL), lambda l, n: (0, 0)),
            pl.BlockSpec((None, D_MODEL, tn), lambda l, n: (l, 0, n)),
            pl.BlockSpec((None, 1, tn), lambda l, n: (l, 0, n)),
        ],
        out_specs=pl.BlockSpec((None, N_COND, tn), lambda l, n: (l, 0, n)),
        compiler_params=_params("arbitrary", "arbitrary"),
        name="ada_table",
    )(cond, ada_w, ada_b.reshape(DEPTH, 1, 3 * D_MODEL))


def _norm_mod(x_ref, g_ref, sh_ref, sc_ref):
    x = x_ref[...]
    y = x * lax.rsqrt(jnp.mean(x * x, axis=-1, keepdims=True) + EPS) * g_ref[...]
    return (y * (1.0 + sc_ref[...]) + sh_ref[...]).astype(BF16)


def _rms(x, g):
    return x * lax.rsqrt(jnp.mean(x * x, axis=-1, keepdims=True) + EPS) * g


def _attn_in_kernel(x_ref, g_ref, sh_ref, sc_ref, w_ref, cos_ref, sin_ref,
                    q_ref, k_ref, v_ref, z_ref):
    h = _norm_mod(x_ref, g_ref, sh_ref, sc_ref)
    cos = cos_ref[...]
    sin = sin_ref[...]
    lane = lax.broadcasted_iota(jnp.int32, (TM, LANES), 1)
    first = (lane % 32) < 16

    def rope(a):
        rot = jnp.where(first, pltpu.roll(a, LANES - 16, 1), pltpu.roll(a, 16, 1))
        return a * cos + rot * sin

    scale = ATTN_HEAD_DIM ** -0.5
    for c in range(ATTN_WIDTH // 256):
        a = _dot(h, w_ref[:, c * 256:(c + 1) * 256])
        for s in range(2):
            lo = c * 256 + s * LANES
            q_ref[:, lo:lo + LANES] = (rope(a[:, s * LANES:(s + 1) * LANES]) * scale).astype(BF16)
    a = _dot(h, w_ref[:, ATTN_WIDTH:ATTN_WIDTH + ATTN_KV_WIDTH])
    for s in range(2):
        k_ref[:, s * LANES:(s + 1) * LANES] = rope(a[:, s * LANES:(s + 1) * LANES])
    v0 = ATTN_WIDTH + ATTN_KV_WIDTH
    v_ref[...] = _dot(h, w_ref[:, v0:v0 + ATTN_KV_WIDTH])
    z0 = ATTN_WIDTH + 2 * ATTN_KV_WIDTH
    for c in range(ATTN_WIDTH // 256):
        z_ref[:, c * 256:(c + 1) * 256] = _dot(h, w_ref[:, z0 + c * 256:z0 + (c + 1) * 256]).astype(BF16)


def _attn_in(x, g, mod, w, cos_t, sin_t):
    n_in = 2 * ATTN_WIDTH + 2 * ATTN_KV_WIDTH
    rope_spec = pl.BlockSpec(
        (TM, LANES), lambda i: (jnp.where(i < N_PROMPT_TILES, 0, 1 + _seq_tile(i)), 0))
    return pl.pallas_call(
        _attn_in_kernel,
        out_shape=(jax.ShapeDtypeStruct((N_TOK, ATTN_WIDTH), BF16),
                   jax.ShapeDtypeStruct((N_TOK, ATTN_KV_WIDTH), F32),
                   jax.ShapeDtypeStruct((N_TOK, ATTN_KV_WIDTH), F32),
                   jax.ShapeDtypeStruct((N_TOK, ATTN_WIDTH), BF16)),
        grid=(N_TILES,),
        in_specs=[_row_spec(D_MODEL), _const_spec((1, D_MODEL)), _mod_spec(0), _mod_spec(1),
                  _const_spec((D_MODEL, n_in)), rope_spec, rope_spec],
        out_specs=(_row_spec(ATTN_WIDTH), _row_spec(ATTN_KV_WIDTH), _row_spec(ATTN_KV_WIDTH),
                   _row_spec(ATTN_WIDTH)),
        compiler_params=_params("arbitrary"),
        name="attn_in",
    )(x, g, mod, mod, w, cos_t, sin_t)


def _rep_head(x, h):
    chunk = lax.broadcasted_iota(jnp.int32, x.shape, 1) // ATTN_HEAD_DIM
    xm = jnp.where(chunk == h, x, 0.0)
    s = xm[:, :LANES] + xm[:, LANES:]
    s = s + pltpu.roll(s, ATTN_HEAD_DIM, 1)
    return jnp.concatenate([s, s], axis=1).astype(BF16)


def _stack_group_queries(q):
    qf = q.astype(F32)
    chunk = lax.broadcasted_iota(jnp.int32, qf.shape, 1) // ATTN_HEAD_DIM
    return jnp.concatenate(
        [jnp.where(chunk == g, qf, 0.0) for g in range(ATTN_GROUP)], axis=0).astype(BF16)


def _gather_group_outputs(o, rows):
    chunk = lax.broadcasted_iota(jnp.int32, (rows, 256), 1) // ATTN_HEAD_DIM
    acc = jnp.zeros((rows, 256), F32)
    for g in range(ATTN_GROUP):
        acc = acc + jnp.where(chunk == g, o[g * rows:(g + 1) * rows], 0.0)
    return acc


def _sink_column(sink_ref, h, rows):
    grp = lax.broadcasted_iota(jnp.int32, (ATTN_GROUP * rows, 1), 0) // rows
    col = jnp.zeros((ATTN_GROUP * rows, 1), F32)
    for g in range(ATTN_GROUP):
        col = jnp.where(grp == g, sink_ref[h * ATTN_GROUP + g], col)
    return col


def _ctx_attn_kernel(sink_ref, q_ref, k_ref, v_ref, z_ref, y_ref):
    h = pl.program_id(1)
    k4 = _rep_head(k_ref[...], h)
    v4 = _rep_head(v_ref[...], h)
    qs = _stack_group_queries(q_ref[...])
    s = _dot_nt(qs, k4)
    sk = _sink_column(sink_ref, h, SEQ)
    m = jnp.maximum(jnp.max(s, axis=1, keepdims=True), sk)
    e = jnp.exp(s - m)
    den = jnp.sum(e, axis=1, keepdims=True) + jnp.exp(sk - m)
    o = _dot(e.astype(BF16), v4) * (1.0 / den)
    acc = _gather_group_outputs(o, SEQ)
    y_ref[...] = (acc * _silu(z_ref[...].astype(F32))).astype(BF16)


def _ctx_attn(sink, q, k, v, z):
    blk = lambda w, col: pl.BlockSpec((SEQ, w), col)
    return pl.pallas_call(
        _ctx_attn_kernel,
        out_shape=jax.ShapeDtypeStruct((N_TOK, ATTN_WIDTH), BF16),
        grid=(BATCH, ATTN_KV_HEADS),
        in_specs=[pl.BlockSpec(memory_space=pltpu.SMEM),
                  blk(256, lambda b, h: (b, h)), blk(256, lambda b, h: (b, 0)),
                  blk(256, lambda b, h: (b, 0)), blk(256, lambda b, h: (b, h))],
        out_specs=blk(256, lambda b, h: (b, h)),
        compiler_params=_params("arbitrary", "arbitrary"),
        name="ctx_attn",
    )(sink, q, k, v, z)


def _lat_attn_kernel(sink_ref, q_ref, k_ref, v_ref, kc_ref, vc_ref, z_ref, yin_ref, y_ref,
                     k4s, v4s, k4c, v4c):
    del yin_ref
    h = pl.program_id(1)
    n = pl.program_id(2)

    @pl.when(n == 0)
    def _():
        for c in range(DEC_SEQ // TM):
            rows = pl.ds(c * TM, TM)
            k4s[rows, :] = _rep_head(k_ref[rows, :], h)
            v4s[rows, :] = _rep_head(v_ref[rows, :], h)
        for c in range(PAST_LEN // TM):
            rows = pl.ds(c * TM, TM)
            k4c[rows, :] = _rep_head(kc_ref[rows, :], h)
            v4c[rows, :] = _rep_head(vc_ref[rows, :], h)

    start = pl.multiple_of(jnp.clip((n - 1) * Q_BLOCK, 0, DEC_SEQ - BAND), Q_BLOCK)
    qs = _stack_group_queries(q_ref[...])
    s_ctx = _dot_nt(qs, k4c[...])
    s_lat = _dot_nt(qs, k4s[pl.ds(start, BAND), :])
    rows = ATTN_GROUP * Q_BLOCK
    qpos = n * Q_BLOCK + lax.broadcasted_iota(jnp.int32, (rows, BAND), 0) % Q_BLOCK
    kpos = start + lax.broadcasted_iota(jnp.int32, (rows, BAND), 1)
    s_lat = jnp.where(jnp.abs(qpos - kpos) <= WINDOW, s_lat, NEG_INF)
    sk = _sink_column(sink_ref, h, Q_BLOCK)
    m = jnp.maximum(jnp.maximum(jnp.max(s_ctx, axis=1, keepdims=True),
                                jnp.max(s_lat, axis=1, keepdims=True)), sk)
    e_ctx = jnp.exp(s_ctx - m)
    e_lat = jnp.exp(s_lat - m)
    den = (jnp.sum(e_ctx, axis=1, keepdims=True) + jnp.sum(e_lat, axis=1, keepdims=True)
           + jnp.exp(sk - m))
    o = _dot(e_ctx.astype(BF16), v4c[...]) + _dot(e_lat.astype(BF16), v4s[pl.ds(start, BAND), :])
    acc = _gather_group_outputs(o * (1.0 / den), Q_BLOCK)
    y_ref[...] = (acc * _silu(z_ref[...].astype(F32))).astype(BF16)


def _lat_attn(sink, q, k, v, kc, vc, z, y):
    qrow = lambda b, n: N_PROMPT_TOK // Q_BLOCK + b * (DEC_SEQ // Q_BLOCK) + n
    seq_spec = pl.BlockSpec((DEC_SEQ, 256), lambda b, h, n: (N_PROMPT_TOK // DEC_SEQ + b, 0))
    ctx_spec = pl.BlockSpec((None, PAST_LEN, 256), lambda b, h, n: (b, 0, 0))
    qz_spec = pl.BlockSpec((Q_BLOCK, 256), lambda b, h, n: (qrow(b, n), h))
    return pl.pallas_call(
        _lat_attn_kernel,
        out_shape=jax.ShapeDtypeStruct((N_TOK, ATTN_WIDTH), BF16),
        grid=(DEC_BATCH, ATTN_KV_HEADS, DEC_SEQ // Q_BLOCK),
        in_specs=[pl.BlockSpec(memory_space=pltpu.SMEM), qz_spec, seq_spec, seq_spec,
                  ctx_spec, ctx_spec, qz_spec, pl.BlockSpec(memory_space=pl.ANY)],
        out_specs=qz_spec,
        scratch_shapes=[pltpu.VMEM((DEC_SEQ, 256), BF16), pltpu.VMEM((DEC_SEQ, 256), BF16),
                        pltpu.VMEM((PAST_LEN, 256), BF16), pltpu.VMEM((PAST_LEN, 256), BF16)],
        input_output_aliases={7: 0},
        compiler_params=_params("arbitrary", "arbitrary", "arbitrary"),
        name="lat_attn",
    )(sink, q, k, v, kc, vc, z, y)


def _out_kernel(y_ref, w_ref, x_ref, gate_ref, o_ref):
    o_ref[...] = x_ref[...] + gate_ref[...] * _dot(y_ref[...], w_ref[...])


def _out_final_kernel(y_ref, w_ref, x_ref, gate_ref, fg_ref, o_ref):
    x = x_ref[...] + gate_ref[...] * _dot(y_ref[...], w_ref[...])
    o_ref[...] = _rms(x, fg_ref[...])


def _out_proj(y, w, x, mod, final_g=None):
    k = y.shape[1]
    in_specs = [_row_spec(k), _const_spec((k, D_MODEL)), _row_spec(D_MODEL), _mod_spec(2)]
    args = [y, w, x, mod]
    kern = _out_kernel
    if final_g is not None:
        in_specs.append(_const_spec((1, D_MODEL)))
        args.append(final_g)
        kern = _out_final_kernel
    return pl.pallas_call(
        kern,
        out_shape=jax.ShapeDtypeStruct((N_TOK, D_MODEL), F32),
        grid=(N_TILES,),
        in_specs=in_specs,
        out_specs=_row_spec(D_MODEL),
        compiler_params=_params("arbitrary"),
        name="out_proj",
    )(*args)


def _pool_in_kernel(x_ref, g_ref, sh_ref, sc_ref, w_ref, u_ref, z_ref):
    h = _norm_mod(x_ref, g_ref, sh_ref, sc_ref)
    for c in range(D_MODEL // 256):
        u_ref[:, c * 256:(c + 1) * 256] = _dot(h, w_ref[:, c * 256:(c + 1) * 256])
    for c in range(D_MODEL // 256):
        lo = D_MODEL + c * 256
        z_ref[:, c * 256:(c + 1) * 256] = _dot(h, w_ref[:, lo:lo + 256]).astype(BF16)


def _pool_in(x, g, mod, w):
    return pl.pallas_call(
        _pool_in_kernel,
        out_shape=(jax.ShapeDtypeStruct((N_TOK, D_MODEL), F32),
                   jax.ShapeDtypeStruct((N_TOK, D_MODEL), BF16)),
        grid=(N_TILES,),
        in_specs=[_row_spec(D_MODEL), _const_spec((1, D_MODEL)), _mod_spec(0), _mod_spec(1),
                  _const_spec((D_MODEL, 2 * D_MODEL))],
        out_specs=(_row_spec(D_MODEL), _row_spec(D_MODEL)),
        compiler_params=_params("arbitrary"),
        name="pool_in",
    )(x, g, mod, mod, w)


def _pool_out_kernel(u_ref, up_ref, un_ref, z_ref, wg_ref, ps_ref, wo_ref, x_ref, gate_ref,
                     o_ref, pad_ref, y_ref):
    i = pl.program_id(0)
    is_dec = i >= N_PROMPT_TILES
    st = _seq_tile(i)
    has_left = jnp.logical_and(is_dec, st != 0)
    has_right = jnp.logical_and(is_dec, st != TILES_PER_DEC_SEQ - 1)
    seq_len = jnp.where(is_dec, DEC_SEQ, SEQ)
    pad_ref[0:POOL_HALO, :] = jnp.where(has_left, up_ref[...], 0.0)
    pad_ref[POOL_HALO:POOL_HALO + TM, :] = u_ref[...]
    pad_ref[POOL_HALO + TM:, :] = jnp.where(has_right, un_ref[...], 0.0)
    t = st * TM + lax.broadcasted_iota(jnp.int32, (TM, 1), 0)
    for g, w in enumerate(POOL_WINDOWS):
        left = w // 2
        right = w - 1 - left
        cols = slice(g * POOL_GROUP_DIM, (g + 1) * POOL_GROUP_DIM)
        s = pad_ref[POOL_HALO - left:POOL_HALO - left + TM, cols]
        for off in range(-left + 1, right + 1):
            s = s + pad_ref[POOL_HALO + off:POOL_HALO + off + TM, cols]
        cnt = (jnp.minimum(t + right + 1, seq_len) - jnp.maximum(t - left, 0)).astype(F32)
        d = (s / cnt - u_ref[:, cols]).astype(BF16)
        yg = _dot(d, wg_ref[g]) * ps_ref[:, cols] * _silu(z_ref[:, cols].astype(F32))
        y_ref[:, cols] = yg.astype(BF16)
    o_ref[...] = x_ref[...] + gate_ref[...] * _dot(y_ref[...], wo_ref[...])


def _pool_out(u, z, wg, ps, wo, x, mod):
    per = TM // POOL_HALO
    n_halo = N_TOK // POOL_HALO
    return pl.pallas_call(
        _pool_out_kernel,
        out_shape=jax.ShapeDtypeStruct((N_TOK, D_MODEL), F32),
        grid=(N_TILES,),
        in_specs=[
            _row_spec(D_MODEL),
            pl.BlockSpec((POOL_HALO, D_MODEL), lambda i: (jnp.maximum(i * per - 1, 0), 0)),
            pl.BlockSpec((POOL_HALO, D_MODEL), lambda i: (jnp.minimum((i + 1) * per, n_halo - 1), 0)),
            _row_spec(D_MODEL),
            _const_spec((len(POOL_WINDOWS), POOL_GROUP_DIM, POOL_GROUP_DIM)),
            _const_spec((1, D_MODEL)),
            _const_spec((D_MODEL, D_MODEL)),
            _row_spec(D_MODEL),
            _mod_spec(2),
        ],
        out_specs=_row_spec(D_MODEL),
        scratch_shapes=[pltpu.VMEM((TM + 2 * POOL_HALO, D_MODEL), F32),
                        pltpu.VMEM((TM, D_MODEL), BF16)],
        compiler_params=_params("arbitrary"),
        name="pool_out",
    )(u, u, u, z, wg, ps, wo, x, mod)


def _ret_in_kernel(x_ref, g_ref, sh_ref, sc_ref, w_ref, q_ref, k_ref, v_ref, z_ref):
    h = _norm_mod(x_ref, g_ref, sh_ref, sc_ref)
    kscale = RET_KEY_DIM ** -0.5
    for c in range(RET_QK_WIDTH // 256):
        q_ref[:, c * 256:(c + 1) * 256] = _dot(h, w_ref[:, c * 256:(c + 1) * 256]).astype(BF16)
    for c in range(RET_QK_WIDTH // 256):
        lo = RET_QK_WIDTH + c * 256
        k_ref[:, c * 256:(c + 1) * 256] = (_dot(h, w_ref[:, lo:lo + 256]) * kscale).astype(BF16)
    for c in range(RET_V_WIDTH // 256):
        lo = 2 * RET_QK_WIDTH + c * 256
        v_ref[:, c * 256:(c + 1) * 256] = _dot(h, w_ref[:, lo:lo + 256]).astype(BF16)
    for c in range(RET_V_WIDTH // 256):
        lo = 2 * RET_QK_WIDTH + RET_V_WIDTH + c * 256
        z_ref[:, c * 256:(c + 1) * 256] = _dot(h, w_ref[:, lo:lo + 256]).astype(BF16)


def _ret_in(x, g, mod, w):
    n_in = 2 * RET_QK_WIDTH + 2 * RET_V_WIDTH
    return pl.pallas_call(
        _ret_in_kernel,
        out_shape=(jax.ShapeDtypeStruct((N_TOK, RET_QK_WIDTH), BF16),
                   jax.ShapeDtypeStruct((N_TOK, RET_QK_WIDTH), BF16),
                   jax.ShapeDtypeStruct((N_TOK, RET_V_WIDTH), BF16),
                   jax.ShapeDtypeStruct((N_TOK, RET_V_WIDTH), BF16)),
        grid=(N_TILES,),
        in_specs=[_row_spec(D_MODEL), _const_spec((1, D_MODEL)), _mod_spec(0), _mod_spec(1),
                  _const_spec((D_MODEL, n_in))],
        out_specs=(_row_spec(RET_QK_WIDTH), _row_spec(RET_QK_WIDTH), _row_spec(RET_V_WIDTH),
                   _row_spec(RET_V_WIDTH)),
        compiler_params=_params("arbitrary"),
        name="ret_in",
    )(x, g, mod, mod, w)


def _pos(shape, axis):
    return lax.broadcasted_iota(jnp.int32, shape, axis).astype(F32)


def _intra_decay(lg_f, lg_b):
    c = RET_CHUNK
    diff = _pos((c, c), 0) - _pos((c, c), 1)
    fwd = jnp.where(diff >= 0, jnp.exp(jnp.maximum(diff, 0.0) * lg_f), 0.0)
    bwd = jnp.where(diff <= 0, jnp.exp(jnp.maximum(-diff, 0.0) * lg_b), 0.0)
    return fwd + bwd


def _state_update(k, v, dec):
    kd = (k.astype(F32) * dec).T.astype(BF16)
    return _dot(kd, v)


def _group_norm_gate(o, gn, z):
    mu = jnp.mean(o, axis=-1, keepdims=True)
    var = jnp.mean(jnp.square(o - mu), axis=-1, keepdims=True)
    on = (o - mu) * lax.rsqrt(var + EPS)
    return (on * gn * _silu(z.astype(F32))).astype(BF16)


def _ret_ctx_kernel(lgf_ref, lgb_ref, q_ref, k_ref, v_ref, z_ref, gn_ref, y_ref, sf_ref, sb_ref):
    h = pl.program_id(1)
    lg_f = lgf_ref[h]
    lg_b = lgb_ref[h]
    c = RET_CHUNK
    q = q_ref[...]
    k = k_ref[...]
    v = v_ref[...]
    att = (_dot_nt(q, k) * _intra_decay(lg_f, lg_b)).astype(BF16)
    o = _dot(att, v)
    y_ref[...] = _group_norm_gate(o, gn_ref[...], z_ref[...])
    j = _pos((c, RET_KEY_DIM), 0)
    sf_ref[...] = _state_update(k, v, jnp.exp((c - 1.0 - j) * lg_f))
    sb_ref[...] = _state_update(k, v, jnp.exp(j * lg_b))


def _ret_ctx(lg_f, lg_b, q, k, v, z, gn):
    smem = pl.BlockSpec(memory_space=pltpu.SMEM)
    qk_spec = pl.BlockSpec((SEQ, RET_KEY_DIM), lambda b, h: (b, h))
    v_spec = pl.BlockSpec((SEQ, RET_VAL_DIM), lambda b, h: (b, h))
    st_spec = pl.BlockSpec((None, None, RET_KEY_DIM, RET_VAL_DIM), lambda b, h: (b, h, 0, 0))
    st_shape = jax.ShapeDtypeStruct((BATCH, RET_HEADS, RET_KEY_DIM, RET_VAL_DIM), F32)
    return pl.pallas_call(
        _ret_ctx_kernel,
        out_shape=(jax.ShapeDtypeStruct((N_TOK, RET_V_WIDTH), BF16), st_shape, st_shape),
        grid=(BATCH, RET_HEADS),
        in_specs=[smem, smem, qk_spec, qk_spec, v_spec, v_spec,
                  pl.BlockSpec((1, RET_VAL_DIM), lambda b, h: (0, h))],
        out_specs=(v_spec, st_spec, st_spec),
        compiler_params=_params("arbitrary", "arbitrary"),
        name="ret_ctx",
    )(lg_f, lg_b, q, k, v, z, gn)


def _ret_lat_kernel(lgf_ref, lgb_ref, q_ref, k_ref, v_ref, z_ref, gn_ref, s0f_ref, s0b_ref,
                    yin_ref, y_ref, sb_all, s_acc):
    del yin_ref
    h = pl.program_id(1)
    lg_f = lgf_ref[h]
    lg_b = lgb_ref[h]
    c = RET_CHUNK
    n_chunks = DEC_SEQ // c
    j = _pos((c, RET_KEY_DIM), 0)

    s_acc[...] = s0b_ref[...]
    kdec_b = jnp.exp(j * lg_b)
    cdec_b = jnp.exp(c * lg_b)

    def bwd_step(it, carry):
        ci = n_chunks - 1 - it
        rows = pl.ds(pl.multiple_of(ci * c, c), c)
        sb_all[ci] = s_acc[...].astype(BF16)
        s_acc[...] = s_acc[...] * cdec_b + _state_update(k_ref[rows, :], v_ref[rows, :], kdec_b)
        return carry

    lax.fori_loop(0, n_chunks, bwd_step, 0)

    s_acc[...] = s0f_ref[...]
    decay = _intra_decay(lg_f, lg_b)
    qdec_f = jnp.exp((j + 1.0) * lg_f)
    qdec_b = jnp.exp((c - j) * lg_b)
    kdec_f = jnp.exp((c - 1.0 - j) * lg_f)
    cdec_f = jnp.exp(c * lg_f)

    def fwd_step(ci, carry):
        rows = pl.ds(pl.multiple_of(ci * c, c), c)
        q = q_ref[rows, :]
        k = k_ref[rows, :]
        v = v_ref[rows, :]
        qf = q.astype(F32)
        att = (_dot_nt(q, k) * decay).astype(BF16)
        o = (_dot(att, v)
             + _dot((qf * qdec_f).astype(BF16), s_acc[...].astype(BF16))
             + _dot((qf * qdec_b).astype(BF16), sb_all[ci]))
        y_ref[rows, :] = _group_norm_gate(o, gn_ref[...], z_ref[rows, :])
        s_acc[...] = s_acc[...] * cdec_f + _state_update(k, v, kdec_f)
        return carry

    lax.fori_loop(0, n_chunks, fwd_step, 0)


def _ret_lat(lg_f, lg_b, q, k, v, z, gn, s0f, s0b, y):
    smem = pl.BlockSpec(memory_space=pltpu.SMEM)
    row0 = N_PROMPT_TOK // DEC_SEQ
    qk_spec = pl.BlockSpec((DEC_SEQ, RET_KEY_DIM), lambda b, h: (row0 + b, h))
    v_spec = pl.BlockSpec((DEC_SEQ, RET_VAL_DIM), lambda b, h: (row0 + b, h))
    st_spec = pl.BlockSpec((None, None, RET_KEY_DIM, RET_VAL_DIM), lambda b, h: (b, h, 0, 0))
    return pl.pallas_call(
        _ret_lat_kernel,
        out_shape=jax.ShapeDtypeStruct((N_TOK, RET_V_WIDTH), BF16),
        grid=(DEC_BATCH, RET_HEADS),
        in_specs=[smem, smem, qk_spec, qk_spec, v_spec, v_spec,
                  pl.BlockSpec((1, RET_VAL_DIM), lambda b, h: (0, h)), st_spec, st_spec,
                  pl.BlockSpec(memory_space=pl.ANY)],
        out_specs=v_spec,
        scratch_shapes=[pltpu.VMEM((DEC_SEQ // RET_CHUNK, RET_KEY_DIM, RET_VAL_DIM), BF16),
                        pltpu.VMEM((RET_KEY_DIM, RET_VAL_DIM), F32)],
        input_output_aliases={9: 0},
        compiler_params=_params("arbitrary", "arbitrary"),
        name="ret_lat",
    )(lg_f, lg_b, q, k, v, z, gn, s0f, s0b, y)


def _rope_tables():
    n_rows = DEC_SEQ // GRID_W
    rows = jnp.repeat(jnp.arange(n_rows), GRID_W).astype(F32)
    cols = jnp.tile(jnp.arange(GRID_W), n_rows).astype(F32)
    half = ATTN_HEAD_DIM // 4
    inv = ROPE_BASE ** (-jnp.arange(half, dtype=F32) / half)
    ang_r = rows[:, None] * inv[None, :]
    ang_c = cols[:, None] * inv[None, :]
    cos = jnp.concatenate([jnp.cos(ang_r), jnp.cos(ang_r), jnp.cos(ang_c), jnp.cos(ang_c)], axis=-1)
    sin = jnp.concatenate([-jnp.sin(ang_r), jnp.sin(ang_r), -jnp.sin(ang_c), jnp.sin(ang_c)], axis=-1)
    cos = jnp.concatenate([jnp.ones((TM, ATTN_HEAD_DIM), F32), cos], axis=0)
    sin = jnp.concatenate([jnp.zeros((TM, ATTN_HEAD_DIM), F32), sin], axis=0)
    return jnp.tile(cos, (1, 2)), jnp.tile(sin, (1, 2))


def kernel(x_prompt, x_sample, cache_k, cache_v, state_fwd, state_bwd, c, c_ctx, norm_g, ada_w, ada_b, attn_w_in, attn_w_out, attn_sink, pool_w_in, pool_w_grp, pool_scale, pool_w_out, ret_w_in, ret_decay_fwd, ret_decay_bwd, ret_gn_g, ret_w_out, final_g):
    x = jnp.concatenate([x_prompt.reshape(N_PROMPT_TOK, D_MODEL),
                         x_sample.reshape(N_SAMPLE_TOK, D_MODEL)], axis=0)
    cond = jnp.concatenate([c_ctx[None, :], c,
                            jnp.zeros((N_COND - 1 - DEC_BATCH, D_MODEL), F32)], axis=0)
    mods = _ada_table(cond, ada_w, ada_b).reshape(DEPTH, N_COND, 1, 3 * D_MODEL)
    cos_t, sin_t = _rope_tables()

    new_k, new_v = [], []
    new_sf = new_sb = None
    for i in range(DEPTH):
        kind, j = i % N_MIXERS, i // N_MIXERS
        g = norm_g[i].reshape(1, D_MODEL)
        mod = mods[i]
        fg = final_g.reshape(1, D_MODEL) if i == DEPTH - 1 else None
        if kind == 0:
            q, k, v, z = _attn_in(x, g, mod, attn_w_in[j].astype(BF16), cos_t, sin_t)
            y = _ctx_attn(attn_sink[j], q, k, v, z)
            y = _lat_attn(attn_sink[j], q, k, v,
                          cache_k[:, j].reshape(DEC_BATCH, PAST_LEN, ATTN_KV_WIDTH),
                          cache_v[:, j].reshape(DEC_BATCH, PAST_LEN, ATTN_KV_WIDTH), z, y)
            x = _out_proj(y, attn_w_out[j].astype(BF16), x, mod, fg)
            new_k.append(k[:N_PROMPT_TOK].reshape(BATCH, SEQ, ATTN_KV_HEADS, ATTN_HEAD_DIM))
            new_v.append(v[:N_PROMPT_TOK].reshape(BATCH, SEQ, ATTN_KV_HEADS, ATTN_HEAD_DIM))
        elif kind == 1:
            u, z = _pool_in(x, g, mod, pool_w_in[j].astype(BF16))
            assert fg is None
            x = _pool_out(u, z, pool_w_grp[j].astype(BF16), pool_scale[j].reshape(1, D_MODEL),
                          pool_w_out[j].astype(BF16), x, mod)
        else:
            lg_f = jax.nn.log_sigmoid(ret_decay_fwd[j].astype(F32))
            lg_b = jax.nn.log_sigmoid(ret_decay_bwd[j].astype(F32))
            gn = ret_gn_g[j].reshape(1, RET_V_WIDTH)
            q, k, v, z = _ret_in(x, g, mod, ret_w_in[j].astype(BF16))
            y, sf, sb = _ret_ctx(lg_f, lg_b, q, k, v, z, gn)
            y = _ret_lat(lg_f, lg_b, q, k, v, z, gn, state_fwd[:, j], state_bwd[:, j], y)
            x = _out_proj(y, ret_w_out[j].astype(BF16), x, mod, fg)
            new_sf, new_sb = sf, sb
    y_prompt = x[:N_PROMPT_TOK].reshape(BATCH, SEQ, D_MODEL)
    y_sample = x[N_PROMPT_TOK:].reshape(DEC_BATCH, DEC_SEQ, D_MODEL)
    return (y_prompt, y_sample, jnp.stack(new_k, axis=1), jnp.stack(new_v, axis=1),
            new_sf[:, None], new_sb[:, None])
```

```python
import functools
from typing import NamedTuple

import jax
import jax.numpy as jnp
from jax import lax
from jax.experimental import pallas as pl
from jax.experimental.pallas import tpu as pltpu

F32 = jnp.float32
BF16 = jnp.bfloat16

D_MODEL = 1024
BATCH = 16
SEQ = 256
DEPTH = 4
DEC_BATCH = 2
DEC_SEQ = 2048
PAST_LEN = 512
GRID_W = 64
N_MIXERS = 3
ATTN_HEADS = 16
ATTN_KV_HEADS = 4
ATTN_HEAD_DIM = 64
ATTN_GROUP = 4
ATTN_WIDTH = 1024
ATTN_KV_WIDTH = 256
WINDOW = 128
ROPE_BASE = 10000.0
POOL_WINDOWS = (2, 4, 8, 16)
POOL_GROUP_DIM = 256
RET_HEADS = 4
RET_KEY_DIM = 256
RET_VAL_DIM = 512
RET_QK_WIDTH = 1024
RET_V_WIDTH = 2048
EPS = 1e-6
NEG_INF = -1e30

N_PROMPT_TOK = BATCH * SEQ
N_SAMPLE_TOK = DEC_BATCH * DEC_SEQ
N_TOK = N_PROMPT_TOK + N_SAMPLE_TOK
N_COND = 8
LANES = 128
MXU_N = 256
Q_BLOCK = 128
Q_SUB = 2
RET_CHUNK = 256
POOL_HALO = 8
VMEM_LIMIT = 48 * 1024 * 1024


class Tiling(NamedTuple):
    tm: int

    @property
    def n_tiles(self):
        return N_TOK // self.tm

    @property
    def n_prompt(self):
        return N_PROMPT_TOK // self.tm

    @property
    def per_dec_seq(self):
        return DEC_SEQ // self.tm

    def cond(self, i):
        return jnp.where(i < self.n_prompt, 0, 1 + (i - self.n_prompt) // self.per_dec_seq)

    def seq_tile(self, i):
        return jnp.where(i < self.n_prompt, 0, (i - self.n_prompt) % self.per_dec_seq)

    def rows(self, width):
        return pl.BlockSpec((self.tm, width), lambda i: (i, 0))

    def prompt_rows(self, width):
        return pl.BlockSpec((self.tm, width), lambda i: (jnp.minimum(i, self.n_prompt - 1), 0))

    def sample_rows(self, width):
        return pl.BlockSpec((self.tm, width), lambda i: (jnp.maximum(i - self.n_prompt, 0), 0))

    def mod(self, part):
        return pl.BlockSpec((None, 1, D_MODEL), lambda i: (self.cond(i), 0, part))


PROJ = Tiling(512)
POOL = Tiling(SEQ)


def _silu(z):
    return z * (1.0 / (1.0 + jnp.exp(-z)))


def _dot(a, b):
    return jnp.dot(a, b, preferred_element_type=F32)


def _dot_nt(a, b):
    return lax.dot_general(a, b, (((1,), (1,)), ((), ())), preferred_element_type=F32)


def _params(*sem, vmem=None):
    return pltpu.CompilerParams(dimension_semantics=sem, vmem_limit_bytes=vmem)


def _const_spec(shape):
    nd = len(shape)
    return pl.BlockSpec(shape, lambda *_: (0,) * nd, pipeline_mode=pl.Buffered(1))


def _x_specs(t, x_parts):
    if len(x_parts) == 1:
        return [t.rows(D_MODEL)]
    return [t.prompt_rows(D_MODEL), t.sample_rows(D_MODEL)]


def _read_x(t, x_refs):
    if len(x_refs) == 1:
        return x_refs[0][...]
    return jnp.where(pl.program_id(0) < t.n_prompt, x_refs[0][...], x_refs[1][...])


def _ada_kernel(cond_ref, w_ref, b_ref, o_ref):
    s = _silu(cond_ref[...])
    o_ref[...] = jnp.dot(s, w_ref[...], preferred_element_type=F32,
                         precision=lax.Precision.HIGHEST) + b_ref[...]


def _ada_table(cond, ada_w, ada_b):
    tn = 768
    return pl.pallas_call(
        _ada_kernel,
        out_shape=jax.ShapeDtypeStruct((DEPTH, N_COND, 3 * D_MODEL), F32),
        grid=(DEPTH, 3 * D_MODEL // tn),
        in_specs=[
            pl.BlockSpec((N_COND, D_MODEL), lambda l, n: (0, 0)),
            pl.BlockSpec((None, D_MODEL, tn), lambda l, n: (l, 0, n)),
            pl.BlockSpec((None, 1, tn), lambda l, n: (l, 0, n)),
        ],
        out_specs=pl.BlockSpec((None, N_COND, tn), lambda l, n: (l, 0, n)),
        compiler_params=_params("arbitrary", "arbitrary"),
        name="ada_table",
    )(cond, ada_w, ada_b.reshape(DEPTH, 1, 3 * D_MODEL))


def _rms(x, g):
    return x * lax.rsqrt(jnp.mean(x * x, axis=-1, keepdims=True) + EPS) * g


def _norm_mod(x, g_ref, sh_ref, sc_ref):
    return (_rms(x, g_ref[...]) * (1.0 + sc_ref[...]) + sh_ref[...]).astype(BF16)


def _proj_chunks(h, w_ref, lo, width, store):
    for c in range(width // MXU_N):
        store(c, _dot(h, w_ref[:, lo + c * MXU_N:lo + (c + 1) * MXU_N]))


def _rep4(a, h):
    half = a[:, (h // 2) * LANES:(h // 2 + 1) * LANES]
    lane = lax.broadcasted_iota(jnp.int32, half.shape, 1)
    keep = (lane < ATTN_HEAD_DIM) if h % 2 == 0 else (lane >= ATTN_HEAD_DIM)
    m = jnp.where(keep, half, 0.0)
    s = m + pltpu.roll(m, ATTN_HEAD_DIM, 1)
    return jnp.concatenate([s, s], axis=1)


def _attn_in_kernel(*refs, n_x):
    t = PROJ
    x_refs, refs = refs[:n_x], refs[n_x:]
    (g_ref, sh_ref, sc_ref, w_ref, cos_ref, sin_ref,
     q_ref, k4_ref, v4_ref, z_ref, kc_ref, vc_ref) = refs
    i = pl.program_id(0)
    h = _norm_mod(_read_x(t, x_refs), g_ref, sh_ref, sc_ref)
    cos = cos_ref[...]
    sin = sin_ref[...]
    lane = lax.broadcasted_iota(jnp.int32, (t.tm, LANES), 1)
    first = (lane % (ATTN_HEAD_DIM // 2)) < ATTN_HEAD_DIM // 4

    def rope(a):
        rot = jnp.where(first, pltpu.roll(a, LANES - ATTN_HEAD_DIM // 4, 1),
                        pltpu.roll(a, ATTN_HEAD_DIM // 4, 1))
        return a * cos + rot * sin

    def rope_wide(a):
        return jnp.concatenate(
            [rope(a[:, s * LANES:(s + 1) * LANES]) for s in range(MXU_N // LANES)], axis=1)

    scale = ATTN_HEAD_DIM ** -0.5

    def store_q(c, a):
        q_ref[:, c * MXU_N:(c + 1) * MXU_N] = (rope_wide(a) * scale).astype(BF16)

    def store_z(c, a):
        z_ref[:, c * MXU_N:(c + 1) * MXU_N] = a.astype(BF16)

    _proj_chunks(h, w_ref, 0, ATTN_WIDTH, store_q)
    k = rope_wide(_dot(h, w_ref[:, ATTN_WIDTH:ATTN_WIDTH + ATTN_KV_WIDTH]))
    v = _dot(h, w_ref[:, ATTN_WIDTH + ATTN_KV_WIDTH:ATTN_WIDTH + 2 * ATTN_KV_WIDTH])
    for hh in range(ATTN_KV_HEADS):
        k4_ref[:, hh * MXU_N:(hh + 1) * MXU_N] = _rep4(k, hh).astype(BF16)
        v4_ref[:, hh * MXU_N:(hh + 1) * MXU_N] = _rep4(v, hh).astype(BF16)
    _proj_chunks(h, w_ref, ATTN_WIDTH + 2 * ATTN_KV_WIDTH, ATTN_WIDTH, store_z)

    @pl.when(i < t.n_prompt)
    def _():
        kc_ref[...] = k
        vc_ref[...] = v


def _attn_in(x_parts, g, mod, w, cos_t, sin_t):
    t = PROJ
    n_in = 2 * ATTN_WIDTH + 2 * ATTN_KV_WIDTH
    rope_spec = pl.BlockSpec(
        (t.tm, LANES), lambda i: (jnp.where(i < t.n_prompt, 0, 1 + t.seq_tile(i)), 0))
    wide = jax.ShapeDtypeStruct((N_TOK, ATTN_WIDTH), BF16)
    cache = jax.ShapeDtypeStruct((N_PROMPT_TOK, ATTN_KV_WIDTH), F32)
    return pl.pallas_call(
        functools.partial(_attn_in_kernel, n_x=len(x_parts)),
        out_shape=(wide, wide, wide, wide, cache, cache),
        grid=(t.n_tiles,),
        in_specs=_x_specs(t, x_parts) + [
            _const_spec((1, D_MODEL)), t.mod(0), t.mod(1), _const_spec((D_MODEL, n_in)),
            rope_spec, rope_spec],
        out_specs=(t.rows(ATTN_WIDTH),) * 4 + (t.prompt_rows(ATTN_KV_WIDTH),) * 2,
        compiler_params=_params("arbitrary", vmem=VMEM_LIMIT),
        name="attn_in",
    )(*x_parts, g, mod, mod, w, cos_t, sin_t)


def _rep_head(x, h):
    chunk = lax.broadcasted_iota(jnp.int32, x.shape, 1) // ATTN_HEAD_DIM
    xm = jnp.where(chunk == h, x, 0.0)
    s = xm[:, :LANES] + xm[:, LANES:]
    s = s + pltpu.roll(s, ATTN_HEAD_DIM, 1)
    return jnp.concatenate([s, s], axis=1).astype(BF16)


def _stack_group_queries(q):
    qf = q.astype(F32)
    chunk = lax.broadcasted_iota(jnp.int32, qf.shape, 1) // ATTN_HEAD_DIM
    return jnp.concatenate(
        [jnp.where(chunk == g, qf, 0.0) for g in range(ATTN_GROUP)], axis=0).astype(BF16)


def _gather_group_outputs(o, rows):
    chunk = lax.broadcasted_iota(jnp.int32, (rows, MXU_N), 1) // ATTN_HEAD_DIM
    acc = jnp.zeros((rows, MXU_N), F32)
    for g in range(ATTN_GROUP):
        acc = acc + jnp.where(chunk == g, o[g * rows:(g + 1) * rows], 0.0)
    return acc


def _sink_column(sink_ref, h, rows):
    grp = lax.broadcasted_iota(jnp.int32, (ATTN_GROUP * rows, 1), 0) // rows
    col = jnp.zeros((ATTN_GROUP * rows, 1), F32)
    for g in range(ATTN_GROUP):
        col = jnp.where(grp == g, sink_ref[h * ATTN_GROUP + g], col)
    return col


def _softmax_pv(blocks, values, sk):
    m = sk
    for s in blocks:
        m = jnp.maximum(m, jnp.max(s, axis=1, keepdims=True))
    den = jnp.exp(sk - m)
    o = None
    for s, v in zip(blocks, values):
        e = jnp.exp(s - m)
        den = den + jnp.sum(e, axis=1, keepdims=True)
        pv = _dot(e.astype(BF16), v)
        o = pv if o is None else o + pv
    return o * (1.0 / den)


def _ctx_attn_kernel(sink_ref, q_ref, k4_ref, v4_ref, z_ref, y_ref):
    for h in range(ATTN_KV_HEADS):
        cols = slice(h * MXU_N, (h + 1) * MXU_N)
        qs = _stack_group_queries(q_ref[:, cols])
        s = _dot_nt(qs, k4_ref[:, cols])
        o = _softmax_pv([s], [v4_ref[:, cols]], _sink_column(sink_ref, h, SEQ))
        acc = _gather_group_outputs(o, SEQ)
        y_ref[:, cols] = (acc * _silu(z_ref[:, cols].astype(F32))).astype(BF16)


def _ctx_attn(sink, q, k4, v4, z):
    spec = pl.BlockSpec((SEQ, ATTN_WIDTH), lambda b: (b, 0))
    return pl.pallas_call(
        _ctx_attn_kernel,
        out_shape=jax.ShapeDtypeStruct((N_TOK, ATTN_WIDTH), BF16),
        grid=(BATCH,),
        in_specs=[pl.BlockSpec(memory_space=pltpu.SMEM), spec, spec, spec, spec],
        out_specs=spec,
        compiler_params=_params("arbitrary"),
        name="ctx_attn",
    )(sink, q, k4, v4, z)


def _lat_attn_kernel(sink_ref, q_ref, k4_ref, v4_ref, kc_ref, vc_ref, z_ref, yin_ref, y_ref,
                     k4c, v4c):
    del yin_ref
    h = pl.program_id(1)
    n = pl.program_id(2)
    n_blocks = DEC_SEQ // Q_BLOCK

    @pl.when(n == 0)
    def _():
        for c in range(PAST_LEN // SEQ):
            rows = pl.ds(c * SEQ, SEQ)
            k4c[rows, :] = _rep_head(kc_ref[rows, :], h)
            v4c[rows, :] = _rep_head(vc_ref[rows, :], h)

    rows4 = ATTN_GROUP * Q_BLOCK
    r = lax.broadcasted_iota(jnp.int32, (rows4, Q_BLOCK), 0) % Q_BLOCK
    c = lax.broadcasted_iota(jnp.int32, (rows4, Q_BLOCK), 1)
    in_left = c >= r
    in_right = c <= r
    sk = _sink_column(sink_ref, h, Q_BLOCK)
    for sub in range(Q_SUB):
        qb = n * Q_SUB + sub
        blk = lambda j: pl.ds(pl.multiple_of(jnp.clip(j, 0, n_blocks - 1) * Q_BLOCK, Q_BLOCK), Q_BLOCK)
        band = [blk(qb - 1), blk(qb), blk(qb + 1)]
        kb = jnp.concatenate([k4_ref[rr, :] for rr in band], axis=0)
        vb = jnp.concatenate([v4_ref[rr, :] for rr in band], axis=0)
        qrows = slice(sub * Q_BLOCK, (sub + 1) * Q_BLOCK)
        qs = _stack_group_queries(q_ref[qrows, :])
        s_ctx = _dot_nt(qs, k4c[...])
        s_lat = _dot_nt(qs, kb)
        s_l = jnp.where(jnp.logical_and(in_left, qb > 0), s_lat[:, :Q_BLOCK], NEG_INF)
        s_m = s_lat[:, Q_BLOCK:2 * Q_BLOCK]
        s_r = jnp.where(jnp.logical_and(in_right, qb < n_blocks - 1), s_lat[:, 2 * Q_BLOCK:], NEG_INF)
        s_band = jnp.concatenate([s_l, s_m, s_r], axis=1)
        o = _softmax_pv([s_ctx, s_band], [v4c[...], vb], sk)
        acc = _gather_group_outputs(o, Q_BLOCK)
        y_ref[qrows, :] = (acc * _silu(z_ref[qrows, :].astype(F32))).astype(BF16)


def _lat_attn(sink, q, k4, v4, kc, vc, z, y):
    step = Q_SUB * Q_BLOCK
    qrow = lambda b, n: N_PROMPT_TOK // step + b * (DEC_SEQ // step) + n
    seq_spec = pl.BlockSpec((DEC_SEQ, MXU_N), lambda b, h, n: (N_PROMPT_TOK // DEC_SEQ + b, h))
    ctx_spec = pl.BlockSpec((None, PAST_LEN, ATTN_KV_WIDTH), lambda b, h, n: (b, 0, 0))
    qz_spec = pl.BlockSpec((step, MXU_N), lambda b, h, n: (qrow(b, n), h))
    return pl.pallas_call(
        _lat_attn_kernel,
        out_shape=jax.ShapeDtypeStruct((N_TOK, ATTN_WIDTH), BF16),
        grid=(DEC_BATCH, ATTN_KV_HEADS, DEC_SEQ // step),
        in_specs=[pl.BlockSpec(memory_space=pltpu.SMEM), qz_spec, seq_spec, seq_spec,
                  ctx_spec, ctx_spec, qz_spec, pl.BlockSpec(memory_space=pl.ANY)],
        out_specs=qz_spec,
        scratch_shapes=[pltpu.VMEM((PAST_LEN, MXU_N), BF16), pltpu.VMEM((PAST_LEN, MXU_N), BF16)],
        input_output_aliases={7: 0},
        compiler_params=_params("arbitrary", "arbitrary", "arbitrary"),
        name="lat_attn",
    )(sink, q, k4, v4, kc, vc, z, y)


def _out_kernel(*refs, n_x):
    t = PROJ
    y_ref, w_ref = refs[:2]
    x_refs = refs[2:2 + n_x]
    gate_ref, o_ref = refs[2 + n_x:]
    o_ref[...] = _read_x(t, x_refs) + gate_ref[...] * _dot(y_ref[...], w_ref[...])


def _out_proj(y, w, x_parts, mod):
    t = PROJ
    k = y.shape[1]
    return pl.pallas_call(
        functools.partial(_out_kernel, n_x=len(x_parts)),
        out_shape=jax.ShapeDtypeStruct((N_TOK, D_MODEL), F32),
        grid=(t.n_tiles,),
        in_specs=[t.rows(k), _const_spec((k, D_MODEL))] + _x_specs(t, x_parts) + [t.mod(2)],
        out_specs=t.rows(D_MODEL),
        compiler_params=_params("arbitrary", vmem=VMEM_LIMIT),
        name="out_proj",
    )(y, w, *x_parts, mod)


def _out_final_kernel(y_ref, w_ref, x_ref, gate_ref, fg_ref, op_ref, os_ref):
    t = PROJ
    i = pl.program_id(0)
    r = _rms(x_ref[...] + gate_ref[...] * _dot(y_ref[...], w_ref[...]), fg_ref[...])

    @pl.when(i < t.n_prompt)
    def _():
        op_ref[...] = r

    @pl.when(i >= t.n_prompt)
    def _():
        os_ref[...] = r


def _out_proj_final(y, w, x, mod, final_g):
    t = PROJ
    k = y.shape[1]
    return pl.pallas_call(
        _out_final_kernel,
        out_shape=(jax.ShapeDtypeStruct((N_PROMPT_TOK, D_MODEL), F32),
                   jax.ShapeDtypeStruct((N_SAMPLE_TOK, D_MODEL), F32)),
        grid=(t.n_tiles,),
        in_specs=[t.rows(k), _const_spec((k, D_MODEL)), t.rows(D_MODEL), t.mod(2),
                  _const_spec((1, D_MODEL))],
        out_specs=(t.prompt_rows(D_MODEL), t.sample_rows(D_MODEL)),
        compiler_params=_params("arbitrary", vmem=VMEM_LIMIT),
        name="out_proj_final",
    )(y, w, x, mod, final_g)


def _pool_in_kernel(x_ref, g_ref, sh_ref, sc_ref, w_ref, u_ref, z_ref):
    h = _norm_mod(x_ref[...], g_ref, sh_ref, sc_ref)

    def store_u(c, a):
        u_ref[:, c * MXU_N:(c + 1) * MXU_N] = a

    def store_z(c, a):
        z_ref[:, c * MXU_N:(c + 1) * MXU_N] = a.astype(BF16)

    _proj_chunks(h, w_ref, 0, D_MODEL, store_u)
    _proj_chunks(h, w_ref, D_MODEL, D_MODEL, store_z)


def _pool_in(x, g, mod, w):
    t = PROJ
    return pl.pallas_call(
        _pool_in_kernel,
        out_shape=(jax.ShapeDtypeStruct((N_TOK, D_MODEL), F32),
                   jax.ShapeDtypeStruct((N_TOK, D_MODEL), BF16)),
        grid=(t.n_tiles,),
        in_specs=[t.rows(D_MODEL), _const_spec((1, D_MODEL)), t.mod(0), t.mod(1),
                  _const_spec((D_MODEL, 2 * D_MODEL))],
        out_specs=(t.rows(D_MODEL), t.rows(D_MODEL)),
        compiler_params=_params("arbitrary", vmem=VMEM_LIMIT),
        name="pool_in",
    )(x, g, mod, mod, w)


def _pool_out_kernel(u_ref, up_ref, un_ref, z_ref, wg_ref, ps_ref, wo_ref, x_ref, gate_ref,
                     o_ref, pad_ref, y_ref):
    t = POOL
    i = pl.program_id(0)
    is_dec = i >= t.n_prompt
    st = t.seq_tile(i)
    has_left = jnp.logical_and(is_dec, st != 0)
    has_right = jnp.logical_and(is_dec, st != t.per_dec_seq - 1)
    seq_len = jnp.where(is_dec, DEC_SEQ, SEQ)
    pad_ref[0:POOL_HALO, :] = jnp.where(has_left, up_ref[...], 0.0)
    pad_ref[POOL_HALO:POOL_HALO + t.tm, :] = u_ref[...]
    pad_ref[POOL_HALO + t.tm:, :] = jnp.where(has_right, un_ref[...], 0.0)
    pos = st * t.tm + lax.broadcasted_iota(jnp.int32, (t.tm, 1), 0)
    for g, w in enumerate(POOL_WINDOWS):
        left = w // 2
        right = w - 1 - left
        cols = slice(g * POOL_GROUP_DIM, (g + 1) * POOL_GROUP_DIM)
        s = pad_ref[POOL_HALO - left:POOL_HALO - left + t.tm, cols]
        for off in range(-left + 1, right + 1):
            s = s + pad_ref[POOL_HALO + off:POOL_HALO + off + t.tm, cols]
        cnt = (jnp.minimum(pos + right + 1, seq_len) - jnp.maximum(pos - left, 0)).astype(F32)
        d = (s / cnt - u_ref[:, cols]).astype(BF16)
        yg = _dot(d, wg_ref[g]) * ps_ref[:, cols] * _silu(z_ref[:, cols].astype(F32))
        y_ref[:, cols] = yg.astype(BF16)
    o_ref[...] = x_ref[...] + gate_ref[...] * _dot(y_ref[...], wo_ref[...])


def _pool_out(u, z, wg, ps, wo, x, mod):
    t = POOL
    per = t.tm // POOL_HALO
    n_halo = N_TOK // POOL_HALO
    return pl.pallas_call(
        _pool_out_kernel,
        out_shape=jax.ShapeDtypeStruct((N_TOK, D_MODEL), F32),
        grid=(t.n_tiles,),
        in_specs=[
            t.rows(D_MODEL),
            pl.BlockSpec((POOL_HALO, D_MODEL), lambda i: (jnp.maximum(i * per - 1, 0), 0)),
            pl.BlockSpec((POOL_HALO, D_MODEL), lambda i: (jnp.minimum((i + 1) * per, n_halo - 1), 0)),
            t.rows(D_MODEL),
            _const_spec((len(POOL_WINDOWS), POOL_GROUP_DIM, POOL_GROUP_DIM)),
            _const_spec((1, D_MODEL)),
            _const_spec((D_MODEL, D_MODEL)),
            t.rows(D_MODEL),
            t.mod(2),
        ],
        out_specs=t.rows(D_MODEL),
        scratch_shapes=[pltpu.VMEM((t.tm + 2 * POOL_HALO, D_MODEL), F32),
                        pltpu.VMEM((t.tm, D_MODEL), BF16)],
        compiler_params=_params("arbitrary"),
        name="pool_out",
    )(u, u, u, z, wg, ps, wo, x, mod)


def _ret_in_kernel(x_ref, g_ref, sh_ref, sc_ref, w_ref, q_ref, k_ref, v_ref, z_ref):
    h = _norm_mod(x_ref[...], g_ref, sh_ref, sc_ref)
    kscale = RET_KEY_DIM ** -0.5

    def store(ref, scale=None):
        def f(c, a):
            ref[:, c * MXU_N:(c + 1) * MXU_N] = (a if scale is None else a * scale).astype(BF16)
        return f

    _proj_chunks(h, w_ref, 0, RET_QK_WIDTH, store(q_ref))
    _proj_chunks(h, w_ref, RET_QK_WIDTH, RET_QK_WIDTH, store(k_ref, kscale))
    _proj_chunks(h, w_ref, 2 * RET_QK_WIDTH, RET_V_WIDTH, store(v_ref))
    _proj_chunks(h, w_ref, 2 * RET_QK_WIDTH + RET_V_WIDTH, RET_V_WIDTH, store(z_ref))


def _ret_in(x, g, mod, w):
    t = PROJ
    n_in = 2 * RET_QK_WIDTH + 2 * RET_V_WIDTH
    return pl.pallas_call(
        _ret_in_kernel,
        out_shape=(jax.ShapeDtypeStruct((N_TOK, RET_QK_WIDTH), BF16),
                   jax.ShapeDtypeStruct((N_TOK, RET_QK_WIDTH), BF16),
                   jax.ShapeDtypeStruct((N_TOK, RET_V_WIDTH), BF16),
                   jax.ShapeDtypeStruct((N_TOK, RET_V_WIDTH), BF16)),
        grid=(t.n_tiles,),
        in_specs=[t.rows(D_MODEL), _const_spec((1, D_MODEL)), t.mod(0), t.mod(1),
                  _const_spec((D_MODEL, n_in))],
        out_specs=(t.rows(RET_QK_WIDTH), t.rows(RET_QK_WIDTH), t.rows(RET_V_WIDTH),
                   t.rows(RET_V_WIDTH)),
        compiler_params=_params("arbitrary", vmem=VMEM_LIMIT),
        name="ret_in",
    )(x, g, mod, mod, w)


def _pos(shape, axis):
    return lax.broadcasted_iota(jnp.int32, shape, axis).astype(F32)


def _intra_decay(lg_f, lg_b):
    c = RET_CHUNK
    diff = _pos((c, c), 0) - _pos((c, c), 1)
    fwd = jnp.where(diff >= 0, jnp.exp(jnp.maximum(diff, 0.0) * lg_f), 0.0)
    bwd = jnp.where(diff <= 0, jnp.exp(jnp.maximum(-diff, 0.0) * lg_b), 0.0)
    return fwd + bwd


def _state_update(k, v, dec):
    kd = (k.astype(F32) * dec).T.astype(BF16)
    return _dot(kd, v)


def _group_norm_gate(o, gn, z):
    mu = jnp.mean(o, axis=-1, keepdims=True)
    var = jnp.mean(jnp.square(o - mu), axis=-1, keepdims=True)
    on = (o - mu) * lax.rsqrt(var + EPS)
    return (on * gn * _silu(z.astype(F32))).astype(BF16)


def _ret_ctx_kernel(lgf_ref, lgb_ref, q_ref, k_ref, v_ref, z_ref, gn_ref, y_ref, sf_ref, sb_ref):
    h = pl.program_id(1)
    lg_f = lgf_ref[h]
    lg_b = lgb_ref[h]
    c = RET_CHUNK
    q = q_ref[...]
    k = k_ref[...]
    v = v_ref[...]
    att = (_dot_nt(q, k) * _intra_decay(lg_f, lg_b)).astype(BF16)
    o = _dot(att, v)
    y_ref[...] = _group_norm_gate(o, gn_ref[...], z_ref[...])
    j = _pos((c, RET_KEY_DIM), 0)
    sf_ref[...] = _state_update(k, v, jnp.exp((c - 1.0 - j) * lg_f))
    sb_ref[...] = _state_update(k, v, jnp.exp(j * lg_b))


def _ret_ctx(lg_f, lg_b, q, k, v, z, gn):
    smem = pl.BlockSpec(memory_space=pltpu.SMEM)
    qk_spec = pl.BlockSpec((SEQ, RET_KEY_DIM), lambda b, h: (b, h))
    v_spec = pl.BlockSpec((SEQ, RET_VAL_DIM), lambda b, h: (b, h))
    st_spec = pl.BlockSpec((None, None, RET_KEY_DIM, RET_VAL_DIM), lambda b, h: (b, h, 0, 0))
    st_shape = jax.ShapeDtypeStruct((BATCH, RET_HEADS, RET_KEY_DIM, RET_VAL_DIM), F32)
    return pl.pallas_call(
        _ret_ctx_kernel,
        out_shape=(jax.ShapeDtypeStruct((N_TOK, RET_V_WIDTH), BF16), st_shape, st_shape),
        grid=(BATCH, RET_HEADS),
        in_specs=[smem, smem, qk_spec, qk_spec, v_spec, v_spec,
                  pl.BlockSpec((1, RET_VAL_DIM), lambda b, h: (0, h))],
        out_specs=(v_spec, st_spec, st_spec),
        compiler_params=_params("arbitrary", "arbitrary"),
        name="ret_ctx",
    )(lg_f, lg_b, q, k, v, z, gn)


def _ret_lat_kernel(lgf_ref, lgb_ref, q_ref, k_ref, v_ref, z_ref, gn_ref, s0f_ref, s0b_ref,
                    yin_ref, y_ref, sb_all, s_acc):
    del yin_ref
    h = pl.program_id(1)
    lg_f = lgf_ref[h]
    lg_b = lgb_ref[h]
    c = RET_CHUNK
    n_chunks = DEC_SEQ // c
    j = _pos((c, RET_KEY_DIM), 0)

    s_acc[...] = s0b_ref[...]
    kdec_b = jnp.exp(j * lg_b)
    cdec_b = jnp.exp(c * lg_b)

    def bwd_step(it, carry):
        ci = n_chunks - 1 - it
        rows = pl.ds(pl.multiple_of(ci * c, c), c)
        sb_all[ci] = s_acc[...].astype(BF16)
        s_acc[...] = s_acc[...] * cdec_b + _state_update(k_ref[rows, :], v_ref[rows, :], kdec_b)
        return carry

    lax.fori_loop(0, n_chunks, bwd_step, 0)

    s_acc[...] = s0f_ref[...]
    decay = _intra_decay(lg_f, lg_b)
    qdec_f = jnp.exp((j + 1.0) * lg_f)
    qdec_b = jnp.exp((c - j) * lg_b)
    kdec_f = jnp.exp((c - 1.0 - j) * lg_f)
    cdec_f = jnp.exp(c * lg_f)

    def fwd_step(ci, carry):
        rows = pl.ds(pl.multiple_of(ci * c, c), c)
        q = q_ref[rows, :]
        k = k_ref[rows, :]
        v = v_ref[rows, :]
        qf = q.astype(F32)
        att = (_dot_nt(q, k) * decay).astype(BF16)
        o = (_dot(att, v)
             + _dot((qf * qdec_f).astype(BF16), s_acc[...].astype(BF16))
             + _dot((qf * qdec_b).astype(BF16), sb_all[ci]))
        y_ref[rows, :] = _group_norm_gate(o, gn_ref[...], z_ref[rows, :])
        s_acc[...] = s_acc[...] * cdec_f + _state_update(k, v, kdec_f)
        return carry

    lax.fori_loop(0, n_chunks, fwd_step, 0)


def _ret_lat(lg_f, lg_b, q, k, v, z, gn, s0f, s0b, y):
    smem = pl.BlockSpec(memory_space=pltpu.SMEM)
    row0 = N_PROMPT_TOK // DEC_SEQ
    qk_spec = pl.BlockSpec((DEC_SEQ, RET_KEY_DIM), lambda b, h: (row0 + b, h))
    v_spec = pl.BlockSpec((DEC_SEQ, RET_VAL_DIM), lambda b, h: (row0 + b, h))
    st_spec = pl.BlockSpec((None, None, RET_KEY_DIM, RET_VAL_DIM), lambda b, h: (b, h, 0, 0))
    return pl.pallas_call(
        _ret_lat_kernel,
        out_shape=jax.ShapeDtypeStruct((N_TOK, RET_V_WIDTH), BF16),
        grid=(DEC_BATCH, RET_HEADS),
        in_specs=[smem, smem, qk_spec, qk_spec, v_spec, v_spec,
                  pl.BlockSpec((1, RET_VAL_DIM), lambda b, h: (0, h)), st_spec, st_spec,
                  pl.BlockSpec(memory_space=pl.ANY)],
        out_specs=v_spec,
        scratch_shapes=[pltpu.VMEM((DEC_SEQ // RET_CHUNK, RET_KEY_DIM, RET_VAL_DIM), BF16),
                        pltpu.VMEM((RET_KEY_DIM, RET_VAL_DIM), F32)],
        input_output_aliases={9: 0},
        compiler_params=_params("arbitrary", "arbitrary", vmem=VMEM_LIMIT),
        name="ret_lat",
    )(lg_f, lg_b, q, k, v, z, gn, s0f, s0b, y)


def _rope_tables(tm):
    n_rows = DEC_SEQ // GRID_W
    rows = jnp.repeat(jnp.arange(n_rows), GRID_W).astype(F32)
    cols = jnp.tile(jnp.arange(GRID_W), n_rows).astype(F32)
    half = ATTN_HEAD_DIM // 4
    inv = ROPE_BASE ** (-jnp.arange(half, dtype=F32) / half)
    ang_r = rows[:, None] * inv[None, :]
    ang_c = cols[:, None] * inv[None, :]
    cos = jnp.concatenate([jnp.cos(ang_r), jnp.cos(ang_r), jnp.cos(ang_c), jnp.cos(ang_c)], axis=-1)
    sin = jnp.concatenate([-jnp.sin(ang_r), jnp.sin(ang_r), -jnp.sin(ang_c), jnp.sin(ang_c)], axis=-1)
    cos = jnp.concatenate([jnp.ones((tm, ATTN_HEAD_DIM), F32), cos], axis=0)
    sin = jnp.concatenate([jnp.zeros((tm, ATTN_HEAD_DIM), F32), sin], axis=0)
    return jnp.tile(cos, (1, 2)), jnp.tile(sin, (1, 2))


def kernel(x_prompt, x_sample, cache_k, cache_v, state_fwd, state_bwd, c, c_ctx, norm_g, ada_w, ada_b, attn_w_in, attn_w_out, attn_sink, pool_w_in, pool_w_grp, pool_scale, pool_w_out, ret_w_in, ret_decay_fwd, ret_decay_bwd, ret_gn_g, ret_w_out, final_g):
    x_parts = (x_prompt.reshape(N_PROMPT_TOK, D_MODEL), x_sample.reshape(N_SAMPLE_TOK, D_MODEL))
    cond = jnp.concatenate([c_ctx[None, :], c,
                            jnp.zeros((N_COND - 1 - DEC_BATCH, D_MODEL), F32)], axis=0)
    mods = _ada_table(cond, ada_w, ada_b).reshape(DEPTH, N_COND, 1, 3 * D_MODEL)
    cos_t, sin_t = _rope_tables(PROJ.tm)

    new_k, new_v = [], []
    new_sf = new_sb = None
    for i in range(DEPTH):
        kind, j = i % N_MIXERS, i // N_MIXERS
        g = norm_g[i].reshape(1, D_MODEL)
        mod = mods[i]
        last = i == DEPTH - 1
        if kind == 0:
            q, k4, v4, z, kc, vc = _attn_in(x_parts, g, mod, attn_w_in[j].astype(BF16), cos_t, sin_t)
            y = _ctx_attn(attn_sink[j], q, k4, v4, z)
            y = _lat_attn(attn_sink[j], q, k4, v4,
                          cache_k[:, j].reshape(DEC_BATCH, PAST_LEN, ATTN_KV_WIDTH),
                          cache_v[:, j].reshape(DEC_BATCH, PAST_LEN, ATTN_KV_WIDTH), z, y)
            w_out = attn_w_out[j].astype(BF16)
            new_k.append(kc.reshape(BATCH, SEQ, ATTN_KV_HEADS, ATTN_HEAD_DIM))
            new_v.append(vc.reshape(BATCH, SEQ, ATTN_KV_HEADS, ATTN_HEAD_DIM))
        elif kind == 1:
            (x,) = x_parts
            u, z = _pool_in(x, g, mod, pool_w_in[j].astype(BF16))
            assert not last
            x_parts = (_pool_out(u, z, pool_w_grp[j].astype(BF16), pool_scale[j].reshape(1, D_MODEL),
                                 pool_w_out[j].astype(BF16), x, mod),)
            continue
        else:
            (x,) = x_parts
            lg_f = jax.nn.log_sigmoid(ret_decay_fwd[j].astype(F32))
            lg_b = jax.nn.log_sigmoid(ret_decay_bwd[j].astype(F32))
            gn = ret_gn_g[j].reshape(1, RET_V_WIDTH)
            q, k, v, z = _ret_in(x, g, mod, ret_w_in[j].astype(BF16))
            y, new_sf, new_sb = _ret_ctx(lg_f, lg_b, q, k, v, z, gn)
            y = _ret_lat(lg_f, lg_b, q, k, v, z, gn, state_fwd[:, j], state_bwd[:, j], y)
            w_out = ret_w_out[j].astype(BF16)
        if last:
            (x,) = x_parts
            y_prompt, y_sample = _out_proj_final(y, w_out, x, mod, final_g.reshape(1, D_MODEL))
        else:
            x_parts = (_out_proj(y, w_out, x_parts, mod),)
    return (y_prompt.reshape(BATCH, SEQ, D_MODEL), y_sample.reshape(DEC_BATCH, DEC_SEQ, D_MODEL),
            jnp.stack(new_k, axis=1), jnp.stack(new_v, axis=1), new_sf[:, None], new_sb[:, None])
```

```python
import functools
from typing import NamedTuple

import jax
import jax.numpy as jnp
from jax import lax
from jax.experimental import pallas as pl
from jax.experimental.pallas import tpu as pltpu

F32 = jnp.float32
BF16 = jnp.bfloat16

D_MODEL = 1024
BATCH = 16
SEQ = 256
DEPTH = 4
DEC_BATCH = 2
DEC_SEQ = 2048
PAST_LEN = 512
GRID_W = 64
N_MIXERS = 3
ATTN_HEADS = 16
ATTN_KV_HEADS = 4
ATTN_HEAD_DIM = 64
ATTN_GROUP = 4
ATTN_WIDTH = 1024
ATTN_KV_WIDTH = 256
WINDOW = 128
ROPE_BASE = 10000.0
POOL_WINDOWS = (2, 4, 8, 16)
POOL_GROUP_DIM = 256
RET_HEADS = 4
RET_KEY_DIM = 256
RET_VAL_DIM = 512
RET_QK_WIDTH = 1024
RET_V_WIDTH = 2048
EPS = 1e-6
NEG_INF = -1e30

N_PROMPT_TOK = BATCH * SEQ
N_SAMPLE_TOK = DEC_BATCH * DEC_SEQ
N_TOK = N_PROMPT_TOK + N_SAMPLE_TOK
N_COND = 8
LANES = 128
MXU_N = 256
Q_BLOCK = 128
Q_SUB = 4
RET_CHUNK = 256
RET_TAB_ROWS = 8
POOL_HALO = 8
VMEM_LIMIT = 48 * 1024 * 1024


class Tiling(NamedTuple):
    tm: int

    @property
    def n_tiles(self):
        return N_TOK // self.tm

    @property
    def n_prompt(self):
        return N_PROMPT_TOK // self.tm

    @property
    def per_dec_seq(self):
        return DEC_SEQ // self.tm

    def cond(self, i):
        return jnp.where(i < self.n_prompt, 0, 1 + (i - self.n_prompt) // self.per_dec_seq)

    def seq_tile(self, i):
        return jnp.where(i < self.n_prompt, 0, (i - self.n_prompt) % self.per_dec_seq)

    def rows(self, width):
        return pl.BlockSpec((self.tm, width), lambda i: (i, 0))

    def prompt_rows(self, width):
        return pl.BlockSpec((self.tm, width), lambda i: (jnp.minimum(i, self.n_prompt - 1), 0))

    def sample_rows(self, width):
        return pl.BlockSpec((self.tm, width), lambda i: (jnp.maximum(i - self.n_prompt, 0), 0))

    def mod(self, part):
        return pl.BlockSpec((None, 1, D_MODEL), lambda i: (self.cond(i), 0, part))


PROJ = Tiling(512)
POOL = Tiling(SEQ)


def _silu(z):
    return z * (1.0 / (1.0 + jnp.exp(-z)))


def _dot(a, b):
    return jnp.dot(a, b, preferred_element_type=F32)


def _dot_nt(a, b):
    return lax.dot_general(a, b, (((1,), (1,)), ((), ())), preferred_element_type=F32)


def _params(*sem, vmem=None):
    return pltpu.CompilerParams(dimension_semantics=sem, vmem_limit_bytes=vmem)


def _const_spec(shape):
    nd = len(shape)
    return pl.BlockSpec(shape, lambda *_: (0,) * nd, pipeline_mode=pl.Buffered(1))


def _part_specs(t, parts, width):
    if len(parts) == 1:
        return [t.rows(width)]
    return [t.prompt_rows(width), t.sample_rows(width)]


def _read_parts(t, refs):
    if len(refs) == 1:
        return refs[0][...]
    return jnp.where(pl.program_id(0) < t.n_prompt, refs[0][...], refs[1][...])


def _ada_kernel(cond_ref, w_ref, b_ref, o_ref):
    s = _silu(cond_ref[...])
    o_ref[...] = jnp.dot(s, w_ref[...], preferred_element_type=F32,
                         precision=lax.Precision.HIGHEST) + b_ref[...]


def _ada_table(cond, ada_w, ada_b):
    tn = 768
    return pl.pallas_call(
        _ada_kernel,
        out_shape=jax.ShapeDtypeStruct((DEPTH, N_COND, 3 * D_MODEL), F32),
        grid=(DEPTH, 3 * D_MODEL // tn),
        in_specs=[
            pl.BlockSpec((N_COND, D_MODEL), lambda l, n: (0, 0)),
            pl.BlockSpec((None, D_MODEL, tn), lambda l, n: (l, 0, n)),
            pl.BlockSpec((None, 1, tn), lambda l, n: (l, 0, n)),
        ],
        out_specs=pl.BlockSpec((None, N_COND, tn), lambda l, n: (l, 0, n)),
        compiler_params=_params("arbitrary", "arbitrary"),
        name="ada_table",
    )(cond, ada_w, ada_b.reshape(DEPTH, 1, 3 * D_MODEL))


def _rms(x, g):
    return x * lax.rsqrt(jnp.mean(x * x, axis=-1, keepdims=True) + EPS) * g


def _norm_mod(x, g_ref, sh_ref, sc_ref):
    return (_rms(x, g_ref[...]) * (1.0 + sc_ref[...]) + sh_ref[...]).astype(BF16)


def _proj_chunks(h, w_ref, lo, width, store):
    for c in range(width // MXU_N):
        store(c, _dot(h, w_ref[:, lo + c * MXU_N:lo + (c + 1) * MXU_N]))


def _rep4(a, h):
    half = a[:, (h // 2) * LANES:(h // 2 + 1) * LANES]
    lane = lax.broadcasted_iota(jnp.int32, half.shape, 1)
    keep = (lane < ATTN_HEAD_DIM) if h % 2 == 0 else (lane >= ATTN_HEAD_DIM)
    m = jnp.where(keep, half, 0.0)
    s = m + pltpu.roll(m, ATTN_HEAD_DIM, 1)
    return jnp.concatenate([s, s], axis=1)


def _attn_in_kernel(*refs, n_x):
    t = PROJ
    x_refs, refs = refs[:n_x], refs[n_x:]
    (g_ref, sh_ref, sc_ref, w_ref, cos_ref, sin_ref,
     q_ref, k4_ref, v4_ref, z_ref, kc_ref, vc_ref) = refs
    i = pl.program_id(0)
    h = _norm_mod(_read_parts(t, x_refs), g_ref, sh_ref, sc_ref)
    cos = cos_ref[...]
    sin = sin_ref[...]
    lane = lax.broadcasted_iota(jnp.int32, (t.tm, LANES), 1)
    first = (lane % (ATTN_HEAD_DIM // 2)) < ATTN_HEAD_DIM // 4

    def rope(a):
        rot = jnp.where(first, pltpu.roll(a, LANES - ATTN_HEAD_DIM // 4, 1),
                        pltpu.roll(a, ATTN_HEAD_DIM // 4, 1))
        return a * cos + rot * sin

    def rope_wide(a):
        return jnp.concatenate(
            [rope(a[:, s * LANES:(s + 1) * LANES]) for s in range(MXU_N // LANES)], axis=1)

    scale = ATTN_HEAD_DIM ** -0.5

    def store_q(c, a):
        q_ref[:, c * MXU_N:(c + 1) * MXU_N] = (rope_wide(a) * scale).astype(BF16)

    def store_z(c, a):
        z_ref[:, c * MXU_N:(c + 1) * MXU_N] = a.astype(BF16)

    _proj_chunks(h, w_ref, 0, ATTN_WIDTH, store_q)
    k = rope_wide(_dot(h, w_ref[:, ATTN_WIDTH:ATTN_WIDTH + ATTN_KV_WIDTH]))
    v = _dot(h, w_ref[:, ATTN_WIDTH + ATTN_KV_WIDTH:ATTN_WIDTH + 2 * ATTN_KV_WIDTH])
    for hh in range(ATTN_KV_HEADS):
        k4_ref[:, hh * MXU_N:(hh + 1) * MXU_N] = _rep4(k, hh).astype(BF16)
        v4_ref[:, hh * MXU_N:(hh + 1) * MXU_N] = _rep4(v, hh).astype(BF16)
    _proj_chunks(h, w_ref, ATTN_WIDTH + 2 * ATTN_KV_WIDTH, ATTN_WIDTH, store_z)

    @pl.when(i < t.n_prompt)
    def _():
        kc_ref[...] = k
        vc_ref[...] = v


def _attn_in(x_parts, g, mod, w, cos_t, sin_t):
    t = PROJ
    n_in = 2 * ATTN_WIDTH + 2 * ATTN_KV_WIDTH
    rope_spec = pl.BlockSpec(
        (t.tm, LANES), lambda i: (jnp.where(i < t.n_prompt, 0, 1 + t.seq_tile(i)), 0))
    wide = jax.ShapeDtypeStruct((N_TOK, ATTN_WIDTH), BF16)
    cache = jax.ShapeDtypeStruct((N_PROMPT_TOK, ATTN_KV_WIDTH), F32)
    return pl.pallas_call(
        functools.partial(_attn_in_kernel, n_x=len(x_parts)),
        out_shape=(wide, wide, wide, wide, cache, cache),
        grid=(t.n_tiles,),
        in_specs=_part_specs(t, x_parts, D_MODEL) + [
            _const_spec((1, D_MODEL)), t.mod(0), t.mod(1), _const_spec((D_MODEL, n_in)),
            rope_spec, rope_spec],
        out_specs=(t.rows(ATTN_WIDTH),) * 4 + (t.prompt_rows(ATTN_KV_WIDTH),) * 2,
        compiler_params=_params("arbitrary", vmem=VMEM_LIMIT),
        name="attn_in",
    )(*x_parts, g, mod, mod, w, cos_t, sin_t)


def _rep_head(x, h):
    chunk = lax.broadcasted_iota(jnp.int32, x.shape, 1) // ATTN_HEAD_DIM
    xm = jnp.where(chunk == h, x, 0.0)
    s = xm[:, :LANES] + xm[:, LANES:]
    s = s + pltpu.roll(s, ATTN_HEAD_DIM, 1)
    return jnp.concatenate([s, s], axis=1).astype(BF16)


def _stack_group_queries(q):
    qf = q.astype(F32)
    chunk = lax.broadcasted_iota(jnp.int32, qf.shape, 1) // ATTN_HEAD_DIM
    return jnp.concatenate(
        [jnp.where(chunk == g, qf, 0.0) for g in range(ATTN_GROUP)], axis=0).astype(BF16)


def _gather_group_outputs(o, rows):
    chunk = lax.broadcasted_iota(jnp.int32, (rows, MXU_N), 1) // ATTN_HEAD_DIM
    acc = jnp.zeros((rows, MXU_N), F32)
    for g in range(ATTN_GROUP):
        acc = acc + jnp.where(chunk == g, o[g * rows:(g + 1) * rows], 0.0)
    return acc


def _sink_column(sink_ref, h, rows):
    grp = lax.broadcasted_iota(jnp.int32, (ATTN_GROUP * rows, 1), 0) // rows
    col = jnp.zeros((ATTN_GROUP * rows, 1), F32)
    for g in range(ATTN_GROUP):
        col = jnp.where(grp == g, sink_ref[h * ATTN_GROUP + g], col)
    return col


def _chunk_rows(dtype):
    chunk = lax.broadcasted_iota(jnp.int32, (1, MXU_N), 1) // ATTN_HEAD_DIM
    return [(chunk == g).astype(F32).astype(dtype) for g in range(ATTN_GROUP)]


def _block_diag_rows(x4):
    return jnp.concatenate([x4 * m for m in _chunk_rows(x4.dtype)], axis=0)


def _ctx_attn_kernel(sink_ref, q_ref, k4_ref, v4_ref, z_ref, y_ref):
    chunk = lax.broadcasted_iota(jnp.int32, (SEQ, MXU_N), 1) // ATTN_HEAD_DIM
    for h in range(ATTN_KV_HEADS):
        cols = slice(h * MXU_N, (h + 1) * MXU_N)
        s = _dot_nt(q_ref[:, cols], _block_diag_rows(k4_ref[:, cols]))
        inv = jnp.zeros((SEQ, MXU_N), F32)
        probs = []
        for g in range(ATTN_GROUP):
            sg = s[:, g * SEQ:(g + 1) * SEQ]
            sk = sink_ref[h * ATTN_GROUP + g]
            m = jnp.maximum(jnp.max(sg, axis=1, keepdims=True), sk)
            e = jnp.exp(sg - m)
            den = jnp.sum(e, axis=1, keepdims=True) + jnp.exp(sk - m)
            probs.append(e.astype(BF16))
            inv = jnp.where(chunk == g, 1.0 / den, inv)
        o = _dot(jnp.concatenate(probs, axis=1), _block_diag_rows(v4_ref[:, cols]))
        y_ref[:, cols] = (o * inv * _silu(z_ref[:, cols].astype(F32))).astype(BF16)


def _ctx_attn(sink, q, k4, v4, z):
    spec = pl.BlockSpec((SEQ, ATTN_WIDTH), lambda b: (b, 0))
    return pl.pallas_call(
        _ctx_attn_kernel,
        out_shape=jax.ShapeDtypeStruct((N_PROMPT_TOK, ATTN_WIDTH), BF16),
        grid=(BATCH,),
        in_specs=[pl.BlockSpec(memory_space=pltpu.SMEM), spec, spec, spec, spec],
        out_specs=spec,
        compiler_params=_params("arbitrary"),
        name="ctx_attn",
    )(sink, q, k4, v4, z)


def _lat_attn_kernel(sink_ref, q_ref, k4_ref, v4_ref, kc_ref, vc_ref, z_ref, y_ref,
                     kp, vp, k4c, v4c, e_scr):
    h = pl.program_id(1)
    n = pl.program_id(2)
    n_blocks = DEC_SEQ // Q_BLOCK

    @pl.when(n == 0)
    def _():
        zeros = jnp.zeros((Q_BLOCK, MXU_N), BF16)
        for ref, src in ((kp, k4_ref), (vp, v4_ref)):
            ref[0:Q_BLOCK, :] = zeros
            ref[Q_BLOCK:Q_BLOCK + DEC_SEQ, :] = src[...]
            ref[Q_BLOCK + DEC_SEQ:, :] = zeros
        for c in range(PAST_LEN // SEQ):
            rows = pl.ds(c * SEQ, SEQ)
            k4c[rows, :] = _rep_head(kc_ref[rows, :], h)
            v4c[rows, :] = _rep_head(vc_ref[rows, :], h)

    rows4 = ATTN_GROUP * Q_BLOCK
    r = lax.broadcasted_iota(jnp.int32, (rows4, Q_BLOCK), 0) % Q_BLOCK
    c = lax.broadcasted_iota(jnp.int32, (rows4, Q_BLOCK), 1)
    in_left = c >= r
    in_right = c <= r
    sk = _sink_column(sink_ref, h, Q_BLOCK)
    for sub in range(Q_SUB):
        qb = n * Q_SUB + sub
        band = pl.ds(pl.multiple_of(qb * Q_BLOCK, Q_BLOCK), 3 * Q_BLOCK)
        qrows = slice(sub * Q_BLOCK, (sub + 1) * Q_BLOCK)
        qs = _stack_group_queries(q_ref[qrows, :])
        s_ctx = _dot_nt(qs, k4c[...])
        s_lat = _dot_nt(qs, kp[band, :])
        blocks = [
            s_ctx,
            jnp.where(jnp.logical_and(in_left, qb > 0), s_lat[:, :Q_BLOCK], NEG_INF),
            s_lat[:, Q_BLOCK:2 * Q_BLOCK],
            jnp.where(jnp.logical_and(in_right, qb < n_blocks - 1), s_lat[:, 2 * Q_BLOCK:], NEG_INF),
        ]
        m = sk
        for s in blocks:
            m = jnp.maximum(m, jnp.max(s, axis=1, keepdims=True))
        den = jnp.exp(sk - m)
        lo = 0
        for s in blocks:
            e = jnp.exp(s - m)
            den = den + jnp.sum(e, axis=1, keepdims=True)
            e_scr[sub, :, lo:lo + s.shape[1]] = e.astype(BF16)
            lo += s.shape[1]
        o = (_dot(e_scr[sub, :, :PAST_LEN], v4c[...])
             + _dot(e_scr[sub, :, PAST_LEN:], vp[band, :])) * (1.0 / den)
        acc = _gather_group_outputs(o, Q_BLOCK)
        y_ref[qrows, :] = (acc * _silu(z_ref[qrows, :].astype(F32))).astype(BF16)


def _lat_attn(sink, q, k4, v4, kc, vc, z):
    step = Q_SUB * Q_BLOCK
    per_seq = DEC_SEQ // step
    padded = DEC_SEQ + 2 * Q_BLOCK
    seq_spec = pl.BlockSpec((DEC_SEQ, MXU_N), lambda b, h, n: (N_PROMPT_TOK // DEC_SEQ + b, h))
    ctx_spec = pl.BlockSpec((None, PAST_LEN, ATTN_KV_WIDTH), lambda b, h, n: (b, 0, 0))
    qz_spec = pl.BlockSpec((step, MXU_N), lambda b, h, n: (N_PROMPT_TOK // step + b * per_seq + n, h))
    return pl.pallas_call(
        _lat_attn_kernel,
        out_shape=jax.ShapeDtypeStruct((N_SAMPLE_TOK, ATTN_WIDTH), BF16),
        grid=(DEC_BATCH, ATTN_KV_HEADS, per_seq),
        in_specs=[pl.BlockSpec(memory_space=pltpu.SMEM), qz_spec, seq_spec, seq_spec,
                  ctx_spec, ctx_spec, qz_spec],
        out_specs=pl.BlockSpec((step, MXU_N), lambda b, h, n: (b * per_seq + n, h)),
        scratch_shapes=[pltpu.VMEM((padded, MXU_N), BF16), pltpu.VMEM((padded, MXU_N), BF16),
                        pltpu.VMEM((PAST_LEN, MXU_N), BF16), pltpu.VMEM((PAST_LEN, MXU_N), BF16),
                        pltpu.VMEM((Q_SUB, ATTN_GROUP * Q_BLOCK, PAST_LEN + 3 * Q_BLOCK), BF16)],
        compiler_params=_params("arbitrary", "arbitrary", "arbitrary"),
        name="lat_attn",
    )(sink, q, k4, v4, kc, vc, z)


def _out_kernel(*refs, n_y, n_x):
    t = PROJ
    y_refs, refs = refs[:n_y], refs[n_y:]
    w_ref, refs = refs[0], refs[1:]
    x_refs, (gate_ref, o_ref) = refs[:n_x], refs[n_x:]
    o_ref[...] = _read_parts(t, x_refs) + gate_ref[...] * _dot(_read_parts(t, y_refs), w_ref[...])


def _out_proj(y_parts, w, x_parts, mod):
    t = PROJ
    k = w.shape[0]
    return pl.pallas_call(
        functools.partial(_out_kernel, n_y=len(y_parts), n_x=len(x_parts)),
        out_shape=jax.ShapeDtypeStruct((N_TOK, D_MODEL), F32),
        grid=(t.n_tiles,),
        in_specs=(_part_specs(t, y_parts, k) + [_const_spec((k, D_MODEL))]
                  + _part_specs(t, x_parts, D_MODEL) + [t.mod(2)]),
        out_specs=t.rows(D_MODEL),
        compiler_params=_params("arbitrary", vmem=VMEM_LIMIT),
        name="out_proj",
    )(*y_parts, w, *x_parts, mod)


def _out_final_kernel(yp_ref, ys_ref, w_ref, x_ref, gate_ref, fg_ref, op_ref, os_ref):
    t = PROJ
    i = pl.program_id(0)
    y = _read_parts(t, (yp_ref, ys_ref))
    r = _rms(x_ref[...] + gate_ref[...] * _dot(y, w_ref[...]), fg_ref[...])

    @pl.when(i < t.n_prompt)
    def _():
        op_ref[...] = r

    @pl.when(i >= t.n_prompt)
    def _():
        os_ref[...] = r


def _out_proj_final(y_parts, w, x, mod, final_g):
    t = PROJ
    k = w.shape[0]
    return pl.pallas_call(
        _out_final_kernel,
        out_shape=(jax.ShapeDtypeStruct((N_PROMPT_TOK, D_MODEL), F32),
                   jax.ShapeDtypeStruct((N_SAMPLE_TOK, D_MODEL), F32)),
        grid=(t.n_tiles,),
        in_specs=(_part_specs(t, y_parts, k)
                  + [_const_spec((k, D_MODEL)), t.rows(D_MODEL), t.mod(2), _const_spec((1, D_MODEL))]),
        out_specs=(t.prompt_rows(D_MODEL), t.sample_rows(D_MODEL)),
        compiler_params=_params("arbitrary", vmem=VMEM_LIMIT),
        name="out_proj_final",
    )(*y_parts, w, x, mod, final_g)


def _pool_in_kernel(x_ref, g_ref, sh_ref, sc_ref, w_ref, u_ref, z_ref):
    h = _norm_mod(x_ref[...], g_ref, sh_ref, sc_ref)

    def store_u(c, a):
        u_ref[:, c * MXU_N:(c + 1) * MXU_N] = a

    def store_z(c, a):
        z_ref[:, c * MXU_N:(c + 1) * MXU_N] = a.astype(BF16)

    _proj_chunks(h, w_ref, 0, D_MODEL, store_u)
    _proj_chunks(h, w_ref, D_MODEL, D_MODEL, store_z)


def _pool_in(x, g, mod, w):
    t = PROJ
    return pl.pallas_call(
        _pool_in_kernel,
        out_shape=(jax.ShapeDtypeStruct((N_TOK, D_MODEL), F32),
                   jax.ShapeDtypeStruct((N_TOK, D_MODEL), BF16)),
        grid=(t.n_tiles,),
        in_specs=[t.rows(D_MODEL), _const_spec((1, D_MODEL)), t.mod(0), t.mod(1),
                  _const_spec((D_MODEL, 2 * D_MODEL))],
        out_specs=(t.rows(D_MODEL), t.rows(D_MODEL)),
        compiler_params=_params("arbitrary", vmem=VMEM_LIMIT),
        name="pool_in",
    )(x, g, mod, mod, w)


def _pool_out_kernel(u_ref, up_ref, un_ref, z_ref, wg_ref, ps_ref, wo_ref, x_ref, gate_ref,
                     o_ref, pad_ref, y_ref):
    t = POOL
    i = pl.program_id(0)
    is_dec = i >= t.n_prompt
    st = t.seq_tile(i)
    has_left = jnp.logical_and(is_dec, st != 0)
    has_right = jnp.logical_and(is_dec, st != t.per_dec_seq - 1)
    seq_len = jnp.where(is_dec, DEC_SEQ, SEQ)
    pad_ref[0:POOL_HALO, :] = jnp.where(has_left, up_ref[...], 0.0)
    pad_ref[POOL_HALO:POOL_HALO + t.tm, :] = u_ref[...]
    pad_ref[POOL_HALO + t.tm:, :] = jnp.where(has_right, un_ref[...], 0.0)
    pos = st * t.tm + lax.broadcasted_iota(jnp.int32, (t.tm, 1), 0)
    for g, w in enumerate(POOL_WINDOWS):
        left = w // 2
        right = w - 1 - left
        cols = slice(g * POOL_GROUP_DIM, (g + 1) * POOL_GROUP_DIM)
        s = pad_ref[POOL_HALO - left:POOL_HALO - left + t.tm, cols]
        for off in range(-left + 1, right + 1):
            s = s + pad_ref[POOL_HALO + off:POOL_HALO + off + t.tm, cols]
        cnt = (jnp.minimum(pos + right + 1, seq_len) - jnp.maximum(pos - left, 0)).astype(F32)
        d = (s / cnt - u_ref[:, cols]).astype(BF16)
        yg = _dot(d, wg_ref[g]) * ps_ref[:, cols] * _silu(z_ref[:, cols].astype(F32))
        y_ref[:, cols] = yg.astype(BF16)
    o_ref[...] = x_ref[...] + gate_ref[...] * _dot(y_ref[...], wo_ref[...])


def _pool_out(u, z, wg, ps, wo, x, mod):
    t = POOL
    per = t.tm // POOL_HALO
    n_halo = N_TOK // POOL_HALO
    return pl.pallas_call(
        _pool_out_kernel,
        out_shape=jax.ShapeDtypeStruct((N_TOK, D_MODEL), F32),
        grid=(t.n_tiles,),
        in_specs=[
            t.rows(D_MODEL),
            pl.BlockSpec((POOL_HALO, D_MODEL), lambda i: (jnp.maximum(i * per - 1, 0), 0)),
            pl.BlockSpec((POOL_HALO, D_MODEL), lambda i: (jnp.minimum((i + 1) * per, n_halo - 1), 0)),
            t.rows(D_MODEL),
            _const_spec((len(POOL_WINDOWS), POOL_GROUP_DIM, POOL_GROUP_DIM)),
            _const_spec((1, D_MODEL)),
            _const_spec((D_MODEL, D_MODEL)),
            t.rows(D_MODEL),
            t.mod(2),
        ],
        out_specs=t.rows(D_MODEL),
        scratch_shapes=[pltpu.VMEM((t.tm + 2 * POOL_HALO, D_MODEL), F32),
                        pltpu.VMEM((t.tm, D_MODEL), BF16)],
        compiler_params=_params("arbitrary"),
        name="pool_out",
    )(u, u, u, z, wg, ps, wo, x, mod)


def _ret_in_kernel(x_ref, g_ref, sh_ref, sc_ref, wq_ref, wkt_ref, wvz_ref, q_ref, kt_ref, v_ref, z_ref):
    t = PROJ
    h = _norm_mod(x_ref[...], g_ref, sh_ref, sc_ref)

    def store(ref):
        def f(c, a):
            ref[:, c * MXU_N:(c + 1) * MXU_N] = a.astype(BF16)
        return f

    _proj_chunks(h, wq_ref, 0, RET_QK_WIDTH, store(q_ref))
    kt = (_dot_nt(wkt_ref[...], h) * RET_KEY_DIM ** -0.5).astype(BF16)
    for c in range(t.tm // RET_CHUNK):
        kt_ref[c] = kt[:, c * RET_CHUNK:(c + 1) * RET_CHUNK]
    _proj_chunks(h, wvz_ref, 0, RET_V_WIDTH, store(v_ref))
    _proj_chunks(h, wvz_ref, RET_V_WIDTH, RET_V_WIDTH, store(z_ref))


def _ret_in(x, g, mod, wq, wkt, wvz):
    t = PROJ
    per = t.tm // RET_CHUNK
    return pl.pallas_call(
        _ret_in_kernel,
        out_shape=(jax.ShapeDtypeStruct((N_TOK, RET_QK_WIDTH), BF16),
                   jax.ShapeDtypeStruct((N_TOK // RET_CHUNK, RET_QK_WIDTH, RET_CHUNK), BF16),
                   jax.ShapeDtypeStruct((N_TOK, RET_V_WIDTH), BF16),
                   jax.ShapeDtypeStruct((N_TOK, RET_V_WIDTH), BF16)),
        grid=(t.n_tiles,),
        in_specs=[t.rows(D_MODEL), _const_spec((1, D_MODEL)), t.mod(0), t.mod(1),
                  _const_spec((D_MODEL, RET_QK_WIDTH)), _const_spec((RET_QK_WIDTH, D_MODEL)),
                  _const_spec((D_MODEL, 2 * RET_V_WIDTH))],
        out_specs=(t.rows(RET_QK_WIDTH),
                   pl.BlockSpec((per, RET_QK_WIDTH, RET_CHUNK), lambda i: (i, 0, 0)),
                   t.rows(RET_V_WIDTH), t.rows(RET_V_WIDTH)),
        compiler_params=_params("arbitrary", vmem=VMEM_LIMIT),
        name="ret_in",
    )(x, g, mod, mod, wq, wkt, wvz)


def _pos(shape, axis):
    return lax.broadcasted_iota(jnp.int32, shape, axis).astype(F32)


def _ret_tables_kernel(lgf_ref, lgb_ref, decay_ref, row_ref, col_ref, cdec_ref):
    h = pl.program_id(0)
    lg_f = lgf_ref[h]
    lg_b = lgb_ref[h]
    c = RET_CHUNK
    diff = _pos((c, c), 0) - _pos((c, c), 1)
    fwd = jnp.where(diff >= 0, jnp.exp(jnp.maximum(diff, 0.0) * lg_f), 0.0)
    bwd = jnp.where(diff <= 0, jnp.exp(jnp.maximum(-diff, 0.0) * lg_b), 0.0)
    decay_ref[...] = fwd + bwd
    j = _pos((RET_TAB_ROWS, c), 1)
    row_ref[0] = jnp.exp((c - 1.0 - j) * lg_f)
    row_ref[1] = jnp.exp(j * lg_b)
    i = _pos((c, LANES), 0)
    col_ref[0] = jnp.exp((i + 1.0) * lg_f)
    col_ref[1] = jnp.exp((c - i) * lg_b)
    full = jnp.full((RET_TAB_ROWS, RET_VAL_DIM), float(c), F32)
    cdec_ref[0] = jnp.exp(full * lg_f)
    cdec_ref[1] = jnp.exp(full * lg_b)


def _ret_tables(lg_f, lg_b):
    smem = pl.BlockSpec(memory_space=pltpu.SMEM)
    c = RET_CHUNK
    shapes = ((c, c), (2, RET_TAB_ROWS, c), (2, c, LANES), (2, RET_TAB_ROWS, RET_VAL_DIM))
    return pl.pallas_call(
        _ret_tables_kernel,
        out_shape=tuple(jax.ShapeDtypeStruct((RET_HEADS,) + s, F32) for s in shapes),
        grid=(RET_HEADS,),
        in_specs=[smem, smem],
        out_specs=tuple(pl.BlockSpec((None,) + s, lambda h, n=len(s): (h,) + (0,) * n) for s in shapes),
        compiler_params=_params("arbitrary"),
        name="ret_tables",
    )(lg_f, lg_b)


def _group_norm_gate(o, gn, z):
    mu = jnp.mean(o, axis=-1, keepdims=True)
    var = jnp.mean(jnp.square(o - mu), axis=-1, keepdims=True)
    on = (o - mu) * lax.rsqrt(var + EPS)
    return (on * gn * _silu(z.astype(F32))).astype(BF16)


def _ret_ctx_kernel(q_ref, kt_ref, v_ref, z_ref, gn_ref, decay_ref, row_ref, y_ref, sf_ref, sb_ref):
    for h in range(RET_HEADS):
        kc = slice(h * RET_KEY_DIM, (h + 1) * RET_KEY_DIM)
        vc = slice(h * RET_VAL_DIM, (h + 1) * RET_VAL_DIM)
        kt = kt_ref[kc, :]
        v = v_ref[:, vc]
        att = (_dot(q_ref[:, kc], kt) * decay_ref[h]).astype(BF16)
        y_ref[:, vc] = _group_norm_gate(_dot(att, v), gn_ref[:, vc], z_ref[:, vc])
        ktf = kt.astype(F32)
        sf_ref[h] = _dot((ktf * row_ref[h, 0, 0:1, :]).astype(BF16), v)
        sb_ref[h] = _dot((ktf * row_ref[h, 1, 0:1, :]).astype(BF16), v)


def _ret_ctx(q, kt, v, z, gn, decay, row):
    c = RET_CHUNK
    wide = lambda w: pl.BlockSpec((SEQ, w), lambda b: (b, 0))
    st_spec = pl.BlockSpec((None, RET_HEADS, RET_KEY_DIM, RET_VAL_DIM), lambda b: (b, 0, 0, 0))
    st_shape = jax.ShapeDtypeStruct((BATCH, RET_HEADS, RET_KEY_DIM, RET_VAL_DIM), F32)
    return pl.pallas_call(
        _ret_ctx_kernel,
        out_shape=(jax.ShapeDtypeStruct((N_PROMPT_TOK, RET_V_WIDTH), BF16), st_shape, st_shape),
        grid=(BATCH,),
        in_specs=[wide(RET_QK_WIDTH), pl.BlockSpec((None, RET_QK_WIDTH, c), lambda b: (b, 0, 0)),
                  wide(RET_V_WIDTH), wide(RET_V_WIDTH), _const_spec((1, RET_V_WIDTH)),
                  _const_spec((RET_HEADS, c, c)), _const_spec((RET_HEADS, 2, RET_TAB_ROWS, c))],
        out_specs=(wide(RET_V_WIDTH), st_spec, st_spec),
        compiler_params=_params("arbitrary", vmem=VMEM_LIMIT),
        name="ret_ctx",
    )(q, kt, v, z, gn, decay, row)


def _ret_lat_kernel(q_ref, kt_ref, v_ref, z_ref, gn_ref, decay_ref, row_ref, col_ref, cdec_ref,
                    s0f_ref, s0b_ref, y_ref, sf_all, sb_all, sf_acc, sb_acc):
    c = RET_CHUNK
    n_chunks = DEC_SEQ // c
    rows_of = lambda ci: pl.ds(pl.multiple_of(ci * c, c), c)

    sf_acc[...] = s0f_ref[...]
    sb_acc[...] = s0b_ref[...]

    def scan_step(i, carry):
        cf = i
        cb = n_chunks - 1 - i
        sf_all[cf] = sf_acc[...].astype(BF16)
        sb_all[cb] = sb_acc[...].astype(BF16)
        uf = _dot((kt_ref[cf].astype(F32) * row_ref[0, 0:1, :]).astype(BF16), v_ref[rows_of(cf), :])
        ub = _dot((kt_ref[cb].astype(F32) * row_ref[1, 0:1, :]).astype(BF16), v_ref[rows_of(cb), :])
        sf_acc[...] = sf_acc[...] * cdec_ref[0, 0:1, :] + uf
        sb_acc[...] = sb_acc[...] * cdec_ref[1, 0:1, :] + ub
        return carry

    lax.fori_loop(0, n_chunks, scan_step, 0)

    def out_step(ci, carry):
        rows = rows_of(ci)
        q = q_ref[rows, :]
        qf = q.astype(F32)
        qdec_f = jnp.concatenate([col_ref[0]] * (RET_KEY_DIM // LANES), axis=1)
        qdec_b = jnp.concatenate([col_ref[1]] * (RET_KEY_DIM // LANES), axis=1)
        att = (_dot(q, kt_ref[ci]) * decay_ref[...]).astype(BF16)
        o = (_dot(att, v_ref[rows, :])
             + _dot((qf * qdec_f).astype(BF16), sf_all[ci])
             + _dot((qf * qdec_b).astype(BF16), sb_all[ci]))
        y_ref[rows, :] = _group_norm_gate(o, gn_ref[...], z_ref[rows, :])
        return carry

    lax.fori_loop(0, n_chunks, out_step, 0, unroll=2)


def _ret_lat(q, kt, v, z, gn, decay, row, col, cdec, s0f, s0b):
    c = RET_CHUNK
    n_chunks = DEC_SEQ // c
    row0 = N_PROMPT_TOK // DEC_SEQ
    qk_spec = pl.BlockSpec((DEC_SEQ, RET_KEY_DIM), lambda b, h: (row0 + b, h))
    v_spec = pl.BlockSpec((DEC_SEQ, RET_VAL_DIM), lambda b, h: (row0 + b, h))
    st_spec = pl.BlockSpec((None, None, RET_KEY_DIM, RET_VAL_DIM), lambda b, h: (b, h, 0, 0))
    tab = lambda *s: pl.BlockSpec((None,) + s, lambda b, h: (h,) + (0,) * len(s))
    states = pltpu.VMEM((n_chunks, RET_KEY_DIM, RET_VAL_DIM), BF16)
    acc = pltpu.VMEM((RET_KEY_DIM, RET_VAL_DIM), F32)
    return pl.pallas_call(
        _ret_lat_kernel,
        out_shape=jax.ShapeDtypeStruct((N_SAMPLE_TOK, RET_V_WIDTH), BF16),
        grid=(DEC_BATCH, RET_HEADS),
        in_specs=[qk_spec,
                  pl.BlockSpec((n_chunks, RET_KEY_DIM, c), lambda b, h: (row0 + b, h, 0)),
                  v_spec, v_spec, pl.BlockSpec((1, RET_VAL_DIM), lambda b, h: (0, h)),
                  tab(c, c), tab(2, RET_TAB_ROWS, c), tab(2, c, LANES), tab(2, RET_TAB_ROWS, RET_VAL_DIM),
                  st_spec, st_spec],
        out_specs=pl.BlockSpec((DEC_SEQ, RET_VAL_DIM), lambda b, h: (b, h)),
        scratch_shapes=[states, states, acc, acc],
        compiler_params=_params("arbitrary", "arbitrary", vmem=VMEM_LIMIT),
        name="ret_lat",
    )(q, kt, v, z, gn, decay, row, col, cdec, s0f, s0b)


def _rope_tables(tm):
    n_rows = DEC_SEQ // GRID_W
    rows = jnp.repeat(jnp.arange(n_rows), GRID_W).astype(F32)
    cols = jnp.tile(jnp.arange(GRID_W), n_rows).astype(F32)
    half = ATTN_HEAD_DIM // 4
    inv = ROPE_BASE ** (-jnp.arange(half, dtype=F32) / half)
    ang_r = rows[:, None] * inv[None, :]
    ang_c = cols[:, None] * inv[None, :]
    cos = jnp.concatenate([jnp.cos(ang_r), jnp.cos(ang_r), jnp.cos(ang_c), jnp.cos(ang_c)], axis=-1)
    sin = jnp.concatenate([-jnp.sin(ang_r), jnp.sin(ang_r), -jnp.sin(ang_c), jnp.sin(ang_c)], axis=-1)
    cos = jnp.concatenate([jnp.ones((tm, ATTN_HEAD_DIM), F32), cos], axis=0)
    sin = jnp.concatenate([jnp.zeros((tm, ATTN_HEAD_DIM), F32), sin], axis=0)
    return jnp.tile(cos, (1, 2)), jnp.tile(sin, (1, 2))


def kernel(x_prompt, x_sample, cache_k, cache_v, state_fwd, state_bwd, c, c_ctx, norm_g, ada_w, ada_b, attn_w_in, attn_w_out, attn_sink, pool_w_in, pool_w_grp, pool_scale, pool_w_out, ret_w_in, ret_decay_fwd, ret_decay_bwd, ret_gn_g, ret_w_out, final_g):
    x_parts = (x_prompt.reshape(N_PROMPT_TOK, D_MODEL), x_sample.reshape(N_SAMPLE_TOK, D_MODEL))
    cond = jnp.concatenate([c_ctx[None, :], c,
                            jnp.zeros((N_COND - 1 - DEC_BATCH, D_MODEL), F32)], axis=0)
    mods = _ada_table(cond, ada_w, ada_b).reshape(DEPTH, N_COND, 1, 3 * D_MODEL)
    cos_t, sin_t = _rope_tables(PROJ.tm)

    new_k, new_v = [], []
    new_sf = new_sb = None
    for i in range(DEPTH):
        kind, j = i % N_MIXERS, i // N_MIXERS
        g = norm_g[i].reshape(1, D_MODEL)
        mod = mods[i]
        last = i == DEPTH - 1
        if kind == 0:
            q, k4, v4, z, kc, vc = _attn_in(x_parts, g, mod, attn_w_in[j].astype(BF16), cos_t, sin_t)
            y_parts = (_ctx_attn(attn_sink[j], q, k4, v4, z),
                       _lat_attn(attn_sink[j], q, k4, v4,
                                 cache_k[:, j].reshape(DEC_BATCH, PAST_LEN, ATTN_KV_WIDTH),
                                 cache_v[:, j].reshape(DEC_BATCH, PAST_LEN, ATTN_KV_WIDTH), z))
            w_out = attn_w_out[j].astype(BF16)
            new_k.append(kc.reshape(BATCH, SEQ, ATTN_KV_HEADS, ATTN_HEAD_DIM))
            new_v.append(vc.reshape(BATCH, SEQ, ATTN_KV_HEADS, ATTN_HEAD_DIM))
        elif kind == 1:
            (x,) = x_parts
            u, z = _pool_in(x, g, mod, pool_w_in[j].astype(BF16))
            assert not last
            x_parts = (_pool_out(u, z, pool_w_grp[j].astype(BF16), pool_scale[j].reshape(1, D_MODEL),
                                 pool_w_out[j].astype(BF16), x, mod),)
            continue
        else:
            (x,) = x_parts
            lg_f = jax.nn.log_sigmoid(ret_decay_fwd[j].astype(F32))
            lg_b = jax.nn.log_sigmoid(ret_decay_bwd[j].astype(F32))
            gn = ret_gn_g[j].reshape(1, RET_V_WIDTH)
            w = ret_w_in[j].astype(BF16)
            q, kt, v, z = _ret_in(x, g, mod, w[:, :RET_QK_WIDTH],
                                  w[:, RET_QK_WIDTH:2 * RET_QK_WIDTH].T, w[:, 2 * RET_QK_WIDTH:])
            decay, row, col, cdec = _ret_tables(lg_f, lg_b)
            y_ctx, new_sf, new_sb = _ret_ctx(q, kt, v, z, gn, decay, row)
            y_parts = (y_ctx, _ret_lat(q, kt, v, z, gn, decay, row, col, cdec,
                                       state_fwd[:, j], state_bwd[:, j]))
            w_out = ret_w_out[j].astype(BF16)
        if last:
            (x,) = x_parts
            y_prompt, y_sample = _out_proj_final(y_parts, w_out, x, mod, final_g.reshape(1, D_MODEL))
        else:
            x_parts = (_out_proj(y_parts, w_out, x_parts, mod),)
    return (y_prompt.reshape(BATCH, SEQ, D_MODEL), y_sample.reshape(DEC_BATCH, DEC_SEQ, D_MODEL),
            jnp.stack(new_k, axis=1), jnp.stack(new_v, axis=1), new_sf[:, None], new_sb[:, None])
```

```python
import functools
from typing import NamedTuple

import jax
import jax.numpy as jnp
from jax import lax
from jax.experimental import pallas as pl
from jax.experimental.pallas import tpu as pltpu

F32 = jnp.float32
BF16 = jnp.bfloat16

D_MODEL = 1024
BATCH = 16
SEQ = 256
DEPTH = 4
DEC_BATCH = 2
DEC_SEQ = 2048
PAST_LEN = 512
GRID_W = 64
N_MIXERS = 3
ATTN_HEADS = 16
ATTN_KV_HEADS = 4
ATTN_HEAD_DIM = 64
ATTN_GROUP = 4
ATTN_WIDTH = 1024
ATTN_KV_WIDTH = 256
WINDOW = 128
ROPE_BASE = 10000.0
POOL_WINDOWS = (2, 4, 8, 16)
POOL_GROUP_DIM = 256
RET_HEADS = 4
RET_KEY_DIM = 256
RET_VAL_DIM = 512
RET_QK_WIDTH = 1024
RET_V_WIDTH = 2048
EPS = 1e-6
NEG_INF = -1e30

N_PROMPT_TOK = BATCH * SEQ
N_SAMPLE_TOK = DEC_BATCH * DEC_SEQ
N_TOK = N_PROMPT_TOK + N_SAMPLE_TOK
N_COND = 8
LANES = 128
MXU_N = 256
Q_BLOCK = 128
Q_SUB = 4
RET_CHUNK = 256
RET_TAB_ROWS = 8
POOL_HALO = 8
VMEM_LIMIT = 48 * 1024 * 1024


class Tiling(NamedTuple):
    tm: int

    @property
    def n_tiles(self):
        return N_TOK // self.tm

    @property
    def n_prompt(self):
        return N_PROMPT_TOK // self.tm

    @property
    def per_dec_seq(self):
        return DEC_SEQ // self.tm

    def cond(self, i):
        return jnp.where(i < self.n_prompt, 0, 1 + (i - self.n_prompt) // self.per_dec_seq)

    def seq_tile(self, i):
        return jnp.where(i < self.n_prompt, 0, (i - self.n_prompt) % self.per_dec_seq)

    def rows(self, width):
        return pl.BlockSpec((self.tm, width), lambda i: (i, 0))

    def prompt_rows(self, width):
        return pl.BlockSpec((self.tm, width), lambda i: (jnp.minimum(i, self.n_prompt - 1), 0))

    def sample_rows(self, width):
        return pl.BlockSpec((self.tm, width), lambda i: (jnp.maximum(i - self.n_prompt, 0), 0))

    def mod(self, part):
        return pl.BlockSpec((None, 1, D_MODEL), lambda i: (self.cond(i), 0, part))


PROJ = Tiling(512)
POOL = Tiling(SEQ)


def _silu(z):
    return z * (1.0 / (1.0 + jnp.exp(-z)))


def _dot(a, b):
    return jnp.dot(a, b, preferred_element_type=F32)


def _dot_nt(a, b):
    return lax.dot_general(a, b, (((1,), (1,)), ((), ())), preferred_element_type=F32)


def _params(*sem, vmem=None):
    return pltpu.CompilerParams(dimension_semantics=sem, vmem_limit_bytes=vmem)


def _const_spec(shape):
    nd = len(shape)
    return pl.BlockSpec(shape, lambda *_: (0,) * nd, pipeline_mode=pl.Buffered(1))


def _part_specs(t, parts, width):
    if len(parts) == 1:
        return [t.rows(width)]
    return [t.prompt_rows(width), t.sample_rows(width)]


def _read_parts(t, refs):
    if len(refs) == 1:
        return refs[0][...]
    return jnp.where(pl.program_id(0) < t.n_prompt, refs[0][...], refs[1][...])


def _ada_kernel(cond_ref, w_ref, b_ref, o_ref):
    s = _silu(cond_ref[...])
    o_ref[...] = jnp.dot(s, w_ref[...], preferred_element_type=F32,
                         precision=lax.Precision.HIGHEST) + b_ref[...]


def _ada_table(cond, ada_w, ada_b):
    tn = 768
    return pl.pallas_call(
        _ada_kernel,
        out_shape=jax.ShapeDtypeStruct((DEPTH, N_COND, 3 * D_MODEL), F32),
        grid=(DEPTH, 3 * D_MODEL // tn),
        in_specs=[
            pl.BlockSpec((N_COND, D_MODEL), lambda l, n: (0, 0)),
            pl.BlockSpec((None, D_MODEL, tn), lambda l, n: (l, 0, n)),
            pl.BlockSpec((None, 1, tn), lambda l, n: (l, 0, n)),
        ],
        out_specs=pl.BlockSpec((None, N_COND, tn), lambda l, n: (l, 0, n)),
        compiler_params=_params("arbitrary", "arbitrary"),
        name="ada_table",
    )(cond, ada_w, ada_b.reshape(DEPTH, 1, 3 * D_MODEL))


def _rms(x, g):
    return x * lax.rsqrt(jnp.mean(x * x, axis=-1, keepdims=True) + EPS) * g


def _norm_mod(x, g_ref, sh_ref, sc_ref):
    return (_rms(x, g_ref[...]) * (1.0 + sc_ref[...]) + sh_ref[...]).astype(BF16)


def _proj_chunks(h, w_ref, lo, width, store):
    for c in range(width // MXU_N):
        store(c, _dot(h, w_ref[:, lo + c * MXU_N:lo + (c + 1) * MXU_N]))


def _layer_spec(w, j):
    nd = w.ndim - 1
    return pl.BlockSpec((None,) + w.shape[1:], lambda *_: (j,) + (0,) * nd, pipeline_mode=pl.Buffered(1))


def _cast_weight_once(w_ref, wb_ref):
    @pl.when(pl.program_id(0) == 0)
    def _():
        rows = wb_ref.shape[-2]
        for r in range(0, rows, MXU_N):
            wb_ref[..., r:r + MXU_N, :] = w_ref[..., r:r + MXU_N, :].astype(BF16)


def _rep4(a, h):
    half = a[:, (h // 2) * LANES:(h // 2 + 1) * LANES]
    lane = lax.broadcasted_iota(jnp.int32, half.shape, 1)
    keep = (lane < ATTN_HEAD_DIM) if h % 2 == 0 else (lane >= ATTN_HEAD_DIM)
    m = jnp.where(keep, half, 0.0)
    s = m + pltpu.roll(m, ATTN_HEAD_DIM, 1)
    return jnp.concatenate([s, s], axis=1)


def _attn_in_kernel(*refs, n_x, n_alias):
    t = PROJ
    x_refs, refs = refs[:n_x], refs[n_x:]
    (g_ref, sh_ref, sc_ref, wf_ref, cos_ref, sin_ref), refs = refs[:6], refs[6 + n_alias:]
    q_ref, k4_ref, v4_ref, z_ref, kc_ref, vc_ref, w_ref = refs
    i = pl.program_id(0)
    _cast_weight_once(wf_ref, w_ref)
    h = _norm_mod(_read_parts(t, x_refs), g_ref, sh_ref, sc_ref)
    cos = cos_ref[...]
    sin = sin_ref[...]
    lane = lax.broadcasted_iota(jnp.int32, (t.tm, LANES), 1)
    first = (lane % (ATTN_HEAD_DIM // 2)) < ATTN_HEAD_DIM // 4

    def rope(a):
        rot = jnp.where(first, pltpu.roll(a, LANES - ATTN_HEAD_DIM // 4, 1),
                        pltpu.roll(a, ATTN_HEAD_DIM // 4, 1))
        return a * cos + rot * sin

    def rope_wide(a):
        return jnp.concatenate(
            [rope(a[:, s * LANES:(s + 1) * LANES]) for s in range(MXU_N // LANES)], axis=1)

    scale = ATTN_HEAD_DIM ** -0.5

    def store_q(c, a):
        q_ref[:, c * MXU_N:(c + 1) * MXU_N] = (rope_wide(a) * scale).astype(BF16)

    def store_z(c, a):
        z_ref[:, c * MXU_N:(c + 1) * MXU_N] = a.astype(BF16)

    _proj_chunks(h, w_ref, 0, ATTN_WIDTH, store_q)
    k = rope_wide(_dot(h, w_ref[:, ATTN_WIDTH:ATTN_WIDTH + ATTN_KV_WIDTH]))
    v = _dot(h, w_ref[:, ATTN_WIDTH + ATTN_KV_WIDTH:ATTN_WIDTH + 2 * ATTN_KV_WIDTH])
    for hh in range(ATTN_KV_HEADS):
        k4_ref[:, hh * MXU_N:(hh + 1) * MXU_N] = _rep4(k, hh).astype(BF16)
        v4_ref[:, hh * MXU_N:(hh + 1) * MXU_N] = _rep4(v, hh).astype(BF16)
    _proj_chunks(h, w_ref, ATTN_WIDTH + 2 * ATTN_KV_WIDTH, ATTN_WIDTH, store_z)

    @pl.when(i < t.n_prompt)
    def _():
        for s in range(t.tm // SEQ):
            kc_ref[s] = k[s * SEQ:(s + 1) * SEQ, :].T
            vc_ref[s] = v[s * SEQ:(s + 1) * SEQ, :].T


def _attn_in(x_parts, g, mod, w_all, j, cos_t, sin_t, caches):
    t = PROJ
    n_in = 2 * ATTN_WIDTH + 2 * ATTN_KV_WIDTH
    per = t.tm // SEQ
    rope_spec = pl.BlockSpec(
        (t.tm, LANES), lambda i: (jnp.where(i < t.n_prompt, 0, 1 + t.seq_tile(i)), 0))
    wide = jax.ShapeDtypeStruct((N_TOK, ATTN_WIDTH), BF16)
    n_attn = w_all.shape[0]
    cache = jax.ShapeDtypeStruct((BATCH, n_attn, ATTN_KV_WIDTH, SEQ), F32)
    cache_spec = pl.BlockSpec((per, None, ATTN_KV_WIDTH, SEQ),
                              lambda i: (jnp.minimum(i, t.n_prompt - 1), j, 0, 0))
    n_x = len(x_parts)
    return pl.pallas_call(
        functools.partial(_attn_in_kernel, n_x=n_x, n_alias=len(caches)),
        out_shape=(wide, wide, wide, wide, cache, cache),
        grid=(t.n_tiles,),
        in_specs=_part_specs(t, x_parts, D_MODEL) + [
            _const_spec((1, D_MODEL)), t.mod(0), t.mod(1), _layer_spec(w_all, j),
            rope_spec, rope_spec] + [pl.BlockSpec(memory_space=pl.ANY)] * len(caches),
        out_specs=(t.rows(ATTN_WIDTH),) * 4 + (cache_spec,) * 2,
        scratch_shapes=[pltpu.VMEM((D_MODEL, n_in), BF16)],
        input_output_aliases={n_x + 6 + c: 4 + c for c in range(len(caches))},
        compiler_params=_params("arbitrary", vmem=VMEM_LIMIT),
        name="attn_in",
    )(*x_parts, g, mod, mod, w_all, cos_t, sin_t, *caches)


def _stack_group_queries(q):
    qf = q.astype(F32)
    chunk = lax.broadcasted_iota(jnp.int32, qf.shape, 1) // ATTN_HEAD_DIM
    return jnp.concatenate(
        [jnp.where(chunk == g, qf, 0.0) for g in range(ATTN_GROUP)], axis=0).astype(BF16)


def _gather_group_outputs(o, rows):
    chunk = lax.broadcasted_iota(jnp.int32, (rows, MXU_N), 1) // ATTN_HEAD_DIM
    acc = jnp.zeros((rows, MXU_N), F32)
    for g in range(ATTN_GROUP):
        acc = acc + jnp.where(chunk == g, o[g * rows:(g + 1) * rows], 0.0)
    return acc


def _sink_column(sink_ref, h, rows):
    grp = lax.broadcasted_iota(jnp.int32, (ATTN_GROUP * rows, 1), 0) // rows
    col = jnp.zeros((ATTN_GROUP * rows, 1), F32)
    for g in range(ATTN_GROUP):
        col = jnp.where(grp == g, sink_ref[h * ATTN_GROUP + g], col)
    return col


def _chunk_rows(dtype):
    chunk = lax.broadcasted_iota(jnp.int32, (1, MXU_N), 1) // ATTN_HEAD_DIM
    return [(chunk == g).astype(F32).astype(dtype) for g in range(ATTN_GROUP)]


def _block_diag_rows(x4):
    return jnp.concatenate([x4 * m for m in _chunk_rows(x4.dtype)], axis=0)


def _ctx_attn_kernel(sink_ref, q_ref, k4_ref, v4_ref, z_ref, y_ref):
    chunk = lax.broadcasted_iota(jnp.int32, (SEQ, MXU_N), 1) // ATTN_HEAD_DIM
    for h in range(ATTN_KV_HEADS):
        cols = slice(h * MXU_N, (h + 1) * MXU_N)
        s = _dot_nt(q_ref[:, cols], _block_diag_rows(k4_ref[:, cols]))
        inv = jnp.zeros((SEQ, MXU_N), F32)
        probs = []
        for g in range(ATTN_GROUP):
            sg = s[:, g * SEQ:(g + 1) * SEQ]
            sk = sink_ref[h * ATTN_GROUP + g]
            m = jnp.maximum(jnp.max(sg, axis=1, keepdims=True), sk)
            e = jnp.exp(sg - m)
            den = jnp.sum(e, axis=1, keepdims=True) + jnp.exp(sk - m)
            probs.append(e.astype(BF16))
            inv = jnp.where(chunk == g, 1.0 / den, inv)
        o = _dot(jnp.concatenate(probs, axis=1), _block_diag_rows(v4_ref[:, cols]))
        y_ref[:, cols] = (o * inv * _silu(z_ref[:, cols].astype(F32))).astype(BF16)


def _ctx_attn(sink, q, k4, v4, z):
    spec = pl.BlockSpec((SEQ, ATTN_WIDTH), lambda b: (b, 0))
    return pl.pallas_call(
        _ctx_attn_kernel,
        out_shape=jax.ShapeDtypeStruct((N_PROMPT_TOK, ATTN_WIDTH), BF16),
        grid=(BATCH,),
        in_specs=[pl.BlockSpec(memory_space=pltpu.SMEM), spec, spec, spec, spec],
        out_specs=spec,
        compiler_params=_params("arbitrary"),
        name="ctx_attn",
    )(sink, q, k4, v4, z)


def _lat_attn_kernel(sink_ref, q_ref, k4_ref, v4_ref, kc_ref, vc_ref, z_ref, y_ref,
                     kp, vp, k4c, v4c, e_scr):
    h = pl.program_id(1)
    n = pl.program_id(2)
    n_blocks = DEC_SEQ // Q_BLOCK

    @pl.when(n == 0)
    def _():
        zeros = jnp.zeros((Q_BLOCK, MXU_N), BF16)
        for ref, src in ((kp, k4_ref), (vp, v4_ref)):
            ref[0:Q_BLOCK, :] = zeros
            ref[Q_BLOCK:Q_BLOCK + DEC_SEQ, :] = src[...]
            ref[Q_BLOCK + DEC_SEQ:, :] = zeros
        k4c[...] = jnp.concatenate([kc_ref[...]] * ATTN_GROUP, axis=0).astype(BF16)
        v4c[...] = jnp.concatenate([vc_ref[...]] * ATTN_GROUP, axis=0).T.astype(BF16)

    rows4 = ATTN_GROUP * Q_BLOCK
    r = lax.broadcasted_iota(jnp.int32, (rows4, Q_BLOCK), 0) % Q_BLOCK
    c = lax.broadcasted_iota(jnp.int32, (rows4, Q_BLOCK), 1)
    in_left = c >= r
    in_right = c <= r
    sk = _sink_column(sink_ref, h, Q_BLOCK)
    for sub in range(Q_SUB):
        qb = n * Q_SUB + sub
        band = pl.ds(pl.multiple_of(qb * Q_BLOCK, Q_BLOCK), 3 * Q_BLOCK)
        qrows = slice(sub * Q_BLOCK, (sub + 1) * Q_BLOCK)
        qs = _stack_group_queries(q_ref[qrows, :])
        s_ctx = _dot(qs, k4c[...])
        s_lat = _dot_nt(qs, kp[band, :])
        blocks = [
            s_ctx,
            jnp.where(jnp.logical_and(in_left, qb > 0), s_lat[:, :Q_BLOCK], NEG_INF),
            s_lat[:, Q_BLOCK:2 * Q_BLOCK],
            jnp.where(jnp.logical_and(in_right, qb < n_blocks - 1), s_lat[:, 2 * Q_BLOCK:], NEG_INF),
        ]
        m = sk
        for s in blocks:
            m = jnp.maximum(m, jnp.max(s, axis=1, keepdims=True))
        den = jnp.exp(sk - m)
        lo = 0
        for s in blocks:
            e = jnp.exp(s - m)
            den = den + jnp.sum(e, axis=1, keepdims=True)
            e_scr[sub, :, lo:lo + s.shape[1]] = e.astype(BF16)
            lo += s.shape[1]
        o = (_dot(e_scr[sub, :, :PAST_LEN], v4c[...])
             + _dot(e_scr[sub, :, PAST_LEN:], vp[band, :])) * (1.0 / den)
        acc = _gather_group_outputs(o, Q_BLOCK)
        y_ref[qrows, :] = (acc * _silu(z_ref[qrows, :].astype(F32))).astype(BF16)


def _lat_attn(sink, q, k4, v4, kc, vc, j, z):
    step = Q_SUB * Q_BLOCK
    per_seq = DEC_SEQ // step
    padded = DEC_SEQ + 2 * Q_BLOCK
    seq_spec = pl.BlockSpec((DEC_SEQ, MXU_N), lambda b, h, n: (N_PROMPT_TOK // DEC_SEQ + b, h))
    ctx_spec = pl.BlockSpec((None, None, None, ATTN_HEAD_DIM, PAST_LEN), lambda b, h, n: (b, j, h, 0, 0))
    qz_spec = pl.BlockSpec((step, MXU_N), lambda b, h, n: (N_PROMPT_TOK // step + b * per_seq + n, h))
    return pl.pallas_call(
        _lat_attn_kernel,
        out_shape=jax.ShapeDtypeStruct((N_SAMPLE_TOK, ATTN_WIDTH), BF16),
        grid=(DEC_BATCH, ATTN_KV_HEADS, per_seq),
        in_specs=[pl.BlockSpec(memory_space=pltpu.SMEM), qz_spec, seq_spec, seq_spec,
                  ctx_spec, ctx_spec, qz_spec],
        out_specs=pl.BlockSpec((step, MXU_N), lambda b, h, n: (b * per_seq + n, h)),
        scratch_shapes=[pltpu.VMEM((padded, MXU_N), BF16), pltpu.VMEM((padded, MXU_N), BF16),
                        pltpu.VMEM((MXU_N, PAST_LEN), BF16), pltpu.VMEM((PAST_LEN, MXU_N), BF16),
                        pltpu.VMEM((Q_SUB, ATTN_GROUP * Q_BLOCK, PAST_LEN + 3 * Q_BLOCK), BF16)],
        compiler_params=_params("arbitrary", "arbitrary", "arbitrary"),
        name="lat_attn",
    )(sink, q, k4, v4, kc, vc, z)


def _out_kernel(*refs, n_y, n_x):
    t = PROJ
    y_refs, refs = refs[:n_y], refs[n_y:]
    wf_ref, refs = refs[0], refs[1:]
    x_refs, (gate_ref, o_ref, w_ref) = refs[:n_x], refs[n_x:]
    _cast_weight_once(wf_ref, w_ref)
    o_ref[...] = _read_parts(t, x_refs) + gate_ref[...] * _dot(_read_parts(t, y_refs), w_ref[...])


def _out_proj(y_parts, w_all, j, x_parts, mod):
    t = PROJ
    k = w_all.shape[1]
    return pl.pallas_call(
        functools.partial(_out_kernel, n_y=len(y_parts), n_x=len(x_parts)),
        out_shape=jax.ShapeDtypeStruct((N_TOK, D_MODEL), F32),
        grid=(t.n_tiles,),
        in_specs=(_part_specs(t, y_parts, k) + [_layer_spec(w_all, j)]
                  + _part_specs(t, x_parts, D_MODEL) + [t.mod(2)]),
        out_specs=t.rows(D_MODEL),
        scratch_shapes=[pltpu.VMEM((k, D_MODEL), BF16)],
        compiler_params=_params("arbitrary", vmem=VMEM_LIMIT),
        name="out_proj",
    )(*y_parts, w_all, *x_parts, mod)


def _out_final_kernel(yp_ref, ys_ref, wf_ref, x_ref, gate_ref, fg_ref, op_ref, os_ref, w_ref):
    t = PROJ
    i = pl.program_id(0)
    _cast_weight_once(wf_ref, w_ref)
    y = _read_parts(t, (yp_ref, ys_ref))
    r = _rms(x_ref[...] + gate_ref[...] * _dot(y, w_ref[...]), fg_ref[...])

    @pl.when(i < t.n_prompt)
    def _():
        op_ref[...] = r

    @pl.when(i >= t.n_prompt)
    def _():
        os_ref[...] = r


def _out_proj_final(y_parts, w_all, j, x, mod, final_g):
    t = PROJ
    k = w_all.shape[1]
    return pl.pallas_call(
        _out_final_kernel,
        out_shape=(jax.ShapeDtypeStruct((N_PROMPT_TOK, D_MODEL), F32),
                   jax.ShapeDtypeStruct((N_SAMPLE_TOK, D_MODEL), F32)),
        grid=(t.n_tiles,),
        in_specs=(_part_specs(t, y_parts, k)
                  + [_layer_spec(w_all, j), t.rows(D_MODEL), t.mod(2), _const_spec((1, D_MODEL))]),
        out_specs=(t.prompt_rows(D_MODEL), t.sample_rows(D_MODEL)),
        scratch_shapes=[pltpu.VMEM((k, D_MODEL), BF16)],
        compiler_params=_params("arbitrary", vmem=VMEM_LIMIT),
        name="out_proj_final",
    )(*y_parts, w_all, x, mod, final_g)


def _pool_in_kernel(x_ref, g_ref, sh_ref, sc_ref, wf_ref, u_ref, z_ref, w_ref):
    _cast_weight_once(wf_ref, w_ref)
    h = _norm_mod(x_ref[...], g_ref, sh_ref, sc_ref)

    def store_u(c, a):
        u_ref[:, c * MXU_N:(c + 1) * MXU_N] = a

    def store_z(c, a):
        z_ref[:, c * MXU_N:(c + 1) * MXU_N] = a.astype(BF16)

    _proj_chunks(h, w_ref, 0, D_MODEL, store_u)
    _proj_chunks(h, w_ref, D_MODEL, D_MODEL, store_z)


def _pool_in(x, g, mod, w_all, j):
    t = PROJ
    return pl.pallas_call(
        _pool_in_kernel,
        out_shape=(jax.ShapeDtypeStruct((N_TOK, D_MODEL), F32),
                   jax.ShapeDtypeStruct((N_TOK, D_MODEL), BF16)),
        grid=(t.n_tiles,),
        in_specs=[t.rows(D_MODEL), _const_spec((1, D_MODEL)), t.mod(0), t.mod(1),
                  _layer_spec(w_all, j)],
        out_specs=(t.rows(D_MODEL), t.rows(D_MODEL)),
        scratch_shapes=[pltpu.VMEM((D_MODEL, 2 * D_MODEL), BF16)],
        compiler_params=_params("arbitrary", vmem=VMEM_LIMIT),
        name="pool_in",
    )(x, g, mod, mod, w_all)


def _pool_out_kernel(u_ref, up_ref, un_ref, z_ref, wgf_ref, ps_ref, wof_ref, x_ref, gate_ref,
                     o_ref, pad_ref, y_ref, wg_ref, wo_ref):
    t = POOL
    i = pl.program_id(0)
    _cast_weight_once(wgf_ref, wg_ref)
    _cast_weight_once(wof_ref, wo_ref)
    is_dec = i >= t.n_prompt
    st = t.seq_tile(i)
    has_left = jnp.logical_and(is_dec, st != 0)
    has_right = jnp.logical_and(is_dec, st != t.per_dec_seq - 1)
    seq_len = jnp.where(is_dec, DEC_SEQ, SEQ)
    pad_ref[0:POOL_HALO, :] = jnp.where(has_left, up_ref[...], 0.0)
    pad_ref[POOL_HALO:POOL_HALO + t.tm, :] = u_ref[...]
    pad_ref[POOL_HALO + t.tm:, :] = jnp.where(has_right, un_ref[...], 0.0)
    pos = st * t.tm + lax.broadcasted_iota(jnp.int32, (t.tm, 1), 0)
    for g, w in enumerate(POOL_WINDOWS):
        left = w // 2
        right = w - 1 - left
        cols = slice(g * POOL_GROUP_DIM, (g + 1) * POOL_GROUP_DIM)
        s = pad_ref[POOL_HALO - left:POOL_HALO - left + t.tm, cols]
        for off in range(-left + 1, right + 1):
            s = s + pad_ref[POOL_HALO + off:POOL_HALO + off + t.tm, cols]
        cnt = (jnp.minimum(pos + right + 1, seq_len) - jnp.maximum(pos - left, 0)).astype(F32)
        d = (s / cnt - u_ref[:, cols]).astype(BF16)
        yg = _dot(d, wg_ref[g]) * ps_ref[:, cols] * _silu(z_ref[:, cols].astype(F32))
        y_ref[:, cols] = yg.astype(BF16)
    o_ref[...] = x_ref[...] + gate_ref[...] * _dot(y_ref[...], wo_ref[...])


def _pool_out(u, z, wg_all, ps, wo_all, j, x, mod):
    t = POOL
    per = t.tm // POOL_HALO
    n_halo = N_TOK // POOL_HALO
    return pl.pallas_call(
        _pool_out_kernel,
        out_shape=jax.ShapeDtypeStruct((N_TOK, D_MODEL), F32),
        grid=(t.n_tiles,),
        in_specs=[
            t.rows(D_MODEL),
            pl.BlockSpec((POOL_HALO, D_MODEL), lambda i: (jnp.maximum(i * per - 1, 0), 0)),
            pl.BlockSpec((POOL_HALO, D_MODEL), lambda i: (jnp.minimum((i + 1) * per, n_halo - 1), 0)),
            t.rows(D_MODEL),
            _layer_spec(wg_all, j),
            _const_spec((1, D_MODEL)),
            _layer_spec(wo_all, j),
            t.rows(D_MODEL),
            t.mod(2),
        ],
        out_specs=t.rows(D_MODEL),
        scratch_shapes=[pltpu.VMEM((t.tm + 2 * POOL_HALO, D_MODEL), F32),
                        pltpu.VMEM((t.tm, D_MODEL), BF16),
                        pltpu.VMEM((len(POOL_WINDOWS), POOL_GROUP_DIM, POOL_GROUP_DIM), BF16),
                        pltpu.VMEM((D_MODEL, D_MODEL), BF16)],
        compiler_params=_params("arbitrary", vmem=VMEM_LIMIT),
        name="pool_out",
    )(u, u, u, z, wg_all, ps, wo_all, x, mod)


def _ret_in_kernel(x_ref, g_ref, sh_ref, sc_ref, w_ref, q_ref, kt_ref, v_ref, z_ref, wkt_ref):
    t = PROJ

    @pl.when(pl.program_id(0) == 0)
    def _():
        for r in range(0, RET_QK_WIDTH, MXU_N):
            wk = w_ref[:, RET_QK_WIDTH + r:RET_QK_WIDTH + r + MXU_N]
            wkt_ref[r:r + MXU_N, :] = wk.astype(F32).T.astype(BF16)

    h = _norm_mod(x_ref[...], g_ref, sh_ref, sc_ref)

    def store(ref):
        def f(c, a):
            ref[:, c * MXU_N:(c + 1) * MXU_N] = a.astype(BF16)
        return f

    _proj_chunks(h, w_ref, 0, RET_QK_WIDTH, store(q_ref))
    kt = (_dot_nt(wkt_ref[...], h) * RET_KEY_DIM ** -0.5).astype(BF16)
    for c in range(t.tm // RET_CHUNK):
        kt_ref[c] = kt[:, c * RET_CHUNK:(c + 1) * RET_CHUNK]
    _proj_chunks(h, w_ref, 2 * RET_QK_WIDTH, RET_V_WIDTH, store(v_ref))
    _proj_chunks(h, w_ref, 2 * RET_QK_WIDTH + RET_V_WIDTH, RET_V_WIDTH, store(z_ref))


def _ret_in(x, g, mod, w):
    t = PROJ
    per = t.tm // RET_CHUNK
    n_in = 2 * RET_QK_WIDTH + 2 * RET_V_WIDTH
    return pl.pallas_call(
        _ret_in_kernel,
        out_shape=(jax.ShapeDtypeStruct((N_TOK, RET_QK_WIDTH), BF16),
                   jax.ShapeDtypeStruct((N_TOK // RET_CHUNK, RET_QK_WIDTH, RET_CHUNK), BF16),
                   jax.ShapeDtypeStruct((N_TOK, RET_V_WIDTH), BF16),
                   jax.ShapeDtypeStruct((N_TOK, RET_V_WIDTH), BF16)),
        grid=(t.n_tiles,),
        in_specs=[t.rows(D_MODEL), _const_spec((1, D_MODEL)), t.mod(0), t.mod(1),
                  _const_spec((D_MODEL, n_in))],
        out_specs=(t.rows(RET_QK_WIDTH),
                   pl.BlockSpec((per, RET_QK_WIDTH, RET_CHUNK), lambda i: (i, 0, 0)),
                   t.rows(RET_V_WIDTH), t.rows(RET_V_WIDTH)),
        scratch_shapes=[pltpu.VMEM((RET_QK_WIDTH, D_MODEL), BF16)],
        compiler_params=_params("arbitrary", vmem=VMEM_LIMIT),
        name="ret_in",
    )(x, g, mod, mod, w)


def _pos(shape, axis):
    return lax.broadcasted_iota(jnp.int32, shape, axis).astype(F32)


def _ret_tables_kernel(lgf_ref, lgb_ref, decay_ref, row_ref, col_ref, cdec_ref):
    h = pl.program_id(0)
    lg_f = lgf_ref[h]
    lg_b = lgb_ref[h]
    c = RET_CHUNK
    diff = _pos((c, c), 0) - _pos((c, c), 1)
    fwd = jnp.where(diff >= 0, jnp.exp(jnp.maximum(diff, 0.0) * lg_f), 0.0)
    bwd = jnp.where(diff <= 0, jnp.exp(jnp.maximum(-diff, 0.0) * lg_b), 0.0)
    decay_ref[...] = fwd + bwd
    j = _pos((RET_TAB_ROWS, c), 1)
    row_ref[0] = jnp.exp((c - 1.0 - j) * lg_f)
    row_ref[1] = jnp.exp(j * lg_b)
    i = _pos((c, LANES), 0)
    col_ref[0] = jnp.exp((i + 1.0) * lg_f)
    col_ref[1] = jnp.exp((c - i) * lg_b)
    full = jnp.full((RET_TAB_ROWS, RET_VAL_DIM), float(c), F32)
    cdec_ref[0] = jnp.exp(full * lg_f)
    cdec_ref[1] = jnp.exp(full * lg_b)


def _ret_tables(lg_f, lg_b):
    smem = pl.BlockSpec(memory_space=pltpu.SMEM)
    c = RET_CHUNK
    shapes = ((c, c), (2, RET_TAB_ROWS, c), (2, c, LANES), (2, RET_TAB_ROWS, RET_VAL_DIM))
    return pl.pallas_call(
        _ret_tables_kernel,
        out_shape=tuple(jax.ShapeDtypeStruct((RET_HEADS,) + s, F32) for s in shapes),
        grid=(RET_HEADS,),
        in_specs=[smem, smem],
        out_specs=tuple(pl.BlockSpec((None,) + s, lambda h, n=len(s): (h,) + (0,) * n) for s in shapes),
        compiler_params=_params("arbitrary"),
        name="ret_tables",
    )(lg_f, lg_b)


def _group_norm_gate(o, gn, z):
    mu = jnp.mean(o, axis=-1, keepdims=True)
    var = jnp.mean(jnp.square(o - mu), axis=-1, keepdims=True)
    on = (o - mu) * lax.rsqrt(var + EPS)
    return (on * gn * _silu(z.astype(F32))).astype(BF16)


def _ret_ctx_kernel(q_ref, kt_ref, v_ref, z_ref, gn_ref, decay_ref, row_ref, y_ref, sf_ref, sb_ref):
    for h in range(RET_HEADS):
        kc = slice(h * RET_KEY_DIM, (h + 1) * RET_KEY_DIM)
        vc = slice(h * RET_VAL_DIM, (h + 1) * RET_VAL_DIM)
        kt = kt_ref[kc, :]
        v = v_ref[:, vc]
        att = (_dot(q_ref[:, kc], kt) * decay_ref[h]).astype(BF16)
        y_ref[:, vc] = _group_norm_gate(_dot(att, v), gn_ref[:, vc], z_ref[:, vc])
        ktf = kt.astype(F32)
        sf_ref[h] = _dot((ktf * row_ref[h, 0, 0:1, :]).astype(BF16), v)
        sb_ref[h] = _dot((ktf * row_ref[h, 1, 0:1, :]).astype(BF16), v)


def _ret_ctx(q, kt, v, z, gn, decay, row):
    c = RET_CHUNK
    wide = lambda w: pl.BlockSpec((SEQ, w), lambda b: (b, 0))
    st_spec = pl.BlockSpec((None, RET_HEADS, RET_KEY_DIM, RET_VAL_DIM), lambda b: (b, 0, 0, 0))
    st_shape = jax.ShapeDtypeStruct((BATCH, RET_HEADS, RET_KEY_DIM, RET_VAL_DIM), F32)
    return pl.pallas_call(
        _ret_ctx_kernel,
        out_shape=(jax.ShapeDtypeStruct((N_PROMPT_TOK, RET_V_WIDTH), BF16), st_shape, st_shape),
        grid=(BATCH,),
        in_specs=[wide(RET_QK_WIDTH), pl.BlockSpec((None, RET_QK_WIDTH, c), lambda b: (b, 0, 0)),
                  wide(RET_V_WIDTH), wide(RET_V_WIDTH), _const_spec((1, RET_V_WIDTH)),
                  _const_spec((RET_HEADS, c, c)), _const_spec((RET_HEADS, 2, RET_TAB_ROWS, c))],
        out_specs=(wide(RET_V_WIDTH), st_spec, st_spec),
        compiler_params=_params("arbitrary", vmem=VMEM_LIMIT),
        name="ret_ctx",
    )(q, kt, v, z, gn, decay, row)


def _ret_lat_kernel(q_ref, kt_ref, v_ref, z_ref, gn_ref, decay_ref, row_ref, col_ref, cdec_ref,
                    s0f_ref, s0b_ref, y_ref, sf_all, sb_all, sf_acc, sb_acc):
    c = RET_CHUNK
    n_chunks = DEC_SEQ // c
    rows_of = lambda ci: pl.ds(pl.multiple_of(ci * c, c), c)

    sf_acc[...] = s0f_ref[...]
    sb_acc[...] = s0b_ref[...]

    def scan_step(i, carry):
        cf = i
        cb = n_chunks - 1 - i
        sf_all[cf] = sf_acc[...].astype(BF16)
        sb_all[cb] = sb_acc[...].astype(BF16)
        uf = _dot((kt_ref[cf].astype(F32) * row_ref[0, 0:1, :]).astype(BF16), v_ref[rows_of(cf), :])
        ub = _dot((kt_ref[cb].astype(F32) * row_ref[1, 0:1, :]).astype(BF16), v_ref[rows_of(cb), :])
        sf_acc[...] = sf_acc[...] * cdec_ref[0, 0:1, :] + uf
        sb_acc[...] = sb_acc[...] * cdec_ref[1, 0:1, :] + ub
        return carry

    lax.fori_loop(0, n_chunks, scan_step, 0)

    def out_step(ci, carry):
        rows = rows_of(ci)
        q = q_ref[rows, :]
        qf = q.astype(F32)
        qdec_f = jnp.concatenate([col_ref[0]] * (RET_KEY_DIM // LANES), axis=1)
        qdec_b = jnp.concatenate([col_ref[1]] * (RET_KEY_DIM // LANES), axis=1)
        att = (_dot(q, kt_ref[ci]) * decay_ref[...]).astype(BF16)
        o = (_dot(att, v_ref[rows, :])
             + _dot((qf * qdec_f).astype(BF16), sf_all[ci])
             + _dot((qf * qdec_b).astype(BF16), sb_all[ci]))
        y_ref[rows, :] = _group_norm_gate(o, gn_ref[...], z_ref[rows, :])
        return carry

    lax.fori_loop(0, n_chunks, out_step, 0, unroll=2)


def _ret_lat(q, kt, v, z, gn, decay, row, col, cdec, s0f, s0b):
    c = RET_CHUNK
    n_chunks = DEC_SEQ // c
    row0 = N_PROMPT_TOK // DEC_SEQ
    qk_spec = pl.BlockSpec((DEC_SEQ, RET_KEY_DIM), lambda b, h: (row0 + b, h))
    v_spec = pl.BlockSpec((DEC_SEQ, RET_VAL_DIM), lambda b, h: (row0 + b, h))
    st_spec = pl.BlockSpec((None, None, RET_KEY_DIM, RET_VAL_DIM), lambda b, h: (b, h, 0, 0))
    tab = lambda *s: pl.BlockSpec((None,) + s, lambda b, h: (h,) + (0,) * len(s))
    states = pltpu.VMEM((n_chunks, RET_KEY_DIM, RET_VAL_DIM), BF16)
    acc = pltpu.VMEM((RET_KEY_DIM, RET_VAL_DIM), F32)
    return pl.pallas_call(
        _ret_lat_kernel,
        out_shape=jax.ShapeDtypeStruct((N_SAMPLE_TOK, RET_V_WIDTH), BF16),
        grid=(DEC_BATCH, RET_HEADS),
        in_specs=[qk_spec,
                  pl.BlockSpec((n_chunks, RET_KEY_DIM, c), lambda b, h: (row0 + b, h, 0)),
                  v_spec, v_spec, pl.BlockSpec((1, RET_VAL_DIM), lambda b, h: (0, h)),
                  tab(c, c), tab(2, RET_TAB_ROWS, c), tab(2, c, LANES), tab(2, RET_TAB_ROWS, RET_VAL_DIM),
                  st_spec, st_spec],
        out_specs=pl.BlockSpec((DEC_SEQ, RET_VAL_DIM), lambda b, h: (b, h)),
        scratch_shapes=[states, states, acc, acc],
        compiler_params=_params("arbitrary", "arbitrary", vmem=VMEM_LIMIT),
        name="ret_lat",
    )(q, kt, v, z, gn, decay, row, col, cdec, s0f, s0b)


def _rope_tables(tm):
    n_rows = DEC_SEQ // GRID_W
    rows = jnp.repeat(jnp.arange(n_rows), GRID_W).astype(F32)
    cols = jnp.tile(jnp.arange(GRID_W), n_rows).astype(F32)
    half = ATTN_HEAD_DIM // 4
    inv = ROPE_BASE ** (-jnp.arange(half, dtype=F32) / half)
    ang_r = rows[:, None] * inv[None, :]
    ang_c = cols[:, None] * inv[None, :]
    cos = jnp.concatenate([jnp.cos(ang_r), jnp.cos(ang_r), jnp.cos(ang_c), jnp.cos(ang_c)], axis=-1)
    sin = jnp.concatenate([-jnp.sin(ang_r), jnp.sin(ang_r), -jnp.sin(ang_c), jnp.sin(ang_c)], axis=-1)
    cos = jnp.concatenate([jnp.ones((tm, ATTN_HEAD_DIM), F32), cos], axis=0)
    sin = jnp.concatenate([jnp.zeros((tm, ATTN_HEAD_DIM), F32), sin], axis=0)
    return jnp.tile(cos, (1, 2)), jnp.tile(sin, (1, 2))


def kernel(x_prompt, x_sample, cache_k, cache_v, state_fwd, state_bwd, c, c_ctx, norm_g, ada_w, ada_b, attn_w_in, attn_w_out, attn_sink, pool_w_in, pool_w_grp, pool_scale, pool_w_out, ret_w_in, ret_decay_fwd, ret_decay_bwd, ret_gn_g, ret_w_out, final_g):
    x_parts = (x_prompt.reshape(N_PROMPT_TOK, D_MODEL), x_sample.reshape(N_SAMPLE_TOK, D_MODEL))
    cond = jnp.concatenate([c_ctx[None, :], c,
                            jnp.zeros((N_COND - 1 - DEC_BATCH, D_MODEL), F32)], axis=0)
    mods = _ada_table(cond, ada_w, ada_b).reshape(DEPTH, N_COND, 1, 3 * D_MODEL)
    cos_t, sin_t = _rope_tables(PROJ.tm)

    to_kernel = lambda a: jnp.transpose(a, (0, 1, 3, 4, 2))
    from_kernel = lambda a: jnp.transpose(
        a.reshape(a.shape[0], a.shape[1], ATTN_KV_HEADS, ATTN_HEAD_DIM, a.shape[3]), (0, 1, 4, 2, 3))
    ctx_k, ctx_v = to_kernel(cache_k), to_kernel(cache_v)

    caches = ()
    new_sf = new_sb = None
    for i in range(DEPTH):
        kind, j = i % N_MIXERS, i // N_MIXERS
        g = norm_g[i].reshape(1, D_MODEL)
        mod = mods[i]
        last = i == DEPTH - 1
        if kind == 0:
            q, k4, v4, z, *caches = _attn_in(x_parts, g, mod, attn_w_in, j, cos_t, sin_t, tuple(caches))
            y_parts = (_ctx_attn(attn_sink[j], q, k4, v4, z),
                       _lat_attn(attn_sink[j], q, k4, v4, ctx_k, ctx_v, j, z))
            w_out = attn_w_out
        elif kind == 1:
            (x,) = x_parts
            u, z = _pool_in(x, g, mod, pool_w_in, j)
            assert not last
            x_parts = (_pool_out(u, z, pool_w_grp, pool_scale[j].reshape(1, D_MODEL), pool_w_out, j,
                                 x, mod),)
            continue
        else:
            (x,) = x_parts
            lg_f = jax.nn.log_sigmoid(ret_decay_fwd[j].astype(F32))
            lg_b = jax.nn.log_sigmoid(ret_decay_bwd[j].astype(F32))
            gn = ret_gn_g[j].reshape(1, RET_V_WIDTH)
            q, kt, v, z = _ret_in(x, g, mod, ret_w_in[j].astype(BF16))
            decay, row, col, cdec = _ret_tables(lg_f, lg_b)
            y_ctx, new_sf, new_sb = _ret_ctx(q, kt, v, z, gn, decay, row)
            y_parts = (y_ctx, _ret_lat(q, kt, v, z, gn, decay, row, col, cdec,
                                       state_fwd[:, j], state_bwd[:, j]))
            w_out = ret_w_out
        if last:
            (x,) = x_parts
            y_prompt, y_sample = _out_proj_final(y_parts, w_out, j, x, mod, final_g.reshape(1, D_MODEL))
        else:
            x_parts = (_out_proj(y_parts, w_out, j, x_parts, mod),)
    new_k, new_v = caches
    return (y_prompt.reshape(BATCH, SEQ, D_MODEL), y_sample.reshape(DEC_BATCH, DEC_SEQ, D_MODEL),
            from_kernel(new_k), from_kernel(new_v), new_sf[:, None], new_sb[:, None])
```

```python
import functools
from typing import NamedTuple

import jax
import jax.numpy as jnp
from jax import lax
from jax.experimental import pallas as pl
from jax.experimental.pallas import tpu as pltpu

F32 = jnp.float32
BF16 = jnp.bfloat16

D_MODEL = 1024
BATCH = 16
SEQ = 256
DEPTH = 4
DEC_BATCH = 2
DEC_SEQ = 2048
PAST_LEN = 512
GRID_W = 64
N_MIXERS = 3
ATTN_HEADS = 16
ATTN_KV_HEADS = 4
ATTN_HEAD_DIM = 64
ATTN_GROUP = 4
ATTN_WIDTH = 1024
ATTN_KV_WIDTH = 256
WINDOW = 128
ROPE_BASE = 10000.0
POOL_WINDOWS = (2, 4, 8, 16)
POOL_GROUP_DIM = 256
RET_HEADS = 4
RET_KEY_DIM = 256
RET_VAL_DIM = 512
RET_QK_WIDTH = 1024
RET_V_WIDTH = 2048
EPS = 1e-6
NEG_INF = -1e30

N_PROMPT_TOK = BATCH * SEQ
N_SAMPLE_TOK = DEC_BATCH * DEC_SEQ
N_TOK = N_PROMPT_TOK + N_SAMPLE_TOK
N_COND = 8
LANES = 128
MXU_N = 256
Q_BLOCK = 128
Q_SUB = 4
RET_CHUNK = 256
RET_TAB_ROWS = 8
POOL_HALO = 8
VMEM_LIMIT = 48 * 1024 * 1024


class Tiling(NamedTuple):
    tm: int

    @property
    def n_tiles(self):
        return N_TOK // self.tm

    @property
    def n_prompt(self):
        return N_PROMPT_TOK // self.tm

    @property
    def per_dec_seq(self):
        return DEC_SEQ // self.tm

    def cond(self, i):
        return jnp.where(i < self.n_prompt, 0, 1 + (i - self.n_prompt) // self.per_dec_seq)

    def seq_tile(self, i):
        return jnp.where(i < self.n_prompt, 0, (i - self.n_prompt) % self.per_dec_seq)

    def rows(self, width):
        return pl.BlockSpec((self.tm, width), lambda i: (i, 0))

    def prompt_rows(self, width):
        return pl.BlockSpec((self.tm, width), lambda i: (jnp.minimum(i, self.n_prompt - 1), 0))

    def sample_rows(self, width):
        return pl.BlockSpec((self.tm, width), lambda i: (jnp.maximum(i - self.n_prompt, 0), 0))

    def mod(self, part):
        return pl.BlockSpec((None, 1, D_MODEL), lambda i: (self.cond(i), 0, part))


PROJ = Tiling(512)
POOL = Tiling(SEQ)


def _silu(z):
    return z * (1.0 / (1.0 + jnp.exp(-z)))


def _dot(a, b):
    return jnp.dot(a, b, preferred_element_type=F32)


def _dot_nt(a, b):
    return lax.dot_general(a, b, (((1,), (1,)), ((), ())), preferred_element_type=F32)


def _params(*sem, vmem=None):
    return pltpu.CompilerParams(dimension_semantics=sem, vmem_limit_bytes=vmem)


def _const_spec(shape):
    nd = len(shape)
    return pl.BlockSpec(shape, lambda *_: (0,) * nd, pipeline_mode=pl.Buffered(1))


def _part_specs(t, parts, width):
    if len(parts) == 1:
        return [t.rows(width)]
    return [t.prompt_rows(width), t.sample_rows(width)]


def _read_parts(t, refs):
    if len(refs) == 1:
        return refs[0][...]
    return jnp.where(pl.program_id(0) < t.n_prompt, refs[0][...], refs[1][...])


def _ada_kernel(cond_ref, w_ref, b_ref, o_ref):
    s = _silu(cond_ref[...])
    w = w_ref[...]
    rows = [jnp.sum(s[:, c:c + 1] * w, axis=0, keepdims=True) + b_ref[...] for c in range(1 + DEC_BATCH)]
    rows.append(jnp.zeros((N_COND - len(rows), w.shape[1]), F32))
    o_ref[...] = jnp.concatenate(rows, axis=0)


def _ada_table(cond, ada_w, ada_b):
    tn = 768
    return pl.pallas_call(
        _ada_kernel,
        out_shape=jax.ShapeDtypeStruct((DEPTH, N_COND, 3 * D_MODEL), F32),
        grid=(DEPTH, 3 * D_MODEL // tn),
        in_specs=[
            pl.BlockSpec((D_MODEL, N_COND), lambda l, n: (0, 0)),
            pl.BlockSpec((None, D_MODEL, tn), lambda l, n: (l, 0, n)),
            pl.BlockSpec((None, 1, tn), lambda l, n: (l, 0, n)),
        ],
        out_specs=pl.BlockSpec((None, N_COND, tn), lambda l, n: (l, 0, n)),
        compiler_params=_params("arbitrary", "arbitrary"),
        name="ada_table",
    )(cond, ada_w, ada_b.reshape(DEPTH, 1, 3 * D_MODEL))


def _rms(x, g):
    return x * lax.rsqrt(jnp.mean(x * x, axis=-1, keepdims=True) + EPS) * g


def _norm_mod(x, g_ref, sh_ref, sc_ref):
    return (_rms(x, g_ref[...]) * (1.0 + sc_ref[...]) + sh_ref[...]).astype(BF16)


def _proj_chunks(h, w_ref, lo, width, store):
    for c in range(width // MXU_N):
        store(c, _dot(h, w_ref[:, lo + c * MXU_N:lo + (c + 1) * MXU_N]))


def _layer_spec(w, j):
    nd = w.ndim - 1
    return pl.BlockSpec((None,) + w.shape[1:], lambda *_: (j,) + (0,) * nd, pipeline_mode=pl.Buffered(1))


def _cast_weight_once(w_ref, wb_ref):
    @pl.when(pl.program_id(0) == 0)
    def _():
        rows = wb_ref.shape[-2]
        for r in range(0, rows, MXU_N):
            wb_ref[..., r:r + MXU_N, :] = w_ref[..., r:r + MXU_N, :].astype(BF16)


def _rep4(a, h):
    half = a[:, (h // 2) * LANES:(h // 2 + 1) * LANES]
    lane = lax.broadcasted_iota(jnp.int32, half.shape, 1)
    keep = (lane < ATTN_HEAD_DIM) if h % 2 == 0 else (lane >= ATTN_HEAD_DIM)
    m = jnp.where(keep, half, 0.0)
    s = m + pltpu.roll(m, ATTN_HEAD_DIM, 1)
    return jnp.concatenate([s, s], axis=1)


def _attn_in_kernel(*refs, n_x, n_alias):
    t = PROJ
    x_refs, refs = refs[:n_x], refs[n_x:]
    (g_ref, sh_ref, sc_ref, wf_ref, cos_ref, sin_ref), refs = refs[:6], refs[6 + n_alias:]
    q_ref, k4_ref, v4_ref, z_ref, kc_ref, vc_ref, w_ref = refs
    i = pl.program_id(0)
    _cast_weight_once(wf_ref, w_ref)
    h = _norm_mod(_read_parts(t, x_refs), g_ref, sh_ref, sc_ref)
    cos = cos_ref[...]
    sin = sin_ref[...]
    lane = lax.broadcasted_iota(jnp.int32, (t.tm, LANES), 1)
    first = (lane % (ATTN_HEAD_DIM // 2)) < ATTN_HEAD_DIM // 4

    def rope(a):
        rot = jnp.where(first, pltpu.roll(a, LANES - ATTN_HEAD_DIM // 4, 1),
                        pltpu.roll(a, ATTN_HEAD_DIM // 4, 1))
        return a * cos + rot * sin

    def rope_wide(a):
        return jnp.concatenate(
            [rope(a[:, s * LANES:(s + 1) * LANES]) for s in range(MXU_N // LANES)], axis=1)

    scale = ATTN_HEAD_DIM ** -0.5

    def store_q(c, a):
        q_ref[:, c * MXU_N:(c + 1) * MXU_N] = (rope_wide(a) * scale).astype(BF16)

    def store_z(c, a):
        z_ref[:, c * MXU_N:(c + 1) * MXU_N] = a.astype(BF16)

    _proj_chunks(h, w_ref, 0, ATTN_WIDTH, store_q)
    k = rope_wide(_dot(h, w_ref[:, ATTN_WIDTH:ATTN_WIDTH + ATTN_KV_WIDTH]))
    v = _dot(h, w_ref[:, ATTN_WIDTH + ATTN_KV_WIDTH:ATTN_WIDTH + 2 * ATTN_KV_WIDTH])
    for hh in range(ATTN_KV_HEADS):
        k4_ref[:, hh * MXU_N:(hh + 1) * MXU_N] = _rep4(k, hh).astype(BF16)
        v4_ref[:, hh * MXU_N:(hh + 1) * MXU_N] = _rep4(v, hh).astype(BF16)
    _proj_chunks(h, w_ref, ATTN_WIDTH + 2 * ATTN_KV_WIDTH, ATTN_WIDTH, store_z)

    @pl.when(i < t.n_prompt)
    def _():
        for s in range(t.tm // SEQ):
            kc_ref[s] = k[s * SEQ:(s + 1) * SEQ, :].T
            vc_ref[s] = v[s * SEQ:(s + 1) * SEQ, :].T


def _attn_in(x_parts, g, mod, w_all, j, cos_t, sin_t, caches):
    t = PROJ
    n_in = 2 * ATTN_WIDTH + 2 * ATTN_KV_WIDTH
    per = t.tm // SEQ
    rope_spec = pl.BlockSpec(
        (t.tm, LANES), lambda i: (jnp.where(i < t.n_prompt, 0, 1 + t.seq_tile(i)), 0))
    wide = jax.ShapeDtypeStruct((N_TOK, ATTN_WIDTH), BF16)
    n_attn = w_all.shape[0]
    cache = jax.ShapeDtypeStruct((BATCH, n_attn, ATTN_KV_WIDTH, SEQ), F32)
    cache_spec = pl.BlockSpec((per, None, ATTN_KV_WIDTH, SEQ),
                              lambda i: (jnp.minimum(i, t.n_prompt - 1), j, 0, 0))
    n_x = len(x_parts)
    return pl.pallas_call(
        functools.partial(_attn_in_kernel, n_x=n_x, n_alias=len(caches)),
        out_shape=(wide, wide, wide, wide, cache, cache),
        grid=(t.n_tiles,),
        in_specs=_part_specs(t, x_parts, D_MODEL) + [
            _const_spec((1, D_MODEL)), t.mod(0), t.mod(1), _layer_spec(w_all, j),
            rope_spec, rope_spec] + [pl.BlockSpec(memory_space=pl.ANY)] * len(caches),
        out_specs=(t.rows(ATTN_WIDTH),) * 4 + (cache_spec,) * 2,
        scratch_shapes=[pltpu.VMEM((D_MODEL, n_in), BF16)],
        input_output_aliases={n_x + 6 + c: 4 + c for c in range(len(caches))},
        compiler_params=_params("arbitrary", vmem=VMEM_LIMIT),
        name="attn_in",
    )(*x_parts, g, mod, mod, w_all, cos_t, sin_t, *caches)


def _stack_group_queries(q):
    qf = q.astype(F32)
    chunk = lax.broadcasted_iota(jnp.int32, qf.shape, 1) // ATTN_HEAD_DIM
    return jnp.concatenate(
        [jnp.where(chunk == g, qf, 0.0) for g in range(ATTN_GROUP)], axis=0).astype(BF16)


def _gather_group_outputs(o, rows):
    chunk = lax.broadcasted_iota(jnp.int32, (rows, MXU_N), 1) // ATTN_HEAD_DIM
    acc = jnp.zeros((rows, MXU_N), F32)
    for g in range(ATTN_GROUP):
        acc = acc + jnp.where(chunk == g, o[g * rows:(g + 1) * rows], 0.0)
    return acc


def _sink_column(sink_ref, h, rows):
    grp = lax.broadcasted_iota(jnp.int32, (ATTN_GROUP * rows, 1), 0) // rows
    col = jnp.zeros((ATTN_GROUP * rows, 1), F32)
    for g in range(ATTN_GROUP):
        col = jnp.where(grp == g, sink_ref[h * ATTN_GROUP + g], col)
    return col


def _chunk_rows(dtype):
    chunk = lax.broadcasted_iota(jnp.int32, (1, MXU_N), 1) // ATTN_HEAD_DIM
    return [(chunk == g).astype(F32).astype(dtype) for g in range(ATTN_GROUP)]


def _block_diag_rows(x4):
    return jnp.concatenate([x4 * m for m in _chunk_rows(x4.dtype)], axis=0)


def _ctx_attn_kernel(sink_ref, q_ref, k4_ref, v4_ref, z_ref, y_ref):
    chunk = lax.broadcasted_iota(jnp.int32, (SEQ, MXU_N), 1) // ATTN_HEAD_DIM
    for h in range(ATTN_KV_HEADS):
        cols = slice(h * MXU_N, (h + 1) * MXU_N)
        s = _dot_nt(q_ref[:, cols], _block_diag_rows(k4_ref[:, cols]))
        inv = jnp.zeros((SEQ, MXU_N), F32)
        probs = []
        for g in range(ATTN_GROUP):
            sg = s[:, g * SEQ:(g + 1) * SEQ]
            sk = sink_ref[h * ATTN_GROUP + g]
            m = jnp.maximum(jnp.max(sg, axis=1, keepdims=True), sk)
            e = jnp.exp(sg - m)
            den = jnp.sum(e, axis=1, keepdims=True) + jnp.exp(sk - m)
            probs.append(e.astype(BF16))
            inv = jnp.where(chunk == g, 1.0 / den, inv)
        o = _dot(jnp.concatenate(probs, axis=1), _block_diag_rows(v4_ref[:, cols]))
        y_ref[:, cols] = (o * inv * _silu(z_ref[:, cols].astype(F32))).astype(BF16)


def _ctx_attn(sink, q, k4, v4, z):
    spec = pl.BlockSpec((SEQ, ATTN_WIDTH), lambda b: (b, 0))
    return pl.pallas_call(
        _ctx_attn_kernel,
        out_shape=jax.ShapeDtypeStruct((N_PROMPT_TOK, ATTN_WIDTH), BF16),
        grid=(BATCH,),
        in_specs=[pl.BlockSpec(memory_space=pltpu.SMEM), spec, spec, spec, spec],
        out_specs=spec,
        compiler_params=_params("arbitrary"),
        name="ctx_attn",
    )(sink, q, k4, v4, z)


LAT_STEP = Q_SUB * Q_BLOCK
LAT_PER_SEQ = DEC_SEQ // LAT_STEP
LAT_BLOCKS = DEC_BATCH * ATTN_KV_HEADS * LAT_PER_SEQ


def _lat_block(blk):
    return (blk // (ATTN_KV_HEADS * LAT_PER_SEQ), (blk // LAT_PER_SEQ) % ATTN_KV_HEADS, blk % LAT_PER_SEQ)


def _band_rows(qb):
    return pl.ds(pl.multiple_of(qb * Q_BLOCK, Q_BLOCK), 3 * Q_BLOCK)


def _pad_sequence(dst, src):
    zeros = jnp.zeros((Q_BLOCK, MXU_N), BF16)
    dst[0:Q_BLOCK, :] = zeros
    dst[Q_BLOCK:Q_BLOCK + DEC_SEQ, :] = src[...]
    dst[Q_BLOCK + DEC_SEQ:, :] = zeros


def _lat_attn_kernel(sink_ref, q_ref, k4_ref, v4_ref, kc_ref, vc_ref, z_ref, y_ref,
                     kp, vp, k4c, v4c, sc_a, sl_a, sc_b, sl_b, e_scr):
    t = pl.program_id(0)
    _, _, n1 = _lat_block(jnp.minimum(t, LAT_BLOCKS - 1))
    _, h0, n0 = _lat_block(jnp.maximum(t - 1, 0))
    n_blocks = DEC_SEQ // Q_BLOCK

    @pl.when(t == 0)
    def _():
        sc_b[...] = jnp.zeros(sc_b.shape, F32)
        sl_b[...] = jnp.zeros(sl_b.shape, F32)

    @pl.when(n1 == 0)
    def _():
        _pad_sequence(kp, k4_ref)
        k4c[...] = jnp.concatenate([kc_ref[...]] * ATTN_GROUP, axis=0).astype(BF16)

    @pl.when(n0 == 0)
    def _():
        _pad_sequence(vp, v4_ref)
        v4c[...] = jnp.concatenate([vc_ref[...]] * ATTN_GROUP, axis=0).T.astype(BF16)

    @pl.when(t % 2 == 0)
    def _():
        _lat_stages(sink_ref, q_ref, z_ref, y_ref, kp, vp, k4c, v4c, e_scr, n1, h0, n0,
                    sc_a, sl_a, sc_b, sl_b)

    @pl.when(t % 2 == 1)
    def _():
        _lat_stages(sink_ref, q_ref, z_ref, y_ref, kp, vp, k4c, v4c, e_scr, n1, h0, n0,
                    sc_b, sl_b, sc_a, sl_a)


def _lat_stages(sink_ref, q_ref, z_ref, y_ref, kp, vp, k4c, v4c, e_scr, n1, h0, n0,
                sc_w, sl_w, sc_r, sl_r):
    n_blocks = DEC_SEQ // Q_BLOCK
    for sub in range(Q_SUB):
        qs = _stack_group_queries(q_ref[sub * Q_BLOCK:(sub + 1) * Q_BLOCK, :])
        sc_w[sub] = _dot(qs, k4c[...])
        sl_w[sub] = _dot_nt(qs, kp[_band_rows(n1 * Q_SUB + sub), :])

    rows4 = ATTN_GROUP * Q_BLOCK
    r = lax.broadcasted_iota(jnp.int32, (rows4, Q_BLOCK), 0) % Q_BLOCK
    c = lax.broadcasted_iota(jnp.int32, (rows4, Q_BLOCK), 1)
    in_left = c >= r
    in_right = c <= r
    sk = _sink_column(sink_ref, h0, Q_BLOCK)
    for sub in range(Q_SUB):
        qb = n0 * Q_SUB + sub
        band = _band_rows(qb)
        qrows = slice(sub * Q_BLOCK, (sub + 1) * Q_BLOCK)
        s_ctx = sc_r[sub]
        s_lat = sl_r[sub]
        blocks = [
            s_ctx,
            jnp.where(jnp.logical_and(in_left, qb > 0), s_lat[:, :Q_BLOCK], NEG_INF),
            s_lat[:, Q_BLOCK:2 * Q_BLOCK],
            jnp.where(jnp.logical_and(in_right, qb < n_blocks - 1), s_lat[:, 2 * Q_BLOCK:], NEG_INF),
        ]
        m = sk
        for s in blocks:
            m = jnp.maximum(m, jnp.max(s, axis=1, keepdims=True))
        den = jnp.exp(sk - m)
        lo = 0
        for s in blocks:
            e = jnp.exp(s - m)
            den = den + jnp.sum(e, axis=1, keepdims=True)
            e_scr[sub, :, lo:lo + s.shape[1]] = e.astype(BF16)
            lo += s.shape[1]
        o = (_dot(e_scr[sub, :, :PAST_LEN], v4c[...])
             + _dot(e_scr[sub, :, PAST_LEN:], vp[band, :])) * (1.0 / den)
        acc = _gather_group_outputs(o, Q_BLOCK)
        y_ref[qrows, :] = (acc * _silu(z_ref[qrows, :].astype(F32))).astype(BF16)


def _lat_attn(sink, q, k4, v4, kc, vc, j, z):
    padded = DEC_SEQ + 2 * Q_BLOCK
    rows4 = ATTN_GROUP * Q_BLOCK
    row0 = N_PROMPT_TOK // LAT_STEP
    seq0 = N_PROMPT_TOK // DEC_SEQ
    stage1 = lambda t: _lat_block(jnp.minimum(t, LAT_BLOCKS - 1))
    stage2 = lambda t: _lat_block(jnp.maximum(t - 1, 0))

    def spec(shape, stage, index):
        return pl.BlockSpec(shape, lambda t: index(*stage(t)))

    tile = (LAT_STEP, MXU_N)
    seq = (DEC_SEQ, MXU_N)
    ctx = (None, None, None, ATTN_HEAD_DIM, PAST_LEN)
    return pl.pallas_call(
        _lat_attn_kernel,
        out_shape=jax.ShapeDtypeStruct((N_SAMPLE_TOK, ATTN_WIDTH), BF16),
        grid=(LAT_BLOCKS + 1,),
        in_specs=[pl.BlockSpec(memory_space=pltpu.SMEM),
                  spec(tile, stage1, lambda b, h, n: (row0 + b * LAT_PER_SEQ + n, h)),
                  spec(seq, stage1, lambda b, h, n: (seq0 + b, h)),
                  spec(seq, stage2, lambda b, h, n: (seq0 + b, h)),
                  spec(ctx, stage1, lambda b, h, n: (b, j, h, 0, 0)),
                  spec(ctx, stage2, lambda b, h, n: (b, j, h, 0, 0)),
                  spec(tile, stage2, lambda b, h, n: (row0 + b * LAT_PER_SEQ + n, h))],
        out_specs=spec(tile, stage2, lambda b, h, n: (b * LAT_PER_SEQ + n, h)),
        scratch_shapes=[pltpu.VMEM((padded, MXU_N), BF16), pltpu.VMEM((padded, MXU_N), BF16),
                        pltpu.VMEM((MXU_N, PAST_LEN), BF16), pltpu.VMEM((PAST_LEN, MXU_N), BF16),
                        pltpu.VMEM((Q_SUB, rows4, PAST_LEN), F32),
                        pltpu.VMEM((Q_SUB, rows4, 3 * Q_BLOCK), F32),
                        pltpu.VMEM((Q_SUB, rows4, PAST_LEN), F32),
                        pltpu.VMEM((Q_SUB, rows4, 3 * Q_BLOCK), F32),
                        pltpu.VMEM((Q_SUB, rows4, PAST_LEN + 3 * Q_BLOCK), BF16)],
        compiler_params=_params("arbitrary", vmem=VMEM_LIMIT),
        name="lat_attn",
    )(sink, q, k4, v4, kc, vc, z)


def _out_kernel(*refs, n_y, n_x):
    t = PROJ
    y_refs, refs = refs[:n_y], refs[n_y:]
    wf_ref, refs = refs[0], refs[1:]
    x_refs, (gate_ref, o_ref, w_ref) = refs[:n_x], refs[n_x:]
    _cast_weight_once(wf_ref, w_ref)
    o_ref[...] = _read_parts(t, x_refs) + gate_ref[...] * _dot(_read_parts(t, y_refs), w_ref[...])


def _out_proj(y_parts, w_all, j, x_parts, mod):
    t = PROJ
    k = w_all.shape[1]
    return pl.pallas_call(
        functools.partial(_out_kernel, n_y=len(y_parts), n_x=len(x_parts)),
        out_shape=jax.ShapeDtypeStruct((N_TOK, D_MODEL), F32),
        grid=(t.n_tiles,),
        in_specs=(_part_specs(t, y_parts, k) + [_layer_spec(w_all, j)]
                  + _part_specs(t, x_parts, D_MODEL) + [t.mod(2)]),
        out_specs=t.rows(D_MODEL),
        scratch_shapes=[pltpu.VMEM((k, D_MODEL), BF16)],
        compiler_params=_params("arbitrary", vmem=VMEM_LIMIT),
        name="out_proj",
    )(*y_parts, w_all, *x_parts, mod)


def _out_final_kernel(yp_ref, ys_ref, wf_ref, x_ref, gate_ref, fg_ref, op_ref, os_ref, w_ref):
    t = PROJ
    i = pl.program_id(0)
    _cast_weight_once(wf_ref, w_ref)
    y = _read_parts(t, (yp_ref, ys_ref))
    r = _rms(x_ref[...] + gate_ref[...] * _dot(y, w_ref[...]), fg_ref[...])

    @pl.when(i < t.n_prompt)
    def _():
        op_ref[...] = r

    @pl.when(i >= t.n_prompt)
    def _():
        os_ref[...] = r


def _out_proj_final(y_parts, w_all, j, x, mod, final_g):
    t = PROJ
    k = w_all.shape[1]
    return pl.pallas_call(
        _out_final_kernel,
        out_shape=(jax.ShapeDtypeStruct((N_PROMPT_TOK, D_MODEL), F32),
                   jax.ShapeDtypeStruct((N_SAMPLE_TOK, D_MODEL), F32)),
        grid=(t.n_tiles,),
        in_specs=(_part_specs(t, y_parts, k)
                  + [_layer_spec(w_all, j), t.rows(D_MODEL), t.mod(2), _const_spec((1, D_MODEL))]),
        out_specs=(t.prompt_rows(D_MODEL), t.sample_rows(D_MODEL)),
        scratch_shapes=[pltpu.VMEM((k, D_MODEL), BF16)],
        compiler_params=_params("arbitrary", vmem=VMEM_LIMIT),
        name="out_proj_final",
    )(*y_parts, w_all, x, mod, final_g)


def _pool_in_kernel(x_ref, g_ref, sh_ref, sc_ref, wf_ref, u_ref, z_ref, w_ref):
    _cast_weight_once(wf_ref, w_ref)
    h = _norm_mod(x_ref[...], g_ref, sh_ref, sc_ref)

    def store_u(c, a):
        u_ref[:, c * MXU_N:(c + 1) * MXU_N] = a

    def store_z(c, a):
        z_ref[:, c * MXU_N:(c + 1) * MXU_N] = a.astype(BF16)

    _proj_chunks(h, w_ref, 0, D_MODEL, store_u)
    _proj_chunks(h, w_ref, D_MODEL, D_MODEL, store_z)


def _pool_in(x, g, mod, w_all, j):
    t = PROJ
    return pl.pallas_call(
        _pool_in_kernel,
        out_shape=(jax.ShapeDtypeStruct((N_TOK, D_MODEL), F32),
                   jax.ShapeDtypeStruct((N_TOK, D_MODEL), BF16)),
        grid=(t.n_tiles,),
        in_specs=[t.rows(D_MODEL), _const_spec((1, D_MODEL)), t.mod(0), t.mod(1),
                  _layer_spec(w_all, j)],
        out_specs=(t.rows(D_MODEL), t.rows(D_MODEL)),
        scratch_shapes=[pltpu.VMEM((D_MODEL, 2 * D_MODEL), BF16)],
        compiler_params=_params("arbitrary", vmem=VMEM_LIMIT),
        name="pool_in",
    )(x, g, mod, mod, w_all)


def _split_bf16(a):
    hi = a.astype(BF16)
    return hi, (a - hi.astype(F32)).astype(BF16)


def _band_ones(shape, lo, hi):
    d = lax.broadcasted_iota(jnp.int32, shape, 1) - lax.broadcasted_iota(jnp.int32, shape, 0)
    return jnp.logical_and(d >= lo, d <= hi).astype(F32).astype(BF16)


def _pool_out_kernel(u_ref, up_ref, un_ref, z_ref, wgf_ref, ps_ref, wof_ref, x_ref, gate_ref,
                     o_ref, y_ref, wg_ref, wo_ref, band_ref):
    t = POOL
    i = pl.program_id(0)
    _cast_weight_once(wgf_ref, wg_ref)
    _cast_weight_once(wof_ref, wo_ref)

    @pl.when(i == 0)
    def _():
        for g, w in enumerate(POOL_WINDOWS):
            band_ref[g] = _band_ones((t.tm, t.tm), -(w // 2), w - 1 - w // 2)

    is_dec = i >= t.n_prompt
    st = t.seq_tile(i)
    has_left = jnp.logical_and(is_dec, st != 0)
    has_right = jnp.logical_and(is_dec, st != t.per_dec_seq - 1)
    seq_len = jnp.where(is_dec, DEC_SEQ, SEQ)
    h = POOL_HALO
    halo = jnp.concatenate([jnp.where(has_left, up_ref[...], 0.0),
                            jnp.where(has_right, un_ref[...], 0.0)], axis=0)
    halo_hi, halo_lo = _split_bf16(halo)
    pos = st * t.tm + lax.broadcasted_iota(jnp.int32, (t.tm, 1), 0)
    rr = lax.broadcasted_iota(jnp.int32, (2 * h, 2 * h), 0)
    cc = lax.broadcasted_iota(jnp.int32, (2 * h, 2 * h), 1)
    for g, w in enumerate(POOL_WINDOWS):
        left = w // 2
        right = w - 1 - left
        cols = slice(g * POOL_GROUP_DIM, (g + 1) * POOL_GROUP_DIM)
        u = u_ref[:, cols]
        u_hi, u_lo = _split_bf16(u)
        s = _dot(band_ref[g], u_hi) + _dot(band_ref[g], u_lo)
        top = jnp.logical_and(jnp.logical_and(rr < h, cc < h), cc - h >= rr - left)
        bot = jnp.logical_and(jnp.logical_and(rr >= h, cc >= h), cc - h <= rr - 2 * h + right)
        edge = jnp.logical_or(top, bot).astype(F32).astype(BF16)
        se = _dot(edge, halo_hi[:, cols]) + _dot(edge, halo_lo[:, cols])
        s = jnp.concatenate([s[:h] + se[:h], s[h:t.tm - h], s[t.tm - h:] + se[h:]], axis=0)
        cnt = (jnp.minimum(pos + right + 1, seq_len) - jnp.maximum(pos - left, 0)).astype(F32)
        d = (s / cnt - u).astype(BF16)
        yg = _dot(d, wg_ref[g]) * ps_ref[:, cols] * _silu(z_ref[:, cols].astype(F32))
        y_ref[:, cols] = yg.astype(BF16)
    o_ref[...] = x_ref[...] + gate_ref[...] * _dot(y_ref[...], wo_ref[...])


def _pool_out(u, z, wg_all, ps, wo_all, j, x, mod):
    t = POOL
    per = t.tm // POOL_HALO
    n_halo = N_TOK // POOL_HALO
    return pl.pallas_call(
        _pool_out_kernel,
        out_shape=jax.ShapeDtypeStruct((N_TOK, D_MODEL), F32),
        grid=(t.n_tiles,),
        in_specs=[
            t.rows(D_MODEL),
            pl.BlockSpec((POOL_HALO, D_MODEL), lambda i: (jnp.maximum(i * per - 1, 0), 0)),
            pl.BlockSpec((POOL_HALO, D_MODEL), lambda i: (jnp.minimum((i + 1) * per, n_halo - 1), 0)),
            t.rows(D_MODEL),
            _layer_spec(wg_all, j),
            _const_spec((1, D_MODEL)),
            _layer_spec(wo_all, j),
            t.rows(D_MODEL),
            t.mod(2),
        ],
        out_specs=t.rows(D_MODEL),
        scratch_shapes=[pltpu.VMEM((t.tm, D_MODEL), BF16),
                        pltpu.VMEM((len(POOL_WINDOWS), POOL_GROUP_DIM, POOL_GROUP_DIM), BF16),
                        pltpu.VMEM((D_MODEL, D_MODEL), BF16),
                        pltpu.VMEM((len(POOL_WINDOWS), t.tm, t.tm), BF16)],
        compiler_params=_params("arbitrary", vmem=VMEM_LIMIT),
        name="pool_out",
    )(u, u, u, z, wg_all, ps, wo_all, x, mod)


def _ret_in_kernel(x_ref, g_ref, sh_ref, sc_ref, w_ref, q_ref, kt_ref, v_ref, z_ref, wkt_ref):
    t = PROJ

    @pl.when(pl.program_id(0) == 0)
    def _():
        for r in range(0, RET_QK_WIDTH, MXU_N):
            wk = w_ref[:, RET_QK_WIDTH + r:RET_QK_WIDTH + r + MXU_N]
            wkt_ref[r:r + MXU_N, :] = wk.astype(F32).T.astype(BF16)

    h = _norm_mod(x_ref[...], g_ref, sh_ref, sc_ref)

    def store(ref):
        def f(c, a):
            ref[:, c * MXU_N:(c + 1) * MXU_N] = a.astype(BF16)
        return f

    _proj_chunks(h, w_ref, 0, RET_QK_WIDTH, store(q_ref))
    kt = (_dot_nt(wkt_ref[...], h) * RET_KEY_DIM ** -0.5).astype(BF16)
    for c in range(t.tm // RET_CHUNK):
        kt_ref[c] = kt[:, c * RET_CHUNK:(c + 1) * RET_CHUNK]
    _proj_chunks(h, w_ref, 2 * RET_QK_WIDTH, RET_V_WIDTH, store(v_ref))
    _proj_chunks(h, w_ref, 2 * RET_QK_WIDTH + RET_V_WIDTH, RET_V_WIDTH, store(z_ref))


def _ret_in(x, g, mod, w):
    t = PROJ
    per = t.tm // RET_CHUNK
    n_in = 2 * RET_QK_WIDTH + 2 * RET_V_WIDTH
    return pl.pallas_call(
        _ret_in_kernel,
        out_shape=(jax.ShapeDtypeStruct((N_TOK, RET_QK_WIDTH), BF16),
                   jax.ShapeDtypeStruct((N_TOK // RET_CHUNK, RET_QK_WIDTH, RET_CHUNK), BF16),
                   jax.ShapeDtypeStruct((N_TOK, RET_V_WIDTH), BF16),
                   jax.ShapeDtypeStruct((N_TOK, RET_V_WIDTH), BF16)),
        grid=(t.n_tiles,),
        in_specs=[t.rows(D_MODEL), _const_spec((1, D_MODEL)), t.mod(0), t.mod(1),
                  _const_spec((D_MODEL, n_in))],
        out_specs=(t.rows(RET_QK_WIDTH),
                   pl.BlockSpec((per, RET_QK_WIDTH, RET_CHUNK), lambda i: (i, 0, 0)),
                   t.rows(RET_V_WIDTH), t.rows(RET_V_WIDTH)),
        scratch_shapes=[pltpu.VMEM((RET_QK_WIDTH, D_MODEL), BF16)],
        compiler_params=_params("arbitrary", vmem=VMEM_LIMIT),
        name="ret_in",
    )(x, g, mod, mod, w)


def _pos(shape, axis):
    return lax.broadcasted_iota(jnp.int32, shape, axis).astype(F32)


def _ret_tables_kernel(lgf_ref, lgb_ref, decay_ref, row_ref, col_ref, cdec_ref):
    h = pl.program_id(0)
    lg_f = lgf_ref[h]
    lg_b = lgb_ref[h]
    c = RET_CHUNK
    diff = _pos((c, c), 0) - _pos((c, c), 1)
    fwd = jnp.where(diff >= 0, jnp.exp(jnp.maximum(diff, 0.0) * lg_f), 0.0)
    bwd = jnp.where(diff <= 0, jnp.exp(jnp.maximum(-diff, 0.0) * lg_b), 0.0)
    decay_ref[...] = fwd + bwd
    j = _pos((RET_TAB_ROWS, c), 1)
    row_ref[0] = jnp.exp((c - 1.0 - j) * lg_f)
    row_ref[1] = jnp.exp(j * lg_b)
    i = _pos((c, LANES), 0)
    col_ref[0] = jnp.exp((i + 1.0) * lg_f)
    col_ref[1] = jnp.exp((c - i) * lg_b)
    full = jnp.full((RET_TAB_ROWS, RET_VAL_DIM), float(c), F32)
    cdec_ref[0] = jnp.exp(full * lg_f)
    cdec_ref[1] = jnp.exp(full * lg_b)


def _ret_tables(lg_f, lg_b):
    smem = pl.BlockSpec(memory_space=pltpu.SMEM)
    c = RET_CHUNK
    shapes = ((c, c), (2, RET_TAB_ROWS, c), (2, c, LANES), (2, RET_TAB_ROWS, RET_VAL_DIM))
    return pl.pallas_call(
        _ret_tables_kernel,
        out_shape=tuple(jax.ShapeDtypeStruct((RET_HEADS,) + s, F32) for s in shapes),
        grid=(RET_HEADS,),
        in_specs=[smem, smem],
        out_specs=tuple(pl.BlockSpec((None,) + s, lambda h, n=len(s): (h,) + (0,) * n) for s in shapes),
        compiler_params=_params("arbitrary"),
        name="ret_tables",
    )(lg_f, lg_b)


def _group_norm_gate(o, gn, z):
    mu = jnp.mean(o, axis=-1, keepdims=True)
    var = jnp.mean(jnp.square(o - mu), axis=-1, keepdims=True)
    on = (o - mu) * lax.rsqrt(var + EPS)
    return (on * gn * _silu(z.astype(F32))).astype(BF16)


def _ret_ctx_kernel(q_ref, kt_ref, v_ref, z_ref, gn_ref, decay_ref, row_ref, y_ref, sf_ref, sb_ref):
    for h in range(RET_HEADS):
        kc = slice(h * RET_KEY_DIM, (h + 1) * RET_KEY_DIM)
        vc = slice(h * RET_VAL_DIM, (h + 1) * RET_VAL_DIM)
        kt = kt_ref[kc, :]
        v = v_ref[:, vc]
        att = (_dot(q_ref[:, kc], kt) * decay_ref[h]).astype(BF16)
        y_ref[:, vc] = _group_norm_gate(_dot(att, v), gn_ref[:, vc], z_ref[:, vc])
        ktf = kt.astype(F32)
        sf_ref[h] = _dot((ktf * row_ref[h, 0, 0:1, :]).astype(BF16), v)
        sb_ref[h] = _dot((ktf * row_ref[h, 1, 0:1, :]).astype(BF16), v)


def _ret_ctx(q, kt, v, z, gn, decay, row):
    c = RET_CHUNK
    wide = lambda w: pl.BlockSpec((SEQ, w), lambda b: (b, 0))
    st_spec = pl.BlockSpec((None, RET_HEADS, RET_KEY_DIM, RET_VAL_DIM), lambda b: (b, 0, 0, 0))
    st_shape = jax.ShapeDtypeStruct((BATCH, RET_HEADS, RET_KEY_DIM, RET_VAL_DIM), F32)
    return pl.pallas_call(
        _ret_ctx_kernel,
        out_shape=(jax.ShapeDtypeStruct((N_PROMPT_TOK, RET_V_WIDTH), BF16), st_shape, st_shape),
        grid=(BATCH,),
        in_specs=[wide(RET_QK_WIDTH), pl.BlockSpec((None, RET_QK_WIDTH, c), lambda b: (b, 0, 0)),
                  wide(RET_V_WIDTH), wide(RET_V_WIDTH), _const_spec((1, RET_V_WIDTH)),
                  _const_spec((RET_HEADS, c, c)), _const_spec((RET_HEADS, 2, RET_TAB_ROWS, c))],
        out_specs=(wide(RET_V_WIDTH), st_spec, st_spec),
        compiler_params=_params("arbitrary", vmem=VMEM_LIMIT),
        name="ret_ctx",
    )(q, kt, v, z, gn, decay, row)


def _ret_lat_kernel(q_ref, kt_ref, v_ref, z_ref, gn_ref, decay_ref, row_ref, col_ref, cdec_ref,
                    s0f_ref, s0b_ref, y_ref, sf_all, sb_all, sf_acc, sb_acc):
    c = RET_CHUNK
    n_chunks = DEC_SEQ // c
    rows_of = lambda ci: pl.ds(pl.multiple_of(ci * c, c), c)

    sf_acc[...] = s0f_ref[...]
    sb_acc[...] = s0b_ref[...]

    def scan_step(i, carry):
        cf = i
        cb = n_chunks - 1 - i
        sf_all[cf] = sf_acc[...].astype(BF16)
        sb_all[cb] = sb_acc[...].astype(BF16)
        uf = _dot((kt_ref[cf].astype(F32) * row_ref[0, 0:1, :]).astype(BF16), v_ref[rows_of(cf), :])
        ub = _dot((kt_ref[cb].astype(F32) * row_ref[1, 0:1, :]).astype(BF16), v_ref[rows_of(cb), :])
        sf_acc[...] = sf_acc[...] * cdec_ref[0, 0:1, :] + uf
        sb_acc[...] = sb_acc[...] * cdec_ref[1, 0:1, :] + ub
        return carry

    lax.fori_loop(0, n_chunks, scan_step, 0)

    def out_step(ci, carry):
        rows = rows_of(ci)
        q = q_ref[rows, :]
        qf = q.astype(F32)
        qdec_f = jnp.concatenate([col_ref[0]] * (RET_KEY_DIM // LANES), axis=1)
        qdec_b = jnp.concatenate([col_ref[1]] * (RET_KEY_DIM // LANES), axis=1)
        att = (_dot(q, kt_ref[ci]) * decay_ref[...]).astype(BF16)
        o = (_dot(att, v_ref[rows, :])
             + _dot((qf * qdec_f).astype(BF16), sf_all[ci])
             + _dot((qf * qdec_b).astype(BF16), sb_all[ci]))
        y_ref[rows, :] = _group_norm_gate(o, gn_ref[...], z_ref[rows, :])
        return carry

    lax.fori_loop(0, n_chunks, out_step, 0, unroll=2)


def _ret_lat(q, kt, v, z, gn, decay, row, col, cdec, s0f, s0b):
    c = RET_CHUNK
    n_chunks = DEC_SEQ // c
    row0 = N_PROMPT_TOK // DEC_SEQ
    qk_spec = pl.BlockSpec((DEC_SEQ, RET_KEY_DIM), lambda b, h: (row0 + b, h))
    v_spec = pl.BlockSpec((DEC_SEQ, RET_VAL_DIM), lambda b, h: (row0 + b, h))
    st_spec = pl.BlockSpec((None, None, RET_KEY_DIM, RET_VAL_DIM), lambda b, h: (b, h, 0, 0))
    tab = lambda *s: pl.BlockSpec((None,) + s, lambda b, h: (h,) + (0,) * len(s))
    states = pltpu.VMEM((n_chunks, RET_KEY_DIM, RET_VAL_DIM), BF16)
    acc = pltpu.VMEM((RET_KEY_DIM, RET_VAL_DIM), F32)
    return pl.pallas_call(
        _ret_lat_kernel,
        out_shape=jax.ShapeDtypeStruct((N_SAMPLE_TOK, RET_V_WIDTH), BF16),
        grid=(DEC_BATCH, RET_HEADS),
        in_specs=[qk_spec,
                  pl.BlockSpec((n_chunks, RET_KEY_DIM, c), lambda b, h: (row0 + b, h, 0)),
                  v_spec, v_spec, pl.BlockSpec((1, RET_VAL_DIM), lambda b, h: (0, h)),
                  tab(c, c), tab(2, RET_TAB_ROWS, c), tab(2, c, LANES), tab(2, RET_TAB_ROWS, RET_VAL_DIM),
                  st_spec, st_spec],
        out_specs=pl.BlockSpec((DEC_SEQ, RET_VAL_DIM), lambda b, h: (b, h)),
        scratch_shapes=[states, states, acc, acc],
        compiler_params=_params("arbitrary", "arbitrary", vmem=VMEM_LIMIT),
        name="ret_lat",
    )(q, kt, v, z, gn, decay, row, col, cdec, s0f, s0b)


def _rope_tables(tm):
    n_rows = DEC_SEQ // GRID_W
    rows = jnp.repeat(jnp.arange(n_rows), GRID_W).astype(F32)
    cols = jnp.tile(jnp.arange(GRID_W), n_rows).astype(F32)
    half = ATTN_HEAD_DIM // 4
    inv = ROPE_BASE ** (-jnp.arange(half, dtype=F32) / half)
    ang_r = rows[:, None] * inv[None, :]
    ang_c = cols[:, None] * inv[None, :]
    cos = jnp.concatenate([jnp.cos(ang_r), jnp.cos(ang_r), jnp.cos(ang_c), jnp.cos(ang_c)], axis=-1)
    sin = jnp.concatenate([-jnp.sin(ang_r), jnp.sin(ang_r), -jnp.sin(ang_c), jnp.sin(ang_c)], axis=-1)
    cos = jnp.concatenate([jnp.ones((tm, ATTN_HEAD_DIM), F32), cos], axis=0)
    sin = jnp.concatenate([jnp.zeros((tm, ATTN_HEAD_DIM), F32), sin], axis=0)
    return jnp.tile(cos, (1, 2)), jnp.tile(sin, (1, 2))


def kernel(x_prompt, x_sample, cache_k, cache_v, state_fwd, state_bwd, c, c_ctx, norm_g, ada_w, ada_b, attn_w_in, attn_w_out, attn_sink, pool_w_in, pool_w_grp, pool_scale, pool_w_out, ret_w_in, ret_decay_fwd, ret_decay_bwd, ret_gn_g, ret_w_out, final_g):
    x_parts = (x_prompt.reshape(N_PROMPT_TOK, D_MODEL), x_sample.reshape(N_SAMPLE_TOK, D_MODEL))
    cond = jnp.concatenate([c_ctx[None, :], c,
                            jnp.zeros((N_COND - 1 - DEC_BATCH, D_MODEL), F32)], axis=0)
    mods = _ada_table(cond.T, ada_w, ada_b).reshape(DEPTH, N_COND, 1, 3 * D_MODEL)
    cos_t, sin_t = _rope_tables(PROJ.tm)

    to_kernel = lambda a: jnp.transpose(a, (0, 1, 3, 4, 2))
    from_kernel = lambda a: jnp.transpose(
        a.reshape(a.shape[0], a.shape[1], ATTN_KV_HEADS, ATTN_HEAD_DIM, a.shape[3]), (0, 1, 4, 2, 3))
    ctx_k, ctx_v = to_kernel(cache_k), to_kernel(cache_v)

    caches = ()
    new_sf = new_sb = None
    for i in range(DEPTH):
        kind, j = i % N_MIXERS, i // N_MIXERS
        g = norm_g[i].reshape(1, D_MODEL)
        mod = mods[i]
        last = i == DEPTH - 1
        if kind == 0:
            q, k4, v4, z, *caches = _attn_in(x_parts, g, mod, attn_w_in, j, cos_t, sin_t, tuple(caches))
            y_parts = (_ctx_attn(attn_sink[j], q, k4, v4, z),
                       _lat_attn(attn_sink[j], q, k4, v4, ctx_k, ctx_v, j, z))
            w_out = attn_w_out
        elif kind == 1:
            (x,) = x_parts
            u, z = _pool_in(x, g, mod, pool_w_in, j)
            assert not last
            x_parts = (_pool_out(u, z, pool_w_grp, pool_scale[j].reshape(1, D_MODEL), pool_w_out, j,
                                 x, mod),)
            continue
        else:
            (x,) = x_parts
            lg_f = jax.nn.log_sigmoid(ret_decay_fwd[j].astype(F32))
            lg_b = jax.nn.log_sigmoid(ret_decay_bwd[j].astype(F32))
            gn = ret_gn_g[j].reshape(1, RET_V_WIDTH)
            q, kt, v, z = _ret_in(x, g, mod, ret_w_in[j].astype(BF16))
            decay, row, col, cdec = _ret_tables(lg_f, lg_b)
            y_ctx, new_sf, new_sb = _ret_ctx(q, kt, v, z, gn, decay, row)
            y_parts = (y_ctx, _ret_lat(q, kt, v, z, gn, decay, row, col, cdec,
                                       state_fwd[:, j], state_bwd[:, j]))
            w_out = ret_w_out
        if last:
            (x,) = x_parts
            y_prompt, y_sample = _out_proj_final(y_parts, w_out, j, x, mod, final_g.reshape(1, D_MODEL))
        else:
            x_parts = (_out_proj(y_parts, w_out, j, x_parts, mod),)
    new_k, new_v = caches
    return (y_prompt.reshape(BATCH, SEQ, D_MODEL), y_sample.reshape(DEC_BATCH, DEC_SEQ, D_MODEL),
            from_kernel(new_k), from_kernel(new_v), new_sf[:, None], new_sb[:, None])
```

```python
import functools
from typing import NamedTuple

import jax
import jax.numpy as jnp
from jax import lax
from jax.experimental import pallas as pl
from jax.experimental.pallas import tpu as pltpu

F32 = jnp.float32
BF16 = jnp.bfloat16

D_MODEL = 1024
BATCH = 16
SEQ = 256
DEPTH = 4
DEC_BATCH = 2
DEC_SEQ = 2048
PAST_LEN = 512
GRID_W = 64
N_MIXERS = 3
ATTN_HEADS = 16
ATTN_KV_HEADS = 4
ATTN_HEAD_DIM = 64
ATTN_GROUP = 4
ATTN_WIDTH = 1024
ATTN_KV_WIDTH = 256
WINDOW = 128
ROPE_BASE = 10000.0
POOL_WINDOWS = (2, 4, 8, 16)
POOL_GROUP_DIM = 256
RET_HEADS = 4
RET_KEY_DIM = 256
RET_VAL_DIM = 512
RET_QK_WIDTH = 1024
RET_V_WIDTH = 2048
EPS = 1e-6
NEG_INF = -1e30

N_PROMPT_TOK = BATCH * SEQ
N_SAMPLE_TOK = DEC_BATCH * DEC_SEQ
N_TOK = N_PROMPT_TOK + N_SAMPLE_TOK
N_COND = 8
LANES = 128
MXU_N = 256
Q_BLOCK = 128
Q_SUB = 4
RET_CHUNK = 256
RET_TAB_ROWS = 8
POOL_HALO = 8
VMEM_LIMIT = 48 * 1024 * 1024


class Tiling(NamedTuple):
    tm: int

    @property
    def n_tiles(self):
        return N_TOK // self.tm

    @property
    def n_prompt(self):
        return N_PROMPT_TOK // self.tm

    @property
    def per_dec_seq(self):
        return DEC_SEQ // self.tm

    def cond(self, i):
        return jnp.where(i < self.n_prompt, 0, 1 + (i - self.n_prompt) // self.per_dec_seq)

    def seq_tile(self, i):
        return jnp.where(i < self.n_prompt, 0, (i - self.n_prompt) % self.per_dec_seq)

    def rows(self, width):
        return pl.BlockSpec((self.tm, width), lambda i: (i, 0))

    def prompt_rows(self, width):
        return pl.BlockSpec((self.tm, width), lambda i: (jnp.minimum(i, self.n_prompt - 1), 0))

    def sample_rows(self, width):
        return pl.BlockSpec((self.tm, width), lambda i: (jnp.maximum(i - self.n_prompt, 0), 0))

    def mod(self, part):
        return pl.BlockSpec((None, 1, D_MODEL), lambda i: (self.cond(i), 0, part))


PROJ = Tiling(512)
POOL = Tiling(SEQ)


def _silu(z):
    return z * (1.0 / (1.0 + jnp.exp(-z)))


def _dot(a, b):
    return jnp.dot(a, b, preferred_element_type=F32)


def _dot_nt(a, b):
    return lax.dot_general(a, b, (((1,), (1,)), ((), ())), preferred_element_type=F32)


def _params(*sem, vmem=None):
    return pltpu.CompilerParams(dimension_semantics=sem, vmem_limit_bytes=vmem)


def _const_spec(shape):
    nd = len(shape)
    return pl.BlockSpec(shape, lambda *_: (0,) * nd, pipeline_mode=pl.Buffered(1))


def _part_specs(t, parts, width):
    if len(parts) == 1:
        return [t.rows(width)]
    return [t.prompt_rows(width), t.sample_rows(width)]


def _read_parts(t, refs):
    if len(refs) == 1:
        return refs[0][...]
    return jnp.where(pl.program_id(0) < t.n_prompt, refs[0][...], refs[1][...])


def _ada_kernel(cond_ref, w_ref, b_ref, o_ref):
    s = _silu(cond_ref[...])
    w = w_ref[...]
    rows = [jnp.sum(s[:, c:c + 1] * w, axis=0, keepdims=True) + b_ref[...] for c in range(1 + DEC_BATCH)]
    rows.append(jnp.zeros((N_COND - len(rows), w.shape[1]), F32))
    o_ref[...] = jnp.concatenate(rows, axis=0)


def _ada_table(cond, ada_w, ada_b):
    tn = 768
    return pl.pallas_call(
        _ada_kernel,
        out_shape=jax.ShapeDtypeStruct((DEPTH, N_COND, 3 * D_MODEL), F32),
        grid=(DEPTH, 3 * D_MODEL // tn),
        in_specs=[
            pl.BlockSpec((D_MODEL, N_COND), lambda l, n: (0, 0)),
            pl.BlockSpec((None, D_MODEL, tn), lambda l, n: (l, 0, n)),
            pl.BlockSpec((None, 1, tn), lambda l, n: (l, 0, n)),
        ],
        out_specs=pl.BlockSpec((None, N_COND, tn), lambda l, n: (l, 0, n)),
        compiler_params=_params("arbitrary", "arbitrary"),
        name="ada_table",
    )(cond, ada_w, ada_b.reshape(DEPTH, 1, 3 * D_MODEL))


def _rms(x, g):
    return x * lax.rsqrt(jnp.mean(x * x, axis=-1, keepdims=True) + EPS) * g


def _norm_mod(x, g_ref, sh_ref, sc_ref):
    return (_rms(x, g_ref[...]) * (1.0 + sc_ref[...]) + sh_ref[...]).astype(BF16)


def _proj_chunks(h, w_ref, lo, width, store):
    for c in range(width // MXU_N):
        store(c, _dot(h, w_ref[:, lo + c * MXU_N:lo + (c + 1) * MXU_N]))


def _layer_spec(w, j):
    nd = w.ndim - 1
    return pl.BlockSpec((None,) + w.shape[1:], lambda *_: (j,) + (0,) * nd, pipeline_mode=pl.Buffered(1))


def _cast_weight_once(w_ref, wb_ref):
    @pl.when(pl.program_id(0) == 0)
    def _():
        rows = wb_ref.shape[-2]
        for r in range(0, rows, MXU_N):
            wb_ref[..., r:r + MXU_N, :] = w_ref[..., r:r + MXU_N, :].astype(BF16)


def _rep4(a, h):
    half = a[:, (h // 2) * LANES:(h // 2 + 1) * LANES]
    lane = lax.broadcasted_iota(jnp.int32, half.shape, 1)
    keep = (lane < ATTN_HEAD_DIM) if h % 2 == 0 else (lane >= ATTN_HEAD_DIM)
    m = jnp.where(keep, half, 0.0)
    s = m + pltpu.roll(m, ATTN_HEAD_DIM, 1)
    return jnp.concatenate([s, s], axis=1)


def _attn_in_kernel(*refs, n_x, n_alias):
    t = PROJ
    x_refs, refs = refs[:n_x], refs[n_x:]
    (g_ref, sh_ref, sc_ref, wf_ref, cos_ref, sin_ref), refs = refs[:6], refs[6 + n_alias:]
    q_ref, k4_ref, v4_ref, z_ref, kc_ref, vc_ref, w_ref = refs
    i = pl.program_id(0)
    _cast_weight_once(wf_ref, w_ref)
    h = _norm_mod(_read_parts(t, x_refs), g_ref, sh_ref, sc_ref)
    cos = cos_ref[...]
    sin = sin_ref[...]
    lane = lax.broadcasted_iota(jnp.int32, (t.tm, LANES), 1)
    first = (lane % (ATTN_HEAD_DIM // 2)) < ATTN_HEAD_DIM // 4

    def rope(a):
        rot = jnp.where(first, pltpu.roll(a, LANES - ATTN_HEAD_DIM // 4, 1),
                        pltpu.roll(a, ATTN_HEAD_DIM // 4, 1))
        return a * cos + rot * sin

    def rope_wide(a):
        return jnp.concatenate(
            [rope(a[:, s * LANES:(s + 1) * LANES]) for s in range(MXU_N // LANES)], axis=1)

    scale = ATTN_HEAD_DIM ** -0.5

    def store_q(c, a):
        q_ref[:, c * MXU_N:(c + 1) * MXU_N] = (rope_wide(a) * scale).astype(BF16)

    def store_z(c, a):
        z_ref[:, c * MXU_N:(c + 1) * MXU_N] = a.astype(BF16)

    _proj_chunks(h, w_ref, 0, ATTN_WIDTH, store_q)
    k = rope_wide(_dot(h, w_ref[:, ATTN_WIDTH:ATTN_WIDTH + ATTN_KV_WIDTH]))
    v = _dot(h, w_ref[:, ATTN_WIDTH + ATTN_KV_WIDTH:ATTN_WIDTH + 2 * ATTN_KV_WIDTH])
    for hh in range(ATTN_KV_HEADS):
        k4_ref[:, hh * MXU_N:(hh + 1) * MXU_N] = _rep4(k, hh).astype(BF16)
        v4_ref[:, hh * MXU_N:(hh + 1) * MXU_N] = _rep4(v, hh).astype(BF16)
    _proj_chunks(h, w_ref, ATTN_WIDTH + 2 * ATTN_KV_WIDTH, ATTN_WIDTH, store_z)

    @pl.when(i < t.n_prompt)
    def _():
        for s in range(t.tm // SEQ):
            kc_ref[s] = k[s * SEQ:(s + 1) * SEQ, :].T
            vc_ref[s] = v[s * SEQ:(s + 1) * SEQ, :].T


def _attn_in(x_parts, g, mod, w_all, j, cos_t, sin_t, caches):
    t = PROJ
    n_in = 2 * ATTN_WIDTH + 2 * ATTN_KV_WIDTH
    per = t.tm // SEQ
    rope_spec = pl.BlockSpec(
        (t.tm, LANES), lambda i: (jnp.where(i < t.n_prompt, 0, 1 + t.seq_tile(i)), 0))
    wide = jax.ShapeDtypeStruct((N_TOK, ATTN_WIDTH), BF16)
    n_attn = w_all.shape[0]
    cache = jax.ShapeDtypeStruct((BATCH, n_attn, ATTN_KV_WIDTH, SEQ), F32)
    cache_spec = pl.BlockSpec((per, None, ATTN_KV_WIDTH, SEQ),
                              lambda i: (jnp.minimum(i, t.n_prompt - 1), j, 0, 0))
    n_x = len(x_parts)
    return pl.pallas_call(
        functools.partial(_attn_in_kernel, n_x=n_x, n_alias=len(caches)),
        out_shape=(wide, wide, wide, wide, cache, cache),
        grid=(t.n_tiles,),
        in_specs=_part_specs(t, x_parts, D_MODEL) + [
            _const_spec((1, D_MODEL)), t.mod(0), t.mod(1), _layer_spec(w_all, j),
            rope_spec, rope_spec] + [pl.BlockSpec(memory_space=pl.ANY)] * len(caches),
        out_specs=(t.rows(ATTN_WIDTH),) * 4 + (cache_spec,) * 2,
        scratch_shapes=[pltpu.VMEM((D_MODEL, n_in), BF16)],
        input_output_aliases={n_x + 6 + c: 4 + c for c in range(len(caches))},
        compiler_params=_params("arbitrary", vmem=VMEM_LIMIT),
        name="attn_in",
    )(*x_parts, g, mod, mod, w_all, cos_t, sin_t, *caches)


def _stack_group_queries(q):
    qf = q.astype(F32)
    chunk = lax.broadcasted_iota(jnp.int32, qf.shape, 1) // ATTN_HEAD_DIM
    return jnp.concatenate(
        [jnp.where(chunk == g, qf, 0.0) for g in range(ATTN_GROUP)], axis=0).astype(BF16)


def _gather_group_outputs(o, rows):
    chunk = lax.broadcasted_iota(jnp.int32, (rows, MXU_N), 1) // ATTN_HEAD_DIM
    acc = jnp.zeros((rows, MXU_N), F32)
    for g in range(ATTN_GROUP):
        acc = acc + jnp.where(chunk == g, o[g * rows:(g + 1) * rows], 0.0)
    return acc


def _sink_column(sink_ref, h, rows):
    grp = lax.broadcasted_iota(jnp.int32, (ATTN_GROUP * rows, 1), 0) // rows
    col = jnp.zeros((ATTN_GROUP * rows, 1), F32)
    for g in range(ATTN_GROUP):
        col = jnp.where(grp == g, sink_ref[h * ATTN_GROUP + g], col)
    return col


def _chunk_rows(dtype):
    chunk = lax.broadcasted_iota(jnp.int32, (1, MXU_N), 1) // ATTN_HEAD_DIM
    return [(chunk == g).astype(F32).astype(dtype) for g in range(ATTN_GROUP)]


def _block_diag_rows(x4):
    return jnp.concatenate([x4 * m for m in _chunk_rows(x4.dtype)], axis=0)


def _ctx_attn_kernel(sink_ref, q_ref, k4_ref, v4_ref, z_ref, y_ref):
    chunk = lax.broadcasted_iota(jnp.int32, (SEQ, MXU_N), 1) // ATTN_HEAD_DIM
    for h in range(ATTN_KV_HEADS):
        cols = slice(h * MXU_N, (h + 1) * MXU_N)
        s = _dot_nt(q_ref[:, cols], _block_diag_rows(k4_ref[:, cols]))
        inv = jnp.zeros((SEQ, MXU_N), F32)
        probs = []
        for g in range(ATTN_GROUP):
            sg = s[:, g * SEQ:(g + 1) * SEQ]
            sk = sink_ref[h * ATTN_GROUP + g]
            m = jnp.maximum(jnp.max(sg, axis=1, keepdims=True), sk)
            e = jnp.exp(sg - m)
            den = jnp.sum(e, axis=1, keepdims=True) + jnp.exp(sk - m)
            probs.append(e.astype(BF16))
            inv = jnp.where(chunk == g, 1.0 / den, inv)
        o = _dot(jnp.concatenate(probs, axis=1), _block_diag_rows(v4_ref[:, cols]))
        y_ref[:, cols] = (o * inv * _silu(z_ref[:, cols].astype(F32))).astype(BF16)


def _ctx_attn(sink, q, k4, v4, z):
    spec = pl.BlockSpec((SEQ, ATTN_WIDTH), lambda b: (b, 0))
    return pl.pallas_call(
        _ctx_attn_kernel,
        out_shape=jax.ShapeDtypeStruct((N_PROMPT_TOK, ATTN_WIDTH), BF16),
        grid=(BATCH,),
        in_specs=[pl.BlockSpec(memory_space=pltpu.SMEM), spec, spec, spec, spec],
        out_specs=spec,
        compiler_params=_params("arbitrary"),
        name="ctx_attn",
    )(sink, q, k4, v4, z)


LAT_STEP = Q_SUB * Q_BLOCK
LAT_PER_SEQ = DEC_SEQ // LAT_STEP
LAT_BLOCKS = DEC_BATCH * ATTN_KV_HEADS * LAT_PER_SEQ


def _lat_block(blk):
    return (blk // (ATTN_KV_HEADS * LAT_PER_SEQ), (blk // LAT_PER_SEQ) % ATTN_KV_HEADS, blk % LAT_PER_SEQ)


def _band_rows(qb):
    return pl.ds(pl.multiple_of(qb * Q_BLOCK, Q_BLOCK), 3 * Q_BLOCK)


def _pad_sequence(dst, src):
    zeros = jnp.zeros((Q_BLOCK, MXU_N), BF16)
    dst[0:Q_BLOCK, :] = zeros
    dst[Q_BLOCK:Q_BLOCK + DEC_SEQ, :] = src[...]
    dst[Q_BLOCK + DEC_SEQ:, :] = zeros


def _lat_attn_kernel(sink_ref, q_ref, k4_ref, v4_ref, kc_ref, vc_ref, z_ref, y_ref,
                     kp, vp, k4c, v4c, sc_a, sl_a, sc_b, sl_b, e_scr):
    t = pl.program_id(0)
    _, _, n1 = _lat_block(jnp.minimum(t, LAT_BLOCKS - 1))
    _, h0, n0 = _lat_block(jnp.maximum(t - 1, 0))
    n_blocks = DEC_SEQ // Q_BLOCK

    @pl.when(t == 0)
    def _():
        sc_b[...] = jnp.zeros(sc_b.shape, F32)
        sl_b[...] = jnp.zeros(sl_b.shape, F32)

    @pl.when(n1 == 0)
    def _():
        _pad_sequence(kp, k4_ref)
        k4c[...] = jnp.concatenate([kc_ref[...]] * ATTN_GROUP, axis=0).astype(BF16)

    @pl.when(n0 == 0)
    def _():
        _pad_sequence(vp, v4_ref)
        v4c[...] = jnp.concatenate([vc_ref[...]] * ATTN_GROUP, axis=0).T.astype(BF16)

    @pl.when(t % 2 == 0)
    def _():
        _lat_stages(sink_ref, q_ref, z_ref, y_ref, kp, vp, k4c, v4c, e_scr, n1, h0, n0,
                    sc_a, sl_a, sc_b, sl_b)

    @pl.when(t % 2 == 1)
    def _():
        _lat_stages(sink_ref, q_ref, z_ref, y_ref, kp, vp, k4c, v4c, e_scr, n1, h0, n0,
                    sc_b, sl_b, sc_a, sl_a)


def _lat_stages(sink_ref, q_ref, z_ref, y_ref, kp, vp, k4c, v4c, e_scr, n1, h0, n0,
                sc_w, sl_w, sc_r, sl_r):
    n_blocks = DEC_SEQ // Q_BLOCK
    for sub in range(Q_SUB):
        qs = _stack_group_queries(q_ref[sub * Q_BLOCK:(sub + 1) * Q_BLOCK, :])
        sc_w[sub] = _dot(qs, k4c[...])
        sl_w[sub] = _dot_nt(qs, kp[_band_rows(n1 * Q_SUB + sub), :])

    r = lax.broadcasted_iota(jnp.int32, (Q_BLOCK, Q_BLOCK), 0)
    c = lax.broadcasted_iota(jnp.int32, (Q_BLOCK, Q_BLOCK), 1)
    in_left = c >= r
    in_right = c <= r
    chunk = lax.broadcasted_iota(jnp.int32, (Q_BLOCK, MXU_N), 1) // ATTN_HEAD_DIM
    n_ctx = PAST_LEN // LANES
    for sub in range(Q_SUB):
        qb = n0 * Q_SUB + sub
        band = _band_rows(qb)
        qrows = slice(sub * Q_BLOCK, (sub + 1) * Q_BLOCK)
        ok_left = jnp.logical_and(in_left, qb > 0)
        ok_right = jnp.logical_and(in_right, qb < n_blocks - 1)
        inv = jnp.zeros((Q_BLOCK, MXU_N), F32)
        for g in range(ATTN_GROUP):
            grows = slice(g * Q_BLOCK, (g + 1) * Q_BLOCK)
            slabs = [sc_r[sub, grows, k * LANES:(k + 1) * LANES] for k in range(n_ctx)]
            slabs.append(jnp.where(ok_left, sl_r[sub, grows, 0:Q_BLOCK], NEG_INF))
            slabs.append(sl_r[sub, grows, Q_BLOCK:2 * Q_BLOCK])
            slabs.append(jnp.where(ok_right, sl_r[sub, grows, 2 * Q_BLOCK:3 * Q_BLOCK], NEG_INF))
            sk = sink_ref[h0 * ATTN_GROUP + g]
            top = slabs[0]
            for s in slabs[1:]:
                top = jnp.maximum(top, s)
            m = jnp.maximum(jnp.max(top, axis=1, keepdims=True), sk)
            part = jnp.zeros((Q_BLOCK, LANES), F32)
            for k, s in enumerate(slabs):
                e = jnp.exp(s - m)
                part = part + e
                e_scr[sub, grows, k * LANES:(k + 1) * LANES] = e.astype(BF16)
            den = jnp.sum(part, axis=1, keepdims=True) + jnp.exp(sk - m)
            inv = jnp.where(chunk == g, 1.0 / den, inv)
        o = (_dot(e_scr[sub, :, :PAST_LEN], v4c[...]) + _dot(e_scr[sub, :, PAST_LEN:], vp[band, :]))
        acc = _gather_group_outputs(o, Q_BLOCK)
        y_ref[qrows, :] = (acc * inv * _silu(z_ref[qrows, :].astype(F32))).astype(BF16)


def _lat_attn(sink, q, k4, v4, kc, vc, j, z):
    padded = DEC_SEQ + 2 * Q_BLOCK
    rows4 = ATTN_GROUP * Q_BLOCK
    row0 = N_PROMPT_TOK // LAT_STEP
    seq0 = N_PROMPT_TOK // DEC_SEQ
    stage1 = lambda t: _lat_block(jnp.minimum(t, LAT_BLOCKS - 1))
    stage2 = lambda t: _lat_block(jnp.maximum(t - 1, 0))

    def spec(shape, stage, index):
        return pl.BlockSpec(shape, lambda t: index(*stage(t)))

    tile = (LAT_STEP, MXU_N)
    seq = (DEC_SEQ, MXU_N)
    ctx = (None, None, None, ATTN_HEAD_DIM, PAST_LEN)
    return pl.pallas_call(
        _lat_attn_kernel,
        out_shape=jax.ShapeDtypeStruct((N_SAMPLE_TOK, ATTN_WIDTH), BF16),
        grid=(LAT_BLOCKS + 1,),
        in_specs=[pl.BlockSpec(memory_space=pltpu.SMEM),
                  spec(tile, stage1, lambda b, h, n: (row0 + b * LAT_PER_SEQ + n, h)),
                  spec(seq, stage1, lambda b, h, n: (seq0 + b, h)),
                  spec(seq, stage2, lambda b, h, n: (seq0 + b, h)),
                  spec(ctx, stage1, lambda b, h, n: (b, j, h, 0, 0)),
                  spec(ctx, stage2, lambda b, h, n: (b, j, h, 0, 0)),
                  spec(tile, stage2, lambda b, h, n: (row0 + b * LAT_PER_SEQ + n, h))],
        out_specs=spec(tile, stage2, lambda b, h, n: (b * LAT_PER_SEQ + n, h)),
        scratch_shapes=[pltpu.VMEM((padded, MXU_N), BF16), pltpu.VMEM((padded, MXU_N), BF16),
                        pltpu.VMEM((MXU_N, PAST_LEN), BF16), pltpu.VMEM((PAST_LEN, MXU_N), BF16),
                        pltpu.VMEM((Q_SUB, rows4, PAST_LEN), F32),
                        pltpu.VMEM((Q_SUB, rows4, 3 * Q_BLOCK), F32),
                        pltpu.VMEM((Q_SUB, rows4, PAST_LEN), F32),
                        pltpu.VMEM((Q_SUB, rows4, 3 * Q_BLOCK), F32),
                        pltpu.VMEM((Q_SUB, rows4, PAST_LEN + 3 * Q_BLOCK), BF16)],
        compiler_params=_params("arbitrary", vmem=VMEM_LIMIT),
        name="lat_attn",
    )(sink, q, k4, v4, kc, vc, z)


def _out_kernel(*refs, n_y, n_x):
    t = PROJ
    y_refs, refs = refs[:n_y], refs[n_y:]
    wf_ref, refs = refs[0], refs[1:]
    x_refs, (gate_ref, o_ref, w_ref) = refs[:n_x], refs[n_x:]
    _cast_weight_once(wf_ref, w_ref)
    o_ref[...] = _read_parts(t, x_refs) + gate_ref[...] * _dot(_read_parts(t, y_refs), w_ref[...])


def _out_proj(y_parts, w_all, j, x_parts, mod):
    t = PROJ
    k = w_all.shape[1]
    return pl.pallas_call(
        functools.partial(_out_kernel, n_y=len(y_parts), n_x=len(x_parts)),
        out_shape=jax.ShapeDtypeStruct((N_TOK, D_MODEL), F32),
        grid=(t.n_tiles,),
        in_specs=(_part_specs(t, y_parts, k) + [_layer_spec(w_all, j)]
                  + _part_specs(t, x_parts, D_MODEL) + [t.mod(2)]),
        out_specs=t.rows(D_MODEL),
        scratch_shapes=[pltpu.VMEM((k, D_MODEL), BF16)],
        compiler_params=_params("arbitrary", vmem=VMEM_LIMIT),
        name="out_proj",
    )(*y_parts, w_all, *x_parts, mod)


def _out_final_kernel(yp_ref, ys_ref, wf_ref, x_ref, gate_ref, fg_ref, op_ref, os_ref, w_ref):
    t = PROJ
    i = pl.program_id(0)
    _cast_weight_once(wf_ref, w_ref)
    y = _read_parts(t, (yp_ref, ys_ref))
    r = _rms(x_ref[...] + gate_ref[...] * _dot(y, w_ref[...]), fg_ref[...])

    @pl.when(i < t.n_prompt)
    def _():
        op_ref[...] = r

    @pl.when(i >= t.n_prompt)
    def _():
        os_ref[...] = r


def _out_proj_final(y_parts, w_all, j, x, mod, final_g):
    t = PROJ
    k = w_all.shape[1]
    return pl.pallas_call(
        _out_final_kernel,
        out_shape=(jax.ShapeDtypeStruct((N_PROMPT_TOK, D_MODEL), F32),
                   jax.ShapeDtypeStruct((N_SAMPLE_TOK, D_MODEL), F32)),
        grid=(t.n_tiles,),
        in_specs=(_part_specs(t, y_parts, k)
                  + [_layer_spec(w_all, j), t.rows(D_MODEL), t.mod(2), _const_spec((1, D_MODEL))]),
        out_specs=(t.prompt_rows(D_MODEL), t.sample_rows(D_MODEL)),
        scratch_shapes=[pltpu.VMEM((k, D_MODEL), BF16)],
        compiler_params=_params("arbitrary", vmem=VMEM_LIMIT),
        name="out_proj_final",
    )(*y_parts, w_all, x, mod, final_g)


def _pool_in_kernel(x_ref, g_ref, sh_ref, sc_ref, wf_ref, u_ref, z_ref, w_ref):
    _cast_weight_once(wf_ref, w_ref)
    h = _norm_mod(x_ref[...], g_ref, sh_ref, sc_ref)

    def store_u(c, a):
        u_ref[:, c * MXU_N:(c + 1) * MXU_N] = a

    def store_z(c, a):
        z_ref[:, c * MXU_N:(c + 1) * MXU_N] = a.astype(BF16)

    _proj_chunks(h, w_ref, 0, D_MODEL, store_u)
    _proj_chunks(h, w_ref, D_MODEL, D_MODEL, store_z)


def _pool_in(x, g, mod, w_all, j):
    t = PROJ
    return pl.pallas_call(
        _pool_in_kernel,
        out_shape=(jax.ShapeDtypeStruct((N_TOK, D_MODEL), F32),
                   jax.ShapeDtypeStruct((N_TOK, D_MODEL), BF16)),
        grid=(t.n_tiles,),
        in_specs=[t.rows(D_MODEL), _const_spec((1, D_MODEL)), t.mod(0), t.mod(1),
                  _layer_spec(w_all, j)],
        out_specs=(t.rows(D_MODEL), t.rows(D_MODEL)),
        scratch_shapes=[pltpu.VMEM((D_MODEL, 2 * D_MODEL), BF16)],
        compiler_params=_params("arbitrary", vmem=VMEM_LIMIT),
        name="pool_in",
    )(x, g, mod, mod, w_all)


def _split_bf16(a):
    hi = a.astype(BF16)
    return hi, (a - hi.astype(F32)).astype(BF16)


def _band_ones(shape, lo, hi):
    d = lax.broadcasted_iota(jnp.int32, shape, 1) - lax.broadcasted_iota(jnp.int32, shape, 0)
    return jnp.logical_and(d >= lo, d <= hi).astype(F32).astype(BF16)


def _pool_out_kernel(u_ref, up_ref, un_ref, z_ref, wgf_ref, ps_ref, wof_ref, x_ref, gate_ref,
                     o_ref, y_ref, wg_ref, wo_ref, band_ref):
    t = POOL
    i = pl.program_id(0)
    _cast_weight_once(wgf_ref, wg_ref)
    _cast_weight_once(wof_ref, wo_ref)

    @pl.when(i == 0)
    def _():
        for g, w in enumerate(POOL_WINDOWS):
            band_ref[g] = _band_ones((t.tm, t.tm), -(w // 2), w - 1 - w // 2)

    is_dec = i >= t.n_prompt
    st = t.seq_tile(i)
    has_left = jnp.logical_and(is_dec, st != 0)
    has_right = jnp.logical_and(is_dec, st != t.per_dec_seq - 1)
    seq_len = jnp.where(is_dec, DEC_SEQ, SEQ)
    h = POOL_HALO
    halo = jnp.concatenate([jnp.where(has_left, up_ref[...], 0.0),
                            jnp.where(has_right, un_ref[...], 0.0)], axis=0)
    halo_hi, halo_lo = _split_bf16(halo)
    pos = st * t.tm + lax.broadcasted_iota(jnp.int32, (t.tm, 1), 0)
    rr = lax.broadcasted_iota(jnp.int32, (2 * h, 2 * h), 0)
    cc = lax.broadcasted_iota(jnp.int32, (2 * h, 2 * h), 1)
    for g, w in enumerate(POOL_WINDOWS):
        left = w // 2
        right = w - 1 - left
        cols = slice(g * POOL_GROUP_DIM, (g + 1) * POOL_GROUP_DIM)
        u = u_ref[:, cols]
        u_hi, u_lo = _split_bf16(u)
        s = _dot(band_ref[g], u_hi) + _dot(band_ref[g], u_lo)
        top = jnp.logical_and(jnp.logical_and(rr < h, cc < h), cc - h >= rr - left)
        bot = jnp.logical_and(jnp.logical_and(rr >= h, cc >= h), cc - h <= rr - 2 * h + right)
        edge = jnp.logical_or(top, bot).astype(F32).astype(BF16)
        se = _dot(edge, halo_hi[:, cols]) + _dot(edge, halo_lo[:, cols])
        s = jnp.concatenate([s[:h] + se[:h], s[h:t.tm - h], s[t.tm - h:] + se[h:]], axis=0)
        cnt = (jnp.minimum(pos + right + 1, seq_len) - jnp.maximum(pos - left, 0)).astype(F32)
        d = (s / cnt - u).astype(BF16)
        yg = _dot(d, wg_ref[g]) * ps_ref[:, cols] * _silu(z_ref[:, cols].astype(F32))
        y_ref[:, cols] = yg.astype(BF16)
    o_ref[...] = x_ref[...] + gate_ref[...] * _dot(y_ref[...], wo_ref[...])


def _pool_out(u, z, wg_all, ps, wo_all, j, x, mod):
    t = POOL
    per = t.tm // POOL_HALO
    n_halo = N_TOK // POOL_HALO
    return pl.pallas_call(
        _pool_out_kernel,
        out_shape=jax.ShapeDtypeStruct((N_TOK, D_MODEL), F32),
        grid=(t.n_tiles,),
        in_specs=[
            t.rows(D_MODEL),
            pl.BlockSpec((POOL_HALO, D_MODEL), lambda i: (jnp.maximum(i * per - 1, 0), 0)),
            pl.BlockSpec((POOL_HALO, D_MODEL), lambda i: (jnp.minimum((i + 1) * per, n_halo - 1), 0)),
            t.rows(D_MODEL),
            _layer_spec(wg_all, j),
            _const_spec((1, D_MODEL)),
            _layer_spec(wo_all, j),
            t.rows(D_MODEL),
            t.mod(2),
        ],
        out_specs=t.rows(D_MODEL),
        scratch_shapes=[pltpu.VMEM((t.tm, D_MODEL), BF16),
                        pltpu.VMEM((len(POOL_WINDOWS), POOL_GROUP_DIM, POOL_GROUP_DIM), BF16),
                        pltpu.VMEM((D_MODEL, D_MODEL), BF16),
                        pltpu.VMEM((len(POOL_WINDOWS), t.tm, t.tm), BF16)],
        compiler_params=_params("arbitrary", vmem=VMEM_LIMIT),
        name="pool_out",
    )(u, u, u, z, wg_all, ps, wo_all, x, mod)


def _ret_in_kernel(x_ref, g_ref, sh_ref, sc_ref, w_ref, q_ref, kt_ref, v_ref, z_ref, wkt_ref):
    t = PROJ

    @pl.when(pl.program_id(0) == 0)
    def _():
        for r in range(0, RET_QK_WIDTH, MXU_N):
            wk = w_ref[:, RET_QK_WIDTH + r:RET_QK_WIDTH + r + MXU_N]
            wkt_ref[r:r + MXU_N, :] = wk.astype(F32).T.astype(BF16)

    h = _norm_mod(x_ref[...], g_ref, sh_ref, sc_ref)

    def store(ref):
        def f(c, a):
            ref[:, c * MXU_N:(c + 1) * MXU_N] = a.astype(BF16)
        return f

    _proj_chunks(h, w_ref, 0, RET_QK_WIDTH, store(q_ref))
    kt = (_dot_nt(wkt_ref[...], h) * RET_KEY_DIM ** -0.5).astype(BF16)
    for c in range(t.tm // RET_CHUNK):
        kt_ref[c] = kt[:, c * RET_CHUNK:(c + 1) * RET_CHUNK]
    _proj_chunks(h, w_ref, 2 * RET_QK_WIDTH, RET_V_WIDTH, store(v_ref))
    _proj_chunks(h, w_ref, 2 * RET_QK_WIDTH + RET_V_WIDTH, RET_V_WIDTH, store(z_ref))


def _ret_in(x, g, mod, w):
    t = PROJ
    per = t.tm // RET_CHUNK
    n_in = 2 * RET_QK_WIDTH + 2 * RET_V_WIDTH
    return pl.pallas_call(
        _ret_in_kernel,
        out_shape=(jax.ShapeDtypeStruct((N_TOK, RET_QK_WIDTH), BF16),
                   jax.ShapeDtypeStruct((N_TOK // RET_CHUNK, RET_QK_WIDTH, RET_CHUNK), BF16),
                   jax.ShapeDtypeStruct((N_TOK, RET_V_WIDTH), BF16),
                   jax.ShapeDtypeStruct((N_TOK, RET_V_WIDTH), BF16)),
        grid=(t.n_tiles,),
        in_specs=[t.rows(D_MODEL), _const_spec((1, D_MODEL)), t.mod(0), t.mod(1),
                  _const_spec((D_MODEL, n_in))],
        out_specs=(t.rows(RET_QK_WIDTH),
                   pl.BlockSpec((per, RET_QK_WIDTH, RET_CHUNK), lambda i: (i, 0, 0)),
                   t.rows(RET_V_WIDTH), t.rows(RET_V_WIDTH)),
        scratch_shapes=[pltpu.VMEM((RET_QK_WIDTH, D_MODEL), BF16)],
        compiler_params=_params("arbitrary", vmem=VMEM_LIMIT),
        name="ret_in",
    )(x, g, mod, mod, w)


def _pos(shape, axis):
    return lax.broadcasted_iota(jnp.int32, shape, axis).astype(F32)


def _ret_tables_kernel(lgf_ref, lgb_ref, decay_ref, row_ref, col_ref, cdec_ref):
    h = pl.program_id(0)
    lg_f = lgf_ref[h]
    lg_b = lgb_ref[h]
    c = RET_CHUNK
    diff = _pos((c, c), 0) - _pos((c, c), 1)
    fwd = jnp.where(diff >= 0, jnp.exp(jnp.maximum(diff, 0.0) * lg_f), 0.0)
    bwd = jnp.where(diff <= 0, jnp.exp(jnp.maximum(-diff, 0.0) * lg_b), 0.0)
    decay_ref[...] = fwd + bwd
    j = _pos((RET_TAB_ROWS, c), 1)
    row_ref[0] = jnp.exp((c - 1.0 - j) * lg_f)
    row_ref[1] = jnp.exp(j * lg_b)
    i = _pos((c, LANES), 0)
    col_ref[0] = jnp.exp((i + 1.0) * lg_f)
    col_ref[1] = jnp.exp((c - i) * lg_b)
    full = jnp.full((RET_TAB_ROWS, RET_VAL_DIM), float(c), F32)
    cdec_ref[0] = jnp.exp(full * lg_f)
    cdec_ref[1] = jnp.exp(full * lg_b)


def _ret_tables(lg_f, lg_b):
    smem = pl.BlockSpec(memory_space=pltpu.SMEM)
    c = RET_CHUNK
    shapes = ((c, c), (2, RET_TAB_ROWS, c), (2, c, LANES), (2, RET_TAB_ROWS, RET_VAL_DIM))
    return pl.pallas_call(
        _ret_tables_kernel,
        out_shape=tuple(jax.ShapeDtypeStruct((RET_HEADS,) + s, F32) for s in shapes),
        grid=(RET_HEADS,),
        in_specs=[smem, smem],
        out_specs=tuple(pl.BlockSpec((None,) + s, lambda h, n=len(s): (h,) + (0,) * n) for s in shapes),
        compiler_params=_params("arbitrary"),
        name="ret_tables",
    )(lg_f, lg_b)


def _group_norm_gate(o, gn, z):
    mu = jnp.mean(o, axis=-1, keepdims=True)
    var = jnp.mean(jnp.square(o - mu), axis=-1, keepdims=True)
    on = (o - mu) * lax.rsqrt(var + EPS)
    return (on * gn * _silu(z.astype(F32))).astype(BF16)


def _ret_ctx_kernel(q_ref, kt_ref, v_ref, z_ref, gn_ref, decay_ref, row_ref, y_ref, sf_ref, sb_ref):
    for h in range(RET_HEADS):
        kc = slice(h * RET_KEY_DIM, (h + 1) * RET_KEY_DIM)
        vc = slice(h * RET_VAL_DIM, (h + 1) * RET_VAL_DIM)
        kt = kt_ref[kc, :]
        v = v_ref[:, vc]
        att = (_dot(q_ref[:, kc], kt) * decay_ref[h]).astype(BF16)
        y_ref[:, vc] = _group_norm_gate(_dot(att, v), gn_ref[:, vc], z_ref[:, vc])
        ktf = kt.astype(F32)
        sf_ref[h] = _dot((ktf * row_ref[h, 0, 0:1, :]).astype(BF16), v)
        sb_ref[h] = _dot((ktf * row_ref[h, 1, 0:1, :]).astype(BF16), v)


def _ret_ctx(q, kt, v, z, gn, decay, row):
    c = RET_CHUNK
    wide = lambda w: pl.BlockSpec((SEQ, w), lambda b: (b, 0))
    st_spec = pl.BlockSpec((None, RET_HEADS, RET_KEY_DIM, RET_VAL_DIM), lambda b: (b, 0, 0, 0))
    st_shape = jax.ShapeDtypeStruct((BATCH, RET_HEADS, RET_KEY_DIM, RET_VAL_DIM), F32)
    return pl.pallas_call(
        _ret_ctx_kernel,
        out_shape=(jax.ShapeDtypeStruct((N_PROMPT_TOK, RET_V_WIDTH), BF16), st_shape, st_shape),
        grid=(BATCH,),
        in_specs=[wide(RET_QK_WIDTH), pl.BlockSpec((None, RET_QK_WIDTH, c), lambda b: (b, 0, 0)),
                  wide(RET_V_WIDTH), wide(RET_V_WIDTH), _const_spec((1, RET_V_WIDTH)),
                  _const_spec((RET_HEADS, c, c)), _const_spec((RET_HEADS, 2, RET_TAB_ROWS, c))],
        out_specs=(wide(RET_V_WIDTH), st_spec, st_spec),
        compiler_params=_params("arbitrary", vmem=VMEM_LIMIT),
        name="ret_ctx",
    )(q, kt, v, z, gn, decay, row)


def _ret_lat_kernel(q_ref, kt_ref, v_ref, z_ref, gn_ref, decay_ref, row_ref, col_ref, cdec_ref,
                    s0f_ref, s0b_ref, y_ref, sf_all, sb_all, sf_acc, sb_acc):
    c = RET_CHUNK
    n_chunks = DEC_SEQ // c
    rows_of = lambda ci: pl.ds(pl.multiple_of(ci * c, c), c)

    sf_acc[...] = s0f_ref[...]
    sb_acc[...] = s0b_ref[...]

    def scan_step(i, carry):
        cf = i
        cb = n_chunks - 1 - i
        sf_all[cf] = sf_acc[...].astype(BF16)
        sb_all[cb] = sb_acc[...].astype(BF16)
        uf = _dot((kt_ref[cf].astype(F32) * row_ref[0, 0:1, :]).astype(BF16), v_ref[rows_of(cf), :])
        ub = _dot((kt_ref[cb].astype(F32) * row_ref[1, 0:1, :]).astype(BF16), v_ref[rows_of(cb), :])
        sf_acc[...] = sf_acc[...] * cdec_ref[0, 0:1, :] + uf
        sb_acc[...] = sb_acc[...] * cdec_ref[1, 0:1, :] + ub
        return carry

    lax.fori_loop(0, n_chunks, scan_step, 0)

    def out_step(ci, carry):
        rows = rows_of(ci)
        q = q_ref[rows, :]
        qf = q.astype(F32)
        qdec_f = jnp.concatenate([col_ref[0]] * (RET_KEY_DIM // LANES), axis=1)
        qdec_b = jnp.concatenate([col_ref[1]] * (RET_KEY_DIM // LANES), axis=1)
        att = (_dot(q, kt_ref[ci]) * decay_ref[...]).astype(BF16)
        o = (_dot(att, v_ref[rows, :])
             + _dot((qf * qdec_f).astype(BF16), sf_all[ci])
             + _dot((qf * qdec_b).astype(BF16), sb_all[ci]))
        y_ref[rows, :] = _group_norm_gate(o, gn_ref[...], z_ref[rows, :])
        return carry

    lax.fori_loop(0, n_chunks, out_step, 0, unroll=2)


def _ret_lat(q, kt, v, z, gn, decay, row, col, cdec, s0f, s0b):
    c = RET_CHUNK
    n_chunks = DEC_SEQ // c
    row0 = N_PROMPT_TOK // DEC_SEQ
    qk_spec = pl.BlockSpec((DEC_SEQ, RET_KEY_DIM), lambda b, h: (row0 + b, h))
    v_spec = pl.BlockSpec((DEC_SEQ, RET_VAL_DIM), lambda b, h: (row0 + b, h))
    st_spec = pl.BlockSpec((None, None, RET_KEY_DIM, RET_VAL_DIM), lambda b, h: (b, h, 0, 0))
    tab = lambda *s: pl.BlockSpec((None,) + s, lambda b, h: (h,) + (0,) * len(s))
    states = pltpu.VMEM((n_chunks, RET_KEY_DIM, RET_VAL_DIM), BF16)
    acc = pltpu.VMEM((RET_KEY_DIM, RET_VAL_DIM), F32)
    return pl.pallas_call(
        _ret_lat_kernel,
        out_shape=jax.ShapeDtypeStruct((N_SAMPLE_TOK, RET_V_WIDTH), BF16),
        grid=(DEC_BATCH, RET_HEADS),
        in_specs=[qk_spec,
                  pl.BlockSpec((n_chunks, RET_KEY_DIM, c), lambda b, h: (row0 + b, h, 0)),
                  v_spec, v_spec, pl.BlockSpec((1, RET_VAL_DIM), lambda b, h: (0, h)),
                  tab(c, c), tab(2, RET_TAB_ROWS, c), tab(2, c, LANES), tab(2, RET_TAB_ROWS, RET_VAL_DIM),
                  st_spec, st_spec],
        out_specs=pl.BlockSpec((DEC_SEQ, RET_VAL_DIM), lambda b, h: (b, h)),
        scratch_shapes=[states, states, acc, acc],
        compiler_params=_params("arbitrary", "arbitrary", vmem=VMEM_LIMIT),
        name="ret_lat",
    )(q, kt, v, z, gn, decay, row, col, cdec, s0f, s0b)


def _rope_tables(tm):
    n_rows = DEC_SEQ // GRID_W
    rows = jnp.repeat(jnp.arange(n_rows), GRID_W).astype(F32)
    cols = jnp.tile(jnp.arange(GRID_W), n_rows).astype(F32)
    half = ATTN_HEAD_DIM // 4
    inv = ROPE_BASE ** (-jnp.arange(half, dtype=F32) / half)
    ang_r = rows[:, None] * inv[None, :]
    ang_c = cols[:, None] * inv[None, :]
    cos = jnp.concatenate([jnp.cos(ang_r), jnp.cos(ang_r), jnp.cos(ang_c), jnp.cos(ang_c)], axis=-1)
    sin = jnp.concatenate([-jnp.sin(ang_r), jnp.sin(ang_r), -jnp.sin(ang_c), jnp.sin(ang_c)], axis=-1)
    cos = jnp.concatenate([jnp.ones((tm, ATTN_HEAD_DIM), F32), cos], axis=0)
    sin = jnp.concatenate([jnp.zeros((tm, ATTN_HEAD_DIM), F32), sin], axis=0)
    return jnp.tile(cos, (1, 2)), jnp.tile(sin, (1, 2))


def kernel(x_prompt, x_sample, cache_k, cache_v, state_fwd, state_bwd, c, c_ctx, norm_g, ada_w, ada_b, attn_w_in, attn_w_out, attn_sink, pool_w_in, pool_w_grp, pool_scale, pool_w_out, ret_w_in, ret_decay_fwd, ret_decay_bwd, ret_gn_g, ret_w_out, final_g):
    x_parts = (x_prompt.reshape(N_PROMPT_TOK, D_MODEL), x_sample.reshape(N_SAMPLE_TOK, D_MODEL))
    cond = jnp.concatenate([c_ctx[None, :], c,
                            jnp.zeros((N_COND - 1 - DEC_BATCH, D_MODEL), F32)], axis=0)
    mods = _ada_table(cond.T, ada_w, ada_b).reshape(DEPTH, N_COND, 1, 3 * D_MODEL)
    cos_t, sin_t = _rope_tables(PROJ.tm)

    to_kernel = lambda a: jnp.transpose(a, (0, 1, 3, 4, 2))
    from_kernel = lambda a: jnp.transpose(
        a.reshape(a.shape[0], a.shape[1], ATTN_KV_HEADS, ATTN_HEAD_DIM, a.shape[3]), (0, 1, 4, 2, 3))
    ctx_k, ctx_v = to_kernel(cache_k), to_kernel(cache_v)

    caches = ()
    new_sf = new_sb = None
    for i in range(DEPTH):
        kind, j = i % N_MIXERS, i // N_MIXERS
        g = norm_g[i].reshape(1, D_MODEL)
        mod = mods[i]
        last = i == DEPTH - 1
        if kind == 0:
            q, k4, v4, z, *caches = _attn_in(x_parts, g, mod, attn_w_in, j, cos_t, sin_t, tuple(caches))
            y_parts = (_ctx_attn(attn_sink[j], q, k4, v4, z),
                       _lat_attn(attn_sink[j], q, k4, v4, ctx_k, ctx_v, j, z))
            w_out = attn_w_out
        elif kind == 1:
            (x,) = x_parts
            u, z = _pool_in(x, g, mod, pool_w_in, j)
            assert not last
            x_parts = (_pool_out(u, z, pool_w_grp, pool_scale[j].reshape(1, D_MODEL), pool_w_out, j,
                                 x, mod),)
            continue
        else:
            (x,) = x_parts
            lg_f = jax.nn.log_sigmoid(ret_decay_fwd[j].astype(F32))
            lg_b = jax.nn.log_sigmoid(ret_decay_bwd[j].astype(F32))
            gn = ret_gn_g[j].reshape(1, RET_V_WIDTH)
            q, kt, v, z = _ret_in(x, g, mod, ret_w_in[j].astype(BF16))
            decay, row, col, cdec = _ret_tables(lg_f, lg_b)
            y_ctx, new_sf, new_sb = _ret_ctx(q, kt, v, z, gn, decay, row)
            y_parts = (y_ctx, _ret_lat(q, kt, v, z, gn, decay, row, col, cdec,
                                       state_fwd[:, j], state_bwd[:, j]))
            w_out = ret_w_out
        if last:
            (x,) = x_parts
            y_prompt, y_sample = _out_proj_final(y_parts, w_out, j, x, mod, final_g.reshape(1, D_MODEL))
        else:
            x_parts = (_out_proj(y_parts, w_out, j, x_parts, mod),)
    new_k, new_v = caches
    return (y_prompt.reshape(BATCH, SEQ, D_MODEL), y_sample.reshape(DEC_BATCH, DEC_SEQ, D_MODEL),
            from_kernel(new_k), from_kernel(new_v), new_sf[:, None], new_sb[:, None])
```

```python
import functools
from typing import NamedTuple

import jax
import jax.numpy as jnp
from jax import lax
from jax.experimental import pallas as pl
from jax.experimental.pallas import tpu as pltpu

F32 = jnp.float32
BF16 = jnp.bfloat16

D_MODEL = 1024
BATCH = 16
SEQ = 256
DEPTH = 4
DEC_BATCH = 2
DEC_SEQ = 2048
PAST_LEN = 512
GRID_W = 64
N_MIXERS = 3
ATTN_HEADS = 16
ATTN_KV_HEADS = 4
ATTN_HEAD_DIM = 64
ATTN_GROUP = 4
ATTN_WIDTH = 1024
ATTN_KV_WIDTH = 256
WINDOW = 128
ROPE_BASE = 10000.0
POOL_WINDOWS = (2, 4, 8, 16)
POOL_GROUP_DIM = 256
RET_HEADS = 4
RET_KEY_DIM = 256
RET_VAL_DIM = 512
RET_QK_WIDTH = 1024
RET_V_WIDTH = 2048
EPS = 1e-6
NEG_INF = -1e30

N_PROMPT_TOK = BATCH * SEQ
N_SAMPLE_TOK = DEC_BATCH * DEC_SEQ
N_TOK = N_PROMPT_TOK + N_SAMPLE_TOK
N_COND = 8
LANES = 128
MXU_N = 256
Q_BLOCK = 128
Q_SUB = 4
RET_CHUNK = 256
RET_TAB_ROWS = 8
POOL_HALO = 8
VMEM_LIMIT = 48 * 1024 * 1024
FUSED_VMEM_LIMIT = 58 * 1024 * 1024


class Tiling(NamedTuple):
    tm: int

    @property
    def n_tiles(self):
        return N_TOK // self.tm

    @property
    def n_prompt(self):
        return N_PROMPT_TOK // self.tm

    @property
    def per_dec_seq(self):
        return DEC_SEQ // self.tm

    def cond(self, i):
        return jnp.where(i < self.n_prompt, 0, 1 + (i - self.n_prompt) // self.per_dec_seq)

    def seq_tile(self, i):
        return jnp.where(i < self.n_prompt, 0, (i - self.n_prompt) % self.per_dec_seq)

    def rows(self, width):
        return pl.BlockSpec((self.tm, width), lambda i: (i, 0))

    def prompt_rows(self, width):
        return pl.BlockSpec((self.tm, width), lambda i: (jnp.minimum(i, self.n_prompt - 1), 0))

    def sample_rows(self, width):
        return pl.BlockSpec((self.tm, width), lambda i: (jnp.maximum(i - self.n_prompt, 0), 0))

    def mod(self, part):
        return pl.BlockSpec((None, 1, D_MODEL), lambda i: (self.cond(i), 0, part))


PROJ = Tiling(512)
POOL = Tiling(SEQ)


def _silu(z):
    return z * (1.0 / (1.0 + jnp.exp(-z)))


def _dot(a, b):
    return jnp.dot(a, b, preferred_element_type=F32)


def _dot_nt(a, b):
    return lax.dot_general(a, b, (((1,), (1,)), ((), ())), preferred_element_type=F32)


def _params(*sem, vmem=None):
    return pltpu.CompilerParams(dimension_semantics=sem, vmem_limit_bytes=vmem)


def _const_spec(shape):
    nd = len(shape)
    return pl.BlockSpec(shape, lambda *_: (0,) * nd, pipeline_mode=pl.Buffered(1))


def _part_specs(t, parts, width):
    if len(parts) == 1:
        return [t.rows(width)]
    return [t.prompt_rows(width), t.sample_rows(width)]


def _read_parts(t, refs):
    if len(refs) == 1:
        return refs[0][...]
    return jnp.where(pl.program_id(0) < t.n_prompt, refs[0][...], refs[1][...])


def _ada_kernel(cond_ref, w_ref, b_ref, o_ref):
    s = _silu(cond_ref[...])
    w = w_ref[...]
    rows = [jnp.sum(s[:, c:c + 1] * w, axis=0, keepdims=True) + b_ref[...] for c in range(1 + DEC_BATCH)]
    rows.append(jnp.zeros((N_COND - len(rows), w.shape[1]), F32))
    o_ref[...] = jnp.concatenate(rows, axis=0)


def _ada_table(cond, ada_w, ada_b):
    tn = 768
    return pl.pallas_call(
        _ada_kernel,
        out_shape=jax.ShapeDtypeStruct((DEPTH, N_COND, 3 * D_MODEL), F32),
        grid=(DEPTH, 3 * D_MODEL // tn),
        in_specs=[
            pl.BlockSpec((D_MODEL, N_COND), lambda l, n: (0, 0)),
            pl.BlockSpec((None, D_MODEL, tn), lambda l, n: (l, 0, n)),
            pl.BlockSpec((None, 1, tn), lambda l, n: (l, 0, n)),
        ],
        out_specs=pl.BlockSpec((None, N_COND, tn), lambda l, n: (l, 0, n)),
        compiler_params=_params("arbitrary", "arbitrary"),
        name="ada_table",
    )(cond, ada_w, ada_b.reshape(DEPTH, 1, 3 * D_MODEL))


def _rms(x, g):
    return x * lax.rsqrt(jnp.mean(x * x, axis=-1, keepdims=True) + EPS) * g


def _norm_mod(x, g_ref, sh_ref, sc_ref):
    return (_rms(x, g_ref[...]) * (1.0 + sc_ref[...]) + sh_ref[...]).astype(BF16)


def _proj_chunks(h, w_ref, lo, width, store):
    for c in range(width // MXU_N):
        store(c, _dot(h, w_ref[:, lo + c * MXU_N:lo + (c + 1) * MXU_N]))


def _layer_spec(w, j):
    nd = w.ndim - 1
    return pl.BlockSpec((None,) + w.shape[1:], lambda *_: (j,) + (0,) * nd, pipeline_mode=pl.Buffered(1))


def _cast_weight_once(w_ref, wb_ref):
    @pl.when(pl.program_id(0) == 0)
    def _():
        rows = wb_ref.shape[-2]
        for r in range(0, rows, MXU_N):
            wb_ref[..., r:r + MXU_N, :] = w_ref[..., r:r + MXU_N, :].astype(BF16)


class Mixed(NamedTuple):
    y_parts: tuple
    w_out: jax.Array
    j: int
    mod: jax.Array


def _residual_specs(t, mixed, x_parts):
    k = mixed.w_out.shape[1]
    return (_part_specs(t, mixed.y_parts, k) + [_layer_spec(mixed.w_out, mixed.j)]
            + _part_specs(t, x_parts, D_MODEL) + [t.mod(2)])


def _residual_args(mixed, x_parts):
    return (*mixed.y_parts, mixed.w_out, *x_parts, mixed.mod)


def _residual_update(t, refs, n_y, n_x, wo_ref, xo_ref):
    y_refs, wof_ref, x_refs, gate_ref = refs[:n_y], refs[n_y], refs[n_y + 1:n_y + 1 + n_x], refs[n_y + 1 + n_x]
    _cast_weight_once(wof_ref, wo_ref)
    x = _read_parts(t, x_refs) + gate_ref[...] * _dot(_read_parts(t, y_refs), wo_ref[...])
    xo_ref[...] = x
    return x


def _rep4(a, h):
    half = a[:, (h // 2) * LANES:(h // 2 + 1) * LANES]
    lane = lax.broadcasted_iota(jnp.int32, half.shape, 1)
    keep = (lane < ATTN_HEAD_DIM) if h % 2 == 0 else (lane >= ATTN_HEAD_DIM)
    m = jnp.where(keep, half, 0.0)
    s = m + pltpu.roll(m, ATTN_HEAD_DIM, 1)
    return jnp.concatenate([s, s], axis=1)


def _attn_in_kernel(*refs, n_y, n_x, n_alias):
    t = PROJ
    n_res = n_y + n_x + 2 if n_y else n_x
    res_refs, refs = refs[:n_res], refs[n_res:]
    (g_ref, sh_ref, sc_ref, wf_ref, cos_ref, sin_ref), refs = refs[:6], refs[6 + n_alias:]
    if n_y:
        xo_ref, q_ref, k4_ref, v4_ref, z_ref, kc_ref, vc_ref, wo_ref, w_ref = refs
        x = _residual_update(t, res_refs, n_y, n_x, wo_ref, xo_ref)
    else:
        q_ref, k4_ref, v4_ref, z_ref, kc_ref, vc_ref, w_ref = refs
        x = _read_parts(t, res_refs)
    i = pl.program_id(0)
    _cast_weight_once(wf_ref, w_ref)
    h = _norm_mod(x, g_ref, sh_ref, sc_ref)
    cos = cos_ref[...]
    sin = sin_ref[...]
    lane = lax.broadcasted_iota(jnp.int32, (t.tm, LANES), 1)
    first = (lane % (ATTN_HEAD_DIM // 2)) < ATTN_HEAD_DIM // 4

    def rope(a):
        rot = jnp.where(first, pltpu.roll(a, LANES - ATTN_HEAD_DIM // 4, 1),
                        pltpu.roll(a, ATTN_HEAD_DIM // 4, 1))
        return a * cos + rot * sin

    def rope_wide(a):
        return jnp.concatenate(
            [rope(a[:, s * LANES:(s + 1) * LANES]) for s in range(MXU_N // LANES)], axis=1)

    scale = ATTN_HEAD_DIM ** -0.5

    def store_q(c, a):
        q_ref[:, c * MXU_N:(c + 1) * MXU_N] = (rope_wide(a) * scale).astype(BF16)

    def store_z(c, a):
        z_ref[:, c * MXU_N:(c + 1) * MXU_N] = a.astype(BF16)

    _proj_chunks(h, w_ref, 0, ATTN_WIDTH, store_q)
    k = rope_wide(_dot(h, w_ref[:, ATTN_WIDTH:ATTN_WIDTH + ATTN_KV_WIDTH]))
    v = _dot(h, w_ref[:, ATTN_WIDTH + ATTN_KV_WIDTH:ATTN_WIDTH + 2 * ATTN_KV_WIDTH])
    for hh in range(ATTN_KV_HEADS):
        k4_ref[:, hh * MXU_N:(hh + 1) * MXU_N] = _rep4(k, hh).astype(BF16)
        v4_ref[:, hh * MXU_N:(hh + 1) * MXU_N] = _rep4(v, hh).astype(BF16)
    _proj_chunks(h, w_ref, ATTN_WIDTH + 2 * ATTN_KV_WIDTH, ATTN_WIDTH, store_z)

    @pl.when(i < t.n_prompt)
    def _():
        for s in range(t.tm // SEQ):
            kc_ref[s] = k[s * SEQ:(s + 1) * SEQ, :].T
            vc_ref[s] = v[s * SEQ:(s + 1) * SEQ, :].T


def _attn_in(mixed, x_parts, g, mod, w_all, j, cos_t, sin_t, caches):
    t = PROJ
    n_in = 2 * ATTN_WIDTH + 2 * ATTN_KV_WIDTH
    per = t.tm // SEQ
    rope_spec = pl.BlockSpec(
        (t.tm, LANES), lambda i: (jnp.where(i < t.n_prompt, 0, 1 + t.seq_tile(i)), 0))
    wide = jax.ShapeDtypeStruct((N_TOK, ATTN_WIDTH), BF16)
    n_attn = w_all.shape[0]
    cache = jax.ShapeDtypeStruct((BATCH, n_attn, ATTN_KV_WIDTH, SEQ), F32)
    cache_spec = pl.BlockSpec((per, None, ATTN_KV_WIDTH, SEQ),
                              lambda i: (jnp.minimum(i, t.n_prompt - 1), j, 0, 0))
    n_x = len(x_parts)
    if mixed is None:
        n_y, res_specs, res_args, res_out, res_ospecs, res_scratch = 0, _part_specs(t, x_parts, D_MODEL), x_parts, (), (), []
    else:
        n_y = len(mixed.y_parts)
        res_specs, res_args = _residual_specs(t, mixed, x_parts), _residual_args(mixed, x_parts)
        res_out = (jax.ShapeDtypeStruct((N_TOK, D_MODEL), F32),)
        res_ospecs = (t.rows(D_MODEL),)
        res_scratch = [pltpu.VMEM(mixed.w_out.shape[1:], BF16)]
    n_front = len(res_specs) + 6
    return pl.pallas_call(
        functools.partial(_attn_in_kernel, n_y=n_y, n_x=n_x, n_alias=len(caches)),
        out_shape=res_out + (wide, wide, wide, wide, cache, cache),
        grid=(t.n_tiles,),
        in_specs=res_specs + [
            _const_spec((1, D_MODEL)), t.mod(0), t.mod(1), _layer_spec(w_all, j),
            rope_spec, rope_spec] + [pl.BlockSpec(memory_space=pl.ANY)] * len(caches),
        out_specs=res_ospecs + (t.rows(ATTN_WIDTH),) * 4 + (cache_spec,) * 2,
        scratch_shapes=res_scratch + [pltpu.VMEM((D_MODEL, n_in), BF16)],
        input_output_aliases={n_front + c: len(res_out) + 4 + c for c in range(len(caches))},
        compiler_params=_params("arbitrary", vmem=FUSED_VMEM_LIMIT),
        name="attn_in",
    )(*res_args, g, mod, mod, w_all, cos_t, sin_t, *caches)


def _stack_group_queries(q):
    qf = q.astype(F32)
    chunk = lax.broadcasted_iota(jnp.int32, qf.shape, 1) // ATTN_HEAD_DIM
    return jnp.concatenate(
        [jnp.where(chunk == g, qf, 0.0) for g in range(ATTN_GROUP)], axis=0).astype(BF16)


def _gather_group_outputs(o, rows):
    chunk = lax.broadcasted_iota(jnp.int32, (rows, MXU_N), 1) // ATTN_HEAD_DIM
    acc = jnp.zeros((rows, MXU_N), F32)
    for g in range(ATTN_GROUP):
        acc = acc + jnp.where(chunk == g, o[g * rows:(g + 1) * rows], 0.0)
    return acc


def _sink_column(sink_ref, h, rows):
    grp = lax.broadcasted_iota(jnp.int32, (ATTN_GROUP * rows, 1), 0) // rows
    col = jnp.zeros((ATTN_GROUP * rows, 1), F32)
    for g in range(ATTN_GROUP):
        col = jnp.where(grp == g, sink_ref[h * ATTN_GROUP + g], col)
    return col


def _chunk_rows(dtype):
    chunk = lax.broadcasted_iota(jnp.int32, (1, MXU_N), 1) // ATTN_HEAD_DIM
    return [(chunk == g).astype(F32).astype(dtype) for g in range(ATTN_GROUP)]


def _block_diag_rows(x4):
    return jnp.concatenate([x4 * m for m in _chunk_rows(x4.dtype)], axis=0)


def _ctx_attn_kernel(sink_ref, q_ref, k4_ref, v4_ref, z_ref, y_ref):
    chunk = lax.broadcasted_iota(jnp.int32, (SEQ, MXU_N), 1) // ATTN_HEAD_DIM
    for h in range(ATTN_KV_HEADS):
        cols = slice(h * MXU_N, (h + 1) * MXU_N)
        s = _dot_nt(q_ref[:, cols], _block_diag_rows(k4_ref[:, cols]))
        inv = jnp.zeros((SEQ, MXU_N), F32)
        probs = []
        for g in range(ATTN_GROUP):
            sg = s[:, g * SEQ:(g + 1) * SEQ]
            sk = sink_ref[h * ATTN_GROUP + g]
            m = jnp.maximum(jnp.max(sg, axis=1, keepdims=True), sk)
            e = jnp.exp(sg - m)
            den = jnp.sum(e, axis=1, keepdims=True) + jnp.exp(sk - m)
            probs.append(e.astype(BF16))
            inv = jnp.where(chunk == g, 1.0 / den, inv)
        o = _dot(jnp.concatenate(probs, axis=1), _block_diag_rows(v4_ref[:, cols]))
        y_ref[:, cols] = (o * inv * _silu(z_ref[:, cols].astype(F32))).astype(BF16)


def _ctx_attn(sink, q, k4, v4, z):
    spec = pl.BlockSpec((SEQ, ATTN_WIDTH), lambda b: (b, 0))
    return pl.pallas_call(
        _ctx_attn_kernel,
        out_shape=jax.ShapeDtypeStruct((N_PROMPT_TOK, ATTN_WIDTH), BF16),
        grid=(BATCH,),
        in_specs=[pl.BlockSpec(memory_space=pltpu.SMEM), spec, spec, spec, spec],
        out_specs=spec,
        compiler_params=_params("arbitrary"),
        name="ctx_attn",
    )(sink, q, k4, v4, z)


LAT_STEP = Q_SUB * Q_BLOCK
LAT_PER_SEQ = DEC_SEQ // LAT_STEP
LAT_BLOCKS = DEC_BATCH * ATTN_KV_HEADS * LAT_PER_SEQ


def _lat_block(blk):
    return (blk // (ATTN_KV_HEADS * LAT_PER_SEQ), (blk // LAT_PER_SEQ) % ATTN_KV_HEADS, blk % LAT_PER_SEQ)


def _band_rows(qb):
    return pl.ds(pl.multiple_of(qb * Q_BLOCK, Q_BLOCK), 3 * Q_BLOCK)


def _pad_sequence(dst, src):
    zeros = jnp.zeros((Q_BLOCK, MXU_N), BF16)
    dst[0:Q_BLOCK, :] = zeros
    dst[Q_BLOCK:Q_BLOCK + DEC_SEQ, :] = src[...]
    dst[Q_BLOCK + DEC_SEQ:, :] = zeros


def _lat_attn_kernel(sink_ref, q_ref, k4_ref, v4_ref, kc_ref, vc_ref, z_ref, y_ref,
                     kp, vp, k4c, v4c, sc_a, sl_a, sc_b, sl_b, e_scr):
    t = pl.program_id(0)
    _, _, n1 = _lat_block(jnp.minimum(t, LAT_BLOCKS - 1))
    _, h0, n0 = _lat_block(jnp.maximum(t - 1, 0))
    n_blocks = DEC_SEQ // Q_BLOCK

    @pl.when(t == 0)
    def _():
        sc_b[...] = jnp.zeros(sc_b.shape, F32)
        sl_b[...] = jnp.zeros(sl_b.shape, F32)

    @pl.when(n1 == 0)
    def _():
        _pad_sequence(kp, k4_ref)
        k4c[...] = jnp.concatenate([kc_ref[...]] * ATTN_GROUP, axis=0).astype(BF16)

    @pl.when(n0 == 0)
    def _():
        _pad_sequence(vp, v4_ref)
        v4c[...] = jnp.concatenate([vc_ref[...]] * ATTN_GROUP, axis=0).T.astype(BF16)

    @pl.when(t % 2 == 0)
    def _():
        _lat_stages(sink_ref, q_ref, z_ref, y_ref, kp, vp, k4c, v4c, e_scr, n1, h0, n0,
                    sc_a, sl_a, sc_b, sl_b)

    @pl.when(t % 2 == 1)
    def _():
        _lat_stages(sink_ref, q_ref, z_ref, y_ref, kp, vp, k4c, v4c, e_scr, n1, h0, n0,
                    sc_b, sl_b, sc_a, sl_a)


def _lat_stages(sink_ref, q_ref, z_ref, y_ref, kp, vp, k4c, v4c, e_scr, n1, h0, n0,
                sc_w, sl_w, sc_r, sl_r):
    n_blocks = DEC_SEQ // Q_BLOCK
    for sub in range(Q_SUB):
        qs = _stack_group_queries(q_ref[sub * Q_BLOCK:(sub + 1) * Q_BLOCK, :])
        sc_w[sub] = _dot(qs, k4c[...])
        sl_w[sub] = _dot_nt(qs, kp[_band_rows(n1 * Q_SUB + sub), :])

    rows4 = ATTN_GROUP * Q_BLOCK
    r = lax.broadcasted_iota(jnp.int32, (rows4, Q_BLOCK), 0) % Q_BLOCK
    c = lax.broadcasted_iota(jnp.int32, (rows4, Q_BLOCK), 1)
    in_left = c >= r
    in_right = c <= r
    sk = _sink_column(sink_ref, h0, Q_BLOCK)
    for sub in range(Q_SUB):
        qb = n0 * Q_SUB + sub
        band = _band_rows(qb)
        qrows = slice(sub * Q_BLOCK, (sub + 1) * Q_BLOCK)
        s_ctx = sc_r[sub]
        s_lat = sl_r[sub]
        blocks = [
            s_ctx,
            jnp.where(jnp.logical_and(in_left, qb > 0), s_lat[:, :Q_BLOCK], NEG_INF),
            s_lat[:, Q_BLOCK:2 * Q_BLOCK],
            jnp.where(jnp.logical_and(in_right, qb < n_blocks - 1), s_lat[:, 2 * Q_BLOCK:], NEG_INF),
        ]
        m = sk
        for s in blocks:
            m = jnp.maximum(m, jnp.max(s, axis=1, keepdims=True))
        den = jnp.exp(sk - m)
        lo = 0
        for s in blocks:
            e = jnp.exp(s - m)
            den = den + jnp.sum(e, axis=1, keepdims=True)
            e_scr[sub, :, lo:lo + s.shape[1]] = e.astype(BF16)
            lo += s.shape[1]
        o = (_dot(e_scr[sub, :, :PAST_LEN], v4c[...])
             + _dot(e_scr[sub, :, PAST_LEN:], vp[band, :])) * (1.0 / den)
        acc = _gather_group_outputs(o, Q_BLOCK)
        y_ref[qrows, :] = (acc * _silu(z_ref[qrows, :].astype(F32))).astype(BF16)


def _lat_attn(sink, q, k4, v4, kc, vc, j, z):
    padded = DEC_SEQ + 2 * Q_BLOCK
    rows4 = ATTN_GROUP * Q_BLOCK
    row0 = N_PROMPT_TOK // LAT_STEP
    seq0 = N_PROMPT_TOK // DEC_SEQ
    stage1 = lambda t: _lat_block(jnp.minimum(t, LAT_BLOCKS - 1))
    stage2 = lambda t: _lat_block(jnp.maximum(t - 1, 0))

    def spec(shape, stage, index):
        return pl.BlockSpec(shape, lambda t: index(*stage(t)))

    tile = (LAT_STEP, MXU_N)
    seq = (DEC_SEQ, MXU_N)
    ctx = (None, None, None, ATTN_HEAD_DIM, PAST_LEN)
    return pl.pallas_call(
        _lat_attn_kernel,
        out_shape=jax.ShapeDtypeStruct((N_SAMPLE_TOK, ATTN_WIDTH), BF16),
        grid=(LAT_BLOCKS + 1,),
        in_specs=[pl.BlockSpec(memory_space=pltpu.SMEM),
                  spec(tile, stage1, lambda b, h, n: (row0 + b * LAT_PER_SEQ + n, h)),
                  spec(seq, stage1, lambda b, h, n: (seq0 + b, h)),
                  spec(seq, stage2, lambda b, h, n: (seq0 + b, h)),
                  spec(ctx, stage1, lambda b, h, n: (b, j, h, 0, 0)),
                  spec(ctx, stage2, lambda b, h, n: (b, j, h, 0, 0)),
                  spec(tile, stage2, lambda b, h, n: (row0 + b * LAT_PER_SEQ + n, h))],
        out_specs=spec(tile, stage2, lambda b, h, n: (b * LAT_PER_SEQ + n, h)),
        scratch_shapes=[pltpu.VMEM((padded, MXU_N), BF16), pltpu.VMEM((padded, MXU_N), BF16),
                        pltpu.VMEM((MXU_N, PAST_LEN), BF16), pltpu.VMEM((PAST_LEN, MXU_N), BF16),
                        pltpu.VMEM((Q_SUB, rows4, PAST_LEN), F32),
                        pltpu.VMEM((Q_SUB, rows4, 3 * Q_BLOCK), F32),
                        pltpu.VMEM((Q_SUB, rows4, PAST_LEN), F32),
                        pltpu.VMEM((Q_SUB, rows4, 3 * Q_BLOCK), F32),
                        pltpu.VMEM((Q_SUB, rows4, PAST_LEN + 3 * Q_BLOCK), BF16)],
        compiler_params=_params("arbitrary", vmem=VMEM_LIMIT),
        name="lat_attn",
    )(sink, q, k4, v4, kc, vc, z)


def _out_final_kernel(yp_ref, ys_ref, wf_ref, x_ref, gate_ref, fg_ref, op_ref, os_ref, w_ref):
    t = PROJ
    i = pl.program_id(0)
    _cast_weight_once(wf_ref, w_ref)
    y = _read_parts(t, (yp_ref, ys_ref))
    r = _rms(x_ref[...] + gate_ref[...] * _dot(y, w_ref[...]), fg_ref[...])

    @pl.when(i < t.n_prompt)
    def _():
        op_ref[...] = r

    @pl.when(i >= t.n_prompt)
    def _():
        os_ref[...] = r


def _out_proj_final(mixed, x, final_g):
    t = PROJ
    return pl.pallas_call(
        _out_final_kernel,
        out_shape=(jax.ShapeDtypeStruct((N_PROMPT_TOK, D_MODEL), F32),
                   jax.ShapeDtypeStruct((N_SAMPLE_TOK, D_MODEL), F32)),
        grid=(t.n_tiles,),
        in_specs=_residual_specs(t, mixed, (x,)) + [_const_spec((1, D_MODEL))],
        out_specs=(t.prompt_rows(D_MODEL), t.sample_rows(D_MODEL)),
        scratch_shapes=[pltpu.VMEM(mixed.w_out.shape[1:], BF16)],
        compiler_params=_params("arbitrary", vmem=VMEM_LIMIT),
        name="out_proj_final",
    )(*_residual_args(mixed, (x,)), final_g)


def _pool_in_kernel(*refs, n_y, n_x):
    t = PROJ
    n_res = n_y + n_x + 2
    res_refs, (g_ref, sh_ref, sc_ref, wf_ref, xo_ref, u_ref, z_ref, wo_ref, w_ref) = refs[:n_res], refs[n_res:]
    x = _residual_update(t, res_refs, n_y, n_x, wo_ref, xo_ref)
    _cast_weight_once(wf_ref, w_ref)
    h = _norm_mod(x, g_ref, sh_ref, sc_ref)

    def store_u(c, a):
        u_ref[:, c * MXU_N:(c + 1) * MXU_N] = a

    def store_z(c, a):
        z_ref[:, c * MXU_N:(c + 1) * MXU_N] = a.astype(BF16)

    _proj_chunks(h, w_ref, 0, D_MODEL, store_u)
    _proj_chunks(h, w_ref, D_MODEL, D_MODEL, store_z)


def _pool_in(mixed, x_parts, g, mod, w_all, j):
    t = PROJ
    slab = jax.ShapeDtypeStruct((N_TOK, D_MODEL), F32)
    return pl.pallas_call(
        functools.partial(_pool_in_kernel, n_y=len(mixed.y_parts), n_x=len(x_parts)),
        out_shape=(slab, slab, jax.ShapeDtypeStruct((N_TOK, D_MODEL), BF16)),
        grid=(t.n_tiles,),
        in_specs=_residual_specs(t, mixed, x_parts) + [
            _const_spec((1, D_MODEL)), t.mod(0), t.mod(1), _layer_spec(w_all, j)],
        out_specs=(t.rows(D_MODEL),) * 3,
        scratch_shapes=[pltpu.VMEM(mixed.w_out.shape[1:], BF16), pltpu.VMEM((D_MODEL, 2 * D_MODEL), BF16)],
        compiler_params=_params("arbitrary", vmem=FUSED_VMEM_LIMIT),
        name="pool_in",
    )(*_residual_args(mixed, x_parts), g, mod, mod, w_all)


def _split_bf16(a):
    hi = a.astype(BF16)
    return hi, (a - hi.astype(F32)).astype(BF16)


def _band_ones(shape, lo, hi):
    d = lax.broadcasted_iota(jnp.int32, shape, 1) - lax.broadcasted_iota(jnp.int32, shape, 0)
    return jnp.logical_and(d >= lo, d <= hi).astype(F32).astype(BF16)


def _pool_ret_kernel(u_ref, up_ref, un_ref, z_ref, wgf_ref, ps_ref, wof_ref, x_ref, gate_ref,
                     g_ref, sh_ref, sc_ref, w_ref,
                     xo_ref, q_ref, kt_ref, v_ref, zr_ref,
                     y_ref, wg_ref, wo_ref, band_ref, wkt_ref):
    t = POOL
    x = _pool_out_body(t, u_ref, up_ref, un_ref, z_ref, wgf_ref, ps_ref, wof_ref, x_ref, gate_ref,
                       y_ref, wg_ref, wo_ref, band_ref)
    xo_ref[...] = x
    _ret_in_body(t, x, g_ref, sh_ref, sc_ref, w_ref, wkt_ref, q_ref, kt_ref, v_ref, zr_ref)


def _pool_out_body(t, u_ref, up_ref, un_ref, z_ref, wgf_ref, ps_ref, wof_ref, x_ref, gate_ref,
                   y_ref, wg_ref, wo_ref, band_ref):
    i = pl.program_id(0)
    _cast_weight_once(wgf_ref, wg_ref)
    _cast_weight_once(wof_ref, wo_ref)

    @pl.when(i == 0)
    def _():
        for g, w in enumerate(POOL_WINDOWS):
            band_ref[g] = _band_ones((t.tm, t.tm), -(w // 2), w - 1 - w // 2)

    is_dec = i >= t.n_prompt
    st = t.seq_tile(i)
    has_left = jnp.logical_and(is_dec, st != 0)
    has_right = jnp.logical_and(is_dec, st != t.per_dec_seq - 1)
    seq_len = jnp.where(is_dec, DEC_SEQ, SEQ)
    h = POOL_HALO
    halo = jnp.concatenate([jnp.where(has_left, up_ref[...], 0.0),
                            jnp.where(has_right, un_ref[...], 0.0)], axis=0)
    halo_hi, halo_lo = _split_bf16(halo)
    pos = st * t.tm + lax.broadcasted_iota(jnp.int32, (t.tm, 1), 0)
    rr = lax.broadcasted_iota(jnp.int32, (2 * h, 2 * h), 0)
    cc = lax.broadcasted_iota(jnp.int32, (2 * h, 2 * h), 1)
    for g, w in enumerate(POOL_WINDOWS):
        left = w // 2
        right = w - 1 - left
        cols = slice(g * POOL_GROUP_DIM, (g + 1) * POOL_GROUP_DIM)
        u = u_ref[:, cols]
        u_hi, u_lo = _split_bf16(u)
        s = _dot(band_ref[g], u_hi) + _dot(band_ref[g], u_lo)
        top = jnp.logical_and(jnp.logical_and(rr < h, cc < h), cc - h >= rr - left)
        bot = jnp.logical_and(jnp.logical_and(rr >= h, cc >= h), cc - h <= rr - 2 * h + right)
        edge = jnp.logical_or(top, bot).astype(F32).astype(BF16)
        se = _dot(edge, halo_hi[:, cols]) + _dot(edge, halo_lo[:, cols])
        s = jnp.concatenate([s[:h] + se[:h], s[h:t.tm - h], s[t.tm - h:] + se[h:]], axis=0)
        cnt = (jnp.minimum(pos + right + 1, seq_len) - jnp.maximum(pos - left, 0)).astype(F32)
        d = (s / cnt - u).astype(BF16)
        yg = _dot(d, wg_ref[g]) * ps_ref[:, cols] * _silu(z_ref[:, cols].astype(F32))
        y_ref[:, cols] = yg.astype(BF16)
    return x_ref[...] + gate_ref[...] * _dot(y_ref[...], wo_ref[...])


def _pool_ret(u, z, wg_all, ps, wo_all, j, x, mod_pool, g, mod, w_ret):
    t = POOL
    per = t.tm // POOL_HALO
    n_halo = N_TOK // POOL_HALO
    n_in = 2 * RET_QK_WIDTH + 2 * RET_V_WIDTH
    kt_per = t.tm // RET_CHUNK
    return pl.pallas_call(
        _pool_ret_kernel,
        out_shape=(jax.ShapeDtypeStruct((N_TOK, D_MODEL), F32),
                   jax.ShapeDtypeStruct((N_TOK, RET_QK_WIDTH), BF16),
                   jax.ShapeDtypeStruct((N_TOK // RET_CHUNK, RET_QK_WIDTH, RET_CHUNK), BF16),
                   jax.ShapeDtypeStruct((N_TOK, RET_V_WIDTH), BF16),
                   jax.ShapeDtypeStruct((N_TOK, RET_V_WIDTH), BF16)),
        grid=(t.n_tiles,),
        in_specs=[
            t.rows(D_MODEL),
            pl.BlockSpec((POOL_HALO, D_MODEL), lambda i: (jnp.maximum(i * per - 1, 0), 0)),
            pl.BlockSpec((POOL_HALO, D_MODEL), lambda i: (jnp.minimum((i + 1) * per, n_halo - 1), 0)),
            t.rows(D_MODEL),
            _layer_spec(wg_all, j),
            _const_spec((1, D_MODEL)),
            _layer_spec(wo_all, j),
            t.rows(D_MODEL),
            t.mod(2),
            _const_spec((1, D_MODEL)), t.mod(0), t.mod(1), _const_spec((D_MODEL, n_in)),
        ],
        out_specs=(t.rows(D_MODEL), t.rows(RET_QK_WIDTH),
                   pl.BlockSpec((kt_per, RET_QK_WIDTH, RET_CHUNK), lambda i: (i, 0, 0)),
                   t.rows(RET_V_WIDTH), t.rows(RET_V_WIDTH)),
        scratch_shapes=[pltpu.VMEM((t.tm, D_MODEL), BF16),
                        pltpu.VMEM((len(POOL_WINDOWS), POOL_GROUP_DIM, POOL_GROUP_DIM), BF16),
                        pltpu.VMEM((D_MODEL, D_MODEL), BF16),
                        pltpu.VMEM((len(POOL_WINDOWS), t.tm, t.tm), BF16),
                        pltpu.VMEM((RET_QK_WIDTH, D_MODEL), BF16)],
        compiler_params=_params("arbitrary", vmem=FUSED_VMEM_LIMIT),
        name="pool_ret",
    )(u, u, u, z, wg_all, ps, wo_all, x, mod_pool, g, mod, mod, w_ret)


def _ret_in_body(t, x, g_ref, sh_ref, sc_ref, w_ref, wkt_ref, q_ref, kt_ref, v_ref, z_ref):
    @pl.when(pl.program_id(0) == 0)
    def _():
        for r in range(0, RET_QK_WIDTH, MXU_N):
            wk = w_ref[:, RET_QK_WIDTH + r:RET_QK_WIDTH + r + MXU_N]
            wkt_ref[r:r + MXU_N, :] = wk.astype(F32).T.astype(BF16)

    h = _norm_mod(x, g_ref, sh_ref, sc_ref)

    def store(ref):
        def f(c, a):
            ref[:, c * MXU_N:(c + 1) * MXU_N] = a.astype(BF16)
        return f

    _proj_chunks(h, w_ref, 0, RET_QK_WIDTH, store(q_ref))
    kt = (_dot_nt(wkt_ref[...], h) * RET_KEY_DIM ** -0.5).astype(BF16)
    for c in range(t.tm // RET_CHUNK):
        kt_ref[c] = kt[:, c * RET_CHUNK:(c + 1) * RET_CHUNK]
    _proj_chunks(h, w_ref, 2 * RET_QK_WIDTH, RET_V_WIDTH, store(v_ref))
    _proj_chunks(h, w_ref, 2 * RET_QK_WIDTH + RET_V_WIDTH, RET_V_WIDTH, store(z_ref))


def _pos(shape, axis):
    return lax.broadcasted_iota(jnp.int32, shape, axis).astype(F32)


def _ret_tables_kernel(lgf_ref, lgb_ref, decay_ref, row_ref, col_ref, cdec_ref):
    h = pl.program_id(0)
    lg_f = lgf_ref[h]
    lg_b = lgb_ref[h]
    c = RET_CHUNK
    diff = _pos((c, c), 0) - _pos((c, c), 1)
    fwd = jnp.where(diff >= 0, jnp.exp(jnp.maximum(diff, 0.0) * lg_f), 0.0)
    bwd = jnp.where(diff <= 0, jnp.exp(jnp.maximum(-diff, 0.0) * lg_b), 0.0)
    decay_ref[...] = fwd + bwd
    j = _pos((RET_TAB_ROWS, c), 1)
    row_ref[0] = jnp.exp((c - 1.0 - j) * lg_f)
    row_ref[1] = jnp.exp(j * lg_b)
    i = _pos((c, LANES), 0)
    col_ref[0] = jnp.exp((i + 1.0) * lg_f)
    col_ref[1] = jnp.exp((c - i) * lg_b)
    full = jnp.full((RET_TAB_ROWS, RET_VAL_DIM), float(c), F32)
    cdec_ref[0] = jnp.exp(full * lg_f)
    cdec_ref[1] = jnp.exp(full * lg_b)


def _ret_tables(lg_f, lg_b):
    smem = pl.BlockSpec(memory_space=pltpu.SMEM)
    c = RET_CHUNK
    shapes = ((c, c), (2, RET_TAB_ROWS, c), (2, c, LANES), (2, RET_TAB_ROWS, RET_VAL_DIM))
    return pl.pallas_call(
        _ret_tables_kernel,
        out_shape=tuple(jax.ShapeDtypeStruct((RET_HEADS,) + s, F32) for s in shapes),
        grid=(RET_HEADS,),
        in_specs=[smem, smem],
        out_specs=tuple(pl.BlockSpec((None,) + s, lambda h, n=len(s): (h,) + (0,) * n) for s in shapes),
        compiler_params=_params("arbitrary"),
        name="ret_tables",
    )(lg_f, lg_b)


def _group_norm_gate(o, gn, z):
    mu = jnp.mean(o, axis=-1, keepdims=True)
    var = jnp.mean(jnp.square(o - mu), axis=-1, keepdims=True)
    on = (o - mu) * lax.rsqrt(var + EPS)
    return (on * gn * _silu(z.astype(F32))).astype(BF16)


def _ret_ctx_kernel(q_ref, kt_ref, v_ref, z_ref, gn_ref, decay_ref, row_ref, y_ref, sf_ref, sb_ref):
    for h in range(RET_HEADS):
        kc = slice(h * RET_KEY_DIM, (h + 1) * RET_KEY_DIM)
        vc = slice(h * RET_VAL_DIM, (h + 1) * RET_VAL_DIM)
        kt = kt_ref[kc, :]
        v = v_ref[:, vc]
        att = (_dot(q_ref[:, kc], kt) * decay_ref[h]).astype(BF16)
        y_ref[:, vc] = _group_norm_gate(_dot(att, v), gn_ref[:, vc], z_ref[:, vc])
        ktf = kt.astype(F32)
        sf_ref[h] = _dot((ktf * row_ref[h, 0, 0:1, :]).astype(BF16), v)
        sb_ref[h] = _dot((ktf * row_ref[h, 1, 0:1, :]).astype(BF16), v)


def _ret_ctx(q, kt, v, z, gn, decay, row):
    c = RET_CHUNK
    wide = lambda w: pl.BlockSpec((SEQ, w), lambda b: (b, 0))
    st_spec = pl.BlockSpec((None, RET_HEADS, RET_KEY_DIM, RET_VAL_DIM), lambda b: (b, 0, 0, 0))
    st_shape = jax.ShapeDtypeStruct((BATCH, RET_HEADS, RET_KEY_DIM, RET_VAL_DIM), F32)
    return pl.pallas_call(
        _ret_ctx_kernel,
        out_shape=(jax.ShapeDtypeStruct((N_PROMPT_TOK, RET_V_WIDTH), BF16), st_shape, st_shape),
        grid=(BATCH,),
        in_specs=[wide(RET_QK_WIDTH), pl.BlockSpec((None, RET_QK_WIDTH, c), lambda b: (b, 0, 0)),
                  wide(RET_V_WIDTH), wide(RET_V_WIDTH), _const_spec((1, RET_V_WIDTH)),
                  _const_spec((RET_HEADS, c, c)), _const_spec((RET_HEADS, 2, RET_TAB_ROWS, c))],
        out_specs=(wide(RET_V_WIDTH), st_spec, st_spec),
        compiler_params=_params("arbitrary", vmem=VMEM_LIMIT),
        name="ret_ctx",
    )(q, kt, v, z, gn, decay, row)


def _ret_lat_kernel(q_ref, kt_ref, v_ref, z_ref, gn_ref, decay_ref, row_ref, col_ref, cdec_ref,
                    s0f_ref, s0b_ref, y_ref, sf_all, sb_all, sf_acc, sb_acc):
    c = RET_CHUNK
    n_chunks = DEC_SEQ // c
    rows_of = lambda ci: pl.ds(pl.multiple_of(ci * c, c), c)

    sf_acc[...] = s0f_ref[...]
    sb_acc[...] = s0b_ref[...]

    def scan_step(i, carry):
        cf = i
        cb = n_chunks - 1 - i
        sf_all[cf] = sf_acc[...].astype(BF16)
        sb_all[cb] = sb_acc[...].astype(BF16)
        uf = _dot((kt_ref[cf].astype(F32) * row_ref[0, 0:1, :]).astype(BF16), v_ref[rows_of(cf), :])
        ub = _dot((kt_ref[cb].astype(F32) * row_ref[1, 0:1, :]).astype(BF16), v_ref[rows_of(cb), :])
        sf_acc[...] = sf_acc[...] * cdec_ref[0, 0:1, :] + uf
        sb_acc[...] = sb_acc[...] * cdec_ref[1, 0:1, :] + ub
        return carry

    lax.fori_loop(0, n_chunks, scan_step, 0)

    def out_step(ci, carry):
        rows = rows_of(ci)
        q = q_ref[rows, :]
        qf = q.astype(F32)
        qdec_f = jnp.concatenate([col_ref[0]] * (RET_KEY_DIM // LANES), axis=1)
        qdec_b = jnp.concatenate([col_ref[1]] * (RET_KEY_DIM // LANES), axis=1)
        att = (_dot(q, kt_ref[ci]) * decay_ref[...]).astype(BF16)
        o = (_dot(att, v_ref[rows, :])
             + _dot((qf * qdec_f).astype(BF16), sf_all[ci])
             + _dot((qf * qdec_b).astype(BF16), sb_all[ci]))
        y_ref[rows, :] = _group_norm_gate(o, gn_ref[...], z_ref[rows, :])
        return carry

    lax.fori_loop(0, n_chunks, out_step, 0, unroll=2)


def _ret_lat(q, kt, v, z, gn, decay, row, col, cdec, s0f, s0b):
    c = RET_CHUNK
    n_chunks = DEC_SEQ // c
    row0 = N_PROMPT_TOK // DEC_SEQ
    qk_spec = pl.BlockSpec((DEC_SEQ, RET_KEY_DIM), lambda b, h: (row0 + b, h))
    v_spec = pl.BlockSpec((DEC_SEQ, RET_VAL_DIM), lambda b, h: (row0 + b, h))
    st_spec = pl.BlockSpec((None, None, RET_KEY_DIM, RET_VAL_DIM), lambda b, h: (b, h, 0, 0))
    tab = lambda *s: pl.BlockSpec((None,) + s, lambda b, h: (h,) + (0,) * len(s))
    states = pltpu.VMEM((n_chunks, RET_KEY_DIM, RET_VAL_DIM), BF16)
    acc = pltpu.VMEM((RET_KEY_DIM, RET_VAL_DIM), F32)
    return pl.pallas_call(
        _ret_lat_kernel,
        out_shape=jax.ShapeDtypeStruct((N_SAMPLE_TOK, RET_V_WIDTH), BF16),
        grid=(DEC_BATCH, RET_HEADS),
        in_specs=[qk_spec,
                  pl.BlockSpec((n_chunks, RET_KEY_DIM, c), lambda b, h: (row0 + b, h, 0)),
                  v_spec, v_spec, pl.BlockSpec((1, RET_VAL_DIM), lambda b, h: (0, h)),
                  tab(c, c), tab(2, RET_TAB_ROWS, c), tab(2, c, LANES), tab(2, RET_TAB_ROWS, RET_VAL_DIM),
                  st_spec, st_spec],
        out_specs=pl.BlockSpec((DEC_SEQ, RET_VAL_DIM), lambda b, h: (b, h)),
        scratch_shapes=[states, states, acc, acc],
        compiler_params=_params("arbitrary", "arbitrary", vmem=VMEM_LIMIT),
        name="ret_lat",
    )(q, kt, v, z, gn, decay, row, col, cdec, s0f, s0b)


def _rope_tables(tm):
    n_rows = DEC_SEQ // GRID_W
    rows = jnp.repeat(jnp.arange(n_rows), GRID_W).astype(F32)
    cols = jnp.tile(jnp.arange(GRID_W), n_rows).astype(F32)
    half = ATTN_HEAD_DIM // 4
    inv = ROPE_BASE ** (-jnp.arange(half, dtype=F32) / half)
    ang_r = rows[:, None] * inv[None, :]
    ang_c = cols[:, None] * inv[None, :]
    cos = jnp.concatenate([jnp.cos(ang_r), jnp.cos(ang_r), jnp.cos(ang_c), jnp.cos(ang_c)], axis=-1)
    sin = jnp.concatenate([-jnp.sin(ang_r), jnp.sin(ang_r), -jnp.sin(ang_c), jnp.sin(ang_c)], axis=-1)
    cos = jnp.concatenate([jnp.ones((tm, ATTN_HEAD_DIM), F32), cos], axis=0)
    sin = jnp.concatenate([jnp.zeros((tm, ATTN_HEAD_DIM), F32), sin], axis=0)
    return jnp.tile(cos, (1, 2)), jnp.tile(sin, (1, 2))


def kernel(x_prompt, x_sample, cache_k, cache_v, state_fwd, state_bwd, c, c_ctx, norm_g, ada_w, ada_b, attn_w_in, attn_w_out, attn_sink, pool_w_in, pool_w_grp, pool_scale, pool_w_out, ret_w_in, ret_decay_fwd, ret_decay_bwd, ret_gn_g, ret_w_out, final_g):
    x_parts = (x_prompt.reshape(N_PROMPT_TOK, D_MODEL), x_sample.reshape(N_SAMPLE_TOK, D_MODEL))
    cond = jnp.concatenate([c_ctx[None, :], c,
                            jnp.zeros((N_COND - 1 - DEC_BATCH, D_MODEL), F32)], axis=0)
    mods = _ada_table(cond.T, ada_w, ada_b).reshape(DEPTH, N_COND, 1, 3 * D_MODEL)
    cos_t, sin_t = _rope_tables(PROJ.tm)

    to_kernel = lambda a: jnp.transpose(a, (0, 1, 3, 4, 2))
    from_kernel = lambda a: jnp.transpose(
        a.reshape(a.shape[0], a.shape[1], ATTN_KV_HEADS, ATTN_HEAD_DIM, a.shape[3]), (0, 1, 4, 2, 3))
    ctx_k, ctx_v = to_kernel(cache_k), to_kernel(cache_v)

    assert DEPTH % N_MIXERS == 1, "the layer stack must end on an attention layer"
    caches = ()
    new_sf = new_sb = None
    mixed = None
    for i in range(DEPTH):
        kind, j = i % N_MIXERS, i // N_MIXERS
        g = norm_g[i].reshape(1, D_MODEL)
        mod = mods[i]
        if kind == 0:
            outs = _attn_in(mixed, x_parts, g, mod, attn_w_in, j, cos_t, sin_t, tuple(caches))
            if mixed is not None:
                x_parts, outs = (outs[0],), outs[1:]
            q, k4, v4, z, *caches = outs
            y_parts = (_ctx_attn(attn_sink[j], q, k4, v4, z),
                       _lat_attn(attn_sink[j], q, k4, v4, ctx_k, ctx_v, j, z))
            mixed = Mixed(y_parts, attn_w_out, j, mod)
        elif kind == 1:
            x, u, z = _pool_in(mixed, x_parts, g, mod, pool_w_in, j)
            x_parts, mixed, pooled = (x,), None, (u, z, j, mod)
        else:
            u, z, jp, mod_pool = pooled
            lg_f = jax.nn.log_sigmoid(ret_decay_fwd[j].astype(F32))
            lg_b = jax.nn.log_sigmoid(ret_decay_bwd[j].astype(F32))
            gn = ret_gn_g[j].reshape(1, RET_V_WIDTH)
            x, q, kt, v, z = _pool_ret(u, z, pool_w_grp, pool_scale[jp].reshape(1, D_MODEL), pool_w_out, jp,
                                       x_parts[0], mod_pool, g, mod, ret_w_in[j].astype(BF16))
            x_parts = (x,)
            decay, row, col, cdec = _ret_tables(lg_f, lg_b)
            y_ctx, new_sf, new_sb = _ret_ctx(q, kt, v, z, gn, decay, row)
            y_parts = (y_ctx, _ret_lat(q, kt, v, z, gn, decay, row, col, cdec,
                                       state_fwd[:, j], state_bwd[:, j]))
            mixed = Mixed(y_parts, ret_w_out, j, mod)
    y_prompt, y_sample = _out_proj_final(mixed, x_parts[0], final_g.reshape(1, D_MODEL))
    new_k, new_v = caches
    return (y_prompt.reshape(BATCH, SEQ, D_MODEL), y_sample.reshape(DEC_BATCH, DEC_SEQ, D_MODEL),
            from_kernel(new_k), from_kernel(new_v), new_sf[:, None], new_sb[:, None])
```

```python
import functools
from typing import NamedTuple

import jax
import jax.numpy as jnp
from jax import lax
from jax.experimental import pallas as pl
from jax.experimental.pallas import tpu as pltpu

F32 = jnp.float32
BF16 = jnp.bfloat16

D_MODEL = 1024
BATCH = 16
SEQ = 256
DEPTH = 4
DEC_BATCH = 2
DEC_SEQ = 2048
PAST_LEN = 512
GRID_W = 64
N_MIXERS = 3
ATTN_HEADS = 16
ATTN_KV_HEADS = 4
ATTN_HEAD_DIM = 64
ATTN_GROUP = 4
ATTN_WIDTH = 1024
ATTN_KV_WIDTH = 256
WINDOW = 128
ROPE_BASE = 10000.0
POOL_WINDOWS = (2, 4, 8, 16)
POOL_GROUP_DIM = 256
RET_HEADS = 4
RET_KEY_DIM = 256
RET_VAL_DIM = 512
RET_QK_WIDTH = 1024
RET_V_WIDTH = 2048
EPS = 1e-6
NEG_INF = -1e30

N_PROMPT_TOK = BATCH * SEQ
N_SAMPLE_TOK = DEC_BATCH * DEC_SEQ
N_TOK = N_PROMPT_TOK + N_SAMPLE_TOK
N_COND = 8
LANES = 128
MXU_N = 256
Q_BLOCK = 128
Q_SUB = 4
RET_CHUNK = 256
RET_TAB_ROWS = 8
POOL_HALO = 8
POOL_SUB = SEQ
VMEM_LIMIT = 48 * 1024 * 1024
FUSED_VMEM_LIMIT = 58 * 1024 * 1024


class Tiling(NamedTuple):
    tm: int

    @property
    def n_tiles(self):
        return N_TOK // self.tm

    @property
    def n_prompt(self):
        return N_PROMPT_TOK // self.tm

    @property
    def per_dec_seq(self):
        return DEC_SEQ // self.tm

    def cond(self, i):
        return jnp.where(i < self.n_prompt, 0, 1 + (i - self.n_prompt) // self.per_dec_seq)

    def seq_tile(self, i):
        return jnp.where(i < self.n_prompt, 0, (i - self.n_prompt) % self.per_dec_seq)

    def rows(self, width):
        return pl.BlockSpec((self.tm, width), lambda i: (i, 0))

    def prompt_rows(self, width):
        return pl.BlockSpec((self.tm, width), lambda i: (jnp.minimum(i, self.n_prompt - 1), 0))

    def sample_rows(self, width):
        return pl.BlockSpec((self.tm, width), lambda i: (jnp.maximum(i - self.n_prompt, 0), 0))

    def mod(self, part):
        return pl.BlockSpec((None, 1, D_MODEL), lambda i: (self.cond(i), 0, part))


PROJ = Tiling(512)
FINAL = Tiling(1024)


def _silu(z):
    return z * (1.0 / (1.0 + jnp.exp(-z)))


def _dot(a, b):
    return jnp.dot(a, b, preferred_element_type=F32)


def _dot_nt(a, b):
    return lax.dot_general(a, b, (((1,), (1,)), ((), ())), preferred_element_type=F32)


def _params(*sem, vmem=None):
    return pltpu.CompilerParams(dimension_semantics=sem, vmem_limit_bytes=vmem)


def _const_spec(shape):
    nd = len(shape)
    return pl.BlockSpec(shape, lambda *_: (0,) * nd, pipeline_mode=pl.Buffered(1))


def _part_specs(t, parts, width):
    if len(parts) == 1:
        return [t.rows(width)]
    return [t.prompt_rows(width), t.sample_rows(width)]


def _read_parts(t, refs):
    if len(refs) == 1:
        return refs[0][...]
    return jnp.where(pl.program_id(0) < t.n_prompt, refs[0][...], refs[1][...])


def _ada_kernel(cond_ref, w_ref, b_ref, o_ref):
    s = _silu(cond_ref[...])
    w = w_ref[...]
    rows = [jnp.sum(s[:, c:c + 1] * w, axis=0, keepdims=True) + b_ref[...] for c in range(1 + DEC_BATCH)]
    rows.append(jnp.zeros((N_COND - len(rows), w.shape[1]), F32))
    o_ref[...] = jnp.concatenate(rows, axis=0)


def _ada_table(cond, ada_w, ada_b):
    tn = 768
    return pl.pallas_call(
        _ada_kernel,
        out_shape=jax.ShapeDtypeStruct((DEPTH, N_COND, 3 * D_MODEL), F32),
        grid=(DEPTH, 3 * D_MODEL // tn),
        in_specs=[
            pl.BlockSpec((D_MODEL, N_COND), lambda l, n: (0, 0)),
            pl.BlockSpec((None, D_MODEL, tn), lambda l, n: (l, 0, n)),
            pl.BlockSpec((None, 1, tn), lambda l, n: (l, 0, n)),
        ],
        out_specs=pl.BlockSpec((None, N_COND, tn), lambda l, n: (l, 0, n)),
        compiler_params=_params("arbitrary", "arbitrary"),
        name="ada_table",
    )(cond, ada_w, ada_b.reshape(DEPTH, 1, 3 * D_MODEL))


def _rms(x, g):
    return x * lax.rsqrt(jnp.mean(x * x, axis=-1, keepdims=True) + EPS) * g


def _norm_mod(x, g_ref, sh_ref, sc_ref):
    return (_rms(x, g_ref[...]) * (1.0 + sc_ref[...]) + sh_ref[...]).astype(BF16)


def _proj_chunks(h, w_ref, lo, width, store):
    for c in range(width // MXU_N):
        store(c, _dot(h, w_ref[:, lo + c * MXU_N:lo + (c + 1) * MXU_N]))


def _layer_spec(w, j):
    nd = w.ndim - 1
    return pl.BlockSpec((None,) + w.shape[1:], lambda *_: (j,) + (0,) * nd, pipeline_mode=pl.Buffered(1))


def _cast_weight_once(w_ref, wb_ref):
    @pl.when(pl.program_id(0) == 0)
    def _():
        rows = wb_ref.shape[-2]
        for r in range(0, rows, MXU_N):
            wb_ref[..., r:r + MXU_N, :] = w_ref[..., r:r + MXU_N, :].astype(BF16)


class Mixed(NamedTuple):
    y_parts: tuple
    w_out: jax.Array
    j: int
    mod: jax.Array


def _residual_specs(t, mixed, x_parts):
    k = mixed.w_out.shape[1]
    return (_part_specs(t, mixed.y_parts, k) + [_layer_spec(mixed.w_out, mixed.j)]
            + _part_specs(t, x_parts, D_MODEL) + [t.mod(2)])


def _residual_args(mixed, x_parts):
    return (*mixed.y_parts, mixed.w_out, *x_parts, mixed.mod)


def _residual_update(t, refs, n_y, n_x, wo_ref, xo_ref):
    y_refs, wof_ref, x_refs, gate_ref = refs[:n_y], refs[n_y], refs[n_y + 1:n_y + 1 + n_x], refs[n_y + 1 + n_x]
    _cast_weight_once(wof_ref, wo_ref)
    x = _read_parts(t, x_refs) + gate_ref[...] * _dot(_read_parts(t, y_refs), wo_ref[...])
    xo_ref[...] = x
    return x


def _rep4(a, h):
    half = a[:, (h // 2) * LANES:(h // 2 + 1) * LANES]
    lane = lax.broadcasted_iota(jnp.int32, half.shape, 1)
    keep = (lane < ATTN_HEAD_DIM) if h % 2 == 0 else (lane >= ATTN_HEAD_DIM)
    m = jnp.where(keep, half, 0.0)
    s = m + pltpu.roll(m, ATTN_HEAD_DIM, 1)
    return jnp.concatenate([s, s], axis=1)


def _attn_in_kernel(*refs, n_y, n_x, n_alias):
    t = PROJ
    n_res = n_y + n_x + 2 if n_y else n_x
    res_refs, refs = refs[:n_res], refs[n_res:]
    (g_ref, sh_ref, sc_ref, wf_ref, cos_ref, sin_ref), refs = refs[:6], refs[6 + n_alias:]
    if n_y:
        xo_ref, q_ref, k4_ref, v4_ref, z_ref, kc_ref, vc_ref, wo_ref, w_ref = refs
        x = _residual_update(t, res_refs, n_y, n_x, wo_ref, xo_ref)
    else:
        q_ref, k4_ref, v4_ref, z_ref, kc_ref, vc_ref, w_ref = refs
        x = _read_parts(t, res_refs)
    i = pl.program_id(0)
    _cast_weight_once(wf_ref, w_ref)
    h = _norm_mod(x, g_ref, sh_ref, sc_ref)
    cos = cos_ref[...]
    sin = sin_ref[...]
    lane = lax.broadcasted_iota(jnp.int32, (t.tm, LANES), 1)
    first = (lane % (ATTN_HEAD_DIM // 2)) < ATTN_HEAD_DIM // 4

    def rope(a):
        rot = jnp.where(first, pltpu.roll(a, LANES - ATTN_HEAD_DIM // 4, 1),
                        pltpu.roll(a, ATTN_HEAD_DIM // 4, 1))
        return a * cos + rot * sin

    def rope_wide(a):
        return jnp.concatenate(
            [rope(a[:, s * LANES:(s + 1) * LANES]) for s in range(MXU_N // LANES)], axis=1)

    scale = ATTN_HEAD_DIM ** -0.5

    def store_q(c, a):
        q_ref[:, c * MXU_N:(c + 1) * MXU_N] = (rope_wide(a) * scale).astype(BF16)

    def store_z(c, a):
        z_ref[:, c * MXU_N:(c + 1) * MXU_N] = a.astype(BF16)

    _proj_chunks(h, w_ref, 0, ATTN_WIDTH, store_q)
    k = rope_wide(_dot(h, w_ref[:, ATTN_WIDTH:ATTN_WIDTH + ATTN_KV_WIDTH]))
    v = _dot(h, w_ref[:, ATTN_WIDTH + ATTN_KV_WIDTH:ATTN_WIDTH + 2 * ATTN_KV_WIDTH])
    for hh in range(ATTN_KV_HEADS):
        k4_ref[:, hh * MXU_N:(hh + 1) * MXU_N] = _rep4(k, hh).astype(BF16)
        v4_ref[:, hh * MXU_N:(hh + 1) * MXU_N] = _rep4(v, hh).astype(BF16)
    _proj_chunks(h, w_ref, ATTN_WIDTH + 2 * ATTN_KV_WIDTH, ATTN_WIDTH, store_z)

    @pl.when(i < t.n_prompt)
    def _():
        for s in range(t.tm // SEQ):
            kc_ref[s] = k[s * SEQ:(s + 1) * SEQ, :].T
            vc_ref[s] = v[s * SEQ:(s + 1) * SEQ, :].T


def _attn_in(mixed, x_parts, g, mod, w_all, j, cos_t, sin_t, caches):
    t = PROJ
    n_in = 2 * ATTN_WIDTH + 2 * ATTN_KV_WIDTH
    per = t.tm // SEQ
    rope_spec = pl.BlockSpec(
        (t.tm, LANES), lambda i: (jnp.where(i < t.n_prompt, 0, 1 + t.seq_tile(i)), 0))
    wide = jax.ShapeDtypeStruct((N_TOK, ATTN_WIDTH), BF16)
    n_attn = w_all.shape[0]
    cache = jax.ShapeDtypeStruct((BATCH, n_attn, ATTN_KV_WIDTH, SEQ), F32)
    cache_spec = pl.BlockSpec((per, None, ATTN_KV_WIDTH, SEQ),
                              lambda i: (jnp.minimum(i, t.n_prompt - 1), j, 0, 0))
    n_x = len(x_parts)
    if mixed is None:
        n_y, res_specs, res_args, res_out, res_ospecs, res_scratch = 0, _part_specs(t, x_parts, D_MODEL), x_parts, (), (), []
    else:
        n_y = len(mixed.y_parts)
        res_specs, res_args = _residual_specs(t, mixed, x_parts), _residual_args(mixed, x_parts)
        res_out = (jax.ShapeDtypeStruct((N_TOK, D_MODEL), F32),)
        res_ospecs = (t.rows(D_MODEL),)
        res_scratch = [pltpu.VMEM(mixed.w_out.shape[1:], BF16)]
    n_front = len(res_specs) + 6
    return pl.pallas_call(
        functools.partial(_attn_in_kernel, n_y=n_y, n_x=n_x, n_alias=len(caches)),
        out_shape=res_out + (wide, wide, wide, wide, cache, cache),
        grid=(t.n_tiles,),
        in_specs=res_specs + [
            _const_spec((1, D_MODEL)), t.mod(0), t.mod(1), _layer_spec(w_all, j),
            rope_spec, rope_spec] + [pl.BlockSpec(memory_space=pl.ANY)] * len(caches),
        out_specs=res_ospecs + (t.rows(ATTN_WIDTH),) * 4 + (cache_spec,) * 2,
        scratch_shapes=res_scratch + [pltpu.VMEM((D_MODEL, n_in), BF16)],
        input_output_aliases={n_front + c: len(res_out) + 4 + c for c in range(len(caches))},
        compiler_params=_params("arbitrary", vmem=FUSED_VMEM_LIMIT),
        name="attn_in",
    )(*res_args, g, mod, mod, w_all, cos_t, sin_t, *caches)


def _stack_group_queries(q):
    qf = q.astype(F32)
    chunk = lax.broadcasted_iota(jnp.int32, qf.shape, 1) // ATTN_HEAD_DIM
    return jnp.concatenate(
        [jnp.where(chunk == g, qf, 0.0) for g in range(ATTN_GROUP)], axis=0).astype(BF16)


def _gather_group_outputs(o, rows):
    chunk = lax.broadcasted_iota(jnp.int32, (rows, MXU_N), 1) // ATTN_HEAD_DIM
    acc = jnp.zeros((rows, MXU_N), F32)
    for g in range(ATTN_GROUP):
        acc = acc + jnp.where(chunk == g, o[g * rows:(g + 1) * rows], 0.0)
    return acc


def _sink_column(sink_ref, h, rows):
    grp = lax.broadcasted_iota(jnp.int32, (ATTN_GROUP * rows, 1), 0) // rows
    col = jnp.zeros((ATTN_GROUP * rows, 1), F32)
    for g in range(ATTN_GROUP):
        col = jnp.where(grp == g, sink_ref[h * ATTN_GROUP + g], col)
    return col


def _chunk_rows(dtype):
    chunk = lax.broadcasted_iota(jnp.int32, (1, MXU_N), 1) // ATTN_HEAD_DIM
    return [(chunk == g).astype(F32).astype(dtype) for g in range(ATTN_GROUP)]


def _block_diag_rows(x4):
    return jnp.concatenate([x4 * m for m in _chunk_rows(x4.dtype)], axis=0)


def _ctx_attn_kernel(sink_ref, q_ref, k4_ref, v4_ref, z_ref, y_ref):
    chunk = lax.broadcasted_iota(jnp.int32, (SEQ, MXU_N), 1) // ATTN_HEAD_DIM
    for h in range(ATTN_KV_HEADS):
        cols = slice(h * MXU_N, (h + 1) * MXU_N)
        s = _dot_nt(q_ref[:, cols], _block_diag_rows(k4_ref[:, cols]))
        inv = jnp.zeros((SEQ, MXU_N), F32)
        probs = []
        for g in range(ATTN_GROUP):
            sg = s[:, g * SEQ:(g + 1) * SEQ]
            sk = sink_ref[h * ATTN_GROUP + g]
            m = jnp.maximum(jnp.max(sg, axis=1, keepdims=True), sk)
            e = jnp.exp(sg - m)
            den = jnp.sum(e, axis=1, keepdims=True) + jnp.exp(sk - m)
            probs.append(e.astype(BF16))
            inv = jnp.where(chunk == g, 1.0 / den, inv)
        o = _dot(jnp.concatenate(probs, axis=1), _block_diag_rows(v4_ref[:, cols]))
        y_ref[:, cols] = (o * inv * _silu(z_ref[:, cols].astype(F32))).astype(BF16)


def _ctx_attn(sink, q, k4, v4, z):
    spec = pl.BlockSpec((SEQ, ATTN_WIDTH), lambda b: (b, 0))
    return pl.pallas_call(
        _ctx_attn_kernel,
        out_shape=jax.ShapeDtypeStruct((N_PROMPT_TOK, ATTN_WIDTH), BF16),
        grid=(BATCH,),
        in_specs=[pl.BlockSpec(memory_space=pltpu.SMEM), spec, spec, spec, spec],
        out_specs=spec,
        compiler_params=_params("arbitrary"),
        name="ctx_attn",
    )(sink, q, k4, v4, z)


LAT_STEP = Q_SUB * Q_BLOCK
LAT_PER_SEQ = DEC_SEQ // LAT_STEP
LAT_BLOCKS = DEC_BATCH * ATTN_KV_HEADS * LAT_PER_SEQ


def _lat_block(blk):
    return (blk // (ATTN_KV_HEADS * LAT_PER_SEQ), (blk // LAT_PER_SEQ) % ATTN_KV_HEADS, blk % LAT_PER_SEQ)


def _band_rows(qb):
    return pl.ds(pl.multiple_of(qb * Q_BLOCK, Q_BLOCK), 3 * Q_BLOCK)


def _pad_sequence(dst, src):
    zeros = jnp.zeros((Q_BLOCK, MXU_N), BF16)
    dst[0:Q_BLOCK, :] = zeros
    dst[Q_BLOCK:Q_BLOCK + DEC_SEQ, :] = src[...]
    dst[Q_BLOCK + DEC_SEQ:, :] = zeros


def _lat_attn_kernel(sink_ref, q_ref, k4_ref, v4_ref, kc_ref, vc_ref, z_ref, y_ref,
                     kp, vp, k4c, v4c, sc_a, sl_a, sc_b, sl_b, e_scr):
    t = pl.program_id(0)
    _, _, n1 = _lat_block(jnp.minimum(t, LAT_BLOCKS - 1))
    _, h0, n0 = _lat_block(jnp.maximum(t - 1, 0))
    n_blocks = DEC_SEQ // Q_BLOCK

    @pl.when(t == 0)
    def _():
        sc_b[...] = jnp.zeros(sc_b.shape, F32)
        sl_b[...] = jnp.zeros(sl_b.shape, F32)

    @pl.when(n1 == 0)
    def _():
        _pad_sequence(kp, k4_ref)
        k4c[...] = jnp.concatenate([kc_ref[...]] * ATTN_GROUP, axis=0).astype(BF16)

    @pl.when(n0 == 0)
    def _():
        _pad_sequence(vp, v4_ref)
        v4c[...] = jnp.concatenate([vc_ref[...]] * ATTN_GROUP, axis=0).T.astype(BF16)

    @pl.when(t % 2 == 0)
    def _():
        _lat_stages(sink_ref, q_ref, z_ref, y_ref, kp, vp, k4c, v4c, e_scr, n1, h0, n0,
                    sc_a, sl_a, sc_b, sl_b)

    @pl.when(t % 2 == 1)
    def _():
        _lat_stages(sink_ref, q_ref, z_ref, y_ref, kp, vp, k4c, v4c, e_scr, n1, h0, n0,
                    sc_b, sl_b, sc_a, sl_a)


def _lat_stages(sink_ref, q_ref, z_ref, y_ref, kp, vp, k4c, v4c, e_scr, n1, h0, n0,
                sc_w, sl_w, sc_r, sl_r):
    n_blocks = DEC_SEQ // Q_BLOCK
    for sub in range(Q_SUB):
        qs = _stack_group_queries(q_ref[sub * Q_BLOCK:(sub + 1) * Q_BLOCK, :])
        sc_w[sub] = _dot(qs, k4c[...])
        sl_w[sub] = _dot_nt(qs, kp[_band_rows(n1 * Q_SUB + sub), :])

    rows4 = ATTN_GROUP * Q_BLOCK
    r = lax.broadcasted_iota(jnp.int32, (rows4, Q_BLOCK), 0) % Q_BLOCK
    c = lax.broadcasted_iota(jnp.int32, (rows4, Q_BLOCK), 1)
    in_left = c >= r
    in_right = c <= r
    sk = _sink_column(sink_ref, h0, Q_BLOCK)
    for sub in range(Q_SUB):
        qb = n0 * Q_SUB + sub
        band = _band_rows(qb)
        qrows = slice(sub * Q_BLOCK, (sub + 1) * Q_BLOCK)
        s_ctx = sc_r[sub]
        s_lat = sl_r[sub]
        blocks = [
            s_ctx,
            jnp.where(jnp.logical_and(in_left, qb > 0), s_lat[:, :Q_BLOCK], NEG_INF),
            s_lat[:, Q_BLOCK:2 * Q_BLOCK],
            jnp.where(jnp.logical_and(in_right, qb < n_blocks - 1), s_lat[:, 2 * Q_BLOCK:], NEG_INF),
        ]
        m = sk
        for s in blocks:
            m = jnp.maximum(m, jnp.max(s, axis=1, keepdims=True))
        den = jnp.exp(sk - m)
        lo = 0
        for s in blocks:
            e = jnp.exp(s - m)
            den = den + jnp.sum(e, axis=1, keepdims=True)
            e_scr[sub, :, lo:lo + s.shape[1]] = e.astype(BF16)
            lo += s.shape[1]
        o = (_dot(e_scr[sub, :, :PAST_LEN], v4c[...])
             + _dot(e_scr[sub, :, PAST_LEN:], vp[band, :])) * (1.0 / den)
        acc = _gather_group_outputs(o, Q_BLOCK)
        y_ref[qrows, :] = (acc * _silu(z_ref[qrows, :].astype(F32))).astype(BF16)


def _lat_attn(sink, q, k4, v4, kc, vc, j, z):
    padded = DEC_SEQ + 2 * Q_BLOCK
    rows4 = ATTN_GROUP * Q_BLOCK
    row0 = N_PROMPT_TOK // LAT_STEP
    seq0 = N_PROMPT_TOK // DEC_SEQ
    stage1 = lambda t: _lat_block(jnp.minimum(t, LAT_BLOCKS - 1))
    stage2 = lambda t: _lat_block(jnp.maximum(t - 1, 0))

    def spec(shape, stage, index):
        return pl.BlockSpec(shape, lambda t: index(*stage(t)))

    tile = (LAT_STEP, MXU_N)
    seq = (DEC_SEQ, MXU_N)
    ctx = (None, None, None, ATTN_HEAD_DIM, PAST_LEN)
    return pl.pallas_call(
        _lat_attn_kernel,
        out_shape=jax.ShapeDtypeStruct((N_SAMPLE_TOK, ATTN_WIDTH), BF16),
        grid=(LAT_BLOCKS + 1,),
        in_specs=[pl.BlockSpec(memory_space=pltpu.SMEM),
                  spec(tile, stage1, lambda b, h, n: (row0 + b * LAT_PER_SEQ + n, h)),
                  spec(seq, stage1, lambda b, h, n: (seq0 + b, h)),
                  spec(seq, stage2, lambda b, h, n: (seq0 + b, h)),
                  spec(ctx, stage1, lambda b, h, n: (b, j, h, 0, 0)),
                  spec(ctx, stage2, lambda b, h, n: (b, j, h, 0, 0)),
                  spec(tile, stage2, lambda b, h, n: (row0 + b * LAT_PER_SEQ + n, h))],
        out_specs=spec(tile, stage2, lambda b, h, n: (b * LAT_PER_SEQ + n, h)),
        scratch_shapes=[pltpu.VMEM((padded, MXU_N), BF16), pltpu.VMEM((padded, MXU_N), BF16),
                        pltpu.VMEM((MXU_N, PAST_LEN), BF16), pltpu.VMEM((PAST_LEN, MXU_N), BF16),
                        pltpu.VMEM((Q_SUB, rows4, PAST_LEN), F32),
                        pltpu.VMEM((Q_SUB, rows4, 3 * Q_BLOCK), F32),
                        pltpu.VMEM((Q_SUB, rows4, PAST_LEN), F32),
                        pltpu.VMEM((Q_SUB, rows4, 3 * Q_BLOCK), F32),
                        pltpu.VMEM((Q_SUB, rows4, PAST_LEN + 3 * Q_BLOCK), BF16)],
        compiler_params=_params("arbitrary", vmem=VMEM_LIMIT),
        name="lat_attn",
    )(sink, q, k4, v4, kc, vc, z)


def _out_final_kernel(yp_ref, ys_ref, wf_ref, x_ref, gate_ref, fg_ref, op_ref, os_ref, w_ref):
    t = FINAL
    i = pl.program_id(0)
    _cast_weight_once(wf_ref, w_ref)
    y = _read_parts(t, (yp_ref, ys_ref))
    r = _rms(x_ref[...] + gate_ref[...] * _dot(y, w_ref[...]), fg_ref[...])

    @pl.when(i < t.n_prompt)
    def _():
        op_ref[...] = r

    @pl.when(i >= t.n_prompt)
    def _():
        os_ref[...] = r


def _out_proj_final(mixed, x, final_g):
    t = FINAL
    return pl.pallas_call(
        _out_final_kernel,
        out_shape=(jax.ShapeDtypeStruct((N_PROMPT_TOK, D_MODEL), F32),
                   jax.ShapeDtypeStruct((N_SAMPLE_TOK, D_MODEL), F32)),
        grid=(t.n_tiles,),
        in_specs=_residual_specs(t, mixed, (x,)) + [_const_spec((1, D_MODEL))],
        out_specs=(t.prompt_rows(D_MODEL), t.sample_rows(D_MODEL)),
        scratch_shapes=[pltpu.VMEM(mixed.w_out.shape[1:], BF16)],
        compiler_params=_params("arbitrary", vmem=VMEM_LIMIT),
        name="out_proj_final",
    )(*_residual_args(mixed, (x,)), final_g)


def _pool_in_kernel(*refs, n_y, n_x):
    t = PROJ
    n_res = n_y + n_x + 2
    res_refs, (g_ref, sh_ref, sc_ref, wf_ref, xo_ref, u_ref, z_ref, wo_ref, w_ref) = refs[:n_res], refs[n_res:]
    x = _residual_update(t, res_refs, n_y, n_x, wo_ref, xo_ref)
    _cast_weight_once(wf_ref, w_ref)
    h = _norm_mod(x, g_ref, sh_ref, sc_ref)

    def store_u(c, a):
        u_ref[:, c * MXU_N:(c + 1) * MXU_N] = a

    def store_z(c, a):
        z_ref[:, c * MXU_N:(c + 1) * MXU_N] = a.astype(BF16)

    _proj_chunks(h, w_ref, 0, D_MODEL, store_u)
    _proj_chunks(h, w_ref, D_MODEL, D_MODEL, store_z)


def _pool_in(mixed, x_parts, g, mod, w_all, j):
    t = PROJ
    slab = jax.ShapeDtypeStruct((N_TOK, D_MODEL), F32)
    return pl.pallas_call(
        functools.partial(_pool_in_kernel, n_y=len(mixed.y_parts), n_x=len(x_parts)),
        out_shape=(slab, slab, jax.ShapeDtypeStruct((N_TOK, D_MODEL), BF16)),
        grid=(t.n_tiles,),
        in_specs=_residual_specs(t, mixed, x_parts) + [
            _const_spec((1, D_MODEL)), t.mod(0), t.mod(1), _layer_spec(w_all, j)],
        out_specs=(t.rows(D_MODEL),) * 3,
        scratch_shapes=[pltpu.VMEM(mixed.w_out.shape[1:], BF16), pltpu.VMEM((D_MODEL, 2 * D_MODEL), BF16)],
        compiler_params=_params("arbitrary", vmem=FUSED_VMEM_LIMIT),
        name="pool_in",
    )(*_residual_args(mixed, x_parts), g, mod, mod, w_all)


def _split_bf16(a):
    hi = a.astype(BF16)
    return hi, (a - hi.astype(F32)).astype(BF16)


def _band_ones(shape, lo, hi):
    d = lax.broadcasted_iota(jnp.int32, shape, 1) - lax.broadcasted_iota(jnp.int32, shape, 0)
    return jnp.logical_and(d >= lo, d <= hi).astype(F32).astype(BF16)


def _pool_ret_kernel(u_ref, up_ref, un_ref, z_ref, wgf_ref, ps_ref, wof_ref, x_ref, gate_ref,
                     g_ref, sh_ref, sc_ref, w_ref,
                     xo_ref, q_ref, kt_ref, v_ref, zr_ref,
                     y_ref, wg_ref, wo_ref, band_ref, wkt_ref):
    t = PROJ
    i = pl.program_id(0)
    _cast_weight_once(wgf_ref, wg_ref)
    _cast_weight_once(wof_ref, wo_ref)

    @pl.when(i == 0)
    def _():
        for g, w in enumerate(POOL_WINDOWS):
            band_ref[g] = _band_ones((POOL_SUB, POOL_SUB), -(w // 2), w - 1 - w // 2)
        for r in range(0, RET_QK_WIDTH, MXU_N):
            wk = w_ref[:, RET_QK_WIDTH + r:RET_QK_WIDTH + r + MXU_N]
            wkt_ref[r:r + MXU_N, :] = wk.astype(F32).T.astype(BF16)

    is_dec = i >= t.n_prompt
    st = t.seq_tile(i)
    seq_len = jnp.where(is_dec, DEC_SEQ, SEQ)
    h = POOL_HALO
    n_sub = t.tm // POOL_SUB
    def pool_steps(sub):
        rows = slice(sub * POOL_SUB, (sub + 1) * POOL_SUB)
        if sub == 0:
            before = jnp.where(jnp.logical_and(is_dec, st != 0), up_ref[...], 0.0)
        else:
            before = jnp.where(is_dec, u_ref[sub * POOL_SUB - h:sub * POOL_SUB, :], 0.0)
        if sub == n_sub - 1:
            after = jnp.where(jnp.logical_and(is_dec, st != t.per_dec_seq - 1), un_ref[...], 0.0)
        else:
            after = jnp.where(is_dec, u_ref[(sub + 1) * POOL_SUB:(sub + 1) * POOL_SUB + h, :], 0.0)
        pos0 = jnp.where(is_dec, st * t.tm + sub * POOL_SUB, 0)
        return _pool_core_steps(u_ref.at[rows], before, after, pos0, seq_len, z_ref.at[rows], ps_ref,
                                wg_ref, band_ref, y_ref.at[rows])

    for step in pool_steps(0):
        step()
    for sub in range(n_sub):
        rows = slice(sub * POOL_SUB, (sub + 1) * POOL_SUB)
        x = x_ref[rows, :] + gate_ref[...] * _dot(y_ref[rows, :], wo_ref[...])
        xo_ref[rows, :] = x
        proj = _ret_in_steps(x, g_ref, sh_ref, sc_ref, w_ref, wkt_ref, q_ref.at[rows],
                             kt_ref.at[pl.ds(sub * POOL_SUB // RET_CHUNK, POOL_SUB // RET_CHUNK)],
                             v_ref.at[rows], zr_ref.at[rows])
        _run_interleaved(proj, pool_steps(sub + 1) if sub + 1 < n_sub else [])


def _pool_core_steps(u_ref, before, after, pos0, seq_len, z_ref, ps_ref, wg_ref, band_ref, y_ref):
    n = u_ref.shape[0]
    h = POOL_HALO
    halo_hi, halo_lo = _split_bf16(jnp.concatenate([before, after], axis=0))
    pos = pos0 + lax.broadcasted_iota(jnp.int32, (n, 1), 0)
    rr = lax.broadcasted_iota(jnp.int32, (2 * h, 2 * h), 0)
    cc = lax.broadcasted_iota(jnp.int32, (2 * h, 2 * h), 1)

    def group(g, w):
        def step():
            left = w // 2
            right = w - 1 - left
            cols = slice(g * POOL_GROUP_DIM, (g + 1) * POOL_GROUP_DIM)
            u = u_ref[:, cols]
            u_hi, u_lo = _split_bf16(u)
            s = _dot(band_ref[g], u_hi) + _dot(band_ref[g], u_lo)
            top = jnp.logical_and(jnp.logical_and(rr < h, cc < h), cc - h >= rr - left)
            bot = jnp.logical_and(jnp.logical_and(rr >= h, cc >= h), cc - h <= rr - 2 * h + right)
            edge = jnp.logical_or(top, bot).astype(F32).astype(BF16)
            se = _dot(edge, halo_hi[:, cols]) + _dot(edge, halo_lo[:, cols])
            s = jnp.concatenate([s[:h] + se[:h], s[h:n - h], s[n - h:] + se[h:]], axis=0)
            cnt = (jnp.minimum(pos + right + 1, seq_len) - jnp.maximum(pos - left, 0)).astype(F32)
            d = (s / cnt - u).astype(BF16)
            yg = _dot(d, wg_ref[g]) * ps_ref[:, cols] * _silu(z_ref[:, cols].astype(F32))
            y_ref[:, cols] = yg.astype(BF16)
        return step

    return [group(g, w) for g, w in enumerate(POOL_WINDOWS)]


def _pool_ret(u, z, wg_all, ps, wo_all, j, x, mod_pool, g, mod, w_ret):
    t = PROJ
    per = t.tm // POOL_HALO
    n_halo = N_TOK // POOL_HALO
    n_in = 2 * RET_QK_WIDTH + 2 * RET_V_WIDTH
    kt_per = t.tm // RET_CHUNK
    return pl.pallas_call(
        _pool_ret_kernel,
        out_shape=(jax.ShapeDtypeStruct((N_TOK, D_MODEL), F32),
                   jax.ShapeDtypeStruct((N_TOK, RET_QK_WIDTH), BF16),
                   jax.ShapeDtypeStruct((N_TOK // RET_CHUNK, RET_QK_WIDTH, RET_CHUNK), BF16),
                   jax.ShapeDtypeStruct((N_TOK, RET_V_WIDTH), BF16),
                   jax.ShapeDtypeStruct((N_TOK, RET_V_WIDTH), BF16)),
        grid=(t.n_tiles,),
        in_specs=[
            t.rows(D_MODEL),
            pl.BlockSpec((POOL_HALO, D_MODEL), lambda i: (jnp.maximum(i * per - 1, 0), 0)),
            pl.BlockSpec((POOL_HALO, D_MODEL), lambda i: (jnp.minimum((i + 1) * per, n_halo - 1), 0)),
            t.rows(D_MODEL),
            _layer_spec(wg_all, j),
            _const_spec((1, D_MODEL)),
            _layer_spec(wo_all, j),
            t.rows(D_MODEL),
            t.mod(2),
            _const_spec((1, D_MODEL)), t.mod(0), t.mod(1), _const_spec((D_MODEL, n_in)),
        ],
        out_specs=(t.rows(D_MODEL), t.rows(RET_QK_WIDTH),
                   pl.BlockSpec((kt_per, RET_QK_WIDTH, RET_CHUNK), lambda i: (i, 0, 0)),
                   t.rows(RET_V_WIDTH), t.rows(RET_V_WIDTH)),
        scratch_shapes=[pltpu.VMEM((t.tm, D_MODEL), BF16),
                        pltpu.VMEM((len(POOL_WINDOWS), POOL_GROUP_DIM, POOL_GROUP_DIM), BF16),
                        pltpu.VMEM((D_MODEL, D_MODEL), BF16),
                        pltpu.VMEM((len(POOL_WINDOWS), POOL_SUB, POOL_SUB), BF16),
                        pltpu.VMEM((RET_QK_WIDTH, D_MODEL), BF16)],
        compiler_params=_params("arbitrary", vmem=FUSED_VMEM_LIMIT),
        name="pool_ret",
    )(u, u, u, z, wg_all, ps, wo_all, x, mod_pool, g, mod, mod, w_ret)


def _ret_in_steps(x, g_ref, sh_ref, sc_ref, w_ref, wkt_ref, q_ref, kt_ref, v_ref, z_ref):
    h = _norm_mod(x, g_ref, sh_ref, sc_ref)

    def chunk(ref, lo, c):
        def step():
            cols = slice(c * MXU_N, (c + 1) * MXU_N)
            ref[:, cols] = _dot(h, w_ref[:, lo + c * MXU_N:lo + (c + 1) * MXU_N]).astype(BF16)
        return step

    def key_chunk(c):
        def step():
            rows = slice(c * MXU_N, (c + 1) * MXU_N)
            kt = (_dot_nt(wkt_ref[rows, :], h) * RET_KEY_DIM ** -0.5).astype(BF16)
            for cc in range(x.shape[0] // RET_CHUNK):
                kt_ref[cc, rows, :] = kt[:, cc * RET_CHUNK:(cc + 1) * RET_CHUNK]
        return step

    steps = [chunk(q_ref, 0, c) for c in range(RET_QK_WIDTH // MXU_N)]
    steps += [key_chunk(c) for c in range(RET_QK_WIDTH // MXU_N)]
    steps += [chunk(v_ref, 2 * RET_QK_WIDTH, c) for c in range(RET_V_WIDTH // MXU_N)]
    steps += [chunk(z_ref, 2 * RET_QK_WIDTH + RET_V_WIDTH, c) for c in range(RET_V_WIDTH // MXU_N)]
    return steps


def _run_interleaved(main, side):
    done = 0
    for k, step in enumerate(main):
        step()
        due = (k + 1) * len(side) // len(main)
        for s in side[done:due]:
            s()
        done = due


def _pos(shape, axis):
    return lax.broadcasted_iota(jnp.int32, shape, axis).astype(F32)


def _ret_tables_kernel(lgf_ref, lgb_ref, decay_ref, row_ref, col_ref, cdec_ref):
    h = pl.program_id(0)
    lg_f = lgf_ref[h]
    lg_b = lgb_ref[h]
    c = RET_CHUNK
    diff = _pos((c, c), 0) - _pos((c, c), 1)
    fwd = jnp.where(diff >= 0, jnp.exp(jnp.maximum(diff, 0.0) * lg_f), 0.0)
    bwd = jnp.where(diff <= 0, jnp.exp(jnp.maximum(-diff, 0.0) * lg_b), 0.0)
    decay_ref[...] = fwd + bwd
    j = _pos((RET_TAB_ROWS, c), 1)
    row_ref[0] = jnp.exp((c - 1.0 - j) * lg_f)
    row_ref[1] = jnp.exp(j * lg_b)
    i = _pos((c, LANES), 0)
    col_ref[0] = jnp.exp((i + 1.0) * lg_f)
    col_ref[1] = jnp.exp((c - i) * lg_b)
    full = jnp.full((RET_TAB_ROWS, RET_VAL_DIM), float(c), F32)
    cdec_ref[0] = jnp.exp(full * lg_f)
    cdec_ref[1] = jnp.exp(full * lg_b)


def _ret_tables(lg_f, lg_b):
    smem = pl.BlockSpec(memory_space=pltpu.SMEM)
    c = RET_CHUNK
    shapes = ((c, c), (2, RET_TAB_ROWS, c), (2, c, LANES), (2, RET_TAB_ROWS, RET_VAL_DIM))
    return pl.pallas_call(
        _ret_tables_kernel,
        out_shape=tuple(jax.ShapeDtypeStruct((RET_HEADS,) + s, F32) for s in shapes),
        grid=(RET_HEADS,),
        in_specs=[smem, smem],
        out_specs=tuple(pl.BlockSpec((None,) + s, lambda h, n=len(s): (h,) + (0,) * n) for s in shapes),
        compiler_params=_params("arbitrary"),
        name="ret_tables",
    )(lg_f, lg_b)


def _group_norm_gate(o, gn, z):
    mu = jnp.mean(o, axis=-1, keepdims=True)
    var = jnp.mean(jnp.square(o - mu), axis=-1, keepdims=True)
    on = (o - mu) * lax.rsqrt(var + EPS)
    return (on * gn * _silu(z.astype(F32))).astype(BF16)


def _ret_ctx_kernel(q_ref, kt_ref, v_ref, z_ref, gn_ref, decay_ref, row_ref, y_ref, sf_ref, sb_ref):
    for h in range(RET_HEADS):
        kc = slice(h * RET_KEY_DIM, (h + 1) * RET_KEY_DIM)
        vc = slice(h * RET_VAL_DIM, (h + 1) * RET_VAL_DIM)
        kt = kt_ref[kc, :]
        v = v_ref[:, vc]
        att = (_dot(q_ref[:, kc], kt) * decay_ref[h]).astype(BF16)
        y_ref[:, vc] = _group_norm_gate(_dot(att, v), gn_ref[:, vc], z_ref[:, vc])
        ktf = kt.astype(F32)
        sf_ref[h] = _dot((ktf * row_ref[h, 0, 0:1, :]).astype(BF16), v)
        sb_ref[h] = _dot((ktf * row_ref[h, 1, 0:1, :]).astype(BF16), v)


def _ret_ctx(q, kt, v, z, gn, decay, row):
    c = RET_CHUNK
    wide = lambda w: pl.BlockSpec((SEQ, w), lambda b: (b, 0))
    st_spec = pl.BlockSpec((None, RET_HEADS, RET_KEY_DIM, RET_VAL_DIM), lambda b: (b, 0, 0, 0))
    st_shape = jax.ShapeDtypeStruct((BATCH, RET_HEADS, RET_KEY_DIM, RET_VAL_DIM), F32)
    return pl.pallas_call(
        _ret_ctx_kernel,
        out_shape=(jax.ShapeDtypeStruct((N_PROMPT_TOK, RET_V_WIDTH), BF16), st_shape, st_shape),
        grid=(BATCH,),
        in_specs=[wide(RET_QK_WIDTH), pl.BlockSpec((None, RET_QK_WIDTH, c), lambda b: (b, 0, 0)),
                  wide(RET_V_WIDTH), wide(RET_V_WIDTH), _const_spec((1, RET_V_WIDTH)),
                  _const_spec((RET_HEADS, c, c)), _const_spec((RET_HEADS, 2, RET_TAB_ROWS, c))],
        out_specs=(wide(RET_V_WIDTH), st_spec, st_spec),
        compiler_params=_params("arbitrary", vmem=VMEM_LIMIT),
        name="ret_ctx",
    )(q, kt, v, z, gn, decay, row)


def _ret_lat_kernel(q_ref, kt_ref, v_ref, z_ref, gn_ref, decay_ref, row_ref, col_ref, cdec_ref,
                    s0f_ref, s0b_ref, y_ref, sf_all, sb_all, sf_acc, sb_acc):
    c = RET_CHUNK
    n_chunks = DEC_SEQ // c
    rows_of = lambda ci: pl.ds(pl.multiple_of(ci * c, c), c)

    sf_acc[...] = s0f_ref[...]
    sb_acc[...] = s0b_ref[...]

    def scan_step(i, carry):
        cf = i
        cb = n_chunks - 1 - i
        sf_all[cf] = sf_acc[...].astype(BF16)
        sb_all[cb] = sb_acc[...].astype(BF16)
        uf = _dot((kt_ref[cf].astype(F32) * row_ref[0, 0:1, :]).astype(BF16), v_ref[rows_of(cf), :])
        ub = _dot((kt_ref[cb].astype(F32) * row_ref[1, 0:1, :]).astype(BF16), v_ref[rows_of(cb), :])
        sf_acc[...] = sf_acc[...] * cdec_ref[0, 0:1, :] + uf
        sb_acc[...] = sb_acc[...] * cdec_ref[1, 0:1, :] + ub
        return carry

    lax.fori_loop(0, n_chunks, scan_step, 0)

    def out_step(ci, carry):
        rows = rows_of(ci)
        q = q_ref[rows, :]
        qf = q.astype(F32)
        qdec_f = jnp.concatenate([col_ref[0]] * (RET_KEY_DIM // LANES), axis=1)
        qdec_b = jnp.concatenate([col_ref[1]] * (RET_KEY_DIM // LANES), axis=1)
        att = (_dot(q, kt_ref[ci]) * decay_ref[...]).astype(BF16)
        o = (_dot(att, v_ref[rows, :])
             + _dot((qf * qdec_f).astype(BF16), sf_all[ci])
             + _dot((qf * qdec_b).astype(BF16), sb_all[ci]))
        y_ref[rows, :] = _group_norm_gate(o, gn_ref[...], z_ref[rows, :])
        return carry

    lax.fori_loop(0, n_chunks, out_step, 0, unroll=2)


def _ret_lat(q, kt, v, z, gn, decay, row, col, cdec, s0f, s0b):
    c = RET_CHUNK
    n_chunks = DEC_SEQ // c
    row0 = N_PROMPT_TOK // DEC_SEQ
    qk_spec = pl.BlockSpec((DEC_SEQ, RET_KEY_DIM), lambda b, h: (row0 + b, h))
    v_spec = pl.BlockSpec((DEC_SEQ, RET_VAL_DIM), lambda b, h: (row0 + b, h))
    st_spec = pl.BlockSpec((None, None, RET_KEY_DIM, RET_VAL_DIM), lambda b, h: (b, h, 0, 0))
    tab = lambda *s: pl.BlockSpec((None,) + s, lambda b, h: (h,) + (0,) * len(s))
    states = pltpu.VMEM((n_chunks, RET_KEY_DIM, RET_VAL_DIM), BF16)
    acc = pltpu.VMEM((RET_KEY_DIM, RET_VAL_DIM), F32)
    return pl.pallas_call(
        _ret_lat_kernel,
        out_shape=jax.ShapeDtypeStruct((N_SAMPLE_TOK, RET_V_WIDTH), BF16),
        grid=(DEC_BATCH, RET_HEADS),
        in_specs=[qk_spec,
                  pl.BlockSpec((n_chunks, RET_KEY_DIM, c), lambda b, h: (row0 + b, h, 0)),
                  v_spec, v_spec, pl.BlockSpec((1, RET_VAL_DIM), lambda b, h: (0, h)),
                  tab(c, c), tab(2, RET_TAB_ROWS, c), tab(2, c, LANES), tab(2, RET_TAB_ROWS, RET_VAL_DIM),
                  st_spec, st_spec],
        out_specs=pl.BlockSpec((DEC_SEQ, RET_VAL_DIM), lambda b, h: (b, h)),
        scratch_shapes=[states, states, acc, acc],
        compiler_params=_params("arbitrary", "arbitrary", vmem=VMEM_LIMIT),
        name="ret_lat",
    )(q, kt, v, z, gn, decay, row, col, cdec, s0f, s0b)


def _rope_tables(tm):
    n_rows = DEC_SEQ // GRID_W
    rows = jnp.repeat(jnp.arange(n_rows), GRID_W).astype(F32)
    cols = jnp.tile(jnp.arange(GRID_W), n_rows).astype(F32)
    half = ATTN_HEAD_DIM // 4
    inv = ROPE_BASE ** (-jnp.arange(half, dtype=F32) / half)
    ang_r = rows[:, None] * inv[None, :]
    ang_c = cols[:, None] * inv[None, :]
    cos = jnp.concatenate([jnp.cos(ang_r), jnp.cos(ang_r), jnp.cos(ang_c), jnp.cos(ang_c)], axis=-1)
    sin = jnp.concatenate([-jnp.sin(ang_r), jnp.sin(ang_r), -jnp.sin(ang_c), jnp.sin(ang_c)], axis=-1)
    cos = jnp.concatenate([jnp.ones((tm, ATTN_HEAD_DIM), F32), cos], axis=0)
    sin = jnp.concatenate([jnp.zeros((tm, ATTN_HEAD_DIM), F32), sin], axis=0)
    return jnp.tile(cos, (1, 2)), jnp.tile(sin, (1, 2))


def kernel(x_prompt, x_sample, cache_k, cache_v, state_fwd, state_bwd, c, c_ctx, norm_g, ada_w, ada_b, attn_w_in, attn_w_out, attn_sink, pool_w_in, pool_w_grp, pool_scale, pool_w_out, ret_w_in, ret_decay_fwd, ret_decay_bwd, ret_gn_g, ret_w_out, final_g):
    x_parts = (x_prompt.reshape(N_PROMPT_TOK, D_MODEL), x_sample.reshape(N_SAMPLE_TOK, D_MODEL))
    cond = jnp.concatenate([c_ctx[None, :], c,
                            jnp.zeros((N_COND - 1 - DEC_BATCH, D_MODEL), F32)], axis=0)
    mods = _ada_table(cond.T, ada_w, ada_b).reshape(DEPTH, N_COND, 1, 3 * D_MODEL)
    cos_t, sin_t = _rope_tables(PROJ.tm)

    to_kernel = lambda a: jnp.transpose(a, (0, 1, 3, 4, 2))
    from_kernel = lambda a: jnp.transpose(
        a.reshape(a.shape[0], a.shape[1], ATTN_KV_HEADS, ATTN_HEAD_DIM, a.shape[3]), (0, 1, 4, 2, 3))
    ctx_k, ctx_v = to_kernel(cache_k), to_kernel(cache_v)

    assert DEPTH % N_MIXERS == 1, "the layer stack must end on an attention layer"
    caches = ()
    new_sf = new_sb = None
    mixed = None
    for i in range(DEPTH):
        kind, j = i % N_MIXERS, i // N_MIXERS
        g = norm_g[i].reshape(1, D_MODEL)
        mod = mods[i]
        if kind == 0:
            outs = _attn_in(mixed, x_parts, g, mod, attn_w_in, j, cos_t, sin_t, tuple(caches))
            if mixed is not None:
                x_parts, outs = (outs[0],), outs[1:]
            q, k4, v4, z, *caches = outs
            y_parts = (_ctx_attn(attn_sink[j], q, k4, v4, z),
                       _lat_attn(attn_sink[j], q, k4, v4, ctx_k, ctx_v, j, z))
            mixed = Mixed(y_parts, attn_w_out, j, mod)
        elif kind == 1:
            x, u, z = _pool_in(mixed, x_parts, g, mod, pool_w_in, j)
            x_parts, mixed, pooled = (x,), None, (u, z, j, mod)
        else:
            u, z, jp, mod_pool = pooled
            lg_f = jax.nn.log_sigmoid(ret_decay_fwd[j].astype(F32))
            lg_b = jax.nn.log_sigmoid(ret_decay_bwd[j].astype(F32))
            gn = ret_gn_g[j].reshape(1, RET_V_WIDTH)
            x, q, kt, v, z = _pool_ret(u, z, pool_w_grp, pool_scale[jp].reshape(1, D_MODEL), pool_w_out, jp,
                                       x_parts[0], mod_pool, g, mod, ret_w_in[j].astype(BF16))
            x_parts = (x,)
            decay, row, col, cdec = _ret_tables(lg_f, lg_b)
            y_ctx, new_sf, new_sb = _ret_ctx(q, kt, v, z, gn, decay, row)
            y_parts = (y_ctx, _ret_lat(q, kt, v, z, gn, decay, row, col, cdec,
                                       state_fwd[:, j], state_bwd[:, j]))
            mixed = Mixed(y_parts, ret_w_out, j, mod)
    y_prompt, y_sample = _out_proj_final(mixed, x_parts[0], final_g.reshape(1, D_MODEL))
    new_k, new_v = caches
    return (y_prompt.reshape(BATCH, SEQ, D_MODEL), y_sample.reshape(DEC_BATCH, DEC_SEQ, D_MODEL),
            from_kernel(new_k), from_kernel(new_v), new_sf[:, None], new_sb[:, None])
```

```python
import functools
from typing import NamedTuple

import jax
import jax.numpy as jnp
from jax import lax
from jax.experimental import pallas as pl
from jax.experimental.pallas import tpu as pltpu

F32 = jnp.float32
BF16 = jnp.bfloat16

D_MODEL = 1024
BATCH = 16
SEQ = 256
DEPTH = 4
DEC_BATCH = 2
DEC_SEQ = 2048
PAST_LEN = 512
GRID_W = 64
N_MIXERS = 3
ATTN_HEADS = 16
ATTN_KV_HEADS = 4
ATTN_HEAD_DIM = 64
ATTN_GROUP = 4
ATTN_WIDTH = 1024
ATTN_KV_WIDTH = 256
WINDOW = 128
ROPE_BASE = 10000.0
POOL_WINDOWS = (2, 4, 8, 16)
POOL_GROUP_DIM = 256
RET_HEADS = 4
RET_KEY_DIM = 256
RET_VAL_DIM = 512
RET_QK_WIDTH = 1024
RET_V_WIDTH = 2048
EPS = 1e-6
NEG_INF = -1e30
LOG2_E = 1.4426950408889634

N_PROMPT_TOK = BATCH * SEQ
N_SAMPLE_TOK = DEC_BATCH * DEC_SEQ
N_TOK = N_PROMPT_TOK + N_SAMPLE_TOK
N_COND = 8
LANES = 128
MXU_N = 256
Q_BLOCK = 128
Q_SUB = 4
RET_CHUNK = 256
RET_TAB_ROWS = 8
RET_CTX_SEQS = 2
POOL_HALO = 8
POOL_SUB = SEQ
VMEM_LIMIT = 48 * 1024 * 1024
FUSED_VMEM_LIMIT = 58 * 1024 * 1024


class Tiling(NamedTuple):
    tm: int

    @property
    def n_tiles(self):
        return N_TOK // self.tm

    @property
    def n_prompt(self):
        return N_PROMPT_TOK // self.tm

    @property
    def per_dec_seq(self):
        return DEC_SEQ // self.tm

    def cond(self, i):
        return jnp.where(i < self.n_prompt, 0, 1 + (i - self.n_prompt) // self.per_dec_seq)

    def seq_tile(self, i):
        return jnp.where(i < self.n_prompt, 0, (i - self.n_prompt) % self.per_dec_seq)

    def rows(self, width):
        return pl.BlockSpec((self.tm, width), lambda i: (i, 0))

    def prompt_rows(self, width):
        return pl.BlockSpec((self.tm, width), lambda i: (jnp.minimum(i, self.n_prompt - 1), 0))

    def sample_rows(self, width):
        return pl.BlockSpec((self.tm, width), lambda i: (jnp.maximum(i - self.n_prompt, 0), 0))

    def mod(self, part):
        return pl.BlockSpec((None, 1, D_MODEL), lambda i: (self.cond(i), 0, part))


PROJ = Tiling(512)
FINAL = Tiling(1024)


def _silu(z):
    return z * (1.0 / (1.0 + jnp.exp(-z)))


def _dot(a, b):
    return jnp.dot(a, b, preferred_element_type=F32)


def _dot_nt(a, b):
    return lax.dot_general(a, b, (((1,), (1,)), ((), ())), preferred_element_type=F32)


def _params(*sem, vmem=None):
    return pltpu.CompilerParams(dimension_semantics=sem, vmem_limit_bytes=vmem)


def _const_spec(shape):
    nd = len(shape)
    return pl.BlockSpec(shape, lambda *_: (0,) * nd, pipeline_mode=pl.Buffered(1))


def _part_specs(t, parts, width):
    if len(parts) == 1:
        return [t.rows(width)]
    return [t.prompt_rows(width), t.sample_rows(width)]


def _read_parts(t, refs):
    if len(refs) == 1:
        return refs[0][...]
    return jnp.where(pl.program_id(0) < t.n_prompt, refs[0][...], refs[1][...])


def _ada_kernel(cond_ref, w_ref, b_ref, o_ref):
    s = _silu(cond_ref[...])
    w = w_ref[...]
    rows = [jnp.sum(s[:, c:c + 1] * w, axis=0, keepdims=True) + b_ref[...] for c in range(1 + DEC_BATCH)]
    rows.append(jnp.zeros((N_COND - len(rows), w.shape[1]), F32))
    o_ref[...] = jnp.concatenate(rows, axis=0)


def _ada_table(cond, ada_w, ada_b):
    tn = 3 * D_MODEL // 2
    return pl.pallas_call(
        _ada_kernel,
        out_shape=jax.ShapeDtypeStruct((DEPTH, N_COND, 3 * D_MODEL), F32),
        grid=(DEPTH, 3 * D_MODEL // tn),
        in_specs=[
            pl.BlockSpec((D_MODEL, N_COND), lambda l, n: (0, 0)),
            pl.BlockSpec((None, D_MODEL, tn), lambda l, n: (l, 0, n)),
            pl.BlockSpec((None, 1, tn), lambda l, n: (l, 0, n)),
        ],
        out_specs=pl.BlockSpec((None, N_COND, tn), lambda l, n: (l, 0, n)),
        compiler_params=_params("arbitrary", "arbitrary", vmem=VMEM_LIMIT),
        name="ada_table",
    )(cond, ada_w, ada_b.reshape(DEPTH, 1, 3 * D_MODEL))


def _rms(x, g):
    return x * lax.rsqrt(jnp.mean(x * x, axis=-1, keepdims=True) + EPS) * g


def _norm_mod(x, g_ref, sh_ref, sc_ref):
    return (_rms(x, g_ref[...]) * (1.0 + sc_ref[...]) + sh_ref[...]).astype(BF16)


def _proj_chunks(h, w_ref, lo, width, store):
    for c in range(width // MXU_N):
        store(c, _dot(h, w_ref[:, lo + c * MXU_N:lo + (c + 1) * MXU_N]))


def _layer_spec(w, j):
    nd = w.ndim - 1
    return pl.BlockSpec((None,) + w.shape[1:], lambda *_: (j,) + (0,) * nd, pipeline_mode=pl.Buffered(1))


def _cast_weight_once(w_ref, wb_ref):
    @pl.when(pl.program_id(0) == 0)
    def _():
        rows = wb_ref.shape[-2]
        for r in range(0, rows, MXU_N):
            wb_ref[..., r:r + MXU_N, :] = w_ref[..., r:r + MXU_N, :].astype(BF16)


class Mixed(NamedTuple):
    y_parts: tuple
    w_out: jax.Array
    j: int
    mod: jax.Array


def _residual_specs(t, mixed, x_parts):
    k = mixed.w_out.shape[1]
    return (_part_specs(t, mixed.y_parts, k) + [_layer_spec(mixed.w_out, mixed.j)]
            + _part_specs(t, x_parts, D_MODEL) + [t.mod(2)])


def _residual_args(mixed, x_parts):
    return (*mixed.y_parts, mixed.w_out, *x_parts, mixed.mod)


def _residual_update(t, refs, n_y, n_x, wo_ref, xo_ref):
    y_refs, wof_ref, x_refs, gate_ref = refs[:n_y], refs[n_y], refs[n_y + 1:n_y + 1 + n_x], refs[n_y + 1 + n_x]
    _cast_weight_once(wof_ref, wo_ref)
    x = _read_parts(t, x_refs) + gate_ref[...] * _dot(_read_parts(t, y_refs), wo_ref[...])
    xo_ref[...] = x
    return x


def _rep4(a, h):
    half = a[:, (h // 2) * LANES:(h // 2 + 1) * LANES]
    lane = lax.broadcasted_iota(jnp.int32, half.shape, 1)
    keep = (lane < ATTN_HEAD_DIM) if h % 2 == 0 else (lane >= ATTN_HEAD_DIM)
    m = jnp.where(keep, half, 0.0)
    s = m + pltpu.roll(m, ATTN_HEAD_DIM, 1)
    return jnp.concatenate([s, s], axis=1)


def _attn_in_kernel(*refs, n_y, n_x, n_alias):
    t = PROJ
    n_res = n_y + n_x + 2 if n_y else n_x
    res_refs, refs = refs[:n_res], refs[n_res:]
    (g_ref, sh_ref, sc_ref, wf_ref, cos_ref, sin_ref), refs = refs[:6], refs[6 + n_alias:]
    if n_y:
        xo_ref, q_ref, k4_ref, v4_ref, z_ref, kc_ref, vc_ref, wo_ref, w_ref = refs
        x = _residual_update(t, res_refs, n_y, n_x, wo_ref, xo_ref)
    else:
        q_ref, k4_ref, v4_ref, z_ref, kc_ref, vc_ref, w_ref = refs
        x = _read_parts(t, res_refs)
    i = pl.program_id(0)
    _cast_weight_once(wf_ref, w_ref)
    h = _norm_mod(x, g_ref, sh_ref, sc_ref)
    cos = cos_ref[...]
    sin = sin_ref[...]
    lane = lax.broadcasted_iota(jnp.int32, (t.tm, LANES), 1)
    first = (lane % (ATTN_HEAD_DIM // 2)) < ATTN_HEAD_DIM // 4

    def rope(a):
        rot = jnp.where(first, pltpu.roll(a, LANES - ATTN_HEAD_DIM // 4, 1),
                        pltpu.roll(a, ATTN_HEAD_DIM // 4, 1))
        return a * cos + rot * sin

    def rope_wide(a):
        return jnp.concatenate(
            [rope(a[:, s * LANES:(s + 1) * LANES]) for s in range(MXU_N // LANES)], axis=1)

    scale = ATTN_HEAD_DIM ** -0.5 * LOG2_E

    def store_q(c, a):
        q_ref[:, c * MXU_N:(c + 1) * MXU_N] = (rope_wide(a) * scale).astype(BF16)

    def store_z(c, a):
        z_ref[:, c * MXU_N:(c + 1) * MXU_N] = a.astype(BF16)

    _proj_chunks(h, w_ref, 0, ATTN_WIDTH, store_q)
    k = rope_wide(_dot(h, w_ref[:, ATTN_WIDTH:ATTN_WIDTH + ATTN_KV_WIDTH]))
    v = _dot(h, w_ref[:, ATTN_WIDTH + ATTN_KV_WIDTH:ATTN_WIDTH + 2 * ATTN_KV_WIDTH])
    for hh in range(ATTN_KV_HEADS):
        k4_ref[:, hh * MXU_N:(hh + 1) * MXU_N] = _rep4(k, hh).astype(BF16)
        v4_ref[:, hh * MXU_N:(hh + 1) * MXU_N] = _rep4(v, hh).astype(BF16)
    _proj_chunks(h, w_ref, ATTN_WIDTH + 2 * ATTN_KV_WIDTH, ATTN_WIDTH, store_z)

    @pl.when(i < t.n_prompt)
    def _():
        for s in range(t.tm // SEQ):
            kc_ref[s] = k[s * SEQ:(s + 1) * SEQ, :].T
            vc_ref[s] = v[s * SEQ:(s + 1) * SEQ, :].T


def _attn_in(mixed, x_parts, g, mod, w_all, j, cos_t, sin_t, caches):
    t = PROJ
    n_in = 2 * ATTN_WIDTH + 2 * ATTN_KV_WIDTH
    per = t.tm // SEQ
    rope_spec = pl.BlockSpec(
        (t.tm, LANES), lambda i: (jnp.where(i < t.n_prompt, 0, 1 + t.seq_tile(i)), 0))
    wide = jax.ShapeDtypeStruct((N_TOK, ATTN_WIDTH), BF16)
    n_attn = w_all.shape[0]
    cache = jax.ShapeDtypeStruct((BATCH, n_attn, ATTN_KV_WIDTH, SEQ), F32)
    cache_spec = pl.BlockSpec((per, None, ATTN_KV_WIDTH, SEQ),
                              lambda i: (jnp.minimum(i, t.n_prompt - 1), j, 0, 0))
    n_x = len(x_parts)
    if mixed is None:
        n_y, res_specs, res_args, res_out, res_ospecs, res_scratch = 0, _part_specs(t, x_parts, D_MODEL), x_parts, (), (), []
    else:
        n_y = len(mixed.y_parts)
        res_specs, res_args = _residual_specs(t, mixed, x_parts), _residual_args(mixed, x_parts)
        res_out = (jax.ShapeDtypeStruct((N_TOK, D_MODEL), F32),)
        res_ospecs = (t.rows(D_MODEL),)
        res_scratch = [pltpu.VMEM(mixed.w_out.shape[1:], BF16)]
    n_front = len(res_specs) + 6
    return pl.pallas_call(
        functools.partial(_attn_in_kernel, n_y=n_y, n_x=n_x, n_alias=len(caches)),
        out_shape=res_out + (wide, wide, wide, wide, cache, cache),
        grid=(t.n_tiles,),
        in_specs=res_specs + [
            _const_spec((1, D_MODEL)), t.mod(0), t.mod(1), _layer_spec(w_all, j),
            rope_spec, rope_spec] + [pl.BlockSpec(memory_space=pl.ANY)] * len(caches),
        out_specs=res_ospecs + (t.rows(ATTN_WIDTH),) * 4 + (cache_spec,) * 2,
        scratch_shapes=res_scratch + [pltpu.VMEM((D_MODEL, n_in), BF16)],
        input_output_aliases={n_front + c: len(res_out) + 4 + c for c in range(len(caches))},
        compiler_params=_params("arbitrary", vmem=FUSED_VMEM_LIMIT),
        name="attn_in",
    )(*res_args, g, mod, mod, w_all, cos_t, sin_t, *caches)


def _stack_group_queries(q):
    qf = q.astype(F32)
    chunk = lax.broadcasted_iota(jnp.int32, qf.shape, 1) // ATTN_HEAD_DIM
    return jnp.concatenate(
        [jnp.where(chunk == g, qf, 0.0) for g in range(ATTN_GROUP)], axis=0).astype(BF16)


def _gather_group_outputs(o, rows):
    chunk = lax.broadcasted_iota(jnp.int32, (rows, MXU_N), 1) // ATTN_HEAD_DIM
    acc = jnp.zeros((rows, MXU_N), F32)
    for g in range(ATTN_GROUP):
        acc = acc + jnp.where(chunk == g, o[g * rows:(g + 1) * rows], 0.0)
    return acc


def _sink_column(sink_ref, h, rows):
    grp = lax.broadcasted_iota(jnp.int32, (ATTN_GROUP * rows, 1), 0) // rows
    col = jnp.zeros((ATTN_GROUP * rows, 1), F32)
    for g in range(ATTN_GROUP):
        col = jnp.where(grp == g, sink_ref[h * ATTN_GROUP + g] * LOG2_E, col)
    return col


def _chunk_rows(dtype):
    chunk = lax.broadcasted_iota(jnp.int32, (1, MXU_N), 1) // ATTN_HEAD_DIM
    return [(chunk == g).astype(F32).astype(dtype) for g in range(ATTN_GROUP)]


def _block_diag_rows(x4):
    return jnp.concatenate([x4 * m for m in _chunk_rows(x4.dtype)], axis=0)


def _ctx_attn_kernel(sink_ref, q_ref, k4_ref, v4_ref, z_ref, y_ref):
    chunk = lax.broadcasted_iota(jnp.int32, (SEQ, MXU_N), 1) // ATTN_HEAD_DIM
    for h in range(ATTN_KV_HEADS):
        cols = slice(h * MXU_N, (h + 1) * MXU_N)
        s = _dot_nt(q_ref[:, cols], _block_diag_rows(k4_ref[:, cols]))
        inv = jnp.zeros((SEQ, MXU_N), F32)
        probs = []
        for g in range(ATTN_GROUP):
            sg = s[:, g * SEQ:(g + 1) * SEQ]
            sk = sink_ref[h * ATTN_GROUP + g] * LOG2_E
            m = jnp.maximum(jnp.max(sg, axis=1, keepdims=True), sk)
            e = jnp.exp2(sg - m)
            den = jnp.sum(e, axis=1, keepdims=True) + jnp.exp2(sk - m)
            probs.append(e.astype(BF16))
            inv = jnp.where(chunk == g, 1.0 / den, inv)
        o = _dot(jnp.concatenate(probs, axis=1), _block_diag_rows(v4_ref[:, cols]))
        y_ref[:, cols] = (o * inv * _silu(z_ref[:, cols].astype(F32))).astype(BF16)


def _ctx_attn(sink, q, k4, v4, z):
    spec = pl.BlockSpec((SEQ, ATTN_WIDTH), lambda b: (b, 0))
    return pl.pallas_call(
        _ctx_attn_kernel,
        out_shape=jax.ShapeDtypeStruct((N_PROMPT_TOK, ATTN_WIDTH), BF16),
        grid=(BATCH,),
        in_specs=[pl.BlockSpec(memory_space=pltpu.SMEM), spec, spec, spec, spec],
        out_specs=spec,
        compiler_params=_params("arbitrary"),
        name="ctx_attn",
    )(sink, q, k4, v4, z)


LAT_STEP = Q_SUB * Q_BLOCK
LAT_PER_SEQ = DEC_SEQ // LAT_STEP
LAT_BLOCKS = DEC_BATCH * ATTN_KV_HEADS * LAT_PER_SEQ


def _lat_block(blk):
    return (blk // (ATTN_KV_HEADS * LAT_PER_SEQ), (blk // LAT_PER_SEQ) % ATTN_KV_HEADS, blk % LAT_PER_SEQ)


def _band_rows(qb):
    return pl.ds(pl.multiple_of(qb * Q_BLOCK, Q_BLOCK), 3 * Q_BLOCK)


def _pad_sequence(dst, src):
    zeros = jnp.zeros((Q_BLOCK, MXU_N), BF16)
    dst[0:Q_BLOCK, :] = zeros
    dst[Q_BLOCK:Q_BLOCK + DEC_SEQ, :] = src[...]
    dst[Q_BLOCK + DEC_SEQ:, :] = zeros


def _lat_attn_kernel(sink_ref, q_ref, k4_ref, v4_ref, kc_ref, vc_ref, z_ref, y_ref,
                     kp, vp, k4c, v4c, sc_a, sl_a, sc_b, sl_b, e_scr):
    t = pl.program_id(0)
    _, _, n1 = _lat_block(jnp.minimum(t, LAT_BLOCKS - 1))
    _, h0, n0 = _lat_block(jnp.maximum(t - 1, 0))
    n_blocks = DEC_SEQ // Q_BLOCK

    @pl.when(t == 0)
    def _():
        sc_b[...] = jnp.zeros(sc_b.shape, F32)
        sl_b[...] = jnp.zeros(sl_b.shape, F32)

    @pl.when(n1 == 0)
    def _():
        _pad_sequence(kp, k4_ref)
        k4c[...] = jnp.concatenate([kc_ref[...]] * ATTN_GROUP, axis=0).astype(BF16)

    @pl.when(n0 == 0)
    def _():
        _pad_sequence(vp, v4_ref)
        v4c[...] = jnp.concatenate([vc_ref[...]] * ATTN_GROUP, axis=0).T.astype(BF16)

    @pl.when(t % 2 == 0)
    def _():
        _lat_stages(sink_ref, q_ref, z_ref, y_ref, kp, vp, k4c, v4c, e_scr, n1, h0, n0,
                    sc_a, sl_a, sc_b, sl_b)

    @pl.when(t % 2 == 1)
    def _():
        _lat_stages(sink_ref, q_ref, z_ref, y_ref, kp, vp, k4c, v4c, e_scr, n1, h0, n0,
                    sc_b, sl_b, sc_a, sl_a)


def _lat_stages(sink_ref, q_ref, z_ref, y_ref, kp, vp, k4c, v4c, e_scr, n1, h0, n0,
                sc_w, sl_w, sc_r, sl_r):
    n_blocks = DEC_SEQ // Q_BLOCK
    for sub in range(Q_SUB):
        qs = _stack_group_queries(q_ref[sub * Q_BLOCK:(sub + 1) * Q_BLOCK, :])
        sc_w[sub] = _dot(qs, k4c[...])
        sl_w[sub] = _dot_nt(qs, kp[_band_rows(n1 * Q_SUB + sub), :])

    rows4 = ATTN_GROUP * Q_BLOCK
    r = lax.broadcasted_iota(jnp.int32, (rows4, Q_BLOCK), 0) % Q_BLOCK
    c = lax.broadcasted_iota(jnp.int32, (rows4, Q_BLOCK), 1)
    in_left = c >= r
    in_right = c <= r
    sk = _sink_column(sink_ref, h0, Q_BLOCK)
    for sub in range(Q_SUB):
        qb = n0 * Q_SUB + sub
        band = _band_rows(qb)
        qrows = slice(sub * Q_BLOCK, (sub + 1) * Q_BLOCK)
        s_ctx = sc_r[sub]
        s_lat = sl_r[sub]
        blocks = [
            s_ctx,
            jnp.where(jnp.logical_and(in_left, qb > 0), s_lat[:, :Q_BLOCK], NEG_INF),
            s_lat[:, Q_BLOCK:2 * Q_BLOCK],
            jnp.where(jnp.logical_and(in_right, qb < n_blocks - 1), s_lat[:, 2 * Q_BLOCK:], NEG_INF),
        ]
        m = sk
        for s in blocks:
            m = jnp.maximum(m, jnp.max(s, axis=1, keepdims=True))
        den = jnp.exp2(sk - m)
        lo = 0
        for s in blocks:
            e = jnp.exp2(s - m)
            den = den + jnp.sum(e, axis=1, keepdims=True)
            e_scr[sub, :, lo:lo + s.shape[1]] = e.astype(BF16)
            lo += s.shape[1]
        o = (_dot(e_scr[sub, :, :PAST_LEN], v4c[...])
             + _dot(e_scr[sub, :, PAST_LEN:], vp[band, :])) * (1.0 / den)
        acc = _gather_group_outputs(o, Q_BLOCK)
        y_ref[qrows, :] = (acc * _silu(z_ref[qrows, :].astype(F32))).astype(BF16)


def _lat_attn(sink, q, k4, v4, kc, vc, j, z):
    padded = DEC_SEQ + 2 * Q_BLOCK
    rows4 = ATTN_GROUP * Q_BLOCK
    row0 = N_PROMPT_TOK // LAT_STEP
    seq0 = N_PROMPT_TOK // DEC_SEQ
    stage1 = lambda t: _lat_block(jnp.minimum(t, LAT_BLOCKS - 1))
    stage2 = lambda t: _lat_block(jnp.maximum(t - 1, 0))

    def spec(shape, stage, index):
        return pl.BlockSpec(shape, lambda t: index(*stage(t)))

    tile = (LAT_STEP, MXU_N)
    seq = (DEC_SEQ, MXU_N)
    ctx = (None, None, None, ATTN_HEAD_DIM, PAST_LEN)
    return pl.pallas_call(
        _lat_attn_kernel,
        out_shape=jax.ShapeDtypeStruct((N_SAMPLE_TOK, ATTN_WIDTH), BF16),
        grid=(LAT_BLOCKS + 1,),
        in_specs=[pl.BlockSpec(memory_space=pltpu.SMEM),
                  spec(tile, stage1, lambda b, h, n: (row0 + b * LAT_PER_SEQ + n, h)),
                  spec(seq, stage1, lambda b, h, n: (seq0 + b, h)),
                  spec(seq, stage2, lambda b, h, n: (seq0 + b, h)),
                  spec(ctx, stage1, lambda b, h, n: (b, j, h, 0, 0)),
                  spec(ctx, stage2, lambda b, h, n: (b, j, h, 0, 0)),
                  spec(tile, stage2, lambda b, h, n: (row0 + b * LAT_PER_SEQ + n, h))],
        out_specs=spec(tile, stage2, lambda b, h, n: (b * LAT_PER_SEQ + n, h)),
        scratch_shapes=[pltpu.VMEM((padded, MXU_N), BF16), pltpu.VMEM((padded, MXU_N), BF16),
                        pltpu.VMEM((MXU_N, PAST_LEN), BF16), pltpu.VMEM((PAST_LEN, MXU_N), BF16),
                        pltpu.VMEM((Q_SUB, rows4, PAST_LEN), F32),
                        pltpu.VMEM((Q_SUB, rows4, 3 * Q_BLOCK), F32),
                        pltpu.VMEM((Q_SUB, rows4, PAST_LEN), F32),
                        pltpu.VMEM((Q_SUB, rows4, 3 * Q_BLOCK), F32),
                        pltpu.VMEM((Q_SUB, rows4, PAST_LEN + 3 * Q_BLOCK), BF16)],
        compiler_params=_params("arbitrary", vmem=VMEM_LIMIT),
        name="lat_attn",
    )(sink, q, k4, v4, kc, vc, z)


def _out_final_kernel(yp_ref, ys_ref, wf_ref, x_ref, gate_ref, fg_ref, op_ref, os_ref, w_ref):
    t = FINAL
    i = pl.program_id(0)
    _cast_weight_once(wf_ref, w_ref)
    y = _read_parts(t, (yp_ref, ys_ref))
    r = _rms(x_ref[...] + gate_ref[...] * _dot(y, w_ref[...]), fg_ref[...])

    @pl.when(i < t.n_prompt)
    def _():
        op_ref[...] = r

    @pl.when(i >= t.n_prompt)
    def _():
        os_ref[...] = r


def _out_proj_final(mixed, x, final_g):
    t = FINAL
    return pl.pallas_call(
        _out_final_kernel,
        out_shape=(jax.ShapeDtypeStruct((N_PROMPT_TOK, D_MODEL), F32),
                   jax.ShapeDtypeStruct((N_SAMPLE_TOK, D_MODEL), F32)),
        grid=(t.n_tiles,),
        in_specs=_residual_specs(t, mixed, (x,)) + [_const_spec((1, D_MODEL))],
        out_specs=(t.prompt_rows(D_MODEL), t.sample_rows(D_MODEL)),
        scratch_shapes=[pltpu.VMEM(mixed.w_out.shape[1:], BF16)],
        compiler_params=_params("arbitrary", vmem=VMEM_LIMIT),
        name="out_proj_final",
    )(*_residual_args(mixed, (x,)), final_g)


def _pool_in_kernel(*refs, n_y, n_x):
    t = PROJ
    n_res = n_y + n_x + 2
    res_refs, (g_ref, sh_ref, sc_ref, wf_ref, xo_ref, u_ref, z_ref, wo_ref, w_ref) = refs[:n_res], refs[n_res:]
    x = _residual_update(t, res_refs, n_y, n_x, wo_ref, xo_ref)
    _cast_weight_once(wf_ref, w_ref)
    h = _norm_mod(x, g_ref, sh_ref, sc_ref)

    def store_u(c, a):
        u_ref[:, c * MXU_N:(c + 1) * MXU_N] = a

    def store_z(c, a):
        z_ref[:, c * MXU_N:(c + 1) * MXU_N] = a.astype(BF16)

    _proj_chunks(h, w_ref, 0, D_MODEL, store_u)
    _proj_chunks(h, w_ref, D_MODEL, D_MODEL, store_z)


def _pool_in(mixed, x_parts, g, mod, w_all, j):
    t = PROJ
    slab = jax.ShapeDtypeStruct((N_TOK, D_MODEL), F32)
    return pl.pallas_call(
        functools.partial(_pool_in_kernel, n_y=len(mixed.y_parts), n_x=len(x_parts)),
        out_shape=(slab, slab, jax.ShapeDtypeStruct((N_TOK, D_MODEL), BF16)),
        grid=(t.n_tiles,),
        in_specs=_residual_specs(t, mixed, x_parts) + [
            _const_spec((1, D_MODEL)), t.mod(0), t.mod(1), _layer_spec(w_all, j)],
        out_specs=(t.rows(D_MODEL),) * 3,
        scratch_shapes=[pltpu.VMEM(mixed.w_out.shape[1:], BF16), pltpu.VMEM((D_MODEL, 2 * D_MODEL), BF16)],
        compiler_params=_params("arbitrary", vmem=FUSED_VMEM_LIMIT),
        name="pool_in",
    )(*_residual_args(mixed, x_parts), g, mod, mod, w_all)


def _split_bf16(a):
    hi = a.astype(BF16)
    return hi, (a - hi.astype(F32)).astype(BF16)


def _band_ones(shape, lo, hi):
    d = lax.broadcasted_iota(jnp.int32, shape, 1) - lax.broadcasted_iota(jnp.int32, shape, 0)
    return jnp.logical_and(d >= lo, d <= hi).astype(F32).astype(BF16)


def _pool_ret_kernel(u_ref, up_ref, un_ref, z_ref, wgf_ref, ps_ref, wof_ref, x_ref, gate_ref,
                     g_ref, sh_ref, sc_ref, w_ref,
                     xo_ref, q_ref, kt_ref, v_ref, zr_ref,
                     y_ref, wg_ref, wo_ref, band_ref, wkt_ref):
    t = PROJ
    i = pl.program_id(0)
    _cast_weight_once(wgf_ref, wg_ref)
    _cast_weight_once(wof_ref, wo_ref)

    @pl.when(i == 0)
    def _():
        for g, w in enumerate(POOL_WINDOWS):
            band_ref[g] = _band_ones((POOL_SUB, POOL_SUB), -(w // 2), w - 1 - w // 2)
        for r in range(0, RET_QK_WIDTH, MXU_N):
            wk = w_ref[:, RET_QK_WIDTH + r:RET_QK_WIDTH + r + MXU_N]
            wkt_ref[r:r + MXU_N, :] = wk.astype(F32).T.astype(BF16)

    is_dec = i >= t.n_prompt
    st = t.seq_tile(i)
    seq_len = jnp.where(is_dec, DEC_SEQ, SEQ)
    h = POOL_HALO
    n_sub = t.tm // POOL_SUB
    def pool_steps(sub):
        rows = slice(sub * POOL_SUB, (sub + 1) * POOL_SUB)
        if sub == 0:
            before = jnp.where(jnp.logical_and(is_dec, st != 0), up_ref[...], 0.0)
        else:
            before = jnp.where(is_dec, u_ref[sub * POOL_SUB - h:sub * POOL_SUB, :], 0.0)
        if sub == n_sub - 1:
            after = jnp.where(jnp.logical_and(is_dec, st != t.per_dec_seq - 1), un_ref[...], 0.0)
        else:
            after = jnp.where(is_dec, u_ref[(sub + 1) * POOL_SUB:(sub + 1) * POOL_SUB + h, :], 0.0)
        pos0 = jnp.where(is_dec, st * t.tm + sub * POOL_SUB, 0)
        return _pool_core_steps(u_ref.at[rows], before, after, pos0, seq_len, z_ref.at[rows], ps_ref,
                                wg_ref, band_ref, y_ref.at[rows])

    for step in pool_steps(0):
        step()
    for sub in range(n_sub):
        rows = slice(sub * POOL_SUB, (sub + 1) * POOL_SUB)
        x = x_ref[rows, :] + gate_ref[...] * _dot(y_ref[rows, :], wo_ref[...])
        xo_ref[rows, :] = x
        proj = _ret_in_steps(x, g_ref, sh_ref, sc_ref, w_ref, wkt_ref, q_ref.at[rows],
                             kt_ref.at[pl.ds(sub * POOL_SUB // RET_CHUNK, POOL_SUB // RET_CHUNK)],
                             v_ref.at[rows], zr_ref.at[rows])
        _run_interleaved(proj, pool_steps(sub + 1) if sub + 1 < n_sub else [])


def _pool_core_steps(u_ref, before, after, pos0, seq_len, z_ref, ps_ref, wg_ref, band_ref, y_ref):
    n = u_ref.shape[0]
    h = POOL_HALO
    halo_hi, halo_lo = _split_bf16(jnp.concatenate([before, after], axis=0))
    pos = pos0 + lax.broadcasted_iota(jnp.int32, (n, 1), 0)
    rr = lax.broadcasted_iota(jnp.int32, (2 * h, 2 * h), 0)
    cc = lax.broadcasted_iota(jnp.int32, (2 * h, 2 * h), 1)

    def group(g, w):
        def step():
            left = w // 2
            right = w - 1 - left
            cols = slice(g * POOL_GROUP_DIM, (g + 1) * POOL_GROUP_DIM)
            u = u_ref[:, cols]
            s = _dot(band_ref[g], u.astype(BF16))
            top = jnp.logical_and(jnp.logical_and(rr < h, cc < h), cc - h >= rr - left)
            bot = jnp.logical_and(jnp.logical_and(rr >= h, cc >= h), cc - h <= rr - 2 * h + right)
            edge = jnp.logical_or(top, bot).astype(F32).astype(BF16)
            se = _dot(edge, halo_hi[:, cols]) + _dot(edge, halo_lo[:, cols])
            s = jnp.concatenate([s[:h] + se[:h], s[h:n - h], s[n - h:] + se[h:]], axis=0)
            cnt = (jnp.minimum(pos + right + 1, seq_len) - jnp.maximum(pos - left, 0)).astype(F32)
            d = (s / cnt - u).astype(BF16)
            yg = _dot(d, wg_ref[g]) * ps_ref[:, cols] * _silu(z_ref[:, cols].astype(F32))
            y_ref[:, cols] = yg.astype(BF16)
        return step

    return [group(g, w) for g, w in enumerate(POOL_WINDOWS)]


def _pool_ret(u, z, wg_all, ps, wo_all, j, x, mod_pool, g, mod, w_ret):
    t = PROJ
    per = t.tm // POOL_HALO
    n_halo = N_TOK // POOL_HALO
    n_in = 2 * RET_QK_WIDTH + 2 * RET_V_WIDTH
    kt_per = t.tm // RET_CHUNK
    return pl.pallas_call(
        _pool_ret_kernel,
        out_shape=(jax.ShapeDtypeStruct((N_TOK, D_MODEL), F32),
                   jax.ShapeDtypeStruct((N_TOK, RET_QK_WIDTH), BF16),
                   jax.ShapeDtypeStruct((N_TOK // RET_CHUNK, RET_QK_WIDTH, RET_CHUNK), BF16),
                   jax.ShapeDtypeStruct((N_TOK, RET_V_WIDTH), BF16),
                   jax.ShapeDtypeStruct((N_TOK, RET_V_WIDTH), BF16)),
        grid=(t.n_tiles,),
        in_specs=[
            t.rows(D_MODEL),
            pl.BlockSpec((POOL_HALO, D_MODEL), lambda i: (jnp.maximum(i * per - 1, 0), 0)),
            pl.BlockSpec((POOL_HALO, D_MODEL), lambda i: (jnp.minimum((i + 1) * per, n_halo - 1), 0)),
            t.rows(D_MODEL),
            _layer_spec(wg_all, j),
            _const_spec((1, D_MODEL)),
            _layer_spec(wo_all, j),
            t.rows(D_MODEL),
            t.mod(2),
            _const_spec((1, D_MODEL)), t.mod(0), t.mod(1), _const_spec((D_MODEL, n_in)),
        ],
        out_specs=(t.rows(D_MODEL), t.rows(RET_QK_WIDTH),
                   pl.BlockSpec((kt_per, RET_QK_WIDTH, RET_CHUNK), lambda i: (i, 0, 0)),
                   t.rows(RET_V_WIDTH), t.rows(RET_V_WIDTH)),
        scratch_shapes=[pltpu.VMEM((t.tm, D_MODEL), BF16),
                        pltpu.VMEM((len(POOL_WINDOWS), POOL_GROUP_DIM, POOL_GROUP_DIM), BF16),
                        pltpu.VMEM((D_MODEL, D_MODEL), BF16),
                        pltpu.VMEM((len(POOL_WINDOWS), POOL_SUB, POOL_SUB), BF16),
                        pltpu.VMEM((RET_QK_WIDTH, D_MODEL), BF16)],
        compiler_params=_params("arbitrary", vmem=FUSED_VMEM_LIMIT),
        name="pool_ret",
    )(u, u, u, z, wg_all, ps, wo_all, x, mod_pool, g, mod, mod, w_ret)


def _ret_in_steps(x, g_ref, sh_ref, sc_ref, w_ref, wkt_ref, q_ref, kt_ref, v_ref, z_ref):
    h = _norm_mod(x, g_ref, sh_ref, sc_ref)

    def chunk(ref, lo, c):
        def step():
            cols = slice(c * MXU_N, (c + 1) * MXU_N)
            ref[:, cols] = _dot(h, w_ref[:, lo + c * MXU_N:lo + (c + 1) * MXU_N]).astype(BF16)
        return step

    def key_chunk(c):
        def step():
            rows = slice(c * MXU_N, (c + 1) * MXU_N)
            kt = (_dot_nt(wkt_ref[rows, :], h) * RET_KEY_DIM ** -0.5).astype(BF16)
            for cc in range(x.shape[0] // RET_CHUNK):
                kt_ref[cc, rows, :] = kt[:, cc * RET_CHUNK:(cc + 1) * RET_CHUNK]
        return step

    steps = [chunk(q_ref, 0, c) for c in range(RET_QK_WIDTH // MXU_N)]
    steps += [key_chunk(c) for c in range(RET_QK_WIDTH // MXU_N)]
    steps += [chunk(v_ref, 2 * RET_QK_WIDTH, c) for c in range(RET_V_WIDTH // MXU_N)]
    steps += [chunk(z_ref, 2 * RET_QK_WIDTH + RET_V_WIDTH, c) for c in range(RET_V_WIDTH // MXU_N)]
    return steps


def _run_interleaved(main, side):
    done = 0
    for k, step in enumerate(main):
        step()
        due = (k + 1) * len(side) // len(main)
        for s in side[done:due]:
            s()
        done = due


def _pos(shape, axis):
    return lax.broadcasted_iota(jnp.int32, shape, axis).astype(F32)


def _ret_tables_kernel(lgf_ref, lgb_ref, decay_ref, row_ref, col_ref, cdec_ref):
    h = pl.program_id(0)
    lg_f = lgf_ref[h]
    lg_b = lgb_ref[h]
    c = RET_CHUNK
    diff = _pos((c, c), 0) - _pos((c, c), 1)
    fwd = jnp.where(diff >= 0, jnp.exp(jnp.maximum(diff, 0.0) * lg_f), 0.0)
    bwd = jnp.where(diff <= 0, jnp.exp(jnp.maximum(-diff, 0.0) * lg_b), 0.0)
    decay_ref[...] = fwd + bwd
    j = _pos((RET_TAB_ROWS, c), 1)
    row_ref[0] = jnp.exp((c - 1.0 - j) * lg_f)
    row_ref[1] = jnp.exp(j * lg_b)
    i = _pos((c, LANES), 0)
    col_ref[0] = jnp.exp((i + 1.0) * lg_f)
    col_ref[1] = jnp.exp((c - i) * lg_b)
    full = jnp.full((RET_TAB_ROWS, RET_VAL_DIM), float(c), F32)
    cdec_ref[0] = jnp.exp(full * lg_f)
    cdec_ref[1] = jnp.exp(full * lg_b)


def _ret_tables(lg_f, lg_b):
    smem = pl.BlockSpec(memory_space=pltpu.SMEM)
    c = RET_CHUNK
    shapes = ((c, c), (2, RET_TAB_ROWS, c), (2, c, LANES), (2, RET_TAB_ROWS, RET_VAL_DIM))
    return pl.pallas_call(
        _ret_tables_kernel,
        out_shape=tuple(jax.ShapeDtypeStruct((RET_HEADS,) + s, F32) for s in shapes),
        grid=(RET_HEADS,),
        in_specs=[smem, smem],
        out_specs=tuple(pl.BlockSpec((None,) + s, lambda h, n=len(s): (h,) + (0,) * n) for s in shapes),
        compiler_params=_params("arbitrary"),
        name="ret_tables",
    )(lg_f, lg_b)


def _group_norm_gate(o, gn, z):
    mu = jnp.mean(o, axis=-1, keepdims=True)
    var = jnp.mean(jnp.square(o - mu), axis=-1, keepdims=True)
    on = (o - mu) * lax.rsqrt(var + EPS)
    return (on * gn * _silu(z.astype(F32))).astype(BF16)


def _ret_ctx_kernel(q_ref, kt_ref, v_ref, z_ref, gn_ref, decay_ref, row_ref, y_ref, sf_ref, sb_ref):
    for s in range(RET_CTX_SEQS):
        rows = slice(s * SEQ, (s + 1) * SEQ)
        for h in range(RET_HEADS):
            kc = slice(h * RET_KEY_DIM, (h + 1) * RET_KEY_DIM)
            vc = slice(h * RET_VAL_DIM, (h + 1) * RET_VAL_DIM)
            kt = kt_ref[s, kc, :]
            v = v_ref[rows, vc]
            att = (_dot(q_ref[rows, kc], kt) * decay_ref[h]).astype(BF16)
            y_ref[rows, vc] = _group_norm_gate(_dot(att, v), gn_ref[:, vc], z_ref[rows, vc])
            ktf = kt.astype(F32)
            sf_ref[s, h] = _dot((ktf * row_ref[h, 0, 0:1, :]).astype(BF16), v)
            sb_ref[s, h] = _dot((ktf * row_ref[h, 1, 0:1, :]).astype(BF16), v)


def _ret_ctx(q, kt, v, z, gn, decay, row):
    c = RET_CHUNK
    n = RET_CTX_SEQS
    wide = lambda w: pl.BlockSpec((n * SEQ, w), lambda b: (b, 0))
    st_spec = pl.BlockSpec((n, RET_HEADS, RET_KEY_DIM, RET_VAL_DIM), lambda b: (b, 0, 0, 0))
    st_shape = jax.ShapeDtypeStruct((BATCH, RET_HEADS, RET_KEY_DIM, RET_VAL_DIM), F32)
    return pl.pallas_call(
        _ret_ctx_kernel,
        out_shape=(jax.ShapeDtypeStruct((N_PROMPT_TOK, RET_V_WIDTH), BF16), st_shape, st_shape),
        grid=(BATCH // n,),
        in_specs=[wide(RET_QK_WIDTH), pl.BlockSpec((n, RET_QK_WIDTH, c), lambda b: (b, 0, 0)),
                  wide(RET_V_WIDTH), wide(RET_V_WIDTH), _const_spec((1, RET_V_WIDTH)),
                  _const_spec((RET_HEADS, c, c)), _const_spec((RET_HEADS, 2, RET_TAB_ROWS, c))],
        out_specs=(wide(RET_V_WIDTH), st_spec, st_spec),
        compiler_params=_params("arbitrary", vmem=VMEM_LIMIT),
        name="ret_ctx",
    )(q, kt, v, z, gn, decay, row)


def _ret_lat_kernel(q_ref, kt_ref, v_ref, z_ref, gn_ref, decay_ref, row_ref, col_ref, cdec_ref,
                    s0f_ref, s0b_ref, y_ref, sf_all, sb_all, sf_acc, sb_acc):
    c = RET_CHUNK
    n_chunks = DEC_SEQ // c
    rows_of = lambda ci: pl.ds(pl.multiple_of(ci * c, c), c)

    sf_acc[...] = s0f_ref[...]
    sb_acc[...] = s0b_ref[...]

    def scan_step(i, carry):
        cf = i
        cb = n_chunks - 1 - i
        sf_all[cf] = sf_acc[...].astype(BF16)
        sb_all[cb] = sb_acc[...].astype(BF16)
        uf = _dot((kt_ref[cf].astype(F32) * row_ref[0, 0:1, :]).astype(BF16), v_ref[rows_of(cf), :])
        ub = _dot((kt_ref[cb].astype(F32) * row_ref[1, 0:1, :]).astype(BF16), v_ref[rows_of(cb), :])
        sf_acc[...] = sf_acc[...] * cdec_ref[0, 0:1, :] + uf
        sb_acc[...] = sb_acc[...] * cdec_ref[1, 0:1, :] + ub
        return carry

    lax.fori_loop(0, n_chunks, scan_step, 0)

    def out_step(ci, carry):
        rows = rows_of(ci)
        q = q_ref[rows, :]
        qf = q.astype(F32)
        qdec_f = jnp.concatenate([col_ref[0]] * (RET_KEY_DIM // LANES), axis=1)
        qdec_b = jnp.concatenate([col_ref[1]] * (RET_KEY_DIM // LANES), axis=1)
        att = (_dot(q, kt_ref[ci]) * decay_ref[...]).astype(BF16)
        o = (_dot(att, v_ref[rows, :])
             + _dot((qf * qdec_f).astype(BF16), sf_all[ci])
             + _dot((qf * qdec_b).astype(BF16), sb_all[ci]))
        y_ref[rows, :] = _group_norm_gate(o, gn_ref[...], z_ref[rows, :])
        return carry

    lax.fori_loop(0, n_chunks, out_step, 0, unroll=2)


def _ret_lat(q, kt, v, z, gn, decay, row, col, cdec, s0f, s0b):
    c = RET_CHUNK
    n_chunks = DEC_SEQ // c
    row0 = N_PROMPT_TOK // DEC_SEQ
    qk_spec = pl.BlockSpec((DEC_SEQ, RET_KEY_DIM), lambda b, h: (row0 + b, h))
    v_spec = pl.BlockSpec((DEC_SEQ, RET_VAL_DIM), lambda b, h: (row0 + b, h))
    st_spec = pl.BlockSpec((None, None, RET_KEY_DIM, RET_VAL_DIM), lambda b, h: (b, h, 0, 0))
    tab = lambda *s: pl.BlockSpec((None,) + s, lambda b, h: (h,) + (0,) * len(s))
    states = pltpu.VMEM((n_chunks, RET_KEY_DIM, RET_VAL_DIM), BF16)
    acc = pltpu.VMEM((RET_KEY_DIM, RET_VAL_DIM), F32)
    return pl.pallas_call(
        _ret_lat_kernel,
        out_shape=jax.ShapeDtypeStruct((N_SAMPLE_TOK, RET_V_WIDTH), BF16),
        grid=(DEC_BATCH, RET_HEADS),
        in_specs=[qk_spec,
                  pl.BlockSpec((n_chunks, RET_KEY_DIM, c), lambda b, h: (row0 + b, h, 0)),
                  v_spec, v_spec, pl.BlockSpec((1, RET_VAL_DIM), lambda b, h: (0, h)),
                  tab(c, c), tab(2, RET_TAB_ROWS, c), tab(2, c, LANES), tab(2, RET_TAB_ROWS, RET_VAL_DIM),
                  st_spec, st_spec],
        out_specs=pl.BlockSpec((DEC_SEQ, RET_VAL_DIM), lambda b, h: (b, h)),
        scratch_shapes=[states, states, acc, acc],
        compiler_params=_params("arbitrary", "arbitrary", vmem=VMEM_LIMIT),
        name="ret_lat",
    )(q, kt, v, z, gn, decay, row, col, cdec, s0f, s0b)


def _rope_tables(tm):
    n_rows = DEC_SEQ // GRID_W
    rows = jnp.repeat(jnp.arange(n_rows), GRID_W).astype(F32)
    cols = jnp.tile(jnp.arange(GRID_W), n_rows).astype(F32)
    half = ATTN_HEAD_DIM // 4
    inv = ROPE_BASE ** (-jnp.arange(half, dtype=F32) / half)
    ang_r = rows[:, None] * inv[None, :]
    ang_c = cols[:, None] * inv[None, :]
    cos = jnp.concatenate([jnp.cos(ang_r), jnp.cos(ang_r), jnp.cos(ang_c), jnp.cos(ang_c)], axis=-1)
    sin = jnp.concatenate([-jnp.sin(ang_r), jnp.sin(ang_r), -jnp.sin(ang_c), jnp.sin(ang_c)], axis=-1)
    cos = jnp.concatenate([jnp.ones((tm, ATTN_HEAD_DIM), F32), cos], axis=0)
    sin = jnp.concatenate([jnp.zeros((tm, ATTN_HEAD_DIM), F32), sin], axis=0)
    return jnp.tile(cos, (1, 2)), jnp.tile(sin, (1, 2))


def kernel(x_prompt, x_sample, cache_k, cache_v, state_fwd, state_bwd, c, c_ctx, norm_g, ada_w, ada_b, attn_w_in, attn_w_out, attn_sink, pool_w_in, pool_w_grp, pool_scale, pool_w_out, ret_w_in, ret_decay_fwd, ret_decay_bwd, ret_gn_g, ret_w_out, final_g):
    x_parts = (x_prompt.reshape(N_PROMPT_TOK, D_MODEL), x_sample.reshape(N_SAMPLE_TOK, D_MODEL))
    cond = jnp.concatenate([c_ctx[None, :], c,
                            jnp.zeros((N_COND - 1 - DEC_BATCH, D_MODEL), F32)], axis=0)
    mods = _ada_table(cond.T, ada_w, ada_b).reshape(DEPTH, N_COND, 1, 3 * D_MODEL)
    cos_t, sin_t = _rope_tables(PROJ.tm)

    to_kernel = lambda a: jnp.transpose(a, (0, 1, 3, 4, 2))
    from_kernel = lambda a: jnp.transpose(
        a.reshape(a.shape[0], a.shape[1], ATTN_KV_HEADS, ATTN_HEAD_DIM, a.shape[3]), (0, 1, 4, 2, 3))
    ctx_k, ctx_v = to_kernel(cache_k), to_kernel(cache_v)

    assert DEPTH % N_MIXERS == 1, "the layer stack must end on an attention layer"
    caches = ()
    new_sf = new_sb = None
    mixed = None
    for i in range(DEPTH):
        kind, j = i % N_MIXERS, i // N_MIXERS
        g = norm_g[i].reshape(1, D_MODEL)
        mod = mods[i]
        if kind == 0:
            outs = _attn_in(mixed, x_parts, g, mod, attn_w_in, j, cos_t, sin_t, tuple(caches))
            if mixed is not None:
                x_parts, outs = (outs[0],), outs[1:]
            q, k4, v4, z, *caches = outs
            y_parts = (_ctx_attn(attn_sink[j], q, k4, v4, z),
                       _lat_attn(attn_sink[j], q, k4, v4, ctx_k, ctx_v, j, z))
            mixed = Mixed(y_parts, attn_w_out, j, mod)
        elif kind == 1:
            x, u, z = _pool_in(mixed, x_parts, g, mod, pool_w_in, j)
            x_parts, mixed, pooled = (x,), None, (u, z, j, mod)
        else:
            u, z, jp, mod_pool = pooled
            lg_f = jax.nn.log_sigmoid(ret_decay_fwd[j].astype(F32))
            lg_b = jax.nn.log_sigmoid(ret_decay_bwd[j].astype(F32))
            gn = ret_gn_g[j].reshape(1, RET_V_WIDTH)
            x, q, kt, v, z = _pool_ret(u, z, pool_w_grp, pool_scale[jp].reshape(1, D_MODEL), pool_w_out, jp,
                                       x_parts[0], mod_pool, g, mod, ret_w_in[j].astype(BF16))
            x_parts = (x,)
            decay, row, col, cdec = _ret_tables(lg_f, lg_b)
            y_ctx, new_sf, new_sb = _ret_ctx(q, kt, v, z, gn, decay, row)
            y_parts = (y_ctx, _ret_lat(q, kt, v, z, gn, decay, row, col, cdec,
                                       state_fwd[:, j], state_bwd[:, j]))
            mixed = Mixed(y_parts, ret_w_out, j, mod)
    y_prompt, y_sample = _out_proj_final(mixed, x_parts[0], final_g.reshape(1, D_MODEL))
    new_k, new_v = caches
    return (y_prompt.reshape(BATCH, SEQ, D_MODEL), y_sample.reshape(DEC_BATCH, DEC_SEQ, D_MODEL),
            from_kernel(new_k), from_kernel(new_v), new_sf[:, None], new_sb[:, None])
```

```python
import functools
from typing import NamedTuple

import jax
import jax.numpy as jnp
from jax import lax
from jax.experimental import pallas as pl
from jax.experimental.pallas import tpu as pltpu

F32 = jnp.float32
BF16 = jnp.bfloat16

D_MODEL = 1024
BATCH = 16
SEQ = 256
DEPTH = 4
DEC_BATCH = 2
DEC_SEQ = 2048
PAST_LEN = 512
GRID_W = 64
N_MIXERS = 3
ATTN_HEADS = 16
ATTN_KV_HEADS = 4
ATTN_HEAD_DIM = 64
ATTN_GROUP = 4
ATTN_WIDTH = 1024
ATTN_KV_WIDTH = 256
WINDOW = 128
ROPE_BASE = 10000.0
POOL_WINDOWS = (2, 4, 8, 16)
POOL_GROUP_DIM = 256
RET_HEADS = 4
RET_KEY_DIM = 256
RET_VAL_DIM = 512
RET_QK_WIDTH = 1024
RET_V_WIDTH = 2048
EPS = 1e-6
NEG_INF = -1e30
LOG2_E = 1.4426950408889634

N_PROMPT_TOK = BATCH * SEQ
N_SAMPLE_TOK = DEC_BATCH * DEC_SEQ
N_TOK = N_PROMPT_TOK + N_SAMPLE_TOK
N_COND = 8
LANES = 128
MXU_N = 256
Q_BLOCK = 128
Q_SUB = 4
RET_CHUNK = 256
RET_TAB_ROWS = 8
RET_CTX_SEQS = 2
POOL_HALO = 8
POOL_SUB = SEQ
VMEM_LIMIT = 48 * 1024 * 1024
FUSED_VMEM_LIMIT = 58 * 1024 * 1024


class Tiling(NamedTuple):
    tm: int

    @property
    def n_tiles(self):
        return N_TOK // self.tm

    @property
    def n_prompt(self):
        return N_PROMPT_TOK // self.tm

    @property
    def per_dec_seq(self):
        return DEC_SEQ // self.tm

    def cond(self, i):
        return jnp.where(i < self.n_prompt, 0, 1 + (i - self.n_prompt) // self.per_dec_seq)

    def seq_tile(self, i):
        return jnp.where(i < self.n_prompt, 0, (i - self.n_prompt) % self.per_dec_seq)

    def rows(self, width):
        return pl.BlockSpec((self.tm, width), lambda i: (i, 0))

    def prompt_rows(self, width):
        return pl.BlockSpec((self.tm, width), lambda i: (jnp.minimum(i, self.n_prompt - 1), 0))

    def sample_rows(self, width):
        return pl.BlockSpec((self.tm, width), lambda i: (jnp.maximum(i - self.n_prompt, 0), 0))

    def mod(self, part):
        return pl.BlockSpec((None, 1, D_MODEL), lambda i: (self.cond(i), 0, part))


PROJ = Tiling(512)
FINAL = Tiling(1024)


def _silu(z):
    return z * (1.0 / (1.0 + jnp.exp(-z)))


def _dot(a, b):
    return jnp.dot(a, b, preferred_element_type=F32)


def _dot_nt(a, b):
    return lax.dot_general(a, b, (((1,), (1,)), ((), ())), preferred_element_type=F32)


def _params(*sem, vmem=None):
    return pltpu.CompilerParams(dimension_semantics=sem, vmem_limit_bytes=vmem)


def _const_spec(shape):
    nd = len(shape)
    return pl.BlockSpec(shape, lambda *_: (0,) * nd, pipeline_mode=pl.Buffered(1))


def _part_specs(t, parts, width):
    if len(parts) == 1:
        return [t.rows(width)]
    return [t.prompt_rows(width), t.sample_rows(width)]


def _read_parts(t, refs):
    if len(refs) == 1:
        return refs[0][...]
    return jnp.where(pl.program_id(0) < t.n_prompt, refs[0][...], refs[1][...])


def _ada_kernel(cond_ref, w_ref, b_ref, o_ref):
    s = _silu(cond_ref[...])
    w = w_ref[...]
    rows = [jnp.sum(s[:, c:c + 1] * w, axis=0, keepdims=True) + b_ref[...] for c in range(1 + DEC_BATCH)]
    rows.append(jnp.zeros((N_COND - len(rows), w.shape[1]), F32))
    o_ref[...] = jnp.concatenate(rows, axis=0)


def _ada_table(cond, ada_w, ada_b):
    tn = 3 * D_MODEL // 2
    return pl.pallas_call(
        _ada_kernel,
        out_shape=jax.ShapeDtypeStruct((DEPTH, N_COND, 3 * D_MODEL), F32),
        grid=(DEPTH, 3 * D_MODEL // tn),
        in_specs=[
            pl.BlockSpec((D_MODEL, N_COND), lambda l, n: (0, 0)),
            pl.BlockSpec((None, D_MODEL, tn), lambda l, n: (l, 0, n)),
            pl.BlockSpec((None, 1, tn), lambda l, n: (l, 0, n)),
        ],
        out_specs=pl.BlockSpec((None, N_COND, tn), lambda l, n: (l, 0, n)),
        compiler_params=_params("arbitrary", "arbitrary", vmem=VMEM_LIMIT),
        name="ada_table",
    )(cond, ada_w, ada_b.reshape(DEPTH, 1, 3 * D_MODEL))


def _rms(x, g):
    return x * lax.rsqrt(jnp.mean(x * x, axis=-1, keepdims=True) + EPS) * g


def _norm_mod(x, g_ref, sh_ref, sc_ref):
    return (_rms(x, g_ref[...]) * (1.0 + sc_ref[...]) + sh_ref[...]).astype(BF16)


def _proj_chunks(h, w_ref, lo, width, store):
    for c in range(width // MXU_N):
        store(c, _dot(h, w_ref[:, lo + c * MXU_N:lo + (c + 1) * MXU_N]))


def _layer_spec(w, j):
    nd = w.ndim - 1
    return pl.BlockSpec((None,) + w.shape[1:], lambda *_: (j,) + (0,) * nd, pipeline_mode=pl.Buffered(1))


def _cast_weight_once(w_ref, wb_ref):
    @pl.when(pl.program_id(0) == 0)
    def _():
        rows = wb_ref.shape[-2]
        for r in range(0, rows, MXU_N):
            wb_ref[..., r:r + MXU_N, :] = w_ref[..., r:r + MXU_N, :].astype(BF16)


class Mixed(NamedTuple):
    y_parts: tuple
    w_out: jax.Array
    j: int
    mod: jax.Array


def _residual_specs(t, mixed, x_parts):
    k = mixed.w_out.shape[1]
    return (_part_specs(t, mixed.y_parts, k) + [_layer_spec(mixed.w_out, mixed.j)]
            + _part_specs(t, x_parts, D_MODEL) + [t.mod(2)])


def _residual_args(mixed, x_parts):
    return (*mixed.y_parts, mixed.w_out, *x_parts, mixed.mod)


def _residual_update(t, refs, n_y, n_x, wo_ref, xo_ref):
    y_refs, wof_ref, x_refs, gate_ref = refs[:n_y], refs[n_y], refs[n_y + 1:n_y + 1 + n_x], refs[n_y + 1 + n_x]
    _cast_weight_once(wof_ref, wo_ref)
    x = _read_parts(t, x_refs) + gate_ref[...] * _dot(_read_parts(t, y_refs), wo_ref[...])
    xo_ref[...] = x
    return x


def _rep4(a, h):
    half = a[:, (h // 2) * LANES:(h // 2 + 1) * LANES]
    lane = lax.broadcasted_iota(jnp.int32, half.shape, 1)
    keep = (lane < ATTN_HEAD_DIM) if h % 2 == 0 else (lane >= ATTN_HEAD_DIM)
    m = jnp.where(keep, half, 0.0)
    s = m + pltpu.roll(m, ATTN_HEAD_DIM, 1)
    return jnp.concatenate([s, s], axis=1)


def _attn_in_kernel(*refs, n_y, n_x, n_alias):
    t = PROJ
    n_res = n_y + n_x + 2 if n_y else n_x
    res_refs, refs = refs[:n_res], refs[n_res:]
    (g_ref, sh_ref, sc_ref, wf_ref, cos_ref, sin_ref), refs = refs[:6], refs[6 + n_alias:]
    if n_y:
        xo_ref, q_ref, k4_ref, v4_ref, z_ref, kc_ref, vc_ref, wo_ref, w_ref = refs
        x = _residual_update(t, res_refs, n_y, n_x, wo_ref, xo_ref)
    else:
        q_ref, k4_ref, v4_ref, z_ref, kc_ref, vc_ref, w_ref = refs
        x = _read_parts(t, res_refs)
    i = pl.program_id(0)
    _cast_weight_once(wf_ref, w_ref)
    h = _norm_mod(x, g_ref, sh_ref, sc_ref)
    cos = cos_ref[...]
    sin = sin_ref[...]
    lane = lax.broadcasted_iota(jnp.int32, (t.tm, LANES), 1)
    first = (lane % (ATTN_HEAD_DIM // 2)) < ATTN_HEAD_DIM // 4

    def rope(a):
        rot = jnp.where(first, pltpu.roll(a, LANES - ATTN_HEAD_DIM // 4, 1),
                        pltpu.roll(a, ATTN_HEAD_DIM // 4, 1))
        return a * cos + rot * sin

    def rope_wide(a):
        return jnp.concatenate(
            [rope(a[:, s * LANES:(s + 1) * LANES]) for s in range(MXU_N // LANES)], axis=1)

    scale = ATTN_HEAD_DIM ** -0.5 * LOG2_E

    def store_q(c, a):
        q_ref[:, c * MXU_N:(c + 1) * MXU_N] = (rope_wide(a) * scale).astype(BF16)

    def store_z(c, a):
        z_ref[:, c * MXU_N:(c + 1) * MXU_N] = a.astype(BF16)

    _proj_chunks(h, w_ref, 0, ATTN_WIDTH, store_q)
    k = rope_wide(_dot(h, w_ref[:, ATTN_WIDTH:ATTN_WIDTH + ATTN_KV_WIDTH]))
    v = _dot(h, w_ref[:, ATTN_WIDTH + ATTN_KV_WIDTH:ATTN_WIDTH + 2 * ATTN_KV_WIDTH])
    for hh in range(ATTN_KV_HEADS):
        k4_ref[:, hh * MXU_N:(hh + 1) * MXU_N] = _rep4(k, hh).astype(BF16)
        v4_ref[:, hh * MXU_N:(hh + 1) * MXU_N] = _rep4(v, hh).astype(BF16)
    _proj_chunks(h, w_ref, ATTN_WIDTH + 2 * ATTN_KV_WIDTH, ATTN_WIDTH, store_z)

    @pl.when(i < t.n_prompt)
    def _():
        for s in range(t.tm // SEQ):
            kc_ref[s] = k[s * SEQ:(s + 1) * SEQ, :].T
            vc_ref[s] = v[s * SEQ:(s + 1) * SEQ, :].T


def _attn_in(mixed, x_parts, g, mod, w_all, j, cos_t, sin_t, caches):
    t = PROJ
    n_in = 2 * ATTN_WIDTH + 2 * ATTN_KV_WIDTH
    per = t.tm // SEQ
    rope_spec = pl.BlockSpec(
        (t.tm, LANES), lambda i: (jnp.where(i < t.n_prompt, 0, 1 + t.seq_tile(i)), 0))
    wide = jax.ShapeDtypeStruct((N_TOK, ATTN_WIDTH), BF16)
    n_attn = w_all.shape[0]
    cache = jax.ShapeDtypeStruct((BATCH, n_attn, ATTN_KV_WIDTH, SEQ), F32)
    cache_spec = pl.BlockSpec((per, None, ATTN_KV_WIDTH, SEQ),
                              lambda i: (jnp.minimum(i, t.n_prompt - 1), j, 0, 0))
    n_x = len(x_parts)
    if mixed is None:
        n_y, res_specs, res_args, res_out, res_ospecs, res_scratch = 0, _part_specs(t, x_parts, D_MODEL), x_parts, (), (), []
    else:
        n_y = len(mixed.y_parts)
        res_specs, res_args = _residual_specs(t, mixed, x_parts), _residual_args(mixed, x_parts)
        res_out = (jax.ShapeDtypeStruct((N_TOK, D_MODEL), F32),)
        res_ospecs = (t.rows(D_MODEL),)
        res_scratch = [pltpu.VMEM(mixed.w_out.shape[1:], BF16)]
    n_front = len(res_specs) + 6
    return pl.pallas_call(
        functools.partial(_attn_in_kernel, n_y=n_y, n_x=n_x, n_alias=len(caches)),
        out_shape=res_out + (wide, wide, wide, wide, cache, cache),
        grid=(t.n_tiles,),
        in_specs=res_specs + [
            _const_spec((1, D_MODEL)), t.mod(0), t.mod(1), _layer_spec(w_all, j),
            rope_spec, rope_spec] + [pl.BlockSpec(memory_space=pl.ANY)] * len(caches),
        out_specs=res_ospecs + (t.rows(ATTN_WIDTH),) * 4 + (cache_spec,) * 2,
        scratch_shapes=res_scratch + [pltpu.VMEM((D_MODEL, n_in), BF16)],
        input_output_aliases={n_front + c: len(res_out) + 4 + c for c in range(len(caches))},
        compiler_params=_params("arbitrary", vmem=FUSED_VMEM_LIMIT),
        name="attn_in",
    )(*res_args, g, mod, mod, w_all, cos_t, sin_t, *caches)


def _stack_group_queries(q):
    qf = q.astype(F32)
    chunk = lax.broadcasted_iota(jnp.int32, qf.shape, 1) // ATTN_HEAD_DIM
    return jnp.concatenate(
        [jnp.where(chunk == g, qf, 0.0) for g in range(ATTN_GROUP)], axis=0).astype(BF16)


def _gather_group_outputs(o, rows):
    chunk = lax.broadcasted_iota(jnp.int32, (rows, MXU_N), 1) // ATTN_HEAD_DIM
    acc = jnp.zeros((rows, MXU_N), F32)
    for g in range(ATTN_GROUP):
        acc = acc + jnp.where(chunk == g, o[g * rows:(g + 1) * rows], 0.0)
    return acc


def _sink_column(sink_ref, h, rows):
    grp = lax.broadcasted_iota(jnp.int32, (ATTN_GROUP * rows, 1), 0) // rows
    col = jnp.zeros((ATTN_GROUP * rows, 1), F32)
    for g in range(ATTN_GROUP):
        col = jnp.where(grp == g, sink_ref[h * ATTN_GROUP + g] * LOG2_E, col)
    return col


def _chunk_rows(dtype):
    chunk = lax.broadcasted_iota(jnp.int32, (1, MXU_N), 1) // ATTN_HEAD_DIM
    return [(chunk == g).astype(F32).astype(dtype) for g in range(ATTN_GROUP)]


def _block_diag_rows(x4):
    return jnp.concatenate([x4 * m for m in _chunk_rows(x4.dtype)], axis=0)


def _ctx_attn_kernel(sink_ref, q_ref, k4_ref, v4_ref, z_ref, y_ref):
    chunk = lax.broadcasted_iota(jnp.int32, (SEQ, MXU_N), 1) // ATTN_HEAD_DIM
    for h in range(ATTN_KV_HEADS):
        cols = slice(h * MXU_N, (h + 1) * MXU_N)
        s = _dot_nt(q_ref[:, cols], _block_diag_rows(k4_ref[:, cols]))
        inv = jnp.zeros((SEQ, MXU_N), F32)
        probs = []
        for g in range(ATTN_GROUP):
            sg = s[:, g * SEQ:(g + 1) * SEQ]
            sk = sink_ref[h * ATTN_GROUP + g] * LOG2_E
            m = jnp.maximum(jnp.max(sg, axis=1, keepdims=True), sk)
            e = jnp.exp2(sg - m)
            den = jnp.sum(e, axis=1, keepdims=True) + jnp.exp2(sk - m)
            probs.append(e.astype(BF16))
            inv = jnp.where(chunk == g, 1.0 / den, inv)
        o = _dot(jnp.concatenate(probs, axis=1), _block_diag_rows(v4_ref[:, cols]))
        y_ref[:, cols] = (o * inv * _silu(z_ref[:, cols].astype(F32))).astype(BF16)


def _ctx_attn(sink, q, k4, v4, z):
    spec = pl.BlockSpec((SEQ, ATTN_WIDTH), lambda b: (b, 0))
    return pl.pallas_call(
        _ctx_attn_kernel,
        out_shape=jax.ShapeDtypeStruct((N_PROMPT_TOK, ATTN_WIDTH), BF16),
        grid=(BATCH,),
        in_specs=[pl.BlockSpec(memory_space=pltpu.SMEM), spec, spec, spec, spec],
        out_specs=spec,
        compiler_params=_params("arbitrary"),
        name="ctx_attn",
    )(sink, q, k4, v4, z)


LAT_STEP = Q_SUB * Q_BLOCK
LAT_PER_SEQ = DEC_SEQ // LAT_STEP
LAT_BLOCKS = DEC_BATCH * ATTN_KV_HEADS * LAT_PER_SEQ


def _lat_block(blk):
    return (blk // (ATTN_KV_HEADS * LAT_PER_SEQ), (blk // LAT_PER_SEQ) % ATTN_KV_HEADS, blk % LAT_PER_SEQ)


def _band_rows(qb):
    return pl.ds(pl.multiple_of(qb * Q_BLOCK, Q_BLOCK), 3 * Q_BLOCK)


def _pad_sequence(dst, src):
    zeros = jnp.zeros((Q_BLOCK, MXU_N), BF16)
    dst[0:Q_BLOCK, :] = zeros
    dst[Q_BLOCK:Q_BLOCK + DEC_SEQ, :] = src[...]
    dst[Q_BLOCK + DEC_SEQ:, :] = zeros


def _lat_attn_kernel(sink_ref, q_ref, k4_ref, v4_ref, kc_ref, vc_ref, z_ref, y_ref,
                     kp, vp, k4c, v4c, sc_a, sl_a, sc_b, sl_b, e_scr):
    t = pl.program_id(0)
    _, _, n1 = _lat_block(jnp.minimum(t, LAT_BLOCKS - 1))
    _, h0, n0 = _lat_block(jnp.maximum(t - 1, 0))
    n_blocks = DEC_SEQ // Q_BLOCK

    @pl.when(t == 0)
    def _():
        sc_b[...] = jnp.zeros(sc_b.shape, F32)
        sl_b[...] = jnp.zeros(sl_b.shape, F32)

    @pl.when(n1 == 0)
    def _():
        _pad_sequence(kp, k4_ref)
        k4c[...] = jnp.concatenate([kc_ref[...]] * ATTN_GROUP, axis=0).astype(BF16)

    @pl.when(n0 == 0)
    def _():
        _pad_sequence(vp, v4_ref)
        v4c[...] = jnp.concatenate([vc_ref[...]] * ATTN_GROUP, axis=0).T.astype(BF16)

    @pl.when(t % 2 == 0)
    def _():
        _lat_stages(sink_ref, q_ref, z_ref, y_ref, kp, vp, k4c, v4c, e_scr, n1, h0, n0,
                    sc_a, sl_a, sc_b, sl_b)

    @pl.when(t % 2 == 1)
    def _():
        _lat_stages(sink_ref, q_ref, z_ref, y_ref, kp, vp, k4c, v4c, e_scr, n1, h0, n0,
                    sc_b, sl_b, sc_a, sl_a)


def _lat_stages(sink_ref, q_ref, z_ref, y_ref, kp, vp, k4c, v4c, e_scr, n1, h0, n0,
                sc_w, sl_w, sc_r, sl_r):
    n_blocks = DEC_SEQ // Q_BLOCK
    for sub in range(Q_SUB):
        qs = _stack_group_queries(q_ref[sub * Q_BLOCK:(sub + 1) * Q_BLOCK, :])
        sc_w[sub] = _dot(qs, k4c[...])
        sl_w[sub] = _dot_nt(qs, kp[_band_rows(n1 * Q_SUB + sub), :])

    rows4 = ATTN_GROUP * Q_BLOCK
    r = lax.broadcasted_iota(jnp.int32, (rows4, Q_BLOCK), 0) % Q_BLOCK
    c = lax.broadcasted_iota(jnp.int32, (rows4, Q_BLOCK), 1)
    in_left = c >= r
    in_right = c <= r
    sk = _sink_column(sink_ref, h0, Q_BLOCK)
    for sub in range(Q_SUB):
        qb = n0 * Q_SUB + sub
        band = _band_rows(qb)
        qrows = slice(sub * Q_BLOCK, (sub + 1) * Q_BLOCK)
        s_ctx = sc_r[sub]
        s_lat = sl_r[sub]
        blocks = [
            s_ctx,
            jnp.where(jnp.logical_and(in_left, qb > 0), s_lat[:, :Q_BLOCK], NEG_INF),
            s_lat[:, Q_BLOCK:2 * Q_BLOCK],
            jnp.where(jnp.logical_and(in_right, qb < n_blocks - 1), s_lat[:, 2 * Q_BLOCK:], NEG_INF),
        ]
        m = sk
        for s in blocks:
            m = jnp.maximum(m, jnp.max(s, axis=1, keepdims=True))
        den = jnp.exp2(sk - m)
        lo = 0
        for s in blocks:
            e = jnp.exp2(s - m)
            den = den + jnp.sum(e, axis=1, keepdims=True)
            e_scr[sub, :, lo:lo + s.shape[1]] = e.astype(BF16)
            lo += s.shape[1]
        o = (_dot(e_scr[sub, :, :PAST_LEN], v4c[...])
             + _dot(e_scr[sub, :, PAST_LEN:], vp[band, :])) * (1.0 / den)
        acc = _gather_group_outputs(o, Q_BLOCK)
        y_ref[qrows, :] = (acc * _silu(z_ref[qrows, :].astype(F32))).astype(BF16)


def _lat_attn(sink, q, k4, v4, kc, vc, j, z):
    padded = DEC_SEQ + 2 * Q_BLOCK
    rows4 = ATTN_GROUP * Q_BLOCK
    row0 = N_PROMPT_TOK // LAT_STEP
    seq0 = N_PROMPT_TOK // DEC_SEQ
    stage1 = lambda t: _lat_block(jnp.minimum(t, LAT_BLOCKS - 1))
    stage2 = lambda t: _lat_block(jnp.maximum(t - 1, 0))

    def spec(shape, stage, index):
        return pl.BlockSpec(shape, lambda t: index(*stage(t)))

    tile = (LAT_STEP, MXU_N)
    seq = (DEC_SEQ, MXU_N)
    ctx = (None, None, None, ATTN_HEAD_DIM, PAST_LEN)
    return pl.pallas_call(
        _lat_attn_kernel,
        out_shape=jax.ShapeDtypeStruct((N_SAMPLE_TOK, ATTN_WIDTH), BF16),
        grid=(LAT_BLOCKS + 1,),
        in_specs=[pl.BlockSpec(memory_space=pltpu.SMEM),
                  spec(tile, stage1, lambda b, h, n: (row0 + b * LAT_PER_SEQ + n, h)),
                  spec(seq, stage1, lambda b, h, n: (seq0 + b, h)),
                  spec(seq, stage2, lambda b, h, n: (seq0 + b, h)),
                  spec(ctx, stage1, lambda b, h, n: (b, j, h, 0, 0)),
                  spec(ctx, stage2, lambda b, h, n: (b, j, h, 0, 0)),
                  spec(tile, stage2, lambda b, h, n: (row0 + b * LAT_PER_SEQ + n, h))],
        out_specs=spec(tile, stage2, lambda b, h, n: (b * LAT_PER_SEQ + n, h)),
        scratch_shapes=[pltpu.VMEM((padded, MXU_N), BF16), pltpu.VMEM((padded, MXU_N), BF16),
                        pltpu.VMEM((MXU_N, PAST_LEN), BF16), pltpu.VMEM((PAST_LEN, MXU_N), BF16),
                        pltpu.VMEM((Q_SUB, rows4, PAST_LEN), F32),
                        pltpu.VMEM((Q_SUB, rows4, 3 * Q_BLOCK), F32),
                        pltpu.VMEM((Q_SUB, rows4, PAST_LEN), F32),
                        pltpu.VMEM((Q_SUB, rows4, 3 * Q_BLOCK), F32),
                        pltpu.VMEM((Q_SUB, rows4, PAST_LEN + 3 * Q_BLOCK), BF16)],
        compiler_params=_params("arbitrary", vmem=VMEM_LIMIT),
        name="lat_attn",
    )(sink, q, k4, v4, kc, vc, z)


def _out_final_kernel(yp_ref, ys_ref, wf_ref, x_ref, gate_ref, fg_ref, op_ref, os_ref, w_ref):
    t = FINAL
    i = pl.program_id(0)
    _cast_weight_once(wf_ref, w_ref)
    y = _read_parts(t, (yp_ref, ys_ref))
    r = _rms(x_ref[...] + gate_ref[...] * _dot(y, w_ref[...]), fg_ref[...])

    @pl.when(i < t.n_prompt)
    def _():
        op_ref[...] = r

    @pl.when(i >= t.n_prompt)
    def _():
        os_ref[...] = r


def _out_proj_final(mixed, x, final_g):
    t = FINAL
    return pl.pallas_call(
        _out_final_kernel,
        out_shape=(jax.ShapeDtypeStruct((N_PROMPT_TOK, D_MODEL), F32),
                   jax.ShapeDtypeStruct((N_SAMPLE_TOK, D_MODEL), F32)),
        grid=(t.n_tiles,),
        in_specs=_residual_specs(t, mixed, (x,)) + [_const_spec((1, D_MODEL))],
        out_specs=(t.prompt_rows(D_MODEL), t.sample_rows(D_MODEL)),
        scratch_shapes=[pltpu.VMEM(mixed.w_out.shape[1:], BF16)],
        compiler_params=_params("arbitrary", vmem=VMEM_LIMIT),
        name="out_proj_final",
    )(*_residual_args(mixed, (x,)), final_g)


def _pool_in_kernel(*refs, n_y, n_x):
    t = PROJ
    n_res = n_y + n_x + 2
    res_refs, (g_ref, sh_ref, sc_ref, wf_ref, xo_ref, u_ref, z_ref, wo_ref, w_ref) = refs[:n_res], refs[n_res:]
    x = _residual_update(t, res_refs, n_y, n_x, wo_ref, xo_ref)
    _cast_weight_once(wf_ref, w_ref)
    h = _norm_mod(x, g_ref, sh_ref, sc_ref)

    def store_u(c, a):
        u_ref[:, c * MXU_N:(c + 1) * MXU_N] = a

    def store_z(c, a):
        z_ref[:, c * MXU_N:(c + 1) * MXU_N] = a.astype(BF16)

    _proj_chunks(h, w_ref, 0, D_MODEL, store_u)
    _proj_chunks(h, w_ref, D_MODEL, D_MODEL, store_z)


def _pool_in(mixed, x_parts, g, mod, w_all, j):
    t = PROJ
    slab = jax.ShapeDtypeStruct((N_TOK, D_MODEL), F32)
    return pl.pallas_call(
        functools.partial(_pool_in_kernel, n_y=len(mixed.y_parts), n_x=len(x_parts)),
        out_shape=(slab, slab, jax.ShapeDtypeStruct((N_TOK, D_MODEL), BF16)),
        grid=(t.n_tiles,),
        in_specs=_residual_specs(t, mixed, x_parts) + [
            _const_spec((1, D_MODEL)), t.mod(0), t.mod(1), _layer_spec(w_all, j)],
        out_specs=(t.rows(D_MODEL),) * 3,
        scratch_shapes=[pltpu.VMEM(mixed.w_out.shape[1:], BF16), pltpu.VMEM((D_MODEL, 2 * D_MODEL), BF16)],
        compiler_params=_params("arbitrary", vmem=FUSED_VMEM_LIMIT),
        name="pool_in",
    )(*_residual_args(mixed, x_parts), g, mod, mod, w_all)


def _split_bf16(a):
    hi = a.astype(BF16)
    return hi, (a - hi.astype(F32)).astype(BF16)


def _band_ones(shape, lo, hi):
    d = lax.broadcasted_iota(jnp.int32, shape, 1) - lax.broadcasted_iota(jnp.int32, shape, 0)
    return jnp.logical_and(d >= lo, d <= hi).astype(F32).astype(BF16)


def _pool_ret_kernel(u_ref, up_ref, un_ref, z_ref, wgf_ref, ps_ref, wof_ref, x_ref, gate_ref,
                     g_ref, sh_ref, sc_ref, w_ref,
                     xo_ref, q_ref, kt_ref, v_ref, zr_ref,
                     y_ref, wg_ref, wo_ref, band_ref, wkt_ref):
    t = PROJ
    i = pl.program_id(0)
    _cast_weight_once(wgf_ref, wg_ref)
    _cast_weight_once(wof_ref, wo_ref)

    @pl.when(i == 0)
    def _():
        for g, w in enumerate(POOL_WINDOWS):
            band_ref[g] = _band_ones((POOL_SUB, POOL_SUB), -(w // 2), w - 1 - w // 2)
        for r in range(0, RET_QK_WIDTH, MXU_N):
            wk = w_ref[:, RET_QK_WIDTH + r:RET_QK_WIDTH + r + MXU_N]
            wkt_ref[r:r + MXU_N, :] = wk.astype(F32).T.astype(BF16)

    is_dec = i >= t.n_prompt
    st = t.seq_tile(i)
    seq_len = jnp.where(is_dec, DEC_SEQ, SEQ)
    h = POOL_HALO
    n_sub = t.tm // POOL_SUB
    def pool_steps(sub):
        rows = slice(sub * POOL_SUB, (sub + 1) * POOL_SUB)
        if sub == 0:
            before = jnp.where(jnp.logical_and(is_dec, st != 0), up_ref[...], 0.0)
        else:
            before = jnp.where(is_dec, u_ref[sub * POOL_SUB - h:sub * POOL_SUB, :], 0.0)
        if sub == n_sub - 1:
            after = jnp.where(jnp.logical_and(is_dec, st != t.per_dec_seq - 1), un_ref[...], 0.0)
        else:
            after = jnp.where(is_dec, u_ref[(sub + 1) * POOL_SUB:(sub + 1) * POOL_SUB + h, :], 0.0)
        pos0 = jnp.where(is_dec, st * t.tm + sub * POOL_SUB, 0)
        return _pool_core_steps(u_ref.at[rows], before, after, pos0, seq_len, z_ref.at[rows], ps_ref,
                                wg_ref, band_ref, y_ref.at[rows])

    for step in pool_steps(0):
        step()
    for sub in range(n_sub):
        rows = slice(sub * POOL_SUB, (sub + 1) * POOL_SUB)
        x = x_ref[rows, :] + gate_ref[...] * _dot(y_ref[rows, :], wo_ref[...])
        xo_ref[rows, :] = x
        proj = _ret_in_steps(x, g_ref, sh_ref, sc_ref, w_ref, wkt_ref, q_ref.at[rows],
                             kt_ref.at[pl.ds(sub * POOL_SUB // RET_CHUNK, POOL_SUB // RET_CHUNK)],
                             v_ref.at[rows], zr_ref.at[rows])
        _run_interleaved(proj, pool_steps(sub + 1) if sub + 1 < n_sub else [])


def _pool_core_steps(u_ref, before, after, pos0, seq_len, z_ref, ps_ref, wg_ref, band_ref, y_ref):
    n = u_ref.shape[0]
    h = POOL_HALO
    halo_hi, halo_lo = _split_bf16(jnp.concatenate([before, after], axis=0))
    pos = pos0 + lax.broadcasted_iota(jnp.int32, (n, 1), 0)
    rr = lax.broadcasted_iota(jnp.int32, (2 * h, 2 * h), 0)
    cc = lax.broadcasted_iota(jnp.int32, (2 * h, 2 * h), 1)

    def group(g, w):
        def step():
            left = w // 2
            right = w - 1 - left
            cols = slice(g * POOL_GROUP_DIM, (g + 1) * POOL_GROUP_DIM)
            u = u_ref[:, cols]
            s = _dot(band_ref[g], u.astype(BF16))
            top = jnp.logical_and(jnp.logical_and(rr < h, cc < h), cc - h >= rr - left)
            bot = jnp.logical_and(jnp.logical_and(rr >= h, cc >= h), cc - h <= rr - 2 * h + right)
            edge = jnp.logical_or(top, bot).astype(F32).astype(BF16)
            se = _dot(edge, halo_hi[:, cols]) + _dot(edge, halo_lo[:, cols])
            s = jnp.concatenate([s[:h] + se[:h], s[h:n - h], s[n - h:] + se[h:]], axis=0)
            cnt = (jnp.minimum(pos + right + 1, seq_len) - jnp.maximum(pos - left, 0)).astype(F32)
            d = (s / cnt - u).astype(BF16)
            yg = _dot(d, wg_ref[g]) * ps_ref[:, cols] * _silu(z_ref[:, cols].astype(F32))
            y_ref[:, cols] = yg.astype(BF16)
        return step

    return [group(g, w) for g, w in enumerate(POOL_WINDOWS)]


def _pool_ret(u, z, wg_all, ps, wo_all, j, x, mod_pool, g, mod, w_ret):
    t = PROJ
    per = t.tm // POOL_HALO
    n_halo = N_TOK // POOL_HALO
    n_in = 2 * RET_QK_WIDTH + 2 * RET_V_WIDTH
    kt_per = t.tm // RET_CHUNK
    return pl.pallas_call(
        _pool_ret_kernel,
        out_shape=(jax.ShapeDtypeStruct((N_TOK, D_MODEL), F32),
                   jax.ShapeDtypeStruct((N_TOK, RET_QK_WIDTH), BF16),
                   jax.ShapeDtypeStruct((N_TOK // RET_CHUNK, RET_QK_WIDTH, RET_CHUNK), BF16),
                   jax.ShapeDtypeStruct((N_TOK, RET_V_WIDTH), BF16),
                   jax.ShapeDtypeStruct((N_TOK, RET_V_WIDTH), BF16)),
        grid=(t.n_tiles,),
        in_specs=[
            t.rows(D_MODEL),
            pl.BlockSpec((POOL_HALO, D_MODEL), lambda i: (jnp.maximum(i * per - 1, 0), 0)),
            pl.BlockSpec((POOL_HALO, D_MODEL), lambda i: (jnp.minimum((i + 1) * per, n_halo - 1), 0)),
            t.rows(D_MODEL),
            _layer_spec(wg_all, j),
            _const_spec((1, D_MODEL)),
            _layer_spec(wo_all, j),
            t.rows(D_MODEL),
            t.mod(2),
            _const_spec((1, D_MODEL)), t.mod(0), t.mod(1), _const_spec((D_MODEL, n_in)),
        ],
        out_specs=(t.rows(D_MODEL), t.rows(RET_QK_WIDTH),
                   pl.BlockSpec((kt_per, RET_QK_WIDTH, RET_CHUNK), lambda i: (i, 0, 0)),
                   t.rows(RET_V_WIDTH), t.rows(RET_V_WIDTH)),
        scratch_shapes=[pltpu.VMEM((t.tm, D_MODEL), BF16),
                        pltpu.VMEM((len(POOL_WINDOWS), POOL_GROUP_DIM, POOL_GROUP_DIM), BF16),
                        pltpu.VMEM((D_MODEL, D_MODEL), BF16),
                        pltpu.VMEM((len(POOL_WINDOWS), POOL_SUB, POOL_SUB), BF16),
                        pltpu.VMEM((RET_QK_WIDTH, D_MODEL), BF16)],
        compiler_params=_params("arbitrary", vmem=FUSED_VMEM_LIMIT),
        name="pool_ret",
    )(u, u, u, z, wg_all, ps, wo_all, x, mod_pool, g, mod, mod, w_ret)


def _ret_in_steps(x, g_ref, sh_ref, sc_ref, w_ref, wkt_ref, q_ref, kt_ref, v_ref, z_ref):
    h = _norm_mod(x, g_ref, sh_ref, sc_ref)

    def chunk(ref, lo, c):
        def step():
            cols = slice(c * MXU_N, (c + 1) * MXU_N)
            ref[:, cols] = _dot(h, w_ref[:, lo + c * MXU_N:lo + (c + 1) * MXU_N]).astype(BF16)
        return step

    def key_chunk(c):
        def step():
            rows = slice(c * MXU_N, (c + 1) * MXU_N)
            kt = (_dot_nt(wkt_ref[rows, :], h) * RET_KEY_DIM ** -0.5).astype(BF16)
            for cc in range(x.shape[0] // RET_CHUNK):
                kt_ref[cc, rows, :] = kt[:, cc * RET_CHUNK:(cc + 1) * RET_CHUNK]
        return step

    steps = [chunk(q_ref, 0, c) for c in range(RET_QK_WIDTH // MXU_N)]
    steps += [key_chunk(c) for c in range(RET_QK_WIDTH // MXU_N)]
    steps += [chunk(v_ref, 2 * RET_QK_WIDTH, c) for c in range(RET_V_WIDTH // MXU_N)]
    steps += [chunk(z_ref, 2 * RET_QK_WIDTH + RET_V_WIDTH, c) for c in range(RET_V_WIDTH // MXU_N)]
    return steps


def _run_interleaved(main, side):
    done = 0
    for k, step in enumerate(main):
        step()
        due = (k + 1) * len(side) // len(main)
        for s in side[done:due]:
            s()
        done = due


def _pos(shape, axis):
    return lax.broadcasted_iota(jnp.int32, shape, axis).astype(F32)


def _ret_tables_kernel(lgf_ref, lgb_ref, decay_ref, row_ref, col_ref, cdec_ref):
    h = pl.program_id(0)
    lg_f = lgf_ref[h]
    lg_b = lgb_ref[h]
    c = RET_CHUNK
    diff = _pos((c, c), 0) - _pos((c, c), 1)
    fwd = jnp.where(diff >= 0, jnp.exp(jnp.maximum(diff, 0.0) * lg_f), 0.0)
    bwd = jnp.where(diff <= 0, jnp.exp(jnp.maximum(-diff, 0.0) * lg_b), 0.0)
    decay_ref[...] = fwd + bwd
    j = _pos((RET_TAB_ROWS, c), 1)
    row_ref[0] = jnp.exp((c - 1.0 - j) * lg_f)
    row_ref[1] = jnp.exp(j * lg_b)
    i = _pos((c, LANES), 0)
    col_ref[0] = jnp.exp((i + 1.0) * lg_f)
    col_ref[1] = jnp.exp((c - i) * lg_b)
    full = jnp.full((RET_TAB_ROWS, RET_VAL_DIM), float(c), F32)
    cdec_ref[0] = jnp.exp(full * lg_f)
    cdec_ref[1] = jnp.exp(full * lg_b)


def _ret_tables(lg_f, lg_b):
    smem = pl.BlockSpec(memory_space=pltpu.SMEM)
    c = RET_CHUNK
    shapes = ((c, c), (2, RET_TAB_ROWS, c), (2, c, LANES), (2, RET_TAB_ROWS, RET_VAL_DIM))
    return pl.pallas_call(
        _ret_tables_kernel,
        out_shape=tuple(jax.ShapeDtypeStruct((RET_HEADS,) + s, F32) for s in shapes),
        grid=(RET_HEADS,),
        in_specs=[smem, smem],
        out_specs=tuple(pl.BlockSpec((None,) + s, lambda h, n=len(s): (h,) + (0,) * n) for s in shapes),
        compiler_params=_params("arbitrary"),
        name="ret_tables",
    )(lg_f, lg_b)


def _group_norm_gate(o, gn, z):
    mu = jnp.mean(o, axis=-1, keepdims=True)
    var = jnp.mean(jnp.square(o - mu), axis=-1, keepdims=True)
    on = (o - mu) * lax.rsqrt(var + EPS)
    return (on * gn * _silu(z.astype(F32))).astype(BF16)


def _ret_ctx_kernel(q_ref, kt_ref, v_ref, z_ref, gn_ref, decay_ref, row_ref, y_ref, sf_ref, sb_ref):
    for s in range(RET_CTX_SEQS):
        rows = slice(s * SEQ, (s + 1) * SEQ)
        for h in range(RET_HEADS):
            kc = slice(h * RET_KEY_DIM, (h + 1) * RET_KEY_DIM)
            vc = slice(h * RET_VAL_DIM, (h + 1) * RET_VAL_DIM)
            kt = kt_ref[s, kc, :]
            v = v_ref[rows, vc]
            att = (_dot(q_ref[rows, kc], kt) * decay_ref[h]).astype(BF16)
            y_ref[rows, vc] = _group_norm_gate(_dot(att, v), gn_ref[:, vc], z_ref[rows, vc])
            ktf = kt.astype(F32)
            sf_ref[s, h] = _dot((ktf * row_ref[h, 0, 0:1, :]).astype(BF16), v)
            sb_ref[s, h] = _dot((ktf * row_ref[h, 1, 0:1, :]).astype(BF16), v)


def _ret_ctx(q, kt, v, z, gn, decay, row):
    c = RET_CHUNK
    n = RET_CTX_SEQS
    wide = lambda w: pl.BlockSpec((n * SEQ, w), lambda b: (b, 0))
    st_spec = pl.BlockSpec((n, RET_HEADS, RET_KEY_DIM, RET_VAL_DIM), lambda b: (b, 0, 0, 0))
    st_shape = jax.ShapeDtypeStruct((BATCH, RET_HEADS, RET_KEY_DIM, RET_VAL_DIM), F32)
    return pl.pallas_call(
        _ret_ctx_kernel,
        out_shape=(jax.ShapeDtypeStruct((N_PROMPT_TOK, RET_V_WIDTH), BF16), st_shape, st_shape),
        grid=(BATCH // n,),
        in_specs=[wide(RET_QK_WIDTH), pl.BlockSpec((n, RET_QK_WIDTH, c), lambda b: (b, 0, 0)),
                  wide(RET_V_WIDTH), wide(RET_V_WIDTH), _const_spec((1, RET_V_WIDTH)),
                  _const_spec((RET_HEADS, c, c)), _const_spec((RET_HEADS, 2, RET_TAB_ROWS, c))],
        out_specs=(wide(RET_V_WIDTH), st_spec, st_spec),
        compiler_params=_params("arbitrary", vmem=VMEM_LIMIT),
        name="ret_ctx",
    )(q, kt, v, z, gn, decay, row)


def _ret_lat_kernel(q_ref, kt_ref, v_ref, z_ref, gn_ref, decay_ref, row_ref, col_ref, cdec_ref,
                    s0f_ref, s0b_ref, y_ref, sf_all, sb_all, sf_acc, sb_acc):
    c = RET_CHUNK
    n_chunks = DEC_SEQ // c
    rows_of = lambda ci: pl.ds(pl.multiple_of(ci * c, c), c)

    sf_acc[...] = s0f_ref[...]
    sb_acc[...] = s0b_ref[...]

    def scan_step(i, carry):
        cf = i
        cb = n_chunks - 1 - i
        sf_all[cf] = sf_acc[...].astype(BF16)
        sb_all[cb] = sb_acc[...].astype(BF16)
        uf = _dot((kt_ref[cf].astype(F32) * row_ref[0, 0:1, :]).astype(BF16), v_ref[rows_of(cf), :])
        ub = _dot((kt_ref[cb].astype(F32) * row_ref[1, 0:1, :]).astype(BF16), v_ref[rows_of(cb), :])
        sf_acc[...] = sf_acc[...] * cdec_ref[0, 0:1, :] + uf
        sb_acc[...] = sb_acc[...] * cdec_ref[1, 0:1, :] + ub
        return carry

    lax.fori_loop(0, n_chunks, scan_step, 0, unroll=4)

    def out_step(ci, carry):
        rows = rows_of(ci)
        q = q_ref[rows, :]
        qf = q.astype(F32)
        qdec_f = jnp.concatenate([col_ref[0]] * (RET_KEY_DIM // LANES), axis=1)
        qdec_b = jnp.concatenate([col_ref[1]] * (RET_KEY_DIM // LANES), axis=1)
        att = (_dot(q, kt_ref[ci]) * decay_ref[...]).astype(BF16)
        o = (_dot(att, v_ref[rows, :])
             + _dot((qf * qdec_f).astype(BF16), sf_all[ci])
             + _dot((qf * qdec_b).astype(BF16), sb_all[ci]))
        y_ref[rows, :] = _group_norm_gate(o, gn_ref[...], z_ref[rows, :])
        return carry

    lax.fori_loop(0, n_chunks, out_step, 0, unroll=8)


def _ret_lat(q, kt, v, z, gn, decay, row, col, cdec, s0f, s0b):
    c = RET_CHUNK
    n_chunks = DEC_SEQ // c
    row0 = N_PROMPT_TOK // DEC_SEQ
    qk_spec = pl.BlockSpec((DEC_SEQ, RET_KEY_DIM), lambda b, h: (row0 + b, h))
    v_spec = pl.BlockSpec((DEC_SEQ, RET_VAL_DIM), lambda b, h: (row0 + b, h))
    st_spec = pl.BlockSpec((None, None, RET_KEY_DIM, RET_VAL_DIM), lambda b, h: (b, h, 0, 0))
    tab = lambda *s: pl.BlockSpec((None,) + s, lambda b, h: (h,) + (0,) * len(s))
    states = pltpu.VMEM((n_chunks, RET_KEY_DIM, RET_VAL_DIM), BF16)
    acc = pltpu.VMEM((RET_KEY_DIM, RET_VAL_DIM), F32)
    return pl.pallas_call(
        _ret_lat_kernel,
        out_shape=jax.ShapeDtypeStruct((N_SAMPLE_TOK, RET_V_WIDTH), BF16),
        grid=(DEC_BATCH, RET_HEADS),
        in_specs=[qk_spec,
                  pl.BlockSpec((n_chunks, RET_KEY_DIM, c), lambda b, h: (row0 + b, h, 0)),
                  v_spec, v_spec, pl.BlockSpec((1, RET_VAL_DIM), lambda b, h: (0, h)),
                  tab(c, c), tab(2, RET_TAB_ROWS, c), tab(2, c, LANES), tab(2, RET_TAB_ROWS, RET_VAL_DIM),
                  st_spec, st_spec],
        out_specs=pl.BlockSpec((DEC_SEQ, RET_VAL_DIM), lambda b, h: (b, h)),
        scratch_shapes=[states, states, acc, acc],
        compiler_params=_params("arbitrary", "arbitrary", vmem=VMEM_LIMIT),
        name="ret_lat",
    )(q, kt, v, z, gn, decay, row, col, cdec, s0f, s0b)


def _rope_tables(tm):
    n_rows = DEC_SEQ // GRID_W
    rows = jnp.repeat(jnp.arange(n_rows), GRID_W).astype(F32)
    cols = jnp.tile(jnp.arange(GRID_W), n_rows).astype(F32)
    half = ATTN_HEAD_DIM // 4
    inv = ROPE_BASE ** (-jnp.arange(half, dtype=F32) / half)
    ang_r = rows[:, None] * inv[None, :]
    ang_c = cols[:, None] * inv[None, :]
    cos = jnp.concatenate([jnp.cos(ang_r), jnp.cos(ang_r), jnp.cos(ang_c), jnp.cos(ang_c)], axis=-1)
    sin = jnp.concatenate([-jnp.sin(ang_r), jnp.sin(ang_r), -jnp.sin(ang_c), jnp.sin(ang_c)], axis=-1)
    cos = jnp.concatenate([jnp.ones((tm, ATTN_HEAD_DIM), F32), cos], axis=0)
    sin = jnp.concatenate([jnp.zeros((tm, ATTN_HEAD_DIM), F32), sin], axis=0)
    return jnp.tile(cos, (1, 2)), jnp.tile(sin, (1, 2))


def kernel(x_prompt, x_sample, cache_k, cache_v, state_fwd, state_bwd, c, c_ctx, norm_g, ada_w, ada_b, attn_w_in, attn_w_out, attn_sink, pool_w_in, pool_w_grp, pool_scale, pool_w_out, ret_w_in, ret_decay_fwd, ret_decay_bwd, ret_gn_g, ret_w_out, final_g):
    x_parts = (x_prompt.reshape(N_PROMPT_TOK, D_MODEL), x_sample.reshape(N_SAMPLE_TOK, D_MODEL))
    cond = jnp.concatenate([c_ctx[None, :], c,
                            jnp.zeros((N_COND - 1 - DEC_BATCH, D_MODEL), F32)], axis=0)
    mods = _ada_table(cond.T, ada_w, ada_b).reshape(DEPTH, N_COND, 1, 3 * D_MODEL)
    cos_t, sin_t = _rope_tables(PROJ.tm)

    to_kernel = lambda a: jnp.transpose(a, (0, 1, 3, 4, 2))
    from_kernel = lambda a: jnp.transpose(
        a.reshape(a.shape[0], a.shape[1], ATTN_KV_HEADS, ATTN_HEAD_DIM, a.shape[3]), (0, 1, 4, 2, 3))
    ctx_k, ctx_v = to_kernel(cache_k), to_kernel(cache_v)

    assert DEPTH % N_MIXERS == 1, "the layer stack must end on an attention layer"
    caches = ()
    new_sf = new_sb = None
    mixed = None
    for i in range(DEPTH):
        kind, j = i % N_MIXERS, i // N_MIXERS
        g = norm_g[i].reshape(1, D_MODEL)
        mod = mods[i]
        if kind == 0:
            outs = _attn_in(mixed, x_parts, g, mod, attn_w_in, j, cos_t, sin_t, tuple(caches))
            if mixed is not None:
                x_parts, outs = (outs[0],), outs[1:]
            q, k4, v4, z, *caches = outs
            y_parts = (_ctx_attn(attn_sink[j], q, k4, v4, z),
                       _lat_attn(attn_sink[j], q, k4, v4, ctx_k, ctx_v, j, z))
            mixed = Mixed(y_parts, attn_w_out, j, mod)
        elif kind == 1:
            x, u, z = _pool_in(mixed, x_parts, g, mod, pool_w_in, j)
            x_parts, mixed, pooled = (x,), None, (u, z, j, mod)
        else:
            u, z, jp, mod_pool = pooled
            lg_f = jax.nn.log_sigmoid(ret_decay_fwd[j].astype(F32))
            lg_b = jax.nn.log_sigmoid(ret_decay_bwd[j].astype(F32))
            gn = ret_gn_g[j].reshape(1, RET_V_WIDTH)
            x, q, kt, v, z = _pool_ret(u, z, pool_w_grp, pool_scale[jp].reshape(1, D_MODEL), pool_w_out, jp,
                                       x_parts[0], mod_pool, g, mod, ret_w_in[j].astype(BF16))
            x_parts = (x,)
            decay, row, col, cdec = _ret_tables(lg_f, lg_b)
            y_ctx, new_sf, new_sb = _ret_ctx(q, kt, v, z, gn, decay, row)
            y_parts = (y_ctx, _ret_lat(q, kt, v, z, gn, decay, row, col, cdec,
                                       state_fwd[:, j], state_bwd[:, j]))
            mixed = Mixed(y_parts, ret_w_out, j, mod)
    y_prompt, y_sample = _out_proj_final(mixed, x_parts[0], final_g.reshape(1, D_MODEL))
    new_k, new_v = caches
    return (y_prompt.reshape(BATCH, SEQ, D_MODEL), y_sample.reshape(DEC_BATCH, DEC_SEQ, D_MODEL),
            from_kernel(new_k), from_kernel(new_v), new_sf[:, None], new_sb[:, None])
```

```python
import functools
from typing import NamedTuple

import jax
import jax.numpy as jnp
from jax import lax
from jax.experimental import pallas as pl
from jax.experimental.pallas import tpu as pltpu

F32 = jnp.float32
BF16 = jnp.bfloat16

D_MODEL = 1024
BATCH = 16
SEQ = 256
DEPTH = 4
DEC_BATCH = 2
DEC_SEQ = 2048
PAST_LEN = 512
GRID_W = 64
N_MIXERS = 3
ATTN_HEADS = 16
ATTN_KV_HEADS = 4
ATTN_HEAD_DIM = 64
ATTN_GROUP = 4
ATTN_WIDTH = 1024
ATTN_KV_WIDTH = 256
WINDOW = 128
ROPE_BASE = 10000.0
POOL_WINDOWS = (2, 4, 8, 16)
POOL_GROUP_DIM = 256
RET_HEADS = 4
RET_KEY_DIM = 256
RET_VAL_DIM = 512
RET_QK_WIDTH = 1024
RET_V_WIDTH = 2048
EPS = 1e-6
NEG_INF = -1e30
LOG2_E = 1.4426950408889634

N_PROMPT_TOK = BATCH * SEQ
N_SAMPLE_TOK = DEC_BATCH * DEC_SEQ
N_TOK = N_PROMPT_TOK + N_SAMPLE_TOK
N_COND = 8
LANES = 128
MXU_N = 256
Q_BLOCK = 128
Q_SUB = 4
RET_CHUNK = 256
RET_TAB_ROWS = 8
RET_CTX_SEQS = 2
POOL_HALO = 8
POOL_SUB = SEQ
VMEM_LIMIT = 48 * 1024 * 1024
FUSED_VMEM_LIMIT = 58 * 1024 * 1024


class Tiling(NamedTuple):
    tm: int

    @property
    def n_tiles(self):
        return N_TOK // self.tm

    @property
    def n_prompt(self):
        return N_PROMPT_TOK // self.tm

    @property
    def per_dec_seq(self):
        return DEC_SEQ // self.tm

    def cond(self, i):
        return jnp.where(i < self.n_prompt, 0, 1 + (i - self.n_prompt) // self.per_dec_seq)

    def seq_tile(self, i):
        return jnp.where(i < self.n_prompt, 0, (i - self.n_prompt) % self.per_dec_seq)

    def rows(self, width):
        return pl.BlockSpec((self.tm, width), lambda i: (i, 0))

    def prompt_rows(self, width):
        return pl.BlockSpec((self.tm, width), lambda i: (jnp.minimum(i, self.n_prompt - 1), 0))

    def sample_rows(self, width):
        return pl.BlockSpec((self.tm, width), lambda i: (jnp.maximum(i - self.n_prompt, 0), 0))

    def mod(self, part):
        return pl.BlockSpec((None, 1, D_MODEL), lambda i: (self.cond(i), 0, part))


PROJ = Tiling(512)
FIRST = Tiling(1024)
FINAL = Tiling(1024)


def _silu(z):
    return z * (1.0 / (1.0 + jnp.exp(-z)))


def _dot(a, b):
    return jnp.dot(a, b, preferred_element_type=F32)


def _dot_nt(a, b):
    return lax.dot_general(a, b, (((1,), (1,)), ((), ())), preferred_element_type=F32)


def _params(*sem, vmem=None):
    return pltpu.CompilerParams(dimension_semantics=sem, vmem_limit_bytes=vmem)


def _const_spec(shape):
    nd = len(shape)
    return pl.BlockSpec(shape, lambda *_: (0,) * nd, pipeline_mode=pl.Buffered(1))


def _part_specs(t, parts, width):
    if len(parts) == 1:
        return [t.rows(width)]
    return [t.prompt_rows(width), t.sample_rows(width)]


def _read_parts(t, refs):
    if len(refs) == 1:
        return refs[0][...]
    return jnp.where(pl.program_id(0) < t.n_prompt, refs[0][...], refs[1][...])


def _ada_kernel(cond_ref, w_ref, b_ref, o_ref):
    s = _silu(cond_ref[...])
    w = w_ref[...]
    rows = [jnp.sum(s[:, c:c + 1] * w, axis=0, keepdims=True) + b_ref[...] for c in range(1 + DEC_BATCH)]
    rows.append(jnp.zeros((N_COND - len(rows), w.shape[1]), F32))
    o_ref[...] = jnp.concatenate(rows, axis=0)


def _ada_table(cond, ada_w, ada_b):
    tn = 3 * D_MODEL // 2
    return pl.pallas_call(
        _ada_kernel,
        out_shape=jax.ShapeDtypeStruct((DEPTH, N_COND, 3 * D_MODEL), F32),
        grid=(DEPTH, 3 * D_MODEL // tn),
        in_specs=[
            pl.BlockSpec((D_MODEL, N_COND), lambda l, n: (0, 0)),
            pl.BlockSpec((None, D_MODEL, tn), lambda l, n: (l, 0, n)),
            pl.BlockSpec((None, 1, tn), lambda l, n: (l, 0, n)),
        ],
        out_specs=pl.BlockSpec((None, N_COND, tn), lambda l, n: (l, 0, n)),
        compiler_params=_params("arbitrary", "arbitrary", vmem=VMEM_LIMIT),
        name="ada_table",
    )(cond, ada_w, ada_b.reshape(DEPTH, 1, 3 * D_MODEL))


def _rms(x, g):
    return x * lax.rsqrt(jnp.mean(x * x, axis=-1, keepdims=True) + EPS) * g


def _norm_mod(x, g_ref, sh_ref, sc_ref):
    return (_rms(x, g_ref[...]) * (1.0 + sc_ref[...]) + sh_ref[...]).astype(BF16)


def _proj_chunks(h, w_ref, lo, width, store):
    for c in range(width // MXU_N):
        store(c, _dot(h, w_ref[:, lo + c * MXU_N:lo + (c + 1) * MXU_N]))


def _layer_spec(w, j):
    nd = w.ndim - 1
    return pl.BlockSpec((None,) + w.shape[1:], lambda *_: (j,) + (0,) * nd, pipeline_mode=pl.Buffered(1))


def _cast_weight_once(w_ref, wb_ref):
    @pl.when(pl.program_id(0) == 0)
    def _():
        rows = wb_ref.shape[-2]
        for r in range(0, rows, MXU_N):
            wb_ref[..., r:r + MXU_N, :] = w_ref[..., r:r + MXU_N, :].astype(BF16)


class Mixed(NamedTuple):
    y_parts: tuple
    w_out: jax.Array
    j: int
    mod: jax.Array


def _residual_specs(t, mixed, x_parts):
    k = mixed.w_out.shape[1]
    return (_part_specs(t, mixed.y_parts, k) + [_layer_spec(mixed.w_out, mixed.j)]
            + _part_specs(t, x_parts, D_MODEL) + [t.mod(2)])


def _residual_args(mixed, x_parts):
    return (*mixed.y_parts, mixed.w_out, *x_parts, mixed.mod)


def _residual_update(t, refs, n_y, n_x, wo_ref, xo_ref):
    y_refs, wof_ref, x_refs, gate_ref = refs[:n_y], refs[n_y], refs[n_y + 1:n_y + 1 + n_x], refs[n_y + 1 + n_x]
    _cast_weight_once(wof_ref, wo_ref)
    x = _read_parts(t, x_refs) + gate_ref[...] * _dot(_read_parts(t, y_refs), wo_ref[...])
    xo_ref[...] = x
    return x


def _rep4(a, h):
    half = a[:, (h // 2) * LANES:(h // 2 + 1) * LANES]
    lane = lax.broadcasted_iota(jnp.int32, half.shape, 1)
    keep = (lane < ATTN_HEAD_DIM) if h % 2 == 0 else (lane >= ATTN_HEAD_DIM)
    m = jnp.where(keep, half, 0.0)
    s = m + pltpu.roll(m, ATTN_HEAD_DIM, 1)
    return jnp.concatenate([s, s], axis=1)


def _attn_in_kernel(*refs, t, n_y, n_x, n_alias):
    n_res = n_y + n_x + 2 if n_y else n_x
    res_refs, refs = refs[:n_res], refs[n_res:]
    (g_ref, sh_ref, sc_ref, wf_ref, cos_ref, sin_ref), refs = refs[:6], refs[6 + n_alias:]
    if n_y:
        xo_ref, q_ref, k4_ref, v4_ref, z_ref, kc_ref, vc_ref, wo_ref, w_ref = refs
        x = _residual_update(t, res_refs, n_y, n_x, wo_ref, xo_ref)
    else:
        q_ref, k4_ref, v4_ref, z_ref, kc_ref, vc_ref, w_ref = refs
        x = _read_parts(t, res_refs)
    i = pl.program_id(0)
    _cast_weight_once(wf_ref, w_ref)
    h = _norm_mod(x, g_ref, sh_ref, sc_ref)
    cos = cos_ref[...]
    sin = sin_ref[...]
    lane = lax.broadcasted_iota(jnp.int32, (t.tm, LANES), 1)
    first = (lane % (ATTN_HEAD_DIM // 2)) < ATTN_HEAD_DIM // 4

    def rope(a):
        rot = jnp.where(first, pltpu.roll(a, LANES - ATTN_HEAD_DIM // 4, 1),
                        pltpu.roll(a, ATTN_HEAD_DIM // 4, 1))
        return a * cos + rot * sin

    def rope_wide(a):
        return jnp.concatenate(
            [rope(a[:, s * LANES:(s + 1) * LANES]) for s in range(MXU_N // LANES)], axis=1)

    scale = ATTN_HEAD_DIM ** -0.5 * LOG2_E

    def store_q(c, a):
        q_ref[:, c * MXU_N:(c + 1) * MXU_N] = (rope_wide(a) * scale).astype(BF16)

    def store_z(c, a):
        z_ref[:, c * MXU_N:(c + 1) * MXU_N] = a.astype(BF16)

    _proj_chunks(h, w_ref, 0, ATTN_WIDTH, store_q)
    k = rope_wide(_dot(h, w_ref[:, ATTN_WIDTH:ATTN_WIDTH + ATTN_KV_WIDTH]))
    v = _dot(h, w_ref[:, ATTN_WIDTH + ATTN_KV_WIDTH:ATTN_WIDTH + 2 * ATTN_KV_WIDTH])
    for hh in range(ATTN_KV_HEADS):
        k4_ref[:, hh * MXU_N:(hh + 1) * MXU_N] = _rep4(k, hh).astype(BF16)
        v4_ref[:, hh * MXU_N:(hh + 1) * MXU_N] = _rep4(v, hh).astype(BF16)
    _proj_chunks(h, w_ref, ATTN_WIDTH + 2 * ATTN_KV_WIDTH, ATTN_WIDTH, store_z)

    @pl.when(i < t.n_prompt)
    def _():
        for s in range(t.tm // SEQ):
            kc_ref[s] = k[s * SEQ:(s + 1) * SEQ, :].T
            vc_ref[s] = v[s * SEQ:(s + 1) * SEQ, :].T


def _attn_in(mixed, x_parts, g, mod, w_all, j, caches):
    t = FIRST if mixed is None else PROJ
    cos_t, sin_t = _rope_tables(t.tm)
    n_in = 2 * ATTN_WIDTH + 2 * ATTN_KV_WIDTH
    per = t.tm // SEQ
    rope_spec = pl.BlockSpec(
        (t.tm, LANES), lambda i: (jnp.where(i < t.n_prompt, 0, 1 + t.seq_tile(i)), 0))
    wide = jax.ShapeDtypeStruct((N_TOK, ATTN_WIDTH), BF16)
    n_attn = w_all.shape[0]
    cache = jax.ShapeDtypeStruct((BATCH, n_attn, ATTN_KV_WIDTH, SEQ), F32)
    cache_spec = pl.BlockSpec((per, None, ATTN_KV_WIDTH, SEQ),
                              lambda i: (jnp.minimum(i, t.n_prompt - 1), j, 0, 0))
    n_x = len(x_parts)
    if mixed is None:
        n_y, res_specs, res_args, res_out, res_ospecs, res_scratch = 0, _part_specs(t, x_parts, D_MODEL), x_parts, (), (), []
    else:
        n_y = len(mixed.y_parts)
        res_specs, res_args = _residual_specs(t, mixed, x_parts), _residual_args(mixed, x_parts)
        res_out = (jax.ShapeDtypeStruct((N_TOK, D_MODEL), F32),)
        res_ospecs = (t.rows(D_MODEL),)
        res_scratch = [pltpu.VMEM(mixed.w_out.shape[1:], BF16)]
    n_front = len(res_specs) + 6
    return pl.pallas_call(
        functools.partial(_attn_in_kernel, t=t, n_y=n_y, n_x=n_x, n_alias=len(caches)),
        out_shape=res_out + (wide, wide, wide, wide, cache, cache),
        grid=(t.n_tiles,),
        in_specs=res_specs + [
            _const_spec((1, D_MODEL)), t.mod(0), t.mod(1), _layer_spec(w_all, j),
            rope_spec, rope_spec] + [pl.BlockSpec(memory_space=pl.ANY)] * len(caches),
        out_specs=res_ospecs + (t.rows(ATTN_WIDTH),) * 4 + (cache_spec,) * 2,
        scratch_shapes=res_scratch + [pltpu.VMEM((D_MODEL, n_in), BF16)],
        input_output_aliases={n_front + c: len(res_out) + 4 + c for c in range(len(caches))},
        compiler_params=_params("arbitrary", vmem=FUSED_VMEM_LIMIT),
        name="attn_in",
    )(*res_args, g, mod, mod, w_all, cos_t, sin_t, *caches)


def _stack_group_queries(q):
    qf = q.astype(F32)
    chunk = lax.broadcasted_iota(jnp.int32, qf.shape, 1) // ATTN_HEAD_DIM
    return jnp.concatenate(
        [jnp.where(chunk == g, qf, 0.0) for g in range(ATTN_GROUP)], axis=0).astype(BF16)


def _gather_group_outputs(o, rows):
    chunk = lax.broadcasted_iota(jnp.int32, (rows, MXU_N), 1) // ATTN_HEAD_DIM
    acc = jnp.zeros((rows, MXU_N), F32)
    for g in range(ATTN_GROUP):
        acc = acc + jnp.where(chunk == g, o[g * rows:(g + 1) * rows], 0.0)
    return acc


def _sink_column(sink_ref, h, rows):
    grp = lax.broadcasted_iota(jnp.int32, (ATTN_GROUP * rows, 1), 0) // rows
    col = jnp.zeros((ATTN_GROUP * rows, 1), F32)
    for g in range(ATTN_GROUP):
        col = jnp.where(grp == g, sink_ref[h * ATTN_GROUP + g] * LOG2_E, col)
    return col


def _chunk_rows(dtype):
    chunk = lax.broadcasted_iota(jnp.int32, (1, MXU_N), 1) // ATTN_HEAD_DIM
    return [(chunk == g).astype(F32).astype(dtype) for g in range(ATTN_GROUP)]


def _block_diag_rows(x4):
    return jnp.concatenate([x4 * m for m in _chunk_rows(x4.dtype)], axis=0)


def _ctx_attn_kernel(sink_ref, q_ref, k4_ref, v4_ref, z_ref, y_ref):
    chunk = lax.broadcasted_iota(jnp.int32, (SEQ, MXU_N), 1) // ATTN_HEAD_DIM
    for h in range(ATTN_KV_HEADS):
        cols = slice(h * MXU_N, (h + 1) * MXU_N)
        s = _dot_nt(q_ref[:, cols], _block_diag_rows(k4_ref[:, cols]))
        inv = jnp.zeros((SEQ, MXU_N), F32)
        probs = []
        for g in range(ATTN_GROUP):
            sg = s[:, g * SEQ:(g + 1) * SEQ]
            sk = sink_ref[h * ATTN_GROUP + g] * LOG2_E
            m = jnp.maximum(jnp.max(sg, axis=1, keepdims=True), sk)
            e = jnp.exp2(sg - m)
            den = jnp.sum(e, axis=1, keepdims=True) + jnp.exp2(sk - m)
            probs.append(e.astype(BF16))
            inv = jnp.where(chunk == g, 1.0 / den, inv)
        o = _dot(jnp.concatenate(probs, axis=1), _block_diag_rows(v4_ref[:, cols]))
        y_ref[:, cols] = (o * inv * _silu(z_ref[:, cols].astype(F32))).astype(BF16)


def _ctx_attn(sink, q, k4, v4, z):
    spec = pl.BlockSpec((SEQ, ATTN_WIDTH), lambda b: (b, 0))
    return pl.pallas_call(
        _ctx_attn_kernel,
        out_shape=jax.ShapeDtypeStruct((N_PROMPT_TOK, ATTN_WIDTH), BF16),
        grid=(BATCH,),
        in_specs=[pl.BlockSpec(memory_space=pltpu.SMEM), spec, spec, spec, spec],
        out_specs=spec,
        compiler_params=_params("arbitrary"),
        name="ctx_attn",
    )(sink, q, k4, v4, z)


LAT_STEP = Q_SUB * Q_BLOCK
LAT_PER_SEQ = DEC_SEQ // LAT_STEP
LAT_BLOCKS = DEC_BATCH * ATTN_KV_HEADS * LAT_PER_SEQ


def _lat_block(blk):
    return (blk // (ATTN_KV_HEADS * LAT_PER_SEQ), (blk // LAT_PER_SEQ) % ATTN_KV_HEADS, blk % LAT_PER_SEQ)


def _band_rows(qb):
    return pl.ds(pl.multiple_of(qb * Q_BLOCK, Q_BLOCK), 3 * Q_BLOCK)


def _pad_sequence(dst, src):
    zeros = jnp.zeros((Q_BLOCK, MXU_N), BF16)
    dst[0:Q_BLOCK, :] = zeros
    dst[Q_BLOCK:Q_BLOCK + DEC_SEQ, :] = src[...]
    dst[Q_BLOCK + DEC_SEQ:, :] = zeros


def _lat_attn_kernel(sink_ref, q_ref, k4_ref, v4_ref, kc_ref, vc_ref, z_ref, y_ref,
                     kp, vp, k4c, v4c, sc_a, sl_a, sc_b, sl_b, e_scr):
    t = pl.program_id(0)
    _, _, n1 = _lat_block(jnp.minimum(t, LAT_BLOCKS - 1))
    _, h0, n0 = _lat_block(jnp.maximum(t - 1, 0))
    n_blocks = DEC_SEQ // Q_BLOCK

    @pl.when(t == 0)
    def _():
        sc_b[...] = jnp.zeros(sc_b.shape, F32)
        sl_b[...] = jnp.zeros(sl_b.shape, F32)

    @pl.when(n1 == 0)
    def _():
        _pad_sequence(kp, k4_ref)
        k4c[...] = jnp.concatenate([kc_ref[...]] * ATTN_GROUP, axis=0).astype(BF16)

    @pl.when(n0 == 0)
    def _():
        _pad_sequence(vp, v4_ref)
        v4c[...] = jnp.concatenate([vc_ref[...]] * ATTN_GROUP, axis=0).T.astype(BF16)

    @pl.when(t % 2 == 0)
    def _():
        _lat_stages(sink_ref, q_ref, z_ref, y_ref, kp, vp, k4c, v4c, e_scr, n1, h0, n0,
                    sc_a, sl_a, sc_b, sl_b)

    @pl.when(t % 2 == 1)
    def _():
        _lat_stages(sink_ref, q_ref, z_ref, y_ref, kp, vp, k4c, v4c, e_scr, n1, h0, n0,
                    sc_b, sl_b, sc_a, sl_a)


def _lat_stages(sink_ref, q_ref, z_ref, y_ref, kp, vp, k4c, v4c, e_scr, n1, h0, n0,
                sc_w, sl_w, sc_r, sl_r):
    n_blocks = DEC_SEQ // Q_BLOCK
    for sub in range(Q_SUB):
        qs = _stack_group_queries(q_ref[sub * Q_BLOCK:(sub + 1) * Q_BLOCK, :])
        sc_w[sub] = _dot(qs, k4c[...])
        sl_w[sub] = _dot_nt(qs, kp[_band_rows(n1 * Q_SUB + sub), :])

    rows4 = ATTN_GROUP * Q_BLOCK
    r = lax.broadcasted_iota(jnp.int32, (rows4, Q_BLOCK), 0) % Q_BLOCK
    c = lax.broadcasted_iota(jnp.int32, (rows4, Q_BLOCK), 1)
    in_left = c >= r
    in_right = c <= r
    sk = _sink_column(sink_ref, h0, Q_BLOCK)
    for sub in range(Q_SUB):
        qb = n0 * Q_SUB + sub
        band = _band_rows(qb)
        qrows = slice(sub * Q_BLOCK, (sub + 1) * Q_BLOCK)
        s_ctx = sc_r[sub]
        s_lat = sl_r[sub]
        blocks = [
            s_ctx,
            jnp.where(jnp.logical_and(in_left, qb > 0), s_lat[:, :Q_BLOCK], NEG_INF),
            s_lat[:, Q_BLOCK:2 * Q_BLOCK],
            jnp.where(jnp.logical_and(in_right, qb < n_blocks - 1), s_lat[:, 2 * Q_BLOCK:], NEG_INF),
        ]
        m = sk
        for s in blocks:
            m = jnp.maximum(m, jnp.max(s, axis=1, keepdims=True))
        den = jnp.exp2(sk - m)
        lo = 0
        for s in blocks:
            e = jnp.exp2(s - m)
            den = den + jnp.sum(e, axis=1, keepdims=True)
            e_scr[sub, :, lo:lo + s.shape[1]] = e.astype(BF16)
            lo += s.shape[1]
        o = (_dot(e_scr[sub, :, :PAST_LEN], v4c[...])
             + _dot(e_scr[sub, :, PAST_LEN:], vp[band, :])) * (1.0 / den)
        acc = _gather_group_outputs(o, Q_BLOCK)
        y_ref[qrows, :] = (acc * _silu(z_ref[qrows, :].astype(F32))).astype(BF16)


def _lat_attn(sink, q, k4, v4, kc, vc, j, z):
    padded = DEC_SEQ + 2 * Q_BLOCK
    rows4 = ATTN_GROUP * Q_BLOCK
    row0 = N_PROMPT_TOK // LAT_STEP
    seq0 = N_PROMPT_TOK // DEC_SEQ
    stage1 = lambda t: _lat_block(jnp.minimum(t, LAT_BLOCKS - 1))
    stage2 = lambda t: _lat_block(jnp.maximum(t - 1, 0))

    def spec(shape, stage, index):
        return pl.BlockSpec(shape, lambda t: index(*stage(t)))

    tile = (LAT_STEP, MXU_N)
    seq = (DEC_SEQ, MXU_N)
    ctx = (None, None, None, ATTN_HEAD_DIM, PAST_LEN)
    return pl.pallas_call(
        _lat_attn_kernel,
        out_shape=jax.ShapeDtypeStruct((N_SAMPLE_TOK, ATTN_WIDTH), BF16),
        grid=(LAT_BLOCKS + 1,),
        in_specs=[pl.BlockSpec(memory_space=pltpu.SMEM),
                  spec(tile, stage1, lambda b, h, n: (row0 + b * LAT_PER_SEQ + n, h)),
                  spec(seq, stage1, lambda b, h, n: (seq0 + b, h)),
                  spec(seq, stage2, lambda b, h, n: (seq0 + b, h)),
                  spec(ctx, stage1, lambda b, h, n: (b, j, h, 0, 0)),
                  spec(ctx, stage2, lambda b, h, n: (b, j, h, 0, 0)),
                  spec(tile, stage2, lambda b, h, n: (row0 + b * LAT_PER_SEQ + n, h))],
        out_specs=spec(tile, stage2, lambda b, h, n: (b * LAT_PER_SEQ + n, h)),
        scratch_shapes=[pltpu.VMEM((padded, MXU_N), BF16), pltpu.VMEM((padded, MXU_N), BF16),
                        pltpu.VMEM((MXU_N, PAST_LEN), BF16), pltpu.VMEM((PAST_LEN, MXU_N), BF16),
                        pltpu.VMEM((Q_SUB, rows4, PAST_LEN), F32),
                        pltpu.VMEM((Q_SUB, rows4, 3 * Q_BLOCK), F32),
                        pltpu.VMEM((Q_SUB, rows4, PAST_LEN), F32),
                        pltpu.VMEM((Q_SUB, rows4, 3 * Q_BLOCK), F32),
                        pltpu.VMEM((Q_SUB, rows4, PAST_LEN + 3 * Q_BLOCK), BF16)],
        compiler_params=_params("arbitrary", vmem=VMEM_LIMIT),
        name="lat_attn",
    )(sink, q, k4, v4, kc, vc, z)


def _out_final_kernel(yp_ref, ys_ref, wf_ref, x_ref, gate_ref, fg_ref, op_ref, os_ref, w_ref):
    t = FINAL
    i = pl.program_id(0)
    _cast_weight_once(wf_ref, w_ref)
    y = _read_parts(t, (yp_ref, ys_ref))
    r = _rms(x_ref[...] + gate_ref[...] * _dot(y, w_ref[...]), fg_ref[...])

    @pl.when(i < t.n_prompt)
    def _():
        op_ref[...] = r

    @pl.when(i >= t.n_prompt)
    def _():
        os_ref[...] = r


def _out_proj_final(mixed, x, final_g):
    t = FINAL
    return pl.pallas_call(
        _out_final_kernel,
        out_shape=(jax.ShapeDtypeStruct((N_PROMPT_TOK, D_MODEL), F32),
                   jax.ShapeDtypeStruct((N_SAMPLE_TOK, D_MODEL), F32)),
        grid=(t.n_tiles,),
        in_specs=_residual_specs(t, mixed, (x,)) + [_const_spec((1, D_MODEL))],
        out_specs=(t.prompt_rows(D_MODEL), t.sample_rows(D_MODEL)),
        scratch_shapes=[pltpu.VMEM(mixed.w_out.shape[1:], BF16)],
        compiler_params=_params("arbitrary", vmem=VMEM_LIMIT),
        name="out_proj_final",
    )(*_residual_args(mixed, (x,)), final_g)


def _pool_in_kernel(*refs, n_y, n_x):
    t = PROJ
    n_res = n_y + n_x + 2
    res_refs, (g_ref, sh_ref, sc_ref, wf_ref, xo_ref, u_ref, z_ref, wo_ref, w_ref) = refs[:n_res], refs[n_res:]
    x = _residual_update(t, res_refs, n_y, n_x, wo_ref, xo_ref)
    _cast_weight_once(wf_ref, w_ref)
    h = _norm_mod(x, g_ref, sh_ref, sc_ref)

    def store_u(c, a):
        u_ref[:, c * MXU_N:(c + 1) * MXU_N] = a

    def store_z(c, a):
        z_ref[:, c * MXU_N:(c + 1) * MXU_N] = a.astype(BF16)

    _proj_chunks(h, w_ref, 0, D_MODEL, store_u)
    _proj_chunks(h, w_ref, D_MODEL, D_MODEL, store_z)


def _pool_in(mixed, x_parts, g, mod, w_all, j):
    t = PROJ
    slab = jax.ShapeDtypeStruct((N_TOK, D_MODEL), F32)
    return pl.pallas_call(
        functools.partial(_pool_in_kernel, n_y=len(mixed.y_parts), n_x=len(x_parts)),
        out_shape=(slab, slab, jax.ShapeDtypeStruct((N_TOK, D_MODEL), BF16)),
        grid=(t.n_tiles,),
        in_specs=_residual_specs(t, mixed, x_parts) + [
            _const_spec((1, D_MODEL)), t.mod(0), t.mod(1), _layer_spec(w_all, j)],
        out_specs=(t.rows(D_MODEL),) * 3,
        scratch_shapes=[pltpu.VMEM(mixed.w_out.shape[1:], BF16), pltpu.VMEM((D_MODEL, 2 * D_MODEL), BF16)],
        compiler_params=_params("arbitrary", vmem=FUSED_VMEM_LIMIT),
        name="pool_in",
    )(*_residual_args(mixed, x_parts), g, mod, mod, w_all)


def _split_bf16(a):
    hi = a.astype(BF16)
    return hi, (a - hi.astype(F32)).astype(BF16)


def _band_ones(shape, lo, hi):
    d = lax.broadcasted_iota(jnp.int32, shape, 1) - lax.broadcasted_iota(jnp.int32, shape, 0)
    return jnp.logical_and(d >= lo, d <= hi).astype(F32).astype(BF16)


def _pool_ret_kernel(u_ref, up_ref, un_ref, z_ref, wgf_ref, ps_ref, wof_ref, x_ref, gate_ref,
                     g_ref, sh_ref, sc_ref, w_ref,
                     xo_ref, q_ref, kt_ref, v_ref, zr_ref,
                     y_ref, wg_ref, wo_ref, band_ref, wkt_ref):
    t = PROJ
    i = pl.program_id(0)
    _cast_weight_once(wgf_ref, wg_ref)
    _cast_weight_once(wof_ref, wo_ref)

    @pl.when(i == 0)
    def _():
        for g, w in enumerate(POOL_WINDOWS):
            band_ref[g] = _band_ones((POOL_SUB, POOL_SUB), -(w // 2), w - 1 - w // 2)
        for r in range(0, RET_QK_WIDTH, MXU_N):
            wk = w_ref[:, RET_QK_WIDTH + r:RET_QK_WIDTH + r + MXU_N]
            wkt_ref[r:r + MXU_N, :] = wk.astype(F32).T.astype(BF16)

    is_dec = i >= t.n_prompt
    st = t.seq_tile(i)
    seq_len = jnp.where(is_dec, DEC_SEQ, SEQ)
    h = POOL_HALO
    n_sub = t.tm // POOL_SUB
    def pool_steps(sub):
        rows = slice(sub * POOL_SUB, (sub + 1) * POOL_SUB)
        if sub == 0:
            before = jnp.where(jnp.logical_and(is_dec, st != 0), up_ref[...], 0.0)
        else:
            before = jnp.where(is_dec, u_ref[sub * POOL_SUB - h:sub * POOL_SUB, :], 0.0)
        if sub == n_sub - 1:
            after = jnp.where(jnp.logical_and(is_dec, st != t.per_dec_seq - 1), un_ref[...], 0.0)
        else:
            after = jnp.where(is_dec, u_ref[(sub + 1) * POOL_SUB:(sub + 1) * POOL_SUB + h, :], 0.0)
        pos0 = jnp.where(is_dec, st * t.tm + sub * POOL_SUB, 0)
        return _pool_core_steps(u_ref.at[rows], before, after, pos0, seq_len, z_ref.at[rows], ps_ref,
                                wg_ref, band_ref, y_ref.at[rows])

    for step in pool_steps(0):
        step()
    for sub in range(n_sub):
        rows = slice(sub * POOL_SUB, (sub + 1) * POOL_SUB)
        x = x_ref[rows, :] + gate_ref[...] * _dot(y_ref[rows, :], wo_ref[...])
        xo_ref[rows, :] = x
        proj = _ret_in_steps(x, g_ref, sh_ref, sc_ref, w_ref, wkt_ref, q_ref.at[rows],
                             kt_ref.at[pl.ds(sub * POOL_SUB // RET_CHUNK, POOL_SUB // RET_CHUNK)],
                             v_ref.at[rows], zr_ref.at[rows])
        _run_interleaved(proj, pool_steps(sub + 1) if sub + 1 < n_sub else [])


def _pool_core_steps(u_ref, before, after, pos0, seq_len, z_ref, ps_ref, wg_ref, band_ref, y_ref):
    n = u_ref.shape[0]
    h = POOL_HALO
    halo_hi, halo_lo = _split_bf16(jnp.concatenate([before, after], axis=0))
    pos = pos0 + lax.broadcasted_iota(jnp.int32, (n, 1), 0)
    rr = lax.broadcasted_iota(jnp.int32, (2 * h, 2 * h), 0)
    cc = lax.broadcasted_iota(jnp.int32, (2 * h, 2 * h), 1)

    def group(g, w):
        def step():
            left = w // 2
            right = w - 1 - left
            cols = slice(g * POOL_GROUP_DIM, (g + 1) * POOL_GROUP_DIM)
            u = u_ref[:, cols]
            s = _dot(band_ref[g], u.astype(BF16))
            top = jnp.logical_and(jnp.logical_and(rr < h, cc < h), cc - h >= rr - left)
            bot = jnp.logical_and(jnp.logical_and(rr >= h, cc >= h), cc - h <= rr - 2 * h + right)
            edge = jnp.logical_or(top, bot).astype(F32).astype(BF16)
            se = _dot(edge, halo_hi[:, cols]) + _dot(edge, halo_lo[:, cols])
            s = jnp.concatenate([s[:h] + se[:h], s[h:n - h], s[n - h:] + se[h:]], axis=0)
            cnt = (jnp.minimum(pos + right + 1, seq_len) - jnp.maximum(pos - left, 0)).astype(F32)
            d = (s / cnt - u).astype(BF16)
            yg = _dot(d, wg_ref[g]) * ps_ref[:, cols] * _silu(z_ref[:, cols].astype(F32))
            y_ref[:, cols] = yg.astype(BF16)
        return step

    return [group(g, w) for g, w in enumerate(POOL_WINDOWS)]


def _pool_ret(u, z, wg_all, ps, wo_all, j, x, mod_pool, g, mod, w_ret):
    t = PROJ
    per = t.tm // POOL_HALO
    n_halo = N_TOK // POOL_HALO
    n_in = 2 * RET_QK_WIDTH + 2 * RET_V_WIDTH
    kt_per = t.tm // RET_CHUNK
    return pl.pallas_call(
        _pool_ret_kernel,
        out_shape=(jax.ShapeDtypeStruct((N_TOK, D_MODEL), F32),
                   jax.ShapeDtypeStruct((N_TOK, RET_QK_WIDTH), BF16),
                   jax.ShapeDtypeStruct((N_TOK // RET_CHUNK, RET_QK_WIDTH, RET_CHUNK), BF16),
                   jax.ShapeDtypeStruct((N_TOK, RET_V_WIDTH), BF16),
                   jax.ShapeDtypeStruct((N_TOK, RET_V_WIDTH), BF16)),
        grid=(t.n_tiles,),
        in_specs=[
            t.rows(D_MODEL),
            pl.BlockSpec((POOL_HALO, D_MODEL), lambda i: (jnp.maximum(i * per - 1, 0), 0)),
            pl.BlockSpec((POOL_HALO, D_MODEL), lambda i: (jnp.minimum((i + 1) * per, n_halo - 1), 0)),
            t.rows(D_MODEL),
            _layer_spec(wg_all, j),
            _const_spec((1, D_MODEL)),
            _layer_spec(wo_all, j),
            t.rows(D_MODEL),
            t.mod(2),
            _const_spec((1, D_MODEL)), t.mod(0), t.mod(1), _const_spec((D_MODEL, n_in)),
        ],
        out_specs=(t.rows(D_MODEL), t.rows(RET_QK_WIDTH),
                   pl.BlockSpec((kt_per, RET_QK_WIDTH, RET_CHUNK), lambda i: (i, 0, 0)),
                   t.rows(RET_V_WIDTH), t.rows(RET_V_WIDTH)),
        scratch_shapes=[pltpu.VMEM((t.tm, D_MODEL), BF16),
                        pltpu.VMEM((len(POOL_WINDOWS), POOL_GROUP_DIM, POOL_GROUP_DIM), BF16),
                        pltpu.VMEM((D_MODEL, D_MODEL), BF16),
                        pltpu.VMEM((len(POOL_WINDOWS), POOL_SUB, POOL_SUB), BF16),
                        pltpu.VMEM((RET_QK_WIDTH, D_MODEL), BF16)],
        compiler_params=_params("arbitrary", vmem=FUSED_VMEM_LIMIT),
        name="pool_ret",
    )(u, u, u, z, wg_all, ps, wo_all, x, mod_pool, g, mod, mod, w_ret)


def _ret_in_steps(x, g_ref, sh_ref, sc_ref, w_ref, wkt_ref, q_ref, kt_ref, v_ref, z_ref):
    h = _norm_mod(x, g_ref, sh_ref, sc_ref)

    def chunk(ref, lo, c):
        def step():
            cols = slice(c * MXU_N, (c + 1) * MXU_N)
            ref[:, cols] = _dot(h, w_ref[:, lo + c * MXU_N:lo + (c + 1) * MXU_N]).astype(BF16)
        return step

    def key_chunk(c):
        def step():
            rows = slice(c * MXU_N, (c + 1) * MXU_N)
            kt = (_dot_nt(wkt_ref[rows, :], h) * RET_KEY_DIM ** -0.5).astype(BF16)
            for cc in range(x.shape[0] // RET_CHUNK):
                kt_ref[cc, rows, :] = kt[:, cc * RET_CHUNK:(cc + 1) * RET_CHUNK]
        return step

    steps = [chunk(q_ref, 0, c) for c in range(RET_QK_WIDTH // MXU_N)]
    steps += [key_chunk(c) for c in range(RET_QK_WIDTH // MXU_N)]
    steps += [chunk(v_ref, 2 * RET_QK_WIDTH, c) for c in range(RET_V_WIDTH // MXU_N)]
    steps += [chunk(z_ref, 2 * RET_QK_WIDTH + RET_V_WIDTH, c) for c in range(RET_V_WIDTH // MXU_N)]
    return steps


def _run_interleaved(main, side):
    done = 0
    for k, step in enumerate(main):
        step()
        due = (k + 1) * len(side) // len(main)
        for s in side[done:due]:
            s()
        done = due


def _pos(shape, axis):
    return lax.broadcasted_iota(jnp.int32, shape, axis).astype(F32)


def _ret_tables_kernel(lgf_ref, lgb_ref, decay_ref, row_ref, col_ref, cdec_ref):
    h = pl.program_id(0)
    lg_f = lgf_ref[h]
    lg_b = lgb_ref[h]
    c = RET_CHUNK
    diff = _pos((c, c), 0) - _pos((c, c), 1)
    fwd = jnp.where(diff >= 0, jnp.exp(jnp.maximum(diff, 0.0) * lg_f), 0.0)
    bwd = jnp.where(diff <= 0, jnp.exp(jnp.maximum(-diff, 0.0) * lg_b), 0.0)
    decay_ref[...] = fwd + bwd
    j = _pos((RET_TAB_ROWS, c), 1)
    row_ref[0] = jnp.exp((c - 1.0 - j) * lg_f)
    row_ref[1] = jnp.exp(j * lg_b)
    i = _pos((c, LANES), 0)
    col_ref[0] = jnp.exp((i + 1.0) * lg_f)
    col_ref[1] = jnp.exp((c - i) * lg_b)
    full = jnp.full((RET_TAB_ROWS, RET_VAL_DIM), float(c), F32)
    cdec_ref[0] = jnp.exp(full * lg_f)
    cdec_ref[1] = jnp.exp(full * lg_b)


def _ret_tables(lg_f, lg_b):
    smem = pl.BlockSpec(memory_space=pltpu.SMEM)
    c = RET_CHUNK
    shapes = ((c, c), (2, RET_TAB_ROWS, c), (2, c, LANES), (2, RET_TAB_ROWS, RET_VAL_DIM))
    return pl.pallas_call(
        _ret_tables_kernel,
        out_shape=tuple(jax.ShapeDtypeStruct((RET_HEADS,) + s, F32) for s in shapes),
        grid=(RET_HEADS,),
        in_specs=[smem, smem],
        out_specs=tuple(pl.BlockSpec((None,) + s, lambda h, n=len(s): (h,) + (0,) * n) for s in shapes),
        compiler_params=_params("arbitrary"),
        name="ret_tables",
    )(lg_f, lg_b)


def _group_norm_gate(o, gn, z):
    mu = jnp.mean(o, axis=-1, keepdims=True)
    var = jnp.mean(jnp.square(o - mu), axis=-1, keepdims=True)
    on = (o - mu) * lax.rsqrt(var + EPS)
    return (on * gn * _silu(z.astype(F32))).astype(BF16)


def _ret_ctx_kernel(q_ref, kt_ref, v_ref, z_ref, gn_ref, decay_ref, row_ref, y_ref, sf_ref, sb_ref):
    for s in range(RET_CTX_SEQS):
        rows = slice(s * SEQ, (s + 1) * SEQ)
        for h in range(RET_HEADS):
            kc = slice(h * RET_KEY_DIM, (h + 1) * RET_KEY_DIM)
            vc = slice(h * RET_VAL_DIM, (h + 1) * RET_VAL_DIM)
            kt = kt_ref[s, kc, :]
            v = v_ref[rows, vc]
            att = (_dot(q_ref[rows, kc], kt) * decay_ref[h]).astype(BF16)
            y_ref[rows, vc] = _group_norm_gate(_dot(att, v), gn_ref[:, vc], z_ref[rows, vc])
            ktf = kt.astype(F32)
            sf_ref[s, h] = _dot((ktf * row_ref[h, 0, 0:1, :]).astype(BF16), v)
            sb_ref[s, h] = _dot((ktf * row_ref[h, 1, 0:1, :]).astype(BF16), v)


def _ret_ctx(q, kt, v, z, gn, decay, row):
    c = RET_CHUNK
    n = RET_CTX_SEQS
    wide = lambda w: pl.BlockSpec((n * SEQ, w), lambda b: (b, 0))
    st_spec = pl.BlockSpec((n, RET_HEADS, RET_KEY_DIM, RET_VAL_DIM), lambda b: (b, 0, 0, 0))
    st_shape = jax.ShapeDtypeStruct((BATCH, RET_HEADS, RET_KEY_DIM, RET_VAL_DIM), F32)
    return pl.pallas_call(
        _ret_ctx_kernel,
        out_shape=(jax.ShapeDtypeStruct((N_PROMPT_TOK, RET_V_WIDTH), BF16), st_shape, st_shape),
        grid=(BATCH // n,),
        in_specs=[wide(RET_QK_WIDTH), pl.BlockSpec((n, RET_QK_WIDTH, c), lambda b: (b, 0, 0)),
                  wide(RET_V_WIDTH), wide(RET_V_WIDTH), _const_spec((1, RET_V_WIDTH)),
                  _const_spec((RET_HEADS, c, c)), _const_spec((RET_HEADS, 2, RET_TAB_ROWS, c))],
        out_specs=(wide(RET_V_WIDTH), st_spec, st_spec),
        compiler_params=_params("arbitrary", vmem=VMEM_LIMIT),
        name="ret_ctx",
    )(q, kt, v, z, gn, decay, row)


def _ret_lat_kernel(q_ref, kt_ref, v_ref, z_ref, gn_ref, decay_ref, row_ref, col_ref, cdec_ref,
                    s0f_ref, s0b_ref, y_ref, sf_all, sb_all, sf_acc, sb_acc):
    c = RET_CHUNK
    n_chunks = DEC_SEQ // c
    rows_of = lambda ci: pl.ds(pl.multiple_of(ci * c, c), c)

    sf_acc[...] = s0f_ref[...]
    sb_acc[...] = s0b_ref[...]

    def scan_step(i, carry):
        cf = i
        cb = n_chunks - 1 - i
        sf_all[cf] = sf_acc[...].astype(BF16)
        sb_all[cb] = sb_acc[...].astype(BF16)
        uf = _dot((kt_ref[cf].astype(F32) * row_ref[0, 0:1, :]).astype(BF16), v_ref[rows_of(cf), :])
        ub = _dot((kt_ref[cb].astype(F32) * row_ref[1, 0:1, :]).astype(BF16), v_ref[rows_of(cb), :])
        sf_acc[...] = sf_acc[...] * cdec_ref[0, 0:1, :] + uf
        sb_acc[...] = sb_acc[...] * cdec_ref[1, 0:1, :] + ub
        return carry

    lax.fori_loop(0, n_chunks, scan_step, 0, unroll=4)

    def out_step(ci, carry):
        rows = rows_of(ci)
        q = q_ref[rows, :]
        qf = q.astype(F32)
        qdec_f = jnp.concatenate([col_ref[0]] * (RET_KEY_DIM // LANES), axis=1)
        qdec_b = jnp.concatenate([col_ref[1]] * (RET_KEY_DIM // LANES), axis=1)
        att = (_dot(q, kt_ref[ci]) * decay_ref[...]).astype(BF16)
        o = (_dot(att, v_ref[rows, :])
             + _dot((qf * qdec_f).astype(BF16), sf_all[ci])
             + _dot((qf * qdec_b).astype(BF16), sb_all[ci]))
        y_ref[rows, :] = _group_norm_gate(o, gn_ref[...], z_ref[rows, :])
        return carry

    lax.fori_loop(0, n_chunks, out_step, 0, unroll=8)


def _ret_lat(q, kt, v, z, gn, decay, row, col, cdec, s0f, s0b):
    c = RET_CHUNK
    n_chunks = DEC_SEQ // c
    row0 = N_PROMPT_TOK // DEC_SEQ
    qk_spec = pl.BlockSpec((DEC_SEQ, RET_KEY_DIM), lambda b, h: (row0 + b, h))
    v_spec = pl.BlockSpec((DEC_SEQ, RET_VAL_DIM), lambda b, h: (row0 + b, h))
    st_spec = pl.BlockSpec((None, None, RET_KEY_DIM, RET_VAL_DIM), lambda b, h: (b, h, 0, 0))
    tab = lambda *s: pl.BlockSpec((None,) + s, lambda b, h: (h,) + (0,) * len(s))
    states = pltpu.VMEM((n_chunks, RET_KEY_DIM, RET_VAL_DIM), BF16)
    acc = pltpu.VMEM((RET_KEY_DIM, RET_VAL_DIM), F32)
    return pl.pallas_call(
        _ret_lat_kernel,
        out_shape=jax.ShapeDtypeStruct((N_SAMPLE_TOK, RET_V_WIDTH), BF16),
        grid=(DEC_BATCH, RET_HEADS),
        in_specs=[qk_spec,
                  pl.BlockSpec((n_chunks, RET_KEY_DIM, c), lambda b, h: (row0 + b, h, 0)),
                  v_spec, v_spec, pl.BlockSpec((1, RET_VAL_DIM), lambda b, h: (0, h)),
                  tab(c, c), tab(2, RET_TAB_ROWS, c), tab(2, c, LANES), tab(2, RET_TAB_ROWS, RET_VAL_DIM),
                  st_spec, st_spec],
        out_specs=pl.BlockSpec((DEC_SEQ, RET_VAL_DIM), lambda b, h: (b, h)),
        scratch_shapes=[states, states, acc, acc],
        compiler_params=_params("arbitrary", "arbitrary", vmem=VMEM_LIMIT),
        name="ret_lat",
    )(q, kt, v, z, gn, decay, row, col, cdec, s0f, s0b)


def _rope_tables(tm):
    n_rows = DEC_SEQ // GRID_W
    rows = jnp.repeat(jnp.arange(n_rows), GRID_W).astype(F32)
    cols = jnp.tile(jnp.arange(GRID_W), n_rows).astype(F32)
    half = ATTN_HEAD_DIM // 4
    inv = ROPE_BASE ** (-jnp.arange(half, dtype=F32) / half)
    ang_r = rows[:, None] * inv[None, :]
    ang_c = cols[:, None] * inv[None, :]
    cos = jnp.concatenate([jnp.cos(ang_r), jnp.cos(ang_r), jnp.cos(ang_c), jnp.cos(ang_c)], axis=-1)
    sin = jnp.concatenate([-jnp.sin(ang_r), jnp.sin(ang_r), -jnp.sin(ang_c), jnp.sin(ang_c)], axis=-1)
    cos = jnp.concatenate([jnp.ones((tm, ATTN_HEAD_DIM), F32), cos], axis=0)
    sin = jnp.concatenate([jnp.zeros((tm, ATTN_HEAD_DIM), F32), sin], axis=0)
    return jnp.tile(cos, (1, 2)), jnp.tile(sin, (1, 2))


def kernel(x_prompt, x_sample, cache_k, cache_v, state_fwd, state_bwd, c, c_ctx, norm_g, ada_w, ada_b, attn_w_in, attn_w_out, attn_sink, pool_w_in, pool_w_grp, pool_scale, pool_w_out, ret_w_in, ret_decay_fwd, ret_decay_bwd, ret_gn_g, ret_w_out, final_g):
    x_parts = (x_prompt.reshape(N_PROMPT_TOK, D_MODEL), x_sample.reshape(N_SAMPLE_TOK, D_MODEL))
    cond = jnp.concatenate([c_ctx[None, :], c,
                            jnp.zeros((N_COND - 1 - DEC_BATCH, D_MODEL), F32)], axis=0)
    mods = _ada_table(cond.T, ada_w, ada_b).reshape(DEPTH, N_COND, 1, 3 * D_MODEL)

    to_kernel = lambda a: jnp.transpose(a, (0, 1, 3, 4, 2))
    from_kernel = lambda a: jnp.transpose(
        a.reshape(a.shape[0], a.shape[1], ATTN_KV_HEADS, ATTN_HEAD_DIM, a.shape[3]), (0, 1, 4, 2, 3))
    ctx_k, ctx_v = to_kernel(cache_k), to_kernel(cache_v)

    assert DEPTH % N_MIXERS == 1, "the layer stack must end on an attention layer"
    caches = ()
    new_sf = new_sb = None
    mixed = None
    for i in range(DEPTH):
        kind, j = i % N_MIXERS, i // N_MIXERS
        g = norm_g[i].reshape(1, D_MODEL)
        mod = mods[i]
        if kind == 0:
            outs = _attn_in(mixed, x_parts, g, mod, attn_w_in, j, tuple(caches))
            if mixed is not None:
                x_parts, outs = (outs[0],), outs[1:]
            q, k4, v4, z, *caches = outs
            y_parts = (_ctx_attn(attn_sink[j], q, k4, v4, z),
                       _lat_attn(attn_sink[j], q, k4, v4, ctx_k, ctx_v, j, z))
            mixed = Mixed(y_parts, attn_w_out, j, mod)
        elif kind == 1:
            x, u, z = _pool_in(mixed, x_parts, g, mod, pool_w_in, j)
            x_parts, mixed, pooled = (x,), None, (u, z, j, mod)
        else:
            u, z, jp, mod_pool = pooled
            lg_f = jax.nn.log_sigmoid(ret_decay_fwd[j].astype(F32))
            lg_b = jax.nn.log_sigmoid(ret_decay_bwd[j].astype(F32))
            gn = ret_gn_g[j].reshape(1, RET_V_WIDTH)
            x, q, kt, v, z = _pool_ret(u, z, pool_w_grp, pool_scale[jp].reshape(1, D_MODEL), pool_w_out, jp,
                                       x_parts[0], mod_pool, g, mod, ret_w_in[j].astype(BF16))
            x_parts = (x,)
            decay, row, col, cdec = _ret_tables(lg_f, lg_b)
            y_ctx, new_sf, new_sb = _ret_ctx(q, kt, v, z, gn, decay, row)
            y_parts = (y_ctx, _ret_lat(q, kt, v, z, gn, decay, row, col, cdec,
                                       state_fwd[:, j], state_bwd[:, j]))
            mixed = Mixed(y_parts, ret_w_out, j, mod)
    y_prompt, y_sample = _out_proj_final(mixed, x_parts[0], final_g.reshape(1, D_MODEL))
    new_k, new_v = caches
    return (y_prompt.reshape(BATCH, SEQ, D_MODEL), y_sample.reshape(DEC_BATCH, DEC_SEQ, D_MODEL),
            from_kernel(new_k), from_kernel(new_v), new_sf[:, None], new_sb[:, None])
```

```python
import functools
from typing import NamedTuple

import jax
import jax.numpy as jnp
from jax import lax
from jax.experimental import pallas as pl
from jax.experimental.pallas import tpu as pltpu

F32 = jnp.float32
BF16 = jnp.bfloat16

D_MODEL = 1024
BATCH = 16
SEQ = 256
DEPTH = 4
DEC_BATCH = 2
DEC_SEQ = 2048
PAST_LEN = 512
GRID_W = 64
N_MIXERS = 3
ATTN_HEADS = 16
ATTN_KV_HEADS = 4
ATTN_HEAD_DIM = 64
ATTN_GROUP = 4
ATTN_WIDTH = 1024
ATTN_KV_WIDTH = 256
WINDOW = 128
ROPE_BASE = 10000.0
POOL_WINDOWS = (2, 4, 8, 16)
POOL_GROUP_DIM = 256
RET_HEADS = 4
RET_KEY_DIM = 256
RET_VAL_DIM = 512
RET_QK_WIDTH = 1024
RET_V_WIDTH = 2048
EPS = 1e-6
NEG_INF = -1e30
LOG2_E = 1.4426950408889634

N_PROMPT_TOK = BATCH * SEQ
N_SAMPLE_TOK = DEC_BATCH * DEC_SEQ
N_TOK = N_PROMPT_TOK + N_SAMPLE_TOK
N_COND = 8
LANES = 128
MXU_N = 256
Q_BLOCK = 128
Q_SUB = 4
RET_CHUNK = 256
RET_TAB_ROWS = 8
RET_CTX_SEQS = 2
POOL_HALO = 8
POOL_SUB = SEQ
WEIGHT_STAGE_COLS = 512
VMEM_LIMIT = 48 * 1024 * 1024
FUSED_VMEM_LIMIT = 58 * 1024 * 1024


class Tiling(NamedTuple):
    tm: int

    @property
    def n_tiles(self):
        return N_TOK // self.tm

    @property
    def n_prompt(self):
        return N_PROMPT_TOK // self.tm

    @property
    def per_dec_seq(self):
        return DEC_SEQ // self.tm

    def cond(self, i):
        return jnp.where(i < self.n_prompt, 0, 1 + (i - self.n_prompt) // self.per_dec_seq)

    def seq_tile(self, i):
        return jnp.where(i < self.n_prompt, 0, (i - self.n_prompt) % self.per_dec_seq)

    def rows(self, width):
        return pl.BlockSpec((self.tm, width), lambda i: (i, 0))

    def prompt_rows(self, width):
        return pl.BlockSpec((self.tm, width), lambda i: (jnp.minimum(i, self.n_prompt - 1), 0))

    def sample_rows(self, width):
        return pl.BlockSpec((self.tm, width), lambda i: (jnp.maximum(i - self.n_prompt, 0), 0))

    def mod(self, part):
        return pl.BlockSpec((None, 1, D_MODEL), lambda i: (self.cond(i), 0, part))


PROJ = Tiling(512)
FINAL = Tiling(1024)


def _silu(z):
    return z * (1.0 / (1.0 + jnp.exp(-z)))


def _dot(a, b):
    return jnp.dot(a, b, preferred_element_type=F32)


def _dot_nt(a, b):
    return lax.dot_general(a, b, (((1,), (1,)), ((), ())), preferred_element_type=F32)


def _params(*sem, vmem=None):
    return pltpu.CompilerParams(dimension_semantics=sem, vmem_limit_bytes=vmem)


def _const_spec(shape):
    nd = len(shape)
    return pl.BlockSpec(shape, lambda *_: (0,) * nd, pipeline_mode=pl.Buffered(1))


def _part_specs(t, parts, width):
    if len(parts) == 1:
        return [t.rows(width)]
    return [t.prompt_rows(width), t.sample_rows(width)]


def _read_parts(t, refs):
    if len(refs) == 1:
        return refs[0][...]
    return jnp.where(pl.program_id(0) < t.n_prompt, refs[0][...], refs[1][...])


def _ada_kernel(cond_ref, w_ref, b_ref, o_ref):
    s = _silu(cond_ref[...])
    w = w_ref[...]
    rows = [jnp.sum(s[:, c:c + 1] * w, axis=0, keepdims=True) + b_ref[...] for c in range(1 + DEC_BATCH)]
    rows.append(jnp.zeros((N_COND - len(rows), w.shape[1]), F32))
    o_ref[...] = jnp.concatenate(rows, axis=0)


def _ada_table(cond, ada_w, ada_b):
    tn = 3 * D_MODEL // 2
    return pl.pallas_call(
        _ada_kernel,
        out_shape=jax.ShapeDtypeStruct((DEPTH, N_COND, 3 * D_MODEL), F32),
        grid=(DEPTH, 3 * D_MODEL // tn),
        in_specs=[
            pl.BlockSpec((D_MODEL, N_COND), lambda l, n: (0, 0)),
            pl.BlockSpec((None, D_MODEL, tn), lambda l, n: (l, 0, n)),
            pl.BlockSpec((None, 1, tn), lambda l, n: (l, 0, n)),
        ],
        out_specs=pl.BlockSpec((None, N_COND, tn), lambda l, n: (l, 0, n)),
        compiler_params=_params("arbitrary", "arbitrary", vmem=VMEM_LIMIT),
        name="ada_table",
    )(cond, ada_w, ada_b.reshape(DEPTH, 1, 3 * D_MODEL))


def _rms(x, g):
    return x * lax.rsqrt(jnp.mean(x * x, axis=-1, keepdims=True) + EPS) * g


def _norm_mod(x, g_ref, sh_ref, sc_ref):
    return (_rms(x, g_ref[...]) * (1.0 + sc_ref[...]) + sh_ref[...]).astype(BF16)


def _proj_chunks(h, w_ref, lo, width, store):
    for c in range(width // MXU_N):
        store(c, _dot(h, w_ref[:, lo + c * MXU_N:lo + (c + 1) * MXU_N]))


def _layer_spec(w, j):
    nd = w.ndim - 1
    return pl.BlockSpec((None,) + w.shape[1:], lambda *_: (j,) + (0,) * nd, pipeline_mode=pl.Buffered(1))


def _cast_weight_once(w_ref, wb_ref):
    @pl.when(pl.program_id(0) == 0)
    def _():
        rows = wb_ref.shape[-2]
        for r in range(0, rows, MXU_N):
            wb_ref[..., r:r + MXU_N, :] = w_ref[..., r:r + MXU_N, :].astype(BF16)


class Mixed(NamedTuple):
    y_parts: tuple
    w_out: jax.Array
    j: int
    mod: jax.Array


def _residual_specs(t, mixed, x_parts):
    k = mixed.w_out.shape[1]
    return (_part_specs(t, mixed.y_parts, k) + [_layer_spec(mixed.w_out, mixed.j)]
            + _part_specs(t, x_parts, D_MODEL) + [t.mod(2)])


def _residual_args(mixed, x_parts):
    return (*mixed.y_parts, mixed.w_out, *x_parts, mixed.mod)


def _residual_update(t, refs, n_y, n_x, wo_ref, xo_ref):
    y_refs, wof_ref, x_refs, gate_ref = refs[:n_y], refs[n_y], refs[n_y + 1:n_y + 1 + n_x], refs[n_y + 1 + n_x]
    _cast_weight_once(wof_ref, wo_ref)
    x = _read_parts(t, x_refs) + gate_ref[...] * _dot(_read_parts(t, y_refs), wo_ref[...])
    xo_ref[...] = x
    return x


def _rep4(a, h):
    half = a[:, (h // 2) * LANES:(h // 2 + 1) * LANES]
    lane = lax.broadcasted_iota(jnp.int32, half.shape, 1)
    keep = (lane < ATTN_HEAD_DIM) if h % 2 == 0 else (lane >= ATTN_HEAD_DIM)
    m = jnp.where(keep, half, 0.0)
    s = m + pltpu.roll(m, ATTN_HEAD_DIM, 1)
    return jnp.concatenate([s, s], axis=1)


def _attn_in_kernel(*refs, j, n_y, n_x, n_alias):
    t = PROJ
    n_res = n_y + n_x + 2 if n_y else n_x
    res_refs, refs = refs[:n_res], refs[n_res:]
    (g_ref, sh_ref, sc_ref, wf_ref, cos_ref, sin_ref), refs = refs[:6], refs[6 + n_alias:]
    if n_y:
        xo_ref, q_ref, k4_ref, v4_ref, z_ref, kc_ref, vc_ref, wo_ref, w_ref = refs
        x = _residual_update(t, res_refs, n_y, n_x, wo_ref, xo_ref)
    else:
        q_ref, k4_ref, v4_ref, z_ref, kc_ref, vc_ref, w_ref = refs
        x = _read_parts(t, res_refs)
    i = pl.program_id(0)
    _cast_weight_once(wf_ref, w_ref)
    h = _norm_mod(x, g_ref, sh_ref, sc_ref)
    cos = cos_ref[...]
    sin = sin_ref[...]
    lane = lax.broadcasted_iota(jnp.int32, (t.tm, LANES), 1)
    first = (lane % (ATTN_HEAD_DIM // 2)) < ATTN_HEAD_DIM // 4

    def rope(a):
        rot = jnp.where(first, pltpu.roll(a, LANES - ATTN_HEAD_DIM // 4, 1),
                        pltpu.roll(a, ATTN_HEAD_DIM // 4, 1))
        return a * cos + rot * sin

    def rope_wide(a):
        return jnp.concatenate(
            [rope(a[:, s * LANES:(s + 1) * LANES]) for s in range(MXU_N // LANES)], axis=1)

    scale = ATTN_HEAD_DIM ** -0.5 * LOG2_E

    def store_q(c, a):
        q_ref[:, c * MXU_N:(c + 1) * MXU_N] = (rope_wide(a) * scale).astype(BF16)

    def store_z(c, a):
        z_ref[:, c * MXU_N:(c + 1) * MXU_N] = a.astype(BF16)

    _proj_chunks(h, w_ref, 0, ATTN_WIDTH, store_q)
    k = rope_wide(_dot(h, w_ref[:, ATTN_WIDTH:ATTN_WIDTH + ATTN_KV_WIDTH]))
    v = _dot(h, w_ref[:, ATTN_WIDTH + ATTN_KV_WIDTH:ATTN_WIDTH + 2 * ATTN_KV_WIDTH])
    for hh in range(ATTN_KV_HEADS):
        k4_ref[:, hh * MXU_N:(hh + 1) * MXU_N] = _rep4(k, hh).astype(BF16)
        v4_ref[:, hh * MXU_N:(hh + 1) * MXU_N] = _rep4(v, hh).astype(BF16)
    _proj_chunks(h, w_ref, ATTN_WIDTH + 2 * ATTN_KV_WIDTH, ATTN_WIDTH, store_z)

    @pl.when(i < t.n_prompt)
    def _():
        for s in range(t.tm // SEQ):
            kt = k[s * SEQ:(s + 1) * SEQ, :].T
            vt = v[s * SEQ:(s + 1) * SEQ, :].T
            if n_alias:
                kc_ref[s] = kt
                vc_ref[s] = vt
            else:
                for l in range(kc_ref.shape[1]):
                    kc_ref[s, l] = kt if l == j else jnp.zeros_like(kt)
                    vc_ref[s, l] = vt if l == j else jnp.zeros_like(vt)


def _attn_in(mixed, x_parts, g, mod, w_all, j, cos_t, sin_t, caches):
    t = PROJ
    n_in = 2 * ATTN_WIDTH + 2 * ATTN_KV_WIDTH
    per = t.tm // SEQ
    rope_spec = pl.BlockSpec(
        (t.tm, LANES), lambda i: (jnp.where(i < t.n_prompt, 0, 1 + t.seq_tile(i)), 0))
    wide = jax.ShapeDtypeStruct((N_TOK, ATTN_WIDTH), BF16)
    n_attn = w_all.shape[0]
    cache = jax.ShapeDtypeStruct((BATCH, n_attn, ATTN_KV_WIDTH, SEQ), F32)
    if caches:
        cache_spec = pl.BlockSpec((per, None, ATTN_KV_WIDTH, SEQ),
                                  lambda i: (jnp.minimum(i, t.n_prompt - 1), j, 0, 0))
    else:
        cache_spec = pl.BlockSpec((per, n_attn, ATTN_KV_WIDTH, SEQ),
                                  lambda i: (jnp.minimum(i, t.n_prompt - 1), 0, 0, 0))
    n_x = len(x_parts)
    if mixed is None:
        n_y, res_specs, res_args, res_out, res_ospecs, res_scratch = 0, _part_specs(t, x_parts, D_MODEL), x_parts, (), (), []
    else:
        n_y = len(mixed.y_parts)
        res_specs, res_args = _residual_specs(t, mixed, x_parts), _residual_args(mixed, x_parts)
        res_out = (jax.ShapeDtypeStruct((N_TOK, D_MODEL), F32),)
        res_ospecs = (t.rows(D_MODEL),)
        res_scratch = [pltpu.VMEM(mixed.w_out.shape[1:], BF16)]
    n_front = len(res_specs) + 6
    return pl.pallas_call(
        functools.partial(_attn_in_kernel, j=j, n_y=n_y, n_x=n_x, n_alias=len(caches)),
        out_shape=res_out + (wide, wide, wide, wide, cache, cache),
        grid=(t.n_tiles,),
        in_specs=res_specs + [
            _const_spec((1, D_MODEL)), t.mod(0), t.mod(1), _layer_spec(w_all, j),
            rope_spec, rope_spec] + [pl.BlockSpec(memory_space=pl.ANY)] * len(caches),
        out_specs=res_ospecs + (t.rows(ATTN_WIDTH),) * 4 + (cache_spec,) * 2,
        scratch_shapes=res_scratch + [pltpu.VMEM((D_MODEL, n_in), BF16)],
        input_output_aliases={n_front + c: len(res_out) + 4 + c for c in range(len(caches))},
        compiler_params=_params("arbitrary", vmem=FUSED_VMEM_LIMIT),
        name="attn_in",
    )(*res_args, g, mod, mod, w_all, cos_t, sin_t, *caches)


def _stack_group_queries(q):
    qf = q.astype(F32)
    chunk = lax.broadcasted_iota(jnp.int32, qf.shape, 1) // ATTN_HEAD_DIM
    return jnp.concatenate(
        [jnp.where(chunk == g, qf, 0.0) for g in range(ATTN_GROUP)], axis=0).astype(BF16)


def _gather_group_outputs(o, rows):
    chunk = lax.broadcasted_iota(jnp.int32, (rows, MXU_N), 1) // ATTN_HEAD_DIM
    acc = jnp.zeros((rows, MXU_N), F32)
    for g in range(ATTN_GROUP):
        acc = acc + jnp.where(chunk == g, o[g * rows:(g + 1) * rows], 0.0)
    return acc


def _sink_column(sink_ref, h, rows):
    grp = lax.broadcasted_iota(jnp.int32, (ATTN_GROUP * rows, 1), 0) // rows
    col = jnp.zeros((ATTN_GROUP * rows, 1), F32)
    for g in range(ATTN_GROUP):
        col = jnp.where(grp == g, sink_ref[h * ATTN_GROUP + g] * LOG2_E, col)
    return col


def _chunk_rows(dtype):
    chunk = lax.broadcasted_iota(jnp.int32, (1, MXU_N), 1) // ATTN_HEAD_DIM
    return [(chunk == g).astype(F32).astype(dtype) for g in range(ATTN_GROUP)]


def _block_diag_rows(x4):
    return jnp.concatenate([x4 * m for m in _chunk_rows(x4.dtype)], axis=0)


def _ctx_attn_kernel(sink_ref, q_ref, k4_ref, v4_ref, z_ref, y_ref):
    chunk = lax.broadcasted_iota(jnp.int32, (SEQ, MXU_N), 1) // ATTN_HEAD_DIM
    for h in range(ATTN_KV_HEADS):
        cols = slice(h * MXU_N, (h + 1) * MXU_N)
        s = _dot_nt(q_ref[:, cols], _block_diag_rows(k4_ref[:, cols]))
        inv = jnp.zeros((SEQ, MXU_N), F32)
        probs = []
        for g in range(ATTN_GROUP):
            sg = s[:, g * SEQ:(g + 1) * SEQ]
            sk = sink_ref[h * ATTN_GROUP + g] * LOG2_E
            m = jnp.maximum(jnp.max(sg, axis=1, keepdims=True), sk)
            e = jnp.exp2(sg - m)
            den = jnp.sum(e, axis=1, keepdims=True) + jnp.exp2(sk - m)
            probs.append(e.astype(BF16))
            inv = jnp.where(chunk == g, 1.0 / den, inv)
        o = _dot(jnp.concatenate(probs, axis=1), _block_diag_rows(v4_ref[:, cols]))
        y_ref[:, cols] = (o * inv * _silu(z_ref[:, cols].astype(F32))).astype(BF16)


def _ctx_attn(sink, q, k4, v4, z):
    spec = pl.BlockSpec((SEQ, ATTN_WIDTH), lambda b: (b, 0))
    return pl.pallas_call(
        _ctx_attn_kernel,
        out_shape=jax.ShapeDtypeStruct((N_PROMPT_TOK, ATTN_WIDTH), BF16),
        grid=(BATCH,),
        in_specs=[pl.BlockSpec(memory_space=pltpu.SMEM), spec, spec, spec, spec],
        out_specs=spec,
        compiler_params=_params("arbitrary"),
        name="ctx_attn",
    )(sink, q, k4, v4, z)


LAT_STEP = Q_SUB * Q_BLOCK
LAT_PER_SEQ = DEC_SEQ // LAT_STEP
LAT_BLOCKS = DEC_BATCH * ATTN_KV_HEADS * LAT_PER_SEQ


def _lat_block(blk):
    return (blk // (ATTN_KV_HEADS * LAT_PER_SEQ), (blk // LAT_PER_SEQ) % ATTN_KV_HEADS, blk % LAT_PER_SEQ)


def _band_rows(qb):
    return pl.ds(pl.multiple_of(qb * Q_BLOCK, Q_BLOCK), 3 * Q_BLOCK)


def _pad_sequence(dst, src):
    zeros = jnp.zeros((Q_BLOCK, MXU_N), BF16)
    dst[0:Q_BLOCK, :] = zeros
    dst[Q_BLOCK:Q_BLOCK + DEC_SEQ, :] = src[...]
    dst[Q_BLOCK + DEC_SEQ:, :] = zeros


def _lat_attn_kernel(sink_ref, q_ref, k4_ref, v4_ref, kc_ref, vc_ref, z_ref, y_ref,
                     kp, vp, k4c, v4c, sc_a, sl_a, sc_b, sl_b, e_scr):
    t = pl.program_id(0)
    _, _, n1 = _lat_block(jnp.minimum(t, LAT_BLOCKS - 1))
    _, h0, n0 = _lat_block(jnp.maximum(t - 1, 0))
    n_blocks = DEC_SEQ // Q_BLOCK

    @pl.when(t == 0)
    def _():
        sc_b[...] = jnp.zeros(sc_b.shape, F32)
        sl_b[...] = jnp.zeros(sl_b.shape, F32)

    @pl.when(n1 == 0)
    def _():
        _pad_sequence(kp, k4_ref)
        k4c[...] = jnp.concatenate([kc_ref[...]] * ATTN_GROUP, axis=0).astype(BF16)

    @pl.when(n0 == 0)
    def _():
        _pad_sequence(vp, v4_ref)
        v4c[...] = jnp.concatenate([vc_ref[...]] * ATTN_GROUP, axis=0).T.astype(BF16)

    @pl.when(t % 2 == 0)
    def _():
        _lat_stages(sink_ref, q_ref, z_ref, y_ref, kp, vp, k4c, v4c, e_scr, n1, h0, n0,
                    sc_a, sl_a, sc_b, sl_b)

    @pl.when(t % 2 == 1)
    def _():
        _lat_stages(sink_ref, q_ref, z_ref, y_ref, kp, vp, k4c, v4c, e_scr, n1, h0, n0,
                    sc_b, sl_b, sc_a, sl_a)


def _lat_stages(sink_ref, q_ref, z_ref, y_ref, kp, vp, k4c, v4c, e_scr, n1, h0, n0,
                sc_w, sl_w, sc_r, sl_r):
    n_blocks = DEC_SEQ // Q_BLOCK
    for sub in range(Q_SUB):
        qs = _stack_group_queries(q_ref[sub * Q_BLOCK:(sub + 1) * Q_BLOCK, :])
        sc_w[sub] = _dot(qs, k4c[...])
        sl_w[sub] = _dot_nt(qs, kp[_band_rows(n1 * Q_SUB + sub), :])

    rows4 = ATTN_GROUP * Q_BLOCK
    r = lax.broadcasted_iota(jnp.int32, (rows4, Q_BLOCK), 0) % Q_BLOCK
    c = lax.broadcasted_iota(jnp.int32, (rows4, Q_BLOCK), 1)
    in_left = c >= r
    in_right = c <= r
    sk = _sink_column(sink_ref, h0, Q_BLOCK)
    for sub in range(Q_SUB):
        qb = n0 * Q_SUB + sub
        band = _band_rows(qb)
        qrows = slice(sub * Q_BLOCK, (sub + 1) * Q_BLOCK)
        s_ctx = sc_r[sub]
        s_lat = sl_r[sub]
        blocks = [
            s_ctx,
            jnp.where(jnp.logical_and(in_left, qb > 0), s_lat[:, :Q_BLOCK], NEG_INF),
            s_lat[:, Q_BLOCK:2 * Q_BLOCK],
            jnp.where(jnp.logical_and(in_right, qb < n_blocks - 1), s_lat[:, 2 * Q_BLOCK:], NEG_INF),
        ]
        m = sk
        for s in blocks:
            m = jnp.maximum(m, jnp.max(s, axis=1, keepdims=True))
        den = jnp.exp2(sk - m)
        lo = 0
        for s in blocks:
            e = jnp.exp2(s - m)
            den = den + jnp.sum(e, axis=1, keepdims=True)
            e_scr[sub, :, lo:lo + s.shape[1]] = e.astype(BF16)
            lo += s.shape[1]
        o = (_dot(e_scr[sub, :, :PAST_LEN], v4c[...])
             + _dot(e_scr[sub, :, PAST_LEN:], vp[band, :])) * (1.0 / den)
        acc = _gather_group_outputs(o, Q_BLOCK)
        y_ref[qrows, :] = (acc * _silu(z_ref[qrows, :].astype(F32))).astype(BF16)


def _lat_attn(sink, q, k4, v4, kc, vc, j, z):
    padded = DEC_SEQ + 2 * Q_BLOCK
    rows4 = ATTN_GROUP * Q_BLOCK
    row0 = N_PROMPT_TOK // LAT_STEP
    seq0 = N_PROMPT_TOK // DEC_SEQ
    stage1 = lambda t: _lat_block(jnp.minimum(t, LAT_BLOCKS - 1))
    stage2 = lambda t: _lat_block(jnp.maximum(t - 1, 0))

    def spec(shape, stage, index):
        return pl.BlockSpec(shape, lambda t: index(*stage(t)))

    tile = (LAT_STEP, MXU_N)
    seq = (DEC_SEQ, MXU_N)
    ctx = (None, None, None, ATTN_HEAD_DIM, PAST_LEN)
    return pl.pallas_call(
        _lat_attn_kernel,
        out_shape=jax.ShapeDtypeStruct((N_SAMPLE_TOK, ATTN_WIDTH), BF16),
        grid=(LAT_BLOCKS + 1,),
        in_specs=[pl.BlockSpec(memory_space=pltpu.SMEM),
                  spec(tile, stage1, lambda b, h, n: (row0 + b * LAT_PER_SEQ + n, h)),
                  spec(seq, stage1, lambda b, h, n: (seq0 + b, h)),
                  spec(seq, stage2, lambda b, h, n: (seq0 + b, h)),
                  spec(ctx, stage1, lambda b, h, n: (b, j, h, 0, 0)),
                  spec(ctx, stage2, lambda b, h, n: (b, j, h, 0, 0)),
                  spec(tile, stage2, lambda b, h, n: (row0 + b * LAT_PER_SEQ + n, h))],
        out_specs=spec(tile, stage2, lambda b, h, n: (b * LAT_PER_SEQ + n, h)),
        scratch_shapes=[pltpu.VMEM((padded, MXU_N), BF16), pltpu.VMEM((padded, MXU_N), BF16),
                        pltpu.VMEM((MXU_N, PAST_LEN), BF16), pltpu.VMEM((PAST_LEN, MXU_N), BF16),
                        pltpu.VMEM((Q_SUB, rows4, PAST_LEN), F32),
                        pltpu.VMEM((Q_SUB, rows4, 3 * Q_BLOCK), F32),
                        pltpu.VMEM((Q_SUB, rows4, PAST_LEN), F32),
                        pltpu.VMEM((Q_SUB, rows4, 3 * Q_BLOCK), F32),
                        pltpu.VMEM((Q_SUB, rows4, PAST_LEN + 3 * Q_BLOCK), BF16)],
        compiler_params=_params("arbitrary", vmem=VMEM_LIMIT),
        name="lat_attn",
    )(sink, q, k4, v4, kc, vc, z)


def _out_final_kernel(yp_ref, ys_ref, wf_ref, x_ref, gate_ref, fg_ref, op_ref, os_ref, w_ref):
    t = FINAL
    i = pl.program_id(0)
    _cast_weight_once(wf_ref, w_ref)
    y = _read_parts(t, (yp_ref, ys_ref))
    r = _rms(x_ref[...] + gate_ref[...] * _dot(y, w_ref[...]), fg_ref[...])

    @pl.when(i < t.n_prompt)
    def _():
        op_ref[...] = r

    @pl.when(i >= t.n_prompt)
    def _():
        os_ref[...] = r


def _out_proj_final(mixed, x, final_g):
    t = FINAL
    return pl.pallas_call(
        _out_final_kernel,
        out_shape=(jax.ShapeDtypeStruct((N_PROMPT_TOK, D_MODEL), F32),
                   jax.ShapeDtypeStruct((N_SAMPLE_TOK, D_MODEL), F32)),
        grid=(t.n_tiles,),
        in_specs=_residual_specs(t, mixed, (x,)) + [_const_spec((1, D_MODEL))],
        out_specs=(t.prompt_rows(D_MODEL), t.sample_rows(D_MODEL)),
        scratch_shapes=[pltpu.VMEM(mixed.w_out.shape[1:], BF16)],
        compiler_params=_params("arbitrary", vmem=VMEM_LIMIT),
        name="out_proj_final",
    )(*_residual_args(mixed, (x,)), final_g)


def _pool_in_kernel(*refs, n_y, n_x):
    t = PROJ
    n_res = n_y + n_x + 2
    res_refs, (g_ref, sh_ref, sc_ref, wf_ref, xo_ref, u_ref, z_ref, wo_ref, w_ref) = refs[:n_res], refs[n_res:]
    x = _residual_update(t, res_refs, n_y, n_x, wo_ref, xo_ref)
    _cast_weight_once(wf_ref, w_ref)
    h = _norm_mod(x, g_ref, sh_ref, sc_ref)

    def store_u(c, a):
        u_ref[:, c * MXU_N:(c + 1) * MXU_N] = a

    def store_z(c, a):
        z_ref[:, c * MXU_N:(c + 1) * MXU_N] = a.astype(BF16)

    _proj_chunks(h, w_ref, 0, D_MODEL, store_u)
    _proj_chunks(h, w_ref, D_MODEL, D_MODEL, store_z)


def _pool_in(mixed, x_parts, g, mod, w_all, j):
    t = PROJ
    slab = jax.ShapeDtypeStruct((N_TOK, D_MODEL), F32)
    return pl.pallas_call(
        functools.partial(_pool_in_kernel, n_y=len(mixed.y_parts), n_x=len(x_parts)),
        out_shape=(slab, slab, jax.ShapeDtypeStruct((N_TOK, D_MODEL), BF16)),
        grid=(t.n_tiles,),
        in_specs=_residual_specs(t, mixed, x_parts) + [
            _const_spec((1, D_MODEL)), t.mod(0), t.mod(1), _layer_spec(w_all, j)],
        out_specs=(t.rows(D_MODEL),) * 3,
        scratch_shapes=[pltpu.VMEM(mixed.w_out.shape[1:], BF16), pltpu.VMEM((D_MODEL, 2 * D_MODEL), BF16)],
        compiler_params=_params("arbitrary", vmem=FUSED_VMEM_LIMIT),
        name="pool_in",
    )(*_residual_args(mixed, x_parts), g, mod, mod, w_all)


def _split_bf16(a):
    hi = a.astype(BF16)
    return hi, (a - hi.astype(F32)).astype(BF16)


def _band_ones(shape, lo, hi):
    d = lax.broadcasted_iota(jnp.int32, shape, 1) - lax.broadcasted_iota(jnp.int32, shape, 0)
    return jnp.logical_and(d >= lo, d <= hi).astype(F32).astype(BF16)


def _stream_cast_weight(w_hbm, wb_ref, stage_ref, sem):
    width = stage_ref.shape[-1]
    n_chunks = wb_ref.shape[-1] // width

    def copy(c):
        return pltpu.make_async_copy(w_hbm.at[:, pl.ds(c * width, width)], stage_ref.at[c % 2], sem.at[c % 2])

    copy(0).start()
    for c in range(n_chunks):
        if c + 1 < n_chunks:
            copy(c + 1).start()
        copy(c).wait()
        wb_ref[:, c * width:(c + 1) * width] = stage_ref[c % 2].astype(BF16)


def _pool_ret_kernel(u_ref, up_ref, un_ref, z_ref, wgf_ref, ps_ref, wof_ref, x_ref, gate_ref,
                     g_ref, sh_ref, sc_ref, w_hbm,
                     xo_ref, q_ref, kt_ref, v_ref, zr_ref,
                     y_ref, wg_ref, wo_ref, band_ref, wkt_ref, w_ref, stage_ref, sem, *, j_ret):
    t = PROJ
    i = pl.program_id(0)
    _cast_weight_once(wgf_ref, wg_ref)
    _cast_weight_once(wof_ref, wo_ref)

    @pl.when(i == 0)
    def _():
        for g, w in enumerate(POOL_WINDOWS):
            band_ref[g] = _band_ones((POOL_SUB, POOL_SUB), -(w // 2), w - 1 - w // 2)
        _stream_cast_weight(w_hbm.at[j_ret], w_ref, stage_ref, sem)
        for r in range(0, RET_QK_WIDTH, MXU_N):
            wk = w_ref[:, RET_QK_WIDTH + r:RET_QK_WIDTH + r + MXU_N]
            wkt_ref[r:r + MXU_N, :] = wk.astype(F32).T.astype(BF16)

    is_dec = i >= t.n_prompt
    st = t.seq_tile(i)
    seq_len = jnp.where(is_dec, DEC_SEQ, SEQ)
    h = POOL_HALO
    n_sub = t.tm // POOL_SUB
    def pool_steps(sub):
        rows = slice(sub * POOL_SUB, (sub + 1) * POOL_SUB)
        if sub == 0:
            before = jnp.where(jnp.logical_and(is_dec, st != 0), up_ref[...], 0.0)
        else:
            before = jnp.where(is_dec, u_ref[sub * POOL_SUB - h:sub * POOL_SUB, :], 0.0)
        if sub == n_sub - 1:
            after = jnp.where(jnp.logical_and(is_dec, st != t.per_dec_seq - 1), un_ref[...], 0.0)
        else:
            after = jnp.where(is_dec, u_ref[(sub + 1) * POOL_SUB:(sub + 1) * POOL_SUB + h, :], 0.0)
        pos0 = jnp.where(is_dec, st * t.tm + sub * POOL_SUB, 0)
        return _pool_core_steps(u_ref.at[rows], before, after, pos0, seq_len, z_ref.at[rows], ps_ref,
                                wg_ref, band_ref, y_ref.at[rows])

    for step in pool_steps(0):
        step()
    for sub in range(n_sub):
        rows = slice(sub * POOL_SUB, (sub + 1) * POOL_SUB)
        x = x_ref[rows, :] + gate_ref[...] * _dot(y_ref[rows, :], wo_ref[...])
        xo_ref[rows, :] = x
        proj = _ret_in_steps(x, g_ref, sh_ref, sc_ref, w_ref, wkt_ref, q_ref.at[rows],
                             kt_ref.at[pl.ds(sub * POOL_SUB // RET_CHUNK, POOL_SUB // RET_CHUNK)],
                             v_ref.at[rows], zr_ref.at[rows])
        _run_interleaved(proj, pool_steps(sub + 1) if sub + 1 < n_sub else [])


def _pool_core_steps(u_ref, before, after, pos0, seq_len, z_ref, ps_ref, wg_ref, band_ref, y_ref):
    n = u_ref.shape[0]
    h = POOL_HALO
    halo_hi, halo_lo = _split_bf16(jnp.concatenate([before, after], axis=0))
    pos = pos0 + lax.broadcasted_iota(jnp.int32, (n, 1), 0)
    rr = lax.broadcasted_iota(jnp.int32, (2 * h, 2 * h), 0)
    cc = lax.broadcasted_iota(jnp.int32, (2 * h, 2 * h), 1)

    def group(g, w):
        def step():
            left = w // 2
            right = w - 1 - left
            cols = slice(g * POOL_GROUP_DIM, (g + 1) * POOL_GROUP_DIM)
            u = u_ref[:, cols]
            s = _dot(band_ref[g], u.astype(BF16))
            top = jnp.logical_and(jnp.logical_and(rr < h, cc < h), cc - h >= rr - left)
            bot = jnp.logical_and(jnp.logical_and(rr >= h, cc >= h), cc - h <= rr - 2 * h + right)
            edge = jnp.logical_or(top, bot).astype(F32).astype(BF16)
            se = _dot(edge, halo_hi[:, cols]) + _dot(edge, halo_lo[:, cols])
            s = jnp.concatenate([s[:h] + se[:h], s[h:n - h], s[n - h:] + se[h:]], axis=0)
            cnt = (jnp.minimum(pos + right + 1, seq_len) - jnp.maximum(pos - left, 0)).astype(F32)
            d = (s / cnt - u).astype(BF16)
            yg = _dot(d, wg_ref[g]) * ps_ref[:, cols] * _silu(z_ref[:, cols].astype(F32))
            y_ref[:, cols] = yg.astype(BF16)
        return step

    return [group(g, w) for g, w in enumerate(POOL_WINDOWS)]


def _pool_ret(u, z, wg_all, ps, wo_all, j, x, mod_pool, g, mod, w_ret_all, j_ret):
    t = PROJ
    per = t.tm // POOL_HALO
    n_halo = N_TOK // POOL_HALO
    n_in = 2 * RET_QK_WIDTH + 2 * RET_V_WIDTH
    kt_per = t.tm // RET_CHUNK
    return pl.pallas_call(
        functools.partial(_pool_ret_kernel, j_ret=j_ret),
        out_shape=(jax.ShapeDtypeStruct((N_TOK, D_MODEL), F32),
                   jax.ShapeDtypeStruct((N_TOK, RET_QK_WIDTH), BF16),
                   jax.ShapeDtypeStruct((N_TOK // RET_CHUNK, RET_QK_WIDTH, RET_CHUNK), BF16),
                   jax.ShapeDtypeStruct((N_TOK, RET_V_WIDTH), BF16),
                   jax.ShapeDtypeStruct((N_TOK, RET_V_WIDTH), BF16)),
        grid=(t.n_tiles,),
        in_specs=[
            t.rows(D_MODEL),
            pl.BlockSpec((POOL_HALO, D_MODEL), lambda i: (jnp.maximum(i * per - 1, 0), 0)),
            pl.BlockSpec((POOL_HALO, D_MODEL), lambda i: (jnp.minimum((i + 1) * per, n_halo - 1), 0)),
            t.rows(D_MODEL),
            _layer_spec(wg_all, j),
            _const_spec((1, D_MODEL)),
            _layer_spec(wo_all, j),
            t.rows(D_MODEL),
            t.mod(2),
            _const_spec((1, D_MODEL)), t.mod(0), t.mod(1), pl.BlockSpec(memory_space=pl.ANY),
        ],
        out_specs=(t.rows(D_MODEL), t.rows(RET_QK_WIDTH),
                   pl.BlockSpec((kt_per, RET_QK_WIDTH, RET_CHUNK), lambda i: (i, 0, 0)),
                   t.rows(RET_V_WIDTH), t.rows(RET_V_WIDTH)),
        scratch_shapes=[pltpu.VMEM((t.tm, D_MODEL), BF16),
                        pltpu.VMEM((len(POOL_WINDOWS), POOL_GROUP_DIM, POOL_GROUP_DIM), BF16),
                        pltpu.VMEM((D_MODEL, D_MODEL), BF16),
                        pltpu.VMEM((len(POOL_WINDOWS), POOL_SUB, POOL_SUB), BF16),
                        pltpu.VMEM((RET_QK_WIDTH, D_MODEL), BF16),
                        pltpu.VMEM((D_MODEL, n_in), BF16),
                        pltpu.VMEM((2, D_MODEL, WEIGHT_STAGE_COLS), F32),
                        pltpu.SemaphoreType.DMA((2,))],
        compiler_params=_params("arbitrary", vmem=FUSED_VMEM_LIMIT),
        name="pool_ret",
    )(u, u, u, z, wg_all, ps, wo_all, x, mod_pool, g, mod, mod, w_ret_all)


def _ret_in_steps(x, g_ref, sh_ref, sc_ref, w_ref, wkt_ref, q_ref, kt_ref, v_ref, z_ref):
    h = _norm_mod(x, g_ref, sh_ref, sc_ref)

    def chunk(ref, lo, c):
        def step():
            cols = slice(c * MXU_N, (c + 1) * MXU_N)
            ref[:, cols] = _dot(h, w_ref[:, lo + c * MXU_N:lo + (c + 1) * MXU_N]).astype(BF16)
        return step

    def key_chunk(c):
        def step():
            rows = slice(c * MXU_N, (c + 1) * MXU_N)
            kt = (_dot_nt(wkt_ref[rows, :], h) * RET_KEY_DIM ** -0.5).astype(BF16)
            for cc in range(x.shape[0] // RET_CHUNK):
                kt_ref[cc, rows, :] = kt[:, cc * RET_CHUNK:(cc + 1) * RET_CHUNK]
        return step

    steps = [chunk(q_ref, 0, c) for c in range(RET_QK_WIDTH // MXU_N)]
    steps += [key_chunk(c) for c in range(RET_QK_WIDTH // MXU_N)]
    steps += [chunk(v_ref, 2 * RET_QK_WIDTH, c) for c in range(RET_V_WIDTH // MXU_N)]
    steps += [chunk(z_ref, 2 * RET_QK_WIDTH + RET_V_WIDTH, c) for c in range(RET_V_WIDTH // MXU_N)]
    return steps


def _run_interleaved(main, side):
    done = 0
    for k, step in enumerate(main):
        step()
        due = (k + 1) * len(side) // len(main)
        for s in side[done:due]:
            s()
        done = due


def _pos(shape, axis):
    return lax.broadcasted_iota(jnp.int32, shape, axis).astype(F32)


def _ret_tables_kernel(lgf_ref, lgb_ref, decay_ref, row_ref, col_ref, cdec_ref):
    h = pl.program_id(0)
    lg_f = lgf_ref[h]
    lg_b = lgb_ref[h]
    c = RET_CHUNK
    diff = _pos((c, c), 0) - _pos((c, c), 1)
    fwd = jnp.where(diff >= 0, jnp.exp(jnp.maximum(diff, 0.0) * lg_f), 0.0)
    bwd = jnp.where(diff <= 0, jnp.exp(jnp.maximum(-diff, 0.0) * lg_b), 0.0)
    decay_ref[...] = fwd + bwd
    j = _pos((RET_TAB_ROWS, c), 1)
    row_ref[0] = jnp.exp((c - 1.0 - j) * lg_f)
    row_ref[1] = jnp.exp(j * lg_b)
    i = _pos((c, LANES), 0)
    col_ref[0] = jnp.exp((i + 1.0) * lg_f)
    col_ref[1] = jnp.exp((c - i) * lg_b)
    full = jnp.full((RET_TAB_ROWS, RET_VAL_DIM), float(c), F32)
    cdec_ref[0] = jnp.exp(full * lg_f)
    cdec_ref[1] = jnp.exp(full * lg_b)


def _ret_tables(lg_f, lg_b):
    smem = pl.BlockSpec(memory_space=pltpu.SMEM)
    c = RET_CHUNK
    shapes = ((c, c), (2, RET_TAB_ROWS, c), (2, c, LANES), (2, RET_TAB_ROWS, RET_VAL_DIM))
    return pl.pallas_call(
        _ret_tables_kernel,
        out_shape=tuple(jax.ShapeDtypeStruct((RET_HEADS,) + s, F32) for s in shapes),
        grid=(RET_HEADS,),
        in_specs=[smem, smem],
        out_specs=tuple(pl.BlockSpec((None,) + s, lambda h, n=len(s): (h,) + (0,) * n) for s in shapes),
        compiler_params=_params("arbitrary"),
        name="ret_tables",
    )(lg_f, lg_b)


def _group_norm_gate(o, gn, z):
    mu = jnp.mean(o, axis=-1, keepdims=True)
    var = jnp.mean(jnp.square(o - mu), axis=-1, keepdims=True)
    on = (o - mu) * lax.rsqrt(var + EPS)
    return (on * gn * _silu(z.astype(F32))).astype(BF16)


def _ret_ctx_kernel(q_ref, kt_ref, v_ref, z_ref, gn_ref, decay_ref, row_ref, y_ref, sf_ref, sb_ref):
    for s in range(RET_CTX_SEQS):
        rows = slice(s * SEQ, (s + 1) * SEQ)
        for h in range(RET_HEADS):
            kc = slice(h * RET_KEY_DIM, (h + 1) * RET_KEY_DIM)
            vc = slice(h * RET_VAL_DIM, (h + 1) * RET_VAL_DIM)
            kt = kt_ref[s, kc, :]
            v = v_ref[rows, vc]
            att = (_dot(q_ref[rows, kc], kt) * decay_ref[h]).astype(BF16)
            y_ref[rows, vc] = _group_norm_gate(_dot(att, v), gn_ref[:, vc], z_ref[rows, vc])
            ktf = kt.astype(F32)
            sf_ref[s, h] = _dot((ktf * row_ref[h, 0, 0:1, :]).astype(BF16), v)
            sb_ref[s, h] = _dot((ktf * row_ref[h, 1, 0:1, :]).astype(BF16), v)


def _ret_ctx(q, kt, v, z, gn, decay, row):
    c = RET_CHUNK
    n = RET_CTX_SEQS
    wide = lambda w: pl.BlockSpec((n * SEQ, w), lambda b: (b, 0))
    st_spec = pl.BlockSpec((n, RET_HEADS, RET_KEY_DIM, RET_VAL_DIM), lambda b: (b, 0, 0, 0))
    st_shape = jax.ShapeDtypeStruct((BATCH, RET_HEADS, RET_KEY_DIM, RET_VAL_DIM), F32)
    return pl.pallas_call(
        _ret_ctx_kernel,
        out_shape=(jax.ShapeDtypeStruct((N_PROMPT_TOK, RET_V_WIDTH), BF16), st_shape, st_shape),
        grid=(BATCH // n,),
        in_specs=[wide(RET_QK_WIDTH), pl.BlockSpec((n, RET_QK_WIDTH, c), lambda b: (b, 0, 0)),
                  wide(RET_V_WIDTH), wide(RET_V_WIDTH), _const_spec((1, RET_V_WIDTH)),
                  _const_spec((RET_HEADS, c, c)), _const_spec((RET_HEADS, 2, RET_TAB_ROWS, c))],
        out_specs=(wide(RET_V_WIDTH), st_spec, st_spec),
        compiler_params=_params("arbitrary", vmem=VMEM_LIMIT),
        name="ret_ctx",
    )(q, kt, v, z, gn, decay, row)


def _ret_lat_kernel(q_ref, kt_ref, v_ref, z_ref, gn_ref, decay_ref, row_ref, col_ref, cdec_ref,
                    s0f_ref, s0b_ref, y_ref, sf_all, sb_all, sf_acc, sb_acc):
    c = RET_CHUNK
    n_chunks = DEC_SEQ // c
    rows_of = lambda ci: pl.ds(pl.multiple_of(ci * c, c), c)

    sf_acc[...] = s0f_ref[...]
    sb_acc[...] = s0b_ref[...]

    def scan_step(i, carry):
        cf = i
        cb = n_chunks - 1 - i
        sf_all[cf] = sf_acc[...].astype(BF16)
        sb_all[cb] = sb_acc[...].astype(BF16)
        uf = _dot((kt_ref[cf].astype(F32) * row_ref[0, 0:1, :]).astype(BF16), v_ref[rows_of(cf), :])
        ub = _dot((kt_ref[cb].astype(F32) * row_ref[1, 0:1, :]).astype(BF16), v_ref[rows_of(cb), :])
        sf_acc[...] = sf_acc[...] * cdec_ref[0, 0:1, :] + uf
        sb_acc[...] = sb_acc[...] * cdec_ref[1, 0:1, :] + ub
        return carry

    lax.fori_loop(0, n_chunks, scan_step, 0, unroll=4)

    def out_step(ci, carry):
        rows = rows_of(ci)
        q = q_ref[rows, :]
        qf = q.astype(F32)
        qdec_f = jnp.concatenate([col_ref[0]] * (RET_KEY_DIM // LANES), axis=1)
        qdec_b = jnp.concatenate([col_ref[1]] * (RET_KEY_DIM // LANES), axis=1)
        att = (_dot(q, kt_ref[ci]) * decay_ref[...]).astype(BF16)
        o = (_dot(att, v_ref[rows, :])
             + _dot((qf * qdec_f).astype(BF16), sf_all[ci])
             + _dot((qf * qdec_b).astype(BF16), sb_all[ci]))
        y_ref[rows, :] = _group_norm_gate(o, gn_ref[...], z_ref[rows, :])
        return carry

    lax.fori_loop(0, n_chunks, out_step, 0, unroll=8)


def _ret_lat(q, kt, v, z, gn, decay, row, col, cdec, s0f, s0b):
    c = RET_CHUNK
    n_chunks = DEC_SEQ // c
    row0 = N_PROMPT_TOK // DEC_SEQ
    qk_spec = pl.BlockSpec((DEC_SEQ, RET_KEY_DIM), lambda b, h: (row0 + b, h))
    v_spec = pl.BlockSpec((DEC_SEQ, RET_VAL_DIM), lambda b, h: (row0 + b, h))
    st_spec = pl.BlockSpec((None, None, RET_KEY_DIM, RET_VAL_DIM), lambda b, h: (b, h, 0, 0))
    tab = lambda *s: pl.BlockSpec((None,) + s, lambda b, h: (h,) + (0,) * len(s))
    states = pltpu.VMEM((n_chunks, RET_KEY_DIM, RET_VAL_DIM), BF16)
    acc = pltpu.VMEM((RET_KEY_DIM, RET_VAL_DIM), F32)
    return pl.pallas_call(
        _ret_lat_kernel,
        out_shape=jax.ShapeDtypeStruct((N_SAMPLE_TOK, RET_V_WIDTH), BF16),
        grid=(DEC_BATCH, RET_HEADS),
        in_specs=[qk_spec,
                  pl.BlockSpec((n_chunks, RET_KEY_DIM, c), lambda b, h: (row0 + b, h, 0)),
                  v_spec, v_spec, pl.BlockSpec((1, RET_VAL_DIM), lambda b, h: (0, h)),
                  tab(c, c), tab(2, RET_TAB_ROWS, c), tab(2, c, LANES), tab(2, RET_TAB_ROWS, RET_VAL_DIM),
                  st_spec, st_spec],
        out_specs=pl.BlockSpec((DEC_SEQ, RET_VAL_DIM), lambda b, h: (b, h)),
        scratch_shapes=[states, states, acc, acc],
        compiler_params=_params("arbitrary", "arbitrary", vmem=VMEM_LIMIT),
        name="ret_lat",
    )(q, kt, v, z, gn, decay, row, col, cdec, s0f, s0b)


def _rope_tables(tm):
    n_rows = DEC_SEQ // GRID_W
    rows = jnp.repeat(jnp.arange(n_rows), GRID_W).astype(F32)
    cols = jnp.tile(jnp.arange(GRID_W), n_rows).astype(F32)
    half = ATTN_HEAD_DIM // 4
    inv = ROPE_BASE ** (-jnp.arange(half, dtype=F32) / half)
    ang_r = rows[:, None] * inv[None, :]
    ang_c = cols[:, None] * inv[None, :]
    cos = jnp.concatenate([jnp.cos(ang_r), jnp.cos(ang_r), jnp.cos(ang_c), jnp.cos(ang_c)], axis=-1)
    sin = jnp.concatenate([-jnp.sin(ang_r), jnp.sin(ang_r), -jnp.sin(ang_c), jnp.sin(ang_c)], axis=-1)
    cos = jnp.concatenate([jnp.ones((tm, ATTN_HEAD_DIM), F32), cos], axis=0)
    sin = jnp.concatenate([jnp.zeros((tm, ATTN_HEAD_DIM), F32), sin], axis=0)
    return jnp.tile(cos, (1, 2)), jnp.tile(sin, (1, 2))


def kernel(x_prompt, x_sample, cache_k, cache_v, state_fwd, state_bwd, c, c_ctx, norm_g, ada_w, ada_b, attn_w_in, attn_w_out, attn_sink, pool_w_in, pool_w_grp, pool_scale, pool_w_out, ret_w_in, ret_decay_fwd, ret_decay_bwd, ret_gn_g, ret_w_out, final_g):
    x_parts = (x_prompt.reshape(N_PROMPT_TOK, D_MODEL), x_sample.reshape(N_SAMPLE_TOK, D_MODEL))
    cond = jnp.concatenate([c_ctx[None, :], c,
                            jnp.zeros((N_COND - 1 - DEC_BATCH, D_MODEL), F32)], axis=0)
    mods = _ada_table(cond.T, ada_w, ada_b).reshape(DEPTH, N_COND, 1, 3 * D_MODEL)
    cos_t, sin_t = _rope_tables(PROJ.tm)

    to_kernel = lambda a: jnp.transpose(a, (0, 1, 3, 4, 2))
    from_kernel = lambda a: jnp.transpose(
        a.reshape(a.shape[0], a.shape[1], ATTN_KV_HEADS, ATTN_HEAD_DIM, a.shape[3]), (0, 1, 4, 2, 3))
    ctx_k, ctx_v = to_kernel(cache_k), to_kernel(cache_v)

    assert DEPTH % N_MIXERS == 1, "the layer stack must end on an attention layer"
    caches = ()
    new_sf = new_sb = None
    mixed = None
    for i in range(DEPTH):
        kind, j = i % N_MIXERS, i // N_MIXERS
        g = norm_g[i].reshape(1, D_MODEL)
        mod = mods[i]
        if kind == 0:
            outs = _attn_in(mixed, x_parts, g, mod, attn_w_in, j, cos_t, sin_t, tuple(caches))
            if mixed is not None:
                x_parts, outs = (outs[0],), outs[1:]
            q, k4, v4, z, *caches = outs
            y_parts = (_ctx_attn(attn_sink[j], q, k4, v4, z),
                       _lat_attn(attn_sink[j], q, k4, v4, ctx_k, ctx_v, j, z))
            mixed = Mixed(y_parts, attn_w_out, j, mod)
        elif kind == 1:
            x, u, z = _pool_in(mixed, x_parts, g, mod, pool_w_in, j)
            x_parts, mixed, pooled = (x,), None, (u, z, j, mod)
        else:
            u, z, jp, mod_pool = pooled
            lg_f = jax.nn.log_sigmoid(ret_decay_fwd[j].astype(F32))
            lg_b = jax.nn.log_sigmoid(ret_decay_bwd[j].astype(F32))
            gn = ret_gn_g[j].reshape(1, RET_V_WIDTH)
            x, q, kt, v, z = _pool_ret(u, z, pool_w_grp, pool_scale[jp].reshape(1, D_MODEL), pool_w_out, jp,
                                       x_parts[0], mod_pool, g, mod, ret_w_in, j)
            x_parts = (x,)
            decay, row, col, cdec = _ret_tables(lg_f, lg_b)
            y_ctx, new_sf, new_sb = _ret_ctx(q, kt, v, z, gn, decay, row)
            y_parts = (y_ctx, _ret_lat(q, kt, v, z, gn, decay, row, col, cdec,
                                       state_fwd[:, j], state_bwd[:, j]))
            mixed = Mixed(y_parts, ret_w_out, j, mod)
    y_prompt, y_sample = _out_proj_final(mixed, x_parts[0], final_g.reshape(1, D_MODEL))
    new_k, new_v = caches
    return (y_prompt.reshape(BATCH, SEQ, D_MODEL), y_sample.reshape(DEC_BATCH, DEC_SEQ, D_MODEL),
            from_kernel(new_k), from_kernel(new_v), new_sf[:, None], new_sb[:, None])
```

```python
import functools
from typing import NamedTuple

import jax
import jax.numpy as jnp
from jax import lax
from jax.experimental import pallas as pl
from jax.experimental.pallas import tpu as pltpu

F32 = jnp.float32
BF16 = jnp.bfloat16

D_MODEL = 1024
BATCH = 16
SEQ = 256
DEPTH = 4
DEC_BATCH = 2
DEC_SEQ = 2048
PAST_LEN = 512
GRID_W = 64
N_MIXERS = 3
ATTN_HEADS = 16
ATTN_KV_HEADS = 4
ATTN_HEAD_DIM = 64
ATTN_GROUP = 4
ATTN_WIDTH = 1024
ATTN_KV_WIDTH = 256
WINDOW = 128
ROPE_BASE = 10000.0
POOL_WINDOWS = (2, 4, 8, 16)
POOL_GROUP_DIM = 256
RET_HEADS = 4
RET_KEY_DIM = 256
RET_VAL_DIM = 512
RET_QK_WIDTH = 1024
RET_V_WIDTH = 2048
EPS = 1e-6
NEG_INF = -1e30
LOG2_E = 1.4426950408889634

N_PROMPT_TOK = BATCH * SEQ
N_SAMPLE_TOK = DEC_BATCH * DEC_SEQ
N_TOK = N_PROMPT_TOK + N_SAMPLE_TOK
N_COND = 8
LANES = 128
MXU_N = 256
Q_BLOCK = 128
Q_SUB = 4
RET_CHUNK = 256
RET_TAB_ROWS = 8
RET_CTX_SEQS = 2
POOL_HALO = 8
POOL_SUB = SEQ
WEIGHT_STAGE_COLS = 512
VMEM_LIMIT = 48 * 1024 * 1024
FUSED_VMEM_LIMIT = 58 * 1024 * 1024


class Tiling(NamedTuple):
    tm: int

    @property
    def n_tiles(self):
        return N_TOK // self.tm

    @property
    def n_prompt(self):
        return N_PROMPT_TOK // self.tm

    @property
    def per_dec_seq(self):
        return DEC_SEQ // self.tm

    def cond(self, i):
        return jnp.where(i < self.n_prompt, 0, 1 + (i - self.n_prompt) // self.per_dec_seq)

    def seq_tile(self, i):
        return jnp.where(i < self.n_prompt, 0, (i - self.n_prompt) % self.per_dec_seq)

    def rows(self, width):
        return pl.BlockSpec((self.tm, width), lambda i: (i, 0))

    def prompt_rows(self, width):
        return pl.BlockSpec((self.tm, width), lambda i: (jnp.minimum(i, self.n_prompt - 1), 0))

    def sample_rows(self, width):
        return pl.BlockSpec((self.tm, width), lambda i: (jnp.maximum(i - self.n_prompt, 0), 0))

    def mod(self, part):
        return pl.BlockSpec((None, 1, D_MODEL), lambda i: (self.cond(i), 0, part))


PROJ = Tiling(512)
FINAL = Tiling(1024)


def _silu(z):
    return z * (1.0 / (1.0 + jnp.exp(-z)))


def _dot(a, b):
    return jnp.dot(a, b, preferred_element_type=F32)


def _dot_nt(a, b):
    return lax.dot_general(a, b, (((1,), (1,)), ((), ())), preferred_element_type=F32)


def _params(*sem, vmem=None):
    return pltpu.CompilerParams(dimension_semantics=sem, vmem_limit_bytes=vmem)


def _const_spec(shape):
    nd = len(shape)
    return pl.BlockSpec(shape, lambda *_: (0,) * nd, pipeline_mode=pl.Buffered(1))


def _part_specs(t, parts, width):
    if len(parts) == 1:
        return [t.rows(width)]
    return [t.prompt_rows(width), t.sample_rows(width)]


def _read_parts(t, refs):
    if len(refs) == 1:
        return refs[0][...]
    return jnp.where(pl.program_id(0) < t.n_prompt, refs[0][...], refs[1][...])


def _ada_kernel(cond_ref, w_ref, b_ref, o_ref):
    s = _silu(cond_ref[...])
    w = w_ref[...]
    rows = [jnp.sum(s[:, c:c + 1] * w, axis=0, keepdims=True) + b_ref[...] for c in range(1 + DEC_BATCH)]
    rows.append(jnp.zeros((N_COND - len(rows), w.shape[1]), F32))
    o_ref[...] = jnp.concatenate(rows, axis=0)


def _ada_table(cond, ada_w, ada_b):
    tn = 3 * D_MODEL // 2
    return pl.pallas_call(
        _ada_kernel,
        out_shape=jax.ShapeDtypeStruct((DEPTH, N_COND, 3 * D_MODEL), F32),
        grid=(DEPTH, 3 * D_MODEL // tn),
        in_specs=[
            pl.BlockSpec((D_MODEL, N_COND), lambda l, n: (0, 0)),
            pl.BlockSpec((None, D_MODEL, tn), lambda l, n: (l, 0, n)),
            pl.BlockSpec((None, 1, tn), lambda l, n: (l, 0, n)),
        ],
        out_specs=pl.BlockSpec((None, N_COND, tn), lambda l, n: (l, 0, n)),
        compiler_params=_params("arbitrary", "arbitrary", vmem=VMEM_LIMIT),
        name="ada_table",
    )(cond, ada_w, ada_b.reshape(DEPTH, 1, 3 * D_MODEL))


def _rms(x, g):
    return x * lax.rsqrt(jnp.mean(x * x, axis=-1, keepdims=True) + EPS) * g


def _norm_mod(x, g_ref, sh_ref, sc_ref):
    return (_rms(x, g_ref[...]) * (1.0 + sc_ref[...]) + sh_ref[...]).astype(BF16)


def _proj_chunks(h, w_ref, lo, width, store):
    for c in range(width // MXU_N):
        store(c, _dot(h, w_ref[:, lo + c * MXU_N:lo + (c + 1) * MXU_N]))


def _layer_spec(w, j):
    nd = w.ndim - 1
    return pl.BlockSpec((None,) + w.shape[1:], lambda *_: (j,) + (0,) * nd, pipeline_mode=pl.Buffered(1))


def _cast_weight_once(w_ref, wb_ref):
    @pl.when(pl.program_id(0) == 0)
    def _():
        rows = wb_ref.shape[-2]
        for r in range(0, rows, MXU_N):
            wb_ref[..., r:r + MXU_N, :] = w_ref[..., r:r + MXU_N, :].astype(BF16)


class Mixed(NamedTuple):
    y_parts: tuple
    w_out: jax.Array
    j: int
    mod: jax.Array


def _residual_specs(t, mixed, x_parts):
    k = mixed.w_out.shape[1]
    return (_part_specs(t, mixed.y_parts, k) + [_layer_spec(mixed.w_out, mixed.j)]
            + _part_specs(t, x_parts, D_MODEL) + [t.mod(2)])


def _residual_args(mixed, x_parts):
    return (*mixed.y_parts, mixed.w_out, *x_parts, mixed.mod)


def _residual_update(t, refs, n_y, n_x, wo_ref, xo_ref):
    y_refs, wof_ref, x_refs, gate_ref = refs[:n_y], refs[n_y], refs[n_y + 1:n_y + 1 + n_x], refs[n_y + 1 + n_x]
    _cast_weight_once(wof_ref, wo_ref)
    x = _read_parts(t, x_refs) + gate_ref[...] * _dot(_read_parts(t, y_refs), wo_ref[...])
    xo_ref[...] = x
    return x


def _rep4(a, h):
    half = a[:, (h // 2) * LANES:(h // 2 + 1) * LANES]
    lane = lax.broadcasted_iota(jnp.int32, half.shape, 1)
    keep = (lane < ATTN_HEAD_DIM) if h % 2 == 0 else (lane >= ATTN_HEAD_DIM)
    m = jnp.where(keep, half, 0.0)
    s = m + pltpu.roll(m, ATTN_HEAD_DIM, 1)
    return jnp.concatenate([s, s], axis=1)


def _attn_in_kernel(*refs, j, n_y, n_x, n_alias):
    t = PROJ
    n_res = n_y + n_x + 2 if n_y else n_x
    res_refs, refs = refs[:n_res], refs[n_res:]
    (g_ref, sh_ref, sc_ref, wf_ref, cos_ref, sin_ref), refs = refs[:6], refs[6 + n_alias:]
    if n_y:
        xo_ref, q_ref, k4_ref, v4_ref, z_ref, kc_ref, vc_ref, wo_ref, w_ref = refs
        x = _residual_update(t, res_refs, n_y, n_x, wo_ref, xo_ref)
    else:
        q_ref, k4_ref, v4_ref, z_ref, kc_ref, vc_ref, w_ref = refs
        x = _read_parts(t, res_refs)
    i = pl.program_id(0)
    _cast_weight_once(wf_ref, w_ref)
    h = _norm_mod(x, g_ref, sh_ref, sc_ref)
    cos = cos_ref[...]
    sin = sin_ref[...]
    lane = lax.broadcasted_iota(jnp.int32, (t.tm, LANES), 1)
    first = (lane % (ATTN_HEAD_DIM // 2)) < ATTN_HEAD_DIM // 4

    def rope(a):
        rot = jnp.where(first, pltpu.roll(a, LANES - ATTN_HEAD_DIM // 4, 1),
                        pltpu.roll(a, ATTN_HEAD_DIM // 4, 1))
        return a * cos + rot * sin

    def rope_wide(a):
        return jnp.concatenate(
            [rope(a[:, s * LANES:(s + 1) * LANES]) for s in range(MXU_N // LANES)], axis=1)

    scale = ATTN_HEAD_DIM ** -0.5 * LOG2_E

    def store_q(c, a):
        q_ref[:, c * MXU_N:(c + 1) * MXU_N] = (rope_wide(a) * scale).astype(BF16)

    def store_z(c, a):
        z_ref[:, c * MXU_N:(c + 1) * MXU_N] = _silu(a).astype(BF16)

    _proj_chunks(h, w_ref, 0, ATTN_WIDTH, store_q)
    k = rope_wide(_dot(h, w_ref[:, ATTN_WIDTH:ATTN_WIDTH + ATTN_KV_WIDTH]))
    v = _dot(h, w_ref[:, ATTN_WIDTH + ATTN_KV_WIDTH:ATTN_WIDTH + 2 * ATTN_KV_WIDTH])
    for hh in range(ATTN_KV_HEADS):
        k4_ref[:, hh * MXU_N:(hh + 1) * MXU_N] = _rep4(k, hh).astype(BF16)
        v4_ref[:, hh * MXU_N:(hh + 1) * MXU_N] = _rep4(v, hh).astype(BF16)
    _proj_chunks(h, w_ref, ATTN_WIDTH + 2 * ATTN_KV_WIDTH, ATTN_WIDTH, store_z)

    @pl.when(i < t.n_prompt)
    def _():
        for s in range(t.tm // SEQ):
            kt = k[s * SEQ:(s + 1) * SEQ, :].T
            vt = v[s * SEQ:(s + 1) * SEQ, :].T
            if n_alias:
                kc_ref[s] = kt
                vc_ref[s] = vt
            else:
                for l in range(kc_ref.shape[1]):
                    kc_ref[s, l] = kt if l == j else jnp.zeros_like(kt)
                    vc_ref[s, l] = vt if l == j else jnp.zeros_like(vt)


def _attn_in(mixed, x_parts, g, mod, w_all, j, cos_t, sin_t, caches):
    t = PROJ
    n_in = 2 * ATTN_WIDTH + 2 * ATTN_KV_WIDTH
    per = t.tm // SEQ
    rope_spec = pl.BlockSpec(
        (t.tm, LANES), lambda i: (jnp.where(i < t.n_prompt, 0, 1 + t.seq_tile(i)), 0))
    wide = jax.ShapeDtypeStruct((N_TOK, ATTN_WIDTH), BF16)
    n_attn = w_all.shape[0]
    cache = jax.ShapeDtypeStruct((BATCH, n_attn, ATTN_KV_WIDTH, SEQ), F32)
    if caches:
        cache_spec = pl.BlockSpec((per, None, ATTN_KV_WIDTH, SEQ),
                                  lambda i: (jnp.minimum(i, t.n_prompt - 1), j, 0, 0))
    else:
        cache_spec = pl.BlockSpec((per, n_attn, ATTN_KV_WIDTH, SEQ),
                                  lambda i: (jnp.minimum(i, t.n_prompt - 1), 0, 0, 0))
    n_x = len(x_parts)
    if mixed is None:
        n_y, res_specs, res_args, res_out, res_ospecs, res_scratch = 0, _part_specs(t, x_parts, D_MODEL), x_parts, (), (), []
    else:
        n_y = len(mixed.y_parts)
        res_specs, res_args = _residual_specs(t, mixed, x_parts), _residual_args(mixed, x_parts)
        res_out = (jax.ShapeDtypeStruct((N_TOK, D_MODEL), F32),)
        res_ospecs = (t.rows(D_MODEL),)
        res_scratch = [pltpu.VMEM(mixed.w_out.shape[1:], BF16)]
    n_front = len(res_specs) + 6
    return pl.pallas_call(
        functools.partial(_attn_in_kernel, j=j, n_y=n_y, n_x=n_x, n_alias=len(caches)),
        out_shape=res_out + (wide, wide, wide, wide, cache, cache),
        grid=(t.n_tiles,),
        in_specs=res_specs + [
            _const_spec((1, D_MODEL)), t.mod(0), t.mod(1), _layer_spec(w_all, j),
            rope_spec, rope_spec] + [pl.BlockSpec(memory_space=pl.ANY)] * len(caches),
        out_specs=res_ospecs + (t.rows(ATTN_WIDTH),) * 4 + (cache_spec,) * 2,
        scratch_shapes=res_scratch + [pltpu.VMEM((D_MODEL, n_in), BF16)],
        input_output_aliases={n_front + c: len(res_out) + 4 + c for c in range(len(caches))},
        compiler_params=_params("arbitrary", vmem=FUSED_VMEM_LIMIT),
        name="attn_in",
    )(*res_args, g, mod, mod, w_all, cos_t, sin_t, *caches)


def _stack_group_queries(q):
    qf = q.astype(F32)
    chunk = lax.broadcasted_iota(jnp.int32, qf.shape, 1) // ATTN_HEAD_DIM
    return jnp.concatenate(
        [jnp.where(chunk == g, qf, 0.0) for g in range(ATTN_GROUP)], axis=0).astype(BF16)


def _gather_group_outputs(o, rows):
    chunk = lax.broadcasted_iota(jnp.int32, (rows, MXU_N), 1) // ATTN_HEAD_DIM
    acc = jnp.zeros((rows, MXU_N), F32)
    for g in range(ATTN_GROUP):
        acc = acc + jnp.where(chunk == g, o[g * rows:(g + 1) * rows], 0.0)
    return acc


def _sink_column(sink_ref, h, rows):
    grp = lax.broadcasted_iota(jnp.int32, (ATTN_GROUP * rows, 1), 0) // rows
    col = jnp.zeros((ATTN_GROUP * rows, 1), F32)
    for g in range(ATTN_GROUP):
        col = jnp.where(grp == g, sink_ref[h * ATTN_GROUP + g] * LOG2_E, col)
    return col


def _chunk_rows(dtype):
    chunk = lax.broadcasted_iota(jnp.int32, (1, MXU_N), 1) // ATTN_HEAD_DIM
    return [(chunk == g).astype(F32).astype(dtype) for g in range(ATTN_GROUP)]


def _block_diag_rows(x4):
    return jnp.concatenate([x4 * m for m in _chunk_rows(x4.dtype)], axis=0)


def _ctx_attn_kernel(sink_ref, q_ref, k4_ref, v4_ref, z_ref, y_ref):
    chunk = lax.broadcasted_iota(jnp.int32, (SEQ, MXU_N), 1) // ATTN_HEAD_DIM
    for h in range(ATTN_KV_HEADS):
        cols = slice(h * MXU_N, (h + 1) * MXU_N)
        s = _dot_nt(q_ref[:, cols], _block_diag_rows(k4_ref[:, cols]))
        inv = jnp.zeros((SEQ, MXU_N), F32)
        probs = []
        for g in range(ATTN_GROUP):
            sg = s[:, g * SEQ:(g + 1) * SEQ]
            sk = sink_ref[h * ATTN_GROUP + g] * LOG2_E
            m = jnp.maximum(jnp.max(sg, axis=1, keepdims=True), sk)
            e = jnp.exp2(sg - m)
            den = jnp.sum(e, axis=1, keepdims=True) + jnp.exp2(sk - m)
            probs.append(e.astype(BF16))
            inv = jnp.where(chunk == g, 1.0 / den, inv)
        o = _dot(jnp.concatenate(probs, axis=1), _block_diag_rows(v4_ref[:, cols]))
        y_ref[:, cols] = (o * inv * z_ref[:, cols].astype(F32)).astype(BF16)


def _ctx_attn(sink, q, k4, v4, z):
    spec = pl.BlockSpec((SEQ, ATTN_WIDTH), lambda b: (b, 0))
    return pl.pallas_call(
        _ctx_attn_kernel,
        out_shape=jax.ShapeDtypeStruct((N_PROMPT_TOK, ATTN_WIDTH), BF16),
        grid=(BATCH,),
        in_specs=[pl.BlockSpec(memory_space=pltpu.SMEM), spec, spec, spec, spec],
        out_specs=spec,
        compiler_params=_params("arbitrary"),
        name="ctx_attn",
    )(sink, q, k4, v4, z)


LAT_STEP = Q_SUB * Q_BLOCK
LAT_PER_SEQ = DEC_SEQ // LAT_STEP
LAT_BLOCKS = DEC_BATCH * ATTN_KV_HEADS * LAT_PER_SEQ


def _lat_block(blk):
    return (blk // (ATTN_KV_HEADS * LAT_PER_SEQ), (blk // LAT_PER_SEQ) % ATTN_KV_HEADS, blk % LAT_PER_SEQ)


def _band_rows(qb):
    return pl.ds(pl.multiple_of(qb * Q_BLOCK, Q_BLOCK), 3 * Q_BLOCK)


def _pad_sequence(dst, src):
    zeros = jnp.zeros((Q_BLOCK, MXU_N), BF16)
    dst[0:Q_BLOCK, :] = zeros
    dst[Q_BLOCK:Q_BLOCK + DEC_SEQ, :] = src[...]
    dst[Q_BLOCK + DEC_SEQ:, :] = zeros


def _lat_attn_kernel(sink_ref, q_ref, k4_ref, v4_ref, kc_ref, vc_ref, z_ref, y_ref,
                     kp, vp, k4c, v4c, sc_a, sl_a, sc_b, sl_b, e_scr):
    t = pl.program_id(0)
    _, _, n1 = _lat_block(jnp.minimum(t, LAT_BLOCKS - 1))
    _, h0, n0 = _lat_block(jnp.maximum(t - 1, 0))
    n_blocks = DEC_SEQ // Q_BLOCK

    @pl.when(t == 0)
    def _():
        sc_b[...] = jnp.zeros(sc_b.shape, F32)
        sl_b[...] = jnp.zeros(sl_b.shape, F32)

    @pl.when(n1 == 0)
    def _():
        _pad_sequence(kp, k4_ref)
        k4c[...] = jnp.concatenate([kc_ref[...]] * ATTN_GROUP, axis=0).astype(BF16)

    @pl.when(n0 == 0)
    def _():
        _pad_sequence(vp, v4_ref)
        v4c[...] = jnp.concatenate([vc_ref[...]] * ATTN_GROUP, axis=0).T.astype(BF16)

    @pl.when(t % 2 == 0)
    def _():
        _lat_stages(sink_ref, q_ref, z_ref, y_ref, kp, vp, k4c, v4c, e_scr, n1, h0, n0,
                    sc_a, sl_a, sc_b, sl_b)

    @pl.when(t % 2 == 1)
    def _():
        _lat_stages(sink_ref, q_ref, z_ref, y_ref, kp, vp, k4c, v4c, e_scr, n1, h0, n0,
                    sc_b, sl_b, sc_a, sl_a)


def _lat_stages(sink_ref, q_ref, z_ref, y_ref, kp, vp, k4c, v4c, e_scr, n1, h0, n0,
                sc_w, sl_w, sc_r, sl_r):
    n_blocks = DEC_SEQ // Q_BLOCK
    for sub in range(Q_SUB):
        qs = _stack_group_queries(q_ref[sub * Q_BLOCK:(sub + 1) * Q_BLOCK, :])
        sc_w[sub] = _dot(qs, k4c[...])
        sl_w[sub] = _dot_nt(qs, kp[_band_rows(n1 * Q_SUB + sub), :])

    rows4 = ATTN_GROUP * Q_BLOCK
    r = lax.broadcasted_iota(jnp.int32, (rows4, Q_BLOCK), 0) % Q_BLOCK
    c = lax.broadcasted_iota(jnp.int32, (rows4, Q_BLOCK), 1)
    in_left = c >= r
    in_right = c <= r
    sk = _sink_column(sink_ref, h0, Q_BLOCK)
    for sub in range(Q_SUB):
        qb = n0 * Q_SUB + sub
        band = _band_rows(qb)
        qrows = slice(sub * Q_BLOCK, (sub + 1) * Q_BLOCK)
        s_ctx = sc_r[sub]
        s_lat = sl_r[sub]
        blocks = [
            s_ctx,
            jnp.where(jnp.logical_and(in_left, qb > 0), s_lat[:, :Q_BLOCK], NEG_INF),
            s_lat[:, Q_BLOCK:2 * Q_BLOCK],
            jnp.where(jnp.logical_and(in_right, qb < n_blocks - 1), s_lat[:, 2 * Q_BLOCK:], NEG_INF),
        ]
        m = sk
        for s in blocks:
            m = jnp.maximum(m, jnp.max(s, axis=1, keepdims=True))
        den = jnp.exp2(sk - m)
        lo = 0
        for s in blocks:
            e = jnp.exp2(s - m)
            den = den + jnp.sum(e, axis=1, keepdims=True)
            e_scr[sub, :, lo:lo + s.shape[1]] = e.astype(BF16)
            lo += s.shape[1]
        o = (_dot(e_scr[sub, :, :PAST_LEN], v4c[...])
             + _dot(e_scr[sub, :, PAST_LEN:], vp[band, :])) * (1.0 / den)
        acc = _gather_group_outputs(o, Q_BLOCK)
        y_ref[qrows, :] = (acc * z_ref[qrows, :].astype(F32)).astype(BF16)


def _lat_attn(sink, q, k4, v4, kc, vc, j, z):
    padded = DEC_SEQ + 2 * Q_BLOCK
    rows4 = ATTN_GROUP * Q_BLOCK
    row0 = N_PROMPT_TOK // LAT_STEP
    seq0 = N_PROMPT_TOK // DEC_SEQ
    stage1 = lambda t: _lat_block(jnp.minimum(t, LAT_BLOCKS - 1))
    stage2 = lambda t: _lat_block(jnp.maximum(t - 1, 0))

    def spec(shape, stage, index):
        return pl.BlockSpec(shape, lambda t: index(*stage(t)))

    tile = (LAT_STEP, MXU_N)
    seq = (DEC_SEQ, MXU_N)
    ctx = (None, None, None, ATTN_HEAD_DIM, PAST_LEN)
    return pl.pallas_call(
        _lat_attn_kernel,
        out_shape=jax.ShapeDtypeStruct((N_SAMPLE_TOK, ATTN_WIDTH), BF16),
        grid=(LAT_BLOCKS + 1,),
        in_specs=[pl.BlockSpec(memory_space=pltpu.SMEM),
                  spec(tile, stage1, lambda b, h, n: (row0 + b * LAT_PER_SEQ + n, h)),
                  spec(seq, stage1, lambda b, h, n: (seq0 + b, h)),
                  spec(seq, stage2, lambda b, h, n: (seq0 + b, h)),
                  spec(ctx, stage1, lambda b, h, n: (b, j, h, 0, 0)),
                  spec(ctx, stage2, lambda b, h, n: (b, j, h, 0, 0)),
                  spec(tile, stage2, lambda b, h, n: (row0 + b * LAT_PER_SEQ + n, h))],
        out_specs=spec(tile, stage2, lambda b, h, n: (b * LAT_PER_SEQ + n, h)),
        scratch_shapes=[pltpu.VMEM((padded, MXU_N), BF16), pltpu.VMEM((padded, MXU_N), BF16),
                        pltpu.VMEM((MXU_N, PAST_LEN), BF16), pltpu.VMEM((PAST_LEN, MXU_N), BF16),
                        pltpu.VMEM((Q_SUB, rows4, PAST_LEN), F32),
                        pltpu.VMEM((Q_SUB, rows4, 3 * Q_BLOCK), F32),
                        pltpu.VMEM((Q_SUB, rows4, PAST_LEN), F32),
                        pltpu.VMEM((Q_SUB, rows4, 3 * Q_BLOCK), F32),
                        pltpu.VMEM((Q_SUB, rows4, PAST_LEN + 3 * Q_BLOCK), BF16)],
        compiler_params=_params("arbitrary", vmem=VMEM_LIMIT),
        name="lat_attn",
    )(sink, q, k4, v4, kc, vc, z)


def _out_final_kernel(yp_ref, ys_ref, wf_ref, x_ref, gate_ref, fg_ref, op_ref, os_ref, w_ref):
    t = FINAL
    i = pl.program_id(0)
    _cast_weight_once(wf_ref, w_ref)
    y = _read_parts(t, (yp_ref, ys_ref))
    r = _rms(x_ref[...] + gate_ref[...] * _dot(y, w_ref[...]), fg_ref[...])

    @pl.when(i < t.n_prompt)
    def _():
        op_ref[...] = r

    @pl.when(i >= t.n_prompt)
    def _():
        os_ref[...] = r


def _out_proj_final(mixed, x, final_g):
    t = FINAL
    return pl.pallas_call(
        _out_final_kernel,
        out_shape=(jax.ShapeDtypeStruct((N_PROMPT_TOK, D_MODEL), F32),
                   jax.ShapeDtypeStruct((N_SAMPLE_TOK, D_MODEL), F32)),
        grid=(t.n_tiles,),
        in_specs=_residual_specs(t, mixed, (x,)) + [_const_spec((1, D_MODEL))],
        out_specs=(t.prompt_rows(D_MODEL), t.sample_rows(D_MODEL)),
        scratch_shapes=[pltpu.VMEM(mixed.w_out.shape[1:], BF16)],
        compiler_params=_params("arbitrary", vmem=VMEM_LIMIT),
        name="out_proj_final",
    )(*_residual_args(mixed, (x,)), final_g)


def _pool_in_kernel(*refs, n_y, n_x):
    t = PROJ
    n_res = n_y + n_x + 2
    res_refs, (g_ref, sh_ref, sc_ref, wf_ref, xo_ref, u_ref, z_ref, wo_ref, w_ref) = refs[:n_res], refs[n_res:]
    x = _residual_update(t, res_refs, n_y, n_x, wo_ref, xo_ref)
    _cast_weight_once(wf_ref, w_ref)
    h = _norm_mod(x, g_ref, sh_ref, sc_ref)

    def store_u(c, a):
        u_ref[:, c * MXU_N:(c + 1) * MXU_N] = a

    def store_z(c, a):
        z_ref[:, c * MXU_N:(c + 1) * MXU_N] = _silu(a).astype(BF16)

    _proj_chunks(h, w_ref, 0, D_MODEL, store_u)
    _proj_chunks(h, w_ref, D_MODEL, D_MODEL, store_z)


def _pool_in(mixed, x_parts, g, mod, w_all, j):
    t = PROJ
    slab = jax.ShapeDtypeStruct((N_TOK, D_MODEL), F32)
    return pl.pallas_call(
        functools.partial(_pool_in_kernel, n_y=len(mixed.y_parts), n_x=len(x_parts)),
        out_shape=(slab, slab, jax.ShapeDtypeStruct((N_TOK, D_MODEL), BF16)),
        grid=(t.n_tiles,),
        in_specs=_residual_specs(t, mixed, x_parts) + [
            _const_spec((1, D_MODEL)), t.mod(0), t.mod(1), _layer_spec(w_all, j)],
        out_specs=(t.rows(D_MODEL),) * 3,
        scratch_shapes=[pltpu.VMEM(mixed.w_out.shape[1:], BF16), pltpu.VMEM((D_MODEL, 2 * D_MODEL), BF16)],
        compiler_params=_params("arbitrary", vmem=FUSED_VMEM_LIMIT),
        name="pool_in",
    )(*_residual_args(mixed, x_parts), g, mod, mod, w_all)


def _split_bf16(a):
    hi = a.astype(BF16)
    return hi, (a - hi.astype(F32)).astype(BF16)


def _band_ones(shape, lo, hi):
    d = lax.broadcasted_iota(jnp.int32, shape, 1) - lax.broadcasted_iota(jnp.int32, shape, 0)
    return jnp.logical_and(d >= lo, d <= hi).astype(F32).astype(BF16)


def _stream_cast_weight(w_hbm, wb_ref, stage_ref, sem):
    width = stage_ref.shape[-1]
    n_chunks = wb_ref.shape[-1] // width

    def copy(c):
        return pltpu.make_async_copy(w_hbm.at[:, pl.ds(c * width, width)], stage_ref.at[c % 2], sem.at[c % 2])

    copy(0).start()
    for c in range(n_chunks):
        if c + 1 < n_chunks:
            copy(c + 1).start()
        copy(c).wait()
        wb_ref[:, c * width:(c + 1) * width] = stage_ref[c % 2].astype(BF16)


def _pool_ret_kernel(u_ref, up_ref, un_ref, z_ref, wgf_ref, ps_ref, wof_ref, x_ref, gate_ref,
                     g_ref, sh_ref, sc_ref, w_hbm,
                     xo_ref, q_ref, kt_ref, v_ref, zr_ref,
                     y_ref, wg_ref, wo_ref, band_ref, wkt_ref, w_ref, stage_ref, sem, *, j_ret):
    t = PROJ
    i = pl.program_id(0)
    _cast_weight_once(wgf_ref, wg_ref)
    _cast_weight_once(wof_ref, wo_ref)

    @pl.when(i == 0)
    def _():
        for g, w in enumerate(POOL_WINDOWS):
            band_ref[g] = _band_ones((POOL_SUB, POOL_SUB), -(w // 2), w - 1 - w // 2)
        _stream_cast_weight(w_hbm.at[j_ret], w_ref, stage_ref, sem)
        for r in range(0, RET_QK_WIDTH, MXU_N):
            wk = w_ref[:, RET_QK_WIDTH + r:RET_QK_WIDTH + r + MXU_N]
            wkt_ref[r:r + MXU_N, :] = wk.astype(F32).T.astype(BF16)

    is_dec = i >= t.n_prompt
    st = t.seq_tile(i)
    seq_len = jnp.where(is_dec, DEC_SEQ, SEQ)
    h = POOL_HALO
    n_sub = t.tm // POOL_SUB
    def pool_steps(sub):
        rows = slice(sub * POOL_SUB, (sub + 1) * POOL_SUB)
        if sub == 0:
            before = jnp.where(jnp.logical_and(is_dec, st != 0), up_ref[...], 0.0)
        else:
            before = jnp.where(is_dec, u_ref[sub * POOL_SUB - h:sub * POOL_SUB, :], 0.0)
        if sub == n_sub - 1:
            after = jnp.where(jnp.logical_and(is_dec, st != t.per_dec_seq - 1), un_ref[...], 0.0)
        else:
            after = jnp.where(is_dec, u_ref[(sub + 1) * POOL_SUB:(sub + 1) * POOL_SUB + h, :], 0.0)
        pos0 = jnp.where(is_dec, st * t.tm + sub * POOL_SUB, 0)
        return _pool_core_steps(u_ref.at[rows], before, after, pos0, seq_len, z_ref.at[rows], ps_ref,
                                wg_ref, band_ref, y_ref.at[rows])

    for step in pool_steps(0):
        step()
    for sub in range(n_sub):
        rows = slice(sub * POOL_SUB, (sub + 1) * POOL_SUB)
        x = x_ref[rows, :] + gate_ref[...] * _dot(y_ref[rows, :], wo_ref[...])
        xo_ref[rows, :] = x
        proj = _ret_in_steps(x, g_ref, sh_ref, sc_ref, w_ref, wkt_ref, q_ref.at[rows],
                             kt_ref.at[pl.ds(sub * POOL_SUB // RET_CHUNK, POOL_SUB // RET_CHUNK)],
                             v_ref.at[rows], zr_ref.at[rows])
        _run_interleaved(proj, pool_steps(sub + 1) if sub + 1 < n_sub else [])


def _pool_core_steps(u_ref, before, after, pos0, seq_len, z_ref, ps_ref, wg_ref, band_ref, y_ref):
    n = u_ref.shape[0]
    h = POOL_HALO
    halo_hi, halo_lo = _split_bf16(jnp.concatenate([before, after], axis=0))
    pos = pos0 + lax.broadcasted_iota(jnp.int32, (n, 1), 0)
    rr = lax.broadcasted_iota(jnp.int32, (2 * h, 2 * h), 0)
    cc = lax.broadcasted_iota(jnp.int32, (2 * h, 2 * h), 1)

    def group(g, w):
        def step():
            left = w // 2
            right = w - 1 - left
            cols = slice(g * POOL_GROUP_DIM, (g + 1) * POOL_GROUP_DIM)
            u = u_ref[:, cols]
            s = _dot(band_ref[g], u.astype(BF16))
            top = jnp.logical_and(jnp.logical_and(rr < h, cc < h), cc - h >= rr - left)
            bot = jnp.logical_and(jnp.logical_and(rr >= h, cc >= h), cc - h <= rr - 2 * h + right)
            edge = jnp.logical_or(top, bot).astype(F32).astype(BF16)
            se = _dot(edge, halo_hi[:, cols]) + _dot(edge, halo_lo[:, cols])
            s = jnp.concatenate([s[:h] + se[:h], s[h:n - h], s[n - h:] + se[h:]], axis=0)
            cnt = (jnp.minimum(pos + right + 1, seq_len) - jnp.maximum(pos - left, 0)).astype(F32)
            d = (s / cnt - u).astype(BF16)
            yg = _dot(d, wg_ref[g]) * ps_ref[:, cols] * z_ref[:, cols].astype(F32)
            y_ref[:, cols] = yg.astype(BF16)
        return step

    return [group(g, w) for g, w in enumerate(POOL_WINDOWS)]


def _pool_ret(u, z, wg_all, ps, wo_all, j, x, mod_pool, g, mod, w_ret_all, j_ret):
    t = PROJ
    per = t.tm // POOL_HALO
    n_halo = N_TOK // POOL_HALO
    n_in = 2 * RET_QK_WIDTH + 2 * RET_V_WIDTH
    kt_per = t.tm // RET_CHUNK
    return pl.pallas_call(
        functools.partial(_pool_ret_kernel, j_ret=j_ret),
        out_shape=(jax.ShapeDtypeStruct((N_TOK, D_MODEL), F32),
                   jax.ShapeDtypeStruct((N_TOK, RET_QK_WIDTH), BF16),
                   jax.ShapeDtypeStruct((N_TOK // RET_CHUNK, RET_QK_WIDTH, RET_CHUNK), BF16),
                   jax.ShapeDtypeStruct((N_TOK, RET_V_WIDTH), BF16),
                   jax.ShapeDtypeStruct((N_TOK, RET_V_WIDTH), BF16)),
        grid=(t.n_tiles,),
        in_specs=[
            t.rows(D_MODEL),
            pl.BlockSpec((POOL_HALO, D_MODEL), lambda i: (jnp.maximum(i * per - 1, 0), 0)),
            pl.BlockSpec((POOL_HALO, D_MODEL), lambda i: (jnp.minimum((i + 1) * per, n_halo - 1), 0)),
            t.rows(D_MODEL),
            _layer_spec(wg_all, j),
            _const_spec((1, D_MODEL)),
            _layer_spec(wo_all, j),
            t.rows(D_MODEL),
            t.mod(2),
            _const_spec((1, D_MODEL)), t.mod(0), t.mod(1), pl.BlockSpec(memory_space=pl.ANY),
        ],
        out_specs=(t.rows(D_MODEL), t.rows(RET_QK_WIDTH),
                   pl.BlockSpec((kt_per, RET_QK_WIDTH, RET_CHUNK), lambda i: (i, 0, 0)),
                   t.rows(RET_V_WIDTH), t.rows(RET_V_WIDTH)),
        scratch_shapes=[pltpu.VMEM((t.tm, D_MODEL), BF16),
                        pltpu.VMEM((len(POOL_WINDOWS), POOL_GROUP_DIM, POOL_GROUP_DIM), BF16),
                        pltpu.VMEM((D_MODEL, D_MODEL), BF16),
                        pltpu.VMEM((len(POOL_WINDOWS), POOL_SUB, POOL_SUB), BF16),
                        pltpu.VMEM((RET_QK_WIDTH, D_MODEL), BF16),
                        pltpu.VMEM((D_MODEL, n_in), BF16),
                        pltpu.VMEM((2, D_MODEL, WEIGHT_STAGE_COLS), F32),
                        pltpu.SemaphoreType.DMA((2,))],
        compiler_params=_params("arbitrary", vmem=FUSED_VMEM_LIMIT),
        name="pool_ret",
    )(u, u, u, z, wg_all, ps, wo_all, x, mod_pool, g, mod, mod, w_ret_all)


def _ret_in_steps(x, g_ref, sh_ref, sc_ref, w_ref, wkt_ref, q_ref, kt_ref, v_ref, z_ref):
    h = _norm_mod(x, g_ref, sh_ref, sc_ref)

    def chunk(ref, lo, c, post=lambda a: a):
        def step():
            cols = slice(c * MXU_N, (c + 1) * MXU_N)
            ref[:, cols] = post(_dot(h, w_ref[:, lo + c * MXU_N:lo + (c + 1) * MXU_N])).astype(BF16)
        return step

    def key_chunk(c):
        def step():
            rows = slice(c * MXU_N, (c + 1) * MXU_N)
            kt = (_dot_nt(wkt_ref[rows, :], h) * RET_KEY_DIM ** -0.5).astype(BF16)
            for cc in range(x.shape[0] // RET_CHUNK):
                kt_ref[cc, rows, :] = kt[:, cc * RET_CHUNK:(cc + 1) * RET_CHUNK]
        return step

    steps = [chunk(q_ref, 0, c) for c in range(RET_QK_WIDTH // MXU_N)]
    steps += [key_chunk(c) for c in range(RET_QK_WIDTH // MXU_N)]
    steps += [chunk(v_ref, 2 * RET_QK_WIDTH, c) for c in range(RET_V_WIDTH // MXU_N)]
    steps += [chunk(z_ref, 2 * RET_QK_WIDTH + RET_V_WIDTH, c, _silu) for c in range(RET_V_WIDTH // MXU_N)]
    return steps


def _run_interleaved(main, side):
    done = 0
    for k, step in enumerate(main):
        step()
        due = (k + 1) * len(side) // len(main)
        for s in side[done:due]:
            s()
        done = due


def _pos(shape, axis):
    return lax.broadcasted_iota(jnp.int32, shape, axis).astype(F32)


def _ret_tables_kernel(lgf_ref, lgb_ref, decay_ref, row_ref, col_ref, cdec_ref):
    h = pl.program_id(0)
    lg_f = lgf_ref[h]
    lg_b = lgb_ref[h]
    c = RET_CHUNK
    diff = _pos((c, c), 0) - _pos((c, c), 1)
    fwd = jnp.where(diff >= 0, jnp.exp(jnp.maximum(diff, 0.0) * lg_f), 0.0)
    bwd = jnp.where(diff <= 0, jnp.exp(jnp.maximum(-diff, 0.0) * lg_b), 0.0)
    decay_ref[...] = fwd + bwd
    j = _pos((RET_TAB_ROWS, c), 1)
    row_ref[0] = jnp.exp((c - 1.0 - j) * lg_f)
    row_ref[1] = jnp.exp(j * lg_b)
    i = _pos((c, LANES), 0)
    col_ref[0] = jnp.exp((i + 1.0) * lg_f)
    col_ref[1] = jnp.exp((c - i) * lg_b)
    full = jnp.full((RET_TAB_ROWS, RET_VAL_DIM), float(c), F32)
    cdec_ref[0] = jnp.exp(full * lg_f)
    cdec_ref[1] = jnp.exp(full * lg_b)


def _ret_tables(lg_f, lg_b):
    smem = pl.BlockSpec(memory_space=pltpu.SMEM)
    c = RET_CHUNK
    shapes = ((c, c), (2, RET_TAB_ROWS, c), (2, c, LANES), (2, RET_TAB_ROWS, RET_VAL_DIM))
    return pl.pallas_call(
        _ret_tables_kernel,
        out_shape=tuple(jax.ShapeDtypeStruct((RET_HEADS,) + s, F32) for s in shapes),
        grid=(RET_HEADS,),
        in_specs=[smem, smem],
        out_specs=tuple(pl.BlockSpec((None,) + s, lambda h, n=len(s): (h,) + (0,) * n) for s in shapes),
        compiler_params=_params("arbitrary"),
        name="ret_tables",
    )(lg_f, lg_b)


def _group_norm_gate(o, gn, z):
    mu = jnp.mean(o, axis=-1, keepdims=True)
    var = jnp.mean(jnp.square(o - mu), axis=-1, keepdims=True)
    on = (o - mu) * lax.rsqrt(var + EPS)
    return (on * gn * z.astype(F32)).astype(BF16)


def _ret_ctx_kernel(q_ref, kt_ref, v_ref, z_ref, gn_ref, decay_ref, row_ref, y_ref, sf_ref, sb_ref):
    for s in range(RET_CTX_SEQS):
        rows = slice(s * SEQ, (s + 1) * SEQ)
        for h in range(RET_HEADS):
            kc = slice(h * RET_KEY_DIM, (h + 1) * RET_KEY_DIM)
            vc = slice(h * RET_VAL_DIM, (h + 1) * RET_VAL_DIM)
            kt = kt_ref[s, kc, :]
            v = v_ref[rows, vc]
            att = (_dot(q_ref[rows, kc], kt) * decay_ref[h]).astype(BF16)
            y_ref[rows, vc] = _group_norm_gate(_dot(att, v), gn_ref[:, vc], z_ref[rows, vc])
            ktf = kt.astype(F32)
            sf_ref[s, h] = _dot((ktf * row_ref[h, 0, 0:1, :]).astype(BF16), v)
            sb_ref[s, h] = _dot((ktf * row_ref[h, 1, 0:1, :]).astype(BF16), v)


def _ret_ctx(q, kt, v, z, gn, decay, row):
    c = RET_CHUNK
    n = RET_CTX_SEQS
    wide = lambda w: pl.BlockSpec((n * SEQ, w), lambda b: (b, 0))
    st_spec = pl.BlockSpec((n, RET_HEADS, RET_KEY_DIM, RET_VAL_DIM), lambda b: (b, 0, 0, 0))
    st_shape = jax.ShapeDtypeStruct((BATCH, RET_HEADS, RET_KEY_DIM, RET_VAL_DIM), F32)
    return pl.pallas_call(
        _ret_ctx_kernel,
        out_shape=(jax.ShapeDtypeStruct((N_PROMPT_TOK, RET_V_WIDTH), BF16), st_shape, st_shape),
        grid=(BATCH // n,),
        in_specs=[wide(RET_QK_WIDTH), pl.BlockSpec((n, RET_QK_WIDTH, c), lambda b: (b, 0, 0)),
                  wide(RET_V_WIDTH), wide(RET_V_WIDTH), _const_spec((1, RET_V_WIDTH)),
                  _const_spec((RET_HEADS, c, c)), _const_spec((RET_HEADS, 2, RET_TAB_ROWS, c))],
        out_specs=(wide(RET_V_WIDTH), st_spec, st_spec),
        compiler_params=_params("arbitrary", vmem=VMEM_LIMIT),
        name="ret_ctx",
    )(q, kt, v, z, gn, decay, row)


def _ret_lat_kernel(q_ref, kt_ref, v_ref, z_ref, gn_ref, decay_ref, row_ref, col_ref, cdec_ref,
                    s0f_ref, s0b_ref, y_ref, sf_all, sb_all, sf_acc, sb_acc):
    c = RET_CHUNK
    n_chunks = DEC_SEQ // c
    rows_of = lambda ci: pl.ds(pl.multiple_of(ci * c, c), c)

    sf_acc[...] = s0f_ref[...]
    sb_acc[...] = s0b_ref[...]

    def scan_step(i, carry):
        cf = i
        cb = n_chunks - 1 - i
        sf_all[cf] = sf_acc[...].astype(BF16)
        sb_all[cb] = sb_acc[...].astype(BF16)
        uf = _dot((kt_ref[cf].astype(F32) * row_ref[0, 0:1, :]).astype(BF16), v_ref[rows_of(cf), :])
        ub = _dot((kt_ref[cb].astype(F32) * row_ref[1, 0:1, :]).astype(BF16), v_ref[rows_of(cb), :])
        sf_acc[...] = sf_acc[...] * cdec_ref[0, 0:1, :] + uf
        sb_acc[...] = sb_acc[...] * cdec_ref[1, 0:1, :] + ub
        return carry

    lax.fori_loop(0, n_chunks, scan_step, 0, unroll=4)

    def out_step(ci, carry):
        rows = rows_of(ci)
        q = q_ref[rows, :]
        qf = q.astype(F32)
        qdec_f = jnp.concatenate([col_ref[0]] * (RET_KEY_DIM // LANES), axis=1)
        qdec_b = jnp.concatenate([col_ref[1]] * (RET_KEY_DIM // LANES), axis=1)
        att = (_dot(q, kt_ref[ci]) * decay_ref[...]).astype(BF16)
        o = (_dot(att, v_ref[rows, :])
             + _dot((qf * qdec_f).astype(BF16), sf_all[ci])
             + _dot((qf * qdec_b).astype(BF16), sb_all[ci]))
        y_ref[rows, :] = _group_norm_gate(o, gn_ref[...], z_ref[rows, :])
        return carry

    lax.fori_loop(0, n_chunks, out_step, 0, unroll=8)


def _ret_lat(q, kt, v, z, gn, decay, row, col, cdec, s0f, s0b):
    c = RET_CHUNK
    n_chunks = DEC_SEQ // c
    row0 = N_PROMPT_TOK // DEC_SEQ
    qk_spec = pl.BlockSpec((DEC_SEQ, RET_KEY_DIM), lambda b, h: (row0 + b, h))
    v_spec = pl.BlockSpec((DEC_SEQ, RET_VAL_DIM), lambda b, h: (row0 + b, h))
    st_spec = pl.BlockSpec((None, None, RET_KEY_DIM, RET_VAL_DIM), lambda b, h: (b, h, 0, 0))
    tab = lambda *s: pl.BlockSpec((None,) + s, lambda b, h: (h,) + (0,) * len(s))
    states = pltpu.VMEM((n_chunks, RET_KEY_DIM, RET_VAL_DIM), BF16)
    acc = pltpu.VMEM((RET_KEY_DIM, RET_VAL_DIM), F32)
    return pl.pallas_call(
        _ret_lat_kernel,
        out_shape=jax.ShapeDtypeStruct((N_SAMPLE_TOK, RET_V_WIDTH), BF16),
        grid=(DEC_BATCH, RET_HEADS),
        in_specs=[qk_spec,
                  pl.BlockSpec((n_chunks, RET_KEY_DIM, c), lambda b, h: (row0 + b, h, 0)),
                  v_spec, v_spec, pl.BlockSpec((1, RET_VAL_DIM), lambda b, h: (0, h)),
                  tab(c, c), tab(2, RET_TAB_ROWS, c), tab(2, c, LANES), tab(2, RET_TAB_ROWS, RET_VAL_DIM),
                  st_spec, st_spec],
        out_specs=pl.BlockSpec((DEC_SEQ, RET_VAL_DIM), lambda b, h: (b, h)),
        scratch_shapes=[states, states, acc, acc],
        compiler_params=_params("arbitrary", "arbitrary", vmem=VMEM_LIMIT),
        name="ret_lat",
    )(q, kt, v, z, gn, decay, row, col, cdec, s0f, s0b)


def _rope_tables(tm):
    n_rows = DEC_SEQ // GRID_W
    rows = jnp.repeat(jnp.arange(n_rows), GRID_W).astype(F32)
    cols = jnp.tile(jnp.arange(GRID_W), n_rows).astype(F32)
    half = ATTN_HEAD_DIM // 4
    inv = ROPE_BASE ** (-jnp.arange(half, dtype=F32) / half)
    ang_r = rows[:, None] * inv[None, :]
    ang_c = cols[:, None] * inv[None, :]
    cos = jnp.concatenate([jnp.cos(ang_r), jnp.cos(ang_r), jnp.cos(ang_c), jnp.cos(ang_c)], axis=-1)
    sin = jnp.concatenate([-jnp.sin(ang_r), jnp.sin(ang_r), -jnp.sin(ang_c), jnp.sin(ang_c)], axis=-1)
    cos = jnp.concatenate([jnp.ones((tm, ATTN_HEAD_DIM), F32), cos], axis=0)
    sin = jnp.concatenate([jnp.zeros((tm, ATTN_HEAD_DIM), F32), sin], axis=0)
    return jnp.tile(cos, (1, 2)), jnp.tile(sin, (1, 2))


def kernel(x_prompt, x_sample, cache_k, cache_v, state_fwd, state_bwd, c, c_ctx, norm_g, ada_w, ada_b, attn_w_in, attn_w_out, attn_sink, pool_w_in, pool_w_grp, pool_scale, pool_w_out, ret_w_in, ret_decay_fwd, ret_decay_bwd, ret_gn_g, ret_w_out, final_g):
    x_parts = (x_prompt.reshape(N_PROMPT_TOK, D_MODEL), x_sample.reshape(N_SAMPLE_TOK, D_MODEL))
    cond = jnp.concatenate([c_ctx[None, :], c,
                            jnp.zeros((N_COND - 1 - DEC_BATCH, D_MODEL), F32)], axis=0)
    mods = _ada_table(cond.T, ada_w, ada_b).reshape(DEPTH, N_COND, 1, 3 * D_MODEL)
    cos_t, sin_t = _rope_tables(PROJ.tm)

    to_kernel = lambda a: jnp.transpose(a, (0, 1, 3, 4, 2))
    from_kernel = lambda a: jnp.transpose(
        a.reshape(a.shape[0], a.shape[1], ATTN_KV_HEADS, ATTN_HEAD_DIM, a.shape[3]), (0, 1, 4, 2, 3))
    ctx_k, ctx_v = to_kernel(cache_k), to_kernel(cache_v)

    assert DEPTH % N_MIXERS == 1, "the layer stack must end on an attention layer"
    caches = ()
    new_sf = new_sb = None
    mixed = None
    for i in range(DEPTH):
        kind, j = i % N_MIXERS, i // N_MIXERS
        g = norm_g[i].reshape(1, D_MODEL)
        mod = mods[i]
        if kind == 0:
            outs = _attn_in(mixed, x_parts, g, mod, attn_w_in, j, cos_t, sin_t, tuple(caches))
            if mixed is not None:
                x_parts, outs = (outs[0],), outs[1:]
            q, k4, v4, z, *caches = outs
            y_parts = (_ctx_attn(attn_sink[j], q, k4, v4, z),
                       _lat_attn(attn_sink[j], q, k4, v4, ctx_k, ctx_v, j, z))
            mixed = Mixed(y_parts, attn_w_out, j, mod)
        elif kind == 1:
            x, u, z = _pool_in(mixed, x_parts, g, mod, pool_w_in, j)
            x_parts, mixed, pooled = (x,), None, (u, z, j, mod)
        else:
            u, z, jp, mod_pool = pooled
            lg_f = jax.nn.log_sigmoid(ret_decay_fwd[j].astype(F32))
            lg_b = jax.nn.log_sigmoid(ret_decay_bwd[j].astype(F32))
            gn = ret_gn_g[j].reshape(1, RET_V_WIDTH)
            x, q, kt, v, z = _pool_ret(u, z, pool_w_grp, pool_scale[jp].reshape(1, D_MODEL), pool_w_out, jp,
                                       x_parts[0], mod_pool, g, mod, ret_w_in, j)
            x_parts = (x,)
            decay, row, col, cdec = _ret_tables(lg_f, lg_b)
            y_ctx, new_sf, new_sb = _ret_ctx(q, kt, v, z, gn, decay, row)
            y_parts = (y_ctx, _ret_lat(q, kt, v, z, gn, decay, row, col, cdec,
                                       state_fwd[:, j], state_bwd[:, j]))
            mixed = Mixed(y_parts, ret_w_out, j, mod)
    y_prompt, y_sample = _out_proj_final(mixed, x_parts[0], final_g.reshape(1, D_MODEL))
    new_k, new_v = caches
    return (y_prompt.reshape(BATCH, SEQ, D_MODEL), y_sample.reshape(DEC_BATCH, DEC_SEQ, D_MODEL),
            from_kernel(new_k), from_kernel(new_v), new_sf[:, None], new_sb[:, None])
```

```python
import functools
from typing import NamedTuple

import jax
import jax.numpy as jnp
from jax import lax
from jax.experimental import pallas as pl
from jax.experimental.pallas import tpu as pltpu

F32 = jnp.float32
BF16 = jnp.bfloat16

D_MODEL = 1024
BATCH = 16
SEQ = 256
DEPTH = 4
DEC_BATCH = 2
DEC_SEQ = 2048
PAST_LEN = 512
GRID_W = 64
N_MIXERS = 3
ATTN_HEADS = 16
ATTN_KV_HEADS = 4
ATTN_HEAD_DIM = 64
ATTN_GROUP = 4
ATTN_WIDTH = 1024
ATTN_KV_WIDTH = 256
WINDOW = 128
ROPE_BASE = 10000.0
POOL_WINDOWS = (2, 4, 8, 16)
POOL_GROUP_DIM = 256
RET_HEADS = 4
RET_KEY_DIM = 256
RET_VAL_DIM = 512
RET_QK_WIDTH = 1024
RET_V_WIDTH = 2048
EPS = 1e-6
NEG_INF = -1e30
LOG2_E = 1.4426950408889634

N_PROMPT_TOK = BATCH * SEQ
N_SAMPLE_TOK = DEC_BATCH * DEC_SEQ
N_TOK = N_PROMPT_TOK + N_SAMPLE_TOK
N_COND = 8
LANES = 128
MXU_N = 256
Q_BLOCK = 128
Q_SUB = 4
RET_CHUNK = 256
RET_TAB_ROWS = 8
RET_CTX_SEQS = 2
POOL_HALO = 8
POOL_SUB = SEQ
WEIGHT_STAGE_COLS = 512
VMEM_LIMIT = 48 * 1024 * 1024
FUSED_VMEM_LIMIT = 58 * 1024 * 1024


class Tiling(NamedTuple):
    tm: int

    @property
    def n_tiles(self):
        return N_TOK // self.tm

    @property
    def n_prompt(self):
        return N_PROMPT_TOK // self.tm

    @property
    def per_dec_seq(self):
        return DEC_SEQ // self.tm

    def cond(self, i):
        return jnp.where(i < self.n_prompt, 0, 1 + (i - self.n_prompt) // self.per_dec_seq)

    def seq_tile(self, i):
        return jnp.where(i < self.n_prompt, 0, (i - self.n_prompt) % self.per_dec_seq)

    def rows(self, width):
        return pl.BlockSpec((self.tm, width), lambda i: (i, 0))

    def prompt_rows(self, width):
        return pl.BlockSpec((self.tm, width), lambda i: (jnp.minimum(i, self.n_prompt - 1), 0))

    def sample_rows(self, width):
        return pl.BlockSpec((self.tm, width), lambda i: (jnp.maximum(i - self.n_prompt, 0), 0))

    def mod(self, part):
        return pl.BlockSpec((None, 1, D_MODEL), lambda i: (self.cond(i), 0, part))


PROJ = Tiling(512)
FINAL = Tiling(1024)


def _silu(z):
    hz = 0.5 * z
    return hz + hz * jnp.tanh(hz)


def _dot(a, b):
    return jnp.dot(a, b, preferred_element_type=F32)


def _dot_nt(a, b):
    return lax.dot_general(a, b, (((1,), (1,)), ((), ())), preferred_element_type=F32)


def _params(*sem, vmem=None):
    return pltpu.CompilerParams(dimension_semantics=sem, vmem_limit_bytes=vmem)


def _const_spec(shape):
    nd = len(shape)
    return pl.BlockSpec(shape, lambda *_: (0,) * nd, pipeline_mode=pl.Buffered(1))


def _part_specs(t, parts, width):
    if len(parts) == 1:
        return [t.rows(width)]
    return [t.prompt_rows(width), t.sample_rows(width)]


def _read_parts(t, refs):
    if len(refs) == 1:
        return refs[0][...]
    return jnp.where(pl.program_id(0) < t.n_prompt, refs[0][...], refs[1][...])


def _ada_kernel(cond_ref, w_ref, b_ref, o_ref):
    s = _silu(cond_ref[...])
    w = w_ref[...]
    rows = [jnp.sum(s[:, c:c + 1] * w, axis=0, keepdims=True) + b_ref[...] for c in range(1 + DEC_BATCH)]
    rows.append(jnp.zeros((N_COND - len(rows), w.shape[1]), F32))
    o_ref[...] = jnp.concatenate(rows, axis=0)


def _ada_table(cond, ada_w, ada_b):
    tn = 3 * D_MODEL // 2
    return pl.pallas_call(
        _ada_kernel,
        out_shape=jax.ShapeDtypeStruct((DEPTH, N_COND, 3 * D_MODEL), F32),
        grid=(DEPTH, 3 * D_MODEL // tn),
        in_specs=[
            pl.BlockSpec((D_MODEL, N_COND), lambda l, n: (0, 0)),
            pl.BlockSpec((None, D_MODEL, tn), lambda l, n: (l, 0, n)),
            pl.BlockSpec((None, 1, tn), lambda l, n: (l, 0, n)),
        ],
        out_specs=pl.BlockSpec((None, N_COND, tn), lambda l, n: (l, 0, n)),
        compiler_params=_params("arbitrary", "arbitrary", vmem=VMEM_LIMIT),
        name="ada_table",
    )(cond, ada_w, ada_b.reshape(DEPTH, 1, 3 * D_MODEL))


def _rms(x, g):
    return x * lax.rsqrt(jnp.mean(x * x, axis=-1, keepdims=True) + EPS) * g


def _norm_mod(x, g_ref, sh_ref, sc_ref):
    return (_rms(x, g_ref[...]) * (1.0 + sc_ref[...]) + sh_ref[...]).astype(BF16)


def _proj_chunks(h, w_ref, lo, width, store):
    for c in range(width // MXU_N):
        store(c, _dot(h, w_ref[:, lo + c * MXU_N:lo + (c + 1) * MXU_N]))


def _layer_spec(w, j):
    nd = w.ndim - 1
    return pl.BlockSpec((None,) + w.shape[1:], lambda *_: (j,) + (0,) * nd, pipeline_mode=pl.Buffered(1))


def _cast_weight_once(w_ref, wb_ref):
    @pl.when(pl.program_id(0) == 0)
    def _():
        rows = wb_ref.shape[-2]
        for r in range(0, rows, MXU_N):
            wb_ref[..., r:r + MXU_N, :] = w_ref[..., r:r + MXU_N, :].astype(BF16)


class Mixed(NamedTuple):
    y_parts: tuple
    w_out: jax.Array
    j: int
    mod: jax.Array


def _residual_specs(t, mixed, x_parts):
    k = mixed.w_out.shape[1]
    return (_part_specs(t, mixed.y_parts, k) + [_layer_spec(mixed.w_out, mixed.j)]
            + _part_specs(t, x_parts, D_MODEL) + [t.mod(2)])


def _residual_args(mixed, x_parts):
    return (*mixed.y_parts, mixed.w_out, *x_parts, mixed.mod)


def _residual_update(t, refs, n_y, n_x, wo_ref, xo_ref):
    y_refs, wof_ref, x_refs, gate_ref = refs[:n_y], refs[n_y], refs[n_y + 1:n_y + 1 + n_x], refs[n_y + 1 + n_x]
    _cast_weight_once(wof_ref, wo_ref)
    x = _read_parts(t, x_refs) + gate_ref[...] * _dot(_read_parts(t, y_refs), wo_ref[...])
    xo_ref[...] = x
    return x


def _rep4(a, h):
    half = a[:, (h // 2) * LANES:(h // 2 + 1) * LANES]
    lane = lax.broadcasted_iota(jnp.int32, half.shape, 1)
    keep = (lane < ATTN_HEAD_DIM) if h % 2 == 0 else (lane >= ATTN_HEAD_DIM)
    m = jnp.where(keep, half, 0.0)
    s = m + pltpu.roll(m, ATTN_HEAD_DIM, 1)
    return jnp.concatenate([s, s], axis=1)


def _attn_in_kernel(*refs, j, n_y, n_x, n_alias):
    t = PROJ
    n_res = n_y + n_x + 2 if n_y else n_x
    res_refs, refs = refs[:n_res], refs[n_res:]
    (g_ref, sh_ref, sc_ref, wf_ref, cos_ref, sin_ref), refs = refs[:6], refs[6 + n_alias:]
    if n_y:
        xo_ref, q_ref, k4_ref, v4_ref, z_ref, kc_ref, vc_ref, wo_ref, w_ref = refs
        x = _residual_update(t, res_refs, n_y, n_x, wo_ref, xo_ref)
    else:
        q_ref, k4_ref, v4_ref, z_ref, kc_ref, vc_ref, w_ref = refs
        x = _read_parts(t, res_refs)
    i = pl.program_id(0)
    _cast_weight_once(wf_ref, w_ref)
    h = _norm_mod(x, g_ref, sh_ref, sc_ref)
    cos = cos_ref[...]
    sin = sin_ref[...]
    lane = lax.broadcasted_iota(jnp.int32, (t.tm, LANES), 1)
    first = (lane % (ATTN_HEAD_DIM // 2)) < ATTN_HEAD_DIM // 4

    def rope(a):
        rot = jnp.where(first, pltpu.roll(a, LANES - ATTN_HEAD_DIM // 4, 1),
                        pltpu.roll(a, ATTN_HEAD_DIM // 4, 1))
        return a * cos + rot * sin

    def rope_wide(a):
        return jnp.concatenate(
            [rope(a[:, s * LANES:(s + 1) * LANES]) for s in range(MXU_N // LANES)], axis=1)

    scale = ATTN_HEAD_DIM ** -0.5 * LOG2_E

    def store_q(c, a):
        q_ref[:, c * MXU_N:(c + 1) * MXU_N] = (rope_wide(a) * scale).astype(BF16)

    def store_z(c, a):
        z_ref[:, c * MXU_N:(c + 1) * MXU_N] = _silu(a).astype(BF16)

    _proj_chunks(h, w_ref, 0, ATTN_WIDTH, store_q)
    k = rope_wide(_dot(h, w_ref[:, ATTN_WIDTH:ATTN_WIDTH + ATTN_KV_WIDTH]))
    v = _dot(h, w_ref[:, ATTN_WIDTH + ATTN_KV_WIDTH:ATTN_WIDTH + 2 * ATTN_KV_WIDTH])
    for hh in range(ATTN_KV_HEADS):
        k4_ref[:, hh * MXU_N:(hh + 1) * MXU_N] = _rep4(k, hh).astype(BF16)
        v4_ref[:, hh * MXU_N:(hh + 1) * MXU_N] = _rep4(v, hh).astype(BF16)
    _proj_chunks(h, w_ref, ATTN_WIDTH + 2 * ATTN_KV_WIDTH, ATTN_WIDTH, store_z)

    @pl.when(i < t.n_prompt)
    def _():
        for s in range(t.tm // SEQ):
            kt = k[s * SEQ:(s + 1) * SEQ, :].T
            vt = v[s * SEQ:(s + 1) * SEQ, :].T
            if n_alias:
                kc_ref[s] = kt
                vc_ref[s] = vt
            else:
                for l in range(kc_ref.shape[1]):
                    kc_ref[s, l] = kt if l == j else jnp.zeros_like(kt)
                    vc_ref[s, l] = vt if l == j else jnp.zeros_like(vt)


def _attn_in(mixed, x_parts, g, mod, w_all, j, cos_t, sin_t, caches):
    t = PROJ
    n_in = 2 * ATTN_WIDTH + 2 * ATTN_KV_WIDTH
    per = t.tm // SEQ
    rope_spec = pl.BlockSpec(
        (t.tm, LANES), lambda i: (jnp.where(i < t.n_prompt, 0, 1 + t.seq_tile(i)), 0))
    wide = jax.ShapeDtypeStruct((N_TOK, ATTN_WIDTH), BF16)
    n_attn = w_all.shape[0]
    cache = jax.ShapeDtypeStruct((BATCH, n_attn, ATTN_KV_WIDTH, SEQ), F32)
    if caches:
        cache_spec = pl.BlockSpec((per, None, ATTN_KV_WIDTH, SEQ),
                                  lambda i: (jnp.minimum(i, t.n_prompt - 1), j, 0, 0))
    else:
        cache_spec = pl.BlockSpec((per, n_attn, ATTN_KV_WIDTH, SEQ),
                                  lambda i: (jnp.minimum(i, t.n_prompt - 1), 0, 0, 0))
    n_x = len(x_parts)
    if mixed is None:
        n_y, res_specs, res_args, res_out, res_ospecs, res_scratch = 0, _part_specs(t, x_parts, D_MODEL), x_parts, (), (), []
    else:
        n_y = len(mixed.y_parts)
        res_specs, res_args = _residual_specs(t, mixed, x_parts), _residual_args(mixed, x_parts)
        res_out = (jax.ShapeDtypeStruct((N_TOK, D_MODEL), F32),)
        res_ospecs = (t.rows(D_MODEL),)
        res_scratch = [pltpu.VMEM(mixed.w_out.shape[1:], BF16)]
    n_front = len(res_specs) + 6
    return pl.pallas_call(
        functools.partial(_attn_in_kernel, j=j, n_y=n_y, n_x=n_x, n_alias=len(caches)),
        out_shape=res_out + (wide, wide, wide, wide, cache, cache),
        grid=(t.n_tiles,),
        in_specs=res_specs + [
            _const_spec((1, D_MODEL)), t.mod(0), t.mod(1), _layer_spec(w_all, j),
            rope_spec, rope_spec] + [pl.BlockSpec(memory_space=pl.ANY)] * len(caches),
        out_specs=res_ospecs + (t.rows(ATTN_WIDTH),) * 4 + (cache_spec,) * 2,
        scratch_shapes=res_scratch + [pltpu.VMEM((D_MODEL, n_in), BF16)],
        input_output_aliases={n_front + c: len(res_out) + 4 + c for c in range(len(caches))},
        compiler_params=_params("arbitrary", vmem=FUSED_VMEM_LIMIT),
        name="attn_in",
    )(*res_args, g, mod, mod, w_all, cos_t, sin_t, *caches)


def _stack_group_queries(q):
    qf = q.astype(F32)
    chunk = lax.broadcasted_iota(jnp.int32, qf.shape, 1) // ATTN_HEAD_DIM
    return jnp.concatenate(
        [jnp.where(chunk == g, qf, 0.0) for g in range(ATTN_GROUP)], axis=0).astype(BF16)


def _gather_group_outputs(o, rows):
    chunk = lax.broadcasted_iota(jnp.int32, (rows, MXU_N), 1) // ATTN_HEAD_DIM
    acc = jnp.zeros((rows, MXU_N), F32)
    for g in range(ATTN_GROUP):
        acc = acc + jnp.where(chunk == g, o[g * rows:(g + 1) * rows], 0.0)
    return acc


def _sink_column(sink_ref, h, rows):
    grp = lax.broadcasted_iota(jnp.int32, (ATTN_GROUP * rows, 1), 0) // rows
    col = jnp.zeros((ATTN_GROUP * rows, 1), F32)
    for g in range(ATTN_GROUP):
        col = jnp.where(grp == g, sink_ref[h * ATTN_GROUP + g] * LOG2_E, col)
    return col


def _chunk_rows(dtype):
    chunk = lax.broadcasted_iota(jnp.int32, (1, MXU_N), 1) // ATTN_HEAD_DIM
    return [(chunk == g).astype(F32).astype(dtype) for g in range(ATTN_GROUP)]


def _block_diag_rows(x4):
    return jnp.concatenate([x4 * m for m in _chunk_rows(x4.dtype)], axis=0)


def _ctx_attn_kernel(sink_ref, q_ref, k4_ref, v4_ref, z_ref, y_ref):
    chunk = lax.broadcasted_iota(jnp.int32, (SEQ, MXU_N), 1) // ATTN_HEAD_DIM
    for h in range(ATTN_KV_HEADS):
        cols = slice(h * MXU_N, (h + 1) * MXU_N)
        s = _dot_nt(q_ref[:, cols], _block_diag_rows(k4_ref[:, cols]))
        inv = jnp.zeros((SEQ, MXU_N), F32)
        probs = []
        for g in range(ATTN_GROUP):
            sg = s[:, g * SEQ:(g + 1) * SEQ]
            sk = sink_ref[h * ATTN_GROUP + g] * LOG2_E
            m = jnp.maximum(jnp.max(sg, axis=1, keepdims=True), sk)
            e = jnp.exp2(sg - m)
            den = jnp.sum(e, axis=1, keepdims=True) + jnp.exp2(sk - m)
            probs.append(e.astype(BF16))
            inv = jnp.where(chunk == g, 1.0 / den, inv)
        o = _dot(jnp.concatenate(probs, axis=1), _block_diag_rows(v4_ref[:, cols]))
        y_ref[:, cols] = (o * inv * z_ref[:, cols].astype(F32)).astype(BF16)


def _ctx_attn(sink, q, k4, v4, z):
    spec = pl.BlockSpec((SEQ, ATTN_WIDTH), lambda b: (b, 0))
    return pl.pallas_call(
        _ctx_attn_kernel,
        out_shape=jax.ShapeDtypeStruct((N_PROMPT_TOK, ATTN_WIDTH), BF16),
        grid=(BATCH,),
        in_specs=[pl.BlockSpec(memory_space=pltpu.SMEM), spec, spec, spec, spec],
        out_specs=spec,
        compiler_params=_params("arbitrary"),
        name="ctx_attn",
    )(sink, q, k4, v4, z)


LAT_STEP = Q_SUB * Q_BLOCK
LAT_PER_SEQ = DEC_SEQ // LAT_STEP
LAT_BLOCKS = DEC_BATCH * ATTN_KV_HEADS * LAT_PER_SEQ


def _lat_block(blk):
    return (blk // (ATTN_KV_HEADS * LAT_PER_SEQ), (blk // LAT_PER_SEQ) % ATTN_KV_HEADS, blk % LAT_PER_SEQ)


def _band_rows(qb):
    return pl.ds(pl.multiple_of(qb * Q_BLOCK, Q_BLOCK), 3 * Q_BLOCK)


def _pad_sequence(dst, src):
    zeros = jnp.zeros((Q_BLOCK, MXU_N), BF16)
    dst[0:Q_BLOCK, :] = zeros
    dst[Q_BLOCK:Q_BLOCK + DEC_SEQ, :] = src[...]
    dst[Q_BLOCK + DEC_SEQ:, :] = zeros


def _lat_attn_kernel(sink_ref, q_ref, k4_ref, v4_ref, kc_ref, vc_ref, z_ref, y_ref,
                     kp, vp, k4c, v4c, sc_a, sl_a, sc_b, sl_b, e_scr):
    t = pl.program_id(0)
    _, _, n1 = _lat_block(jnp.minimum(t, LAT_BLOCKS - 1))
    _, h0, n0 = _lat_block(jnp.maximum(t - 1, 0))
    n_blocks = DEC_SEQ // Q_BLOCK

    @pl.when(t == 0)
    def _():
        sc_b[...] = jnp.zeros(sc_b.shape, F32)
        sl_b[...] = jnp.zeros(sl_b.shape, F32)

    @pl.when(n1 == 0)
    def _():
        _pad_sequence(kp, k4_ref)
        k4c[...] = jnp.concatenate([kc_ref[...]] * ATTN_GROUP, axis=0).astype(BF16)

    @pl.when(n0 == 0)
    def _():
        _pad_sequence(vp, v4_ref)
        v4c[...] = jnp.concatenate([vc_ref[...]] * ATTN_GROUP, axis=0).T.astype(BF16)

    @pl.when(t % 2 == 0)
    def _():
        _lat_stages(sink_ref, q_ref, z_ref, y_ref, kp, vp, k4c, v4c, e_scr, n1, h0, n0,
                    sc_a, sl_a, sc_b, sl_b)

    @pl.when(t % 2 == 1)
    def _():
        _lat_stages(sink_ref, q_ref, z_ref, y_ref, kp, vp, k4c, v4c, e_scr, n1, h0, n0,
                    sc_b, sl_b, sc_a, sl_a)


def _lat_stages(sink_ref, q_ref, z_ref, y_ref, kp, vp, k4c, v4c, e_scr, n1, h0, n0,
                sc_w, sl_w, sc_r, sl_r):
    n_blocks = DEC_SEQ // Q_BLOCK
    for sub in range(Q_SUB):
        qs = _stack_group_queries(q_ref[sub * Q_BLOCK:(sub + 1) * Q_BLOCK, :])
        sc_w[sub] = _dot(qs, k4c[...])
        sl_w[sub] = _dot_nt(qs, kp[_band_rows(n1 * Q_SUB + sub), :])

    rows4 = ATTN_GROUP * Q_BLOCK
    r = lax.broadcasted_iota(jnp.int32, (rows4, Q_BLOCK), 0) % Q_BLOCK
    c = lax.broadcasted_iota(jnp.int32, (rows4, Q_BLOCK), 1)
    in_left = c >= r
    in_right = c <= r
    sk = _sink_column(sink_ref, h0, Q_BLOCK)
    for sub in range(Q_SUB):
        qb = n0 * Q_SUB + sub
        band = _band_rows(qb)
        qrows = slice(sub * Q_BLOCK, (sub + 1) * Q_BLOCK)
        s_ctx = sc_r[sub]
        s_lat = sl_r[sub]
        blocks = [
            s_ctx,
            jnp.where(jnp.logical_and(in_left, qb > 0), s_lat[:, :Q_BLOCK], NEG_INF),
            s_lat[:, Q_BLOCK:2 * Q_BLOCK],
            jnp.where(jnp.logical_and(in_right, qb < n_blocks - 1), s_lat[:, 2 * Q_BLOCK:], NEG_INF),
        ]
        m = sk
        for s in blocks:
            m = jnp.maximum(m, jnp.max(s, axis=1, keepdims=True))
        den = jnp.exp2(sk - m)
        lo = 0
        for s in blocks:
            e = jnp.exp2(s - m)
            den = den + jnp.sum(e, axis=1, keepdims=True)
            e_scr[sub, :, lo:lo + s.shape[1]] = e.astype(BF16)
            lo += s.shape[1]
        o = (_dot(e_scr[sub, :, :PAST_LEN], v4c[...])
             + _dot(e_scr[sub, :, PAST_LEN:], vp[band, :])) * (1.0 / den)
        acc = _gather_group_outputs(o, Q_BLOCK)
        y_ref[qrows, :] = (acc * z_ref[qrows, :].astype(F32)).astype(BF16)


def _lat_attn(sink, q, k4, v4, kc, vc, j, z):
    padded = DEC_SEQ + 2 * Q_BLOCK
    rows4 = ATTN_GROUP * Q_BLOCK
    row0 = N_PROMPT_TOK // LAT_STEP
    seq0 = N_PROMPT_TOK // DEC_SEQ
    stage1 = lambda t: _lat_block(jnp.minimum(t, LAT_BLOCKS - 1))
    stage2 = lambda t: _lat_block(jnp.maximum(t - 1, 0))

    def spec(shape, stage, index):
        return pl.BlockSpec(shape, lambda t: index(*stage(t)))

    tile = (LAT_STEP, MXU_N)
    seq = (DEC_SEQ, MXU_N)
    ctx = (None, None, None, ATTN_HEAD_DIM, PAST_LEN)
    return pl.pallas_call(
        _lat_attn_kernel,
        out_shape=jax.ShapeDtypeStruct((N_SAMPLE_TOK, ATTN_WIDTH), BF16),
        grid=(LAT_BLOCKS + 1,),
        in_specs=[pl.BlockSpec(memory_space=pltpu.SMEM),
                  spec(tile, stage1, lambda b, h, n: (row0 + b * LAT_PER_SEQ + n, h)),
                  spec(seq, stage1, lambda b, h, n: (seq0 + b, h)),
                  spec(seq, stage2, lambda b, h, n: (seq0 + b, h)),
                  spec(ctx, stage1, lambda b, h, n: (b, j, h, 0, 0)),
                  spec(ctx, stage2, lambda b, h, n: (b, j, h, 0, 0)),
                  spec(tile, stage2, lambda b, h, n: (row0 + b * LAT_PER_SEQ + n, h))],
        out_specs=spec(tile, stage2, lambda b, h, n: (b * LAT_PER_SEQ + n, h)),
        scratch_shapes=[pltpu.VMEM((padded, MXU_N), BF16), pltpu.VMEM((padded, MXU_N), BF16),
                        pltpu.VMEM((MXU_N, PAST_LEN), BF16), pltpu.VMEM((PAST_LEN, MXU_N), BF16),
                        pltpu.VMEM((Q_SUB, rows4, PAST_LEN), F32),
                        pltpu.VMEM((Q_SUB, rows4, 3 * Q_BLOCK), F32),
                        pltpu.VMEM((Q_SUB, rows4, PAST_LEN), F32),
                        pltpu.VMEM((Q_SUB, rows4, 3 * Q_BLOCK), F32),
                        pltpu.VMEM((Q_SUB, rows4, PAST_LEN + 3 * Q_BLOCK), BF16)],
        compiler_params=_params("arbitrary", vmem=VMEM_LIMIT),
        name="lat_attn",
    )(sink, q, k4, v4, kc, vc, z)


def _out_final_kernel(yp_ref, ys_ref, wf_ref, x_ref, gate_ref, fg_ref, op_ref, os_ref, w_ref):
    t = FINAL
    i = pl.program_id(0)
    _cast_weight_once(wf_ref, w_ref)
    y = _read_parts(t, (yp_ref, ys_ref))
    r = _rms(x_ref[...] + gate_ref[...] * _dot(y, w_ref[...]), fg_ref[...])

    @pl.when(i < t.n_prompt)
    def _():
        op_ref[...] = r

    @pl.when(i >= t.n_prompt)
    def _():
        os_ref[...] = r


def _out_proj_final(mixed, x, final_g):
    t = FINAL
    return pl.pallas_call(
        _out_final_kernel,
        out_shape=(jax.ShapeDtypeStruct((N_PROMPT_TOK, D_MODEL), F32),
                   jax.ShapeDtypeStruct((N_SAMPLE_TOK, D_MODEL), F32)),
        grid=(t.n_tiles,),
        in_specs=_residual_specs(t, mixed, (x,)) + [_const_spec((1, D_MODEL))],
        out_specs=(t.prompt_rows(D_MODEL), t.sample_rows(D_MODEL)),
        scratch_shapes=[pltpu.VMEM(mixed.w_out.shape[1:], BF16)],
        compiler_params=_params("arbitrary", vmem=VMEM_LIMIT),
        name="out_proj_final",
    )(*_residual_args(mixed, (x,)), final_g)


def _pool_in_kernel(*refs, n_y, n_x):
    t = PROJ
    n_res = n_y + n_x + 2
    res_refs, (g_ref, sh_ref, sc_ref, wf_ref, xo_ref, u_ref, z_ref, wo_ref, w_ref) = refs[:n_res], refs[n_res:]
    x = _residual_update(t, res_refs, n_y, n_x, wo_ref, xo_ref)
    _cast_weight_once(wf_ref, w_ref)
    h = _norm_mod(x, g_ref, sh_ref, sc_ref)

    def store_u(c, a):
        u_ref[:, c * MXU_N:(c + 1) * MXU_N] = a

    def store_z(c, a):
        z_ref[:, c * MXU_N:(c + 1) * MXU_N] = _silu(a).astype(BF16)

    _proj_chunks(h, w_ref, 0, D_MODEL, store_u)
    _proj_chunks(h, w_ref, D_MODEL, D_MODEL, store_z)


def _pool_in(mixed, x_parts, g, mod, w_all, j):
    t = PROJ
    slab = jax.ShapeDtypeStruct((N_TOK, D_MODEL), F32)
    return pl.pallas_call(
        functools.partial(_pool_in_kernel, n_y=len(mixed.y_parts), n_x=len(x_parts)),
        out_shape=(slab, slab, jax.ShapeDtypeStruct((N_TOK, D_MODEL), BF16)),
        grid=(t.n_tiles,),
        in_specs=_residual_specs(t, mixed, x_parts) + [
            _const_spec((1, D_MODEL)), t.mod(0), t.mod(1), _layer_spec(w_all, j)],
        out_specs=(t.rows(D_MODEL),) * 3,
        scratch_shapes=[pltpu.VMEM(mixed.w_out.shape[1:], BF16), pltpu.VMEM((D_MODEL, 2 * D_MODEL), BF16)],
        compiler_params=_params("arbitrary", vmem=FUSED_VMEM_LIMIT),
        name="pool_in",
    )(*_residual_args(mixed, x_parts), g, mod, mod, w_all)


def _split_bf16(a):
    hi = a.astype(BF16)
    return hi, (a - hi.astype(F32)).astype(BF16)


def _band_ones(shape, lo, hi):
    d = lax.broadcasted_iota(jnp.int32, shape, 1) - lax.broadcasted_iota(jnp.int32, shape, 0)
    return jnp.logical_and(d >= lo, d <= hi).astype(F32).astype(BF16)


def _stream_cast_weight(w_hbm, wb_ref, stage_ref, sem):
    width = stage_ref.shape[-1]
    n_chunks = wb_ref.shape[-1] // width

    def copy(c):
        return pltpu.make_async_copy(w_hbm.at[:, pl.ds(c * width, width)], stage_ref.at[c % 2], sem.at[c % 2])

    copy(0).start()
    for c in range(n_chunks):
        if c + 1 < n_chunks:
            copy(c + 1).start()
        copy(c).wait()
        wb_ref[:, c * width:(c + 1) * width] = stage_ref[c % 2].astype(BF16)


def _pool_ret_kernel(u_ref, up_ref, un_ref, z_ref, wgf_ref, ps_ref, wof_ref, x_ref, gate_ref,
                     g_ref, sh_ref, sc_ref, w_hbm,
                     xo_ref, q_ref, kt_ref, v_ref, zr_ref,
                     y_ref, wg_ref, wo_ref, band_ref, wkt_ref, w_ref, stage_ref, sem, *, j_ret):
    t = PROJ
    i = pl.program_id(0)
    _cast_weight_once(wgf_ref, wg_ref)
    _cast_weight_once(wof_ref, wo_ref)

    @pl.when(i == 0)
    def _():
        for g, w in enumerate(POOL_WINDOWS):
            band_ref[g] = _band_ones((POOL_SUB, POOL_SUB), -(w // 2), w - 1 - w // 2)
        _stream_cast_weight(w_hbm.at[j_ret], w_ref, stage_ref, sem)
        for r in range(0, RET_QK_WIDTH, MXU_N):
            wk = w_ref[:, RET_QK_WIDTH + r:RET_QK_WIDTH + r + MXU_N]
            wkt_ref[r:r + MXU_N, :] = wk.astype(F32).T.astype(BF16)

    is_dec = i >= t.n_prompt
    st = t.seq_tile(i)
    seq_len = jnp.where(is_dec, DEC_SEQ, SEQ)
    h = POOL_HALO
    n_sub = t.tm // POOL_SUB
    def pool_steps(sub):
        rows = slice(sub * POOL_SUB, (sub + 1) * POOL_SUB)
        if sub == 0:
            before = jnp.where(jnp.logical_and(is_dec, st != 0), up_ref[...], 0.0)
        else:
            before = jnp.where(is_dec, u_ref[sub * POOL_SUB - h:sub * POOL_SUB, :], 0.0)
        if sub == n_sub - 1:
            after = jnp.where(jnp.logical_and(is_dec, st != t.per_dec_seq - 1), un_ref[...], 0.0)
        else:
            after = jnp.where(is_dec, u_ref[(sub + 1) * POOL_SUB:(sub + 1) * POOL_SUB + h, :], 0.0)
        pos0 = jnp.where(is_dec, st * t.tm + sub * POOL_SUB, 0)
        return _pool_core_steps(u_ref.at[rows], before, after, pos0, seq_len, z_ref.at[rows], ps_ref,
                                wg_ref, band_ref, y_ref.at[rows])

    for step in pool_steps(0):
        step()
    for sub in range(n_sub):
        rows = slice(sub * POOL_SUB, (sub + 1) * POOL_SUB)
        x = x_ref[rows, :] + gate_ref[...] * _dot(y_ref[rows, :], wo_ref[...])
        xo_ref[rows, :] = x
        proj = _ret_in_steps(x, g_ref, sh_ref, sc_ref, w_ref, wkt_ref, q_ref.at[rows],
                             kt_ref.at[pl.ds(sub * POOL_SUB // RET_CHUNK, POOL_SUB // RET_CHUNK)],
                             v_ref.at[rows], zr_ref.at[rows])
        _run_interleaved(proj, pool_steps(sub + 1) if sub + 1 < n_sub else [])


def _pool_core_steps(u_ref, before, after, pos0, seq_len, z_ref, ps_ref, wg_ref, band_ref, y_ref):
    n = u_ref.shape[0]
    h = POOL_HALO
    halo_hi, halo_lo = _split_bf16(jnp.concatenate([before, after], axis=0))
    pos = pos0 + lax.broadcasted_iota(jnp.int32, (n, 1), 0)
    rr = lax.broadcasted_iota(jnp.int32, (2 * h, 2 * h), 0)
    cc = lax.broadcasted_iota(jnp.int32, (2 * h, 2 * h), 1)

    def group(g, w):
        def step():
            left = w // 2
            right = w - 1 - left
            cols = slice(g * POOL_GROUP_DIM, (g + 1) * POOL_GROUP_DIM)
            u = u_ref[:, cols]
            s = _dot(band_ref[g], u.astype(BF16))
            top = jnp.logical_and(jnp.logical_and(rr < h, cc < h), cc - h >= rr - left)
            bot = jnp.logical_and(jnp.logical_and(rr >= h, cc >= h), cc - h <= rr - 2 * h + right)
            edge = jnp.logical_or(top, bot).astype(F32).astype(BF16)
            se = _dot(edge, halo_hi[:, cols]) + _dot(edge, halo_lo[:, cols])
            s = jnp.concatenate([s[:h] + se[:h], s[h:n - h], s[n - h:] + se[h:]], axis=0)
            cnt = (jnp.minimum(pos + right + 1, seq_len) - jnp.maximum(pos - left, 0)).astype(F32)
            d = (s / cnt - u).astype(BF16)
            yg = _dot(d, wg_ref[g]) * ps_ref[:, cols] * z_ref[:, cols].astype(F32)
            y_ref[:, cols] = yg.astype(BF16)
        return step

    return [group(g, w) for g, w in enumerate(POOL_WINDOWS)]


def _pool_ret(u, z, wg_all, ps, wo_all, j, x, mod_pool, g, mod, w_ret_all, j_ret):
    t = PROJ
    per = t.tm // POOL_HALO
    n_halo = N_TOK // POOL_HALO
    n_in = 2 * RET_QK_WIDTH + 2 * RET_V_WIDTH
    kt_per = t.tm // RET_CHUNK
    return pl.pallas_call(
        functools.partial(_pool_ret_kernel, j_ret=j_ret),
        out_shape=(jax.ShapeDtypeStruct((N_TOK, D_MODEL), F32),
                   jax.ShapeDtypeStruct((N_TOK, RET_QK_WIDTH), BF16),
                   jax.ShapeDtypeStruct((N_TOK // RET_CHUNK, RET_QK_WIDTH, RET_CHUNK), BF16),
                   jax.ShapeDtypeStruct((N_TOK, RET_V_WIDTH), BF16),
                   jax.ShapeDtypeStruct((N_TOK, RET_V_WIDTH), BF16)),
        grid=(t.n_tiles,),
        in_specs=[
            t.rows(D_MODEL),
            pl.BlockSpec((POOL_HALO, D_MODEL), lambda i: (jnp.maximum(i * per - 1, 0), 0)),
            pl.BlockSpec((POOL_HALO, D_MODEL), lambda i: (jnp.minimum((i + 1) * per, n_halo - 1), 0)),
            t.rows(D_MODEL),
            _layer_spec(wg_all, j),
            _const_spec((1, D_MODEL)),
            _layer_spec(wo_all, j),
            t.rows(D_MODEL),
            t.mod(2),
            _const_spec((1, D_MODEL)), t.mod(0), t.mod(1), pl.BlockSpec(memory_space=pl.ANY),
        ],
        out_specs=(t.rows(D_MODEL), t.rows(RET_QK_WIDTH),
                   pl.BlockSpec((kt_per, RET_QK_WIDTH, RET_CHUNK), lambda i: (i, 0, 0)),
                   t.rows(RET_V_WIDTH), t.rows(RET_V_WIDTH)),
        scratch_shapes=[pltpu.VMEM((t.tm, D_MODEL), BF16),
                        pltpu.VMEM((len(POOL_WINDOWS), POOL_GROUP_DIM, POOL_GROUP_DIM), BF16),
                        pltpu.VMEM((D_MODEL, D_MODEL), BF16),
                        pltpu.VMEM((len(POOL_WINDOWS), POOL_SUB, POOL_SUB), BF16),
                        pltpu.VMEM((RET_QK_WIDTH, D_MODEL), BF16),
                        pltpu.VMEM((D_MODEL, n_in), BF16),
                        pltpu.VMEM((2, D_MODEL, WEIGHT_STAGE_COLS), F32),
                        pltpu.SemaphoreType.DMA((2,))],
        compiler_params=_params("arbitrary", vmem=FUSED_VMEM_LIMIT),
        name="pool_ret",
    )(u, u, u, z, wg_all, ps, wo_all, x, mod_pool, g, mod, mod, w_ret_all)


def _ret_in_steps(x, g_ref, sh_ref, sc_ref, w_ref, wkt_ref, q_ref, kt_ref, v_ref, z_ref):
    h = _norm_mod(x, g_ref, sh_ref, sc_ref)

    def chunk(ref, lo, c, post=lambda a: a):
        def step():
            cols = slice(c * MXU_N, (c + 1) * MXU_N)
            ref[:, cols] = post(_dot(h, w_ref[:, lo + c * MXU_N:lo + (c + 1) * MXU_N])).astype(BF16)
        return step

    def key_chunk(c):
        def step():
            rows = slice(c * MXU_N, (c + 1) * MXU_N)
            kt = (_dot_nt(wkt_ref[rows, :], h) * RET_KEY_DIM ** -0.5).astype(BF16)
            for cc in range(x.shape[0] // RET_CHUNK):
                kt_ref[cc, rows, :] = kt[:, cc * RET_CHUNK:(cc + 1) * RET_CHUNK]
        return step

    steps = [chunk(q_ref, 0, c) for c in range(RET_QK_WIDTH // MXU_N)]
    steps += [key_chunk(c) for c in range(RET_QK_WIDTH // MXU_N)]
    steps += [chunk(v_ref, 2 * RET_QK_WIDTH, c) for c in range(RET_V_WIDTH // MXU_N)]
    steps += [chunk(z_ref, 2 * RET_QK_WIDTH + RET_V_WIDTH, c, _silu) for c in range(RET_V_WIDTH // MXU_N)]
    return steps


def _run_interleaved(main, side):
    done = 0
    for k, step in enumerate(main):
        step()
        due = (k + 1) * len(side) // len(main)
        for s in side[done:due]:
            s()
        done = due


def _pos(shape, axis):
    return lax.broadcasted_iota(jnp.int32, shape, axis).astype(F32)


def _ret_tables_kernel(lgf_ref, lgb_ref, decay_ref, row_ref, col_ref, cdec_ref):
    h = pl.program_id(0)
    lg_f = lgf_ref[h]
    lg_b = lgb_ref[h]
    c = RET_CHUNK
    diff = _pos((c, c), 0) - _pos((c, c), 1)
    fwd = jnp.where(diff >= 0, jnp.exp(jnp.maximum(diff, 0.0) * lg_f), 0.0)
    bwd = jnp.where(diff <= 0, jnp.exp(jnp.maximum(-diff, 0.0) * lg_b), 0.0)
    decay_ref[...] = fwd + bwd
    j = _pos((RET_TAB_ROWS, c), 1)
    row_ref[0] = jnp.exp((c - 1.0 - j) * lg_f)
    row_ref[1] = jnp.exp(j * lg_b)
    i = _pos((c, LANES), 0)
    col_ref[0] = jnp.exp((i + 1.0) * lg_f)
    col_ref[1] = jnp.exp((c - i) * lg_b)
    full = jnp.full((RET_TAB_ROWS, RET_VAL_DIM), float(c), F32)
    cdec_ref[0] = jnp.exp(full * lg_f)
    cdec_ref[1] = jnp.exp(full * lg_b)


def _ret_tables(lg_f, lg_b):
    smem = pl.BlockSpec(memory_space=pltpu.SMEM)
    c = RET_CHUNK
    shapes = ((c, c), (2, RET_TAB_ROWS, c), (2, c, LANES), (2, RET_TAB_ROWS, RET_VAL_DIM))
    return pl.pallas_call(
        _ret_tables_kernel,
        out_shape=tuple(jax.ShapeDtypeStruct((RET_HEADS,) + s, F32) for s in shapes),
        grid=(RET_HEADS,),
        in_specs=[smem, smem],
        out_specs=tuple(pl.BlockSpec((None,) + s, lambda h, n=len(s): (h,) + (0,) * n) for s in shapes),
        compiler_params=_params("arbitrary"),
        name="ret_tables",
    )(lg_f, lg_b)


def _group_norm_gate(o, gn, z):
    mu = jnp.mean(o, axis=-1, keepdims=True)
    var = jnp.mean(jnp.square(o - mu), axis=-1, keepdims=True)
    on = (o - mu) * lax.rsqrt(var + EPS)
    return (on * gn * z.astype(F32)).astype(BF16)


def _ret_ctx_kernel(q_ref, kt_ref, v_ref, z_ref, gn_ref, decay_ref, row_ref, y_ref, sf_ref, sb_ref):
    for s in range(RET_CTX_SEQS):
        rows = slice(s * SEQ, (s + 1) * SEQ)
        for h in range(RET_HEADS):
            kc = slice(h * RET_KEY_DIM, (h + 1) * RET_KEY_DIM)
            vc = slice(h * RET_VAL_DIM, (h + 1) * RET_VAL_DIM)
            kt = kt_ref[s, kc, :]
            v = v_ref[rows, vc]
            att = (_dot(q_ref[rows, kc], kt) * decay_ref[h]).astype(BF16)
            y_ref[rows, vc] = _group_norm_gate(_dot(att, v), gn_ref[:, vc], z_ref[rows, vc])
            ktf = kt.astype(F32)
            sf_ref[s, h] = _dot((ktf * row_ref[h, 0, 0:1, :]).astype(BF16), v)
            sb_ref[s, h] = _dot((ktf * row_ref[h, 1, 0:1, :]).astype(BF16), v)


def _ret_ctx(q, kt, v, z, gn, decay, row):
    c = RET_CHUNK
    n = RET_CTX_SEQS
    wide = lambda w: pl.BlockSpec((n * SEQ, w), lambda b: (b, 0))
    st_spec = pl.BlockSpec((n, RET_HEADS, RET_KEY_DIM, RET_VAL_DIM), lambda b: (b, 0, 0, 0))
    st_shape = jax.ShapeDtypeStruct((BATCH, RET_HEADS, RET_KEY_DIM, RET_VAL_DIM), F32)
    return pl.pallas_call(
        _ret_ctx_kernel,
        out_shape=(jax.ShapeDtypeStruct((N_PROMPT_TOK, RET_V_WIDTH), BF16), st_shape, st_shape),
        grid=(BATCH // n,),
        in_specs=[wide(RET_QK_WIDTH), pl.BlockSpec((n, RET_QK_WIDTH, c), lambda b: (b, 0, 0)),
                  wide(RET_V_WIDTH), wide(RET_V_WIDTH), _const_spec((1, RET_V_WIDTH)),
                  _const_spec((RET_HEADS, c, c)), _const_spec((RET_HEADS, 2, RET_TAB_ROWS, c))],
        out_specs=(wide(RET_V_WIDTH), st_spec, st_spec),
        compiler_params=_params("arbitrary", vmem=VMEM_LIMIT),
        name="ret_ctx",
    )(q, kt, v, z, gn, decay, row)


def _ret_lat_kernel(q_ref, kt_ref, v_ref, z_ref, gn_ref, decay_ref, row_ref, col_ref, cdec_ref,
                    s0f_ref, s0b_ref, y_ref, sf_all, sb_all, sf_acc, sb_acc):
    c = RET_CHUNK
    n_chunks = DEC_SEQ // c
    rows_of = lambda ci: pl.ds(pl.multiple_of(ci * c, c), c)

    sf_acc[...] = s0f_ref[...]
    sb_acc[...] = s0b_ref[...]

    def scan_step(i, carry):
        cf = i
        cb = n_chunks - 1 - i
        sf_all[cf] = sf_acc[...].astype(BF16)
        sb_all[cb] = sb_acc[...].astype(BF16)
        uf = _dot((kt_ref[cf].astype(F32) * row_ref[0, 0:1, :]).astype(BF16), v_ref[rows_of(cf), :])
        ub = _dot((kt_ref[cb].astype(F32) * row_ref[1, 0:1, :]).astype(BF16), v_ref[rows_of(cb), :])
        sf_acc[...] = sf_acc[...] * cdec_ref[0, 0:1, :] + uf
        sb_acc[...] = sb_acc[...] * cdec_ref[1, 0:1, :] + ub
        return carry

    lax.fori_loop(0, n_chunks, scan_step, 0, unroll=4)

    def out_step(ci, carry):
        rows = rows_of(ci)
        q = q_ref[rows, :]
        qf = q.astype(F32)
        qdec_f = jnp.concatenate([col_ref[0]] * (RET_KEY_DIM // LANES), axis=1)
        qdec_b = jnp.concatenate([col_ref[1]] * (RET_KEY_DIM // LANES), axis=1)
        att = (_dot(q, kt_ref[ci]) * decay_ref[...]).astype(BF16)
        o = (_dot(att, v_ref[rows, :])
             + _dot((qf * qdec_f).astype(BF16), sf_all[ci])
             + _dot((qf * qdec_b).astype(BF16), sb_all[ci]))
        y_ref[rows, :] = _group_norm_gate(o, gn_ref[...], z_ref[rows, :])
        return carry

    lax.fori_loop(0, n_chunks, out_step, 0, unroll=8)


def _ret_lat(q, kt, v, z, gn, decay, row, col, cdec, s0f, s0b):
    c = RET_CHUNK
    n_chunks = DEC_SEQ // c
    row0 = N_PROMPT_TOK // DEC_SEQ
    qk_spec = pl.BlockSpec((DEC_SEQ, RET_KEY_DIM), lambda b, h: (row0 + b, h))
    v_spec = pl.BlockSpec((DEC_SEQ, RET_VAL_DIM), lambda b, h: (row0 + b, h))
    st_spec = pl.BlockSpec((None, None, RET_KEY_DIM, RET_VAL_DIM), lambda b, h: (b, h, 0, 0))
    tab = lambda *s: pl.BlockSpec((None,) + s, lambda b, h: (h,) + (0,) * len(s))
    states = pltpu.VMEM((n_chunks, RET_KEY_DIM, RET_VAL_DIM), BF16)
    acc = pltpu.VMEM((RET_KEY_DIM, RET_VAL_DIM), F32)
    return pl.pallas_call(
        _ret_lat_kernel,
        out_shape=jax.ShapeDtypeStruct((N_SAMPLE_TOK, RET_V_WIDTH), BF16),
        grid=(DEC_BATCH, RET_HEADS),
        in_specs=[qk_spec,
                  pl.BlockSpec((n_chunks, RET_KEY_DIM, c), lambda b, h: (row0 + b, h, 0)),
                  v_spec, v_spec, pl.BlockSpec((1, RET_VAL_DIM), lambda b, h: (0, h)),
                  tab(c, c), tab(2, RET_TAB_ROWS, c), tab(2, c, LANES), tab(2, RET_TAB_ROWS, RET_VAL_DIM),
                  st_spec, st_spec],
        out_specs=pl.BlockSpec((DEC_SEQ, RET_VAL_DIM), lambda b, h: (b, h)),
        scratch_shapes=[states, states, acc, acc],
        compiler_params=_params("arbitrary", "arbitrary", vmem=VMEM_LIMIT),
        name="ret_lat",
    )(q, kt, v, z, gn, decay, row, col, cdec, s0f, s0b)


def _rope_tables(tm):
    n_rows = DEC_SEQ // GRID_W
    rows = jnp.repeat(jnp.arange(n_rows), GRID_W).astype(F32)
    cols = jnp.tile(jnp.arange(GRID_W), n_rows).astype(F32)
    half = ATTN_HEAD_DIM // 4
    inv = ROPE_BASE ** (-jnp.arange(half, dtype=F32) / half)
    ang_r = rows[:, None] * inv[None, :]
    ang_c = cols[:, None] * inv[None, :]
    cos = jnp.concatenate([jnp.cos(ang_r), jnp.cos(ang_r), jnp.cos(ang_c), jnp.cos(ang_c)], axis=-1)
    sin = jnp.concatenate([-jnp.sin(ang_r), jnp.sin(ang_r), -jnp.sin(ang_c), jnp.sin(ang_c)], axis=-1)
    cos = jnp.concatenate([jnp.ones((tm, ATTN_HEAD_DIM), F32), cos], axis=0)
    sin = jnp.concatenate([jnp.zeros((tm, ATTN_HEAD_DIM), F32), sin], axis=0)
    return jnp.tile(cos, (1, 2)), jnp.tile(sin, (1, 2))


def kernel(x_prompt, x_sample, cache_k, cache_v, state_fwd, state_bwd, c, c_ctx, norm_g, ada_w, ada_b, attn_w_in, attn_w_out, attn_sink, pool_w_in, pool_w_grp, pool_scale, pool_w_out, ret_w_in, ret_decay_fwd, ret_decay_bwd, ret_gn_g, ret_w_out, final_g):
    x_parts = (x_prompt.reshape(N_PROMPT_TOK, D_MODEL), x_sample.reshape(N_SAMPLE_TOK, D_MODEL))
    cond = jnp.concatenate([c_ctx[None, :], c,
                            jnp.zeros((N_COND - 1 - DEC_BATCH, D_MODEL), F32)], axis=0)
    mods = _ada_table(cond.T, ada_w, ada_b).reshape(DEPTH, N_COND, 1, 3 * D_MODEL)
    cos_t, sin_t = _rope_tables(PROJ.tm)

    to_kernel = lambda a: jnp.transpose(a, (0, 1, 3, 4, 2))
    from_kernel = lambda a: jnp.transpose(
        a.reshape(a.shape[0], a.shape[1], ATTN_KV_HEADS, ATTN_HEAD_DIM, a.shape[3]), (0, 1, 4, 2, 3))
    ctx_k, ctx_v = to_kernel(cache_k), to_kernel(cache_v)

    assert DEPTH % N_MIXERS == 1, "the layer stack must end on an attention layer"
    caches = ()
    new_sf = new_sb = None
    mixed = None
    for i in range(DEPTH):
        kind, j = i % N_MIXERS, i // N_MIXERS
        g = norm_g[i].reshape(1, D_MODEL)
        mod = mods[i]
        if kind == 0:
            outs = _attn_in(mixed, x_parts, g, mod, attn_w_in, j, cos_t, sin_t, tuple(caches))
            if mixed is not None:
                x_parts, outs = (outs[0],), outs[1:]
            q, k4, v4, z, *caches = outs
            y_parts = (_ctx_attn(attn_sink[j], q, k4, v4, z),
                       _lat_attn(attn_sink[j], q, k4, v4, ctx_k, ctx_v, j, z))
            mixed = Mixed(y_parts, attn_w_out, j, mod)
        elif kind == 1:
            x, u, z = _pool_in(mixed, x_parts, g, mod, pool_w_in, j)
            x_parts, mixed, pooled = (x,), None, (u, z, j, mod)
        else:
            u, z, jp, mod_pool = pooled
            lg_f = jax.nn.log_sigmoid(ret_decay_fwd[j].astype(F32))
            lg_b = jax.nn.log_sigmoid(ret_decay_bwd[j].astype(F32))
            gn = ret_gn_g[j].reshape(1, RET_V_WIDTH)
            x, q, kt, v, z = _pool_ret(u, z, pool_w_grp, pool_scale[jp].reshape(1, D_MODEL), pool_w_out, jp,
                                       x_parts[0], mod_pool, g, mod, ret_w_in, j)
            x_parts = (x,)
            decay, row, col, cdec = _ret_tables(lg_f, lg_b)
            y_ctx, new_sf, new_sb = _ret_ctx(q, kt, v, z, gn, decay, row)
            y_parts = (y_ctx, _ret_lat(q, kt, v, z, gn, decay, row, col, cdec,
                                       state_fwd[:, j], state_bwd[:, j]))
            mixed = Mixed(y_parts, ret_w_out, j, mod)
    y_prompt, y_sample = _out_proj_final(mixed, x_parts[0], final_g.reshape(1, D_MODEL))
    new_k, new_v = caches
    return (y_prompt.reshape(BATCH, SEQ, D_MODEL), y_sample.reshape(DEC_BATCH, DEC_SEQ, D_MODEL),
            from_kernel(new_k), from_kernel(new_v), new_sf[:, None], new_sb[:, None])
```

```python
import functools
from typing import NamedTuple

import jax
import jax.numpy as jnp
from jax import lax
from jax.experimental import pallas as pl
from jax.experimental.pallas import tpu as pltpu

F32 = jnp.float32
BF16 = jnp.bfloat16

D_MODEL = 1024
BATCH = 16
SEQ = 256
DEPTH = 4
DEC_BATCH = 2
DEC_SEQ = 2048
PAST_LEN = 512
GRID_W = 64
N_MIXERS = 3
ATTN_HEADS = 16
ATTN_KV_HEADS = 4
ATTN_HEAD_DIM = 64
ATTN_GROUP = 4
ATTN_WIDTH = 1024
ATTN_KV_WIDTH = 256
WINDOW = 128
ROPE_BASE = 10000.0
POOL_WINDOWS = (2, 4, 8, 16)
POOL_GROUP_DIM = 256
RET_HEADS = 4
RET_KEY_DIM = 256
RET_VAL_DIM = 512
RET_QK_WIDTH = 1024
RET_V_WIDTH = 2048
EPS = 1e-6
NEG_INF = -1e30
LOG2_E = 1.4426950408889634

N_PROMPT_TOK = BATCH * SEQ
N_SAMPLE_TOK = DEC_BATCH * DEC_SEQ
N_TOK = N_PROMPT_TOK + N_SAMPLE_TOK
N_COND = 8
LANES = 128
MXU_N = 256
Q_BLOCK = 128
Q_SUB = 4
RET_CHUNK = 256
RET_TAB_ROWS = 8
RET_CTX_SEQS = 2
POOL_HALO = 8
POOL_SUB = SEQ
WEIGHT_STAGE_COLS = 512
VMEM_LIMIT = 48 * 1024 * 1024
FUSED_VMEM_LIMIT = 58 * 1024 * 1024


class Tiling(NamedTuple):
    tm: int

    @property
    def n_tiles(self):
        return N_TOK // self.tm

    @property
    def n_prompt(self):
        return N_PROMPT_TOK // self.tm

    @property
    def per_dec_seq(self):
        return DEC_SEQ // self.tm

    def cond(self, i):
        return jnp.where(i < self.n_prompt, 0, 1 + (i - self.n_prompt) // self.per_dec_seq)

    def seq_tile(self, i):
        return jnp.where(i < self.n_prompt, 0, (i - self.n_prompt) % self.per_dec_seq)

    def rows(self, width):
        return pl.BlockSpec((self.tm, width), lambda i: (i, 0))

    def prompt_rows(self, width):
        return pl.BlockSpec((self.tm, width), lambda i: (jnp.minimum(i, self.n_prompt - 1), 0))

    def sample_rows(self, width):
        return pl.BlockSpec((self.tm, width), lambda i: (jnp.maximum(i - self.n_prompt, 0), 0))

    def mod(self, part):
        return pl.BlockSpec((None, 1, D_MODEL), lambda i: (self.cond(i), 0, part))


PROJ = Tiling(512)
FINAL = Tiling(1024)


def _silu(z):
    hz = 0.5 * z
    return hz + hz * jnp.tanh(hz)


def _dot(a, b):
    return jnp.dot(a, b, preferred_element_type=F32)


def _dot_nt(a, b):
    return lax.dot_general(a, b, (((1,), (1,)), ((), ())), preferred_element_type=F32)


def _params(*sem, vmem=None):
    return pltpu.CompilerParams(dimension_semantics=sem, vmem_limit_bytes=vmem)


def _const_spec(shape):
    nd = len(shape)
    return pl.BlockSpec(shape, lambda *_: (0,) * nd, pipeline_mode=pl.Buffered(1))


def _part_specs(t, parts, width):
    if len(parts) == 1:
        return [t.rows(width)]
    return [t.prompt_rows(width), t.sample_rows(width)]


def _read_parts(t, refs):
    if len(refs) == 1:
        return refs[0][...]
    return jnp.where(pl.program_id(0) < t.n_prompt, refs[0][...], refs[1][...])


def _ada_kernel(cond_ref, w_ref, b_ref, o_ref):
    s = _silu(cond_ref[...])
    w = w_ref[...]
    rows = [jnp.sum(s[:, c:c + 1] * w, axis=0, keepdims=True) + b_ref[...] for c in range(1 + DEC_BATCH)]
    rows.append(jnp.zeros((N_COND - len(rows), w.shape[1]), F32))
    o_ref[...] = jnp.concatenate(rows, axis=0)


def _ada_table(cond, ada_w, ada_b):
    tn = 3 * D_MODEL // 2
    return pl.pallas_call(
        _ada_kernel,
        out_shape=jax.ShapeDtypeStruct((DEPTH, N_COND, 3 * D_MODEL), F32),
        grid=(DEPTH, 3 * D_MODEL // tn),
        in_specs=[
            pl.BlockSpec((D_MODEL, N_COND), lambda l, n: (0, 0)),
            pl.BlockSpec((None, D_MODEL, tn), lambda l, n: (l, 0, n)),
            pl.BlockSpec((None, 1, tn), lambda l, n: (l, 0, n)),
        ],
        out_specs=pl.BlockSpec((None, N_COND, tn), lambda l, n: (l, 0, n)),
        compiler_params=_params("arbitrary", "arbitrary", vmem=VMEM_LIMIT),
        name="ada_table",
    )(cond, ada_w, ada_b.reshape(DEPTH, 1, 3 * D_MODEL))


def _rms(x, g):
    return x * lax.rsqrt(jnp.mean(x * x, axis=-1, keepdims=True) + EPS) * g


def _norm_mod(x, g_ref, sh_ref, sc_ref):
    return (_rms(x, g_ref[...]) * (1.0 + sc_ref[...]) + sh_ref[...]).astype(BF16)


def _proj_chunks(h, w_ref, lo, width, store):
    for c in range(width // MXU_N):
        store(c, _dot(h, w_ref[:, lo + c * MXU_N:lo + (c + 1) * MXU_N]))


def _layer_spec(w, j):
    nd = w.ndim - 1
    return pl.BlockSpec((None,) + w.shape[1:], lambda *_: (j,) + (0,) * nd, pipeline_mode=pl.Buffered(1))


def _cast_weight_once(w_ref, wb_ref):
    @pl.when(pl.program_id(0) == 0)
    def _():
        rows = wb_ref.shape[-2]
        for r in range(0, rows, MXU_N):
            wb_ref[..., r:r + MXU_N, :] = w_ref[..., r:r + MXU_N, :].astype(BF16)


class Mixed(NamedTuple):
    y_parts: tuple
    w_out: jax.Array
    j: int
    mod: jax.Array


def _residual_specs(t, mixed, x_parts):
    k = mixed.w_out.shape[1]
    return (_part_specs(t, mixed.y_parts, k) + [_layer_spec(mixed.w_out, mixed.j)]
            + _part_specs(t, x_parts, D_MODEL) + [t.mod(2)])


def _residual_args(mixed, x_parts):
    return (*mixed.y_parts, mixed.w_out, *x_parts, mixed.mod)


def _residual_update(t, refs, n_y, n_x, wo_ref, xo_ref):
    y_refs, wof_ref, x_refs, gate_ref = refs[:n_y], refs[n_y], refs[n_y + 1:n_y + 1 + n_x], refs[n_y + 1 + n_x]
    _cast_weight_once(wof_ref, wo_ref)
    x = _read_parts(t, x_refs) + gate_ref[...] * _dot(_read_parts(t, y_refs), wo_ref[...])
    xo_ref[...] = x
    return x


def _rep4(a, h):
    half = a[:, (h // 2) * LANES:(h // 2 + 1) * LANES]
    lane = lax.broadcasted_iota(jnp.int32, half.shape, 1)
    keep = (lane < ATTN_HEAD_DIM) if h % 2 == 0 else (lane >= ATTN_HEAD_DIM)
    m = jnp.where(keep, half, 0.0)
    s = m + pltpu.roll(m, ATTN_HEAD_DIM, 1)
    return jnp.concatenate([s, s], axis=1)


def _attn_in_kernel(*refs, j, n_y, n_x, n_alias):
    t = PROJ
    n_res = n_y + n_x + 2 if n_y else n_x
    res_refs, refs = refs[:n_res], refs[n_res:]
    (g_ref, sh_ref, sc_ref, wf_ref, cos_ref, sin_ref), refs = refs[:6], refs[6 + n_alias:]
    if n_y:
        xo_ref, q_ref, k4_ref, v4_ref, z_ref, kc_ref, vc_ref, wo_ref, w_ref = refs
        x = _residual_update(t, res_refs, n_y, n_x, wo_ref, xo_ref)
    else:
        q_ref, k4_ref, v4_ref, z_ref, kc_ref, vc_ref, w_ref = refs
        x = _read_parts(t, res_refs)
    i = pl.program_id(0)
    _cast_weight_once(wf_ref, w_ref)
    h = _norm_mod(x, g_ref, sh_ref, sc_ref)
    cos = cos_ref[...]
    sin = sin_ref[...]
    lane = lax.broadcasted_iota(jnp.int32, (t.tm, LANES), 1)
    first = (lane % (ATTN_HEAD_DIM // 2)) < ATTN_HEAD_DIM // 4

    def rope(a):
        rot = jnp.where(first, pltpu.roll(a, LANES - ATTN_HEAD_DIM // 4, 1),
                        pltpu.roll(a, ATTN_HEAD_DIM // 4, 1))
        return a * cos + rot * sin

    def rope_wide(a):
        return jnp.concatenate(
            [rope(a[:, s * LANES:(s + 1) * LANES]) for s in range(MXU_N // LANES)], axis=1)

    scale = ATTN_HEAD_DIM ** -0.5 * LOG2_E

    def store_q(c, a):
        q_ref[:, c * MXU_N:(c + 1) * MXU_N] = (rope_wide(a) * scale).astype(BF16)

    def store_z(c, a):
        z_ref[:, c * MXU_N:(c + 1) * MXU_N] = _silu(a).astype(BF16)

    _proj_chunks(h, w_ref, 0, ATTN_WIDTH, store_q)
    k = rope_wide(_dot(h, w_ref[:, ATTN_WIDTH:ATTN_WIDTH + ATTN_KV_WIDTH]))
    v = _dot(h, w_ref[:, ATTN_WIDTH + ATTN_KV_WIDTH:ATTN_WIDTH + 2 * ATTN_KV_WIDTH])
    for hh in range(ATTN_KV_HEADS):
        k4_ref[:, hh * MXU_N:(hh + 1) * MXU_N] = _rep4(k, hh).astype(BF16)
        v4_ref[:, hh * MXU_N:(hh + 1) * MXU_N] = _rep4(v, hh).astype(BF16)
    _proj_chunks(h, w_ref, ATTN_WIDTH + 2 * ATTN_KV_WIDTH, ATTN_WIDTH, store_z)

    @pl.when(i < t.n_prompt)
    def _():
        for s in range(t.tm // SEQ):
            kt = k[s * SEQ:(s + 1) * SEQ, :].T
            vt = v[s * SEQ:(s + 1) * SEQ, :].T
            if n_alias:
                kc_ref[s] = kt
                vc_ref[s] = vt
            else:
                for l in range(kc_ref.shape[1]):
                    kc_ref[s, l] = kt if l == j else jnp.zeros_like(kt)
                    vc_ref[s, l] = vt if l == j else jnp.zeros_like(vt)


def _attn_in(mixed, x_parts, g, mod, w_all, j, cos_t, sin_t, caches):
    t = PROJ
    n_in = 2 * ATTN_WIDTH + 2 * ATTN_KV_WIDTH
    per = t.tm // SEQ
    rope_spec = pl.BlockSpec(
        (t.tm, LANES), lambda i: (jnp.where(i < t.n_prompt, 0, 1 + t.seq_tile(i)), 0))
    wide = jax.ShapeDtypeStruct((N_TOK, ATTN_WIDTH), BF16)
    n_attn = w_all.shape[0]
    cache = jax.ShapeDtypeStruct((BATCH, n_attn, ATTN_KV_WIDTH, SEQ), F32)
    if caches:
        cache_spec = pl.BlockSpec((per, None, ATTN_KV_WIDTH, SEQ),
                                  lambda i: (jnp.minimum(i, t.n_prompt - 1), j, 0, 0))
    else:
        cache_spec = pl.BlockSpec((per, n_attn, ATTN_KV_WIDTH, SEQ),
                                  lambda i: (jnp.minimum(i, t.n_prompt - 1), 0, 0, 0))
    n_x = len(x_parts)
    if mixed is None:
        n_y, res_specs, res_args, res_out, res_ospecs, res_scratch = 0, _part_specs(t, x_parts, D_MODEL), x_parts, (), (), []
    else:
        n_y = len(mixed.y_parts)
        res_specs, res_args = _residual_specs(t, mixed, x_parts), _residual_args(mixed, x_parts)
        res_out = (jax.ShapeDtypeStruct((N_TOK, D_MODEL), F32),)
        res_ospecs = (t.rows(D_MODEL),)
        res_scratch = [pltpu.VMEM(mixed.w_out.shape[1:], BF16)]
    n_front = len(res_specs) + 6
    return pl.pallas_call(
        functools.partial(_attn_in_kernel, j=j, n_y=n_y, n_x=n_x, n_alias=len(caches)),
        out_shape=res_out + (wide, wide, wide, wide, cache, cache),
        grid=(t.n_tiles,),
        in_specs=res_specs + [
            _const_spec((1, D_MODEL)), t.mod(0), t.mod(1), _layer_spec(w_all, j),
            rope_spec, rope_spec] + [pl.BlockSpec(memory_space=pl.ANY)] * len(caches),
        out_specs=res_ospecs + (t.rows(ATTN_WIDTH),) * 4 + (cache_spec,) * 2,
        scratch_shapes=res_scratch + [pltpu.VMEM((D_MODEL, n_in), BF16)],
        input_output_aliases={n_front + c: len(res_out) + 4 + c for c in range(len(caches))},
        compiler_params=_params("arbitrary", vmem=FUSED_VMEM_LIMIT),
        name="attn_in",
    )(*res_args, g, mod, mod, w_all, cos_t, sin_t, *caches)


def _stack_group_queries(q):
    qf = q.astype(F32)
    chunk = lax.broadcasted_iota(jnp.int32, qf.shape, 1) // ATTN_HEAD_DIM
    return jnp.concatenate(
        [jnp.where(chunk == g, qf, 0.0) for g in range(ATTN_GROUP)], axis=0).astype(BF16)


def _gather_group_outputs(o, rows):
    chunk = lax.broadcasted_iota(jnp.int32, (rows, MXU_N), 1) // ATTN_HEAD_DIM
    acc = jnp.zeros((rows, MXU_N), F32)
    for g in range(ATTN_GROUP):
        acc = acc + jnp.where(chunk == g, o[g * rows:(g + 1) * rows], 0.0)
    return acc


def _sink_column(sink_ref, h, rows):
    grp = lax.broadcasted_iota(jnp.int32, (ATTN_GROUP * rows, 1), 0) // rows
    col = jnp.zeros((ATTN_GROUP * rows, 1), F32)
    for g in range(ATTN_GROUP):
        col = jnp.where(grp == g, sink_ref[h * ATTN_GROUP + g] * LOG2_E, col)
    return col


def _chunk_rows(dtype):
    chunk = lax.broadcasted_iota(jnp.int32, (1, MXU_N), 1) // ATTN_HEAD_DIM
    return [(chunk == g).astype(F32).astype(dtype) for g in range(ATTN_GROUP)]


def _block_diag_rows(x4):
    return jnp.concatenate([x4 * m for m in _chunk_rows(x4.dtype)], axis=0)


def _ctx_attn_kernel(sink_ref, q_ref, k4_ref, v4_ref, z_ref, y_ref):
    chunk = lax.broadcasted_iota(jnp.int32, (SEQ, MXU_N), 1) // ATTN_HEAD_DIM
    for h in range(ATTN_KV_HEADS):
        cols = slice(h * MXU_N, (h + 1) * MXU_N)
        s = _dot_nt(q_ref[:, cols], _block_diag_rows(k4_ref[:, cols]))
        inv = jnp.zeros((SEQ, MXU_N), F32)
        probs = []
        for g in range(ATTN_GROUP):
            sg = s[:, g * SEQ:(g + 1) * SEQ]
            sk = sink_ref[h * ATTN_GROUP + g] * LOG2_E
            m = jnp.maximum(jnp.max(sg, axis=1, keepdims=True), sk)
            e = jnp.exp2(sg - m)
            den = jnp.sum(e, axis=1, keepdims=True) + jnp.exp2(sk - m)
            probs.append(e.astype(BF16))
            inv = jnp.where(chunk == g, 1.0 / den, inv)
        o = _dot(jnp.concatenate(probs, axis=1), _block_diag_rows(v4_ref[:, cols]))
        y_ref[:, cols] = (o * inv * z_ref[:, cols].astype(F32)).astype(BF16)


def _ctx_attn(sink, q, k4, v4, z):
    spec = pl.BlockSpec((SEQ, ATTN_WIDTH), lambda b: (b, 0))
    return pl.pallas_call(
        _ctx_attn_kernel,
        out_shape=jax.ShapeDtypeStruct((N_PROMPT_TOK, ATTN_WIDTH), BF16),
        grid=(BATCH,),
        in_specs=[pl.BlockSpec(memory_space=pltpu.SMEM), spec, spec, spec, spec],
        out_specs=spec,
        compiler_params=_params("arbitrary"),
        name="ctx_attn",
    )(sink, q, k4, v4, z)


LAT_STEP = Q_SUB * Q_BLOCK
LAT_PER_SEQ = DEC_SEQ // LAT_STEP
LAT_BLOCKS = DEC_BATCH * ATTN_KV_HEADS * LAT_PER_SEQ


def _lat_block(blk):
    return (blk // (ATTN_KV_HEADS * LAT_PER_SEQ), (blk // LAT_PER_SEQ) % ATTN_KV_HEADS, blk % LAT_PER_SEQ)


def _band_rows(qb):
    return pl.ds(pl.multiple_of(qb * Q_BLOCK, Q_BLOCK), 3 * Q_BLOCK)


def _pad_sequence(dst, src):
    zeros = jnp.zeros((Q_BLOCK, MXU_N), BF16)
    dst[0:Q_BLOCK, :] = zeros
    dst[Q_BLOCK:Q_BLOCK + DEC_SEQ, :] = src[...]
    dst[Q_BLOCK + DEC_SEQ:, :] = zeros


def _lat_attn_kernel(sink_ref, q_ref, k4_ref, v4_ref, kc_ref, vc_ref, z_ref, y_ref,
                     kp, vp, k4c, v4c, sc_a, sl_a, sc_b, sl_b, e_scr):
    t = pl.program_id(0)
    _, _, n1 = _lat_block(jnp.minimum(t, LAT_BLOCKS - 1))
    _, h0, n0 = _lat_block(jnp.maximum(t - 1, 0))
    n_blocks = DEC_SEQ // Q_BLOCK

    @pl.when(t == 0)
    def _():
        sc_b[...] = jnp.zeros(sc_b.shape, F32)
        sl_b[...] = jnp.zeros(sl_b.shape, F32)

    @pl.when(n1 == 0)
    def _():
        _pad_sequence(kp, k4_ref)
        k4c[...] = jnp.concatenate([kc_ref[...]] * ATTN_GROUP, axis=0).astype(BF16)

    @pl.when(n0 == 0)
    def _():
        _pad_sequence(vp, v4_ref)
        v4c[...] = jnp.concatenate([vc_ref[...]] * ATTN_GROUP, axis=0).T.astype(BF16)

    @pl.when(t % 2 == 0)
    def _():
        _lat_stages(sink_ref, q_ref, z_ref, y_ref, kp, vp, k4c, v4c, e_scr, n1, h0, n0,
                    sc_a, sl_a, sc_b, sl_b)

    @pl.when(t % 2 == 1)
    def _():
        _lat_stages(sink_ref, q_ref, z_ref, y_ref, kp, vp, k4c, v4c, e_scr, n1, h0, n0,
                    sc_b, sl_b, sc_a, sl_a)


def _lat_stages(sink_ref, q_ref, z_ref, y_ref, kp, vp, k4c, v4c, e_scr, n1, h0, n0,
                sc_w, sl_w, sc_r, sl_r):
    n_blocks = DEC_SEQ // Q_BLOCK
    for sub in range(Q_SUB):
        qs = _stack_group_queries(q_ref[sub * Q_BLOCK:(sub + 1) * Q_BLOCK, :])
        sc_w[sub] = _dot(qs, k4c[...])
        sl_w[sub] = _dot_nt(qs, kp[_band_rows(n1 * Q_SUB + sub), :])

    rows4 = ATTN_GROUP * Q_BLOCK
    r = lax.broadcasted_iota(jnp.int32, (rows4, Q_BLOCK), 0) % Q_BLOCK
    c = lax.broadcasted_iota(jnp.int32, (rows4, Q_BLOCK), 1)
    in_left = c >= r
    in_right = c <= r
    sk = _sink_column(sink_ref, h0, Q_BLOCK)
    for sub in range(Q_SUB):
        qb = n0 * Q_SUB + sub
        band = _band_rows(qb)
        qrows = slice(sub * Q_BLOCK, (sub + 1) * Q_BLOCK)
        s_ctx = sc_r[sub]
        s_lat = sl_r[sub]
        blocks = [
            s_ctx,
            jnp.where(jnp.logical_and(in_left, qb > 0), s_lat[:, :Q_BLOCK], NEG_INF),
            s_lat[:, Q_BLOCK:2 * Q_BLOCK],
            jnp.where(jnp.logical_and(in_right, qb < n_blocks - 1), s_lat[:, 2 * Q_BLOCK:], NEG_INF),
        ]
        m = sk
        for s in blocks:
            m = jnp.maximum(m, jnp.max(s, axis=1, keepdims=True))
        den = jnp.exp2(sk - m)
        lo = 0
        for s in blocks:
            e = jnp.exp2(s - m)
            den = den + jnp.sum(e, axis=1, keepdims=True)
            e_scr[sub, :, lo:lo + s.shape[1]] = e.astype(BF16)
            lo += s.shape[1]
        o = (_dot(e_scr[sub, :, :PAST_LEN], v4c[...])
             + _dot(e_scr[sub, :, PAST_LEN:], vp[band, :])) * (1.0 / den)
        acc = _gather_group_outputs(o, Q_BLOCK)
        y_ref[qrows, :] = (acc * z_ref[qrows, :].astype(F32)).astype(BF16)


def _lat_attn(sink, q, k4, v4, kc, vc, j, z):
    padded = DEC_SEQ + 2 * Q_BLOCK
    rows4 = ATTN_GROUP * Q_BLOCK
    row0 = N_PROMPT_TOK // LAT_STEP
    seq0 = N_PROMPT_TOK // DEC_SEQ
    stage1 = lambda t: _lat_block(jnp.minimum(t, LAT_BLOCKS - 1))
    stage2 = lambda t: _lat_block(jnp.maximum(t - 1, 0))

    def spec(shape, stage, index):
        return pl.BlockSpec(shape, lambda t: index(*stage(t)))

    tile = (LAT_STEP, MXU_N)
    seq = (DEC_SEQ, MXU_N)
    ctx = (None, None, None, ATTN_HEAD_DIM, PAST_LEN)
    return pl.pallas_call(
        _lat_attn_kernel,
        out_shape=jax.ShapeDtypeStruct((N_SAMPLE_TOK, ATTN_WIDTH), BF16),
        grid=(LAT_BLOCKS + 1,),
        in_specs=[pl.BlockSpec(memory_space=pltpu.SMEM),
                  spec(tile, stage1, lambda b, h, n: (row0 + b * LAT_PER_SEQ + n, h)),
                  spec(seq, stage1, lambda b, h, n: (seq0 + b, h)),
                  spec(seq, stage2, lambda b, h, n: (seq0 + b, h)),
                  spec(ctx, stage1, lambda b, h, n: (b, j, h, 0, 0)),
                  spec(ctx, stage2, lambda b, h, n: (b, j, h, 0, 0)),
                  spec(tile, stage2, lambda b, h, n: (row0 + b * LAT_PER_SEQ + n, h))],
        out_specs=spec(tile, stage2, lambda b, h, n: (b * LAT_PER_SEQ + n, h)),
        scratch_shapes=[pltpu.VMEM((padded, MXU_N), BF16), pltpu.VMEM((padded, MXU_N), BF16),
                        pltpu.VMEM((MXU_N, PAST_LEN), BF16), pltpu.VMEM((PAST_LEN, MXU_N), BF16),
                        pltpu.VMEM((Q_SUB, rows4, PAST_LEN), F32),
                        pltpu.VMEM((Q_SUB, rows4, 3 * Q_BLOCK), F32),
                        pltpu.VMEM((Q_SUB, rows4, PAST_LEN), F32),
                        pltpu.VMEM((Q_SUB, rows4, 3 * Q_BLOCK), F32),
                        pltpu.VMEM((Q_SUB, rows4, PAST_LEN + 3 * Q_BLOCK), BF16)],
        compiler_params=_params("arbitrary", vmem=VMEM_LIMIT),
        name="lat_attn",
    )(sink, q, k4, v4, kc, vc, z)


def _out_final_kernel(yp_ref, ys_ref, wf_ref, x_ref, gate_ref, fg_ref, op_ref, os_ref, w_ref):
    t = FINAL
    i = pl.program_id(0)
    _cast_weight_once(wf_ref, w_ref)
    y = _read_parts(t, (yp_ref, ys_ref))
    r = _rms(x_ref[...] + gate_ref[...] * _dot(y, w_ref[...]), fg_ref[...])

    @pl.when(i < t.n_prompt)
    def _():
        op_ref[...] = r

    @pl.when(i >= t.n_prompt)
    def _():
        os_ref[...] = r


def _out_proj_final(mixed, x, final_g):
    t = FINAL
    return pl.pallas_call(
        _out_final_kernel,
        out_shape=(jax.ShapeDtypeStruct((N_PROMPT_TOK, D_MODEL), F32),
                   jax.ShapeDtypeStruct((N_SAMPLE_TOK, D_MODEL), F32)),
        grid=(t.n_tiles,),
        in_specs=_residual_specs(t, mixed, (x,)) + [_const_spec((1, D_MODEL))],
        out_specs=(t.prompt_rows(D_MODEL), t.sample_rows(D_MODEL)),
        scratch_shapes=[pltpu.VMEM(mixed.w_out.shape[1:], BF16)],
        compiler_params=_params("arbitrary", vmem=VMEM_LIMIT),
        name="out_proj_final",
    )(*_residual_args(mixed, (x,)), final_g)


def _pool_in_kernel(*refs, n_y, n_x):
    t = PROJ
    n_res = n_y + n_x + 2
    res_refs, (g_ref, sh_ref, sc_ref, wf_ref, xo_ref, u_ref, z_ref, wo_ref, w_ref) = refs[:n_res], refs[n_res:]
    x = _residual_update(t, res_refs, n_y, n_x, wo_ref, xo_ref)
    _cast_weight_once(wf_ref, w_ref)
    h = _norm_mod(x, g_ref, sh_ref, sc_ref)

    def store_u(c, a):
        u_ref[:, c * MXU_N:(c + 1) * MXU_N] = a

    def store_z(c, a):
        z_ref[:, c * MXU_N:(c + 1) * MXU_N] = _silu(a).astype(BF16)

    _proj_chunks(h, w_ref, 0, D_MODEL, store_u)
    _proj_chunks(h, w_ref, D_MODEL, D_MODEL, store_z)


def _pool_in(mixed, x_parts, g, mod, w_all, j):
    t = PROJ
    slab = jax.ShapeDtypeStruct((N_TOK, D_MODEL), F32)
    return pl.pallas_call(
        functools.partial(_pool_in_kernel, n_y=len(mixed.y_parts), n_x=len(x_parts)),
        out_shape=(slab, slab, jax.ShapeDtypeStruct((N_TOK, D_MODEL), BF16)),
        grid=(t.n_tiles,),
        in_specs=_residual_specs(t, mixed, x_parts) + [
            _const_spec((1, D_MODEL)), t.mod(0), t.mod(1), _layer_spec(w_all, j)],
        out_specs=(t.rows(D_MODEL),) * 3,
        scratch_shapes=[pltpu.VMEM(mixed.w_out.shape[1:], BF16), pltpu.VMEM((D_MODEL, 2 * D_MODEL), BF16)],
        compiler_params=_params("arbitrary", vmem=FUSED_VMEM_LIMIT),
        name="pool_in",
    )(*_residual_args(mixed, x_parts), g, mod, mod, w_all)


def _split_bf16(a):
    hi = a.astype(BF16)
    return hi, (a - hi.astype(F32)).astype(BF16)


def _band_ones(shape, lo, hi):
    d = lax.broadcasted_iota(jnp.int32, shape, 1) - lax.broadcasted_iota(jnp.int32, shape, 0)
    return jnp.logical_and(d >= lo, d <= hi).astype(F32).astype(BF16)


def _stream_cast_weight(w_hbm, wb_ref, stage_ref, sem):
    width = stage_ref.shape[-1]
    n_chunks = wb_ref.shape[-1] // width

    def copy(c):
        return pltpu.make_async_copy(w_hbm.at[:, pl.ds(c * width, width)], stage_ref.at[c % 2], sem.at[c % 2])

    copy(0).start()
    for c in range(n_chunks):
        if c + 1 < n_chunks:
            copy(c + 1).start()
        copy(c).wait()
        wb_ref[:, c * width:(c + 1) * width] = stage_ref[c % 2].astype(BF16)


def _pool_ret_kernel(u_ref, up_ref, un_ref, z_ref, wgf_ref, ps_ref, wof_ref, x_ref, gate_ref,
                     g_ref, sh_ref, sc_ref, w_hbm,
                     xo_ref, q_ref, kt_ref, v_ref, zr_ref,
                     y_ref, wg_ref, wo_ref, band_ref, wkt_ref, w_ref, stage_ref, sem, h_ref, *, j_ret):
    t = PROJ
    i = pl.program_id(0)
    _cast_weight_once(wgf_ref, wg_ref)
    _cast_weight_once(wof_ref, wo_ref)

    @pl.when(i == 0)
    def _():
        for g, w in enumerate(POOL_WINDOWS):
            band_ref[g] = _band_ones((POOL_SUB, POOL_SUB), -(w // 2), w - 1 - w // 2)
        _stream_cast_weight(w_hbm.at[j_ret], w_ref, stage_ref, sem)
        for r in range(0, RET_QK_WIDTH, MXU_N):
            wk = w_ref[:, RET_QK_WIDTH + r:RET_QK_WIDTH + r + MXU_N]
            wkt_ref[r:r + MXU_N, :] = wk.astype(F32).T.astype(BF16)

    is_dec = i >= t.n_prompt
    st = t.seq_tile(i)
    seq_len = jnp.where(is_dec, DEC_SEQ, SEQ)
    h = POOL_HALO
    n_sub = t.tm // POOL_SUB
    def pool_steps(sub):
        rows = slice(sub * POOL_SUB, (sub + 1) * POOL_SUB)
        if sub == 0:
            before = jnp.where(jnp.logical_and(is_dec, st != 0), up_ref[...], 0.0)
        else:
            before = jnp.where(is_dec, u_ref[sub * POOL_SUB - h:sub * POOL_SUB, :], 0.0)
        if sub == n_sub - 1:
            after = jnp.where(jnp.logical_and(is_dec, st != t.per_dec_seq - 1), un_ref[...], 0.0)
        else:
            after = jnp.where(is_dec, u_ref[(sub + 1) * POOL_SUB:(sub + 1) * POOL_SUB + h, :], 0.0)
        pos0 = jnp.where(is_dec, st * t.tm + sub * POOL_SUB, 0)
        return _pool_core_steps(u_ref.at[rows], before, after, pos0, seq_len, z_ref.at[rows], ps_ref,
                                wg_ref, band_ref, y_ref.at[rows])

    def mix_steps(sub):
        rows = slice(sub * POOL_SUB, (sub + 1) * POOL_SUB)

        def residual():
            xo_ref[rows, :] = x_ref[rows, :] + gate_ref[...] * _dot(y_ref[rows, :], wo_ref[...])

        def normalise():
            h_ref[rows, :] = _norm_mod(xo_ref[rows, :], g_ref, sh_ref, sc_ref)

        return pool_steps(sub) + [residual, normalise]

    for step in mix_steps(0):
        step()
    for sub in range(n_sub):
        rows = slice(sub * POOL_SUB, (sub + 1) * POOL_SUB)
        proj = _ret_in_steps(h_ref.at[rows], w_ref, wkt_ref, q_ref.at[rows],
                             kt_ref.at[pl.ds(sub * POOL_SUB // RET_CHUNK, POOL_SUB // RET_CHUNK)],
                             v_ref.at[rows], zr_ref.at[rows])
        _run_interleaved(proj, mix_steps(sub + 1) if sub + 1 < n_sub else [])


def _pool_core_steps(u_ref, before, after, pos0, seq_len, z_ref, ps_ref, wg_ref, band_ref, y_ref):
    n = u_ref.shape[0]
    h = POOL_HALO
    halo_hi, halo_lo = _split_bf16(jnp.concatenate([before, after], axis=0))
    pos = pos0 + lax.broadcasted_iota(jnp.int32, (n, 1), 0)
    rr = lax.broadcasted_iota(jnp.int32, (2 * h, 2 * h), 0)
    cc = lax.broadcasted_iota(jnp.int32, (2 * h, 2 * h), 1)

    def pooled(g, w):
        def step():
            left = w // 2
            right = w - 1 - left
            cols = slice(g * POOL_GROUP_DIM, (g + 1) * POOL_GROUP_DIM)
            u = u_ref[:, cols]
            s = _dot(band_ref[g], u.astype(BF16))
            top = jnp.logical_and(jnp.logical_and(rr < h, cc < h), cc - h >= rr - left)
            bot = jnp.logical_and(jnp.logical_and(rr >= h, cc >= h), cc - h <= rr - 2 * h + right)
            edge = jnp.logical_or(top, bot).astype(F32).astype(BF16)
            se = _dot(edge, halo_hi[:, cols]) + _dot(edge, halo_lo[:, cols])
            s = jnp.concatenate([s[:h] + se[:h], s[h:n - h], s[n - h:] + se[h:]], axis=0)
            cnt = (jnp.minimum(pos + right + 1, seq_len) - jnp.maximum(pos - left, 0)).astype(F32)
            y_ref[:, cols] = (s / cnt - u).astype(BF16)
        return step

    def mixed(g):
        def step():
            cols = slice(g * POOL_GROUP_DIM, (g + 1) * POOL_GROUP_DIM)
            yg = _dot(y_ref[:, cols], wg_ref[g]) * ps_ref[:, cols] * z_ref[:, cols].astype(F32)
            y_ref[:, cols] = yg.astype(BF16)
        return step

    return ([pooled(g, w) for g, w in enumerate(POOL_WINDOWS)]
            + [mixed(g) for g in range(len(POOL_WINDOWS))])


def _pool_ret(u, z, wg_all, ps, wo_all, j, x, mod_pool, g, mod, w_ret_all, j_ret):
    t = PROJ
    per = t.tm // POOL_HALO
    n_halo = N_TOK // POOL_HALO
    n_in = 2 * RET_QK_WIDTH + 2 * RET_V_WIDTH
    kt_per = t.tm // RET_CHUNK
    return pl.pallas_call(
        functools.partial(_pool_ret_kernel, j_ret=j_ret),
        out_shape=(jax.ShapeDtypeStruct((N_TOK, D_MODEL), F32),
                   jax.ShapeDtypeStruct((N_TOK, RET_QK_WIDTH), BF16),
                   jax.ShapeDtypeStruct((N_TOK // RET_CHUNK, RET_QK_WIDTH, RET_CHUNK), BF16),
                   jax.ShapeDtypeStruct((N_TOK, RET_V_WIDTH), BF16),
                   jax.ShapeDtypeStruct((N_TOK, RET_V_WIDTH), BF16)),
        grid=(t.n_tiles,),
        in_specs=[
            t.rows(D_MODEL),
            pl.BlockSpec((POOL_HALO, D_MODEL), lambda i: (jnp.maximum(i * per - 1, 0), 0)),
            pl.BlockSpec((POOL_HALO, D_MODEL), lambda i: (jnp.minimum((i + 1) * per, n_halo - 1), 0)),
            t.rows(D_MODEL),
            _layer_spec(wg_all, j),
            _const_spec((1, D_MODEL)),
            _layer_spec(wo_all, j),
            t.rows(D_MODEL),
            t.mod(2),
            _const_spec((1, D_MODEL)), t.mod(0), t.mod(1), pl.BlockSpec(memory_space=pl.ANY),
        ],
        out_specs=(t.rows(D_MODEL), t.rows(RET_QK_WIDTH),
                   pl.BlockSpec((kt_per, RET_QK_WIDTH, RET_CHUNK), lambda i: (i, 0, 0)),
                   t.rows(RET_V_WIDTH), t.rows(RET_V_WIDTH)),
        scratch_shapes=[pltpu.VMEM((t.tm, D_MODEL), BF16),
                        pltpu.VMEM((len(POOL_WINDOWS), POOL_GROUP_DIM, POOL_GROUP_DIM), BF16),
                        pltpu.VMEM((D_MODEL, D_MODEL), BF16),
                        pltpu.VMEM((len(POOL_WINDOWS), POOL_SUB, POOL_SUB), BF16),
                        pltpu.VMEM((RET_QK_WIDTH, D_MODEL), BF16),
                        pltpu.VMEM((D_MODEL, n_in), BF16),
                        pltpu.VMEM((2, D_MODEL, WEIGHT_STAGE_COLS), F32),
                        pltpu.SemaphoreType.DMA((2,)),
                        pltpu.VMEM((t.tm, D_MODEL), BF16)],
        compiler_params=_params("arbitrary", vmem=FUSED_VMEM_LIMIT),
        name="pool_ret",
    )(u, u, u, z, wg_all, ps, wo_all, x, mod_pool, g, mod, mod, w_ret_all)


def _ret_in_steps(h_ref, w_ref, wkt_ref, q_ref, kt_ref, v_ref, z_ref):
    n_rows = h_ref.shape[0]

    def chunk(ref, lo, c, post=lambda a: a):
        def step():
            cols = slice(c * MXU_N, (c + 1) * MXU_N)
            ref[:, cols] = post(_dot(h_ref[...], w_ref[:, lo + c * MXU_N:lo + (c + 1) * MXU_N])).astype(BF16)
        return step

    def key_chunk(c):
        def step():
            rows = slice(c * MXU_N, (c + 1) * MXU_N)
            kt = (_dot_nt(wkt_ref[rows, :], h_ref[...]) * RET_KEY_DIM ** -0.5).astype(BF16)
            for cc in range(n_rows // RET_CHUNK):
                kt_ref[cc, rows, :] = kt[:, cc * RET_CHUNK:(cc + 1) * RET_CHUNK]
        return step

    steps = [chunk(q_ref, 0, c) for c in range(RET_QK_WIDTH // MXU_N)]
    steps += [key_chunk(c) for c in range(RET_QK_WIDTH // MXU_N)]
    steps += [chunk(v_ref, 2 * RET_QK_WIDTH, c) for c in range(RET_V_WIDTH // MXU_N)]
    steps += [chunk(z_ref, 2 * RET_QK_WIDTH + RET_V_WIDTH, c, _silu) for c in range(RET_V_WIDTH // MXU_N)]
    return steps


def _run_interleaved(main, side):
    done = 0
    for k, step in enumerate(main):
        step()
        due = (k + 1) * len(side) // len(main)
        for s in side[done:due]:
            s()
        done = due


def _pos(shape, axis):
    return lax.broadcasted_iota(jnp.int32, shape, axis).astype(F32)


def _ret_tables_kernel(lgf_ref, lgb_ref, decay_ref, row_ref, col_ref, cdec_ref):
    h = pl.program_id(0)
    lg_f = lgf_ref[h]
    lg_b = lgb_ref[h]
    c = RET_CHUNK
    diff = _pos((c, c), 0) - _pos((c, c), 1)
    fwd = jnp.where(diff >= 0, jnp.exp(jnp.maximum(diff, 0.0) * lg_f), 0.0)
    bwd = jnp.where(diff <= 0, jnp.exp(jnp.maximum(-diff, 0.0) * lg_b), 0.0)
    decay_ref[...] = fwd + bwd
    j = _pos((RET_TAB_ROWS, c), 1)
    row_ref[0] = jnp.exp((c - 1.0 - j) * lg_f)
    row_ref[1] = jnp.exp(j * lg_b)
    i = _pos((c, LANES), 0)
    col_ref[0] = jnp.exp((i + 1.0) * lg_f)
    col_ref[1] = jnp.exp((c - i) * lg_b)
    full = jnp.full((RET_TAB_ROWS, RET_VAL_DIM), float(c), F32)
    cdec_ref[0] = jnp.exp(full * lg_f)
    cdec_ref[1] = jnp.exp(full * lg_b)


def _ret_tables(lg_f, lg_b):
    smem = pl.BlockSpec(memory_space=pltpu.SMEM)
    c = RET_CHUNK
    shapes = ((c, c), (2, RET_TAB_ROWS, c), (2, c, LANES), (2, RET_TAB_ROWS, RET_VAL_DIM))
    return pl.pallas_call(
        _ret_tables_kernel,
        out_shape=tuple(jax.ShapeDtypeStruct((RET_HEADS,) + s, F32) for s in shapes),
        grid=(RET_HEADS,),
        in_specs=[smem, smem],
        out_specs=tuple(pl.BlockSpec((None,) + s, lambda h, n=len(s): (h,) + (0,) * n) for s in shapes),
        compiler_params=_params("arbitrary"),
        name="ret_tables",
    )(lg_f, lg_b)


def _group_norm_gate(o, gn, z):
    mu = jnp.mean(o, axis=-1, keepdims=True)
    var = jnp.mean(jnp.square(o - mu), axis=-1, keepdims=True)
    on = (o - mu) * lax.rsqrt(var + EPS)
    return (on * gn * z.astype(F32)).astype(BF16)


def _ret_ctx_kernel(q_ref, kt_ref, v_ref, z_ref, gn_ref, decay_ref, row_ref, y_ref, sf_ref, sb_ref):
    for s in range(RET_CTX_SEQS):
        rows = slice(s * SEQ, (s + 1) * SEQ)
        for h in range(RET_HEADS):
            kc = slice(h * RET_KEY_DIM, (h + 1) * RET_KEY_DIM)
            vc = slice(h * RET_VAL_DIM, (h + 1) * RET_VAL_DIM)
            kt = kt_ref[s, kc, :]
            v = v_ref[rows, vc]
            att = (_dot(q_ref[rows, kc], kt) * decay_ref[h]).astype(BF16)
            y_ref[rows, vc] = _group_norm_gate(_dot(att, v), gn_ref[:, vc], z_ref[rows, vc])
            ktf = kt.astype(F32)
            sf_ref[s, h] = _dot((ktf * row_ref[h, 0, 0:1, :]).astype(BF16), v)
            sb_ref[s, h] = _dot((ktf * row_ref[h, 1, 0:1, :]).astype(BF16), v)


def _ret_ctx(q, kt, v, z, gn, decay, row):
    c = RET_CHUNK
    n = RET_CTX_SEQS
    wide = lambda w: pl.BlockSpec((n * SEQ, w), lambda b: (b, 0))
    st_spec = pl.BlockSpec((n, RET_HEADS, RET_KEY_DIM, RET_VAL_DIM), lambda b: (b, 0, 0, 0))
    st_shape = jax.ShapeDtypeStruct((BATCH, RET_HEADS, RET_KEY_DIM, RET_VAL_DIM), F32)
    return pl.pallas_call(
        _ret_ctx_kernel,
        out_shape=(jax.ShapeDtypeStruct((N_PROMPT_TOK, RET_V_WIDTH), BF16), st_shape, st_shape),
        grid=(BATCH // n,),
        in_specs=[wide(RET_QK_WIDTH), pl.BlockSpec((n, RET_QK_WIDTH, c), lambda b: (b, 0, 0)),
                  wide(RET_V_WIDTH), wide(RET_V_WIDTH), _const_spec((1, RET_V_WIDTH)),
                  _const_spec((RET_HEADS, c, c)), _const_spec((RET_HEADS, 2, RET_TAB_ROWS, c))],
        out_specs=(wide(RET_V_WIDTH), st_spec, st_spec),
        compiler_params=_params("arbitrary", vmem=VMEM_LIMIT),
        name="ret_ctx",
    )(q, kt, v, z, gn, decay, row)


def _ret_lat_kernel(q_ref, kt_ref, v_ref, z_ref, gn_ref, decay_ref, row_ref, col_ref, cdec_ref,
                    s0f_ref, s0b_ref, y_ref, sf_all, sb_all, sf_acc, sb_acc):
    c = RET_CHUNK
    n_chunks = DEC_SEQ // c
    rows_of = lambda ci: pl.ds(pl.multiple_of(ci * c, c), c)

    sf_acc[...] = s0f_ref[...]
    sb_acc[...] = s0b_ref[...]

    def scan_step(i, carry):
        cf = i
        cb = n_chunks - 1 - i
        sf_all[cf] = sf_acc[...].astype(BF16)
        sb_all[cb] = sb_acc[...].astype(BF16)
        uf = _dot((kt_ref[cf].astype(F32) * row_ref[0, 0:1, :]).astype(BF16), v_ref[rows_of(cf), :])
        ub = _dot((kt_ref[cb].astype(F32) * row_ref[1, 0:1, :]).astype(BF16), v_ref[rows_of(cb), :])
        sf_acc[...] = sf_acc[...] * cdec_ref[0, 0:1, :] + uf
        sb_acc[...] = sb_acc[...] * cdec_ref[1, 0:1, :] + ub
        return carry

    lax.fori_loop(0, n_chunks, scan_step, 0, unroll=4)

    def out_step(ci, carry):
        rows = rows_of(ci)
        q = q_ref[rows, :]
        qf = q.astype(F32)
        qdec_f = jnp.concatenate([col_ref[0]] * (RET_KEY_DIM // LANES), axis=1)
        qdec_b = jnp.concatenate([col_ref[1]] * (RET_KEY_DIM // LANES), axis=1)
        att = (_dot(q, kt_ref[ci]) * decay_ref[...]).astype(BF16)
        o = (_dot(att, v_ref[rows, :])
             + _dot((qf * qdec_f).astype(BF16), sf_all[ci])
             + _dot((qf * qdec_b).astype(BF16), sb_all[ci]))
        y_ref[rows, :] = _group_norm_gate(o, gn_ref[...], z_ref[rows, :])
        return carry

    lax.fori_loop(0, n_chunks, out_step, 0, unroll=8)


def _ret_lat(q, kt, v, z, gn, decay, row, col, cdec, s0f, s0b):
    c = RET_CHUNK
    n_chunks = DEC_SEQ // c
    row0 = N_PROMPT_TOK // DEC_SEQ
    qk_spec = pl.BlockSpec((DEC_SEQ, RET_KEY_DIM), lambda b, h: (row0 + b, h))
    v_spec = pl.BlockSpec((DEC_SEQ, RET_VAL_DIM), lambda b, h: (row0 + b, h))
    st_spec = pl.BlockSpec((None, None, RET_KEY_DIM, RET_VAL_DIM), lambda b, h: (b, h, 0, 0))
    tab = lambda *s: pl.BlockSpec((None,) + s, lambda b, h: (h,) + (0,) * len(s))
    states = pltpu.VMEM((n_chunks, RET_KEY_DIM, RET_VAL_DIM), BF16)
    acc = pltpu.VMEM((RET_KEY_DIM, RET_VAL_DIM), F32)
    return pl.pallas_call(
        _ret_lat_kernel,
        out_shape=jax.ShapeDtypeStruct((N_SAMPLE_TOK, RET_V_WIDTH), BF16),
        grid=(DEC_BATCH, RET_HEADS),
        in_specs=[qk_spec,
                  pl.BlockSpec((n_chunks, RET_KEY_DIM, c), lambda b, h: (row0 + b, h, 0)),
                  v_spec, v_spec, pl.BlockSpec((1, RET_VAL_DIM), lambda b, h: (0, h)),
                  tab(c, c), tab(2, RET_TAB_ROWS, c), tab(2, c, LANES), tab(2, RET_TAB_ROWS, RET_VAL_DIM),
                  st_spec, st_spec],
        out_specs=pl.BlockSpec((DEC_SEQ, RET_VAL_DIM), lambda b, h: (b, h)),
        scratch_shapes=[states, states, acc, acc],
        compiler_params=_params("arbitrary", "arbitrary", vmem=VMEM_LIMIT),
        name="ret_lat",
    )(q, kt, v, z, gn, decay, row, col, cdec, s0f, s0b)


def _rope_tables(tm):
    n_rows = DEC_SEQ // GRID_W
    rows = jnp.repeat(jnp.arange(n_rows), GRID_W).astype(F32)
    cols = jnp.tile(jnp.arange(GRID_W), n_rows).astype(F32)
    half = ATTN_HEAD_DIM // 4
    inv = ROPE_BASE ** (-jnp.arange(half, dtype=F32) / half)
    ang_r = rows[:, None] * inv[None, :]
    ang_c = cols[:, None] * inv[None, :]
    cos = jnp.concatenate([jnp.cos(ang_r), jnp.cos(ang_r), jnp.cos(ang_c), jnp.cos(ang_c)], axis=-1)
    sin = jnp.concatenate([-jnp.sin(ang_r), jnp.sin(ang_r), -jnp.sin(ang_c), jnp.sin(ang_c)], axis=-1)
    cos = jnp.concatenate([jnp.ones((tm, ATTN_HEAD_DIM), F32), cos], axis=0)
    sin = jnp.concatenate([jnp.zeros((tm, ATTN_HEAD_DIM), F32), sin], axis=0)
    return jnp.tile(cos, (1, 2)), jnp.tile(sin, (1, 2))


def kernel(x_prompt, x_sample, cache_k, cache_v, state_fwd, state_bwd, c, c_ctx, norm_g, ada_w, ada_b, attn_w_in, attn_w_out, attn_sink, pool_w_in, pool_w_grp, pool_scale, pool_w_out, ret_w_in, ret_decay_fwd, ret_decay_bwd, ret_gn_g, ret_w_out, final_g):
    x_parts = (x_prompt.reshape(N_PROMPT_TOK, D_MODEL), x_sample.reshape(N_SAMPLE_TOK, D_MODEL))
    cond = jnp.concatenate([c_ctx[None, :], c,
                            jnp.zeros((N_COND - 1 - DEC_BATCH, D_MODEL), F32)], axis=0)
    mods = _ada_table(cond.T, ada_w, ada_b).reshape(DEPTH, N_COND, 1, 3 * D_MODEL)
    cos_t, sin_t = _rope_tables(PROJ.tm)

    to_kernel = lambda a: jnp.transpose(a, (0, 1, 3, 4, 2))
    from_kernel = lambda a: jnp.transpose(
        a.reshape(a.shape[0], a.shape[1], ATTN_KV_HEADS, ATTN_HEAD_DIM, a.shape[3]), (0, 1, 4, 2, 3))
    ctx_k, ctx_v = to_kernel(cache_k), to_kernel(cache_v)

    assert DEPTH % N_MIXERS == 1, "the layer stack must end on an attention layer"
    caches = ()
    new_sf = new_sb = None
    mixed = None
    for i in range(DEPTH):
        kind, j = i % N_MIXERS, i // N_MIXERS
        g = norm_g[i].reshape(1, D_MODEL)
        mod = mods[i]
        if kind == 0:
            outs = _attn_in(mixed, x_parts, g, mod, attn_w_in, j, cos_t, sin_t, tuple(caches))
            if mixed is not None:
                x_parts, outs = (outs[0],), outs[1:]
            q, k4, v4, z, *caches = outs
            y_parts = (_ctx_attn(attn_sink[j], q, k4, v4, z),
                       _lat_attn(attn_sink[j], q, k4, v4, ctx_k, ctx_v, j, z))
            mixed = Mixed(y_parts, attn_w_out, j, mod)
        elif kind == 1:
            x, u, z = _pool_in(mixed, x_parts, g, mod, pool_w_in, j)
            x_parts, mixed, pooled = (x,), None, (u, z, j, mod)
        else:
            u, z, jp, mod_pool = pooled
            lg_f = jax.nn.log_sigmoid(ret_decay_fwd[j].astype(F32))
            lg_b = jax.nn.log_sigmoid(ret_decay_bwd[j].astype(F32))
            gn = ret_gn_g[j].reshape(1, RET_V_WIDTH)
            x, q, kt, v, z = _pool_ret(u, z, pool_w_grp, pool_scale[jp].reshape(1, D_MODEL), pool_w_out, jp,
                                       x_parts[0], mod_pool, g, mod, ret_w_in, j)
            x_parts = (x,)
            decay, row, col, cdec = _ret_tables(lg_f, lg_b)
            y_ctx, new_sf, new_sb = _ret_ctx(q, kt, v, z, gn, decay, row)
            y_parts = (y_ctx, _ret_lat(q, kt, v, z, gn, decay, row, col, cdec,
                                       state_fwd[:, j], state_bwd[:, j]))
            mixed = Mixed(y_parts, ret_w_out, j, mod)
    y_prompt, y_sample = _out_proj_final(mixed, x_parts[0], final_g.reshape(1, D_MODEL))
    new_k, new_v = caches
    return (y_prompt.reshape(BATCH, SEQ, D_MODEL), y_sample.reshape(DEC_BATCH, DEC_SEQ, D_MODEL),
            from_kernel(new_k), from_kernel(new_v), new_sf[:, None], new_sb[:, None])
```

```python
import functools
from typing import NamedTuple

import jax
import jax.numpy as jnp
from jax import lax
from jax.experimental import pallas as pl
from jax.experimental.pallas import tpu as pltpu

F32 = jnp.float32
BF16 = jnp.bfloat16

D_MODEL = 1024
BATCH = 16
SEQ = 256
DEPTH = 4
DEC_BATCH = 2
DEC_SEQ = 2048
PAST_LEN = 512
GRID_W = 64
N_MIXERS = 3
ATTN_HEADS = 16
ATTN_KV_HEADS = 4
ATTN_HEAD_DIM = 64
ATTN_GROUP = 4
ATTN_WIDTH = 1024
ATTN_KV_WIDTH = 256
WINDOW = 128
ROPE_BASE = 10000.0
POOL_WINDOWS = (2, 4, 8, 16)
POOL_GROUP_DIM = 256
RET_HEADS = 4
RET_KEY_DIM = 256
RET_VAL_DIM = 512
RET_QK_WIDTH = 1024
RET_V_WIDTH = 2048
EPS = 1e-6
NEG_INF = -1e30
LOG2_E = 1.4426950408889634

N_PROMPT_TOK = BATCH * SEQ
N_SAMPLE_TOK = DEC_BATCH * DEC_SEQ
N_TOK = N_PROMPT_TOK + N_SAMPLE_TOK
N_COND = 8
LANES = 128
MXU_N = 256
Q_BLOCK = 128
Q_SUB = 4
RET_CHUNK = 256
RET_TAB_ROWS = 8
RET_CTX_SEQS = 2
POOL_HALO = 8
POOL_SUB = SEQ
PROJ_SUB = 256
WEIGHT_STAGE_COLS = 512
VMEM_LIMIT = 48 * 1024 * 1024
FUSED_VMEM_LIMIT = 58 * 1024 * 1024


class Tiling(NamedTuple):
    tm: int

    @property
    def n_tiles(self):
        return N_TOK // self.tm

    @property
    def n_prompt(self):
        return N_PROMPT_TOK // self.tm

    @property
    def per_dec_seq(self):
        return DEC_SEQ // self.tm

    def cond(self, i):
        return jnp.where(i < self.n_prompt, 0, 1 + (i - self.n_prompt) // self.per_dec_seq)

    def seq_tile(self, i):
        return jnp.where(i < self.n_prompt, 0, (i - self.n_prompt) % self.per_dec_seq)

    def rows(self, width):
        return pl.BlockSpec((self.tm, width), lambda i: (i, 0))

    def prompt_rows(self, width):
        return pl.BlockSpec((self.tm, width), lambda i: (jnp.minimum(i, self.n_prompt - 1), 0))

    def sample_rows(self, width):
        return pl.BlockSpec((self.tm, width), lambda i: (jnp.maximum(i - self.n_prompt, 0), 0))

    def mod(self, part):
        return pl.BlockSpec((None, 1, D_MODEL), lambda i: (self.cond(i), 0, part))


PROJ = Tiling(512)
FINAL = Tiling(1024)


def _silu(z):
    hz = 0.5 * z
    return hz + hz * jnp.tanh(hz)


def _dot(a, b):
    return jnp.dot(a, b, preferred_element_type=F32)


def _dot_nt(a, b):
    return lax.dot_general(a, b, (((1,), (1,)), ((), ())), preferred_element_type=F32)


def _params(*sem, vmem=None):
    return pltpu.CompilerParams(dimension_semantics=sem, vmem_limit_bytes=vmem)


def _const_spec(shape):
    nd = len(shape)
    return pl.BlockSpec(shape, lambda *_: (0,) * nd, pipeline_mode=pl.Buffered(1))


def _part_specs(t, parts, width):
    if len(parts) == 1:
        return [t.rows(width)]
    return [t.prompt_rows(width), t.sample_rows(width)]


def _read_parts(t, refs, rows=slice(None)):
    if len(refs) == 1:
        return refs[0][rows, :]
    return jnp.where(pl.program_id(0) < t.n_prompt, refs[0][rows, :], refs[1][rows, :])


def _ada_kernel(cond_ref, w_ref, b_ref, o_ref):
    s = _silu(cond_ref[...])
    w = w_ref[...]
    rows = [jnp.sum(s[:, c:c + 1] * w, axis=0, keepdims=True) + b_ref[...] for c in range(1 + DEC_BATCH)]
    rows.append(jnp.zeros((N_COND - len(rows), w.shape[1]), F32))
    o_ref[...] = jnp.concatenate(rows, axis=0)


def _ada_table(cond, ada_w, ada_b):
    tn = 3 * D_MODEL // 2
    return pl.pallas_call(
        _ada_kernel,
        out_shape=jax.ShapeDtypeStruct((DEPTH, N_COND, 3 * D_MODEL), F32),
        grid=(DEPTH, 3 * D_MODEL // tn),
        in_specs=[
            pl.BlockSpec((D_MODEL, N_COND), lambda l, n: (0, 0)),
            pl.BlockSpec((None, D_MODEL, tn), lambda l, n: (l, 0, n)),
            pl.BlockSpec((None, 1, tn), lambda l, n: (l, 0, n)),
        ],
        out_specs=pl.BlockSpec((None, N_COND, tn), lambda l, n: (l, 0, n)),
        compiler_params=_params("arbitrary", "arbitrary", vmem=VMEM_LIMIT),
        name="ada_table",
    )(cond, ada_w, ada_b.reshape(DEPTH, 1, 3 * D_MODEL))


def _rms(x, g):
    return x * lax.rsqrt(jnp.mean(x * x, axis=-1, keepdims=True) + EPS) * g


def _norm_mod(x, g_ref, sh_ref, sc_ref):
    return (_rms(x, g_ref[...]) * (1.0 + sc_ref[...]) + sh_ref[...]).astype(BF16)


def _layer_spec(w, j):
    nd = w.ndim - 1
    return pl.BlockSpec((None,) + w.shape[1:], lambda *_: (j,) + (0,) * nd, pipeline_mode=pl.Buffered(1))


def _cast_weight_once(w_ref, wb_ref):
    @pl.when(pl.program_id(0) == 0)
    def _():
        rows = wb_ref.shape[-2]
        for r in range(0, rows, MXU_N):
            wb_ref[..., r:r + MXU_N, :] = w_ref[..., r:r + MXU_N, :].astype(BF16)


class Mixed(NamedTuple):
    y_parts: tuple
    w_out: jax.Array
    j: int
    mod: jax.Array


def _residual_specs(t, mixed, x_parts):
    k = mixed.w_out.shape[1]
    return (_part_specs(t, mixed.y_parts, k) + [_layer_spec(mixed.w_out, mixed.j)]
            + _part_specs(t, x_parts, D_MODEL) + [t.mod(2)])


def _residual_args(mixed, x_parts):
    return (*mixed.y_parts, mixed.w_out, *x_parts, mixed.mod)


def _run_halves(t, x_refs, res, g_ref, sh_ref, sc_ref, xo_ref, wo_ref, h_ref, proj_steps):
    n_sub = t.tm // PROJ_SUB
    rows_of = lambda sub: slice(sub * PROJ_SUB, (sub + 1) * PROJ_SUB)

    if res is not None:
        y_refs, gate_ref = res
        for sub in range(n_sub):
            rows = rows_of(sub)
            xo_ref[rows, :] = (_read_parts(t, x_refs, rows)
                               + gate_ref[...] * _dot(_read_parts(t, y_refs, rows), wo_ref[...]))

    def normalise(sub):
        rows = rows_of(sub)
        x = xo_ref[rows, :] if res is not None else _read_parts(t, x_refs, rows)
        h_ref[rows, :] = _norm_mod(x, g_ref, sh_ref, sc_ref)

    normalise(0)
    for sub in range(n_sub):
        side = [functools.partial(normalise, sub + 1)] if sub + 1 < n_sub else []
        _run_interleaved(proj_steps(sub), side)


def _rep4(a, h):
    half = a[:, (h // 2) * LANES:(h // 2 + 1) * LANES]
    lane = lax.broadcasted_iota(jnp.int32, half.shape, 1)
    keep = (lane < ATTN_HEAD_DIM) if h % 2 == 0 else (lane >= ATTN_HEAD_DIM)
    m = jnp.where(keep, half, 0.0)
    s = m + pltpu.roll(m, ATTN_HEAD_DIM, 1)
    return jnp.concatenate([s, s], axis=1)


def _attn_in_kernel(*refs, j, n_y, n_x, n_alias):
    t = PROJ
    n_res = n_y + n_x + 2 if n_y else n_x
    res_refs, refs = refs[:n_res], refs[n_res:]
    (g_ref, sh_ref, sc_ref, wf_ref, cos_ref, sin_ref), refs = refs[:6], refs[6 + n_alias:]
    if n_y:
        xo_ref, q_ref, k4_ref, v4_ref, z_ref, kc_ref, vc_ref, wo_ref, w_ref, h_ref, kv_ref = refs
        y_refs, wof_ref, x_refs, gate_ref = (res_refs[:n_y], res_refs[n_y], res_refs[n_y + 1:n_y + 1 + n_x],
                                             res_refs[n_y + 1 + n_x])
        _cast_weight_once(wof_ref, wo_ref)
        res = (y_refs, gate_ref)
    else:
        q_ref, k4_ref, v4_ref, z_ref, kc_ref, vc_ref, w_ref, h_ref, kv_ref = refs
        x_refs, res, xo_ref, wo_ref = res_refs, None, None, None
    i = pl.program_id(0)
    _cast_weight_once(wf_ref, w_ref)
    lane = lax.broadcasted_iota(jnp.int32, (PROJ_SUB, LANES), 1)
    first = (lane % (ATTN_HEAD_DIM // 2)) < ATTN_HEAD_DIM // 4
    scale = ATTN_HEAD_DIM ** -0.5 * LOG2_E
    k_lo = ATTN_WIDTH
    v_lo = ATTN_WIDTH + ATTN_KV_WIDTH
    z_lo = ATTN_WIDTH + 2 * ATTN_KV_WIDTH

    def proj_steps(sub):
        rows = slice(sub * PROJ_SUB, (sub + 1) * PROJ_SUB)

        def rope(a):
            rot = jnp.where(first, pltpu.roll(a, LANES - ATTN_HEAD_DIM // 4, 1),
                            pltpu.roll(a, ATTN_HEAD_DIM // 4, 1))
            return a * cos_ref[rows, :] + rot * sin_ref[rows, :]

        def rope_wide(a):
            return jnp.concatenate(
                [rope(a[:, s * LANES:(s + 1) * LANES]) for s in range(MXU_N // LANES)], axis=1)

        def proj(lo):
            return _dot(h_ref[rows, :], w_ref[:, lo:lo + MXU_N])

        def q_chunk(c):
            cols = slice(c * MXU_N, (c + 1) * MXU_N)
            q_ref[rows, cols] = (rope_wide(proj(c * MXU_N)) * scale).astype(BF16)

        def keys():
            k = rope_wide(proj(k_lo))
            kv_ref[rows, :ATTN_KV_WIDTH] = k
            for hh in range(ATTN_KV_HEADS):
                k4_ref[rows, hh * MXU_N:(hh + 1) * MXU_N] = _rep4(k, hh).astype(BF16)

        def values():
            v = proj(v_lo)
            kv_ref[rows, ATTN_KV_WIDTH:] = v
            for hh in range(ATTN_KV_HEADS):
                v4_ref[rows, hh * MXU_N:(hh + 1) * MXU_N] = _rep4(v, hh).astype(BF16)

        def z_chunk(c):
            cols = slice(c * MXU_N, (c + 1) * MXU_N)
            z_ref[rows, cols] = _silu(proj(z_lo + c * MXU_N)).astype(BF16)

        n_chunks = ATTN_WIDTH // MXU_N
        return ([functools.partial(q_chunk, c) for c in range(n_chunks)] + [keys, values]
                + [functools.partial(z_chunk, c) for c in range(n_chunks)])

    _run_halves(t, x_refs, res, g_ref, sh_ref, sc_ref, xo_ref, wo_ref, h_ref, proj_steps)

    @pl.when(i < t.n_prompt)
    def _():
        for s in range(t.tm // SEQ):
            kt = kv_ref[s * SEQ:(s + 1) * SEQ, :ATTN_KV_WIDTH].T
            vt = kv_ref[s * SEQ:(s + 1) * SEQ, ATTN_KV_WIDTH:].T
            if n_alias:
                kc_ref[s] = kt
                vc_ref[s] = vt
            else:
                for l in range(kc_ref.shape[1]):
                    kc_ref[s, l] = kt if l == j else jnp.zeros_like(kt)
                    vc_ref[s, l] = vt if l == j else jnp.zeros_like(vt)


def _attn_in(mixed, x_parts, g, mod, w_all, j, cos_t, sin_t, caches):
    t = PROJ
    n_in = 2 * ATTN_WIDTH + 2 * ATTN_KV_WIDTH
    per = t.tm // SEQ
    rope_spec = pl.BlockSpec(
        (t.tm, LANES), lambda i: (jnp.where(i < t.n_prompt, 0, 1 + t.seq_tile(i)), 0))
    wide = jax.ShapeDtypeStruct((N_TOK, ATTN_WIDTH), BF16)
    n_attn = w_all.shape[0]
    cache = jax.ShapeDtypeStruct((BATCH, n_attn, ATTN_KV_WIDTH, SEQ), F32)
    if caches:
        cache_spec = pl.BlockSpec((per, None, ATTN_KV_WIDTH, SEQ),
                                  lambda i: (jnp.minimum(i, t.n_prompt - 1), j, 0, 0))
    else:
        cache_spec = pl.BlockSpec((per, n_attn, ATTN_KV_WIDTH, SEQ),
                                  lambda i: (jnp.minimum(i, t.n_prompt - 1), 0, 0, 0))
    n_x = len(x_parts)
    if mixed is None:
        n_y, res_specs, res_args, res_out, res_ospecs, res_scratch = 0, _part_specs(t, x_parts, D_MODEL), x_parts, (), (), []
    else:
        n_y = len(mixed.y_parts)
        res_specs, res_args = _residual_specs(t, mixed, x_parts), _residual_args(mixed, x_parts)
        res_out = (jax.ShapeDtypeStruct((N_TOK, D_MODEL), F32),)
        res_ospecs = (t.rows(D_MODEL),)
        res_scratch = [pltpu.VMEM(mixed.w_out.shape[1:], BF16)]
    n_front = len(res_specs) + 6
    return pl.pallas_call(
        functools.partial(_attn_in_kernel, j=j, n_y=n_y, n_x=n_x, n_alias=len(caches)),
        out_shape=res_out + (wide, wide, wide, wide, cache, cache),
        grid=(t.n_tiles,),
        in_specs=res_specs + [
            _const_spec((1, D_MODEL)), t.mod(0), t.mod(1), _layer_spec(w_all, j),
            rope_spec, rope_spec] + [pl.BlockSpec(memory_space=pl.ANY)] * len(caches),
        out_specs=res_ospecs + (t.rows(ATTN_WIDTH),) * 4 + (cache_spec,) * 2,
        scratch_shapes=res_scratch + [pltpu.VMEM((D_MODEL, n_in), BF16), pltpu.VMEM((t.tm, D_MODEL), BF16),
                                      pltpu.VMEM((t.tm, 2 * ATTN_KV_WIDTH), F32)],
        input_output_aliases={n_front + c: len(res_out) + 4 + c for c in range(len(caches))},
        compiler_params=_params("arbitrary", vmem=FUSED_VMEM_LIMIT),
        name="attn_in",
    )(*res_args, g, mod, mod, w_all, cos_t, sin_t, *caches)


def _stack_group_queries(q):
    qf = q.astype(F32)
    chunk = lax.broadcasted_iota(jnp.int32, qf.shape, 1) // ATTN_HEAD_DIM
    return jnp.concatenate(
        [jnp.where(chunk == g, qf, 0.0) for g in range(ATTN_GROUP)], axis=0).astype(BF16)


def _gather_group_outputs(o, rows):
    chunk = lax.broadcasted_iota(jnp.int32, (rows, MXU_N), 1) // ATTN_HEAD_DIM
    acc = jnp.zeros((rows, MXU_N), F32)
    for g in range(ATTN_GROUP):
        acc = acc + jnp.where(chunk == g, o[g * rows:(g + 1) * rows], 0.0)
    return acc


def _sink_column(sink_ref, h, rows):
    grp = lax.broadcasted_iota(jnp.int32, (ATTN_GROUP * rows, 1), 0) // rows
    col = jnp.zeros((ATTN_GROUP * rows, 1), F32)
    for g in range(ATTN_GROUP):
        col = jnp.where(grp == g, sink_ref[h * ATTN_GROUP + g] * LOG2_E, col)
    return col


def _chunk_rows(dtype):
    chunk = lax.broadcasted_iota(jnp.int32, (1, MXU_N), 1) // ATTN_HEAD_DIM
    return [(chunk == g).astype(F32).astype(dtype) for g in range(ATTN_GROUP)]


def _block_diag_rows(x4):
    return jnp.concatenate([x4 * m for m in _chunk_rows(x4.dtype)], axis=0)


def _ctx_attn_kernel(sink_ref, q_ref, k4_ref, v4_ref, z_ref, y_ref):
    chunk = lax.broadcasted_iota(jnp.int32, (SEQ, MXU_N), 1) // ATTN_HEAD_DIM
    for h in range(ATTN_KV_HEADS):
        cols = slice(h * MXU_N, (h + 1) * MXU_N)
        s = _dot_nt(q_ref[:, cols], _block_diag_rows(k4_ref[:, cols]))
        inv = jnp.zeros((SEQ, MXU_N), F32)
        probs = []
        for g in range(ATTN_GROUP):
            sg = s[:, g * SEQ:(g + 1) * SEQ]
            sk = sink_ref[h * ATTN_GROUP + g] * LOG2_E
            m = jnp.maximum(jnp.max(sg, axis=1, keepdims=True), sk)
            e = jnp.exp2(sg - m)
            den = jnp.sum(e, axis=1, keepdims=True) + jnp.exp2(sk - m)
            probs.append(e.astype(BF16))
            inv = jnp.where(chunk == g, 1.0 / den, inv)
        o = _dot(jnp.concatenate(probs, axis=1), _block_diag_rows(v4_ref[:, cols]))
        y_ref[:, cols] = (o * inv * z_ref[:, cols].astype(F32)).astype(BF16)


def _ctx_attn(sink, q, k4, v4, z):
    spec = pl.BlockSpec((SEQ, ATTN_WIDTH), lambda b: (b, 0))
    return pl.pallas_call(
        _ctx_attn_kernel,
        out_shape=jax.ShapeDtypeStruct((N_PROMPT_TOK, ATTN_WIDTH), BF16),
        grid=(BATCH,),
        in_specs=[pl.BlockSpec(memory_space=pltpu.SMEM), spec, spec, spec, spec],
        out_specs=spec,
        compiler_params=_params("arbitrary"),
        name="ctx_attn",
    )(sink, q, k4, v4, z)


LAT_STEP = Q_SUB * Q_BLOCK
LAT_PER_SEQ = DEC_SEQ // LAT_STEP
LAT_BLOCKS = DEC_BATCH * ATTN_KV_HEADS * LAT_PER_SEQ


def _lat_block(blk):
    return (blk // (ATTN_KV_HEADS * LAT_PER_SEQ), (blk // LAT_PER_SEQ) % ATTN_KV_HEADS, blk % LAT_PER_SEQ)


def _band_rows(qb):
    return pl.ds(pl.multiple_of(qb * Q_BLOCK, Q_BLOCK), 3 * Q_BLOCK)


def _pad_sequence(dst, src):
    zeros = jnp.zeros((Q_BLOCK, MXU_N), BF16)
    dst[0:Q_BLOCK, :] = zeros
    dst[Q_BLOCK:Q_BLOCK + DEC_SEQ, :] = src[...]
    dst[Q_BLOCK + DEC_SEQ:, :] = zeros


def _lat_attn_kernel(sink_ref, q_ref, k4_ref, v4_ref, kc_ref, vc_ref, z_ref, y_ref,
                     kp, vp, k4c, v4c, sc_a, sl_a, sc_b, sl_b, e_scr):
    t = pl.program_id(0)
    _, _, n1 = _lat_block(jnp.minimum(t, LAT_BLOCKS - 1))
    _, h0, n0 = _lat_block(jnp.maximum(t - 1, 0))
    n_blocks = DEC_SEQ // Q_BLOCK

    @pl.when(t == 0)
    def _():
        sc_b[...] = jnp.zeros(sc_b.shape, F32)
        sl_b[...] = jnp.zeros(sl_b.shape, F32)

    @pl.when(n1 == 0)
    def _():
        _pad_sequence(kp, k4_ref)
        k4c[...] = jnp.concatenate([kc_ref[...]] * ATTN_GROUP, axis=0).astype(BF16)

    @pl.when(n0 == 0)
    def _():
        _pad_sequence(vp, v4_ref)
        v4c[...] = jnp.concatenate([vc_ref[...]] * ATTN_GROUP, axis=0).T.astype(BF16)

    @pl.when(t % 2 == 0)
    def _():
        _lat_stages(sink_ref, q_ref, z_ref, y_ref, kp, vp, k4c, v4c, e_scr, n1, h0, n0,
                    sc_a, sl_a, sc_b, sl_b)

    @pl.when(t % 2 == 1)
    def _():
        _lat_stages(sink_ref, q_ref, z_ref, y_ref, kp, vp, k4c, v4c, e_scr, n1, h0, n0,
                    sc_b, sl_b, sc_a, sl_a)


def _lat_stages(sink_ref, q_ref, z_ref, y_ref, kp, vp, k4c, v4c, e_scr, n1, h0, n0,
                sc_w, sl_w, sc_r, sl_r):
    n_blocks = DEC_SEQ // Q_BLOCK
    for sub in range(Q_SUB):
        qs = _stack_group_queries(q_ref[sub * Q_BLOCK:(sub + 1) * Q_BLOCK, :])
        sc_w[sub] = _dot(qs, k4c[...])
        sl_w[sub] = _dot_nt(qs, kp[_band_rows(n1 * Q_SUB + sub), :])

    rows4 = ATTN_GROUP * Q_BLOCK
    r = lax.broadcasted_iota(jnp.int32, (rows4, Q_BLOCK), 0) % Q_BLOCK
    c = lax.broadcasted_iota(jnp.int32, (rows4, Q_BLOCK), 1)
    in_left = c >= r
    in_right = c <= r
    sk = _sink_column(sink_ref, h0, Q_BLOCK)
    for sub in range(Q_SUB):
        qb = n0 * Q_SUB + sub
        band = _band_rows(qb)
        qrows = slice(sub * Q_BLOCK, (sub + 1) * Q_BLOCK)
        s_ctx = sc_r[sub]
        s_lat = sl_r[sub]
        blocks = [
            s_ctx,
            jnp.where(jnp.logical_and(in_left, qb > 0), s_lat[:, :Q_BLOCK], NEG_INF),
            s_lat[:, Q_BLOCK:2 * Q_BLOCK],
            jnp.where(jnp.logical_and(in_right, qb < n_blocks - 1), s_lat[:, 2 * Q_BLOCK:], NEG_INF),
        ]
        m = sk
        for s in blocks:
            m = jnp.maximum(m, jnp.max(s, axis=1, keepdims=True))
        den = jnp.exp2(sk - m)
        lo = 0
        for s in blocks:
            e = jnp.exp2(s - m)
            den = den + jnp.sum(e, axis=1, keepdims=True)
            e_scr[sub, :, lo:lo + s.shape[1]] = e.astype(BF16)
            lo += s.shape[1]
        o = (_dot(e_scr[sub, :, :PAST_LEN], v4c[...])
             + _dot(e_scr[sub, :, PAST_LEN:], vp[band, :])) * (1.0 / den)
        acc = _gather_group_outputs(o, Q_BLOCK)
        y_ref[qrows, :] = (acc * z_ref[qrows, :].astype(F32)).astype(BF16)


def _lat_attn(sink, q, k4, v4, kc, vc, j, z):
    padded = DEC_SEQ + 2 * Q_BLOCK
    rows4 = ATTN_GROUP * Q_BLOCK
    row0 = N_PROMPT_TOK // LAT_STEP
    seq0 = N_PROMPT_TOK // DEC_SEQ
    stage1 = lambda t: _lat_block(jnp.minimum(t, LAT_BLOCKS - 1))
    stage2 = lambda t: _lat_block(jnp.maximum(t - 1, 0))

    def spec(shape, stage, index):
        return pl.BlockSpec(shape, lambda t: index(*stage(t)))

    tile = (LAT_STEP, MXU_N)
    seq = (DEC_SEQ, MXU_N)
    ctx = (None, None, None, ATTN_HEAD_DIM, PAST_LEN)
    return pl.pallas_call(
        _lat_attn_kernel,
        out_shape=jax.ShapeDtypeStruct((N_SAMPLE_TOK, ATTN_WIDTH), BF16),
        grid=(LAT_BLOCKS + 1,),
        in_specs=[pl.BlockSpec(memory_space=pltpu.SMEM),
                  spec(tile, stage1, lambda b, h, n: (row0 + b * LAT_PER_SEQ + n, h)),
                  spec(seq, stage1, lambda b, h, n: (seq0 + b, h)),
                  spec(seq, stage2, lambda b, h, n: (seq0 + b, h)),
                  spec(ctx, stage1, lambda b, h, n: (b, j, h, 0, 0)),
                  spec(ctx, stage2, lambda b, h, n: (b, j, h, 0, 0)),
                  spec(tile, stage2, lambda b, h, n: (row0 + b * LAT_PER_SEQ + n, h))],
        out_specs=spec(tile, stage2, lambda b, h, n: (b * LAT_PER_SEQ + n, h)),
        scratch_shapes=[pltpu.VMEM((padded, MXU_N), BF16), pltpu.VMEM((padded, MXU_N), BF16),
                        pltpu.VMEM((MXU_N, PAST_LEN), BF16), pltpu.VMEM((PAST_LEN, MXU_N), BF16),
                        pltpu.VMEM((Q_SUB, rows4, PAST_LEN), F32),
                        pltpu.VMEM((Q_SUB, rows4, 3 * Q_BLOCK), F32),
                        pltpu.VMEM((Q_SUB, rows4, PAST_LEN), F32),
                        pltpu.VMEM((Q_SUB, rows4, 3 * Q_BLOCK), F32),
                        pltpu.VMEM((Q_SUB, rows4, PAST_LEN + 3 * Q_BLOCK), BF16)],
        compiler_params=_params("arbitrary", vmem=VMEM_LIMIT),
        name="lat_attn",
    )(sink, q, k4, v4, kc, vc, z)


def _out_final_kernel(yp_ref, ys_ref, wf_ref, x_ref, gate_ref, fg_ref, op_ref, os_ref, w_ref):
    t = FINAL
    i = pl.program_id(0)
    _cast_weight_once(wf_ref, w_ref)
    y = _read_parts(t, (yp_ref, ys_ref))
    r = _rms(x_ref[...] + gate_ref[...] * _dot(y, w_ref[...]), fg_ref[...])

    @pl.when(i < t.n_prompt)
    def _():
        op_ref[...] = r

    @pl.when(i >= t.n_prompt)
    def _():
        os_ref[...] = r


def _out_proj_final(mixed, x, final_g):
    t = FINAL
    return pl.pallas_call(
        _out_final_kernel,
        out_shape=(jax.ShapeDtypeStruct((N_PROMPT_TOK, D_MODEL), F32),
                   jax.ShapeDtypeStruct((N_SAMPLE_TOK, D_MODEL), F32)),
        grid=(t.n_tiles,),
        in_specs=_residual_specs(t, mixed, (x,)) + [_const_spec((1, D_MODEL))],
        out_specs=(t.prompt_rows(D_MODEL), t.sample_rows(D_MODEL)),
        scratch_shapes=[pltpu.VMEM(mixed.w_out.shape[1:], BF16)],
        compiler_params=_params("arbitrary", vmem=VMEM_LIMIT),
        name="out_proj_final",
    )(*_residual_args(mixed, (x,)), final_g)


def _pool_in_kernel(*refs, n_y, n_x):
    t = PROJ
    n_res = n_y + n_x + 2
    res_refs, refs = refs[:n_res], refs[n_res:]
    g_ref, sh_ref, sc_ref, wf_ref, xo_ref, u_ref, z_ref, wo_ref, w_ref, h_ref = refs
    y_refs, wof_ref, x_refs, gate_ref = (res_refs[:n_y], res_refs[n_y], res_refs[n_y + 1:n_y + 1 + n_x],
                                         res_refs[n_y + 1 + n_x])
    _cast_weight_once(wof_ref, wo_ref)
    _cast_weight_once(wf_ref, w_ref)

    def proj_steps(sub):
        rows = slice(sub * PROJ_SUB, (sub + 1) * PROJ_SUB)

        def chunk(c):
            def step():
                cols = slice(c * MXU_N, (c + 1) * MXU_N)
                u_ref[rows, cols] = _dot(h_ref[rows, :], w_ref[:, cols])
            return step

        def gate_chunk(c):
            def step():
                a = _dot(h_ref[rows, :], w_ref[:, D_MODEL + c * MXU_N:D_MODEL + (c + 1) * MXU_N])
                z_ref[rows, c * MXU_N:(c + 1) * MXU_N] = _silu(a).astype(BF16)
            return step

        n = D_MODEL // MXU_N
        return [chunk(c) for c in range(n)] + [gate_chunk(c) for c in range(n)]

    _run_halves(t, x_refs, (y_refs, gate_ref), g_ref, sh_ref, sc_ref, xo_ref, wo_ref, h_ref, proj_steps)


def _pool_in(mixed, x_parts, g, mod, w_all, j):
    t = PROJ
    slab = jax.ShapeDtypeStruct((N_TOK, D_MODEL), F32)
    return pl.pallas_call(
        functools.partial(_pool_in_kernel, n_y=len(mixed.y_parts), n_x=len(x_parts)),
        out_shape=(slab, slab, jax.ShapeDtypeStruct((N_TOK, D_MODEL), BF16)),
        grid=(t.n_tiles,),
        in_specs=_residual_specs(t, mixed, x_parts) + [
            _const_spec((1, D_MODEL)), t.mod(0), t.mod(1), _layer_spec(w_all, j)],
        out_specs=(t.rows(D_MODEL),) * 3,
        scratch_shapes=[pltpu.VMEM(mixed.w_out.shape[1:], BF16), pltpu.VMEM((D_MODEL, 2 * D_MODEL), BF16),
                        pltpu.VMEM((t.tm, D_MODEL), BF16)],
        compiler_params=_params("arbitrary", vmem=FUSED_VMEM_LIMIT),
        name="pool_in",
    )(*_residual_args(mixed, x_parts), g, mod, mod, w_all)


def _split_bf16(a):
    hi = a.astype(BF16)
    return hi, (a - hi.astype(F32)).astype(BF16)


def _band_ones(shape, lo, hi):
    d = lax.broadcasted_iota(jnp.int32, shape, 1) - lax.broadcasted_iota(jnp.int32, shape, 0)
    return jnp.logical_and(d >= lo, d <= hi).astype(F32).astype(BF16)


def _stream_cast_weight(w_hbm, wb_ref, stage_ref, sem):
    width = stage_ref.shape[-1]
    n_chunks = wb_ref.shape[-1] // width

    def copy(c):
        return pltpu.make_async_copy(w_hbm.at[:, pl.ds(c * width, width)], stage_ref.at[c % 2], sem.at[c % 2])

    copy(0).start()
    for c in range(n_chunks):
        if c + 1 < n_chunks:
            copy(c + 1).start()
        copy(c).wait()
        wb_ref[:, c * width:(c + 1) * width] = stage_ref[c % 2].astype(BF16)


def _pool_ret_kernel(u_ref, up_ref, un_ref, z_ref, wgf_ref, ps_ref, wof_ref, x_ref, gate_ref,
                     g_ref, sh_ref, sc_ref, w_hbm,
                     xo_ref, q_ref, kt_ref, v_ref, zr_ref,
                     y_ref, wg_ref, wo_ref, band_ref, wkt_ref, w_ref, stage_ref, sem, h_ref, *, j_ret):
    t = PROJ
    i = pl.program_id(0)
    _cast_weight_once(wgf_ref, wg_ref)
    _cast_weight_once(wof_ref, wo_ref)

    @pl.when(i == 0)
    def _():
        for g, w in enumerate(POOL_WINDOWS):
            band_ref[g] = _band_ones((POOL_SUB, POOL_SUB), -(w // 2), w - 1 - w // 2)
        _stream_cast_weight(w_hbm.at[j_ret], w_ref, stage_ref, sem)
        for r in range(0, RET_QK_WIDTH, MXU_N):
            wk = w_ref[:, RET_QK_WIDTH + r:RET_QK_WIDTH + r + MXU_N]
            wkt_ref[r:r + MXU_N, :] = wk.astype(F32).T.astype(BF16)

    is_dec = i >= t.n_prompt
    st = t.seq_tile(i)
    seq_len = jnp.where(is_dec, DEC_SEQ, SEQ)
    h = POOL_HALO
    n_sub = t.tm // POOL_SUB
    def pool_steps(sub):
        rows = slice(sub * POOL_SUB, (sub + 1) * POOL_SUB)
        if sub == 0:
            before = jnp.where(jnp.logical_and(is_dec, st != 0), up_ref[...], 0.0)
        else:
            before = jnp.where(is_dec, u_ref[sub * POOL_SUB - h:sub * POOL_SUB, :], 0.0)
        if sub == n_sub - 1:
            after = jnp.where(jnp.logical_and(is_dec, st != t.per_dec_seq - 1), un_ref[...], 0.0)
        else:
            after = jnp.where(is_dec, u_ref[(sub + 1) * POOL_SUB:(sub + 1) * POOL_SUB + h, :], 0.0)
        pos0 = jnp.where(is_dec, st * t.tm + sub * POOL_SUB, 0)
        return _pool_core_steps(u_ref.at[rows], before, after, pos0, seq_len, z_ref.at[rows], ps_ref,
                                wg_ref, band_ref, y_ref.at[rows])

    def mix_steps(sub):
        rows = slice(sub * POOL_SUB, (sub + 1) * POOL_SUB)

        def residual():
            xo_ref[rows, :] = x_ref[rows, :] + gate_ref[...] * _dot(y_ref[rows, :], wo_ref[...])

        def normalise():
            h_ref[rows, :] = _norm_mod(xo_ref[rows, :], g_ref, sh_ref, sc_ref)

        return pool_steps(sub) + [residual, normalise]

    for step in mix_steps(0):
        step()
    for sub in range(n_sub):
        rows = slice(sub * POOL_SUB, (sub + 1) * POOL_SUB)
        proj = _ret_in_steps(h_ref.at[rows], w_ref, wkt_ref, q_ref.at[rows],
                             kt_ref.at[pl.ds(sub * POOL_SUB // RET_CHUNK, POOL_SUB // RET_CHUNK)],
                             v_ref.at[rows], zr_ref.at[rows])
        _run_interleaved(proj, mix_steps(sub + 1) if sub + 1 < n_sub else [])


def _pool_core_steps(u_ref, before, after, pos0, seq_len, z_ref, ps_ref, wg_ref, band_ref, y_ref):
    n = u_ref.shape[0]
    h = POOL_HALO
    halo_hi, halo_lo = _split_bf16(jnp.concatenate([before, after], axis=0))
    pos = pos0 + lax.broadcasted_iota(jnp.int32, (n, 1), 0)
    rr = lax.broadcasted_iota(jnp.int32, (2 * h, 2 * h), 0)
    cc = lax.broadcasted_iota(jnp.int32, (2 * h, 2 * h), 1)

    def pooled(g, w):
        def step():
            left = w // 2
            right = w - 1 - left
            cols = slice(g * POOL_GROUP_DIM, (g + 1) * POOL_GROUP_DIM)
            u = u_ref[:, cols]
            s = _dot(band_ref[g], u.astype(BF16))
            top = jnp.logical_and(jnp.logical_and(rr < h, cc < h), cc - h >= rr - left)
            bot = jnp.logical_and(jnp.logical_and(rr >= h, cc >= h), cc - h <= rr - 2 * h + right)
            edge = jnp.logical_or(top, bot).astype(F32).astype(BF16)
            se = _dot(edge, halo_hi[:, cols]) + _dot(edge, halo_lo[:, cols])
            s = jnp.concatenate([s[:h] + se[:h], s[h:n - h], s[n - h:] + se[h:]], axis=0)
            cnt = (jnp.minimum(pos + right + 1, seq_len) - jnp.maximum(pos - left, 0)).astype(F32)
            y_ref[:, cols] = (s / cnt - u).astype(BF16)
        return step

    def mixed(g):
        def step():
            cols = slice(g * POOL_GROUP_DIM, (g + 1) * POOL_GROUP_DIM)
            yg = _dot(y_ref[:, cols], wg_ref[g]) * ps_ref[:, cols] * z_ref[:, cols].astype(F32)
            y_ref[:, cols] = yg.astype(BF16)
        return step

    return ([pooled(g, w) for g, w in enumerate(POOL_WINDOWS)]
            + [mixed(g) for g in range(len(POOL_WINDOWS))])


def _pool_ret(u, z, wg_all, ps, wo_all, j, x, mod_pool, g, mod, w_ret_all, j_ret):
    t = PROJ
    per = t.tm // POOL_HALO
    n_halo = N_TOK // POOL_HALO
    n_in = 2 * RET_QK_WIDTH + 2 * RET_V_WIDTH
    kt_per = t.tm // RET_CHUNK
    return pl.pallas_call(
        functools.partial(_pool_ret_kernel, j_ret=j_ret),
        out_shape=(jax.ShapeDtypeStruct((N_TOK, D_MODEL), F32),
                   jax.ShapeDtypeStruct((N_TOK, RET_QK_WIDTH), BF16),
                   jax.ShapeDtypeStruct((N_TOK // RET_CHUNK, RET_QK_WIDTH, RET_CHUNK), BF16),
                   jax.ShapeDtypeStruct((N_TOK, RET_V_WIDTH), BF16),
                   jax.ShapeDtypeStruct((N_TOK, RET_V_WIDTH), BF16)),
        grid=(t.n_tiles,),
        in_specs=[
            t.rows(D_MODEL),
            pl.BlockSpec((POOL_HALO, D_MODEL), lambda i: (jnp.maximum(i * per - 1, 0), 0)),
            pl.BlockSpec((POOL_HALO, D_MODEL), lambda i: (jnp.minimum((i + 1) * per, n_halo - 1), 0)),
            t.rows(D_MODEL),
            _layer_spec(wg_all, j),
            _const_spec((1, D_MODEL)),
            _layer_spec(wo_all, j),
            t.rows(D_MODEL),
            t.mod(2),
            _const_spec((1, D_MODEL)), t.mod(0), t.mod(1), pl.BlockSpec(memory_space=pl.ANY),
        ],
        out_specs=(t.rows(D_MODEL), t.rows(RET_QK_WIDTH),
                   pl.BlockSpec((kt_per, RET_QK_WIDTH, RET_CHUNK), lambda i: (i, 0, 0)),
                   t.rows(RET_V_WIDTH), t.rows(RET_V_WIDTH)),
        scratch_shapes=[pltpu.VMEM((t.tm, D_MODEL), BF16),
                        pltpu.VMEM((len(POOL_WINDOWS), POOL_GROUP_DIM, POOL_GROUP_DIM), BF16),
                        pltpu.VMEM((D_MODEL, D_MODEL), BF16),
                        pltpu.VMEM((len(POOL_WINDOWS), POOL_SUB, POOL_SUB), BF16),
                        pltpu.VMEM((RET_QK_WIDTH, D_MODEL), BF16),
                        pltpu.VMEM((D_MODEL, n_in), BF16),
                        pltpu.VMEM((2, D_MODEL, WEIGHT_STAGE_COLS), F32),
                        pltpu.SemaphoreType.DMA((2,)),
                        pltpu.VMEM((t.tm, D_MODEL), BF16)],
        compiler_params=_params("arbitrary", vmem=FUSED_VMEM_LIMIT),
        name="pool_ret",
    )(u, u, u, z, wg_all, ps, wo_all, x, mod_pool, g, mod, mod, w_ret_all)


def _ret_in_steps(h_ref, w_ref, wkt_ref, q_ref, kt_ref, v_ref, z_ref):
    n_rows = h_ref.shape[0]

    def chunk(ref, lo, c, post=lambda a: a):
        def step():
            cols = slice(c * MXU_N, (c + 1) * MXU_N)
            ref[:, cols] = post(_dot(h_ref[...], w_ref[:, lo + c * MXU_N:lo + (c + 1) * MXU_N])).astype(BF16)
        return step

    def key_chunk(c):
        def step():
            rows = slice(c * MXU_N, (c + 1) * MXU_N)
            kt = (_dot_nt(wkt_ref[rows, :], h_ref[...]) * RET_KEY_DIM ** -0.5).astype(BF16)
            for cc in range(n_rows // RET_CHUNK):
                kt_ref[cc, rows, :] = kt[:, cc * RET_CHUNK:(cc + 1) * RET_CHUNK]
        return step

    steps = [chunk(q_ref, 0, c) for c in range(RET_QK_WIDTH // MXU_N)]
    steps += [key_chunk(c) for c in range(RET_QK_WIDTH // MXU_N)]
    steps += [chunk(v_ref, 2 * RET_QK_WIDTH, c) for c in range(RET_V_WIDTH // MXU_N)]
    steps += [chunk(z_ref, 2 * RET_QK_WIDTH + RET_V_WIDTH, c, _silu) for c in range(RET_V_WIDTH // MXU_N)]
    return steps


def _run_interleaved(main, side):
    done = 0
    for k, step in enumerate(main):
        step()
        due = (k + 1) * len(side) // len(main)
        for s in side[done:due]:
            s()
        done = due


def _pos(shape, axis):
    return lax.broadcasted_iota(jnp.int32, shape, axis).astype(F32)


def _ret_tables_kernel(lgf_ref, lgb_ref, decay_ref, row_ref, col_ref, cdec_ref):
    h = pl.program_id(0)
    lg_f = lgf_ref[h]
    lg_b = lgb_ref[h]
    c = RET_CHUNK
    diff = _pos((c, c), 0) - _pos((c, c), 1)
    fwd = jnp.where(diff >= 0, jnp.exp(jnp.maximum(diff, 0.0) * lg_f), 0.0)
    bwd = jnp.where(diff <= 0, jnp.exp(jnp.maximum(-diff, 0.0) * lg_b), 0.0)
    decay_ref[...] = fwd + bwd
    j = _pos((RET_TAB_ROWS, c), 1)
    row_ref[0] = jnp.exp((c - 1.0 - j) * lg_f)
    row_ref[1] = jnp.exp(j * lg_b)
    i = _pos((c, LANES), 0)
    col_ref[0] = jnp.exp((i + 1.0) * lg_f)
    col_ref[1] = jnp.exp((c - i) * lg_b)
    full = jnp.full((RET_TAB_ROWS, RET_VAL_DIM), float(c), F32)
    cdec_ref[0] = jnp.exp(full * lg_f)
    cdec_ref[1] = jnp.exp(full * lg_b)


def _ret_tables(lg_f, lg_b):
    smem = pl.BlockSpec(memory_space=pltpu.SMEM)
    c = RET_CHUNK
    shapes = ((c, c), (2, RET_TAB_ROWS, c), (2, c, LANES), (2, RET_TAB_ROWS, RET_VAL_DIM))
    return pl.pallas_call(
        _ret_tables_kernel,
        out_shape=tuple(jax.ShapeDtypeStruct((RET_HEADS,) + s, F32) for s in shapes),
        grid=(RET_HEADS,),
        in_specs=[smem, smem],
        out_specs=tuple(pl.BlockSpec((None,) + s, lambda h, n=len(s): (h,) + (0,) * n) for s in shapes),
        compiler_params=_params("arbitrary"),
        name="ret_tables",
    )(lg_f, lg_b)


def _group_norm_gate(o, gn, z):
    mu = jnp.mean(o, axis=-1, keepdims=True)
    var = jnp.mean(jnp.square(o - mu), axis=-1, keepdims=True)
    on = (o - mu) * lax.rsqrt(var + EPS)
    return (on * gn * z.astype(F32)).astype(BF16)


def _ret_ctx_kernel(q_ref, kt_ref, v_ref, z_ref, gn_ref, decay_ref, row_ref, y_ref, sf_ref, sb_ref):
    for s in range(RET_CTX_SEQS):
        rows = slice(s * SEQ, (s + 1) * SEQ)
        for h in range(RET_HEADS):
            kc = slice(h * RET_KEY_DIM, (h + 1) * RET_KEY_DIM)
            vc = slice(h * RET_VAL_DIM, (h + 1) * RET_VAL_DIM)
            kt = kt_ref[s, kc, :]
            v = v_ref[rows, vc]
            att = (_dot(q_ref[rows, kc], kt) * decay_ref[h]).astype(BF16)
            y_ref[rows, vc] = _group_norm_gate(_dot(att, v), gn_ref[:, vc], z_ref[rows, vc])
            ktf = kt.astype(F32)
            sf_ref[s, h] = _dot((ktf * row_ref[h, 0, 0:1, :]).astype(BF16), v)
            sb_ref[s, h] = _dot((ktf * row_ref[h, 1, 0:1, :]).astype(BF16), v)


def _ret_ctx(q, kt, v, z, gn, decay, row):
    c = RET_CHUNK
    n = RET_CTX_SEQS
    wide = lambda w: pl.BlockSpec((n * SEQ, w), lambda b: (b, 0))
    st_spec = pl.BlockSpec((n, RET_HEADS, RET_KEY_DIM, RET_VAL_DIM), lambda b: (b, 0, 0, 0))
    st_shape = jax.ShapeDtypeStruct((BATCH, RET_HEADS, RET_KEY_DIM, RET_VAL_DIM), F32)
    return pl.pallas_call(
        _ret_ctx_kernel,
        out_shape=(jax.ShapeDtypeStruct((N_PROMPT_TOK, RET_V_WIDTH), BF16), st_shape, st_shape),
        grid=(BATCH // n,),
        in_specs=[wide(RET_QK_WIDTH), pl.BlockSpec((n, RET_QK_WIDTH, c), lambda b: (b, 0, 0)),
                  wide(RET_V_WIDTH), wide(RET_V_WIDTH), _const_spec((1, RET_V_WIDTH)),
                  _const_spec((RET_HEADS, c, c)), _const_spec((RET_HEADS, 2, RET_TAB_ROWS, c))],
        out_specs=(wide(RET_V_WIDTH), st_spec, st_spec),
        compiler_params=_params("arbitrary", vmem=VMEM_LIMIT),
        name="ret_ctx",
    )(q, kt, v, z, gn, decay, row)


def _ret_lat_kernel(q_ref, kt_ref, v_ref, z_ref, gn_ref, decay_ref, row_ref, col_ref, cdec_ref,
                    s0f_ref, s0b_ref, y_ref, sf_all, sb_all, sf_acc, sb_acc):
    c = RET_CHUNK
    n_chunks = DEC_SEQ // c
    rows_of = lambda ci: pl.ds(pl.multiple_of(ci * c, c), c)

    sf_acc[...] = s0f_ref[...]
    sb_acc[...] = s0b_ref[...]

    def scan_step(i, carry):
        cf = i
        cb = n_chunks - 1 - i
        sf_all[cf] = sf_acc[...].astype(BF16)
        sb_all[cb] = sb_acc[...].astype(BF16)
        uf = _dot((kt_ref[cf].astype(F32) * row_ref[0, 0:1, :]).astype(BF16), v_ref[rows_of(cf), :])
        ub = _dot((kt_ref[cb].astype(F32) * row_ref[1, 0:1, :]).astype(BF16), v_ref[rows_of(cb), :])
        sf_acc[...] = sf_acc[...] * cdec_ref[0, 0:1, :] + uf
        sb_acc[...] = sb_acc[...] * cdec_ref[1, 0:1, :] + ub
        return carry

    lax.fori_loop(0, n_chunks, scan_step, 0, unroll=4)

    def out_step(ci, carry):
        rows = rows_of(ci)
        q = q_ref[rows, :]
        qf = q.astype(F32)
        qdec_f = jnp.concatenate([col_ref[0]] * (RET_KEY_DIM // LANES), axis=1)
        qdec_b = jnp.concatenate([col_ref[1]] * (RET_KEY_DIM // LANES), axis=1)
        att = (_dot(q, kt_ref[ci]) * decay_ref[...]).astype(BF16)
        o = (_dot(att, v_ref[rows, :])
             + _dot((qf * qdec_f).astype(BF16), sf_all[ci])
             + _dot((qf * qdec_b).astype(BF16), sb_all[ci]))
        y_ref[rows, :] = _group_norm_gate(o, gn_ref[...], z_ref[rows, :])
        return carry

    lax.fori_loop(0, n_chunks, out_step, 0, unroll=8)


def _ret_lat(q, kt, v, z, gn, decay, row, col, cdec, s0f, s0b):
    c = RET_CHUNK
    n_chunks = DEC_SEQ // c
    row0 = N_PROMPT_TOK // DEC_SEQ
    qk_spec = pl.BlockSpec((DEC_SEQ, RET_KEY_DIM), lambda b, h: (row0 + b, h))
    v_spec = pl.BlockSpec((DEC_SEQ, RET_VAL_DIM), lambda b, h: (row0 + b, h))
    st_spec = pl.BlockSpec((None, None, RET_KEY_DIM, RET_VAL_DIM), lambda b, h: (b, h, 0, 0))
    tab = lambda *s: pl.BlockSpec((None,) + s, lambda b, h: (h,) + (0,) * len(s))
    states = pltpu.VMEM((n_chunks, RET_KEY_DIM, RET_VAL_DIM), BF16)
    acc = pltpu.VMEM((RET_KEY_DIM, RET_VAL_DIM), F32)
    return pl.pallas_call(
        _ret_lat_kernel,
        out_shape=jax.ShapeDtypeStruct((N_SAMPLE_TOK, RET_V_WIDTH), BF16),
        grid=(DEC_BATCH, RET_HEADS),
        in_specs=[qk_spec,
                  pl.BlockSpec((n_chunks, RET_KEY_DIM, c), lambda b, h: (row0 + b, h, 0)),
                  v_spec, v_spec, pl.BlockSpec((1, RET_VAL_DIM), lambda b, h: (0, h)),
                  tab(c, c), tab(2, RET_TAB_ROWS, c), tab(2, c, LANES), tab(2, RET_TAB_ROWS, RET_VAL_DIM),
                  st_spec, st_spec],
        out_specs=pl.BlockSpec((DEC_SEQ, RET_VAL_DIM), lambda b, h: (b, h)),
        scratch_shapes=[states, states, acc, acc],
        compiler_params=_params("arbitrary", "arbitrary", vmem=VMEM_LIMIT),
        name="ret_lat",
    )(q, kt, v, z, gn, decay, row, col, cdec, s0f, s0b)


def _rope_tables(tm):
    n_rows = DEC_SEQ // GRID_W
    rows = jnp.repeat(jnp.arange(n_rows), GRID_W).astype(F32)
    cols = jnp.tile(jnp.arange(GRID_W), n_rows).astype(F32)
    half = ATTN_HEAD_DIM // 4
    inv = ROPE_BASE ** (-jnp.arange(half, dtype=F32) / half)
    ang_r = rows[:, None] * inv[None, :]
    ang_c = cols[:, None] * inv[None, :]
    cos = jnp.concatenate([jnp.cos(ang_r), jnp.cos(ang_r), jnp.cos(ang_c), jnp.cos(ang_c)], axis=-1)
    sin = jnp.concatenate([-jnp.sin(ang_r), jnp.sin(ang_r), -jnp.sin(ang_c), jnp.sin(ang_c)], axis=-1)
    cos = jnp.concatenate([jnp.ones((tm, ATTN_HEAD_DIM), F32), cos], axis=0)
    sin = jnp.concatenate([jnp.zeros((tm, ATTN_HEAD_DIM), F32), sin], axis=0)
    return jnp.tile(cos, (1, 2)), jnp.tile(sin, (1, 2))


def kernel(x_prompt, x_sample, cache_k, cache_v, state_fwd, state_bwd, c, c_ctx, norm_g, ada_w, ada_b, attn_w_in, attn_w_out, attn_sink, pool_w_in, pool_w_grp, pool_scale, pool_w_out, ret_w_in, ret_decay_fwd, ret_decay_bwd, ret_gn_g, ret_w_out, final_g):
    x_parts = (x_prompt.reshape(N_PROMPT_TOK, D_MODEL), x_sample.reshape(N_SAMPLE_TOK, D_MODEL))
    cond = jnp.concatenate([c_ctx[None, :], c,
                            jnp.zeros((N_COND - 1 - DEC_BATCH, D_MODEL), F32)], axis=0)
    mods = _ada_table(cond.T, ada_w, ada_b).reshape(DEPTH, N_COND, 1, 3 * D_MODEL)
    cos_t, sin_t = _rope_tables(PROJ.tm)

    to_kernel = lambda a: jnp.transpose(a, (0, 1, 3, 4, 2))
    from_kernel = lambda a: jnp.transpose(
        a.reshape(a.shape[0], a.shape[1], ATTN_KV_HEADS, ATTN_HEAD_DIM, a.shape[3]), (0, 1, 4, 2, 3))
    ctx_k, ctx_v = to_kernel(cache_k), to_kernel(cache_v)

    assert DEPTH % N_MIXERS == 1, "the layer stack must end on an attention layer"
    caches = ()
    new_sf = new_sb = None
    mixed = None
    for i in range(DEPTH):
        kind, j = i % N_MIXERS, i // N_MIXERS
        g = norm_g[i].reshape(1, D_MODEL)
        mod = mods[i]
        if kind == 0:
            outs = _attn_in(mixed, x_parts, g, mod, attn_w_in, j, cos_t, sin_t, tuple(caches))
            if mixed is not None:
                x_parts, outs = (outs[0],), outs[1:]
            q, k4, v4, z, *caches = outs
            y_parts = (_ctx_attn(attn_sink[j], q, k4, v4, z),
                       _lat_attn(attn_sink[j], q, k4, v4, ctx_k, ctx_v, j, z))
            mixed = Mixed(y_parts, attn_w_out, j, mod)
        elif kind == 1:
            x, u, z = _pool_in(mixed, x_parts, g, mod, pool_w_in, j)
            x_parts, mixed, pooled = (x,), None, (u, z, j, mod)
        else:
            u, z, jp, mod_pool = pooled
            lg_f = jax.nn.log_sigmoid(ret_decay_fwd[j].astype(F32))
            lg_b = jax.nn.log_sigmoid(ret_decay_bwd[j].astype(F32))
            gn = ret_gn_g[j].reshape(1, RET_V_WIDTH)
            x, q, kt, v, z = _pool_ret(u, z, pool_w_grp, pool_scale[jp].reshape(1, D_MODEL), pool_w_out, jp,
                                       x_parts[0], mod_pool, g, mod, ret_w_in, j)
            x_parts = (x,)
            decay, row, col, cdec = _ret_tables(lg_f, lg_b)
            y_ctx, new_sf, new_sb = _ret_ctx(q, kt, v, z, gn, decay, row)
            y_parts = (y_ctx, _ret_lat(q, kt, v, z, gn, decay, row, col, cdec,
                                       state_fwd[:, j], state_bwd[:, j]))
            mixed = Mixed(y_parts, ret_w_out, j, mod)
    y_prompt, y_sample = _out_proj_final(mixed, x_parts[0], final_g.reshape(1, D_MODEL))
    new_k, new_v = caches
    return (y_prompt.reshape(BATCH, SEQ, D_MODEL), y_sample.reshape(DEC_BATCH, DEC_SEQ, D_MODEL),
            from_kernel(new_k), from_kernel(new_v), new_sf[:, None], new_sb[:, None])
```

```python
import functools
import itertools
from typing import NamedTuple

import jax
import jax.numpy as jnp
from jax import lax
from jax.experimental import pallas as pl
from jax.experimental.pallas import tpu as pltpu

F32 = jnp.float32
BF16 = jnp.bfloat16

D_MODEL = 1024
BATCH = 16
SEQ = 256
DEPTH = 4
DEC_BATCH = 2
DEC_SEQ = 2048
PAST_LEN = 512
GRID_W = 64
N_MIXERS = 3
ATTN_HEADS = 16
ATTN_KV_HEADS = 4
ATTN_HEAD_DIM = 64
ATTN_GROUP = 4
ATTN_WIDTH = 1024
ATTN_KV_WIDTH = 256
WINDOW = 128
ROPE_BASE = 10000.0
POOL_WINDOWS = (2, 4, 8, 16)
POOL_GROUP_DIM = 256
RET_HEADS = 4
RET_KEY_DIM = 256
RET_VAL_DIM = 512
RET_QK_WIDTH = 1024
RET_V_WIDTH = 2048
EPS = 1e-6
NEG_INF = -1e30
LOG2_E = 1.4426950408889634

N_PROMPT_TOK = BATCH * SEQ
N_SAMPLE_TOK = DEC_BATCH * DEC_SEQ
N_TOK = N_PROMPT_TOK + N_SAMPLE_TOK
N_COND = 8
LANES = 128
MXU_N = 256
Q_BLOCK = 128
Q_SUB = 4
RET_CHUNK = 256
RET_TAB_ROWS = 8
RET_CTX_SEQS = 2
CTX_ATTN_SEQS = 2
POOL_HALO = 8
POOL_SUB = SEQ
PROJ_SUB = 256
WEIGHT_STAGE_COLS = 512
VMEM_LIMIT = 48 * 1024 * 1024
FUSED_VMEM_LIMIT = 58 * 1024 * 1024


class Tiling(NamedTuple):
    tm: int

    @property
    def n_tiles(self):
        return N_TOK // self.tm

    @property
    def n_prompt(self):
        return N_PROMPT_TOK // self.tm

    @property
    def per_dec_seq(self):
        return DEC_SEQ // self.tm

    def cond(self, i):
        return jnp.where(i < self.n_prompt, 0, 1 + (i - self.n_prompt) // self.per_dec_seq)

    def seq_tile(self, i):
        return jnp.where(i < self.n_prompt, 0, (i - self.n_prompt) % self.per_dec_seq)

    def rows(self, width):
        return pl.BlockSpec((self.tm, width), lambda i: (i, 0))

    def prompt_rows(self, width):
        return pl.BlockSpec((self.tm, width), lambda i: (jnp.minimum(i, self.n_prompt - 1), 0))

    def sample_rows(self, width):
        return pl.BlockSpec((self.tm, width), lambda i: (jnp.maximum(i - self.n_prompt, 0), 0))

    def mod(self, part):
        return pl.BlockSpec((None, 1, D_MODEL), lambda i: (self.cond(i), 0, part))


PROJ = Tiling(512)
FINAL = Tiling(1024)


def _silu(z):
    hz = 0.5 * z
    return hz + hz * jnp.tanh(hz)


def _dot(a, b):
    return jnp.dot(a, b, preferred_element_type=F32)


def _dot_nt(a, b):
    return lax.dot_general(a, b, (((1,), (1,)), ((), ())), preferred_element_type=F32)


def _params(*sem, vmem=None):
    return pltpu.CompilerParams(dimension_semantics=sem, vmem_limit_bytes=vmem)


def _const_spec(shape):
    nd = len(shape)
    return pl.BlockSpec(shape, lambda *_: (0,) * nd, pipeline_mode=pl.Buffered(1))


def _part_specs(t, parts, width):
    if len(parts) == 1:
        return [t.rows(width)]
    return [t.prompt_rows(width), t.sample_rows(width)]


def _read_parts(t, refs, rows=slice(None)):
    if len(refs) == 1:
        return refs[0][rows, :]
    return jnp.where(pl.program_id(0) < t.n_prompt, refs[0][rows, :], refs[1][rows, :])


def _ada_kernel(cond_ref, w_ref, b_ref, o_ref):
    s = _silu(cond_ref[...])
    w = w_ref[...]
    rows = [jnp.sum(s[:, c:c + 1] * w, axis=0, keepdims=True) + b_ref[...] for c in range(1 + DEC_BATCH)]
    rows.append(jnp.zeros((N_COND - len(rows), w.shape[1]), F32))
    o_ref[...] = jnp.concatenate(rows, axis=0)


def _ada_table(cond, ada_w, ada_b):
    tn = 3 * D_MODEL // 2
    return pl.pallas_call(
        _ada_kernel,
        out_shape=jax.ShapeDtypeStruct((DEPTH, N_COND, 3 * D_MODEL), F32),
        grid=(DEPTH, 3 * D_MODEL // tn),
        in_specs=[
            pl.BlockSpec((D_MODEL, N_COND), lambda l, n: (0, 0)),
            pl.BlockSpec((None, D_MODEL, tn), lambda l, n: (l, 0, n)),
            pl.BlockSpec((None, 1, tn), lambda l, n: (l, 0, n)),
        ],
        out_specs=pl.BlockSpec((None, N_COND, tn), lambda l, n: (l, 0, n)),
        compiler_params=_params("arbitrary", "arbitrary", vmem=VMEM_LIMIT),
        name="ada_table",
    )(cond, ada_w, ada_b.reshape(DEPTH, 1, 3 * D_MODEL))


def _rms(x, g):
    return x * lax.rsqrt(jnp.mean(x * x, axis=-1, keepdims=True) + EPS) * g


def _norm_mod(x, g_ref, sh_ref, sc_ref):
    return (_rms(x, g_ref[...]) * (1.0 + sc_ref[...]) + sh_ref[...]).astype(BF16)


def _layer_spec(w, j):
    nd = w.ndim - 1
    return pl.BlockSpec((None,) + w.shape[1:], lambda *_: (j,) + (0,) * nd, pipeline_mode=pl.Buffered(1))


def _cast_weight_once(w_ref, wb_ref):
    @pl.when(pl.program_id(0) == 0)
    def _():
        rows = wb_ref.shape[-2]
        for r in range(0, rows, MXU_N):
            wb_ref[..., r:r + MXU_N, :] = w_ref[..., r:r + MXU_N, :].astype(BF16)


class Mixed(NamedTuple):
    y_parts: tuple
    w_out: jax.Array
    j: int
    mod: jax.Array


def _residual_specs(t, mixed, x_parts):
    k = mixed.w_out.shape[1]
    return (_part_specs(t, mixed.y_parts, k) + [_layer_spec(mixed.w_out, mixed.j)]
            + _part_specs(t, x_parts, D_MODEL) + [t.mod(2)])


def _residual_args(mixed, x_parts):
    return (*mixed.y_parts, mixed.w_out, *x_parts, mixed.mod)


def _run_halves(t, x_refs, res, g_ref, sh_ref, sc_ref, xo_ref, wo_ref, h_ref, proj_steps):
    n_sub = t.tm // PROJ_SUB
    rows_of = lambda sub: slice(sub * PROJ_SUB, (sub + 1) * PROJ_SUB)

    if res is not None:
        y_refs, gate_ref = res
        for sub in range(n_sub):
            rows = rows_of(sub)
            xo_ref[rows, :] = (_read_parts(t, x_refs, rows)
                               + gate_ref[...] * _dot(_read_parts(t, y_refs, rows), wo_ref[...]))

    def normalise(sub):
        rows = rows_of(sub)
        x = xo_ref[rows, :] if res is not None else _read_parts(t, x_refs, rows)
        h_ref[rows, :] = _norm_mod(x, g_ref, sh_ref, sc_ref)

    normalise(0)
    for sub in range(n_sub):
        side = [functools.partial(normalise, sub + 1)] if sub + 1 < n_sub else []
        _run_interleaved(proj_steps(sub), side)


def _rep4(a, h):
    half = a[:, (h // 2) * LANES:(h // 2 + 1) * LANES]
    lane = lax.broadcasted_iota(jnp.int32, half.shape, 1)
    keep = (lane < ATTN_HEAD_DIM) if h % 2 == 0 else (lane >= ATTN_HEAD_DIM)
    m = jnp.where(keep, half, 0.0)
    s = m + pltpu.roll(m, ATTN_HEAD_DIM, 1)
    return jnp.concatenate([s, s], axis=1)


def _attn_in_kernel(*refs, j, n_y, n_x, n_alias):
    t = PROJ
    n_res = n_y + n_x + 2 if n_y else n_x
    res_refs, refs = refs[:n_res], refs[n_res:]
    (g_ref, sh_ref, sc_ref, wf_ref, cos_ref, sin_ref), refs = refs[:6], refs[6 + n_alias:]
    if n_y:
        xo_ref, q_ref, k4_ref, v4_ref, z_ref, kc_ref, vc_ref, wo_ref, w_ref, h_ref, kv_ref = refs
        y_refs, wof_ref, x_refs, gate_ref = (res_refs[:n_y], res_refs[n_y], res_refs[n_y + 1:n_y + 1 + n_x],
                                             res_refs[n_y + 1 + n_x])
        _cast_weight_once(wof_ref, wo_ref)
        res = (y_refs, gate_ref)
    else:
        q_ref, k4_ref, v4_ref, z_ref, kc_ref, vc_ref, w_ref, h_ref, kv_ref = refs
        x_refs, res, xo_ref, wo_ref = res_refs, None, None, None
    i = pl.program_id(0)
    _cast_weight_once(wf_ref, w_ref)
    lane = lax.broadcasted_iota(jnp.int32, (PROJ_SUB, LANES), 1)
    first = (lane % (ATTN_HEAD_DIM // 2)) < ATTN_HEAD_DIM // 4
    scale = ATTN_HEAD_DIM ** -0.5 * LOG2_E
    k_lo = ATTN_WIDTH
    v_lo = ATTN_WIDTH + ATTN_KV_WIDTH
    z_lo = ATTN_WIDTH + 2 * ATTN_KV_WIDTH

    def proj_steps(sub):
        rows = slice(sub * PROJ_SUB, (sub + 1) * PROJ_SUB)

        def rope(a):
            rot = jnp.where(first, pltpu.roll(a, LANES - ATTN_HEAD_DIM // 4, 1),
                            pltpu.roll(a, ATTN_HEAD_DIM // 4, 1))
            return a * cos_ref[rows, :] + rot * sin_ref[rows, :]

        def rope_wide(a):
            return jnp.concatenate(
                [rope(a[:, s * LANES:(s + 1) * LANES]) for s in range(MXU_N // LANES)], axis=1)

        def proj(lo):
            return _dot(h_ref[rows, :], w_ref[:, lo:lo + MXU_N])

        def q_chunk(c):
            cols = slice(c * MXU_N, (c + 1) * MXU_N)
            q_ref[rows, cols] = (rope_wide(proj(c * MXU_N)) * scale).astype(BF16)

        def keys():
            k = rope_wide(proj(k_lo))
            kv_ref[rows, :ATTN_KV_WIDTH] = k
            for hh in range(ATTN_KV_HEADS):
                k4_ref[rows, hh * MXU_N:(hh + 1) * MXU_N] = _rep4(k, hh).astype(BF16)

        def values():
            v = proj(v_lo)
            kv_ref[rows, ATTN_KV_WIDTH:] = v
            for hh in range(ATTN_KV_HEADS):
                v4_ref[rows, hh * MXU_N:(hh + 1) * MXU_N] = _rep4(v, hh).astype(BF16)

        def z_chunk(c):
            cols = slice(c * MXU_N, (c + 1) * MXU_N)
            z_ref[rows, cols] = _silu(proj(z_lo + c * MXU_N)).astype(BF16)

        n_chunks = ATTN_WIDTH // MXU_N
        return ([functools.partial(q_chunk, c) for c in range(n_chunks)] + [keys, values]
                + [functools.partial(z_chunk, c) for c in range(n_chunks)])

    _run_halves(t, x_refs, res, g_ref, sh_ref, sc_ref, xo_ref, wo_ref, h_ref, proj_steps)

    @pl.when(i < t.n_prompt)
    def _():
        for s in range(t.tm // SEQ):
            kt = kv_ref[s * SEQ:(s + 1) * SEQ, :ATTN_KV_WIDTH].T
            vt = kv_ref[s * SEQ:(s + 1) * SEQ, ATTN_KV_WIDTH:].T
            if n_alias:
                kc_ref[s] = kt
                vc_ref[s] = vt
            else:
                for l in range(kc_ref.shape[1]):
                    kc_ref[s, l] = kt if l == j else jnp.zeros_like(kt)
                    vc_ref[s, l] = vt if l == j else jnp.zeros_like(vt)


def _attn_in(mixed, x_parts, g, mod, w_all, j, cos_t, sin_t, caches):
    t = PROJ
    n_in = 2 * ATTN_WIDTH + 2 * ATTN_KV_WIDTH
    per = t.tm // SEQ
    rope_spec = pl.BlockSpec(
        (t.tm, LANES), lambda i: (jnp.where(i < t.n_prompt, 0, 1 + t.seq_tile(i)), 0))
    wide = jax.ShapeDtypeStruct((N_TOK, ATTN_WIDTH), BF16)
    n_attn = w_all.shape[0]
    cache = jax.ShapeDtypeStruct((BATCH, n_attn, ATTN_KV_WIDTH, SEQ), F32)
    if caches:
        cache_spec = pl.BlockSpec((per, None, ATTN_KV_WIDTH, SEQ),
                                  lambda i: (jnp.minimum(i, t.n_prompt - 1), j, 0, 0))
    else:
        cache_spec = pl.BlockSpec((per, n_attn, ATTN_KV_WIDTH, SEQ),
                                  lambda i: (jnp.minimum(i, t.n_prompt - 1), 0, 0, 0))
    n_x = len(x_parts)
    if mixed is None:
        n_y, res_specs, res_args, res_out, res_ospecs, res_scratch = 0, _part_specs(t, x_parts, D_MODEL), x_parts, (), (), []
    else:
        n_y = len(mixed.y_parts)
        res_specs, res_args = _residual_specs(t, mixed, x_parts), _residual_args(mixed, x_parts)
        res_out = (jax.ShapeDtypeStruct((N_TOK, D_MODEL), F32),)
        res_ospecs = (t.rows(D_MODEL),)
        res_scratch = [pltpu.VMEM(mixed.w_out.shape[1:], BF16)]
    n_front = len(res_specs) + 6
    return pl.pallas_call(
        functools.partial(_attn_in_kernel, j=j, n_y=n_y, n_x=n_x, n_alias=len(caches)),
        out_shape=res_out + (wide, wide, wide, wide, cache, cache),
        grid=(t.n_tiles,),
        in_specs=res_specs + [
            _const_spec((1, D_MODEL)), t.mod(0), t.mod(1), _layer_spec(w_all, j),
            rope_spec, rope_spec] + [pl.BlockSpec(memory_space=pl.ANY)] * len(caches),
        out_specs=res_ospecs + (t.rows(ATTN_WIDTH),) * 4 + (cache_spec,) * 2,
        scratch_shapes=res_scratch + [pltpu.VMEM((D_MODEL, n_in), BF16), pltpu.VMEM((t.tm, D_MODEL), BF16),
                                      pltpu.VMEM((t.tm, 2 * ATTN_KV_WIDTH), F32)],
        input_output_aliases={n_front + c: len(res_out) + 4 + c for c in range(len(caches))},
        compiler_params=_params("arbitrary", vmem=FUSED_VMEM_LIMIT),
        name="attn_in",
    )(*res_args, g, mod, mod, w_all, cos_t, sin_t, *caches)


def _stack_group_queries(q):
    qf = q.astype(F32)
    chunk = lax.broadcasted_iota(jnp.int32, qf.shape, 1) // ATTN_HEAD_DIM
    return jnp.concatenate(
        [jnp.where(chunk == g, qf, 0.0) for g in range(ATTN_GROUP)], axis=0).astype(BF16)


def _gather_group_outputs(o, rows):
    chunk = lax.broadcasted_iota(jnp.int32, (rows, MXU_N), 1) // ATTN_HEAD_DIM
    acc = jnp.zeros((rows, MXU_N), F32)
    for g in range(ATTN_GROUP):
        acc = acc + jnp.where(chunk == g, o[g * rows:(g + 1) * rows], 0.0)
    return acc


def _sink_column(sink_ref, h, rows):
    grp = lax.broadcasted_iota(jnp.int32, (ATTN_GROUP * rows, 1), 0) // rows
    col = jnp.zeros((ATTN_GROUP * rows, 1), F32)
    for g in range(ATTN_GROUP):
        col = jnp.where(grp == g, sink_ref[h * ATTN_GROUP + g] * LOG2_E, col)
    return col


def _chunk_rows(dtype):
    chunk = lax.broadcasted_iota(jnp.int32, (1, MXU_N), 1) // ATTN_HEAD_DIM
    return [(chunk == g).astype(F32).astype(dtype) for g in range(ATTN_GROUP)]


def _block_diag_rows(x4):
    return jnp.concatenate([x4 * m for m in _chunk_rows(x4.dtype)], axis=0)


def _ctx_attn_kernel(sink_ref, q_ref, k4_ref, v4_ref, z_ref, y_ref):
    chunk = lax.broadcasted_iota(jnp.int32, (SEQ, MXU_N), 1) // ATTN_HEAD_DIM
    for b, h in itertools.product(range(CTX_ATTN_SEQS), range(ATTN_KV_HEADS)):
        rows = slice(b * SEQ, (b + 1) * SEQ)
        cols = slice(h * MXU_N, (h + 1) * MXU_N)
        s = _dot_nt(q_ref[rows, cols], _block_diag_rows(k4_ref[rows, cols]))
        inv = jnp.zeros((SEQ, MXU_N), F32)
        probs = []
        for g in range(ATTN_GROUP):
            sg = s[:, g * SEQ:(g + 1) * SEQ]
            sk = sink_ref[h * ATTN_GROUP + g] * LOG2_E
            m = jnp.maximum(jnp.max(sg, axis=1, keepdims=True), sk)
            e = jnp.exp2(sg - m)
            den = jnp.sum(e, axis=1, keepdims=True) + jnp.exp2(sk - m)
            probs.append(e.astype(BF16))
            inv = jnp.where(chunk == g, 1.0 / den, inv)
        o = _dot(jnp.concatenate(probs, axis=1), _block_diag_rows(v4_ref[rows, cols]))
        y_ref[rows, cols] = (o * inv * z_ref[rows, cols].astype(F32)).astype(BF16)


def _ctx_attn(sink, q, k4, v4, z):
    spec = pl.BlockSpec((CTX_ATTN_SEQS * SEQ, ATTN_WIDTH), lambda b: (b, 0))
    return pl.pallas_call(
        _ctx_attn_kernel,
        out_shape=jax.ShapeDtypeStruct((N_PROMPT_TOK, ATTN_WIDTH), BF16),
        grid=(BATCH // CTX_ATTN_SEQS,),
        in_specs=[pl.BlockSpec(memory_space=pltpu.SMEM), spec, spec, spec, spec],
        out_specs=spec,
        compiler_params=_params("arbitrary"),
        name="ctx_attn",
    )(sink, q, k4, v4, z)


LAT_STEP = Q_SUB * Q_BLOCK
LAT_PER_SEQ = DEC_SEQ // LAT_STEP
LAT_BLOCKS = DEC_BATCH * ATTN_KV_HEADS * LAT_PER_SEQ


def _lat_block(blk):
    return (blk // (ATTN_KV_HEADS * LAT_PER_SEQ), (blk // LAT_PER_SEQ) % ATTN_KV_HEADS, blk % LAT_PER_SEQ)


def _band_rows(qb):
    return pl.ds(pl.multiple_of(qb * Q_BLOCK, Q_BLOCK), 3 * Q_BLOCK)


def _pad_sequence(dst, src):
    zeros = jnp.zeros((Q_BLOCK, MXU_N), BF16)
    dst[0:Q_BLOCK, :] = zeros
    dst[Q_BLOCK:Q_BLOCK + DEC_SEQ, :] = src[...]
    dst[Q_BLOCK + DEC_SEQ:, :] = zeros


def _lat_attn_kernel(sink_ref, q_ref, k4_ref, v4_ref, kc_ref, vc_ref, z_ref, y_ref,
                     kp, vp, k4c, v4c, sc_a, sl_a, sc_b, sl_b, e_scr):
    t = pl.program_id(0)
    _, _, n1 = _lat_block(jnp.minimum(t, LAT_BLOCKS - 1))
    _, h0, n0 = _lat_block(jnp.maximum(t - 1, 0))

    @pl.when(t == 0)
    def _():
        sc_b[...] = jnp.zeros(sc_b.shape, F32)
        sl_b[...] = jnp.zeros(sl_b.shape, F32)

    @pl.when(n1 == 0)
    def _():
        _pad_sequence(kp, k4_ref)
        k4c[...] = jnp.concatenate([kc_ref[...]] * ATTN_GROUP, axis=0).astype(BF16)

    @pl.when(n0 == 0)
    def _():
        _pad_sequence(vp, v4_ref)
        v4c[...] = jnp.concatenate([vc_ref[...]] * ATTN_GROUP, axis=0).T.astype(BF16)

    @pl.when(t % 2 == 0)
    def _():
        _lat_stages(sink_ref, q_ref, z_ref, y_ref, kp, vp, k4c, v4c, e_scr, n1, h0, n0,
                    sc_a, sl_a, sc_b, sl_b)

    @pl.when(t % 2 == 1)
    def _():
        _lat_stages(sink_ref, q_ref, z_ref, y_ref, kp, vp, k4c, v4c, e_scr, n1, h0, n0,
                    sc_b, sl_b, sc_a, sl_a)


def _lat_stages(sink_ref, q_ref, z_ref, y_ref, kp, vp, k4c, v4c, e_scr, n1, h0, n0,
                sc_w, sl_w, sc_r, sl_r):
    n_blocks = DEC_SEQ // Q_BLOCK
    rows4 = ATTN_GROUP * Q_BLOCK
    r = lax.broadcasted_iota(jnp.int32, (rows4, Q_BLOCK), 0) % Q_BLOCK
    c = lax.broadcasted_iota(jnp.int32, (rows4, Q_BLOCK), 1)
    in_left = c >= r
    in_right = c <= r
    sk = _sink_column(sink_ref, h0, Q_BLOCK)

    def scores(sub):
        qs = _stack_group_queries(q_ref[sub * Q_BLOCK:(sub + 1) * Q_BLOCK, :])
        sc_w[sub] = _dot(qs, k4c[...])
        sl_w[sub] = _dot_nt(qs, kp[_band_rows(n1 * Q_SUB + sub), :])

    def softmax(sub):
        qb = n0 * Q_SUB + sub
        s_ctx = sc_r[sub]
        s_lat = sl_r[sub]
        blocks = [
            s_ctx,
            jnp.where(jnp.logical_and(in_left, qb > 0), s_lat[:, :Q_BLOCK], NEG_INF),
            s_lat[:, Q_BLOCK:2 * Q_BLOCK],
            jnp.where(jnp.logical_and(in_right, qb < n_blocks - 1), s_lat[:, 2 * Q_BLOCK:], NEG_INF),
        ]
        slabs = [s[:, c:c + LANES] for s in blocks for c in range(0, s.shape[1], LANES)]
        m = jnp.maximum(sk, jnp.max(functools.reduce(jnp.maximum, slabs), axis=1, keepdims=True))
        es = [jnp.exp2(s - m) for s in slabs]
        for c, e in enumerate(es):
            e_scr[sub, :, c * LANES:(c + 1) * LANES] = e.astype(BF16)
        return jnp.exp2(sk - m) + jnp.sum(functools.reduce(jnp.add, es), axis=1, keepdims=True)

    def finish(sub, den):
        qrows = slice(sub * Q_BLOCK, (sub + 1) * Q_BLOCK)
        o = (_dot(e_scr[sub, :, :PAST_LEN], v4c[...])
             + _dot(e_scr[sub, :, PAST_LEN:], vp[_band_rows(n0 * Q_SUB + sub), :])) * (1.0 / den)
        acc = _gather_group_outputs(o, Q_BLOCK)
        y_ref[qrows, :] = (acc * z_ref[qrows, :].astype(F32)).astype(BF16)

    for sub in range(Q_SUB):
        den = softmax(sub)
        scores(sub)
        finish(sub, den)


def _lat_attn(sink, q, k4, v4, kc, vc, j, z):
    padded = DEC_SEQ + 2 * Q_BLOCK
    rows4 = ATTN_GROUP * Q_BLOCK
    row0 = N_PROMPT_TOK // LAT_STEP
    seq0 = N_PROMPT_TOK // DEC_SEQ
    stage1 = lambda t: _lat_block(jnp.minimum(t, LAT_BLOCKS - 1))
    stage2 = lambda t: _lat_block(jnp.maximum(t - 1, 0))

    def spec(shape, stage, index):
        return pl.BlockSpec(shape, lambda t: index(*stage(t)))

    tile = (LAT_STEP, MXU_N)
    seq = (DEC_SEQ, MXU_N)
    ctx = (None, None, None, ATTN_HEAD_DIM, PAST_LEN)
    return pl.pallas_call(
        _lat_attn_kernel,
        out_shape=jax.ShapeDtypeStruct((N_SAMPLE_TOK, ATTN_WIDTH), BF16),
        grid=(LAT_BLOCKS + 1,),
        in_specs=[pl.BlockSpec(memory_space=pltpu.SMEM),
                  spec(tile, stage1, lambda b, h, n: (row0 + b * LAT_PER_SEQ + n, h)),
                  spec(seq, stage1, lambda b, h, n: (seq0 + b, h)),
                  spec(seq, stage2, lambda b, h, n: (seq0 + b, h)),
                  spec(ctx, stage1, lambda b, h, n: (b, j, h, 0, 0)),
                  spec(ctx, stage2, lambda b, h, n: (b, j, h, 0, 0)),
                  spec(tile, stage2, lambda b, h, n: (row0 + b * LAT_PER_SEQ + n, h))],
        out_specs=spec(tile, stage2, lambda b, h, n: (b * LAT_PER_SEQ + n, h)),
        scratch_shapes=[pltpu.VMEM((padded, MXU_N), BF16), pltpu.VMEM((padded, MXU_N), BF16),
                        pltpu.VMEM((MXU_N, PAST_LEN), BF16), pltpu.VMEM((PAST_LEN, MXU_N), BF16),
                        pltpu.VMEM((Q_SUB, rows4, PAST_LEN), F32),
                        pltpu.VMEM((Q_SUB, rows4, 3 * Q_BLOCK), F32),
                        pltpu.VMEM((Q_SUB, rows4, PAST_LEN), F32),
                        pltpu.VMEM((Q_SUB, rows4, 3 * Q_BLOCK), F32),
                        pltpu.VMEM((Q_SUB, rows4, PAST_LEN + 3 * Q_BLOCK), BF16)],
        compiler_params=_params("arbitrary", vmem=VMEM_LIMIT),
        name="lat_attn",
    )(sink, q, k4, v4, kc, vc, z)


def _out_final_kernel(yp_ref, ys_ref, wf_ref, x_ref, gate_ref, fg_ref, op_ref, os_ref, w_ref):
    t = FINAL
    i = pl.program_id(0)
    _cast_weight_once(wf_ref, w_ref)
    y = _read_parts(t, (yp_ref, ys_ref))
    r = _rms(x_ref[...] + gate_ref[...] * _dot(y, w_ref[...]), fg_ref[...])

    @pl.when(i < t.n_prompt)
    def _():
        op_ref[...] = r

    @pl.when(i >= t.n_prompt)
    def _():
        os_ref[...] = r


def _out_proj_final(mixed, x, final_g):
    t = FINAL
    return pl.pallas_call(
        _out_final_kernel,
        out_shape=(jax.ShapeDtypeStruct((N_PROMPT_TOK, D_MODEL), F32),
                   jax.ShapeDtypeStruct((N_SAMPLE_TOK, D_MODEL), F32)),
        grid=(t.n_tiles,),
        in_specs=_residual_specs(t, mixed, (x,)) + [_const_spec((1, D_MODEL))],
        out_specs=(t.prompt_rows(D_MODEL), t.sample_rows(D_MODEL)),
        scratch_shapes=[pltpu.VMEM(mixed.w_out.shape[1:], BF16)],
        compiler_params=_params("arbitrary", vmem=VMEM_LIMIT),
        name="out_proj_final",
    )(*_residual_args(mixed, (x,)), final_g)


def _pool_in_kernel(*refs, n_y, n_x):
    t = PROJ
    n_res = n_y + n_x + 2
    res_refs, refs = refs[:n_res], refs[n_res:]
    g_ref, sh_ref, sc_ref, wf_ref, xo_ref, u_ref, z_ref, wo_ref, w_ref, h_ref = refs
    y_refs, wof_ref, x_refs, gate_ref = (res_refs[:n_y], res_refs[n_y], res_refs[n_y + 1:n_y + 1 + n_x],
                                         res_refs[n_y + 1 + n_x])
    _cast_weight_once(wof_ref, wo_ref)
    _cast_weight_once(wf_ref, w_ref)

    def proj_steps(sub):
        rows = slice(sub * PROJ_SUB, (sub + 1) * PROJ_SUB)

        def chunk(c):
            def step():
                cols = slice(c * MXU_N, (c + 1) * MXU_N)
                u_ref[rows, cols] = _dot(h_ref[rows, :], w_ref[:, cols])
            return step

        def gate_chunk(c):
            def step():
                a = _dot(h_ref[rows, :], w_ref[:, D_MODEL + c * MXU_N:D_MODEL + (c + 1) * MXU_N])
                z_ref[rows, c * MXU_N:(c + 1) * MXU_N] = _silu(a).astype(BF16)
            return step

        n = D_MODEL // MXU_N
        return [chunk(c) for c in range(n)] + [gate_chunk(c) for c in range(n)]

    _run_halves(t, x_refs, (y_refs, gate_ref), g_ref, sh_ref, sc_ref, xo_ref, wo_ref, h_ref, proj_steps)


def _pool_in(mixed, x_parts, g, mod, w_all, j):
    t = PROJ
    slab = jax.ShapeDtypeStruct((N_TOK, D_MODEL), F32)
    return pl.pallas_call(
        functools.partial(_pool_in_kernel, n_y=len(mixed.y_parts), n_x=len(x_parts)),
        out_shape=(slab, slab, jax.ShapeDtypeStruct((N_TOK, D_MODEL), BF16)),
        grid=(t.n_tiles,),
        in_specs=_residual_specs(t, mixed, x_parts) + [
            _const_spec((1, D_MODEL)), t.mod(0), t.mod(1), _layer_spec(w_all, j)],
        out_specs=(t.rows(D_MODEL),) * 3,
        scratch_shapes=[pltpu.VMEM(mixed.w_out.shape[1:], BF16), pltpu.VMEM((D_MODEL, 2 * D_MODEL), BF16),
                        pltpu.VMEM((t.tm, D_MODEL), BF16)],
        compiler_params=_params("arbitrary", vmem=FUSED_VMEM_LIMIT),
        name="pool_in",
    )(*_residual_args(mixed, x_parts), g, mod, mod, w_all)


def _split_bf16(a):
    hi = a.astype(BF16)
    return hi, (a - hi.astype(F32)).astype(BF16)


def _band_ones(shape, lo, hi):
    d = lax.broadcasted_iota(jnp.int32, shape, 1) - lax.broadcasted_iota(jnp.int32, shape, 0)
    return jnp.logical_and(d >= lo, d <= hi).astype(F32).astype(BF16)


def _stream_cast_weight(w_hbm, wb_ref, stage_ref, sem):
    width = stage_ref.shape[-1]
    n_chunks = wb_ref.shape[-1] // width

    def copy(c):
        return pltpu.make_async_copy(w_hbm.at[:, pl.ds(c * width, width)], stage_ref.at[c % 2], sem.at[c % 2])

    copy(0).start()
    for c in range(n_chunks):
        if c + 1 < n_chunks:
            copy(c + 1).start()
        copy(c).wait()
        wb_ref[:, c * width:(c + 1) * width] = stage_ref[c % 2].astype(BF16)


def _pool_ret_kernel(u_ref, up_ref, un_ref, z_ref, wgf_ref, ps_ref, wof_ref, x_ref, gate_ref,
                     g_ref, sh_ref, sc_ref, w_hbm,
                     xo_ref, q_ref, kt_ref, v_ref, zr_ref,
                     y_ref, wg_ref, wo_ref, band_ref, wkt_ref, w_ref, stage_ref, sem, h_ref, *, j_ret):
    t = PROJ
    i = pl.program_id(0)
    _cast_weight_once(wgf_ref, wg_ref)
    _cast_weight_once(wof_ref, wo_ref)

    @pl.when(i == 0)
    def _():
        for g, w in enumerate(POOL_WINDOWS):
            band_ref[g] = _band_ones((POOL_SUB, POOL_SUB), -(w // 2), w - 1 - w // 2)
        _stream_cast_weight(w_hbm.at[j_ret], w_ref, stage_ref, sem)
        for r in range(0, RET_QK_WIDTH, MXU_N):
            wk = w_ref[:, RET_QK_WIDTH + r:RET_QK_WIDTH + r + MXU_N]
            wkt_ref[r:r + MXU_N, :] = wk.astype(F32).T.astype(BF16)

    is_dec = i >= t.n_prompt
    st = t.seq_tile(i)
    seq_len = jnp.where(is_dec, DEC_SEQ, SEQ)
    h = POOL_HALO
    n_sub = t.tm // POOL_SUB
    def pool_steps(sub):
        rows = slice(sub * POOL_SUB, (sub + 1) * POOL_SUB)
        if sub == 0:
            before = jnp.where(jnp.logical_and(is_dec, st != 0), up_ref[...], 0.0)
        else:
            before = jnp.where(is_dec, u_ref[sub * POOL_SUB - h:sub * POOL_SUB, :], 0.0)
        if sub == n_sub - 1:
            after = jnp.where(jnp.logical_and(is_dec, st != t.per_dec_seq - 1), un_ref[...], 0.0)
        else:
            after = jnp.where(is_dec, u_ref[(sub + 1) * POOL_SUB:(sub + 1) * POOL_SUB + h, :], 0.0)
        pos0 = jnp.where(is_dec, st * t.tm + sub * POOL_SUB, 0)
        return _pool_core_steps(u_ref.at[rows], before, after, pos0, seq_len, z_ref.at[rows], ps_ref,
                                wg_ref, band_ref, y_ref.at[rows])

    def mix_steps(sub):
        rows = slice(sub * POOL_SUB, (sub + 1) * POOL_SUB)

        def residual():
            xo_ref[rows, :] = x_ref[rows, :] + gate_ref[...] * _dot(y_ref[rows, :], wo_ref[...])

        def normalise():
            h_ref[rows, :] = _norm_mod(xo_ref[rows, :], g_ref, sh_ref, sc_ref)

        return pool_steps(sub) + [residual, normalise]

    for step in mix_steps(0):
        step()
    for sub in range(n_sub):
        rows = slice(sub * POOL_SUB, (sub + 1) * POOL_SUB)
        proj = _ret_in_steps(h_ref.at[rows], w_ref, wkt_ref, q_ref.at[rows],
                             kt_ref.at[pl.ds(sub * POOL_SUB // RET_CHUNK, POOL_SUB // RET_CHUNK)],
                             v_ref.at[rows], zr_ref.at[rows])
        _run_interleaved(proj, mix_steps(sub + 1) if sub + 1 < n_sub else [])


def _pool_core_steps(u_ref, before, after, pos0, seq_len, z_ref, ps_ref, wg_ref, band_ref, y_ref):
    n = u_ref.shape[0]
    h = POOL_HALO
    halo_hi, halo_lo = _split_bf16(jnp.concatenate([before, after], axis=0))
    pos = pos0 + lax.broadcasted_iota(jnp.int32, (n, 1), 0)
    rr = lax.broadcasted_iota(jnp.int32, (2 * h, 2 * h), 0)
    cc = lax.broadcasted_iota(jnp.int32, (2 * h, 2 * h), 1)

    def pooled(g, w):
        def step():
            left = w // 2
            right = w - 1 - left
            cols = slice(g * POOL_GROUP_DIM, (g + 1) * POOL_GROUP_DIM)
            u = u_ref[:, cols]
            s = _dot(band_ref[g], u.astype(BF16))
            top = jnp.logical_and(jnp.logical_and(rr < h, cc < h), cc - h >= rr - left)
            bot = jnp.logical_and(jnp.logical_and(rr >= h, cc >= h), cc - h <= rr - 2 * h + right)
            edge = jnp.logical_or(top, bot).astype(F32).astype(BF16)
            se = _dot(edge, halo_hi[:, cols]) + _dot(edge, halo_lo[:, cols])
            s = jnp.concatenate([s[:h] + se[:h], s[h:n - h], s[n - h:] + se[h:]], axis=0)
            cnt = (jnp.minimum(pos + right + 1, seq_len) - jnp.maximum(pos - left, 0)).astype(F32)
            y_ref[:, cols] = (s / cnt - u).astype(BF16)
        return step

    def mixed(g):
        def step():
            cols = slice(g * POOL_GROUP_DIM, (g + 1) * POOL_GROUP_DIM)
            yg = _dot(y_ref[:, cols], wg_ref[g]) * ps_ref[:, cols] * z_ref[:, cols].astype(F32)
            y_ref[:, cols] = yg.astype(BF16)
        return step

    return ([pooled(g, w) for g, w in enumerate(POOL_WINDOWS)]
            + [mixed(g) for g in range(len(POOL_WINDOWS))])


def _pool_ret(u, z, wg_all, ps, wo_all, j, x, mod_pool, g, mod, w_ret_all, j_ret):
    t = PROJ
    per = t.tm // POOL_HALO
    n_halo = N_TOK // POOL_HALO
    n_in = 2 * RET_QK_WIDTH + 2 * RET_V_WIDTH
    kt_per = t.tm // RET_CHUNK
    return pl.pallas_call(
        functools.partial(_pool_ret_kernel, j_ret=j_ret),
        out_shape=(jax.ShapeDtypeStruct((N_TOK, D_MODEL), F32),
                   jax.ShapeDtypeStruct((N_TOK, RET_QK_WIDTH), BF16),
                   jax.ShapeDtypeStruct((N_TOK // RET_CHUNK, RET_QK_WIDTH, RET_CHUNK), BF16),
                   jax.ShapeDtypeStruct((N_TOK, RET_V_WIDTH), BF16),
                   jax.ShapeDtypeStruct((N_TOK, RET_V_WIDTH), BF16)),
        grid=(t.n_tiles,),
        in_specs=[
            t.rows(D_MODEL),
            pl.BlockSpec((POOL_HALO, D_MODEL), lambda i: (jnp.maximum(i * per - 1, 0), 0)),
            pl.BlockSpec((POOL_HALO, D_MODEL), lambda i: (jnp.minimum((i + 1) * per, n_halo - 1), 0)),
            t.rows(D_MODEL),
            _layer_spec(wg_all, j),
            _const_spec((1, D_MODEL)),
            _layer_spec(wo_all, j),
            t.rows(D_MODEL),
            t.mod(2),
            _const_spec((1, D_MODEL)), t.mod(0), t.mod(1), pl.BlockSpec(memory_space=pl.ANY),
        ],
        out_specs=(t.rows(D_MODEL), t.rows(RET_QK_WIDTH),
                   pl.BlockSpec((kt_per, RET_QK_WIDTH, RET_CHUNK), lambda i: (i, 0, 0)),
                   t.rows(RET_V_WIDTH), t.rows(RET_V_WIDTH)),
        scratch_shapes=[pltpu.VMEM((t.tm, D_MODEL), BF16),
                        pltpu.VMEM((len(POOL_WINDOWS), POOL_GROUP_DIM, POOL_GROUP_DIM), BF16),
                        pltpu.VMEM((D_MODEL, D_MODEL), BF16),
                        pltpu.VMEM((len(POOL_WINDOWS), POOL_SUB, POOL_SUB), BF16),
                        pltpu.VMEM((RET_QK_WIDTH, D_MODEL), BF16),
                        pltpu.VMEM((D_MODEL, n_in), BF16),
                        pltpu.VMEM((2, D_MODEL, WEIGHT_STAGE_COLS), F32),
                        pltpu.SemaphoreType.DMA((2,)),
                        pltpu.VMEM((t.tm, D_MODEL), BF16)],
        compiler_params=_params("arbitrary", vmem=FUSED_VMEM_LIMIT),
        name="pool_ret",
    )(u, u, u, z, wg_all, ps, wo_all, x, mod_pool, g, mod, mod, w_ret_all)


def _ret_in_steps(h_ref, w_ref, wkt_ref, q_ref, kt_ref, v_ref, z_ref):
    n_rows = h_ref.shape[0]

    def chunk(ref, lo, c, post=lambda a: a):
        def step():
            cols = slice(c * MXU_N, (c + 1) * MXU_N)
            ref[:, cols] = post(_dot(h_ref[...], w_ref[:, lo + c * MXU_N:lo + (c + 1) * MXU_N])).astype(BF16)
        return step

    def key_chunk(c):
        def step():
            rows = slice(c * MXU_N, (c + 1) * MXU_N)
            kt = (_dot_nt(wkt_ref[rows, :], h_ref[...]) * RET_KEY_DIM ** -0.5).astype(BF16)
            for cc in range(n_rows // RET_CHUNK):
                kt_ref[cc, rows, :] = kt[:, cc * RET_CHUNK:(cc + 1) * RET_CHUNK]
        return step

    steps = [chunk(q_ref, 0, c) for c in range(RET_QK_WIDTH // MXU_N)]
    steps += [key_chunk(c) for c in range(RET_QK_WIDTH // MXU_N)]
    steps += [chunk(v_ref, 2 * RET_QK_WIDTH, c) for c in range(RET_V_WIDTH // MXU_N)]
    steps += [chunk(z_ref, 2 * RET_QK_WIDTH + RET_V_WIDTH, c, _silu) for c in range(RET_V_WIDTH // MXU_N)]
    return steps


def _run_interleaved(main, side):
    done = 0
    for k, step in enumerate(main):
        step()
        due = (k + 1) * len(side) // len(main)
        for s in side[done:due]:
            s()
        done = due


def _pos(shape, axis):
    return lax.broadcasted_iota(jnp.int32, shape, axis).astype(F32)


def _ret_tables_kernel(lgf_ref, lgb_ref, decay_ref, row_ref, col_ref, cdec_ref):
    h = pl.program_id(0)
    lg_f = lgf_ref[h]
    lg_b = lgb_ref[h]
    c = RET_CHUNK
    diff = _pos((c, c), 0) - _pos((c, c), 1)
    fwd = jnp.where(diff >= 0, jnp.exp(jnp.maximum(diff, 0.0) * lg_f), 0.0)
    bwd = jnp.where(diff <= 0, jnp.exp(jnp.maximum(-diff, 0.0) * lg_b), 0.0)
    decay_ref[...] = fwd + bwd
    j = _pos((RET_TAB_ROWS, c), 1)
    row_ref[0] = jnp.exp((c - 1.0 - j) * lg_f)
    row_ref[1] = jnp.exp(j * lg_b)
    i = _pos((c, LANES), 0)
    col_ref[0] = jnp.exp((i + 1.0) * lg_f)
    col_ref[1] = jnp.exp((c - i) * lg_b)
    full = jnp.full((RET_TAB_ROWS, RET_VAL_DIM), float(c), F32)
    cdec_ref[0] = jnp.exp(full * lg_f)
    cdec_ref[1] = jnp.exp(full * lg_b)


def _ret_tables(lg_f, lg_b):
    smem = pl.BlockSpec(memory_space=pltpu.SMEM)
    c = RET_CHUNK
    shapes = ((c, c), (2, RET_TAB_ROWS, c), (2, c, LANES), (2, RET_TAB_ROWS, RET_VAL_DIM))
    return pl.pallas_call(
        _ret_tables_kernel,
        out_shape=tuple(jax.ShapeDtypeStruct((RET_HEADS,) + s, F32) for s in shapes),
        grid=(RET_HEADS,),
        in_specs=[smem, smem],
        out_specs=tuple(pl.BlockSpec((None,) + s, lambda h, n=len(s): (h,) + (0,) * n) for s in shapes),
        compiler_params=_params("arbitrary"),
        name="ret_tables",
    )(lg_f, lg_b)


def _group_norm_gate(o, gn, z):
    mu = jnp.mean(o, axis=-1, keepdims=True)
    var = jnp.mean(jnp.square(o - mu), axis=-1, keepdims=True)
    on = (o - mu) * lax.rsqrt(var + EPS)
    return (on * gn * z.astype(F32)).astype(BF16)


def _ret_ctx_kernel(q_ref, kt_ref, v_ref, z_ref, gn_ref, decay_ref, row_ref, y_ref, sf_ref, sb_ref):
    for s in range(RET_CTX_SEQS):
        rows = slice(s * SEQ, (s + 1) * SEQ)
        for h in range(RET_HEADS):
            kc = slice(h * RET_KEY_DIM, (h + 1) * RET_KEY_DIM)
            vc = slice(h * RET_VAL_DIM, (h + 1) * RET_VAL_DIM)
            kt = kt_ref[s, kc, :]
            v = v_ref[rows, vc]
            att = (_dot(q_ref[rows, kc], kt) * decay_ref[h]).astype(BF16)
            y_ref[rows, vc] = _group_norm_gate(_dot(att, v), gn_ref[:, vc], z_ref[rows, vc])
            ktf = kt.astype(F32)
            sf_ref[s, h] = _dot((ktf * row_ref[h, 0, 0:1, :]).astype(BF16), v)
            sb_ref[s, h] = _dot((ktf * row_ref[h, 1, 0:1, :]).astype(BF16), v)


def _ret_ctx(q, kt, v, z, gn, decay, row):
    c = RET_CHUNK
    n = RET_CTX_SEQS
    wide = lambda w: pl.BlockSpec((n * SEQ, w), lambda b: (b, 0))
    st_spec = pl.BlockSpec((n, RET_HEADS, RET_KEY_DIM, RET_VAL_DIM), lambda b: (b, 0, 0, 0))
    st_shape = jax.ShapeDtypeStruct((BATCH, RET_HEADS, RET_KEY_DIM, RET_VAL_DIM), F32)
    return pl.pallas_call(
        _ret_ctx_kernel,
        out_shape=(jax.ShapeDtypeStruct((N_PROMPT_TOK, RET_V_WIDTH), BF16), st_shape, st_shape),
        grid=(BATCH // n,),
        in_specs=[wide(RET_QK_WIDTH), pl.BlockSpec((n, RET_QK_WIDTH, c), lambda b: (b, 0, 0)),
                  wide(RET_V_WIDTH), wide(RET_V_WIDTH), _const_spec((1, RET_V_WIDTH)),
                  _const_spec((RET_HEADS, c, c)), _const_spec((RET_HEADS, 2, RET_TAB_ROWS, c))],
        out_specs=(wide(RET_V_WIDTH), st_spec, st_spec),
        compiler_params=_params("arbitrary", vmem=VMEM_LIMIT),
        name="ret_ctx",
    )(q, kt, v, z, gn, decay, row)


def _ret_lat_kernel(q_ref, kt_ref, v_ref, z_ref, gn_ref, decay_ref, row_ref, col_ref, cdec_ref,
                    s0f_ref, s0b_ref, y_ref, sf_all, sb_all, sf_acc, sb_acc):
    c = RET_CHUNK
    n_chunks = DEC_SEQ // c
    rows_of = lambda ci: pl.ds(pl.multiple_of(ci * c, c), c)

    sf_acc[...] = s0f_ref[...]
    sb_acc[...] = s0b_ref[...]

    def scan_step(i, carry):
        cf = i
        cb = n_chunks - 1 - i
        sf_all[cf] = sf_acc[...].astype(BF16)
        sb_all[cb] = sb_acc[...].astype(BF16)
        uf = _dot((kt_ref[cf].astype(F32) * row_ref[0, 0:1, :]).astype(BF16), v_ref[rows_of(cf), :])
        ub = _dot((kt_ref[cb].astype(F32) * row_ref[1, 0:1, :]).astype(BF16), v_ref[rows_of(cb), :])
        sf_acc[...] = sf_acc[...] * cdec_ref[0, 0:1, :] + uf
        sb_acc[...] = sb_acc[...] * cdec_ref[1, 0:1, :] + ub
        return carry

    lax.fori_loop(0, n_chunks, scan_step, 0, unroll=4)

    def out_step(ci, carry):
        rows = rows_of(ci)
        q = q_ref[rows, :]
        qf = q.astype(F32)
        qdec_f = jnp.concatenate([col_ref[0]] * (RET_KEY_DIM // LANES), axis=1)
        qdec_b = jnp.concatenate([col_ref[1]] * (RET_KEY_DIM // LANES), axis=1)
        att = (_dot(q, kt_ref[ci]) * decay_ref[...]).astype(BF16)
        o = (_dot(att, v_ref[rows, :])
             + _dot((qf * qdec_f).astype(BF16), sf_all[ci])
             + _dot((qf * qdec_b).astype(BF16), sb_all[ci]))
        y_ref[rows, :] = _group_norm_gate(o, gn_ref[...], z_ref[rows, :])
        return carry

    lax.fori_loop(0, n_chunks, out_step, 0, unroll=8)


def _ret_lat(q, kt, v, z, gn, decay, row, col, cdec, s0f, s0b):
    c = RET_CHUNK
    n_chunks = DEC_SEQ // c
    row0 = N_PROMPT_TOK // DEC_SEQ
    qk_spec = pl.BlockSpec((DEC_SEQ, RET_KEY_DIM), lambda b, h: (row0 + b, h))
    v_spec = pl.BlockSpec((DEC_SEQ, RET_VAL_DIM), lambda b, h: (row0 + b, h))
    st_spec = pl.BlockSpec((None, None, RET_KEY_DIM, RET_VAL_DIM), lambda b, h: (b, h, 0, 0))
    tab = lambda *s: pl.BlockSpec((None,) + s, lambda b, h: (h,) + (0,) * len(s))
    states = pltpu.VMEM((n_chunks, RET_KEY_DIM, RET_VAL_DIM), BF16)
    acc = pltpu.VMEM((RET_KEY_DIM, RET_VAL_DIM), F32)
    return pl.pallas_call(
        _ret_lat_kernel,
        out_shape=jax.ShapeDtypeStruct((N_SAMPLE_TOK, RET_V_WIDTH), BF16),
        grid=(DEC_BATCH, RET_HEADS),
        in_specs=[qk_spec,
                  pl.BlockSpec((n_chunks, RET_KEY_DIM, c), lambda b, h: (row0 + b, h, 0)),
                  v_spec, v_spec, pl.BlockSpec((1, RET_VAL_DIM), lambda b, h: (0, h)),
                  tab(c, c), tab(2, RET_TAB_ROWS, c), tab(2, c, LANES), tab(2, RET_TAB_ROWS, RET_VAL_DIM),
                  st_spec, st_spec],
        out_specs=pl.BlockSpec((DEC_SEQ, RET_VAL_DIM), lambda b, h: (b, h)),
        scratch_shapes=[states, states, acc, acc],
        compiler_params=_params("arbitrary", "arbitrary", vmem=VMEM_LIMIT),
        name="ret_lat",
    )(q, kt, v, z, gn, decay, row, col, cdec, s0f, s0b)


def _rope_tables(tm):
    n_rows = DEC_SEQ // GRID_W
    rows = jnp.repeat(jnp.arange(n_rows), GRID_W).astype(F32)
    cols = jnp.tile(jnp.arange(GRID_W), n_rows).astype(F32)
    half = ATTN_HEAD_DIM // 4
    inv = ROPE_BASE ** (-jnp.arange(half, dtype=F32) / half)
    ang_r = rows[:, None] * inv[None, :]
    ang_c = cols[:, None] * inv[None, :]
    cos = jnp.concatenate([jnp.cos(ang_r), jnp.cos(ang_r), jnp.cos(ang_c), jnp.cos(ang_c)], axis=-1)
    sin = jnp.concatenate([-jnp.sin(ang_r), jnp.sin(ang_r), -jnp.sin(ang_c), jnp.sin(ang_c)], axis=-1)
    cos = jnp.concatenate([jnp.ones((tm, ATTN_HEAD_DIM), F32), cos], axis=0)
    sin = jnp.concatenate([jnp.zeros((tm, ATTN_HEAD_DIM), F32), sin], axis=0)
    return jnp.tile(cos, (1, 2)), jnp.tile(sin, (1, 2))


def kernel(x_prompt, x_sample, cache_k, cache_v, state_fwd, state_bwd, c, c_ctx, norm_g, ada_w, ada_b, attn_w_in, attn_w_out, attn_sink, pool_w_in, pool_w_grp, pool_scale, pool_w_out, ret_w_in, ret_decay_fwd, ret_decay_bwd, ret_gn_g, ret_w_out, final_g):
    x_parts = (x_prompt.reshape(N_PROMPT_TOK, D_MODEL), x_sample.reshape(N_SAMPLE_TOK, D_MODEL))
    cond = jnp.concatenate([c_ctx[None, :], c,
                            jnp.zeros((N_COND - 1 - DEC_BATCH, D_MODEL), F32)], axis=0)
    mods = _ada_table(cond.T, ada_w, ada_b).reshape(DEPTH, N_COND, 1, 3 * D_MODEL)
    cos_t, sin_t = _rope_tables(PROJ.tm)

    to_kernel = lambda a: jnp.transpose(a, (0, 1, 3, 4, 2))
    from_kernel = lambda a: jnp.transpose(
        a.reshape(a.shape[0], a.shape[1], ATTN_KV_HEADS, ATTN_HEAD_DIM, a.shape[3]), (0, 1, 4, 2, 3))
    ctx_k, ctx_v = to_kernel(cache_k), to_kernel(cache_v)

    assert DEPTH % N_MIXERS == 1, "the layer stack must end on an attention layer"
    caches = ()
    new_sf = new_sb = None
    mixed = None
    for i in range(DEPTH):
        kind, j = i % N_MIXERS, i // N_MIXERS
        g = norm_g[i].reshape(1, D_MODEL)
        mod = mods[i]
        if kind == 0:
            outs = _attn_in(mixed, x_parts, g, mod, attn_w_in, j, cos_t, sin_t, tuple(caches))
            if mixed is not None:
                x_parts, outs = (outs[0],), outs[1:]
            q, k4, v4, z, *caches = outs
            y_parts = (_ctx_attn(attn_sink[j], q, k4, v4, z),
                       _lat_attn(attn_sink[j], q, k4, v4, ctx_k, ctx_v, j, z))
            mixed = Mixed(y_parts, attn_w_out, j, mod)
        elif kind == 1:
            x, u, z = _pool_in(mixed, x_parts, g, mod, pool_w_in, j)
            x_parts, mixed, pooled = (x,), None, (u, z, j, mod)
        else:
            u, z, jp, mod_pool = pooled
            lg_f = jax.nn.log_sigmoid(ret_decay_fwd[j].astype(F32))
            lg_b = jax.nn.log_sigmoid(ret_decay_bwd[j].astype(F32))
            gn = ret_gn_g[j].reshape(1, RET_V_WIDTH)
            x, q, kt, v, z = _pool_ret(u, z, pool_w_grp, pool_scale[jp].reshape(1, D_MODEL), pool_w_out, jp,
                                       x_parts[0], mod_pool, g, mod, ret_w_in, j)
            x_parts = (x,)
            decay, row, col, cdec = _ret_tables(lg_f, lg_b)
            y_ctx, new_sf, new_sb = _ret_ctx(q, kt, v, z, gn, decay, row)
            y_parts = (y_ctx, _ret_lat(q, kt, v, z, gn, decay, row, col, cdec,
                                       state_fwd[:, j], state_bwd[:, j]))
            mixed = Mixed(y_parts, ret_w_out, j, mod)
    y_prompt, y_sample = _out_proj_final(mixed, x_parts[0], final_g.reshape(1, D_MODEL))
    new_k, new_v = caches
    return (y_prompt.reshape(BATCH, SEQ, D_MODEL), y_sample.reshape(DEC_BATCH, DEC_SEQ, D_MODEL),
            from_kernel(new_k), from_kernel(new_v), new_sf[:, None], new_sb[:, None])
```

```python
import functools
import itertools
from typing import NamedTuple

import jax
import jax.numpy as jnp
from jax import lax
from jax.experimental import pallas as pl
from jax.experimental.pallas import tpu as pltpu

F32 = jnp.float32
BF16 = jnp.bfloat16

D_MODEL = 1024
BATCH = 16
SEQ = 256
DEPTH = 4
DEC_BATCH = 2
DEC_SEQ = 2048
PAST_LEN = 512
GRID_W = 64
N_MIXERS = 3
ATTN_HEADS = 16
ATTN_KV_HEADS = 4
ATTN_HEAD_DIM = 64
ATTN_GROUP = 4
ATTN_WIDTH = 1024
ATTN_KV_WIDTH = 256
WINDOW = 128
ROPE_BASE = 10000.0
POOL_WINDOWS = (2, 4, 8, 16)
POOL_GROUP_DIM = 256
RET_HEADS = 4
RET_KEY_DIM = 256
RET_VAL_DIM = 512
RET_QK_WIDTH = 1024
RET_V_WIDTH = 2048
EPS = 1e-6
NEG_INF = -1e30
LOG2_E = 1.4426950408889634

N_PROMPT_TOK = BATCH * SEQ
N_SAMPLE_TOK = DEC_BATCH * DEC_SEQ
N_TOK = N_PROMPT_TOK + N_SAMPLE_TOK
N_COND = 8
LANES = 128
MXU_N = 256
Q_BLOCK = 128
Q_SUB = 4
RET_CHUNK = 256
RET_TAB_ROWS = 8
RET_CTX_SEQS = 2
CTX_ATTN_SEQS = 2
POOL_HALO = 8
POOL_SUB = SEQ
PROJ_SUB = 256
WEIGHT_STAGE_COLS = 512
VMEM_LIMIT = 48 * 1024 * 1024
FUSED_VMEM_LIMIT = 58 * 1024 * 1024


class Tiling(NamedTuple):
    tm: int

    @property
    def n_tiles(self):
        return N_TOK // self.tm

    @property
    def n_prompt(self):
        return N_PROMPT_TOK // self.tm

    @property
    def per_dec_seq(self):
        return DEC_SEQ // self.tm

    def cond(self, i):
        return jnp.where(i < self.n_prompt, 0, 1 + (i - self.n_prompt) // self.per_dec_seq)

    def seq_tile(self, i):
        return jnp.where(i < self.n_prompt, 0, (i - self.n_prompt) % self.per_dec_seq)

    def rows(self, width):
        return pl.BlockSpec((self.tm, width), lambda i: (i, 0))

    def prompt_rows(self, width):
        return pl.BlockSpec((self.tm, width), lambda i: (jnp.minimum(i, self.n_prompt - 1), 0))

    def sample_rows(self, width):
        return pl.BlockSpec((self.tm, width), lambda i: (jnp.maximum(i - self.n_prompt, 0), 0))

    def mod(self, part):
        return pl.BlockSpec((None, 1, D_MODEL), lambda i: (self.cond(i), 0, part))


PROJ = Tiling(512)
FINAL = Tiling(1024)


def _silu(z):
    hz = 0.5 * z
    return hz + hz * jnp.tanh(hz)


def _dot(a, b):
    return jnp.dot(a, b, preferred_element_type=F32)


def _dot_nt(a, b):
    return lax.dot_general(a, b, (((1,), (1,)), ((), ())), preferred_element_type=F32)


def _params(*sem, vmem=None):
    return pltpu.CompilerParams(dimension_semantics=sem, vmem_limit_bytes=vmem)


def _const_spec(shape):
    nd = len(shape)
    return pl.BlockSpec(shape, lambda *_: (0,) * nd, pipeline_mode=pl.Buffered(1))


def _part_specs(t, parts, width):
    if len(parts) == 1:
        return [t.rows(width)]
    return [t.prompt_rows(width), t.sample_rows(width)]


def _read_parts(t, refs, rows=slice(None)):
    if len(refs) == 1:
        return refs[0][rows, :]
    return jnp.where(pl.program_id(0) < t.n_prompt, refs[0][rows, :], refs[1][rows, :])


def _ada_kernel(cond_ref, w_ref, b_ref, o_ref):
    s = _silu(cond_ref[...])
    w = w_ref[...]
    rows = [jnp.sum(s[:, c:c + 1] * w, axis=0, keepdims=True) + b_ref[...] for c in range(1 + DEC_BATCH)]
    rows.append(jnp.zeros((N_COND - len(rows), w.shape[1]), F32))
    o_ref[...] = jnp.concatenate(rows, axis=0)


def _ada_table(cond, ada_w, ada_b):
    tn = 3 * D_MODEL // 2
    return pl.pallas_call(
        _ada_kernel,
        out_shape=jax.ShapeDtypeStruct((DEPTH, N_COND, 3 * D_MODEL), F32),
        grid=(DEPTH, 3 * D_MODEL // tn),
        in_specs=[
            pl.BlockSpec((D_MODEL, N_COND), lambda l, n: (0, 0)),
            pl.BlockSpec((None, D_MODEL, tn), lambda l, n: (l, 0, n)),
            pl.BlockSpec((None, 1, tn), lambda l, n: (l, 0, n)),
        ],
        out_specs=pl.BlockSpec((None, N_COND, tn), lambda l, n: (l, 0, n)),
        compiler_params=_params("arbitrary", "arbitrary", vmem=VMEM_LIMIT),
        name="ada_table",
    )(cond, ada_w, ada_b.reshape(DEPTH, 1, 3 * D_MODEL))


def _rms(x, g):
    return x * lax.rsqrt(jnp.mean(x * x, axis=-1, keepdims=True) + EPS) * g


def _norm_mod(x, g_ref, sh_ref, sc_ref):
    return (_rms(x, g_ref[...]) * (1.0 + sc_ref[...]) + sh_ref[...]).astype(BF16)


def _layer_spec(w, j):
    nd = w.ndim - 1
    return pl.BlockSpec((None,) + w.shape[1:], lambda *_: (j,) + (0,) * nd, pipeline_mode=pl.Buffered(1))


def _cast_weight_once(w_ref, wb_ref):
    @pl.when(pl.program_id(0) == 0)
    def _():
        rows = wb_ref.shape[-2]
        for r in range(0, rows, MXU_N):
            wb_ref[..., r:r + MXU_N, :] = w_ref[..., r:r + MXU_N, :].astype(BF16)


class Mixed(NamedTuple):
    y_parts: tuple
    w_out: jax.Array
    j: int
    mod: jax.Array


def _residual_specs(t, mixed, x_parts):
    k = mixed.w_out.shape[1]
    return (_part_specs(t, mixed.y_parts, k) + [_layer_spec(mixed.w_out, mixed.j)]
            + _part_specs(t, x_parts, D_MODEL) + [t.mod(2)])


def _residual_args(mixed, x_parts):
    return (*mixed.y_parts, mixed.w_out, *x_parts, mixed.mod)


def _run_halves(t, x_refs, res, g_ref, sh_ref, sc_ref, xo_ref, wo_ref, h_ref, proj_steps):
    n_sub = t.tm // PROJ_SUB
    rows_of = lambda sub: slice(sub * PROJ_SUB, (sub + 1) * PROJ_SUB)

    if res is not None:
        y_refs, gate_ref = res
        for sub in range(n_sub):
            rows = rows_of(sub)
            xo_ref[rows, :] = (_read_parts(t, x_refs, rows)
                               + gate_ref[...] * _dot(_read_parts(t, y_refs, rows), wo_ref[...]))

    def normalise(sub):
        rows = rows_of(sub)
        x = xo_ref[rows, :] if res is not None else _read_parts(t, x_refs, rows)
        h_ref[rows, :] = _norm_mod(x, g_ref, sh_ref, sc_ref)

    normalise(0)
    for sub in range(n_sub):
        side = [functools.partial(normalise, sub + 1)] if sub + 1 < n_sub else []
        _run_interleaved(proj_steps(sub), side)


def _rep4(a, h):
    half = a[:, (h // 2) * LANES:(h // 2 + 1) * LANES]
    lane = lax.broadcasted_iota(jnp.int32, half.shape, 1)
    keep = (lane < ATTN_HEAD_DIM) if h % 2 == 0 else (lane >= ATTN_HEAD_DIM)
    m = jnp.where(keep, half, 0.0)
    s = m + pltpu.roll(m, ATTN_HEAD_DIM, 1)
    return jnp.concatenate([s, s], axis=1)


def _attn_in_kernel(*refs, j, n_y, n_x, n_alias):
    t = PROJ
    n_res = n_y + n_x + 2 if n_y else n_x
    res_refs, refs = refs[:n_res], refs[n_res:]
    (g_ref, sh_ref, sc_ref, wf_ref, cos_ref, sin_ref), refs = refs[:6], refs[6 + n_alias:]
    if n_y:
        xo_ref, q_ref, k4_ref, v4_ref, z_ref, kc_ref, vc_ref, wo_ref, w_ref, h_ref, kv_ref = refs
        y_refs, wof_ref, x_refs, gate_ref = (res_refs[:n_y], res_refs[n_y], res_refs[n_y + 1:n_y + 1 + n_x],
                                             res_refs[n_y + 1 + n_x])
        _cast_weight_once(wof_ref, wo_ref)
        res = (y_refs, gate_ref)
    else:
        q_ref, k4_ref, v4_ref, z_ref, kc_ref, vc_ref, w_ref, h_ref, kv_ref = refs
        x_refs, res, xo_ref, wo_ref = res_refs, None, None, None
    i = pl.program_id(0)
    _cast_weight_once(wf_ref, w_ref)
    lane = lax.broadcasted_iota(jnp.int32, (PROJ_SUB, LANES), 1)
    first = (lane % (ATTN_HEAD_DIM // 2)) < ATTN_HEAD_DIM // 4
    scale = ATTN_HEAD_DIM ** -0.5 * LOG2_E
    k_lo = ATTN_WIDTH
    v_lo = ATTN_WIDTH + ATTN_KV_WIDTH
    z_lo = ATTN_WIDTH + 2 * ATTN_KV_WIDTH

    def proj_steps(sub):
        rows = slice(sub * PROJ_SUB, (sub + 1) * PROJ_SUB)

        def rope(a):
            rot = jnp.where(first, pltpu.roll(a, LANES - ATTN_HEAD_DIM // 4, 1),
                            pltpu.roll(a, ATTN_HEAD_DIM // 4, 1))
            return a * cos_ref[rows, :] + rot * sin_ref[rows, :]

        def rope_wide(a):
            return jnp.concatenate(
                [rope(a[:, s * LANES:(s + 1) * LANES]) for s in range(MXU_N // LANES)], axis=1)

        def proj(lo):
            return _dot(h_ref[rows, :], w_ref[:, lo:lo + MXU_N])

        def q_chunk(c):
            cols = slice(c * MXU_N, (c + 1) * MXU_N)
            q_ref[rows, cols] = (rope_wide(proj(c * MXU_N)) * scale).astype(BF16)

        def keys():
            k = rope_wide(proj(k_lo))
            kv_ref[rows, :ATTN_KV_WIDTH] = k
            for hh in range(ATTN_KV_HEADS):
                k4_ref[rows, hh * MXU_N:(hh + 1) * MXU_N] = _rep4(k, hh).astype(BF16)

        def values():
            v = proj(v_lo)
            kv_ref[rows, ATTN_KV_WIDTH:] = v
            for hh in range(ATTN_KV_HEADS):
                v4_ref[rows, hh * MXU_N:(hh + 1) * MXU_N] = _rep4(v, hh).astype(BF16)

        def z_chunk(c):
            cols = slice(c * MXU_N, (c + 1) * MXU_N)
            z_ref[rows, cols] = _silu(proj(z_lo + c * MXU_N)).astype(BF16)

        n_chunks = ATTN_WIDTH // MXU_N
        return ([functools.partial(q_chunk, c) for c in range(n_chunks)] + [keys, values]
                + [functools.partial(z_chunk, c) for c in range(n_chunks)])

    _run_halves(t, x_refs, res, g_ref, sh_ref, sc_ref, xo_ref, wo_ref, h_ref, proj_steps)

    @pl.when(i < t.n_prompt)
    def _():
        for s in range(t.tm // SEQ):
            kt = kv_ref[s * SEQ:(s + 1) * SEQ, :ATTN_KV_WIDTH].T
            vt = kv_ref[s * SEQ:(s + 1) * SEQ, ATTN_KV_WIDTH:].T
            if n_alias:
                kc_ref[s] = kt
                vc_ref[s] = vt
            else:
                for l in range(kc_ref.shape[1]):
                    kc_ref[s, l] = kt if l == j else jnp.zeros_like(kt)
                    vc_ref[s, l] = vt if l == j else jnp.zeros_like(vt)


def _attn_in(mixed, x_parts, g, mod, w_all, j, cos_t, sin_t, caches):
    t = PROJ
    n_in = 2 * ATTN_WIDTH + 2 * ATTN_KV_WIDTH
    per = t.tm // SEQ
    rope_spec = pl.BlockSpec(
        (t.tm, LANES), lambda i: (jnp.where(i < t.n_prompt, 0, 1 + t.seq_tile(i)), 0))
    wide = jax.ShapeDtypeStruct((N_TOK, ATTN_WIDTH), BF16)
    n_attn = w_all.shape[0]
    cache = jax.ShapeDtypeStruct((BATCH, n_attn, ATTN_KV_WIDTH, SEQ), F32)
    if caches:
        cache_spec = pl.BlockSpec((per, None, ATTN_KV_WIDTH, SEQ),
                                  lambda i: (jnp.minimum(i, t.n_prompt - 1), j, 0, 0))
    else:
        cache_spec = pl.BlockSpec((per, n_attn, ATTN_KV_WIDTH, SEQ),
                                  lambda i: (jnp.minimum(i, t.n_prompt - 1), 0, 0, 0))
    n_x = len(x_parts)
    if mixed is None:
        n_y, res_specs, res_args, res_out, res_ospecs, res_scratch = 0, _part_specs(t, x_parts, D_MODEL), x_parts, (), (), []
    else:
        n_y = len(mixed.y_parts)
        res_specs, res_args = _residual_specs(t, mixed, x_parts), _residual_args(mixed, x_parts)
        res_out = (jax.ShapeDtypeStruct((N_TOK, D_MODEL), F32),)
        res_ospecs = (t.rows(D_MODEL),)
        res_scratch = [pltpu.VMEM(mixed.w_out.shape[1:], BF16)]
    n_front = len(res_specs) + 6
    return pl.pallas_call(
        functools.partial(_attn_in_kernel, j=j, n_y=n_y, n_x=n_x, n_alias=len(caches)),
        out_shape=res_out + (wide, wide, wide, wide, cache, cache),
        grid=(t.n_tiles,),
        in_specs=res_specs + [
            _const_spec((1, D_MODEL)), t.mod(0), t.mod(1), _layer_spec(w_all, j),
            rope_spec, rope_spec] + [pl.BlockSpec(memory_space=pl.ANY)] * len(caches),
        out_specs=res_ospecs + (t.rows(ATTN_WIDTH),) * 4 + (cache_spec,) * 2,
        scratch_shapes=res_scratch + [pltpu.VMEM((D_MODEL, n_in), BF16), pltpu.VMEM((t.tm, D_MODEL), BF16),
                                      pltpu.VMEM((t.tm, 2 * ATTN_KV_WIDTH), F32)],
        input_output_aliases={n_front + c: len(res_out) + 4 + c for c in range(len(caches))},
        compiler_params=_params("arbitrary", vmem=FUSED_VMEM_LIMIT),
        name="attn_in",
    )(*res_args, g, mod, mod, w_all, cos_t, sin_t, *caches)


def _stack_group_queries(q):
    qf = q.astype(F32)
    chunk = lax.broadcasted_iota(jnp.int32, qf.shape, 1) // ATTN_HEAD_DIM
    return jnp.concatenate(
        [jnp.where(chunk == g, qf, 0.0) for g in range(ATTN_GROUP)], axis=0).astype(BF16)


def _gather_group_outputs(o, rows):
    chunk = lax.broadcasted_iota(jnp.int32, (rows, MXU_N), 1) // ATTN_HEAD_DIM
    acc = jnp.zeros((rows, MXU_N), F32)
    for g in range(ATTN_GROUP):
        acc = acc + jnp.where(chunk == g, o[g * rows:(g + 1) * rows], 0.0)
    return acc


def _sink_column(sink_ref, h, rows):
    grp = lax.broadcasted_iota(jnp.int32, (ATTN_GROUP * rows, 1), 0) // rows
    col = jnp.zeros((ATTN_GROUP * rows, 1), F32)
    for g in range(ATTN_GROUP):
        col = jnp.where(grp == g, sink_ref[h * ATTN_GROUP + g] * LOG2_E, col)
    return col


def _chunk_rows(dtype):
    chunk = lax.broadcasted_iota(jnp.int32, (1, MXU_N), 1) // ATTN_HEAD_DIM
    return [(chunk == g).astype(F32).astype(dtype) for g in range(ATTN_GROUP)]


def _block_diag_rows(x4):
    return jnp.concatenate([x4 * m for m in _chunk_rows(x4.dtype)], axis=0)


def _ctx_attn_kernel(sink_ref, q_ref, k4_ref, v4_ref, z_ref, y_ref):
    chunk = lax.broadcasted_iota(jnp.int32, (SEQ, MXU_N), 1) // ATTN_HEAD_DIM
    for b, h in itertools.product(range(CTX_ATTN_SEQS), range(ATTN_KV_HEADS)):
        rows = slice(b * SEQ, (b + 1) * SEQ)
        cols = slice(h * MXU_N, (h + 1) * MXU_N)
        s = _dot_nt(q_ref[rows, cols], _block_diag_rows(k4_ref[rows, cols]))
        inv = jnp.zeros((SEQ, MXU_N), F32)
        probs = []
        for g in range(ATTN_GROUP):
            sg = s[:, g * SEQ:(g + 1) * SEQ]
            sk = sink_ref[h * ATTN_GROUP + g] * LOG2_E
            m = jnp.maximum(jnp.max(sg, axis=1, keepdims=True), sk)
            e = jnp.exp2(sg - m)
            den = jnp.sum(e, axis=1, keepdims=True) + jnp.exp2(sk - m)
            probs.append(e.astype(BF16))
            inv = jnp.where(chunk == g, 1.0 / den, inv)
        o = _dot(jnp.concatenate(probs, axis=1), _block_diag_rows(v4_ref[rows, cols]))
        y_ref[rows, cols] = (o * inv * z_ref[rows, cols].astype(F32)).astype(BF16)


def _ctx_attn(sink, q, k4, v4, z):
    spec = pl.BlockSpec((CTX_ATTN_SEQS * SEQ, ATTN_WIDTH), lambda b: (b, 0))
    return pl.pallas_call(
        _ctx_attn_kernel,
        out_shape=jax.ShapeDtypeStruct((N_PROMPT_TOK, ATTN_WIDTH), BF16),
        grid=(BATCH // CTX_ATTN_SEQS,),
        in_specs=[pl.BlockSpec(memory_space=pltpu.SMEM), spec, spec, spec, spec],
        out_specs=spec,
        compiler_params=_params("arbitrary"),
        name="ctx_attn",
    )(sink, q, k4, v4, z)


LAT_STEP = Q_SUB * Q_BLOCK
LAT_PER_SEQ = DEC_SEQ // LAT_STEP
LAT_BLOCKS = DEC_BATCH * ATTN_KV_HEADS * LAT_PER_SEQ
assert LAT_BLOCKS % 2 == 0


def _lat_block(blk):
    return (blk // (ATTN_KV_HEADS * LAT_PER_SEQ), (blk // LAT_PER_SEQ) % ATTN_KV_HEADS, blk % LAT_PER_SEQ)


def _band_rows(qb):
    return pl.ds(pl.multiple_of(qb * Q_BLOCK, Q_BLOCK), 3 * Q_BLOCK)


def _pad_sequence(dst, src):
    zeros = jnp.zeros((Q_BLOCK, MXU_N), BF16)
    dst[0:Q_BLOCK, :] = zeros
    dst[Q_BLOCK:Q_BLOCK + DEC_SEQ, :] = src[...]
    dst[Q_BLOCK + DEC_SEQ:, :] = zeros


def _lat_attn_kernel(sink_ref, q_ref, k4_ref, v4_ref, kc_ref, vc_ref, z_ref, y_ref,
                     kp, vp, k4c, v4c, sc_a, sl_a, sc_b, sl_b, e_scr):
    t = pl.program_id(0)
    _, _, n1 = _lat_block(jnp.minimum(t, LAT_BLOCKS - 1))
    _, h0, n0 = _lat_block(jnp.maximum(t - 1, 0))

    @pl.when(n1 == 0)
    def _():
        _pad_sequence(kp, k4_ref)
        k4c[...] = jnp.concatenate([kc_ref[...]] * ATTN_GROUP, axis=0).astype(BF16)

    @pl.when(n0 == 0)
    def _():
        _pad_sequence(vp, v4_ref)
        v4c[...] = jnp.concatenate([vc_ref[...]] * ATTN_GROUP, axis=0).T.astype(BF16)

    stages = functools.partial(_lat_stages, sink_ref, q_ref, z_ref, y_ref, kp, vp, k4c, v4c, e_scr, n1, h0, n0)
    even = t % 2 == 0
    pl.when(t == 0)(functools.partial(stages, sc_a, sl_a, sc_b, sl_b, finish_previous=False))
    pl.when(jnp.logical_and(even, jnp.logical_and(t > 0, t < LAT_BLOCKS)))(
        functools.partial(stages, sc_a, sl_a, sc_b, sl_b))
    pl.when(t == LAT_BLOCKS)(functools.partial(stages, sc_a, sl_a, sc_b, sl_b, score_next=False))
    pl.when(jnp.logical_not(even))(functools.partial(stages, sc_b, sl_b, sc_a, sl_a))


def _lat_stages(sink_ref, q_ref, z_ref, y_ref, kp, vp, k4c, v4c, e_scr, n1, h0, n0,
                sc_w, sl_w, sc_r, sl_r, score_next=True, finish_previous=True):
    n_blocks = DEC_SEQ // Q_BLOCK
    rows4 = ATTN_GROUP * Q_BLOCK
    r = lax.broadcasted_iota(jnp.int32, (rows4, Q_BLOCK), 0) % Q_BLOCK
    c = lax.broadcasted_iota(jnp.int32, (rows4, Q_BLOCK), 1)
    in_left = c >= r
    in_right = c <= r
    sk = _sink_column(sink_ref, h0, Q_BLOCK)

    def scores(sub):
        qs = _stack_group_queries(q_ref[sub * Q_BLOCK:(sub + 1) * Q_BLOCK, :])
        sc_w[sub] = _dot(qs, k4c[...])
        sl_w[sub] = _dot_nt(qs, kp[_band_rows(n1 * Q_SUB + sub), :])

    def softmax(sub):
        qb = n0 * Q_SUB + sub
        s_ctx = sc_r[sub]
        s_lat = sl_r[sub]
        blocks = [
            s_ctx,
            jnp.where(jnp.logical_and(in_left, qb > 0), s_lat[:, :Q_BLOCK], NEG_INF),
            s_lat[:, Q_BLOCK:2 * Q_BLOCK],
            jnp.where(jnp.logical_and(in_right, qb < n_blocks - 1), s_lat[:, 2 * Q_BLOCK:], NEG_INF),
        ]
        slabs = [s[:, c:c + LANES] for s in blocks for c in range(0, s.shape[1], LANES)]
        m = jnp.maximum(sk, jnp.max(functools.reduce(jnp.maximum, slabs), axis=1, keepdims=True))
        es = [jnp.exp2(s - m) for s in slabs]
        for c, e in enumerate(es):
            e_scr[sub, :, c * LANES:(c + 1) * LANES] = e.astype(BF16)
        return jnp.exp2(sk - m) + jnp.sum(functools.reduce(jnp.add, es), axis=1, keepdims=True)

    def finish(sub, den):
        qrows = slice(sub * Q_BLOCK, (sub + 1) * Q_BLOCK)
        o = (_dot(e_scr[sub, :, :PAST_LEN], v4c[...])
             + _dot(e_scr[sub, :, PAST_LEN:], vp[_band_rows(n0 * Q_SUB + sub), :])) * (1.0 / den)
        acc = _gather_group_outputs(o, Q_BLOCK)
        y_ref[qrows, :] = (acc * z_ref[qrows, :].astype(F32)).astype(BF16)

    for sub in range(Q_SUB):
        den = softmax(sub) if finish_previous else None
        if score_next:
            scores(sub)
        if finish_previous:
            finish(sub, den)


def _lat_attn(sink, q, k4, v4, kc, vc, j, z):
    padded = DEC_SEQ + 2 * Q_BLOCK
    rows4 = ATTN_GROUP * Q_BLOCK
    row0 = N_PROMPT_TOK // LAT_STEP
    seq0 = N_PROMPT_TOK // DEC_SEQ
    stage1 = lambda t: _lat_block(jnp.minimum(t, LAT_BLOCKS - 1))
    stage2 = lambda t: _lat_block(jnp.maximum(t - 1, 0))

    def spec(shape, stage, index):
        return pl.BlockSpec(shape, lambda t: index(*stage(t)))

    tile = (LAT_STEP, MXU_N)
    seq = (DEC_SEQ, MXU_N)
    ctx = (None, None, None, ATTN_HEAD_DIM, PAST_LEN)
    return pl.pallas_call(
        _lat_attn_kernel,
        out_shape=jax.ShapeDtypeStruct((N_SAMPLE_TOK, ATTN_WIDTH), BF16),
        grid=(LAT_BLOCKS + 1,),
        in_specs=[pl.BlockSpec(memory_space=pltpu.SMEM),
                  spec(tile, stage1, lambda b, h, n: (row0 + b * LAT_PER_SEQ + n, h)),
                  spec(seq, stage1, lambda b, h, n: (seq0 + b, h)),
                  spec(seq, stage2, lambda b, h, n: (seq0 + b, h)),
                  spec(ctx, stage1, lambda b, h, n: (b, j, h, 0, 0)),
                  spec(ctx, stage2, lambda b, h, n: (b, j, h, 0, 0)),
                  spec(tile, stage2, lambda b, h, n: (row0 + b * LAT_PER_SEQ + n, h))],
        out_specs=spec(tile, stage2, lambda b, h, n: (b * LAT_PER_SEQ + n, h)),
        scratch_shapes=[pltpu.VMEM((padded, MXU_N), BF16), pltpu.VMEM((padded, MXU_N), BF16),
                        pltpu.VMEM((MXU_N, PAST_LEN), BF16), pltpu.VMEM((PAST_LEN, MXU_N), BF16),
                        pltpu.VMEM((Q_SUB, rows4, PAST_LEN), F32),
                        pltpu.VMEM((Q_SUB, rows4, 3 * Q_BLOCK), F32),
                        pltpu.VMEM((Q_SUB, rows4, PAST_LEN), F32),
                        pltpu.VMEM((Q_SUB, rows4, 3 * Q_BLOCK), F32),
                        pltpu.VMEM((Q_SUB, rows4, PAST_LEN + 3 * Q_BLOCK), BF16)],
        compiler_params=_params("arbitrary", vmem=VMEM_LIMIT),
        name="lat_attn",
    )(sink, q, k4, v4, kc, vc, z)


def _out_final_kernel(yp_ref, ys_ref, wf_ref, x_ref, gate_ref, fg_ref, op_ref, os_ref, w_ref):
    t = FINAL
    i = pl.program_id(0)
    _cast_weight_once(wf_ref, w_ref)
    y = _read_parts(t, (yp_ref, ys_ref))
    r = _rms(x_ref[...] + gate_ref[...] * _dot(y, w_ref[...]), fg_ref[...])

    @pl.when(i < t.n_prompt)
    def _():
        op_ref[...] = r

    @pl.when(i >= t.n_prompt)
    def _():
        os_ref[...] = r


def _out_proj_final(mixed, x, final_g):
    t = FINAL
    return pl.pallas_call(
        _out_final_kernel,
        out_shape=(jax.ShapeDtypeStruct((N_PROMPT_TOK, D_MODEL), F32),
                   jax.ShapeDtypeStruct((N_SAMPLE_TOK, D_MODEL), F32)),
        grid=(t.n_tiles,),
        in_specs=_residual_specs(t, mixed, (x,)) + [_const_spec((1, D_MODEL))],
        out_specs=(t.prompt_rows(D_MODEL), t.sample_rows(D_MODEL)),
        scratch_shapes=[pltpu.VMEM(mixed.w_out.shape[1:], BF16)],
        compiler_params=_params("arbitrary", vmem=VMEM_LIMIT),
        name="out_proj_final",
    )(*_residual_args(mixed, (x,)), final_g)


def _pool_in_kernel(*refs, n_y, n_x):
    t = PROJ
    n_res = n_y + n_x + 2
    res_refs, refs = refs[:n_res], refs[n_res:]
    g_ref, sh_ref, sc_ref, wf_ref, xo_ref, u_ref, z_ref, wo_ref, w_ref, h_ref = refs
    y_refs, wof_ref, x_refs, gate_ref = (res_refs[:n_y], res_refs[n_y], res_refs[n_y + 1:n_y + 1 + n_x],
                                         res_refs[n_y + 1 + n_x])
    _cast_weight_once(wof_ref, wo_ref)
    _cast_weight_once(wf_ref, w_ref)

    def proj_steps(sub):
        rows = slice(sub * PROJ_SUB, (sub + 1) * PROJ_SUB)

        def chunk(c):
            def step():
                cols = slice(c * MXU_N, (c + 1) * MXU_N)
                u_ref[rows, cols] = _dot(h_ref[rows, :], w_ref[:, cols])
            return step

        def gate_chunk(c):
            def step():
                a = _dot(h_ref[rows, :], w_ref[:, D_MODEL + c * MXU_N:D_MODEL + (c + 1) * MXU_N])
                z_ref[rows, c * MXU_N:(c + 1) * MXU_N] = _silu(a).astype(BF16)
            return step

        n = D_MODEL // MXU_N
        return [chunk(c) for c in range(n)] + [gate_chunk(c) for c in range(n)]

    _run_halves(t, x_refs, (y_refs, gate_ref), g_ref, sh_ref, sc_ref, xo_ref, wo_ref, h_ref, proj_steps)


def _pool_in(mixed, x_parts, g, mod, w_all, j):
    t = PROJ
    slab = jax.ShapeDtypeStruct((N_TOK, D_MODEL), F32)
    return pl.pallas_call(
        functools.partial(_pool_in_kernel, n_y=len(mixed.y_parts), n_x=len(x_parts)),
        out_shape=(slab, slab, jax.ShapeDtypeStruct((N_TOK, D_MODEL), BF16)),
        grid=(t.n_tiles,),
        in_specs=_residual_specs(t, mixed, x_parts) + [
            _const_spec((1, D_MODEL)), t.mod(0), t.mod(1), _layer_spec(w_all, j)],
        out_specs=(t.rows(D_MODEL),) * 3,
        scratch_shapes=[pltpu.VMEM(mixed.w_out.shape[1:], BF16), pltpu.VMEM((D_MODEL, 2 * D_MODEL), BF16),
                        pltpu.VMEM((t.tm, D_MODEL), BF16)],
        compiler_params=_params("arbitrary", vmem=FUSED_VMEM_LIMIT),
        name="pool_in",
    )(*_residual_args(mixed, x_parts), g, mod, mod, w_all)


def _split_bf16(a):
    hi = a.astype(BF16)
    return hi, (a - hi.astype(F32)).astype(BF16)


def _band_ones(shape, lo, hi):
    d = lax.broadcasted_iota(jnp.int32, shape, 1) - lax.broadcasted_iota(jnp.int32, shape, 0)
    return jnp.logical_and(d >= lo, d <= hi).astype(F32).astype(BF16)


def _stream_cast_weight(w_hbm, wb_ref, stage_ref, sem):
    width = stage_ref.shape[-1]
    n_chunks = wb_ref.shape[-1] // width

    def copy(c):
        return pltpu.make_async_copy(w_hbm.at[:, pl.ds(c * width, width)], stage_ref.at[c % 2], sem.at[c % 2])

    copy(0).start()
    for c in range(n_chunks):
        if c + 1 < n_chunks:
            copy(c + 1).start()
        copy(c).wait()
        wb_ref[:, c * width:(c + 1) * width] = stage_ref[c % 2].astype(BF16)


def _pool_ret_kernel(u_ref, up_ref, un_ref, z_ref, wgf_ref, ps_ref, wof_ref, x_ref, gate_ref,
                     g_ref, sh_ref, sc_ref, w_hbm,
                     xo_ref, q_ref, kt_ref, v_ref, zr_ref,
                     y_ref, wg_ref, wo_ref, band_ref, wkt_ref, w_ref, stage_ref, sem, h_ref, *, j_ret):
    t = PROJ
    i = pl.program_id(0)
    _cast_weight_once(wgf_ref, wg_ref)
    _cast_weight_once(wof_ref, wo_ref)

    @pl.when(i == 0)
    def _():
        for g, w in enumerate(POOL_WINDOWS):
            band_ref[g] = _band_ones((POOL_SUB, POOL_SUB), -(w // 2), w - 1 - w // 2)
        _stream_cast_weight(w_hbm.at[j_ret], w_ref, stage_ref, sem)
        for r in range(0, RET_QK_WIDTH, MXU_N):
            wk = w_ref[:, RET_QK_WIDTH + r:RET_QK_WIDTH + r + MXU_N]
            wkt_ref[r:r + MXU_N, :] = wk.astype(F32).T.astype(BF16)

    is_dec = i >= t.n_prompt
    st = t.seq_tile(i)
    seq_len = jnp.where(is_dec, DEC_SEQ, SEQ)
    h = POOL_HALO
    n_sub = t.tm // POOL_SUB
    def pool_steps(sub):
        rows = slice(sub * POOL_SUB, (sub + 1) * POOL_SUB)
        if sub == 0:
            before = jnp.where(jnp.logical_and(is_dec, st != 0), up_ref[...], 0.0)
        else:
            before = jnp.where(is_dec, u_ref[sub * POOL_SUB - h:sub * POOL_SUB, :], 0.0)
        if sub == n_sub - 1:
            after = jnp.where(jnp.logical_and(is_dec, st != t.per_dec_seq - 1), un_ref[...], 0.0)
        else:
            after = jnp.where(is_dec, u_ref[(sub + 1) * POOL_SUB:(sub + 1) * POOL_SUB + h, :], 0.0)
        pos0 = jnp.where(is_dec, st * t.tm + sub * POOL_SUB, 0)
        return _pool_core_steps(u_ref.at[rows], before, after, pos0, seq_len, z_ref.at[rows], ps_ref,
                                wg_ref, band_ref, y_ref.at[rows])

    def mix_steps(sub):
        rows = slice(sub * POOL_SUB, (sub + 1) * POOL_SUB)

        def residual():
            xo_ref[rows, :] = x_ref[rows, :] + gate_ref[...] * _dot(y_ref[rows, :], wo_ref[...])

        def normalise():
            h_ref[rows, :] = _norm_mod(xo_ref[rows, :], g_ref, sh_ref, sc_ref)

        return pool_steps(sub) + [residual, normalise]

    for step in mix_steps(0):
        step()
    for sub in range(n_sub):
        rows = slice(sub * POOL_SUB, (sub + 1) * POOL_SUB)
        proj = _ret_in_steps(h_ref.at[rows], w_ref, wkt_ref, q_ref.at[rows],
                             kt_ref.at[pl.ds(sub * POOL_SUB // RET_CHUNK, POOL_SUB // RET_CHUNK)],
                             v_ref.at[rows], zr_ref.at[rows])
        _run_interleaved(proj, mix_steps(sub + 1) if sub + 1 < n_sub else [])


def _pool_core_steps(u_ref, before, after, pos0, seq_len, z_ref, ps_ref, wg_ref, band_ref, y_ref):
    n = u_ref.shape[0]
    h = POOL_HALO
    halo_hi, halo_lo = _split_bf16(jnp.concatenate([before, after], axis=0))
    pos = pos0 + lax.broadcasted_iota(jnp.int32, (n, 1), 0)
    rr = lax.broadcasted_iota(jnp.int32, (2 * h, 2 * h), 0)
    cc = lax.broadcasted_iota(jnp.int32, (2 * h, 2 * h), 1)

    def pooled(g, w):
        def step():
            left = w // 2
            right = w - 1 - left
            cols = slice(g * POOL_GROUP_DIM, (g + 1) * POOL_GROUP_DIM)
            u = u_ref[:, cols]
            s = _dot(band_ref[g], u.astype(BF16))
            top = jnp.logical_and(jnp.logical_and(rr < h, cc < h), cc - h >= rr - left)
            bot = jnp.logical_and(jnp.logical_and(rr >= h, cc >= h), cc - h <= rr - 2 * h + right)
            edge = jnp.logical_or(top, bot).astype(F32).astype(BF16)
            se = _dot(edge, halo_hi[:, cols]) + _dot(edge, halo_lo[:, cols])
            s = jnp.concatenate([s[:h] + se[:h], s[h:n - h], s[n - h:] + se[h:]], axis=0)
            cnt = (jnp.minimum(pos + right + 1, seq_len) - jnp.maximum(pos - left, 0)).astype(F32)
            y_ref[:, cols] = (s / cnt - u).astype(BF16)
        return step

    def mixed(g):
        def step():
            cols = slice(g * POOL_GROUP_DIM, (g + 1) * POOL_GROUP_DIM)
            yg = _dot(y_ref[:, cols], wg_ref[g]) * ps_ref[:, cols] * z_ref[:, cols].astype(F32)
            y_ref[:, cols] = yg.astype(BF16)
        return step

    return ([pooled(g, w) for g, w in enumerate(POOL_WINDOWS)]
            + [mixed(g) for g in range(len(POOL_WINDOWS))])


def _pool_ret(u, z, wg_all, ps, wo_all, j, x, mod_pool, g, mod, w_ret_all, j_ret):
    t = PROJ
    per = t.tm // POOL_HALO
    n_halo = N_TOK // POOL_HALO
    n_in = 2 * RET_QK_WIDTH + 2 * RET_V_WIDTH
    kt_per = t.tm // RET_CHUNK
    return pl.pallas_call(
        functools.partial(_pool_ret_kernel, j_ret=j_ret),
        out_shape=(jax.ShapeDtypeStruct((N_TOK, D_MODEL), F32),
                   jax.ShapeDtypeStruct((N_TOK, RET_QK_WIDTH), BF16),
                   jax.ShapeDtypeStruct((N_TOK // RET_CHUNK, RET_QK_WIDTH, RET_CHUNK), BF16),
                   jax.ShapeDtypeStruct((N_TOK, RET_V_WIDTH), BF16),
                   jax.ShapeDtypeStruct((N_TOK, RET_V_WIDTH), BF16)),
        grid=(t.n_tiles,),
        in_specs=[
            t.rows(D_MODEL),
            pl.BlockSpec((POOL_HALO, D_MODEL), lambda i: (jnp.maximum(i * per - 1, 0), 0)),
            pl.BlockSpec((POOL_HALO, D_MODEL), lambda i: (jnp.minimum((i + 1) * per, n_halo - 1), 0)),
            t.rows(D_MODEL),
            _layer_spec(wg_all, j),
            _const_spec((1, D_MODEL)),
            _layer_spec(wo_all, j),
            t.rows(D_MODEL),
            t.mod(2),
            _const_spec((1, D_MODEL)), t.mod(0), t.mod(1), pl.BlockSpec(memory_space=pl.ANY),
        ],
        out_specs=(t.rows(D_MODEL), t.rows(RET_QK_WIDTH),
                   pl.BlockSpec((kt_per, RET_QK_WIDTH, RET_CHUNK), lambda i: (i, 0, 0)),
                   t.rows(RET_V_WIDTH), t.rows(RET_V_WIDTH)),
        scratch_shapes=[pltpu.VMEM((t.tm, D_MODEL), BF16),
                        pltpu.VMEM((len(POOL_WINDOWS), POOL_GROUP_DIM, POOL_GROUP_DIM), BF16),
                        pltpu.VMEM((D_MODEL, D_MODEL), BF16),
                        pltpu.VMEM((len(POOL_WINDOWS), POOL_SUB, POOL_SUB), BF16),
                        pltpu.VMEM((RET_QK_WIDTH, D_MODEL), BF16),
                        pltpu.VMEM((D_MODEL, n_in), BF16),
                        pltpu.VMEM((2, D_MODEL, WEIGHT_STAGE_COLS), F32),
                        pltpu.SemaphoreType.DMA((2,)),
                        pltpu.VMEM((t.tm, D_MODEL), BF16)],
        compiler_params=_params("arbitrary", vmem=FUSED_VMEM_LIMIT),
        name="pool_ret",
    )(u, u, u, z, wg_all, ps, wo_all, x, mod_pool, g, mod, mod, w_ret_all)


def _ret_in_steps(h_ref, w_ref, wkt_ref, q_ref, kt_ref, v_ref, z_ref):
    n_rows = h_ref.shape[0]

    def chunk(ref, lo, c, post=lambda a: a):
        def step():
            cols = slice(c * MXU_N, (c + 1) * MXU_N)
            ref[:, cols] = post(_dot(h_ref[...], w_ref[:, lo + c * MXU_N:lo + (c + 1) * MXU_N])).astype(BF16)
        return step

    def key_chunk(c):
        def step():
            rows = slice(c * MXU_N, (c + 1) * MXU_N)
            kt = (_dot_nt(wkt_ref[rows, :], h_ref[...]) * RET_KEY_DIM ** -0.5).astype(BF16)
            for cc in range(n_rows // RET_CHUNK):
                kt_ref[cc, rows, :] = kt[:, cc * RET_CHUNK:(cc + 1) * RET_CHUNK]
        return step

    steps = [chunk(q_ref, 0, c) for c in range(RET_QK_WIDTH // MXU_N)]
    steps += [key_chunk(c) for c in range(RET_QK_WIDTH // MXU_N)]
    steps += [chunk(v_ref, 2 * RET_QK_WIDTH, c) for c in range(RET_V_WIDTH // MXU_N)]
    steps += [chunk(z_ref, 2 * RET_QK_WIDTH + RET_V_WIDTH, c, _silu) for c in range(RET_V_WIDTH // MXU_N)]
    return steps


def _run_interleaved(main, side):
    done = 0
    for k, step in enumerate(main):
        step()
        due = (k + 1) * len(side) // len(main)
        for s in side[done:due]:
            s()
        done = due


def _pos(shape, axis):
    return lax.broadcasted_iota(jnp.int32, shape, axis).astype(F32)


def _ret_tables_kernel(lgf_ref, lgb_ref, decay_ref, row_ref, col_ref, cdec_ref):
    h = pl.program_id(0)
    lg_f = lgf_ref[h]
    lg_b = lgb_ref[h]
    c = RET_CHUNK
    diff = _pos((c, c), 0) - _pos((c, c), 1)
    fwd = jnp.where(diff >= 0, jnp.exp(jnp.maximum(diff, 0.0) * lg_f), 0.0)
    bwd = jnp.where(diff <= 0, jnp.exp(jnp.maximum(-diff, 0.0) * lg_b), 0.0)
    decay_ref[...] = fwd + bwd
    j = _pos((RET_TAB_ROWS, c), 1)
    row_ref[0] = jnp.exp((c - 1.0 - j) * lg_f)
    row_ref[1] = jnp.exp(j * lg_b)
    i = _pos((c, LANES), 0)
    col_ref[0] = jnp.exp((i + 1.0) * lg_f)
    col_ref[1] = jnp.exp((c - i) * lg_b)
    full = jnp.full((RET_TAB_ROWS, RET_VAL_DIM), float(c), F32)
    cdec_ref[0] = jnp.exp(full * lg_f)
    cdec_ref[1] = jnp.exp(full * lg_b)


def _ret_tables(lg_f, lg_b):
    smem = pl.BlockSpec(memory_space=pltpu.SMEM)
    c = RET_CHUNK
    shapes = ((c, c), (2, RET_TAB_ROWS, c), (2, c, LANES), (2, RET_TAB_ROWS, RET_VAL_DIM))
    return pl.pallas_call(
        _ret_tables_kernel,
        out_shape=tuple(jax.ShapeDtypeStruct((RET_HEADS,) + s, F32) for s in shapes),
        grid=(RET_HEADS,),
        in_specs=[smem, smem],
        out_specs=tuple(pl.BlockSpec((None,) + s, lambda h, n=len(s): (h,) + (0,) * n) for s in shapes),
        compiler_params=_params("arbitrary"),
        name="ret_tables",
    )(lg_f, lg_b)


def _group_norm_gate(o, gn, z):
    mu = jnp.mean(o, axis=-1, keepdims=True)
    var = jnp.mean(jnp.square(o - mu), axis=-1, keepdims=True)
    on = (o - mu) * lax.rsqrt(var + EPS)
    return (on * gn * z.astype(F32)).astype(BF16)


def _ret_ctx_kernel(q_ref, kt_ref, v_ref, z_ref, gn_ref, decay_ref, row_ref, y_ref, sf_ref, sb_ref):
    for s in range(RET_CTX_SEQS):
        rows = slice(s * SEQ, (s + 1) * SEQ)
        for h in range(RET_HEADS):
            kc = slice(h * RET_KEY_DIM, (h + 1) * RET_KEY_DIM)
            vc = slice(h * RET_VAL_DIM, (h + 1) * RET_VAL_DIM)
            kt = kt_ref[s, kc, :]
            v = v_ref[rows, vc]
            att = (_dot(q_ref[rows, kc], kt) * decay_ref[h]).astype(BF16)
            y_ref[rows, vc] = _group_norm_gate(_dot(att, v), gn_ref[:, vc], z_ref[rows, vc])
            ktf = kt.astype(F32)
            sf_ref[s, h] = _dot((ktf * row_ref[h, 0, 0:1, :]).astype(BF16), v)
            sb_ref[s, h] = _dot((ktf * row_ref[h, 1, 0:1, :]).astype(BF16), v)


def _ret_ctx(q, kt, v, z, gn, decay, row):
    c = RET_CHUNK
    n = RET_CTX_SEQS
    wide = lambda w: pl.BlockSpec((n * SEQ, w), lambda b: (b, 0))
    st_spec = pl.BlockSpec((n, RET_HEADS, RET_KEY_DIM, RET_VAL_DIM), lambda b: (b, 0, 0, 0))
    st_shape = jax.ShapeDtypeStruct((BATCH, RET_HEADS, RET_KEY_DIM, RET_VAL_DIM), F32)
    return pl.pallas_call(
        _ret_ctx_kernel,
        out_shape=(jax.ShapeDtypeStruct((N_PROMPT_TOK, RET_V_WIDTH), BF16), st_shape, st_shape),
        grid=(BATCH // n,),
        in_specs=[wide(RET_QK_WIDTH), pl.BlockSpec((n, RET_QK_WIDTH, c), lambda b: (b, 0, 0)),
                  wide(RET_V_WIDTH), wide(RET_V_WIDTH), _const_spec((1, RET_V_WIDTH)),
                  _const_spec((RET_HEADS, c, c)), _const_spec((RET_HEADS, 2, RET_TAB_ROWS, c))],
        out_specs=(wide(RET_V_WIDTH), st_spec, st_spec),
        compiler_params=_params("arbitrary", vmem=VMEM_LIMIT),
        name="ret_ctx",
    )(q, kt, v, z, gn, decay, row)


def _ret_lat_kernel(q_ref, kt_ref, v_ref, z_ref, gn_ref, decay_ref, row_ref, col_ref, cdec_ref,
                    s0f_ref, s0b_ref, y_ref, sf_all, sb_all, sf_acc, sb_acc):
    c = RET_CHUNK
    n_chunks = DEC_SEQ // c
    rows_of = lambda ci: pl.ds(pl.multiple_of(ci * c, c), c)

    sf_acc[...] = s0f_ref[...]
    sb_acc[...] = s0b_ref[...]

    def scan_step(i, carry):
        cf = i
        cb = n_chunks - 1 - i
        sf_all[cf] = sf_acc[...].astype(BF16)
        sb_all[cb] = sb_acc[...].astype(BF16)
        uf = _dot((kt_ref[cf].astype(F32) * row_ref[0, 0:1, :]).astype(BF16), v_ref[rows_of(cf), :])
        ub = _dot((kt_ref[cb].astype(F32) * row_ref[1, 0:1, :]).astype(BF16), v_ref[rows_of(cb), :])
        sf_acc[...] = sf_acc[...] * cdec_ref[0, 0:1, :] + uf
        sb_acc[...] = sb_acc[...] * cdec_ref[1, 0:1, :] + ub
        return carry

    lax.fori_loop(0, n_chunks, scan_step, 0, unroll=4)

    def out_step(ci, carry):
        rows = rows_of(ci)
        q = q_ref[rows, :]
        qf = q.astype(F32)
        qdec_f = jnp.concatenate([col_ref[0]] * (RET_KEY_DIM // LANES), axis=1)
        qdec_b = jnp.concatenate([col_ref[1]] * (RET_KEY_DIM // LANES), axis=1)
        att = (_dot(q, kt_ref[ci]) * decay_ref[...]).astype(BF16)
        o = (_dot(att, v_ref[rows, :])
             + _dot((qf * qdec_f).astype(BF16), sf_all[ci])
             + _dot((qf * qdec_b).astype(BF16), sb_all[ci]))
        y_ref[rows, :] = _group_norm_gate(o, gn_ref[...], z_ref[rows, :])
        return carry

    lax.fori_loop(0, n_chunks, out_step, 0, unroll=8)


def _ret_lat(q, kt, v, z, gn, decay, row, col, cdec, s0f, s0b):
    c = RET_CHUNK
    n_chunks = DEC_SEQ // c
    row0 = N_PROMPT_TOK // DEC_SEQ
    qk_spec = pl.BlockSpec((DEC_SEQ, RET_KEY_DIM), lambda b, h: (row0 + b, h))
    v_spec = pl.BlockSpec((DEC_SEQ, RET_VAL_DIM), lambda b, h: (row0 + b, h))
    st_spec = pl.BlockSpec((None, None, RET_KEY_DIM, RET_VAL_DIM), lambda b, h: (b, h, 0, 0))
    tab = lambda *s: pl.BlockSpec((None,) + s, lambda b, h: (h,) + (0,) * len(s))
    states = pltpu.VMEM((n_chunks, RET_KEY_DIM, RET_VAL_DIM), BF16)
    acc = pltpu.VMEM((RET_KEY_DIM, RET_VAL_DIM), F32)
    return pl.pallas_call(
        _ret_lat_kernel,
        out_shape=jax.ShapeDtypeStruct((N_SAMPLE_TOK, RET_V_WIDTH), BF16),
        grid=(DEC_BATCH, RET_HEADS),
        in_specs=[qk_spec,
                  pl.BlockSpec((n_chunks, RET_KEY_DIM, c), lambda b, h: (row0 + b, h, 0)),
                  v_spec, v_spec, pl.BlockSpec((1, RET_VAL_DIM), lambda b, h: (0, h)),
                  tab(c, c), tab(2, RET_TAB_ROWS, c), tab(2, c, LANES), tab(2, RET_TAB_ROWS, RET_VAL_DIM),
                  st_spec, st_spec],
        out_specs=pl.BlockSpec((DEC_SEQ, RET_VAL_DIM), lambda b, h: (b, h)),
        scratch_shapes=[states, states, acc, acc],
        compiler_params=_params("arbitrary", "arbitrary", vmem=VMEM_LIMIT),
        name="ret_lat",
    )(q, kt, v, z, gn, decay, row, col, cdec, s0f, s0b)


def _rope_tables(tm):
    n_rows = DEC_SEQ // GRID_W
    rows = jnp.repeat(jnp.arange(n_rows), GRID_W).astype(F32)
    cols = jnp.tile(jnp.arange(GRID_W), n_rows).astype(F32)
    half = ATTN_HEAD_DIM // 4
    inv = ROPE_BASE ** (-jnp.arange(half, dtype=F32) / half)
    ang_r = rows[:, None] * inv[None, :]
    ang_c = cols[:, None] * inv[None, :]
    cos = jnp.concatenate([jnp.cos(ang_r), jnp.cos(ang_r), jnp.cos(ang_c), jnp.cos(ang_c)], axis=-1)
    sin = jnp.concatenate([-jnp.sin(ang_r), jnp.sin(ang_r), -jnp.sin(ang_c), jnp.sin(ang_c)], axis=-1)
    cos = jnp.concatenate([jnp.ones((tm, ATTN_HEAD_DIM), F32), cos], axis=0)
    sin = jnp.concatenate([jnp.zeros((tm, ATTN_HEAD_DIM), F32), sin], axis=0)
    return jnp.tile(cos, (1, 2)), jnp.tile(sin, (1, 2))


def kernel(x_prompt, x_sample, cache_k, cache_v, state_fwd, state_bwd, c, c_ctx, norm_g, ada_w, ada_b, attn_w_in, attn_w_out, attn_sink, pool_w_in, pool_w_grp, pool_scale, pool_w_out, ret_w_in, ret_decay_fwd, ret_decay_bwd, ret_gn_g, ret_w_out, final_g):
    x_parts = (x_prompt.reshape(N_PROMPT_TOK, D_MODEL), x_sample.reshape(N_SAMPLE_TOK, D_MODEL))
    cond = jnp.concatenate([c_ctx[None, :], c,
                            jnp.zeros((N_COND - 1 - DEC_BATCH, D_MODEL), F32)], axis=0)
    mods = _ada_table(cond.T, ada_w, ada_b).reshape(DEPTH, N_COND, 1, 3 * D_MODEL)
    cos_t, sin_t = _rope_tables(PROJ.tm)

    to_kernel = lambda a: jnp.transpose(a, (0, 1, 3, 4, 2))
    from_kernel = lambda a: jnp.transpose(
        a.reshape(a.shape[0], a.shape[1], ATTN_KV_HEADS, ATTN_HEAD_DIM, a.shape[3]), (0, 1, 4, 2, 3))
    ctx_k, ctx_v = to_kernel(cache_k), to_kernel(cache_v)

    assert DEPTH % N_MIXERS == 1, "the layer stack must end on an attention layer"
    caches = ()
    new_sf = new_sb = None
    mixed = None
    for i in range(DEPTH):
        kind, j = i % N_MIXERS, i // N_MIXERS
        g = norm_g[i].reshape(1, D_MODEL)
        mod = mods[i]
        if kind == 0:
            outs = _attn_in(mixed, x_parts, g, mod, attn_w_in, j, cos_t, sin_t, tuple(caches))
            if mixed is not None:
                x_parts, outs = (outs[0],), outs[1:]
            q, k4, v4, z, *caches = outs
            y_parts = (_ctx_attn(attn_sink[j], q, k4, v4, z),
                       _lat_attn(attn_sink[j], q, k4, v4, ctx_k, ctx_v, j, z))
            mixed = Mixed(y_parts, attn_w_out, j, mod)
        elif kind == 1:
            x, u, z = _pool_in(mixed, x_parts, g, mod, pool_w_in, j)
            x_parts, mixed, pooled = (x,), None, (u, z, j, mod)
        else:
            u, z, jp, mod_pool = pooled
            lg_f = jax.nn.log_sigmoid(ret_decay_fwd[j].astype(F32))
            lg_b = jax.nn.log_sigmoid(ret_decay_bwd[j].astype(F32))
            gn = ret_gn_g[j].reshape(1, RET_V_WIDTH)
            x, q, kt, v, z = _pool_ret(u, z, pool_w_grp, pool_scale[jp].reshape(1, D_MODEL), pool_w_out, jp,
                                       x_parts[0], mod_pool, g, mod, ret_w_in, j)
            x_parts = (x,)
            decay, row, col, cdec = _ret_tables(lg_f, lg_b)
            y_ctx, new_sf, new_sb = _ret_ctx(q, kt, v, z, gn, decay, row)
            y_parts = (y_ctx, _ret_lat(q, kt, v, z, gn, decay, row, col, cdec,
                                       state_fwd[:, j], state_bwd[:, j]))
            mixed = Mixed(y_parts, ret_w_out, j, mod)
    y_prompt, y_sample = _out_proj_final(mixed, x_parts[0], final_g.reshape(1, D_MODEL))
    new_k, new_v = caches
    return (y_prompt.reshape(BATCH, SEQ, D_MODEL), y_sample.reshape(DEC_BATCH, DEC_SEQ, D_MODEL),
            from_kernel(new_k), from_kernel(new_v), new_sf[:, None], new_sb[:, None])
```

```python
import functools
import itertools
from typing import NamedTuple

import jax
import jax.numpy as jnp
from jax import lax
from jax.experimental import pallas as pl
from jax.experimental.pallas import tpu as pltpu

F32 = jnp.float32
BF16 = jnp.bfloat16

D_MODEL = 1024
BATCH = 16
SEQ = 256
DEPTH = 4
DEC_BATCH = 2
DEC_SEQ = 2048
PAST_LEN = 512
GRID_W = 64
N_MIXERS = 3
ATTN_HEADS = 16
ATTN_KV_HEADS = 4
ATTN_HEAD_DIM = 64
ATTN_GROUP = 4
ATTN_WIDTH = 1024
ATTN_KV_WIDTH = 256
WINDOW = 128
ROPE_BASE = 10000.0
POOL_WINDOWS = (2, 4, 8, 16)
POOL_GROUP_DIM = 256
RET_HEADS = 4
RET_KEY_DIM = 256
RET_VAL_DIM = 512
RET_QK_WIDTH = 1024
RET_V_WIDTH = 2048
EPS = 1e-6
NEG_INF = -1e30
LOG2_E = 1.4426950408889634

N_PROMPT_TOK = BATCH * SEQ
N_SAMPLE_TOK = DEC_BATCH * DEC_SEQ
N_TOK = N_PROMPT_TOK + N_SAMPLE_TOK
N_COND = 8
LANES = 128
MXU_N = 256
Q_BLOCK = 128
Q_SUB = 4
RET_CHUNK = 256
RET_TAB_ROWS = 8
RET_CTX_SEQS = 2
CTX_ATTN_SEQS = 2
POOL_HALO = 8
POOL_SUB = SEQ
PROJ_SUB = 256
VMEM_LIMIT = 48 * 1024 * 1024
FUSED_VMEM_LIMIT = 58 * 1024 * 1024


class Tiling(NamedTuple):
    tm: int

    @property
    def n_tiles(self):
        return N_TOK // self.tm

    @property
    def n_prompt(self):
        return N_PROMPT_TOK // self.tm

    @property
    def per_dec_seq(self):
        return DEC_SEQ // self.tm

    def cond(self, i):
        return jnp.where(i < self.n_prompt, 0, 1 + (i - self.n_prompt) // self.per_dec_seq)

    def seq_tile(self, i):
        return jnp.where(i < self.n_prompt, 0, (i - self.n_prompt) % self.per_dec_seq)

    def rows(self, width):
        return pl.BlockSpec((self.tm, width), lambda i: (i, 0))

    def prompt_rows(self, width):
        return pl.BlockSpec((self.tm, width), lambda i: (jnp.minimum(i, self.n_prompt - 1), 0))

    def sample_rows(self, width):
        return pl.BlockSpec((self.tm, width), lambda i: (jnp.maximum(i - self.n_prompt, 0), 0))

    def mod(self, part):
        return pl.BlockSpec((None, 1, D_MODEL), lambda i: (self.cond(i), 0, part))


PROJ = Tiling(512)
FINAL = Tiling(1024)


def _silu(z):
    hz = 0.5 * z
    return hz + hz * jnp.tanh(hz)


def _dot(a, b):
    return jnp.dot(a, b, preferred_element_type=F32)


def _dot_nt(a, b):
    return lax.dot_general(a, b, (((1,), (1,)), ((), ())), preferred_element_type=F32)


def _params(*sem, vmem=None):
    return pltpu.CompilerParams(dimension_semantics=sem, vmem_limit_bytes=vmem)


def _const_spec(shape):
    nd = len(shape)
    return pl.BlockSpec(shape, lambda *_: (0,) * nd, pipeline_mode=pl.Buffered(1))


def _part_specs(t, parts, width):
    if len(parts) == 1:
        return [t.rows(width)]
    return [t.prompt_rows(width), t.sample_rows(width)]


def _read_parts(t, refs, rows=slice(None)):
    if len(refs) == 1:
        return refs[0][rows, :]
    return jnp.where(pl.program_id(0) < t.n_prompt, refs[0][rows, :], refs[1][rows, :])


def _ada_kernel(cond_ref, w_ref, b_ref, o_ref):
    s = _silu(cond_ref[...])
    w = w_ref[...]
    rows = [jnp.sum(s[:, c:c + 1] * w, axis=0, keepdims=True) + b_ref[...] for c in range(1 + DEC_BATCH)]
    rows.append(jnp.zeros((N_COND - len(rows), w.shape[1]), F32))
    o_ref[...] = jnp.concatenate(rows, axis=0)


def _ada_table(cond, ada_w, ada_b):
    tn = 3 * D_MODEL // 2
    return pl.pallas_call(
        _ada_kernel,
        out_shape=jax.ShapeDtypeStruct((DEPTH, N_COND, 3 * D_MODEL), F32),
        grid=(DEPTH, 3 * D_MODEL // tn),
        in_specs=[
            pl.BlockSpec((D_MODEL, N_COND), lambda l, n: (0, 0)),
            pl.BlockSpec((None, D_MODEL, tn), lambda l, n: (l, 0, n)),
            pl.BlockSpec((None, 1, tn), lambda l, n: (l, 0, n)),
        ],
        out_specs=pl.BlockSpec((None, N_COND, tn), lambda l, n: (l, 0, n)),
        compiler_params=_params("arbitrary", "arbitrary", vmem=VMEM_LIMIT),
        name="ada_table",
    )(cond, ada_w, ada_b.reshape(DEPTH, 1, 3 * D_MODEL))


def _rms(x, g):
    return x * lax.rsqrt(jnp.mean(x * x, axis=-1, keepdims=True) + EPS) * g


def _norm_mod(x, g_ref, sh_ref, sc_ref):
    return (_rms(x, g_ref[...]) * (1.0 + sc_ref[...]) + sh_ref[...]).astype(BF16)


def _layer_spec(w, j):
    nd = w.ndim - 1
    return pl.BlockSpec((None,) + w.shape[1:], lambda *_: (j,) + (0,) * nd, pipeline_mode=pl.Buffered(1))


def _cast_weight_once(w_ref, wb_ref):
    @pl.when(pl.program_id(0) == 0)
    def _():
        rows = wb_ref.shape[-2]
        for r in range(0, rows, MXU_N):
            wb_ref[..., r:r + MXU_N, :] = w_ref[..., r:r + MXU_N, :].astype(BF16)


class Mixed(NamedTuple):
    y_parts: tuple
    w_out: jax.Array
    j: int
    mod: jax.Array


def _residual_specs(t, mixed, x_parts):
    k = mixed.w_out.shape[1]
    return (_part_specs(t, mixed.y_parts, k) + [_layer_spec(mixed.w_out, mixed.j)]
            + _part_specs(t, x_parts, D_MODEL) + [t.mod(2)])


def _residual_args(mixed, x_parts):
    return (*mixed.y_parts, mixed.w_out, *x_parts, mixed.mod)


def _run_halves(t, x_refs, res, g_ref, sh_ref, sc_ref, xo_ref, wo_ref, h_ref, proj_steps):
    n_sub = t.tm // PROJ_SUB
    rows_of = lambda sub: slice(sub * PROJ_SUB, (sub + 1) * PROJ_SUB)

    if res is not None:
        y_refs, gate_ref = res
        for sub in range(n_sub):
            rows = rows_of(sub)
            xo_ref[rows, :] = (_read_parts(t, x_refs, rows)
                               + gate_ref[...] * _dot(_read_parts(t, y_refs, rows), wo_ref[...]))

    def normalise(sub):
        rows = rows_of(sub)
        x = xo_ref[rows, :] if res is not None else _read_parts(t, x_refs, rows)
        h_ref[rows, :] = _norm_mod(x, g_ref, sh_ref, sc_ref)

    normalise(0)
    for sub in range(n_sub):
        side = [functools.partial(normalise, sub + 1)] if sub + 1 < n_sub else []
        _run_interleaved(proj_steps(sub), side)


def _rep4(a, h):
    half = a[:, (h // 2) * LANES:(h // 2 + 1) * LANES]
    lane = lax.broadcasted_iota(jnp.int32, half.shape, 1)
    keep = (lane < ATTN_HEAD_DIM) if h % 2 == 0 else (lane >= ATTN_HEAD_DIM)
    m = jnp.where(keep, half, 0.0)
    s = m + pltpu.roll(m, ATTN_HEAD_DIM, 1)
    return jnp.concatenate([s, s], axis=1)


def _attn_in_kernel(*refs, j, n_y, n_x, n_alias):
    t = PROJ
    n_res = n_y + n_x + 2 if n_y else n_x
    res_refs, refs = refs[:n_res], refs[n_res:]
    (g_ref, sh_ref, sc_ref, wf_ref, cos_ref, sin_ref), refs = refs[:6], refs[6 + n_alias:]
    if n_y:
        xo_ref, q_ref, k4_ref, v4_ref, z_ref, kc_ref, vc_ref, wo_ref, w_ref, h_ref, kv_ref = refs
        y_refs, wof_ref, x_refs, gate_ref = (res_refs[:n_y], res_refs[n_y], res_refs[n_y + 1:n_y + 1 + n_x],
                                             res_refs[n_y + 1 + n_x])
        _cast_weight_once(wof_ref, wo_ref)
        res = (y_refs, gate_ref)
    else:
        q_ref, k4_ref, v4_ref, z_ref, kc_ref, vc_ref, w_ref, h_ref, kv_ref = refs
        x_refs, res, xo_ref, wo_ref = res_refs, None, None, None
    i = pl.program_id(0)
    _cast_weight_once(wf_ref, w_ref)
    lane = lax.broadcasted_iota(jnp.int32, (PROJ_SUB, LANES), 1)
    first = (lane % (ATTN_HEAD_DIM // 2)) < ATTN_HEAD_DIM // 4
    scale = ATTN_HEAD_DIM ** -0.5 * LOG2_E
    k_lo = ATTN_WIDTH
    v_lo = ATTN_WIDTH + ATTN_KV_WIDTH
    z_lo = ATTN_WIDTH + 2 * ATTN_KV_WIDTH

    def proj_steps(sub):
        rows = slice(sub * PROJ_SUB, (sub + 1) * PROJ_SUB)

        def rope(a):
            rot = jnp.where(first, pltpu.roll(a, LANES - ATTN_HEAD_DIM // 4, 1),
                            pltpu.roll(a, ATTN_HEAD_DIM // 4, 1))
            return a * cos_ref[rows, :] + rot * sin_ref[rows, :]

        def rope_wide(a):
            return jnp.concatenate(
                [rope(a[:, s * LANES:(s + 1) * LANES]) for s in range(MXU_N // LANES)], axis=1)

        def proj(lo):
            return _dot(h_ref[rows, :], w_ref[:, lo:lo + MXU_N])

        def q_chunk(c):
            cols = slice(c * MXU_N, (c + 1) * MXU_N)
            q_ref[rows, cols] = (rope_wide(proj(c * MXU_N)) * scale).astype(BF16)

        def keys():
            k = rope_wide(proj(k_lo))
            kv_ref[rows, :ATTN_KV_WIDTH] = k
            for hh in range(ATTN_KV_HEADS):
                k4_ref[rows, hh * MXU_N:(hh + 1) * MXU_N] = _rep4(k, hh).astype(BF16)

        def values():
            v = proj(v_lo)
            kv_ref[rows, ATTN_KV_WIDTH:] = v
            for hh in range(ATTN_KV_HEADS):
                v4_ref[rows, hh * MXU_N:(hh + 1) * MXU_N] = _rep4(v, hh).astype(BF16)

        def z_chunk(c):
            cols = slice(c * MXU_N, (c + 1) * MXU_N)
            z_ref[rows, cols] = _silu(proj(z_lo + c * MXU_N)).astype(BF16)

        n_chunks = ATTN_WIDTH // MXU_N
        return ([functools.partial(q_chunk, c) for c in range(n_chunks)] + [keys, values]
                + [functools.partial(z_chunk, c) for c in range(n_chunks)])

    _run_halves(t, x_refs, res, g_ref, sh_ref, sc_ref, xo_ref, wo_ref, h_ref, proj_steps)

    @pl.when(i < t.n_prompt)
    def _():
        for s in range(t.tm // SEQ):
            kt = kv_ref[s * SEQ:(s + 1) * SEQ, :ATTN_KV_WIDTH].T
            vt = kv_ref[s * SEQ:(s + 1) * SEQ, ATTN_KV_WIDTH:].T
            if n_alias:
                kc_ref[s] = kt
                vc_ref[s] = vt
            else:
                for l in range(kc_ref.shape[1]):
                    kc_ref[s, l] = kt if l == j else jnp.zeros_like(kt)
                    vc_ref[s, l] = vt if l == j else jnp.zeros_like(vt)


def _attn_in(mixed, x_parts, g, mod, w_all, j, cos_t, sin_t, caches):
    t = PROJ
    n_in = 2 * ATTN_WIDTH + 2 * ATTN_KV_WIDTH
    per = t.tm // SEQ
    rope_spec = pl.BlockSpec(
        (t.tm, LANES), lambda i: (jnp.where(i < t.n_prompt, 0, 1 + t.seq_tile(i)), 0))
    wide = jax.ShapeDtypeStruct((N_TOK, ATTN_WIDTH), BF16)
    n_attn = w_all.shape[0]
    cache = jax.ShapeDtypeStruct((BATCH, n_attn, ATTN_KV_WIDTH, SEQ), F32)
    if caches:
        cache_spec = pl.BlockSpec((per, None, ATTN_KV_WIDTH, SEQ),
                                  lambda i: (jnp.minimum(i, t.n_prompt - 1), j, 0, 0))
    else:
        cache_spec = pl.BlockSpec((per, n_attn, ATTN_KV_WIDTH, SEQ),
                                  lambda i: (jnp.minimum(i, t.n_prompt - 1), 0, 0, 0))
    n_x = len(x_parts)
    if mixed is None:
        n_y, res_specs, res_args, res_out, res_ospecs, res_scratch = 0, _part_specs(t, x_parts, D_MODEL), x_parts, (), (), []
    else:
        n_y = len(mixed.y_parts)
        res_specs, res_args = _residual_specs(t, mixed, x_parts), _residual_args(mixed, x_parts)
        res_out = (jax.ShapeDtypeStruct((N_TOK, D_MODEL), F32),)
        res_ospecs = (t.rows(D_MODEL),)
        res_scratch = [pltpu.VMEM(mixed.w_out.shape[1:], BF16)]
    n_front = len(res_specs) + 6
    return pl.pallas_call(
        functools.partial(_attn_in_kernel, j=j, n_y=n_y, n_x=n_x, n_alias=len(caches)),
        out_shape=res_out + (wide, wide, wide, wide, cache, cache),
        grid=(t.n_tiles,),
        in_specs=res_specs + [
            _const_spec((1, D_MODEL)), t.mod(0), t.mod(1), _layer_spec(w_all, j),
            rope_spec, rope_spec] + [pl.BlockSpec(memory_space=pl.ANY)] * len(caches),
        out_specs=res_ospecs + (t.rows(ATTN_WIDTH),) * 4 + (cache_spec,) * 2,
        scratch_shapes=res_scratch + [pltpu.VMEM((D_MODEL, n_in), BF16), pltpu.VMEM((t.tm, D_MODEL), BF16),
                                      pltpu.VMEM((t.tm, 2 * ATTN_KV_WIDTH), F32)],
        input_output_aliases={n_front + c: len(res_out) + 4 + c for c in range(len(caches))},
        compiler_params=_params("arbitrary", vmem=FUSED_VMEM_LIMIT),
        name="attn_in",
    )(*res_args, g, mod, mod, w_all, cos_t, sin_t, *caches)


def _stack_group_queries(q):
    qf = q.astype(F32)
    chunk = lax.broadcasted_iota(jnp.int32, qf.shape, 1) // ATTN_HEAD_DIM
    return jnp.concatenate(
        [jnp.where(chunk == g, qf, 0.0) for g in range(ATTN_GROUP)], axis=0).astype(BF16)


def _gather_group_outputs(o, rows):
    chunk = lax.broadcasted_iota(jnp.int32, (rows, MXU_N), 1) // ATTN_HEAD_DIM
    acc = jnp.zeros((rows, MXU_N), F32)
    for g in range(ATTN_GROUP):
        acc = acc + jnp.where(chunk == g, o[g * rows:(g + 1) * rows], 0.0)
    return acc


def _sink_column(sink_ref, h, rows):
    grp = lax.broadcasted_iota(jnp.int32, (ATTN_GROUP * rows, 1), 0) // rows
    col = jnp.zeros((ATTN_GROUP * rows, 1), F32)
    for g in range(ATTN_GROUP):
        col = jnp.where(grp == g, sink_ref[h * ATTN_GROUP + g] * LOG2_E, col)
    return col


def _chunk_rows(dtype):
    chunk = lax.broadcasted_iota(jnp.int32, (1, MXU_N), 1) // ATTN_HEAD_DIM
    return [(chunk == g).astype(F32).astype(dtype) for g in range(ATTN_GROUP)]


def _block_diag_rows(x4):
    return jnp.concatenate([x4 * m for m in _chunk_rows(x4.dtype)], axis=0)


def _ctx_attn_kernel(sink_ref, q_ref, k4_ref, v4_ref, z_ref, y_ref):
    chunk = lax.broadcasted_iota(jnp.int32, (SEQ, MXU_N), 1) // ATTN_HEAD_DIM
    for b, h in itertools.product(range(CTX_ATTN_SEQS), range(ATTN_KV_HEADS)):
        rows = slice(b * SEQ, (b + 1) * SEQ)
        cols = slice(h * MXU_N, (h + 1) * MXU_N)
        s = _dot_nt(q_ref[rows, cols], _block_diag_rows(k4_ref[rows, cols]))
        inv = jnp.zeros((SEQ, MXU_N), F32)
        probs = []
        for g in range(ATTN_GROUP):
            sg = s[:, g * SEQ:(g + 1) * SEQ]
            sk = sink_ref[h * ATTN_GROUP + g] * LOG2_E
            m = jnp.maximum(jnp.max(sg, axis=1, keepdims=True), sk)
            e = jnp.exp2(sg - m)
            den = jnp.sum(e, axis=1, keepdims=True) + jnp.exp2(sk - m)
            probs.append(e.astype(BF16))
            inv = jnp.where(chunk == g, 1.0 / den, inv)
        o = _dot(jnp.concatenate(probs, axis=1), _block_diag_rows(v4_ref[rows, cols]))
        y_ref[rows, cols] = (o * inv * z_ref[rows, cols].astype(F32)).astype(BF16)


def _ctx_attn(sink, q, k4, v4, z):
    spec = pl.BlockSpec((CTX_ATTN_SEQS * SEQ, ATTN_WIDTH), lambda b: (b, 0))
    return pl.pallas_call(
        _ctx_attn_kernel,
        out_shape=jax.ShapeDtypeStruct((N_PROMPT_TOK, ATTN_WIDTH), BF16),
        grid=(BATCH // CTX_ATTN_SEQS,),
        in_specs=[pl.BlockSpec(memory_space=pltpu.SMEM), spec, spec, spec, spec],
        out_specs=spec,
        compiler_params=_params("arbitrary"),
        name="ctx_attn",
    )(sink, q, k4, v4, z)


LAT_STEP = Q_SUB * Q_BLOCK
LAT_PER_SEQ = DEC_SEQ // LAT_STEP
LAT_BLOCKS = DEC_BATCH * ATTN_KV_HEADS * LAT_PER_SEQ
assert LAT_BLOCKS % 2 == 0


def _lat_block(blk):
    return (blk // (ATTN_KV_HEADS * LAT_PER_SEQ), (blk // LAT_PER_SEQ) % ATTN_KV_HEADS, blk % LAT_PER_SEQ)


def _band_rows(qb):
    return pl.ds(pl.multiple_of(qb * Q_BLOCK, Q_BLOCK), 3 * Q_BLOCK)


def _pad_sequence(dst, src):
    zeros = jnp.zeros((Q_BLOCK, MXU_N), BF16)
    dst[0:Q_BLOCK, :] = zeros
    dst[Q_BLOCK:Q_BLOCK + DEC_SEQ, :] = src[...]
    dst[Q_BLOCK + DEC_SEQ:, :] = zeros


def _lat_attn_kernel(sink_ref, q_ref, k4_ref, v4_ref, kc_ref, vc_ref, z_ref, y_ref,
                     kp, vp, k4c, v4c, sc_a, sl_a, sc_b, sl_b, e_scr):
    t = pl.program_id(0)
    _, _, n1 = _lat_block(jnp.minimum(t, LAT_BLOCKS - 1))
    _, h0, n0 = _lat_block(jnp.maximum(t - 1, 0))

    @pl.when(n1 == 0)
    def _():
        _pad_sequence(kp, k4_ref)
        k4c[...] = jnp.concatenate([kc_ref[...]] * ATTN_GROUP, axis=0).astype(BF16)

    @pl.when(n0 == 0)
    def _():
        _pad_sequence(vp, v4_ref)
        v4c[...] = jnp.concatenate([vc_ref[...]] * ATTN_GROUP, axis=0).T.astype(BF16)

    stages = functools.partial(_lat_stages, sink_ref, q_ref, z_ref, y_ref, kp, vp, k4c, v4c, e_scr, n1, h0, n0)
    even = t % 2 == 0
    pl.when(t == 0)(functools.partial(stages, sc_a, sl_a, sc_b, sl_b, finish_previous=False))
    pl.when(jnp.logical_and(even, jnp.logical_and(t > 0, t < LAT_BLOCKS)))(
        functools.partial(stages, sc_a, sl_a, sc_b, sl_b))
    pl.when(t == LAT_BLOCKS)(functools.partial(stages, sc_a, sl_a, sc_b, sl_b, score_next=False))
    pl.when(jnp.logical_not(even))(functools.partial(stages, sc_b, sl_b, sc_a, sl_a))


def _lat_stages(sink_ref, q_ref, z_ref, y_ref, kp, vp, k4c, v4c, e_scr, n1, h0, n0,
                sc_w, sl_w, sc_r, sl_r, score_next=True, finish_previous=True):
    n_blocks = DEC_SEQ // Q_BLOCK
    rows4 = ATTN_GROUP * Q_BLOCK
    r = lax.broadcasted_iota(jnp.int32, (rows4, Q_BLOCK), 0) % Q_BLOCK
    c = lax.broadcasted_iota(jnp.int32, (rows4, Q_BLOCK), 1)
    in_left = c >= r
    in_right = c <= r
    sk = _sink_column(sink_ref, h0, Q_BLOCK)

    def scores(sub):
        qs = _stack_group_queries(q_ref[sub * Q_BLOCK:(sub + 1) * Q_BLOCK, :])
        sc_w[sub] = _dot(qs, k4c[...])
        sl_w[sub] = _dot_nt(qs, kp[_band_rows(n1 * Q_SUB + sub), :])

    def softmax(sub):
        qb = n0 * Q_SUB + sub
        s_ctx = sc_r[sub]
        s_lat = sl_r[sub]
        blocks = [
            s_ctx,
            jnp.where(jnp.logical_and(in_left, qb > 0), s_lat[:, :Q_BLOCK], NEG_INF),
            s_lat[:, Q_BLOCK:2 * Q_BLOCK],
            jnp.where(jnp.logical_and(in_right, qb < n_blocks - 1), s_lat[:, 2 * Q_BLOCK:], NEG_INF),
        ]
        slabs = [s[:, c:c + LANES] for s in blocks for c in range(0, s.shape[1], LANES)]
        m = jnp.maximum(sk, jnp.max(functools.reduce(jnp.maximum, slabs), axis=1, keepdims=True))
        es = [jnp.exp2(s - m) for s in slabs]
        for c, e in enumerate(es):
            e_scr[sub, :, c * LANES:(c + 1) * LANES] = e.astype(BF16)
        return jnp.exp2(sk - m) + jnp.sum(functools.reduce(jnp.add, es), axis=1, keepdims=True)

    def finish(sub, den):
        qrows = slice(sub * Q_BLOCK, (sub + 1) * Q_BLOCK)
        o = (_dot(e_scr[sub, :, :PAST_LEN], v4c[...])
             + _dot(e_scr[sub, :, PAST_LEN:], vp[_band_rows(n0 * Q_SUB + sub), :])) * (1.0 / den)
        acc = _gather_group_outputs(o, Q_BLOCK)
        y_ref[qrows, :] = (acc * z_ref[qrows, :].astype(F32)).astype(BF16)

    for sub in range(Q_SUB):
        den = softmax(sub) if finish_previous else None
        if score_next:
            scores(sub)
        if finish_previous:
            finish(sub, den)


def _lat_attn(sink, q, k4, v4, kc, vc, j, z):
    padded = DEC_SEQ + 2 * Q_BLOCK
    rows4 = ATTN_GROUP * Q_BLOCK
    row0 = N_PROMPT_TOK // LAT_STEP
    seq0 = N_PROMPT_TOK // DEC_SEQ
    stage1 = lambda t: _lat_block(jnp.minimum(t, LAT_BLOCKS - 1))
    stage2 = lambda t: _lat_block(jnp.maximum(t - 1, 0))

    def spec(shape, stage, index):
        return pl.BlockSpec(shape, lambda t: index(*stage(t)))

    tile = (LAT_STEP, MXU_N)
    seq = (DEC_SEQ, MXU_N)
    ctx = (None, None, None, ATTN_HEAD_DIM, PAST_LEN)
    return pl.pallas_call(
        _lat_attn_kernel,
        out_shape=jax.ShapeDtypeStruct((N_SAMPLE_TOK, ATTN_WIDTH), BF16),
        grid=(LAT_BLOCKS + 1,),
        in_specs=[pl.BlockSpec(memory_space=pltpu.SMEM),
                  spec(tile, stage1, lambda b, h, n: (row0 + b * LAT_PER_SEQ + n, h)),
                  spec(seq, stage1, lambda b, h, n: (seq0 + b, h)),
                  spec(seq, stage2, lambda b, h, n: (seq0 + b, h)),
                  spec(ctx, stage1, lambda b, h, n: (b, j, h, 0, 0)),
                  spec(ctx, stage2, lambda b, h, n: (b, j, h, 0, 0)),
                  spec(tile, stage2, lambda b, h, n: (row0 + b * LAT_PER_SEQ + n, h))],
        out_specs=spec(tile, stage2, lambda b, h, n: (b * LAT_PER_SEQ + n, h)),
        scratch_shapes=[pltpu.VMEM((padded, MXU_N), BF16), pltpu.VMEM((padded, MXU_N), BF16),
                        pltpu.VMEM((MXU_N, PAST_LEN), BF16), pltpu.VMEM((PAST_LEN, MXU_N), BF16),
                        pltpu.VMEM((Q_SUB, rows4, PAST_LEN), F32),
                        pltpu.VMEM((Q_SUB, rows4, 3 * Q_BLOCK), F32),
                        pltpu.VMEM((Q_SUB, rows4, PAST_LEN), F32),
                        pltpu.VMEM((Q_SUB, rows4, 3 * Q_BLOCK), F32),
                        pltpu.VMEM((Q_SUB, rows4, PAST_LEN + 3 * Q_BLOCK), BF16)],
        compiler_params=_params("arbitrary", vmem=VMEM_LIMIT),
        name="lat_attn",
    )(sink, q, k4, v4, kc, vc, z)


def _out_final_kernel(yp_ref, ys_ref, wf_ref, x_ref, gate_ref, fg_ref, op_ref, os_ref, w_ref):
    t = FINAL
    i = pl.program_id(0)
    _cast_weight_once(wf_ref, w_ref)
    y = _read_parts(t, (yp_ref, ys_ref))
    r = _rms(x_ref[...] + gate_ref[...] * _dot(y, w_ref[...]), fg_ref[...])

    @pl.when(i < t.n_prompt)
    def _():
        op_ref[...] = r

    @pl.when(i >= t.n_prompt)
    def _():
        os_ref[...] = r


def _out_proj_final(mixed, x, final_g):
    t = FINAL
    return pl.pallas_call(
        _out_final_kernel,
        out_shape=(jax.ShapeDtypeStruct((N_PROMPT_TOK, D_MODEL), F32),
                   jax.ShapeDtypeStruct((N_SAMPLE_TOK, D_MODEL), F32)),
        grid=(t.n_tiles,),
        in_specs=_residual_specs(t, mixed, (x,)) + [_const_spec((1, D_MODEL))],
        out_specs=(t.prompt_rows(D_MODEL), t.sample_rows(D_MODEL)),
        scratch_shapes=[pltpu.VMEM(mixed.w_out.shape[1:], BF16)],
        compiler_params=_params("arbitrary", vmem=VMEM_LIMIT),
        name="out_proj_final",
    )(*_residual_args(mixed, (x,)), final_g)


def _pool_in_kernel(*refs, n_y, n_x):
    t = PROJ
    n_res = n_y + n_x + 2
    res_refs, refs = refs[:n_res], refs[n_res:]
    g_ref, sh_ref, sc_ref, wf_ref, wn_ref, xo_ref, u_ref, z_ref, wnb_ref, wo_ref, w_ref, h_ref = refs
    y_refs, wof_ref, x_refs, gate_ref = (res_refs[:n_y], res_refs[n_y], res_refs[n_y + 1:n_y + 1 + n_x],
                                         res_refs[n_y + 1 + n_x])
    _cast_weight_once(wof_ref, wo_ref)
    _cast_weight_once(wf_ref, w_ref)
    wnb_ref[...] = wn_ref[...].astype(BF16)

    def proj_steps(sub):
        rows = slice(sub * PROJ_SUB, (sub + 1) * PROJ_SUB)

        def chunk(c):
            def step():
                cols = slice(c * MXU_N, (c + 1) * MXU_N)
                u_ref[rows, cols] = _dot(h_ref[rows, :], w_ref[:, cols])
            return step

        def gate_chunk(c):
            def step():
                a = _dot(h_ref[rows, :], w_ref[:, D_MODEL + c * MXU_N:D_MODEL + (c + 1) * MXU_N])
                z_ref[rows, c * MXU_N:(c + 1) * MXU_N] = _silu(a).astype(BF16)
            return step

        n = D_MODEL // MXU_N
        return [chunk(c) for c in range(n)] + [gate_chunk(c) for c in range(n)]

    _run_halves(t, x_refs, (y_refs, gate_ref), g_ref, sh_ref, sc_ref, xo_ref, wo_ref, h_ref, proj_steps)


def _pool_in(mixed, x_parts, g, mod, w_all, j, w_next_all, j_next):
    t = PROJ
    slab = jax.ShapeDtypeStruct((N_TOK, D_MODEL), F32)
    k_next, n_next = w_next_all.shape[1:]
    return pl.pallas_call(
        functools.partial(_pool_in_kernel, n_y=len(mixed.y_parts), n_x=len(x_parts)),
        out_shape=(slab, slab, jax.ShapeDtypeStruct((N_TOK, D_MODEL), BF16),
                   jax.ShapeDtypeStruct((k_next, n_next), BF16)),
        grid=(t.n_tiles,),
        in_specs=_residual_specs(t, mixed, x_parts) + [
            _const_spec((1, D_MODEL)), t.mod(0), t.mod(1), _layer_spec(w_all, j),
            pl.BlockSpec((None, k_next // t.n_tiles, n_next), lambda i: (j_next, i, 0))],
        out_specs=(t.rows(D_MODEL),) * 3 + (pl.BlockSpec((k_next // t.n_tiles, n_next), lambda i: (i, 0)),),
        scratch_shapes=[pltpu.VMEM(mixed.w_out.shape[1:], BF16), pltpu.VMEM((D_MODEL, 2 * D_MODEL), BF16),
                        pltpu.VMEM((t.tm, D_MODEL), BF16)],
        compiler_params=_params("arbitrary", vmem=FUSED_VMEM_LIMIT),
        name="pool_in",
    )(*_residual_args(mixed, x_parts), g, mod, mod, w_all, w_next_all)


def _split_bf16(a):
    hi = a.astype(BF16)
    return hi, (a - hi.astype(F32)).astype(BF16)


def _band_ones(shape, lo, hi):
    d = lax.broadcasted_iota(jnp.int32, shape, 1) - lax.broadcasted_iota(jnp.int32, shape, 0)
    return jnp.logical_and(d >= lo, d <= hi).astype(F32).astype(BF16)


def _pool_ret_kernel(u_ref, up_ref, un_ref, z_ref, wgf_ref, ps_ref, wof_ref, x_ref, gate_ref,
                     g_ref, sh_ref, sc_ref, w_ref,
                     xo_ref, q_ref, kt_ref, v_ref, zr_ref,
                     y_ref, wg_ref, wo_ref, band_ref, wkt_ref, h_ref):
    t = PROJ
    i = pl.program_id(0)
    _cast_weight_once(wgf_ref, wg_ref)
    _cast_weight_once(wof_ref, wo_ref)

    @pl.when(i == 0)
    def _():
        for g, w in enumerate(POOL_WINDOWS):
            band_ref[g] = _band_ones((POOL_SUB, POOL_SUB), -(w // 2), w - 1 - w // 2)
        for r in range(0, RET_QK_WIDTH, MXU_N):
            wk = w_ref[:, RET_QK_WIDTH + r:RET_QK_WIDTH + r + MXU_N]
            wkt_ref[r:r + MXU_N, :] = wk.astype(F32).T.astype(BF16)

    is_dec = i >= t.n_prompt
    st = t.seq_tile(i)
    seq_len = jnp.where(is_dec, DEC_SEQ, SEQ)
    h = POOL_HALO
    n_sub = t.tm // POOL_SUB
    def pool_steps(sub):
        rows = slice(sub * POOL_SUB, (sub + 1) * POOL_SUB)
        if sub == 0:
            before = jnp.where(jnp.logical_and(is_dec, st != 0), up_ref[...], 0.0)
        else:
            before = jnp.where(is_dec, u_ref[sub * POOL_SUB - h:sub * POOL_SUB, :], 0.0)
        if sub == n_sub - 1:
            after = jnp.where(jnp.logical_and(is_dec, st != t.per_dec_seq - 1), un_ref[...], 0.0)
        else:
            after = jnp.where(is_dec, u_ref[(sub + 1) * POOL_SUB:(sub + 1) * POOL_SUB + h, :], 0.0)
        pos0 = jnp.where(is_dec, st * t.tm + sub * POOL_SUB, 0)
        return _pool_core_steps(u_ref.at[rows], before, after, pos0, seq_len, z_ref.at[rows], ps_ref,
                                wg_ref, band_ref, y_ref.at[rows])

    def mix_steps(sub):
        rows = slice(sub * POOL_SUB, (sub + 1) * POOL_SUB)

        def residual():
            xo_ref[rows, :] = x_ref[rows, :] + gate_ref[...] * _dot(y_ref[rows, :], wo_ref[...])

        def normalise():
            h_ref[rows, :] = _norm_mod(xo_ref[rows, :], g_ref, sh_ref, sc_ref)

        return pool_steps(sub) + [residual, normalise]

    for step in mix_steps(0):
        step()
    for sub in range(n_sub):
        rows = slice(sub * POOL_SUB, (sub + 1) * POOL_SUB)
        proj = _ret_in_steps(h_ref.at[rows], w_ref, wkt_ref, q_ref.at[rows],
                             kt_ref.at[pl.ds(sub * POOL_SUB // RET_CHUNK, POOL_SUB // RET_CHUNK)],
                             v_ref.at[rows], zr_ref.at[rows])
        _run_interleaved(proj, mix_steps(sub + 1) if sub + 1 < n_sub else [])


def _pool_core_steps(u_ref, before, after, pos0, seq_len, z_ref, ps_ref, wg_ref, band_ref, y_ref):
    n = u_ref.shape[0]
    h = POOL_HALO
    halo_hi, halo_lo = _split_bf16(jnp.concatenate([before, after], axis=0))
    pos = pos0 + lax.broadcasted_iota(jnp.int32, (n, 1), 0)
    rr = lax.broadcasted_iota(jnp.int32, (2 * h, 2 * h), 0)
    cc = lax.broadcasted_iota(jnp.int32, (2 * h, 2 * h), 1)

    def pooled(g, w):
        def step():
            left = w // 2
            right = w - 1 - left
            cols = slice(g * POOL_GROUP_DIM, (g + 1) * POOL_GROUP_DIM)
            u = u_ref[:, cols]
            s = _dot(band_ref[g], u.astype(BF16))
            top = jnp.logical_and(jnp.logical_and(rr < h, cc < h), cc - h >= rr - left)
            bot = jnp.logical_and(jnp.logical_and(rr >= h, cc >= h), cc - h <= rr - 2 * h + right)
            edge = jnp.logical_or(top, bot).astype(F32).astype(BF16)
            se = _dot(edge, halo_hi[:, cols]) + _dot(edge, halo_lo[:, cols])
            s = jnp.concatenate([s[:h] + se[:h], s[h:n - h], s[n - h:] + se[h:]], axis=0)
            cnt = (jnp.minimum(pos + right + 1, seq_len) - jnp.maximum(pos - left, 0)).astype(F32)
            y_ref[:, cols] = (s / cnt - u).astype(BF16)
        return step

    def mixed(g):
        def step():
            cols = slice(g * POOL_GROUP_DIM, (g + 1) * POOL_GROUP_DIM)
            yg = _dot(y_ref[:, cols], wg_ref[g]) * ps_ref[:, cols] * z_ref[:, cols].astype(F32)
            y_ref[:, cols] = yg.astype(BF16)
        return step

    return ([pooled(g, w) for g, w in enumerate(POOL_WINDOWS)]
            + [mixed(g) for g in range(len(POOL_WINDOWS))])


def _pool_ret(u, z, wg_all, ps, wo_all, j, x, mod_pool, g, mod, w_ret):
    t = PROJ
    per = t.tm // POOL_HALO
    n_halo = N_TOK // POOL_HALO
    n_in = 2 * RET_QK_WIDTH + 2 * RET_V_WIDTH
    kt_per = t.tm // RET_CHUNK
    return pl.pallas_call(
        _pool_ret_kernel,
        out_shape=(jax.ShapeDtypeStruct((N_TOK, D_MODEL), F32),
                   jax.ShapeDtypeStruct((N_TOK, RET_QK_WIDTH), BF16),
                   jax.ShapeDtypeStruct((N_TOK // RET_CHUNK, RET_QK_WIDTH, RET_CHUNK), BF16),
                   jax.ShapeDtypeStruct((N_TOK, RET_V_WIDTH), BF16),
                   jax.ShapeDtypeStruct((N_TOK, RET_V_WIDTH), BF16)),
        grid=(t.n_tiles,),
        in_specs=[
            t.rows(D_MODEL),
            pl.BlockSpec((POOL_HALO, D_MODEL), lambda i: (jnp.maximum(i * per - 1, 0), 0)),
            pl.BlockSpec((POOL_HALO, D_MODEL), lambda i: (jnp.minimum((i + 1) * per, n_halo - 1), 0)),
            t.rows(D_MODEL),
            _layer_spec(wg_all, j),
            _const_spec((1, D_MODEL)),
            _layer_spec(wo_all, j),
            t.rows(D_MODEL),
            t.mod(2),
            _const_spec((1, D_MODEL)), t.mod(0), t.mod(1),
            pl.BlockSpec((D_MODEL, n_in), lambda i: (0, 0), pipeline_mode=pl.Buffered(1)),
        ],
        out_specs=(t.rows(D_MODEL), t.rows(RET_QK_WIDTH),
                   pl.BlockSpec((kt_per, RET_QK_WIDTH, RET_CHUNK), lambda i: (i, 0, 0)),
                   t.rows(RET_V_WIDTH), t.rows(RET_V_WIDTH)),
        scratch_shapes=[pltpu.VMEM((t.tm, D_MODEL), BF16),
                        pltpu.VMEM((len(POOL_WINDOWS), POOL_GROUP_DIM, POOL_GROUP_DIM), BF16),
                        pltpu.VMEM((D_MODEL, D_MODEL), BF16),
                        pltpu.VMEM((len(POOL_WINDOWS), POOL_SUB, POOL_SUB), BF16),
                        pltpu.VMEM((RET_QK_WIDTH, D_MODEL), BF16),
                        pltpu.VMEM((t.tm, D_MODEL), BF16)],
        compiler_params=_params("arbitrary", vmem=FUSED_VMEM_LIMIT),
        name="pool_ret",
    )(u, u, u, z, wg_all, ps, wo_all, x, mod_pool, g, mod, mod, w_ret)


def _ret_in_steps(h_ref, w_ref, wkt_ref, q_ref, kt_ref, v_ref, z_ref):
    n_rows = h_ref.shape[0]

    def chunk(ref, lo, c, post=lambda a: a):
        def step():
            cols = slice(c * MXU_N, (c + 1) * MXU_N)
            ref[:, cols] = post(_dot(h_ref[...], w_ref[:, lo + c * MXU_N:lo + (c + 1) * MXU_N])).astype(BF16)
        return step

    def key_chunk(c):
        def step():
            rows = slice(c * MXU_N, (c + 1) * MXU_N)
            kt = (_dot_nt(wkt_ref[rows, :], h_ref[...]) * RET_KEY_DIM ** -0.5).astype(BF16)
            for cc in range(n_rows // RET_CHUNK):
                kt_ref[cc, rows, :] = kt[:, cc * RET_CHUNK:(cc + 1) * RET_CHUNK]
        return step

    steps = [chunk(q_ref, 0, c) for c in range(RET_QK_WIDTH // MXU_N)]
    steps += [key_chunk(c) for c in range(RET_QK_WIDTH // MXU_N)]
    steps += [chunk(v_ref, 2 * RET_QK_WIDTH, c) for c in range(RET_V_WIDTH // MXU_N)]
    steps += [chunk(z_ref, 2 * RET_QK_WIDTH + RET_V_WIDTH, c, _silu) for c in range(RET_V_WIDTH // MXU_N)]
    return steps


def _run_interleaved(main, side):
    done = 0
    for k, step in enumerate(main):
        step()
        due = (k + 1) * len(side) // len(main)
        for s in side[done:due]:
            s()
        done = due


def _pos(shape, axis):
    return lax.broadcasted_iota(jnp.int32, shape, axis).astype(F32)


def _ret_tables_kernel(lgf_ref, lgb_ref, decay_ref, row_ref, col_ref, cdec_ref):
    h = pl.program_id(0)
    lg_f = lgf_ref[h]
    lg_b = lgb_ref[h]
    c = RET_CHUNK
    diff = _pos((c, c), 0) - _pos((c, c), 1)
    fwd = jnp.where(diff >= 0, jnp.exp(jnp.maximum(diff, 0.0) * lg_f), 0.0)
    bwd = jnp.where(diff <= 0, jnp.exp(jnp.maximum(-diff, 0.0) * lg_b), 0.0)
    decay_ref[...] = fwd + bwd
    j = _pos((RET_TAB_ROWS, c), 1)
    row_ref[0] = jnp.exp((c - 1.0 - j) * lg_f)
    row_ref[1] = jnp.exp(j * lg_b)
    i = _pos((c, LANES), 0)
    col_ref[0] = jnp.exp((i + 1.0) * lg_f)
    col_ref[1] = jnp.exp((c - i) * lg_b)
    full = jnp.full((RET_TAB_ROWS, RET_VAL_DIM), float(c), F32)
    cdec_ref[0] = jnp.exp(full * lg_f)
    cdec_ref[1] = jnp.exp(full * lg_b)


def _ret_tables(lg_f, lg_b):
    smem = pl.BlockSpec(memory_space=pltpu.SMEM)
    c = RET_CHUNK
    shapes = ((c, c), (2, RET_TAB_ROWS, c), (2, c, LANES), (2, RET_TAB_ROWS, RET_VAL_DIM))
    return pl.pallas_call(
        _ret_tables_kernel,
        out_shape=tuple(jax.ShapeDtypeStruct((RET_HEADS,) + s, F32) for s in shapes),
        grid=(RET_HEADS,),
        in_specs=[smem, smem],
        out_specs=tuple(pl.BlockSpec((None,) + s, lambda h, n=len(s): (h,) + (0,) * n) for s in shapes),
        compiler_params=_params("arbitrary"),
        name="ret_tables",
    )(lg_f, lg_b)


def _group_norm_gate(o, gn, z):
    mu = jnp.mean(o, axis=-1, keepdims=True)
    var = jnp.mean(jnp.square(o - mu), axis=-1, keepdims=True)
    on = (o - mu) * lax.rsqrt(var + EPS)
    return (on * gn * z.astype(F32)).astype(BF16)


def _ret_ctx_kernel(q_ref, kt_ref, v_ref, z_ref, gn_ref, decay_ref, row_ref, y_ref, sf_ref, sb_ref):
    for s in range(RET_CTX_SEQS):
        rows = slice(s * SEQ, (s + 1) * SEQ)
        for h in range(RET_HEADS):
            kc = slice(h * RET_KEY_DIM, (h + 1) * RET_KEY_DIM)
            vc = slice(h * RET_VAL_DIM, (h + 1) * RET_VAL_DIM)
            kt = kt_ref[s, kc, :]
            v = v_ref[rows, vc]
            att = (_dot(q_ref[rows, kc], kt) * decay_ref[h]).astype(BF16)
            y_ref[rows, vc] = _group_norm_gate(_dot(att, v), gn_ref[:, vc], z_ref[rows, vc])
            ktf = kt.astype(F32)
            sf_ref[s, h] = _dot((ktf * row_ref[h, 0, 0:1, :]).astype(BF16), v)
            sb_ref[s, h] = _dot((ktf * row_ref[h, 1, 0:1, :]).astype(BF16), v)


def _ret_ctx(q, kt, v, z, gn, decay, row):
    c = RET_CHUNK
    n = RET_CTX_SEQS
    wide = lambda w: pl.BlockSpec((n * SEQ, w), lambda b: (b, 0))
    st_spec = pl.BlockSpec((n, RET_HEADS, RET_KEY_DIM, RET_VAL_DIM), lambda b: (b, 0, 0, 0))
    st_shape = jax.ShapeDtypeStruct((BATCH, RET_HEADS, RET_KEY_DIM, RET_VAL_DIM), F32)
    return pl.pallas_call(
        _ret_ctx_kernel,
        out_shape=(jax.ShapeDtypeStruct((N_PROMPT_TOK, RET_V_WIDTH), BF16), st_shape, st_shape),
        grid=(BATCH // n,),
        in_specs=[wide(RET_QK_WIDTH), pl.BlockSpec((n, RET_QK_WIDTH, c), lambda b: (b, 0, 0)),
                  wide(RET_V_WIDTH), wide(RET_V_WIDTH), _const_spec((1, RET_V_WIDTH)),
                  _const_spec((RET_HEADS, c, c)), _const_spec((RET_HEADS, 2, RET_TAB_ROWS, c))],
        out_specs=(wide(RET_V_WIDTH), st_spec, st_spec),
        compiler_params=_params("arbitrary", vmem=VMEM_LIMIT),
        name="ret_ctx",
    )(q, kt, v, z, gn, decay, row)


def _ret_lat_kernel(q_ref, kt_ref, v_ref, z_ref, gn_ref, decay_ref, row_ref, col_ref, cdec_ref,
                    s0f_ref, s0b_ref, y_ref, sf_all, sb_all, sf_acc, sb_acc):
    c = RET_CHUNK
    n_chunks = DEC_SEQ // c
    rows_of = lambda ci: pl.ds(pl.multiple_of(ci * c, c), c)

    sf_acc[...] = s0f_ref[...]
    sb_acc[...] = s0b_ref[...]

    def scan_step(i, carry):
        cf = i
        cb = n_chunks - 1 - i
        sf_all[cf] = sf_acc[...].astype(BF16)
        sb_all[cb] = sb_acc[...].astype(BF16)
        uf = _dot((kt_ref[cf].astype(F32) * row_ref[0, 0:1, :]).astype(BF16), v_ref[rows_of(cf), :])
        ub = _dot((kt_ref[cb].astype(F32) * row_ref[1, 0:1, :]).astype(BF16), v_ref[rows_of(cb), :])
        sf_acc[...] = sf_acc[...] * cdec_ref[0, 0:1, :] + uf
        sb_acc[...] = sb_acc[...] * cdec_ref[1, 0:1, :] + ub
        return carry

    lax.fori_loop(0, n_chunks, scan_step, 0, unroll=4)

    def out_step(ci, carry):
        rows = rows_of(ci)
        q = q_ref[rows, :]
        qf = q.astype(F32)
        qdec_f = jnp.concatenate([col_ref[0]] * (RET_KEY_DIM // LANES), axis=1)
        qdec_b = jnp.concatenate([col_ref[1]] * (RET_KEY_DIM // LANES), axis=1)
        att = (_dot(q, kt_ref[ci]) * decay_ref[...]).astype(BF16)
        o = (_dot(att, v_ref[rows, :])
             + _dot((qf * qdec_f).astype(BF16), sf_all[ci])
             + _dot((qf * qdec_b).astype(BF16), sb_all[ci]))
        y_ref[rows, :] = _group_norm_gate(o, gn_ref[...], z_ref[rows, :])
        return carry

    lax.fori_loop(0, n_chunks, out_step, 0, unroll=8)


def _ret_lat(q, kt, v, z, gn, decay, row, col, cdec, s0f, s0b):
    c = RET_CHUNK
    n_chunks = DEC_SEQ // c
    row0 = N_PROMPT_TOK // DEC_SEQ
    qk_spec = pl.BlockSpec((DEC_SEQ, RET_KEY_DIM), lambda b, h: (row0 + b, h))
    v_spec = pl.BlockSpec((DEC_SEQ, RET_VAL_DIM), lambda b, h: (row0 + b, h))
    st_spec = pl.BlockSpec((None, None, RET_KEY_DIM, RET_VAL_DIM), lambda b, h: (b, h, 0, 0))
    tab = lambda *s: pl.BlockSpec((None,) + s, lambda b, h: (h,) + (0,) * len(s))
    states = pltpu.VMEM((n_chunks, RET_KEY_DIM, RET_VAL_DIM), BF16)
    acc = pltpu.VMEM((RET_KEY_DIM, RET_VAL_DIM), F32)
    return pl.pallas_call(
        _ret_lat_kernel,
        out_shape=jax.ShapeDtypeStruct((N_SAMPLE_TOK, RET_V_WIDTH), BF16),
        grid=(DEC_BATCH, RET_HEADS),
        in_specs=[qk_spec,
                  pl.BlockSpec((n_chunks, RET_KEY_DIM, c), lambda b, h: (row0 + b, h, 0)),
                  v_spec, v_spec, pl.BlockSpec((1, RET_VAL_DIM), lambda b, h: (0, h)),
                  tab(c, c), tab(2, RET_TAB_ROWS, c), tab(2, c, LANES), tab(2, RET_TAB_ROWS, RET_VAL_DIM),
                  st_spec, st_spec],
        out_specs=pl.BlockSpec((DEC_SEQ, RET_VAL_DIM), lambda b, h: (b, h)),
        scratch_shapes=[states, states, acc, acc],
        compiler_params=_params("arbitrary", "arbitrary", vmem=VMEM_LIMIT),
        name="ret_lat",
    )(q, kt, v, z, gn, decay, row, col, cdec, s0f, s0b)


def _rope_tables(tm):
    n_rows = DEC_SEQ // GRID_W
    rows = jnp.repeat(jnp.arange(n_rows), GRID_W).astype(F32)
    cols = jnp.tile(jnp.arange(GRID_W), n_rows).astype(F32)
    half = ATTN_HEAD_DIM // 4
    inv = ROPE_BASE ** (-jnp.arange(half, dtype=F32) / half)
    ang_r = rows[:, None] * inv[None, :]
    ang_c = cols[:, None] * inv[None, :]
    cos = jnp.concatenate([jnp.cos(ang_r), jnp.cos(ang_r), jnp.cos(ang_c), jnp.cos(ang_c)], axis=-1)
    sin = jnp.concatenate([-jnp.sin(ang_r), jnp.sin(ang_r), -jnp.sin(ang_c), jnp.sin(ang_c)], axis=-1)
    cos = jnp.concatenate([jnp.ones((tm, ATTN_HEAD_DIM), F32), cos], axis=0)
    sin = jnp.concatenate([jnp.zeros((tm, ATTN_HEAD_DIM), F32), sin], axis=0)
    return jnp.tile(cos, (1, 2)), jnp.tile(sin, (1, 2))


def kernel(x_prompt, x_sample, cache_k, cache_v, state_fwd, state_bwd, c, c_ctx, norm_g, ada_w, ada_b, attn_w_in, attn_w_out, attn_sink, pool_w_in, pool_w_grp, pool_scale, pool_w_out, ret_w_in, ret_decay_fwd, ret_decay_bwd, ret_gn_g, ret_w_out, final_g):
    x_parts = (x_prompt.reshape(N_PROMPT_TOK, D_MODEL), x_sample.reshape(N_SAMPLE_TOK, D_MODEL))
    cond = jnp.concatenate([c_ctx[None, :], c,
                            jnp.zeros((N_COND - 1 - DEC_BATCH, D_MODEL), F32)], axis=0)
    mods = _ada_table(cond.T, ada_w, ada_b).reshape(DEPTH, N_COND, 1, 3 * D_MODEL)
    cos_t, sin_t = _rope_tables(PROJ.tm)

    to_kernel = lambda a: jnp.transpose(a, (0, 1, 3, 4, 2))
    from_kernel = lambda a: jnp.transpose(
        a.reshape(a.shape[0], a.shape[1], ATTN_KV_HEADS, ATTN_HEAD_DIM, a.shape[3]), (0, 1, 4, 2, 3))
    ctx_k, ctx_v = to_kernel(cache_k), to_kernel(cache_v)

    assert DEPTH % N_MIXERS == 1, "the layer stack must end on an attention layer"
    caches = ()
    new_sf = new_sb = None
    mixed = None
    for i in range(DEPTH):
        kind, j = i % N_MIXERS, i // N_MIXERS
        g = norm_g[i].reshape(1, D_MODEL)
        mod = mods[i]
        if kind == 0:
            outs = _attn_in(mixed, x_parts, g, mod, attn_w_in, j, cos_t, sin_t, tuple(caches))
            if mixed is not None:
                x_parts, outs = (outs[0],), outs[1:]
            q, k4, v4, z, *caches = outs
            y_parts = (_ctx_attn(attn_sink[j], q, k4, v4, z),
                       _lat_attn(attn_sink[j], q, k4, v4, ctx_k, ctx_v, j, z))
            mixed = Mixed(y_parts, attn_w_out, j, mod)
        elif kind == 1:
            x, u, z, w_ret = _pool_in(mixed, x_parts, g, mod, pool_w_in, j, ret_w_in, j)
            x_parts, mixed, pooled = (x,), None, (u, z, j, mod, w_ret)
        else:
            u, z, jp, mod_pool, w_ret = pooled
            lg_f = jax.nn.log_sigmoid(ret_decay_fwd[j].astype(F32))
            lg_b = jax.nn.log_sigmoid(ret_decay_bwd[j].astype(F32))
            gn = ret_gn_g[j].reshape(1, RET_V_WIDTH)
            x, q, kt, v, z = _pool_ret(u, z, pool_w_grp, pool_scale[jp].reshape(1, D_MODEL), pool_w_out, jp,
                                       x_parts[0], mod_pool, g, mod, w_ret)
            x_parts = (x,)
            decay, row, col, cdec = _ret_tables(lg_f, lg_b)
            y_ctx, new_sf, new_sb = _ret_ctx(q, kt, v, z, gn, decay, row)
            y_parts = (y_ctx, _ret_lat(q, kt, v, z, gn, decay, row, col, cdec,
                                       state_fwd[:, j], state_bwd[:, j]))
            mixed = Mixed(y_parts, ret_w_out, j, mod)
    y_prompt, y_sample = _out_proj_final(mixed, x_parts[0], final_g.reshape(1, D_MODEL))
    new_k, new_v = caches
    return (y_prompt.reshape(BATCH, SEQ, D_MODEL), y_sample.reshape(DEC_BATCH, DEC_SEQ, D_MODEL),
            from_kernel(new_k), from_kernel(new_v), new_sf[:, None], new_sb[:, None])
```

```python
import functools
import itertools
from typing import NamedTuple

import jax
import jax.numpy as jnp
from jax import lax
from jax.experimental import pallas as pl
from jax.experimental.pallas import tpu as pltpu

F32 = jnp.float32
BF16 = jnp.bfloat16

D_MODEL = 1024
BATCH = 16
SEQ = 256
DEPTH = 4
DEC_BATCH = 2
DEC_SEQ = 2048
PAST_LEN = 512
GRID_W = 64
N_MIXERS = 3
ATTN_HEADS = 16
ATTN_KV_HEADS = 4
ATTN_HEAD_DIM = 64
ATTN_GROUP = 4
ATTN_WIDTH = 1024
ATTN_KV_WIDTH = 256
WINDOW = 128
ROPE_BASE = 10000.0
POOL_WINDOWS = (2, 4, 8, 16)
POOL_GROUP_DIM = 256
RET_HEADS = 4
RET_KEY_DIM = 256
RET_VAL_DIM = 512
RET_QK_WIDTH = 1024
RET_V_WIDTH = 2048
EPS = 1e-6
NEG_INF = -1e30
LOG2_E = 1.4426950408889634

N_PROMPT_TOK = BATCH * SEQ
N_SAMPLE_TOK = DEC_BATCH * DEC_SEQ
N_TOK = N_PROMPT_TOK + N_SAMPLE_TOK
N_COND = 8
LANES = 128
MXU_N = 256
Q_BLOCK = 128
Q_SUB = 4
RET_CHUNK = 256
RET_TAB_ROWS = 8
RET_CTX_SEQS = 2
CTX_ATTN_SEQS = 2
POOL_HALO = 8
POOL_SUB = SEQ
PROJ_SUB = 256
VMEM_LIMIT = 48 * 1024 * 1024
FUSED_VMEM_LIMIT = 58 * 1024 * 1024


class Tiling(NamedTuple):
    tm: int

    @property
    def n_tiles(self):
        return N_TOK // self.tm

    @property
    def n_prompt(self):
        return N_PROMPT_TOK // self.tm

    @property
    def per_dec_seq(self):
        return DEC_SEQ // self.tm

    def cond(self, i):
        return jnp.where(i < self.n_prompt, 0, 1 + (i - self.n_prompt) // self.per_dec_seq)

    def seq_tile(self, i):
        return jnp.where(i < self.n_prompt, 0, (i - self.n_prompt) % self.per_dec_seq)

    def rows(self, width):
        return pl.BlockSpec((self.tm, width), lambda i: (i, 0))

    def prompt_rows(self, width):
        return pl.BlockSpec((self.tm, width), lambda i: (jnp.minimum(i, self.n_prompt - 1), 0))

    def sample_rows(self, width):
        return pl.BlockSpec((self.tm, width), lambda i: (jnp.maximum(i - self.n_prompt, 0), 0))

    def mod(self, part):
        return pl.BlockSpec((None, 1, D_MODEL), lambda i: (self.cond(i), 0, part))


PROJ = Tiling(512)
FINAL = Tiling(1024)


def _silu(z):
    hz = 0.5 * z
    return hz + hz * jnp.tanh(hz)


def _dot(a, b):
    return jnp.dot(a, b, preferred_element_type=F32)


def _dot_nt(a, b):
    return lax.dot_general(a, b, (((1,), (1,)), ((), ())), preferred_element_type=F32)


def _params(*sem, vmem=None):
    return pltpu.CompilerParams(dimension_semantics=sem, vmem_limit_bytes=vmem)


def _const_spec(shape):
    nd = len(shape)
    return pl.BlockSpec(shape, lambda *_: (0,) * nd, pipeline_mode=pl.Buffered(1))


def _part_specs(t, parts, width):
    if len(parts) == 1:
        return [t.rows(width)]
    return [t.prompt_rows(width), t.sample_rows(width)]


def _read_parts(t, refs, rows=slice(None)):
    if len(refs) == 1:
        return refs[0][rows, :]
    return jnp.where(pl.program_id(0) < t.n_prompt, refs[0][rows, :], refs[1][rows, :])


def _ada_kernel(cond_ref, w_ref, b_ref, o_ref):
    s = _silu(cond_ref[...])
    w = w_ref[...]
    rows = [jnp.sum(s[:, c:c + 1] * w, axis=0, keepdims=True) + b_ref[...] for c in range(1 + DEC_BATCH)]
    rows.append(jnp.zeros((N_COND - len(rows), w.shape[1]), F32))
    o_ref[...] = jnp.concatenate(rows, axis=0)


def _ada_table(cond, ada_w, ada_b):
    tn = 3 * D_MODEL // 2
    return pl.pallas_call(
        _ada_kernel,
        out_shape=jax.ShapeDtypeStruct((DEPTH, N_COND, 3 * D_MODEL), F32),
        grid=(DEPTH, 3 * D_MODEL // tn),
        in_specs=[
            pl.BlockSpec((D_MODEL, N_COND), lambda l, n: (0, 0)),
            pl.BlockSpec((None, D_MODEL, tn), lambda l, n: (l, 0, n)),
            pl.BlockSpec((None, 1, tn), lambda l, n: (l, 0, n)),
        ],
        out_specs=pl.BlockSpec((None, N_COND, tn), lambda l, n: (l, 0, n)),
        compiler_params=_params("arbitrary", "arbitrary", vmem=VMEM_LIMIT),
        name="ada_table",
    )(cond, ada_w, ada_b.reshape(DEPTH, 1, 3 * D_MODEL))


def _rms(x, g):
    return x * lax.rsqrt(jnp.mean(x * x, axis=-1, keepdims=True) + EPS) * g


def _norm_mod(x, g_ref, sh_ref, sc_ref):
    return (_rms(x, g_ref[...]) * (1.0 + sc_ref[...]) + sh_ref[...]).astype(BF16)


def _layer_spec(w, j):
    nd = w.ndim - 1
    return pl.BlockSpec((None,) + w.shape[1:], lambda *_: (j,) + (0,) * nd, pipeline_mode=pl.Buffered(1))


def _cast_weight_once(w_ref, wb_ref):
    @pl.when(pl.program_id(0) == 0)
    def _():
        rows = wb_ref.shape[-2]
        for r in range(0, rows, MXU_N):
            wb_ref[..., r:r + MXU_N, :] = w_ref[..., r:r + MXU_N, :].astype(BF16)


class Mixed(NamedTuple):
    y_parts: tuple
    w_out: jax.Array
    j: int
    mod: jax.Array


def _residual_specs(t, mixed, x_parts):
    k = mixed.w_out.shape[1]
    return (_part_specs(t, mixed.y_parts, k) + [_layer_spec(mixed.w_out, mixed.j)]
            + _part_specs(t, x_parts, D_MODEL) + [t.mod(2)])


def _residual_args(mixed, x_parts):
    return (*mixed.y_parts, mixed.w_out, *x_parts, mixed.mod)


def _run_halves(t, x_refs, res, g_ref, sh_ref, sc_ref, xo_ref, wo_ref, h_ref, proj_steps):
    n_sub = t.tm // PROJ_SUB
    rows_of = lambda sub: slice(sub * PROJ_SUB, (sub + 1) * PROJ_SUB)

    if res is not None:
        y_refs, gate_ref = res
        for sub in range(n_sub):
            rows = rows_of(sub)
            xo_ref[rows, :] = (_read_parts(t, x_refs, rows)
                               + gate_ref[...] * _dot(_read_parts(t, y_refs, rows), wo_ref[...]))

    def normalise(sub):
        rows = rows_of(sub)
        x = xo_ref[rows, :] if res is not None else _read_parts(t, x_refs, rows)
        h_ref[rows, :] = _norm_mod(x, g_ref, sh_ref, sc_ref)

    normalise(0)
    for sub in range(n_sub):
        side = [functools.partial(normalise, sub + 1)] if sub + 1 < n_sub else []
        _run_interleaved(proj_steps(sub), side)


def _rep4(a, h):
    half = a[:, (h // 2) * LANES:(h // 2 + 1) * LANES]
    lane = lax.broadcasted_iota(jnp.int32, half.shape, 1)
    keep = (lane < ATTN_HEAD_DIM) if h % 2 == 0 else (lane >= ATTN_HEAD_DIM)
    m = jnp.where(keep, half, 0.0)
    s = m + pltpu.roll(m, ATTN_HEAD_DIM, 1)
    return jnp.concatenate([s, s], axis=1)


def _attn_in_kernel(*refs, j, n_y, n_x, n_alias):
    t = PROJ
    n_res = n_y + n_x + 2 if n_y else n_x
    res_refs, refs = refs[:n_res], refs[n_res:]
    (g_ref, sh_ref, sc_ref, wf_ref, cos_ref, sin_ref), refs = refs[:6], refs[6 + n_alias:]
    if n_y:
        xo_ref, q_ref, k4_ref, v4_ref, z_ref, kc_ref, vc_ref, wo_ref, w_ref, h_ref, kv_ref = refs
        y_refs, wof_ref, x_refs, gate_ref = (res_refs[:n_y], res_refs[n_y], res_refs[n_y + 1:n_y + 1 + n_x],
                                             res_refs[n_y + 1 + n_x])
        _cast_weight_once(wof_ref, wo_ref)
        res = (y_refs, gate_ref)
    else:
        q_ref, k4_ref, v4_ref, z_ref, kc_ref, vc_ref, w_ref, h_ref, kv_ref = refs
        x_refs, res, xo_ref, wo_ref = res_refs, None, None, None
    i = pl.program_id(0)
    _cast_weight_once(wf_ref, w_ref)
    lane = lax.broadcasted_iota(jnp.int32, (PROJ_SUB, LANES), 1)
    first = (lane % (ATTN_HEAD_DIM // 2)) < ATTN_HEAD_DIM // 4
    scale = ATTN_HEAD_DIM ** -0.5 * LOG2_E
    k_lo = ATTN_WIDTH
    v_lo = ATTN_WIDTH + ATTN_KV_WIDTH
    z_lo = ATTN_WIDTH + 2 * ATTN_KV_WIDTH

    def proj_steps(sub):
        rows = slice(sub * PROJ_SUB, (sub + 1) * PROJ_SUB)

        def rope(a):
            rot = jnp.where(first, pltpu.roll(a, LANES - ATTN_HEAD_DIM // 4, 1),
                            pltpu.roll(a, ATTN_HEAD_DIM // 4, 1))
            return a * cos_ref[rows, :] + rot * sin_ref[rows, :]

        def rope_wide(a):
            return jnp.concatenate(
                [rope(a[:, s * LANES:(s + 1) * LANES]) for s in range(MXU_N // LANES)], axis=1)

        def proj(lo):
            return _dot(h_ref[rows, :], w_ref[:, lo:lo + MXU_N])

        def q_chunk(c):
            cols = slice(c * MXU_N, (c + 1) * MXU_N)
            q_ref[rows, cols] = (rope_wide(proj(c * MXU_N)) * scale).astype(BF16)

        def keys():
            k = rope_wide(proj(k_lo))
            kv_ref[rows, :ATTN_KV_WIDTH] = k
            for hh in range(ATTN_KV_HEADS):
                k4_ref[rows, hh * MXU_N:(hh + 1) * MXU_N] = _rep4(k, hh).astype(BF16)

        def values():
            v = proj(v_lo)
            kv_ref[rows, ATTN_KV_WIDTH:] = v
            for hh in range(ATTN_KV_HEADS):
                v4_ref[rows, hh * MXU_N:(hh + 1) * MXU_N] = _rep4(v, hh).astype(BF16)

        def z_chunk(c):
            cols = slice(c * MXU_N, (c + 1) * MXU_N)
            z_ref[rows, cols] = _silu(proj(z_lo + c * MXU_N)).astype(BF16)

        n_chunks = ATTN_WIDTH // MXU_N
        return ([functools.partial(q_chunk, c) for c in range(n_chunks)] + [keys, values]
                + [functools.partial(z_chunk, c) for c in range(n_chunks)])

    _run_halves(t, x_refs, res, g_ref, sh_ref, sc_ref, xo_ref, wo_ref, h_ref, proj_steps)

    @pl.when(i < t.n_prompt)
    def _():
        for s in range(t.tm // SEQ):
            kt = kv_ref[s * SEQ:(s + 1) * SEQ, :ATTN_KV_WIDTH].T
            vt = kv_ref[s * SEQ:(s + 1) * SEQ, ATTN_KV_WIDTH:].T
            if n_alias:
                kc_ref[s] = kt
                vc_ref[s] = vt
            else:
                for l in range(kc_ref.shape[1]):
                    kc_ref[s, l] = kt if l == j else jnp.zeros_like(kt)
                    vc_ref[s, l] = vt if l == j else jnp.zeros_like(vt)


def _attn_in(mixed, x_parts, g, mod, w_all, j, cos_t, sin_t, caches):
    t = PROJ
    n_in = 2 * ATTN_WIDTH + 2 * ATTN_KV_WIDTH
    per = t.tm // SEQ
    rope_spec = pl.BlockSpec(
        (t.tm, LANES), lambda i: (jnp.where(i < t.n_prompt, 0, 1 + t.seq_tile(i)), 0))
    wide = jax.ShapeDtypeStruct((N_TOK, ATTN_WIDTH), BF16)
    n_attn = w_all.shape[0]
    cache = jax.ShapeDtypeStruct((BATCH, n_attn, ATTN_KV_WIDTH, SEQ), F32)
    if caches:
        cache_spec = pl.BlockSpec((per, None, ATTN_KV_WIDTH, SEQ),
                                  lambda i: (jnp.minimum(i, t.n_prompt - 1), j, 0, 0))
    else:
        cache_spec = pl.BlockSpec((per, n_attn, ATTN_KV_WIDTH, SEQ),
                                  lambda i: (jnp.minimum(i, t.n_prompt - 1), 0, 0, 0))
    n_x = len(x_parts)
    if mixed is None:
        n_y, res_specs, res_args, res_out, res_ospecs, res_scratch = 0, _part_specs(t, x_parts, D_MODEL), x_parts, (), (), []
    else:
        n_y = len(mixed.y_parts)
        res_specs, res_args = _residual_specs(t, mixed, x_parts), _residual_args(mixed, x_parts)
        res_out = (jax.ShapeDtypeStruct((N_TOK, D_MODEL), F32),)
        res_ospecs = (t.rows(D_MODEL),)
        res_scratch = [pltpu.VMEM(mixed.w_out.shape[1:], BF16)]
    n_front = len(res_specs) + 6
    return pl.pallas_call(
        functools.partial(_attn_in_kernel, j=j, n_y=n_y, n_x=n_x, n_alias=len(caches)),
        out_shape=res_out + (wide, wide, wide, wide, cache, cache),
        grid=(t.n_tiles,),
        in_specs=res_specs + [
            _const_spec((1, D_MODEL)), t.mod(0), t.mod(1), _layer_spec(w_all, j),
            rope_spec, rope_spec] + [pl.BlockSpec(memory_space=pl.ANY)] * len(caches),
        out_specs=res_ospecs + (t.rows(ATTN_WIDTH),) * 4 + (cache_spec,) * 2,
        scratch_shapes=res_scratch + [pltpu.VMEM((D_MODEL, n_in), BF16), pltpu.VMEM((t.tm, D_MODEL), BF16),
                                      pltpu.VMEM((t.tm, 2 * ATTN_KV_WIDTH), F32)],
        input_output_aliases={n_front + c: len(res_out) + 4 + c for c in range(len(caches))},
        compiler_params=_params("arbitrary", vmem=FUSED_VMEM_LIMIT),
        name="attn_in",
    )(*res_args, g, mod, mod, w_all, cos_t, sin_t, *caches)


def _stack_group_queries(q):
    qf = q.astype(F32)
    chunk = lax.broadcasted_iota(jnp.int32, qf.shape, 1) // ATTN_HEAD_DIM
    return jnp.concatenate(
        [jnp.where(chunk == g, qf, 0.0) for g in range(ATTN_GROUP)], axis=0).astype(BF16)


def _gather_group_outputs(o, rows):
    chunk = lax.broadcasted_iota(jnp.int32, (rows, MXU_N), 1) // ATTN_HEAD_DIM
    acc = jnp.zeros((rows, MXU_N), F32)
    for g in range(ATTN_GROUP):
        acc = acc + jnp.where(chunk == g, o[g * rows:(g + 1) * rows], 0.0)
    return acc


def _sink_column(sink_ref, h, rows):
    grp = lax.broadcasted_iota(jnp.int32, (ATTN_GROUP * rows, 1), 0) // rows
    col = jnp.zeros((ATTN_GROUP * rows, 1), F32)
    for g in range(ATTN_GROUP):
        col = jnp.where(grp == g, sink_ref[h * ATTN_GROUP + g] * LOG2_E, col)
    return col


def _chunk_rows(dtype):
    chunk = lax.broadcasted_iota(jnp.int32, (1, MXU_N), 1) // ATTN_HEAD_DIM
    return [(chunk == g).astype(F32).astype(dtype) for g in range(ATTN_GROUP)]


def _block_diag_rows(x4):
    return jnp.concatenate([x4 * m for m in _chunk_rows(x4.dtype)], axis=0)


def _ctx_attn_kernel(sink_ref, q_ref, k4_ref, v4_ref, z_ref, y_ref):
    chunk = lax.broadcasted_iota(jnp.int32, (SEQ, MXU_N), 1) // ATTN_HEAD_DIM
    for b, h in itertools.product(range(CTX_ATTN_SEQS), range(ATTN_KV_HEADS)):
        rows = slice(b * SEQ, (b + 1) * SEQ)
        cols = slice(h * MXU_N, (h + 1) * MXU_N)
        s = _dot_nt(q_ref[rows, cols], _block_diag_rows(k4_ref[rows, cols]))
        inv = jnp.zeros((SEQ, MXU_N), F32)
        probs = []
        for g in range(ATTN_GROUP):
            sg = s[:, g * SEQ:(g + 1) * SEQ]
            sk = sink_ref[h * ATTN_GROUP + g] * LOG2_E
            m = jnp.maximum(jnp.max(sg, axis=1, keepdims=True), sk)
            e = jnp.exp2(sg - m)
            den = jnp.sum(e, axis=1, keepdims=True) + jnp.exp2(sk - m)
            probs.append(e.astype(BF16))
            inv = jnp.where(chunk == g, 1.0 / den, inv)
        o = _dot(jnp.concatenate(probs, axis=1), _block_diag_rows(v4_ref[rows, cols]))
        y_ref[rows, cols] = (o * inv * z_ref[rows, cols].astype(F32)).astype(BF16)


def _ctx_attn(sink, q, k4, v4, z):
    spec = pl.BlockSpec((CTX_ATTN_SEQS * SEQ, ATTN_WIDTH), lambda b: (b, 0))
    return pl.pallas_call(
        _ctx_attn_kernel,
        out_shape=jax.ShapeDtypeStruct((N_PROMPT_TOK, ATTN_WIDTH), BF16),
        grid=(BATCH // CTX_ATTN_SEQS,),
        in_specs=[pl.BlockSpec(memory_space=pltpu.SMEM), spec, spec, spec, spec],
        out_specs=spec,
        compiler_params=_params("arbitrary"),
        name="ctx_attn",
    )(sink, q, k4, v4, z)


LAT_STEP = Q_SUB * Q_BLOCK
LAT_PER_SEQ = DEC_SEQ // LAT_STEP
LAT_BLOCKS = DEC_BATCH * ATTN_KV_HEADS * LAT_PER_SEQ
assert LAT_BLOCKS % 2 == 0


def _lat_block(blk):
    return (blk // (ATTN_KV_HEADS * LAT_PER_SEQ), (blk // LAT_PER_SEQ) % ATTN_KV_HEADS, blk % LAT_PER_SEQ)


def _band_rows(qb):
    return pl.ds(pl.multiple_of(qb * Q_BLOCK, Q_BLOCK), 3 * Q_BLOCK)


def _pad_sequence(dst, src):
    zeros = jnp.zeros((Q_BLOCK, MXU_N), BF16)
    dst[0:Q_BLOCK, :] = zeros
    dst[Q_BLOCK:Q_BLOCK + DEC_SEQ, :] = src[...]
    dst[Q_BLOCK + DEC_SEQ:, :] = zeros


def _lat_attn_kernel(sink_ref, q_ref, k4_ref, v4_ref, kc_ref, vc_ref, z_ref, y_ref,
                     kp, vp, k4c, v4c, sc_a, sl_a, sc_b, sl_b, e_scr):
    t = pl.program_id(0)
    _, _, n1 = _lat_block(jnp.minimum(t, LAT_BLOCKS - 1))
    _, h0, n0 = _lat_block(jnp.maximum(t - 1, 0))

    @pl.when(n1 == 0)
    def _():
        _pad_sequence(kp, k4_ref)
        k4c[...] = jnp.concatenate([kc_ref[...]] * ATTN_GROUP, axis=0).astype(BF16)

    @pl.when(n0 == 0)
    def _():
        _pad_sequence(vp, v4_ref)
        v4c[...] = jnp.concatenate([vc_ref[...]] * ATTN_GROUP, axis=0).T.astype(BF16)

    stages = functools.partial(_lat_stages, sink_ref, q_ref, z_ref, y_ref, kp, vp, k4c, v4c, e_scr, n1, h0, n0)
    even = t % 2 == 0
    pl.when(t == 0)(functools.partial(stages, sc_a, sl_a, sc_b, sl_b, finish_previous=False))
    pl.when(jnp.logical_and(even, jnp.logical_and(t > 0, t < LAT_BLOCKS)))(
        functools.partial(stages, sc_a, sl_a, sc_b, sl_b))
    pl.when(t == LAT_BLOCKS)(functools.partial(stages, sc_a, sl_a, sc_b, sl_b, score_next=False))
    pl.when(jnp.logical_not(even))(functools.partial(stages, sc_b, sl_b, sc_a, sl_a))


def _lat_stages(sink_ref, q_ref, z_ref, y_ref, kp, vp, k4c, v4c, e_scr, n1, h0, n0,
                sc_w, sl_w, sc_r, sl_r, score_next=True, finish_previous=True):
    n_blocks = DEC_SEQ // Q_BLOCK
    rows4 = ATTN_GROUP * Q_BLOCK
    r = lax.broadcasted_iota(jnp.int32, (rows4, Q_BLOCK), 0) % Q_BLOCK
    c = lax.broadcasted_iota(jnp.int32, (rows4, Q_BLOCK), 1)
    in_left = c >= r
    in_right = c <= r
    sk = _sink_column(sink_ref, h0, Q_BLOCK)

    def scores(sub):
        qs = _stack_group_queries(q_ref[sub * Q_BLOCK:(sub + 1) * Q_BLOCK, :])
        sc_w[sub] = _dot(qs, k4c[...])
        sl_w[sub] = _dot_nt(qs, kp[_band_rows(n1 * Q_SUB + sub), :])

    def softmax(sub):
        qb = n0 * Q_SUB + sub
        s_ctx = sc_r[sub]
        s_lat = sl_r[sub]
        blocks = [
            s_ctx,
            jnp.where(jnp.logical_and(in_left, qb > 0), s_lat[:, :Q_BLOCK], NEG_INF),
            s_lat[:, Q_BLOCK:2 * Q_BLOCK],
            jnp.where(jnp.logical_and(in_right, qb < n_blocks - 1), s_lat[:, 2 * Q_BLOCK:], NEG_INF),
        ]
        slabs = [s[:, c:c + LANES] for s in blocks for c in range(0, s.shape[1], LANES)]
        m = jnp.maximum(sk, jnp.max(functools.reduce(jnp.maximum, slabs), axis=1, keepdims=True))
        es = [jnp.exp2(s - m) for s in slabs]
        for c, e in enumerate(es):
            e_scr[sub, :, c * LANES:(c + 1) * LANES] = e.astype(BF16)
        return jnp.exp2(sk - m) + jnp.sum(functools.reduce(jnp.add, es), axis=1, keepdims=True)

    def finish(sub, den):
        qrows = slice(sub * Q_BLOCK, (sub + 1) * Q_BLOCK)
        o = (_dot(e_scr[sub, :, :PAST_LEN], v4c[...])
             + _dot(e_scr[sub, :, PAST_LEN:], vp[_band_rows(n0 * Q_SUB + sub), :])) * (1.0 / den)
        acc = _gather_group_outputs(o, Q_BLOCK)
        y_ref[qrows, :] = (acc * z_ref[qrows, :].astype(F32)).astype(BF16)

    for sub in range(Q_SUB):
        den = softmax(sub) if finish_previous else None
        if score_next:
            scores(sub)
        if finish_previous:
            finish(sub, den)


def _lat_attn(sink, q, k4, v4, kc, vc, j, z):
    padded = DEC_SEQ + 2 * Q_BLOCK
    rows4 = ATTN_GROUP * Q_BLOCK
    row0 = N_PROMPT_TOK // LAT_STEP
    seq0 = N_PROMPT_TOK // DEC_SEQ
    stage1 = lambda t: _lat_block(jnp.minimum(t, LAT_BLOCKS - 1))
    stage2 = lambda t: _lat_block(jnp.maximum(t - 1, 0))

    def spec(shape, stage, index):
        return pl.BlockSpec(shape, lambda t: index(*stage(t)))

    tile = (LAT_STEP, MXU_N)
    seq = (DEC_SEQ, MXU_N)
    ctx = (None, None, None, ATTN_HEAD_DIM, PAST_LEN)
    return pl.pallas_call(
        _lat_attn_kernel,
        out_shape=jax.ShapeDtypeStruct((N_SAMPLE_TOK, ATTN_WIDTH), BF16),
        grid=(LAT_BLOCKS + 1,),
        in_specs=[pl.BlockSpec(memory_space=pltpu.SMEM),
                  spec(tile, stage1, lambda b, h, n: (row0 + b * LAT_PER_SEQ + n, h)),
                  spec(seq, stage1, lambda b, h, n: (seq0 + b, h)),
                  spec(seq, stage2, lambda b, h, n: (seq0 + b, h)),
                  spec(ctx, stage1, lambda b, h, n: (b, j, h, 0, 0)),
                  spec(ctx, stage2, lambda b, h, n: (b, j, h, 0, 0)),
                  spec(tile, stage2, lambda b, h, n: (row0 + b * LAT_PER_SEQ + n, h))],
        out_specs=spec(tile, stage2, lambda b, h, n: (b * LAT_PER_SEQ + n, h)),
        scratch_shapes=[pltpu.VMEM((padded, MXU_N), BF16), pltpu.VMEM((padded, MXU_N), BF16),
                        pltpu.VMEM((MXU_N, PAST_LEN), BF16), pltpu.VMEM((PAST_LEN, MXU_N), BF16),
                        pltpu.VMEM((Q_SUB, rows4, PAST_LEN), F32),
                        pltpu.VMEM((Q_SUB, rows4, 3 * Q_BLOCK), F32),
                        pltpu.VMEM((Q_SUB, rows4, PAST_LEN), F32),
                        pltpu.VMEM((Q_SUB, rows4, 3 * Q_BLOCK), F32),
                        pltpu.VMEM((Q_SUB, rows4, PAST_LEN + 3 * Q_BLOCK), BF16)],
        compiler_params=_params("arbitrary", vmem=VMEM_LIMIT),
        name="lat_attn",
    )(sink, q, k4, v4, kc, vc, z)


def _out_final_kernel(yp_ref, ys_ref, wf_ref, x_ref, gate_ref, fg_ref, op_ref, os_ref, w_ref):
    t = FINAL
    i = pl.program_id(0)
    _cast_weight_once(wf_ref, w_ref)

    def body(y_ref, o_ref):
        for lo in range(0, t.tm, PROJ_SUB):
            rows = slice(lo, lo + PROJ_SUB)
            o_ref[rows, :] = _rms(x_ref[rows, :] + gate_ref[...] * _dot(y_ref[rows, :], w_ref[...]), fg_ref[...])

    pl.when(i < t.n_prompt)(functools.partial(body, yp_ref, op_ref))
    pl.when(i >= t.n_prompt)(functools.partial(body, ys_ref, os_ref))


def _out_proj_final(mixed, x, final_g):
    t = FINAL
    return pl.pallas_call(
        _out_final_kernel,
        out_shape=(jax.ShapeDtypeStruct((N_PROMPT_TOK, D_MODEL), F32),
                   jax.ShapeDtypeStruct((N_SAMPLE_TOK, D_MODEL), F32)),
        grid=(t.n_tiles,),
        in_specs=_residual_specs(t, mixed, (x,)) + [_const_spec((1, D_MODEL))],
        out_specs=(t.prompt_rows(D_MODEL), t.sample_rows(D_MODEL)),
        scratch_shapes=[pltpu.VMEM(mixed.w_out.shape[1:], BF16)],
        compiler_params=_params("arbitrary", vmem=VMEM_LIMIT),
        name="out_proj_final",
    )(*_residual_args(mixed, (x,)), final_g)


def _pool_in_kernel(*refs, n_y, n_x):
    t = PROJ
    n_res = n_y + n_x + 2
    res_refs, refs = refs[:n_res], refs[n_res:]
    g_ref, sh_ref, sc_ref, wf_ref, wn_ref, xo_ref, u_ref, z_ref, wnb_ref, wo_ref, w_ref, h_ref = refs
    y_refs, wof_ref, x_refs, gate_ref = (res_refs[:n_y], res_refs[n_y], res_refs[n_y + 1:n_y + 1 + n_x],
                                         res_refs[n_y + 1 + n_x])
    _cast_weight_once(wof_ref, wo_ref)
    _cast_weight_once(wf_ref, w_ref)
    wnb_ref[...] = wn_ref[...].astype(BF16)

    def proj_steps(sub):
        rows = slice(sub * PROJ_SUB, (sub + 1) * PROJ_SUB)

        def chunk(c):
            def step():
                cols = slice(c * MXU_N, (c + 1) * MXU_N)
                u_ref[rows, cols] = _dot(h_ref[rows, :], w_ref[:, cols])
            return step

        def gate_chunk(c):
            def step():
                a = _dot(h_ref[rows, :], w_ref[:, D_MODEL + c * MXU_N:D_MODEL + (c + 1) * MXU_N])
                z_ref[rows, c * MXU_N:(c + 1) * MXU_N] = _silu(a).astype(BF16)
            return step

        n = D_MODEL // MXU_N
        return [chunk(c) for c in range(n)] + [gate_chunk(c) for c in range(n)]

    _run_halves(t, x_refs, (y_refs, gate_ref), g_ref, sh_ref, sc_ref, xo_ref, wo_ref, h_ref, proj_steps)


def _pool_in(mixed, x_parts, g, mod, w_all, j, w_next_all, j_next):
    t = PROJ
    slab = jax.ShapeDtypeStruct((N_TOK, D_MODEL), F32)
    k_next, n_next = w_next_all.shape[1:]
    return pl.pallas_call(
        functools.partial(_pool_in_kernel, n_y=len(mixed.y_parts), n_x=len(x_parts)),
        out_shape=(slab, slab, jax.ShapeDtypeStruct((N_TOK, D_MODEL), BF16),
                   jax.ShapeDtypeStruct((k_next, n_next), BF16)),
        grid=(t.n_tiles,),
        in_specs=_residual_specs(t, mixed, x_parts) + [
            _const_spec((1, D_MODEL)), t.mod(0), t.mod(1), _layer_spec(w_all, j),
            pl.BlockSpec((None, k_next // t.n_tiles, n_next), lambda i: (j_next, i, 0))],
        out_specs=(t.rows(D_MODEL),) * 3 + (pl.BlockSpec((k_next // t.n_tiles, n_next), lambda i: (i, 0)),),
        scratch_shapes=[pltpu.VMEM(mixed.w_out.shape[1:], BF16), pltpu.VMEM((D_MODEL, 2 * D_MODEL), BF16),
                        pltpu.VMEM((t.tm, D_MODEL), BF16)],
        compiler_params=_params("arbitrary", vmem=FUSED_VMEM_LIMIT),
        name="pool_in",
    )(*_residual_args(mixed, x_parts), g, mod, mod, w_all, w_next_all)


def _split_bf16(a):
    hi = a.astype(BF16)
    return hi, (a - hi.astype(F32)).astype(BF16)


def _band_ones(shape, lo, hi):
    d = lax.broadcasted_iota(jnp.int32, shape, 1) - lax.broadcasted_iota(jnp.int32, shape, 0)
    return jnp.logical_and(d >= lo, d <= hi).astype(F32).astype(BF16)


def _pool_ret_kernel(u_ref, up_ref, un_ref, z_ref, wgf_ref, ps_ref, wof_ref, x_ref, gate_ref,
                     g_ref, sh_ref, sc_ref, w_ref,
                     xo_ref, q_ref, kt_ref, v_ref, zr_ref,
                     y_ref, wg_ref, wo_ref, band_ref, wkt_ref, h_ref):
    t = PROJ
    i = pl.program_id(0)
    _cast_weight_once(wgf_ref, wg_ref)
    _cast_weight_once(wof_ref, wo_ref)

    @pl.when(i == 0)
    def _():
        for g, w in enumerate(POOL_WINDOWS):
            band_ref[g] = _band_ones((POOL_SUB, POOL_SUB), -(w // 2), w - 1 - w // 2)
        for r in range(0, RET_QK_WIDTH, MXU_N):
            wk = w_ref[:, RET_QK_WIDTH + r:RET_QK_WIDTH + r + MXU_N]
            wkt_ref[r:r + MXU_N, :] = wk.astype(F32).T.astype(BF16)

    is_dec = i >= t.n_prompt
    st = t.seq_tile(i)
    seq_len = jnp.where(is_dec, DEC_SEQ, SEQ)
    h = POOL_HALO
    n_sub = t.tm // POOL_SUB
    def pool_steps(sub):
        rows = slice(sub * POOL_SUB, (sub + 1) * POOL_SUB)
        if sub == 0:
            before = jnp.where(jnp.logical_and(is_dec, st != 0), up_ref[...], 0.0)
        else:
            before = jnp.where(is_dec, u_ref[sub * POOL_SUB - h:sub * POOL_SUB, :], 0.0)
        if sub == n_sub - 1:
            after = jnp.where(jnp.logical_and(is_dec, st != t.per_dec_seq - 1), un_ref[...], 0.0)
        else:
            after = jnp.where(is_dec, u_ref[(sub + 1) * POOL_SUB:(sub + 1) * POOL_SUB + h, :], 0.0)
        pos0 = jnp.where(is_dec, st * t.tm + sub * POOL_SUB, 0)
        return _pool_core_steps(u_ref.at[rows], before, after, pos0, seq_len, z_ref.at[rows], ps_ref,
                                wg_ref, band_ref, y_ref.at[rows])

    def mix_steps(sub):
        rows = slice(sub * POOL_SUB, (sub + 1) * POOL_SUB)

        def residual():
            xo_ref[rows, :] = x_ref[rows, :] + gate_ref[...] * _dot(y_ref[rows, :], wo_ref[...])

        def normalise():
            h_ref[rows, :] = _norm_mod(xo_ref[rows, :], g_ref, sh_ref, sc_ref)

        return pool_steps(sub) + [residual, normalise]

    for step in mix_steps(0):
        step()
    for sub in range(n_sub):
        rows = slice(sub * POOL_SUB, (sub + 1) * POOL_SUB)
        proj = _ret_in_steps(h_ref.at[rows], w_ref, wkt_ref, q_ref.at[rows],
                             kt_ref.at[pl.ds(sub * POOL_SUB // RET_CHUNK, POOL_SUB // RET_CHUNK)],
                             v_ref.at[rows], zr_ref.at[rows])
        _run_interleaved(proj, mix_steps(sub + 1) if sub + 1 < n_sub else [])


def _pool_core_steps(u_ref, before, after, pos0, seq_len, z_ref, ps_ref, wg_ref, band_ref, y_ref):
    n = u_ref.shape[0]
    h = POOL_HALO
    halo_hi, halo_lo = _split_bf16(jnp.concatenate([before, after], axis=0))
    pos = pos0 + lax.broadcasted_iota(jnp.int32, (n, 1), 0)
    rr = lax.broadcasted_iota(jnp.int32, (2 * h, 2 * h), 0)
    cc = lax.broadcasted_iota(jnp.int32, (2 * h, 2 * h), 1)

    def pooled(g, w):
        def step():
            left = w // 2
            right = w - 1 - left
            cols = slice(g * POOL_GROUP_DIM, (g + 1) * POOL_GROUP_DIM)
            u = u_ref[:, cols]
            s = _dot(band_ref[g], u.astype(BF16))
            top = jnp.logical_and(jnp.logical_and(rr < h, cc < h), cc - h >= rr - left)
            bot = jnp.logical_and(jnp.logical_and(rr >= h, cc >= h), cc - h <= rr - 2 * h + right)
            edge = jnp.logical_or(top, bot).astype(F32).astype(BF16)
            se = _dot(edge, halo_hi[:, cols]) + _dot(edge, halo_lo[:, cols])
            s = jnp.concatenate([s[:h] + se[:h], s[h:n - h], s[n - h:] + se[h:]], axis=0)
            cnt = (jnp.minimum(pos + right + 1, seq_len) - jnp.maximum(pos - left, 0)).astype(F32)
            y_ref[:, cols] = (s / cnt - u).astype(BF16)
        return step

    def mixed(g):
        def step():
            cols = slice(g * POOL_GROUP_DIM, (g + 1) * POOL_GROUP_DIM)
            yg = _dot(y_ref[:, cols], wg_ref[g]) * ps_ref[:, cols] * z_ref[:, cols].astype(F32)
            y_ref[:, cols] = yg.astype(BF16)
        return step

    return ([pooled(g, w) for g, w in enumerate(POOL_WINDOWS)]
            + [mixed(g) for g in range(len(POOL_WINDOWS))])


def _pool_ret(u, z, wg_all, ps, wo_all, j, x, mod_pool, g, mod, w_ret):
    t = PROJ
    per = t.tm // POOL_HALO
    n_halo = N_TOK // POOL_HALO
    n_in = 2 * RET_QK_WIDTH + 2 * RET_V_WIDTH
    kt_per = t.tm // RET_CHUNK
    return pl.pallas_call(
        _pool_ret_kernel,
        out_shape=(jax.ShapeDtypeStruct((N_TOK, D_MODEL), F32),
                   jax.ShapeDtypeStruct((N_TOK, RET_QK_WIDTH), BF16),
                   jax.ShapeDtypeStruct((N_TOK // RET_CHUNK, RET_QK_WIDTH, RET_CHUNK), BF16),
                   jax.ShapeDtypeStruct((N_TOK, RET_V_WIDTH), BF16),
                   jax.ShapeDtypeStruct((N_TOK, RET_V_WIDTH), BF16)),
        grid=(t.n_tiles,),
        in_specs=[
            t.rows(D_MODEL),
            pl.BlockSpec((POOL_HALO, D_MODEL), lambda i: (jnp.maximum(i * per - 1, 0), 0)),
            pl.BlockSpec((POOL_HALO, D_MODEL), lambda i: (jnp.minimum((i + 1) * per, n_halo - 1), 0)),
            t.rows(D_MODEL),
            _layer_spec(wg_all, j),
            _const_spec((1, D_MODEL)),
            _layer_spec(wo_all, j),
            t.rows(D_MODEL),
            t.mod(2),
            _const_spec((1, D_MODEL)), t.mod(0), t.mod(1),
            pl.BlockSpec((D_MODEL, n_in), lambda i: (0, 0), pipeline_mode=pl.Buffered(1)),
        ],
        out_specs=(t.rows(D_MODEL), t.rows(RET_QK_WIDTH),
                   pl.BlockSpec((kt_per, RET_QK_WIDTH, RET_CHUNK), lambda i: (i, 0, 0)),
                   t.rows(RET_V_WIDTH), t.rows(RET_V_WIDTH)),
        scratch_shapes=[pltpu.VMEM((t.tm, D_MODEL), BF16),
                        pltpu.VMEM((len(POOL_WINDOWS), POOL_GROUP_DIM, POOL_GROUP_DIM), BF16),
                        pltpu.VMEM((D_MODEL, D_MODEL), BF16),
                        pltpu.VMEM((len(POOL_WINDOWS), POOL_SUB, POOL_SUB), BF16),
                        pltpu.VMEM((RET_QK_WIDTH, D_MODEL), BF16),
                        pltpu.VMEM((t.tm, D_MODEL), BF16)],
        compiler_params=_params("arbitrary", vmem=FUSED_VMEM_LIMIT),
        name="pool_ret",
    )(u, u, u, z, wg_all, ps, wo_all, x, mod_pool, g, mod, mod, w_ret)


def _ret_in_steps(h_ref, w_ref, wkt_ref, q_ref, kt_ref, v_ref, z_ref):
    n_rows = h_ref.shape[0]

    def chunk(ref, lo, c, post=lambda a: a):
        def step():
            cols = slice(c * MXU_N, (c + 1) * MXU_N)
            ref[:, cols] = post(_dot(h_ref[...], w_ref[:, lo + c * MXU_N:lo + (c + 1) * MXU_N])).astype(BF16)
        return step

    def key_chunk(c):
        def step():
            rows = slice(c * MXU_N, (c + 1) * MXU_N)
            kt = (_dot_nt(wkt_ref[rows, :], h_ref[...]) * RET_KEY_DIM ** -0.5).astype(BF16)
            for cc in range(n_rows // RET_CHUNK):
                kt_ref[cc, rows, :] = kt[:, cc * RET_CHUNK:(cc + 1) * RET_CHUNK]
        return step

    steps = [chunk(q_ref, 0, c) for c in range(RET_QK_WIDTH // MXU_N)]
    steps += [key_chunk(c) for c in range(RET_QK_WIDTH // MXU_N)]
    steps += [chunk(v_ref, 2 * RET_QK_WIDTH, c) for c in range(RET_V_WIDTH // MXU_N)]
    steps += [chunk(z_ref, 2 * RET_QK_WIDTH + RET_V_WIDTH, c, _silu) for c in range(RET_V_WIDTH // MXU_N)]
    return steps


def _run_interleaved(main, side):
    done = 0
    for k, step in enumerate(main):
        step()
        due = (k + 1) * len(side) // len(main)
        for s in side[done:due]:
            s()
        done = due


def _pos(shape, axis):
    return lax.broadcasted_iota(jnp.int32, shape, axis).astype(F32)


def _ret_tables_kernel(lgf_ref, lgb_ref, decay_ref, row_ref, col_ref, cdec_ref):
    h = pl.program_id(0)
    lg_f = lgf_ref[h]
    lg_b = lgb_ref[h]
    c = RET_CHUNK
    diff = _pos((c, c), 0) - _pos((c, c), 1)
    fwd = jnp.where(diff >= 0, jnp.exp(jnp.maximum(diff, 0.0) * lg_f), 0.0)
    bwd = jnp.where(diff <= 0, jnp.exp(jnp.maximum(-diff, 0.0) * lg_b), 0.0)
    decay_ref[...] = fwd + bwd
    j = _pos((RET_TAB_ROWS, c), 1)
    row_ref[0] = jnp.exp((c - 1.0 - j) * lg_f)
    row_ref[1] = jnp.exp(j * lg_b)
    i = _pos((c, LANES), 0)
    col_ref[0] = jnp.exp((i + 1.0) * lg_f)
    col_ref[1] = jnp.exp((c - i) * lg_b)
    full = jnp.full((RET_TAB_ROWS, RET_VAL_DIM), float(c), F32)
    cdec_ref[0] = jnp.exp(full * lg_f)
    cdec_ref[1] = jnp.exp(full * lg_b)


def _ret_tables(lg_f, lg_b):
    smem = pl.BlockSpec(memory_space=pltpu.SMEM)
    c = RET_CHUNK
    shapes = ((c, c), (2, RET_TAB_ROWS, c), (2, c, LANES), (2, RET_TAB_ROWS, RET_VAL_DIM))
    return pl.pallas_call(
        _ret_tables_kernel,
        out_shape=tuple(jax.ShapeDtypeStruct((RET_HEADS,) + s, F32) for s in shapes),
        grid=(RET_HEADS,),
        in_specs=[smem, smem],
        out_specs=tuple(pl.BlockSpec((None,) + s, lambda h, n=len(s): (h,) + (0,) * n) for s in shapes),
        compiler_params=_params("arbitrary"),
        name="ret_tables",
    )(lg_f, lg_b)


def _group_norm_gate(o, gn, z):
    mu = jnp.mean(o, axis=-1, keepdims=True)
    var = jnp.mean(jnp.square(o - mu), axis=-1, keepdims=True)
    on = (o - mu) * lax.rsqrt(var + EPS)
    return (on * gn * z.astype(F32)).astype(BF16)


def _ret_ctx_kernel(q_ref, kt_ref, v_ref, z_ref, gn_ref, decay_ref, row_ref, y_ref, sf_ref, sb_ref):
    for s in range(RET_CTX_SEQS):
        rows = slice(s * SEQ, (s + 1) * SEQ)
        for h in range(RET_HEADS):
            kc = slice(h * RET_KEY_DIM, (h + 1) * RET_KEY_DIM)
            vc = slice(h * RET_VAL_DIM, (h + 1) * RET_VAL_DIM)
            kt = kt_ref[s, kc, :]
            v = v_ref[rows, vc]
            att = (_dot(q_ref[rows, kc], kt) * decay_ref[h]).astype(BF16)
            y_ref[rows, vc] = _group_norm_gate(_dot(att, v), gn_ref[:, vc], z_ref[rows, vc])
            ktf = kt.astype(F32)
            sf_ref[s, h] = _dot((ktf * row_ref[h, 0, 0:1, :]).astype(BF16), v)
            sb_ref[s, h] = _dot((ktf * row_ref[h, 1, 0:1, :]).astype(BF16), v)


def _ret_ctx(q, kt, v, z, gn, decay, row):
    c = RET_CHUNK
    n = RET_CTX_SEQS
    wide = lambda w: pl.BlockSpec((n * SEQ, w), lambda b: (b, 0))
    st_spec = pl.BlockSpec((n, RET_HEADS, RET_KEY_DIM, RET_VAL_DIM), lambda b: (b, 0, 0, 0))
    st_shape = jax.ShapeDtypeStruct((BATCH, RET_HEADS, RET_KEY_DIM, RET_VAL_DIM), F32)
    return pl.pallas_call(
        _ret_ctx_kernel,
        out_shape=(jax.ShapeDtypeStruct((N_PROMPT_TOK, RET_V_WIDTH), BF16), st_shape, st_shape),
        grid=(BATCH // n,),
        in_specs=[wide(RET_QK_WIDTH), pl.BlockSpec((n, RET_QK_WIDTH, c), lambda b: (b, 0, 0)),
                  wide(RET_V_WIDTH), wide(RET_V_WIDTH), _const_spec((1, RET_V_WIDTH)),
                  _const_spec((RET_HEADS, c, c)), _const_spec((RET_HEADS, 2, RET_TAB_ROWS, c))],
        out_specs=(wide(RET_V_WIDTH), st_spec, st_spec),
        compiler_params=_params("arbitrary", vmem=VMEM_LIMIT),
        name="ret_ctx",
    )(q, kt, v, z, gn, decay, row)


def _ret_lat_kernel(q_ref, kt_ref, v_ref, z_ref, gn_ref, decay_ref, row_ref, col_ref, cdec_ref,
                    s0f_ref, s0b_ref, y_ref, sf_all, sb_all, sf_acc, sb_acc):
    c = RET_CHUNK
    n_chunks = DEC_SEQ // c
    rows_of = lambda ci: pl.ds(pl.multiple_of(ci * c, c), c)

    sf_acc[...] = s0f_ref[...]
    sb_acc[...] = s0b_ref[...]

    def scan_step(i, carry):
        cf = i
        cb = n_chunks - 1 - i
        sf_all[cf] = sf_acc[...].astype(BF16)
        sb_all[cb] = sb_acc[...].astype(BF16)
        uf = _dot((kt_ref[cf].astype(F32) * row_ref[0, 0:1, :]).astype(BF16), v_ref[rows_of(cf), :])
        ub = _dot((kt_ref[cb].astype(F32) * row_ref[1, 0:1, :]).astype(BF16), v_ref[rows_of(cb), :])
        sf_acc[...] = sf_acc[...] * cdec_ref[0, 0:1, :] + uf
        sb_acc[...] = sb_acc[...] * cdec_ref[1, 0:1, :] + ub
        return carry

    lax.fori_loop(0, n_chunks, scan_step, 0, unroll=4)

    def out_step(ci, carry):
        rows = rows_of(ci)
        q = q_ref[rows, :]
        qf = q.astype(F32)
        qdec_f = jnp.concatenate([col_ref[0]] * (RET_KEY_DIM // LANES), axis=1)
        qdec_b = jnp.concatenate([col_ref[1]] * (RET_KEY_DIM // LANES), axis=1)
        att = (_dot(q, kt_ref[ci]) * decay_ref[...]).astype(BF16)
        o = (_dot(att, v_ref[rows, :])
             + _dot((qf * qdec_f).astype(BF16), sf_all[ci])
             + _dot((qf * qdec_b).astype(BF16), sb_all[ci]))
        y_ref[rows, :] = _group_norm_gate(o, gn_ref[...], z_ref[rows, :])
        return carry

    lax.fori_loop(0, n_chunks, out_step, 0, unroll=8)


def _ret_lat(q, kt, v, z, gn, decay, row, col, cdec, s0f, s0b):
    c = RET_CHUNK
    n_chunks = DEC_SEQ // c
    row0 = N_PROMPT_TOK // DEC_SEQ
    qk_spec = pl.BlockSpec((DEC_SEQ, RET_KEY_DIM), lambda b, h: (row0 + b, h))
    v_spec = pl.BlockSpec((DEC_SEQ, RET_VAL_DIM), lambda b, h: (row0 + b, h))
    st_spec = pl.BlockSpec((None, None, RET_KEY_DIM, RET_VAL_DIM), lambda b, h: (b, h, 0, 0))
    tab = lambda *s: pl.BlockSpec((None,) + s, lambda b, h: (h,) + (0,) * len(s))
    states = pltpu.VMEM((n_chunks, RET_KEY_DIM, RET_VAL_DIM), BF16)
    acc = pltpu.VMEM((RET_KEY_DIM, RET_VAL_DIM), F32)
    return pl.pallas_call(
        _ret_lat_kernel,
        out_shape=jax.ShapeDtypeStruct((N_SAMPLE_TOK, RET_V_WIDTH), BF16),
        grid=(DEC_BATCH, RET_HEADS),
        in_specs=[qk_spec,
                  pl.BlockSpec((n_chunks, RET_KEY_DIM, c), lambda b, h: (row0 + b, h, 0)),
                  v_spec, v_spec, pl.BlockSpec((1, RET_VAL_DIM), lambda b, h: (0, h)),
                  tab(c, c), tab(2, RET_TAB_ROWS, c), tab(2, c, LANES), tab(2, RET_TAB_ROWS, RET_VAL_DIM),
                  st_spec, st_spec],
        out_specs=pl.BlockSpec((DEC_SEQ, RET_VAL_DIM), lambda b, h: (b, h)),
        scratch_shapes=[states, states, acc, acc],
        compiler_params=_params("arbitrary", "arbitrary", vmem=VMEM_LIMIT),
        name="ret_lat",
    )(q, kt, v, z, gn, decay, row, col, cdec, s0f, s0b)


def _rope_tables(tm):
    n_rows = DEC_SEQ // GRID_W
    rows = jnp.repeat(jnp.arange(n_rows), GRID_W).astype(F32)
    cols = jnp.tile(jnp.arange(GRID_W), n_rows).astype(F32)
    half = ATTN_HEAD_DIM // 4
    inv = ROPE_BASE ** (-jnp.arange(half, dtype=F32) / half)
    ang_r = rows[:, None] * inv[None, :]
    ang_c = cols[:, None] * inv[None, :]
    cos = jnp.concatenate([jnp.cos(ang_r), jnp.cos(ang_r), jnp.cos(ang_c), jnp.cos(ang_c)], axis=-1)
    sin = jnp.concatenate([-jnp.sin(ang_r), jnp.sin(ang_r), -jnp.sin(ang_c), jnp.sin(ang_c)], axis=-1)
    cos = jnp.concatenate([jnp.ones((tm, ATTN_HEAD_DIM), F32), cos], axis=0)
    sin = jnp.concatenate([jnp.zeros((tm, ATTN_HEAD_DIM), F32), sin], axis=0)
    return jnp.tile(cos, (1, 2)), jnp.tile(sin, (1, 2))


def kernel(x_prompt, x_sample, cache_k, cache_v, state_fwd, state_bwd, c, c_ctx, norm_g, ada_w, ada_b, attn_w_in, attn_w_out, attn_sink, pool_w_in, pool_w_grp, pool_scale, pool_w_out, ret_w_in, ret_decay_fwd, ret_decay_bwd, ret_gn_g, ret_w_out, final_g):
    x_parts = (x_prompt.reshape(N_PROMPT_TOK, D_MODEL), x_sample.reshape(N_SAMPLE_TOK, D_MODEL))
    cond = jnp.concatenate([c_ctx[None, :], c,
                            jnp.zeros((N_COND - 1 - DEC_BATCH, D_MODEL), F32)], axis=0)
    mods = _ada_table(cond.T, ada_w, ada_b).reshape(DEPTH, N_COND, 1, 3 * D_MODEL)
    cos_t, sin_t = _rope_tables(PROJ.tm)

    to_kernel = lambda a: jnp.transpose(a, (0, 1, 3, 4, 2))
    from_kernel = lambda a: jnp.transpose(
        a.reshape(a.shape[0], a.shape[1], ATTN_KV_HEADS, ATTN_HEAD_DIM, a.shape[3]), (0, 1, 4, 2, 3))
    ctx_k, ctx_v = to_kernel(cache_k), to_kernel(cache_v)

    assert DEPTH % N_MIXERS == 1, "the layer stack must end on an attention layer"
    caches = ()
    new_sf = new_sb = None
    mixed = None
    for i in range(DEPTH):
        kind, j = i % N_MIXERS, i // N_MIXERS
        g = norm_g[i].reshape(1, D_MODEL)
        mod = mods[i]
        if kind == 0:
            outs = _attn_in(mixed, x_parts, g, mod, attn_w_in, j, cos_t, sin_t, tuple(caches))
            if mixed is not None:
                x_parts, outs = (outs[0],), outs[1:]
            q, k4, v4, z, *caches = outs
            y_parts = (_ctx_attn(attn_sink[j], q, k4, v4, z),
                       _lat_attn(attn_sink[j], q, k4, v4, ctx_k, ctx_v, j, z))
            mixed = Mixed(y_parts, attn_w_out, j, mod)
        elif kind == 1:
            x, u, z, w_ret = _pool_in(mixed, x_parts, g, mod, pool_w_in, j, ret_w_in, j)
            x_parts, mixed, pooled = (x,), None, (u, z, j, mod, w_ret)
        else:
            u, z, jp, mod_pool, w_ret = pooled
            lg_f = jax.nn.log_sigmoid(ret_decay_fwd[j].astype(F32))
            lg_b = jax.nn.log_sigmoid(ret_decay_bwd[j].astype(F32))
            gn = ret_gn_g[j].reshape(1, RET_V_WIDTH)
            x, q, kt, v, z = _pool_ret(u, z, pool_w_grp, pool_scale[jp].reshape(1, D_MODEL), pool_w_out, jp,
                                       x_parts[0], mod_pool, g, mod, w_ret)
            x_parts = (x,)
            decay, row, col, cdec = _ret_tables(lg_f, lg_b)
            y_ctx, new_sf, new_sb = _ret_ctx(q, kt, v, z, gn, decay, row)
            y_parts = (y_ctx, _ret_lat(q, kt, v, z, gn, decay, row, col, cdec,
                                       state_fwd[:, j], state_bwd[:, j]))
            mixed = Mixed(y_parts, ret_w_out, j, mod)
    y_prompt, y_sample = _out_proj_final(mixed, x_parts[0], final_g.reshape(1, D_MODEL))
    new_k, new_v = caches
    return (y_prompt.reshape(BATCH, SEQ, D_MODEL), y_sample.reshape(DEC_BATCH, DEC_SEQ, D_MODEL),
            from_kernel(new_k), from_kernel(new_v), new_sf[:, None], new_sb[:, None])
```

```python
import functools
import itertools
from typing import NamedTuple

import jax
import jax.numpy as jnp
from jax import lax
from jax.experimental import pallas as pl
from jax.experimental.pallas import tpu as pltpu

F32 = jnp.float32
BF16 = jnp.bfloat16

D_MODEL = 1024
BATCH = 16
SEQ = 256
DEPTH = 4
DEC_BATCH = 2
DEC_SEQ = 2048
PAST_LEN = 512
GRID_W = 64
N_MIXERS = 3
ATTN_HEADS = 16
ATTN_KV_HEADS = 4
ATTN_HEAD_DIM = 64
ATTN_GROUP = 4
ATTN_WIDTH = 1024
ATTN_KV_WIDTH = 256
WINDOW = 128
ROPE_BASE = 10000.0
POOL_WINDOWS = (2, 4, 8, 16)
POOL_GROUP_DIM = 256
RET_HEADS = 4
RET_KEY_DIM = 256
RET_VAL_DIM = 512
RET_QK_WIDTH = 1024
RET_V_WIDTH = 2048
EPS = 1e-6
NEG_INF = -1e30
LOG2_E = 1.4426950408889634

N_PROMPT_TOK = BATCH * SEQ
N_SAMPLE_TOK = DEC_BATCH * DEC_SEQ
N_TOK = N_PROMPT_TOK + N_SAMPLE_TOK
N_COND = 8
LANES = 128
MXU_N = 256
Q_BLOCK = 128
Q_SUB = 4
RET_CHUNK = 256
RET_TAB_ROWS = 8
RET_CTX_SEQS = 2
CTX_ATTN_SEQS = 2
POOL_HALO = 8
POOL_SUB = SEQ
PROJ_SUB = 256
VMEM_LIMIT = 48 * 1024 * 1024
FUSED_VMEM_LIMIT = 58 * 1024 * 1024


class Tiling(NamedTuple):
    tm: int

    @property
    def n_tiles(self):
        return N_TOK // self.tm

    @property
    def n_prompt(self):
        return N_PROMPT_TOK // self.tm

    @property
    def per_dec_seq(self):
        return DEC_SEQ // self.tm

    def cond(self, i):
        return jnp.where(i < self.n_prompt, 0, 1 + (i - self.n_prompt) // self.per_dec_seq)

    def seq_tile(self, i):
        return jnp.where(i < self.n_prompt, 0, (i - self.n_prompt) % self.per_dec_seq)

    def rows(self, width):
        return pl.BlockSpec((self.tm, width), lambda i: (i, 0))

    def prompt_rows(self, width):
        return pl.BlockSpec((self.tm, width), lambda i: (jnp.minimum(i, self.n_prompt - 1), 0))

    def sample_rows(self, width):
        return pl.BlockSpec((self.tm, width), lambda i: (jnp.maximum(i - self.n_prompt, 0), 0))

    def mod(self, part):
        return pl.BlockSpec((None, 1, D_MODEL), lambda i: (self.cond(i), 0, part))


PROJ = Tiling(512)
FINAL = Tiling(1024)


def _silu(z):
    hz = 0.5 * z
    return hz + hz * jnp.tanh(hz)


def _dot(a, b):
    return jnp.dot(a, b, preferred_element_type=F32)


def _dot_nt(a, b):
    return lax.dot_general(a, b, (((1,), (1,)), ((), ())), preferred_element_type=F32)


def _params(*sem, vmem=None):
    return pltpu.CompilerParams(dimension_semantics=sem, vmem_limit_bytes=vmem)


def _const_spec(shape):
    nd = len(shape)
    return pl.BlockSpec(shape, lambda *_: (0,) * nd, pipeline_mode=pl.Buffered(1))


def _part_specs(t, parts, width):
    if len(parts) == 1:
        return [t.rows(width)]
    return [t.prompt_rows(width), t.sample_rows(width)]


def _read_parts(t, refs, rows=slice(None)):
    if len(refs) == 1:
        return refs[0][rows, :]
    return jnp.where(pl.program_id(0) < t.n_prompt, refs[0][rows, :], refs[1][rows, :])


def _ada_kernel(cond_ref, w_ref, b_ref, o_ref):
    s = _silu(cond_ref[...])
    w = w_ref[...]
    rows = [jnp.sum(s[:, c:c + 1] * w, axis=0, keepdims=True) + b_ref[...] for c in range(1 + DEC_BATCH)]
    rows.append(jnp.zeros((N_COND - len(rows), w.shape[1]), F32))
    o_ref[...] = jnp.concatenate(rows, axis=0)


def _ada_table(cond, ada_w, ada_b):
    tn = 3 * D_MODEL // 2
    return pl.pallas_call(
        _ada_kernel,
        out_shape=jax.ShapeDtypeStruct((DEPTH, N_COND, 3 * D_MODEL), F32),
        grid=(DEPTH, 3 * D_MODEL // tn),
        in_specs=[
            pl.BlockSpec((D_MODEL, N_COND), lambda l, n: (0, 0)),
            pl.BlockSpec((None, D_MODEL, tn), lambda l, n: (l, 0, n)),
            pl.BlockSpec((None, 1, tn), lambda l, n: (l, 0, n)),
        ],
        out_specs=pl.BlockSpec((None, N_COND, tn), lambda l, n: (l, 0, n)),
        compiler_params=_params("arbitrary", "arbitrary", vmem=VMEM_LIMIT),
        name="ada_table",
    )(cond, ada_w, ada_b.reshape(DEPTH, 1, 3 * D_MODEL))


def _rms(x, g):
    return x * lax.rsqrt(jnp.mean(x * x, axis=-1, keepdims=True) + EPS) * g


def _norm_mod(x, g_ref, sh_ref, sc_ref):
    return (_rms(x, g_ref[...]) * (1.0 + sc_ref[...]) + sh_ref[...]).astype(BF16)


def _layer_spec(w, j):
    nd = w.ndim - 1
    return pl.BlockSpec((None,) + w.shape[1:], lambda *_: (j,) + (0,) * nd, pipeline_mode=pl.Buffered(1))


def _cast_weight_once(w_ref, wb_ref):
    @pl.when(pl.program_id(0) == 0)
    def _():
        rows = wb_ref.shape[-2]
        for r in range(0, rows, MXU_N):
            wb_ref[..., r:r + MXU_N, :] = w_ref[..., r:r + MXU_N, :].astype(BF16)


class Mixed(NamedTuple):
    y_parts: tuple
    w_out: jax.Array
    j: int
    mod: jax.Array


def _residual_specs(t, mixed, x_parts):
    k = mixed.w_out.shape[1]
    return (_part_specs(t, mixed.y_parts, k) + [_layer_spec(mixed.w_out, mixed.j)]
            + _part_specs(t, x_parts, D_MODEL) + [t.mod(2)])


def _residual_args(mixed, x_parts):
    return (*mixed.y_parts, mixed.w_out, *x_parts, mixed.mod)


def _run_halves(t, x_refs, res, g_ref, sh_ref, sc_ref, xo_ref, wo_ref, h_ref, proj_steps):
    n_sub = t.tm // PROJ_SUB
    rows_of = lambda sub: slice(sub * PROJ_SUB, (sub + 1) * PROJ_SUB)

    if res is not None:
        y_refs, gate_ref = res
        for sub in range(n_sub):
            rows = rows_of(sub)
            xo_ref[rows, :] = (_read_parts(t, x_refs, rows)
                               + gate_ref[...] * _dot(_read_parts(t, y_refs, rows), wo_ref[...]))

    def normalise(sub):
        rows = rows_of(sub)
        x = xo_ref[rows, :] if res is not None else _read_parts(t, x_refs, rows)
        h_ref[rows, :] = _norm_mod(x, g_ref, sh_ref, sc_ref)

    normalise(0)
    for sub in range(n_sub):
        side = [functools.partial(normalise, sub + 1)] if sub + 1 < n_sub else []
        _run_interleaved(proj_steps(sub), side)


def _rep4(a, h):
    half = a[:, (h // 2) * LANES:(h // 2 + 1) * LANES]
    lane = lax.broadcasted_iota(jnp.int32, half.shape, 1)
    keep = (lane < ATTN_HEAD_DIM) if h % 2 == 0 else (lane >= ATTN_HEAD_DIM)
    m = jnp.where(keep, half, 0.0)
    s = m + pltpu.roll(m, ATTN_HEAD_DIM, 1)
    return jnp.concatenate([s, s], axis=1)


def _attn_in_kernel(*refs, j, n_y, n_x, n_alias):
    t = PROJ
    n_res = n_y + n_x + 2 if n_y else n_x
    res_refs, refs = refs[:n_res], refs[n_res:]
    (g_ref, sh_ref, sc_ref, wf_ref, cos_ref, sin_ref), refs = refs[:6], refs[6 + n_alias:]
    if n_y:
        xo_ref, q_ref, k4_ref, v4_ref, z_ref, kc_ref, vc_ref, wo_ref, w_ref, h_ref, kv_ref = refs
        y_refs, wof_ref, x_refs, gate_ref = (res_refs[:n_y], res_refs[n_y], res_refs[n_y + 1:n_y + 1 + n_x],
                                             res_refs[n_y + 1 + n_x])
        _cast_weight_once(wof_ref, wo_ref)
        res = (y_refs, gate_ref)
    else:
        q_ref, k4_ref, v4_ref, z_ref, kc_ref, vc_ref, w_ref, h_ref, kv_ref = refs
        x_refs, res, xo_ref, wo_ref = res_refs, None, None, None
    i = pl.program_id(0)
    _cast_weight_once(wf_ref, w_ref)
    lane = lax.broadcasted_iota(jnp.int32, (PROJ_SUB, LANES), 1)
    first = (lane % (ATTN_HEAD_DIM // 2)) < ATTN_HEAD_DIM // 4
    scale = ATTN_HEAD_DIM ** -0.5 * LOG2_E
    k_lo = ATTN_WIDTH
    v_lo = ATTN_WIDTH + ATTN_KV_WIDTH
    z_lo = ATTN_WIDTH + 2 * ATTN_KV_WIDTH

    def proj_steps(sub):
        rows = slice(sub * PROJ_SUB, (sub + 1) * PROJ_SUB)

        def rope(a):
            rot = jnp.where(first, pltpu.roll(a, LANES - ATTN_HEAD_DIM // 4, 1),
                            pltpu.roll(a, ATTN_HEAD_DIM // 4, 1))
            return a * cos_ref[rows, :] + rot * sin_ref[rows, :]

        def rope_wide(a):
            return jnp.concatenate(
                [rope(a[:, s * LANES:(s + 1) * LANES]) for s in range(MXU_N // LANES)], axis=1)

        def proj(lo):
            return _dot(h_ref[rows, :], w_ref[:, lo:lo + MXU_N])

        def q_chunk(c):
            cols = slice(c * MXU_N, (c + 1) * MXU_N)
            q_ref[rows, cols] = (rope_wide(proj(c * MXU_N)) * scale).astype(BF16)

        def keys():
            k = rope_wide(proj(k_lo))
            kv_ref[rows, :ATTN_KV_WIDTH] = k
            for hh in range(ATTN_KV_HEADS):
                k4_ref[rows, hh * MXU_N:(hh + 1) * MXU_N] = _rep4(k, hh).astype(BF16)

        def values():
            v = proj(v_lo)
            kv_ref[rows, ATTN_KV_WIDTH:] = v
            for hh in range(ATTN_KV_HEADS):
                v4_ref[rows, hh * MXU_N:(hh + 1) * MXU_N] = _rep4(v, hh).astype(BF16)

        def z_chunk(c):
            cols = slice(c * MXU_N, (c + 1) * MXU_N)
            z_ref[rows, cols] = _silu(proj(z_lo + c * MXU_N)).astype(BF16)

        n_chunks = ATTN_WIDTH // MXU_N
        return ([functools.partial(q_chunk, c) for c in range(n_chunks)] + [keys, values]
                + [functools.partial(z_chunk, c) for c in range(n_chunks)])

    _run_halves(t, x_refs, res, g_ref, sh_ref, sc_ref, xo_ref, wo_ref, h_ref, proj_steps)

    @pl.when(i < t.n_prompt)
    def _():
        for s in range(t.tm // SEQ):
            kt = kv_ref[s * SEQ:(s + 1) * SEQ, :ATTN_KV_WIDTH].T
            vt = kv_ref[s * SEQ:(s + 1) * SEQ, ATTN_KV_WIDTH:].T
            if n_alias:
                kc_ref[s] = kt
                vc_ref[s] = vt
            else:
                for l in range(kc_ref.shape[1]):
                    kc_ref[s, l] = kt if l == j else jnp.zeros_like(kt)
                    vc_ref[s, l] = vt if l == j else jnp.zeros_like(vt)


def _attn_in(mixed, x_parts, g, mod, w_all, j, cos_t, sin_t, caches):
    t = PROJ
    n_in = 2 * ATTN_WIDTH + 2 * ATTN_KV_WIDTH
    per = t.tm // SEQ
    rope_spec = pl.BlockSpec(
        (t.tm, LANES), lambda i: (jnp.where(i < t.n_prompt, 0, 1 + t.seq_tile(i)), 0))
    wide = jax.ShapeDtypeStruct((N_TOK, ATTN_WIDTH), BF16)
    n_attn = w_all.shape[0]
    cache = jax.ShapeDtypeStruct((BATCH, n_attn, ATTN_KV_WIDTH, SEQ), F32)
    if caches:
        cache_spec = pl.BlockSpec((per, None, ATTN_KV_WIDTH, SEQ),
                                  lambda i: (jnp.minimum(i, t.n_prompt - 1), j, 0, 0))
    else:
        cache_spec = pl.BlockSpec((per, n_attn, ATTN_KV_WIDTH, SEQ),
                                  lambda i: (jnp.minimum(i, t.n_prompt - 1), 0, 0, 0))
    n_x = len(x_parts)
    if mixed is None:
        n_y, res_specs, res_args, res_out, res_ospecs, res_scratch = 0, _part_specs(t, x_parts, D_MODEL), x_parts, (), (), []
    else:
        n_y = len(mixed.y_parts)
        res_specs, res_args = _residual_specs(t, mixed, x_parts), _residual_args(mixed, x_parts)
        res_out = (jax.ShapeDtypeStruct((N_TOK, D_MODEL), F32),)
        res_ospecs = (t.rows(D_MODEL),)
        res_scratch = [pltpu.VMEM(mixed.w_out.shape[1:], BF16)]
    n_front = len(res_specs) + 6
    return pl.pallas_call(
        functools.partial(_attn_in_kernel, j=j, n_y=n_y, n_x=n_x, n_alias=len(caches)),
        out_shape=res_out + (wide, wide, wide, wide, cache, cache),
        grid=(t.n_tiles,),
        in_specs=res_specs + [
            _const_spec((1, D_MODEL)), t.mod(0), t.mod(1), _layer_spec(w_all, j),
            rope_spec, rope_spec] + [pl.BlockSpec(memory_space=pl.ANY)] * len(caches),
        out_specs=res_ospecs + (t.rows(ATTN_WIDTH),) * 4 + (cache_spec,) * 2,
        scratch_shapes=res_scratch + [pltpu.VMEM((D_MODEL, n_in), BF16), pltpu.VMEM((t.tm, D_MODEL), BF16),
                                      pltpu.VMEM((t.tm, 2 * ATTN_KV_WIDTH), F32)],
        input_output_aliases={n_front + c: len(res_out) + 4 + c for c in range(len(caches))},
        compiler_params=_params("arbitrary", vmem=FUSED_VMEM_LIMIT),
        name="attn_in",
    )(*res_args, g, mod, mod, w_all, cos_t, sin_t, *caches)


def _stack_group_queries(q):
    qf = q.astype(F32)
    chunk = lax.broadcasted_iota(jnp.int32, qf.shape, 1) // ATTN_HEAD_DIM
    return jnp.concatenate(
        [jnp.where(chunk == g, qf, 0.0) for g in range(ATTN_GROUP)], axis=0).astype(BF16)


def _gather_group_outputs(o, rows):
    chunk = lax.broadcasted_iota(jnp.int32, (rows, MXU_N), 1) // ATTN_HEAD_DIM
    acc = jnp.zeros((rows, MXU_N), F32)
    for g in range(ATTN_GROUP):
        acc = acc + jnp.where(chunk == g, o[g * rows:(g + 1) * rows], 0.0)
    return acc


def _sink_column(sink_ref, h, rows):
    grp = lax.broadcasted_iota(jnp.int32, (ATTN_GROUP * rows, 1), 0) // rows
    col = jnp.zeros((ATTN_GROUP * rows, 1), F32)
    for g in range(ATTN_GROUP):
        col = jnp.where(grp == g, sink_ref[h * ATTN_GROUP + g] * LOG2_E, col)
    return col


def _chunk_rows(dtype):
    chunk = lax.broadcasted_iota(jnp.int32, (1, MXU_N), 1) // ATTN_HEAD_DIM
    return [(chunk == g).astype(F32).astype(dtype) for g in range(ATTN_GROUP)]


def _block_diag_rows(x4):
    return jnp.concatenate([x4 * m for m in _chunk_rows(x4.dtype)], axis=0)


def _ctx_attn_kernel(sink_ref, q_ref, k4_ref, v4_ref, z_ref, y_ref):
    chunk = lax.broadcasted_iota(jnp.int32, (SEQ, MXU_N), 1) // ATTN_HEAD_DIM
    for b, h in itertools.product(range(CTX_ATTN_SEQS), range(ATTN_KV_HEADS)):
        rows = slice(b * SEQ, (b + 1) * SEQ)
        cols = slice(h * MXU_N, (h + 1) * MXU_N)
        s = _dot_nt(q_ref[rows, cols], _block_diag_rows(k4_ref[rows, cols]))
        inv = jnp.zeros((SEQ, MXU_N), F32)
        probs = []
        for g in range(ATTN_GROUP):
            sg = s[:, g * SEQ:(g + 1) * SEQ]
            sk = sink_ref[h * ATTN_GROUP + g] * LOG2_E
            m = jnp.maximum(jnp.max(sg, axis=1, keepdims=True), sk)
            e = jnp.exp2(sg - m)
            den = jnp.sum(e, axis=1, keepdims=True) + jnp.exp2(sk - m)
            probs.append(e.astype(BF16))
            inv = jnp.where(chunk == g, 1.0 / den, inv)
        o = _dot(jnp.concatenate(probs, axis=1), _block_diag_rows(v4_ref[rows, cols]))
        y_ref[rows, cols] = (o * inv * z_ref[rows, cols].astype(F32)).astype(BF16)


def _ctx_attn(sink, q, k4, v4, z):
    spec = pl.BlockSpec((CTX_ATTN_SEQS * SEQ, ATTN_WIDTH), lambda b: (b, 0))
    return pl.pallas_call(
        _ctx_attn_kernel,
        out_shape=jax.ShapeDtypeStruct((N_PROMPT_TOK, ATTN_WIDTH), BF16),
        grid=(BATCH // CTX_ATTN_SEQS,),
        in_specs=[pl.BlockSpec(memory_space=pltpu.SMEM), spec, spec, spec, spec],
        out_specs=spec,
        compiler_params=_params("arbitrary"),
        name="ctx_attn",
    )(sink, q, k4, v4, z)


LAT_STEP = Q_SUB * Q_BLOCK
LAT_PER_SEQ = DEC_SEQ // LAT_STEP
LAT_BLOCKS = DEC_BATCH * ATTN_KV_HEADS * LAT_PER_SEQ
assert LAT_BLOCKS % 2 == 0


def _lat_block(blk):
    return (blk // (ATTN_KV_HEADS * LAT_PER_SEQ), (blk // LAT_PER_SEQ) % ATTN_KV_HEADS, blk % LAT_PER_SEQ)


def _band_rows(qb):
    return pl.ds(pl.multiple_of(qb * Q_BLOCK, Q_BLOCK), 3 * Q_BLOCK)


def _pad_sequence(dst, src):
    zeros = jnp.zeros((Q_BLOCK, MXU_N), BF16)
    dst[0:Q_BLOCK, :] = zeros
    dst[Q_BLOCK:Q_BLOCK + DEC_SEQ, :] = src[...]
    dst[Q_BLOCK + DEC_SEQ:, :] = zeros


def _lat_attn_kernel(sink_ref, q_ref, k4_ref, v4_ref, kc_ref, vc_ref, z_ref, y_ref,
                     kp, vp, k4c, v4c, sc_a, sl_a, m_a, sc_b, sl_b, m_b, e_scr):
    t = pl.program_id(0)
    _, _, n1 = _lat_block(jnp.minimum(t, LAT_BLOCKS - 1))
    _, h0, n0 = _lat_block(jnp.maximum(t - 1, 0))

    @pl.when(n1 == 0)
    def _():
        _pad_sequence(kp, k4_ref)
        k4c[...] = jnp.concatenate([kc_ref[...]] * ATTN_GROUP, axis=0).astype(BF16)

    @pl.when(n0 == 0)
    def _():
        _pad_sequence(vp, v4_ref)
        v4c[...] = jnp.concatenate([vc_ref[...]] * ATTN_GROUP, axis=0).T.astype(BF16)

    stages = functools.partial(_lat_stages, sink_ref, q_ref, z_ref, y_ref, kp, vp, k4c, v4c, e_scr, n1, h0, n0)
    even = t % 2 == 0
    a, b = (sc_a, sl_a, m_a), (sc_b, sl_b, m_b)
    pl.when(t == 0)(functools.partial(stages, *a, *b, finish_previous=False))
    pl.when(jnp.logical_and(even, jnp.logical_and(t > 0, t < LAT_BLOCKS)))(functools.partial(stages, *a, *b))
    pl.when(t == LAT_BLOCKS)(functools.partial(stages, *a, *b, score_next=False))
    pl.when(jnp.logical_not(even))(functools.partial(stages, *b, *a))


def _lat_stages(sink_ref, q_ref, z_ref, y_ref, kp, vp, k4c, v4c, e_scr, n1, h0, n0,
                sc_w, sl_w, m_w, sc_r, sl_r, m_r, score_next=True, finish_previous=True):
    n_blocks = DEC_SEQ // Q_BLOCK
    rows4 = ATTN_GROUP * Q_BLOCK
    r = lax.broadcasted_iota(jnp.int32, (rows4, Q_BLOCK), 0) % Q_BLOCK
    c = lax.broadcasted_iota(jnp.int32, (rows4, Q_BLOCK), 1)
    in_left = c >= r
    in_right = c <= r
    sk = _sink_column(sink_ref, h0, Q_BLOCK)
    lane_slabs = lambda a: [a[:, lo:lo + LANES] for lo in range(0, a.shape[1], LANES)]

    def scores(sub):
        qb = n1 * Q_SUB + sub
        qs = _stack_group_queries(q_ref[sub * Q_BLOCK:(sub + 1) * Q_BLOCK, :])
        s_ctx = _dot(qs, k4c[...])
        s_lat = _dot_nt(qs, kp[_band_rows(qb), :])
        band = [jnp.where(jnp.logical_and(in_left, qb > 0), s_lat[:, :Q_BLOCK], NEG_INF),
                s_lat[:, Q_BLOCK:2 * Q_BLOCK],
                jnp.where(jnp.logical_and(in_right, qb < n_blocks - 1), s_lat[:, 2 * Q_BLOCK:], NEG_INF)]
        sc_w[sub] = s_ctx
        for k, s in enumerate(band):
            sl_w[sub, :, k * Q_BLOCK:(k + 1) * Q_BLOCK] = s
        m_w[sub] = functools.reduce(jnp.maximum, lane_slabs(s_ctx) + band)

    def softmax(sub):
        m = jnp.maximum(sk, jnp.max(m_r[sub], axis=1, keepdims=True))
        es = [jnp.exp2(s - m) for s in lane_slabs(sc_r[sub]) + lane_slabs(sl_r[sub])]
        for c, e in enumerate(es):
            e_scr[sub, :, c * LANES:(c + 1) * LANES] = e.astype(BF16)
        return jnp.exp2(sk - m) + jnp.sum(functools.reduce(jnp.add, es), axis=1, keepdims=True)

    def finish(sub, den):
        qrows = slice(sub * Q_BLOCK, (sub + 1) * Q_BLOCK)
        o = (_dot(e_scr[sub, :, :PAST_LEN], v4c[...])
             + _dot(e_scr[sub, :, PAST_LEN:], vp[_band_rows(n0 * Q_SUB + sub), :])) * (1.0 / den)
        acc = _gather_group_outputs(o, Q_BLOCK)
        y_ref[qrows, :] = (acc * z_ref[qrows, :].astype(F32)).astype(BF16)

    for sub in range(Q_SUB):
        den = softmax(sub) if finish_previous else None
        if score_next:
            scores(sub)
        if finish_previous:
            finish(sub, den)


def _lat_attn(sink, q, k4, v4, kc, vc, j, z):
    padded = DEC_SEQ + 2 * Q_BLOCK
    rows4 = ATTN_GROUP * Q_BLOCK
    row0 = N_PROMPT_TOK // LAT_STEP
    seq0 = N_PROMPT_TOK // DEC_SEQ
    stage1 = lambda t: _lat_block(jnp.minimum(t, LAT_BLOCKS - 1))
    stage2 = lambda t: _lat_block(jnp.maximum(t - 1, 0))

    def spec(shape, stage, index):
        return pl.BlockSpec(shape, lambda t: index(*stage(t)))

    tile = (LAT_STEP, MXU_N)
    seq = (DEC_SEQ, MXU_N)
    ctx = (None, None, None, ATTN_HEAD_DIM, PAST_LEN)
    return pl.pallas_call(
        _lat_attn_kernel,
        out_shape=jax.ShapeDtypeStruct((N_SAMPLE_TOK, ATTN_WIDTH), BF16),
        grid=(LAT_BLOCKS + 1,),
        in_specs=[pl.BlockSpec(memory_space=pltpu.SMEM),
                  spec(tile, stage1, lambda b, h, n: (row0 + b * LAT_PER_SEQ + n, h)),
                  spec(seq, stage1, lambda b, h, n: (seq0 + b, h)),
                  spec(seq, stage2, lambda b, h, n: (seq0 + b, h)),
                  spec(ctx, stage1, lambda b, h, n: (b, j, h, 0, 0)),
                  spec(ctx, stage2, lambda b, h, n: (b, j, h, 0, 0)),
                  spec(tile, stage2, lambda b, h, n: (row0 + b * LAT_PER_SEQ + n, h))],
        out_specs=spec(tile, stage2, lambda b, h, n: (b * LAT_PER_SEQ + n, h)),
        scratch_shapes=[pltpu.VMEM((padded, MXU_N), BF16), pltpu.VMEM((padded, MXU_N), BF16),
                        pltpu.VMEM((MXU_N, PAST_LEN), BF16), pltpu.VMEM((PAST_LEN, MXU_N), BF16),
                        pltpu.VMEM((Q_SUB, rows4, PAST_LEN), F32),
                        pltpu.VMEM((Q_SUB, rows4, 3 * Q_BLOCK), F32),
                        pltpu.VMEM((Q_SUB, rows4, LANES), F32),
                        pltpu.VMEM((Q_SUB, rows4, PAST_LEN), F32),
                        pltpu.VMEM((Q_SUB, rows4, 3 * Q_BLOCK), F32),
                        pltpu.VMEM((Q_SUB, rows4, LANES), F32),
                        pltpu.VMEM((Q_SUB, rows4, PAST_LEN + 3 * Q_BLOCK), BF16)],
        compiler_params=_params("arbitrary", vmem=VMEM_LIMIT),
        name="lat_attn",
    )(sink, q, k4, v4, kc, vc, z)


def _out_final_kernel(yp_ref, ys_ref, wf_ref, x_ref, gate_ref, fg_ref, op_ref, os_ref, w_ref):
    t = FINAL
    i = pl.program_id(0)
    _cast_weight_once(wf_ref, w_ref)

    def body(y_ref, o_ref):
        for lo in range(0, t.tm, PROJ_SUB):
            rows = slice(lo, lo + PROJ_SUB)
            o_ref[rows, :] = _rms(x_ref[rows, :] + gate_ref[...] * _dot(y_ref[rows, :], w_ref[...]), fg_ref[...])

    pl.when(i < t.n_prompt)(functools.partial(body, yp_ref, op_ref))
    pl.when(i >= t.n_prompt)(functools.partial(body, ys_ref, os_ref))


def _out_proj_final(mixed, x, final_g):
    t = FINAL
    return pl.pallas_call(
        _out_final_kernel,
        out_shape=(jax.ShapeDtypeStruct((N_PROMPT_TOK, D_MODEL), F32),
                   jax.ShapeDtypeStruct((N_SAMPLE_TOK, D_MODEL), F32)),
        grid=(t.n_tiles,),
        in_specs=_residual_specs(t, mixed, (x,)) + [_const_spec((1, D_MODEL))],
        out_specs=(t.prompt_rows(D_MODEL), t.sample_rows(D_MODEL)),
        scratch_shapes=[pltpu.VMEM(mixed.w_out.shape[1:], BF16)],
        compiler_params=_params("arbitrary", vmem=VMEM_LIMIT),
        name="out_proj_final",
    )(*_residual_args(mixed, (x,)), final_g)


def _pool_in_kernel(*refs, n_y, n_x):
    t = PROJ
    n_res = n_y + n_x + 2
    res_refs, refs = refs[:n_res], refs[n_res:]
    g_ref, sh_ref, sc_ref, wf_ref, wn_ref, xo_ref, u_ref, z_ref, wnb_ref, wo_ref, w_ref, h_ref = refs
    y_refs, wof_ref, x_refs, gate_ref = (res_refs[:n_y], res_refs[n_y], res_refs[n_y + 1:n_y + 1 + n_x],
                                         res_refs[n_y + 1 + n_x])
    _cast_weight_once(wof_ref, wo_ref)
    _cast_weight_once(wf_ref, w_ref)
    wnb_ref[...] = wn_ref[...].astype(BF16)

    def proj_steps(sub):
        rows = slice(sub * PROJ_SUB, (sub + 1) * PROJ_SUB)

        def chunk(c):
            def step():
                cols = slice(c * MXU_N, (c + 1) * MXU_N)
                u_ref[rows, cols] = _dot(h_ref[rows, :], w_ref[:, cols])
            return step

        def gate_chunk(c):
            def step():
                a = _dot(h_ref[rows, :], w_ref[:, D_MODEL + c * MXU_N:D_MODEL + (c + 1) * MXU_N])
                z_ref[rows, c * MXU_N:(c + 1) * MXU_N] = _silu(a).astype(BF16)
            return step

        n = D_MODEL // MXU_N
        return [chunk(c) for c in range(n)] + [gate_chunk(c) for c in range(n)]

    _run_halves(t, x_refs, (y_refs, gate_ref), g_ref, sh_ref, sc_ref, xo_ref, wo_ref, h_ref, proj_steps)


def _pool_in(mixed, x_parts, g, mod, w_all, j, w_next_all, j_next):
    t = PROJ
    slab = jax.ShapeDtypeStruct((N_TOK, D_MODEL), F32)
    k_next, n_next = w_next_all.shape[1:]
    return pl.pallas_call(
        functools.partial(_pool_in_kernel, n_y=len(mixed.y_parts), n_x=len(x_parts)),
        out_shape=(slab, slab, jax.ShapeDtypeStruct((N_TOK, D_MODEL), BF16),
                   jax.ShapeDtypeStruct((k_next, n_next), BF16)),
        grid=(t.n_tiles,),
        in_specs=_residual_specs(t, mixed, x_parts) + [
            _const_spec((1, D_MODEL)), t.mod(0), t.mod(1), _layer_spec(w_all, j),
            pl.BlockSpec((None, k_next // t.n_tiles, n_next), lambda i: (j_next, i, 0))],
        out_specs=(t.rows(D_MODEL),) * 3 + (pl.BlockSpec((k_next // t.n_tiles, n_next), lambda i: (i, 0)),),
        scratch_shapes=[pltpu.VMEM(mixed.w_out.shape[1:], BF16), pltpu.VMEM((D_MODEL, 2 * D_MODEL), BF16),
                        pltpu.VMEM((t.tm, D_MODEL), BF16)],
        compiler_params=_params("arbitrary", vmem=FUSED_VMEM_LIMIT),
        name="pool_in",
    )(*_residual_args(mixed, x_parts), g, mod, mod, w_all, w_next_all)


def _split_bf16(a):
    hi = a.astype(BF16)
    return hi, (a - hi.astype(F32)).astype(BF16)


def _band_ones(shape, lo, hi):
    d = lax.broadcasted_iota(jnp.int32, shape, 1) - lax.broadcasted_iota(jnp.int32, shape, 0)
    return jnp.logical_and(d >= lo, d <= hi).astype(F32).astype(BF16)


def _pool_ret_kernel(u_ref, up_ref, un_ref, z_ref, wgf_ref, ps_ref, wof_ref, x_ref, gate_ref,
                     g_ref, sh_ref, sc_ref, w_ref,
                     xo_ref, q_ref, kt_ref, v_ref, zr_ref,
                     y_ref, wg_ref, wo_ref, band_ref, wkt_ref, h_ref):
    t = PROJ
    i = pl.program_id(0)
    _cast_weight_once(wgf_ref, wg_ref)
    _cast_weight_once(wof_ref, wo_ref)

    @pl.when(i == 0)
    def _():
        for g, w in enumerate(POOL_WINDOWS):
            band_ref[g] = _band_ones((POOL_SUB, POOL_SUB), -(w // 2), w - 1 - w // 2)
        for r in range(0, RET_QK_WIDTH, MXU_N):
            wk = w_ref[:, RET_QK_WIDTH + r:RET_QK_WIDTH + r + MXU_N]
            wkt_ref[r:r + MXU_N, :] = wk.astype(F32).T.astype(BF16)

    is_dec = i >= t.n_prompt
    st = t.seq_tile(i)
    seq_len = jnp.where(is_dec, DEC_SEQ, SEQ)
    h = POOL_HALO
    n_sub = t.tm // POOL_SUB
    def pool_steps(sub):
        rows = slice(sub * POOL_SUB, (sub + 1) * POOL_SUB)
        if sub == 0:
            before = jnp.where(jnp.logical_and(is_dec, st != 0), up_ref[...], 0.0)
        else:
            before = jnp.where(is_dec, u_ref[sub * POOL_SUB - h:sub * POOL_SUB, :], 0.0)
        if sub == n_sub - 1:
            after = jnp.where(jnp.logical_and(is_dec, st != t.per_dec_seq - 1), un_ref[...], 0.0)
        else:
            after = jnp.where(is_dec, u_ref[(sub + 1) * POOL_SUB:(sub + 1) * POOL_SUB + h, :], 0.0)
        pos0 = jnp.where(is_dec, st * t.tm + sub * POOL_SUB, 0)
        return _pool_core_steps(u_ref.at[rows], before, after, pos0, seq_len, z_ref.at[rows], ps_ref,
                                wg_ref, band_ref, y_ref.at[rows])

    def mix_steps(sub):
        rows = slice(sub * POOL_SUB, (sub + 1) * POOL_SUB)

        def residual():
            xo_ref[rows, :] = x_ref[rows, :] + gate_ref[...] * _dot(y_ref[rows, :], wo_ref[...])

        def normalise():
            h_ref[rows, :] = _norm_mod(xo_ref[rows, :], g_ref, sh_ref, sc_ref)

        return pool_steps(sub) + [residual, normalise]

    for step in mix_steps(0):
        step()
    for sub in range(n_sub):
        rows = slice(sub * POOL_SUB, (sub + 1) * POOL_SUB)
        proj = _ret_in_steps(h_ref.at[rows], w_ref, wkt_ref, q_ref.at[rows],
                             kt_ref.at[pl.ds(sub * POOL_SUB // RET_CHUNK, POOL_SUB // RET_CHUNK)],
                             v_ref.at[rows], zr_ref.at[rows])
        _run_interleaved(proj, mix_steps(sub + 1) if sub + 1 < n_sub else [])


def _pool_core_steps(u_ref, before, after, pos0, seq_len, z_ref, ps_ref, wg_ref, band_ref, y_ref):
    n = u_ref.shape[0]
    h = POOL_HALO
    halo_hi, halo_lo = _split_bf16(jnp.concatenate([before, after], axis=0))
    pos = pos0 + lax.broadcasted_iota(jnp.int32, (n, 1), 0)
    rr = lax.broadcasted_iota(jnp.int32, (2 * h, 2 * h), 0)
    cc = lax.broadcasted_iota(jnp.int32, (2 * h, 2 * h), 1)

    def pooled(g, w):
        def step():
            left = w // 2
            right = w - 1 - left
            cols = slice(g * POOL_GROUP_DIM, (g + 1) * POOL_GROUP_DIM)
            u = u_ref[:, cols]
            s = _dot(band_ref[g], u.astype(BF16))
            top = jnp.logical_and(jnp.logical_and(rr < h, cc < h), cc - h >= rr - left)
            bot = jnp.logical_and(jnp.logical_and(rr >= h, cc >= h), cc - h <= rr - 2 * h + right)
            edge = jnp.logical_or(top, bot).astype(F32).astype(BF16)
            se = _dot(edge, halo_hi[:, cols]) + _dot(edge, halo_lo[:, cols])
            s = jnp.concatenate([s[:h] + se[:h], s[h:n - h], s[n - h:] + se[h:]], axis=0)
            cnt = (jnp.minimum(pos + right + 1, seq_len) - jnp.maximum(pos - left, 0)).astype(F32)
            y_ref[:, cols] = (s / cnt - u).astype(BF16)
        return step

    def mixed(g):
        def step():
            cols = slice(g * POOL_GROUP_DIM, (g + 1) * POOL_GROUP_DIM)
            yg = _dot(y_ref[:, cols], wg_ref[g]) * ps_ref[:, cols] * z_ref[:, cols].astype(F32)
            y_ref[:, cols] = yg.astype(BF16)
        return step

    return ([pooled(g, w) for g, w in enumerate(POOL_WINDOWS)]
            + [mixed(g) for g in range(len(POOL_WINDOWS))])


def _pool_ret(u, z, wg_all, ps, wo_all, j, x, mod_pool, g, mod, w_ret):
    t = PROJ
    per = t.tm // POOL_HALO
    n_halo = N_TOK // POOL_HALO
    n_in = 2 * RET_QK_WIDTH + 2 * RET_V_WIDTH
    kt_per = t.tm // RET_CHUNK
    return pl.pallas_call(
        _pool_ret_kernel,
        out_shape=(jax.ShapeDtypeStruct((N_TOK, D_MODEL), F32),
                   jax.ShapeDtypeStruct((N_TOK, RET_QK_WIDTH), BF16),
                   jax.ShapeDtypeStruct((N_TOK // RET_CHUNK, RET_QK_WIDTH, RET_CHUNK), BF16),
                   jax.ShapeDtypeStruct((N_TOK, RET_V_WIDTH), BF16),
                   jax.ShapeDtypeStruct((N_TOK, RET_V_WIDTH), BF16)),
        grid=(t.n_tiles,),
        in_specs=[
            t.rows(D_MODEL),
            pl.BlockSpec((POOL_HALO, D_MODEL), lambda i: (jnp.maximum(i * per - 1, 0), 0)),
            pl.BlockSpec((POOL_HALO, D_MODEL), lambda i: (jnp.minimum((i + 1) * per, n_halo - 1), 0)),
            t.rows(D_MODEL),
            _layer_spec(wg_all, j),
            _const_spec((1, D_MODEL)),
            _layer_spec(wo_all, j),
            t.rows(D_MODEL),
            t.mod(2),
            _const_spec((1, D_MODEL)), t.mod(0), t.mod(1),
            pl.BlockSpec((D_MODEL, n_in), lambda i: (0, 0), pipeline_mode=pl.Buffered(1)),
        ],
        out_specs=(t.rows(D_MODEL), t.rows(RET_QK_WIDTH),
                   pl.BlockSpec((kt_per, RET_QK_WIDTH, RET_CHUNK), lambda i: (i, 0, 0)),
                   t.rows(RET_V_WIDTH), t.rows(RET_V_WIDTH)),
        scratch_shapes=[pltpu.VMEM((t.tm, D_MODEL), BF16),
                        pltpu.VMEM((len(POOL_WINDOWS), POOL_GROUP_DIM, POOL_GROUP_DIM), BF16),
                        pltpu.VMEM((D_MODEL, D_MODEL), BF16),
                        pltpu.VMEM((len(POOL_WINDOWS), POOL_SUB, POOL_SUB), BF16),
                        pltpu.VMEM((RET_QK_WIDTH, D_MODEL), BF16),
                        pltpu.VMEM((t.tm, D_MODEL), BF16)],
        compiler_params=_params("arbitrary", vmem=FUSED_VMEM_LIMIT),
        name="pool_ret",
    )(u, u, u, z, wg_all, ps, wo_all, x, mod_pool, g, mod, mod, w_ret)


def _ret_in_steps(h_ref, w_ref, wkt_ref, q_ref, kt_ref, v_ref, z_ref):
    n_rows = h_ref.shape[0]

    def chunk(ref, lo, c, post=lambda a: a):
        def step():
            cols = slice(c * MXU_N, (c + 1) * MXU_N)
            ref[:, cols] = post(_dot(h_ref[...], w_ref[:, lo + c * MXU_N:lo + (c + 1) * MXU_N])).astype(BF16)
        return step

    def key_chunk(c):
        def step():
            rows = slice(c * MXU_N, (c + 1) * MXU_N)
            kt = (_dot_nt(wkt_ref[rows, :], h_ref[...]) * RET_KEY_DIM ** -0.5).astype(BF16)
            for cc in range(n_rows // RET_CHUNK):
                kt_ref[cc, rows, :] = kt[:, cc * RET_CHUNK:(cc + 1) * RET_CHUNK]
        return step

    steps = [chunk(q_ref, 0, c) for c in range(RET_QK_WIDTH // MXU_N)]
    steps += [key_chunk(c) for c in range(RET_QK_WIDTH // MXU_N)]
    steps += [chunk(v_ref, 2 * RET_QK_WIDTH, c) for c in range(RET_V_WIDTH // MXU_N)]
    steps += [chunk(z_ref, 2 * RET_QK_WIDTH + RET_V_WIDTH, c, _silu) for c in range(RET_V_WIDTH // MXU_N)]
    return steps


def _run_interleaved(main, side):
    done = 0
    for k, step in enumerate(main):
        step()
        due = (k + 1) * len(side) // len(main)
        for s in side[done:due]:
            s()
        done = due


def _pos(shape, axis):
    return lax.broadcasted_iota(jnp.int32, shape, axis).astype(F32)


def _ret_tables_kernel(lgf_ref, lgb_ref, decay_ref, row_ref, col_ref, cdec_ref):
    h = pl.program_id(0)
    lg_f = lgf_ref[h]
    lg_b = lgb_ref[h]
    c = RET_CHUNK
    diff = _pos((c, c), 0) - _pos((c, c), 1)
    fwd = jnp.where(diff >= 0, jnp.exp(jnp.maximum(diff, 0.0) * lg_f), 0.0)
    bwd = jnp.where(diff <= 0, jnp.exp(jnp.maximum(-diff, 0.0) * lg_b), 0.0)
    decay_ref[...] = fwd + bwd
    j = _pos((RET_TAB_ROWS, c), 1)
    row_ref[0] = jnp.exp((c - 1.0 - j) * lg_f)
    row_ref[1] = jnp.exp(j * lg_b)
    i = _pos((c, LANES), 0)
    col_ref[0] = jnp.exp((i + 1.0) * lg_f)
    col_ref[1] = jnp.exp((c - i) * lg_b)
    full = jnp.full((RET_TAB_ROWS, RET_VAL_DIM), float(c), F32)
    cdec_ref[0] = jnp.exp(full * lg_f)
    cdec_ref[1] = jnp.exp(full * lg_b)


def _ret_tables(lg_f, lg_b):
    smem = pl.BlockSpec(memory_space=pltpu.SMEM)
    c = RET_CHUNK
    shapes = ((c, c), (2, RET_TAB_ROWS, c), (2, c, LANES), (2, RET_TAB_ROWS, RET_VAL_DIM))
    return pl.pallas_call(
        _ret_tables_kernel,
        out_shape=tuple(jax.ShapeDtypeStruct((RET_HEADS,) + s, F32) for s in shapes),
        grid=(RET_HEADS,),
        in_specs=[smem, smem],
        out_specs=tuple(pl.BlockSpec((None,) + s, lambda h, n=len(s): (h,) + (0,) * n) for s in shapes),
        compiler_params=_params("arbitrary"),
        name="ret_tables",
    )(lg_f, lg_b)


def _group_norm_gate(o, gn, z):
    mu = jnp.mean(o, axis=-1, keepdims=True)
    var = jnp.mean(jnp.square(o - mu), axis=-1, keepdims=True)
    on = (o - mu) * lax.rsqrt(var + EPS)
    return (on * gn * z.astype(F32)).astype(BF16)


def _ret_ctx_kernel(q_ref, kt_ref, v_ref, z_ref, gn_ref, decay_ref, row_ref, y_ref, sf_ref, sb_ref):
    for s in range(RET_CTX_SEQS):
        rows = slice(s * SEQ, (s + 1) * SEQ)
        for h in range(RET_HEADS):
            kc = slice(h * RET_KEY_DIM, (h + 1) * RET_KEY_DIM)
            vc = slice(h * RET_VAL_DIM, (h + 1) * RET_VAL_DIM)
            kt = kt_ref[s, kc, :]
            v = v_ref[rows, vc]
            att = (_dot(q_ref[rows, kc], kt) * decay_ref[h]).astype(BF16)
            y_ref[rows, vc] = _group_norm_gate(_dot(att, v), gn_ref[:, vc], z_ref[rows, vc])
            ktf = kt.astype(F32)
            sf_ref[s, h] = _dot((ktf * row_ref[h, 0, 0:1, :]).astype(BF16), v)
            sb_ref[s, h] = _dot((ktf * row_ref[h, 1, 0:1, :]).astype(BF16), v)


def _ret_ctx(q, kt, v, z, gn, decay, row):
    c = RET_CHUNK
    n = RET_CTX_SEQS
    wide = lambda w: pl.BlockSpec((n * SEQ, w), lambda b: (b, 0))
    st_spec = pl.BlockSpec((n, RET_HEADS, RET_KEY_DIM, RET_VAL_DIM), lambda b: (b, 0, 0, 0))
    st_shape = jax.ShapeDtypeStruct((BATCH, RET_HEADS, RET_KEY_DIM, RET_VAL_DIM), F32)
    return pl.pallas_call(
        _ret_ctx_kernel,
        out_shape=(jax.ShapeDtypeStruct((N_PROMPT_TOK, RET_V_WIDTH), BF16), st_shape, st_shape),
        grid=(BATCH // n,),
        in_specs=[wide(RET_QK_WIDTH), pl.BlockSpec((n, RET_QK_WIDTH, c), lambda b: (b, 0, 0)),
                  wide(RET_V_WIDTH), wide(RET_V_WIDTH), _const_spec((1, RET_V_WIDTH)),
                  _const_spec((RET_HEADS, c, c)), _const_spec((RET_HEADS, 2, RET_TAB_ROWS, c))],
        out_specs=(wide(RET_V_WIDTH), st_spec, st_spec),
        compiler_params=_params("arbitrary", vmem=VMEM_LIMIT),
        name="ret_ctx",
    )(q, kt, v, z, gn, decay, row)


def _ret_lat_kernel(q_ref, kt_ref, v_ref, z_ref, gn_ref, decay_ref, row_ref, col_ref, cdec_ref,
                    s0f_ref, s0b_ref, y_ref, sf_all, sb_all, sf_acc, sb_acc):
    c = RET_CHUNK
    n_chunks = DEC_SEQ // c
    rows_of = lambda ci: pl.ds(pl.multiple_of(ci * c, c), c)

    sf_acc[...] = s0f_ref[...]
    sb_acc[...] = s0b_ref[...]

    def scan_step(i, carry):
        cf = i
        cb = n_chunks - 1 - i
        sf_all[cf] = sf_acc[...].astype(BF16)
        sb_all[cb] = sb_acc[...].astype(BF16)
        uf = _dot((kt_ref[cf].astype(F32) * row_ref[0, 0:1, :]).astype(BF16), v_ref[rows_of(cf), :])
        ub = _dot((kt_ref[cb].astype(F32) * row_ref[1, 0:1, :]).astype(BF16), v_ref[rows_of(cb), :])
        sf_acc[...] = sf_acc[...] * cdec_ref[0, 0:1, :] + uf
        sb_acc[...] = sb_acc[...] * cdec_ref[1, 0:1, :] + ub
        return carry

    lax.fori_loop(0, n_chunks, scan_step, 0, unroll=4)

    def out_step(ci, carry):
        rows = rows_of(ci)
        q = q_ref[rows, :]
        qf = q.astype(F32)
        qdec_f = jnp.concatenate([col_ref[0]] * (RET_KEY_DIM // LANES), axis=1)
        qdec_b = jnp.concatenate([col_ref[1]] * (RET_KEY_DIM // LANES), axis=1)
        att = (_dot(q, kt_ref[ci]) * decay_ref[...]).astype(BF16)
        o = (_dot(att, v_ref[rows, :])
             + _dot((qf * qdec_f).astype(BF16), sf_all[ci])
             + _dot((qf * qdec_b).astype(BF16), sb_all[ci]))
        y_ref[rows, :] = _group_norm_gate(o, gn_ref[...], z_ref[rows, :])
        return carry

    lax.fori_loop(0, n_chunks, out_step, 0, unroll=8)


def _ret_lat(q, kt, v, z, gn, decay, row, col, cdec, s0f, s0b):
    c = RET_CHUNK
    n_chunks = DEC_SEQ // c
    row0 = N_PROMPT_TOK // DEC_SEQ
    qk_spec = pl.BlockSpec((DEC_SEQ, RET_KEY_DIM), lambda b, h: (row0 + b, h))
    v_spec = pl.BlockSpec((DEC_SEQ, RET_VAL_DIM), lambda b, h: (row0 + b, h))
    st_spec = pl.BlockSpec((None, None, RET_KEY_DIM, RET_VAL_DIM), lambda b, h: (b, h, 0, 0))
    tab = lambda *s: pl.BlockSpec((None,) + s, lambda b, h: (h,) + (0,) * len(s))
    states = pltpu.VMEM((n_chunks, RET_KEY_DIM, RET_VAL_DIM), BF16)
    acc = pltpu.VMEM((RET_KEY_DIM, RET_VAL_DIM), F32)
    return pl.pallas_call(
        _ret_lat_kernel,
        out_shape=jax.ShapeDtypeStruct((N_SAMPLE_TOK, RET_V_WIDTH), BF16),
        grid=(DEC_BATCH, RET_HEADS),
        in_specs=[qk_spec,
                  pl.BlockSpec((n_chunks, RET_KEY_DIM, c), lambda b, h: (row0 + b, h, 0)),
                  v_spec, v_spec, pl.BlockSpec((1, RET_VAL_DIM), lambda b, h: (0, h)),
                  tab(c, c), tab(2, RET_TAB_ROWS, c), tab(2, c, LANES), tab(2, RET_TAB_ROWS, RET_VAL_DIM),
                  st_spec, st_spec],
        out_specs=pl.BlockSpec((DEC_SEQ, RET_VAL_DIM), lambda b, h: (b, h)),
        scratch_shapes=[states, states, acc, acc],
        compiler_params=_params("arbitrary", "arbitrary", vmem=VMEM_LIMIT),
        name="ret_lat",
    )(q, kt, v, z, gn, decay, row, col, cdec, s0f, s0b)


def _rope_tables(tm):
    n_rows = DEC_SEQ // GRID_W
    rows = jnp.repeat(jnp.arange(n_rows), GRID_W).astype(F32)
    cols = jnp.tile(jnp.arange(GRID_W), n_rows).astype(F32)
    half = ATTN_HEAD_DIM // 4
    inv = ROPE_BASE ** (-jnp.arange(half, dtype=F32) / half)
    ang_r = rows[:, None] * inv[None, :]
    ang_c = cols[:, None] * inv[None, :]
    cos = jnp.concatenate([jnp.cos(ang_r), jnp.cos(ang_r), jnp.cos(ang_c), jnp.cos(ang_c)], axis=-1)
    sin = jnp.concatenate([-jnp.sin(ang_r), jnp.sin(ang_r), -jnp.sin(ang_c), jnp.sin(ang_c)], axis=-1)
    cos = jnp.concatenate([jnp.ones((tm, ATTN_HEAD_DIM), F32), cos], axis=0)
    sin = jnp.concatenate([jnp.zeros((tm, ATTN_HEAD_DIM), F32), sin], axis=0)
    return jnp.tile(cos, (1, 2)), jnp.tile(sin, (1, 2))


def kernel(x_prompt, x_sample, cache_k, cache_v, state_fwd, state_bwd, c, c_ctx, norm_g, ada_w, ada_b, attn_w_in, attn_w_out, attn_sink, pool_w_in, pool_w_grp, pool_scale, pool_w_out, ret_w_in, ret_decay_fwd, ret_decay_bwd, ret_gn_g, ret_w_out, final_g):
    x_parts = (x_prompt.reshape(N_PROMPT_TOK, D_MODEL), x_sample.reshape(N_SAMPLE_TOK, D_MODEL))
    cond = jnp.concatenate([c_ctx[None, :], c,
                            jnp.zeros((N_COND - 1 - DEC_BATCH, D_MODEL), F32)], axis=0)
    mods = _ada_table(cond.T, ada_w, ada_b).reshape(DEPTH, N_COND, 1, 3 * D_MODEL)
    cos_t, sin_t = _rope_tables(PROJ.tm)

    to_kernel = lambda a: jnp.transpose(a, (0, 1, 3, 4, 2))
    from_kernel = lambda a: jnp.transpose(
        a.reshape(a.shape[0], a.shape[1], ATTN_KV_HEADS, ATTN_HEAD_DIM, a.shape[3]), (0, 1, 4, 2, 3))
    ctx_k, ctx_v = to_kernel(cache_k), to_kernel(cache_v)

    assert DEPTH % N_MIXERS == 1, "the layer stack must end on an attention layer"
    caches = ()
    new_sf = new_sb = None
    mixed = None
    for i in range(DEPTH):
        kind, j = i % N_MIXERS, i // N_MIXERS
        g = norm_g[i].reshape(1, D_MODEL)
        mod = mods[i]
        if kind == 0:
            outs = _attn_in(mixed, x_parts, g, mod, attn_w_in, j, cos_t, sin_t, tuple(caches))
            if mixed is not None:
                x_parts, outs = (outs[0],), outs[1:]
            q, k4, v4, z, *caches = outs
            y_parts = (_ctx_attn(attn_sink[j], q, k4, v4, z),
                       _lat_attn(attn_sink[j], q, k4, v4, ctx_k, ctx_v, j, z))
            mixed = Mixed(y_parts, attn_w_out, j, mod)
        elif kind == 1:
            x, u, z, w_ret = _pool_in(mixed, x_parts, g, mod, pool_w_in, j, ret_w_in, j)
            x_parts, mixed, pooled = (x,), None, (u, z, j, mod, w_ret)
        else:
            u, z, jp, mod_pool, w_ret = pooled
            lg_f = jax.nn.log_sigmoid(ret_decay_fwd[j].astype(F32))
            lg_b = jax.nn.log_sigmoid(ret_decay_bwd[j].astype(F32))
            gn = ret_gn_g[j].reshape(1, RET_V_WIDTH)
            x, q, kt, v, z = _pool_ret(u, z, pool_w_grp, pool_scale[jp].reshape(1, D_MODEL), pool_w_out, jp,
                                       x_parts[0], mod_pool, g, mod, w_ret)
            x_parts = (x,)
            decay, row, col, cdec = _ret_tables(lg_f, lg_b)
            y_ctx, new_sf, new_sb = _ret_ctx(q, kt, v, z, gn, decay, row)
            y_parts = (y_ctx, _ret_lat(q, kt, v, z, gn, decay, row, col, cdec,
                                       state_fwd[:, j], state_bwd[:, j]))
            mixed = Mixed(y_parts, ret_w_out, j, mod)
    y_prompt, y_sample = _out_proj_final(mixed, x_parts[0], final_g.reshape(1, D_MODEL))
    new_k, new_v = caches
    return (y_prompt.reshape(BATCH, SEQ, D_MODEL), y_sample.reshape(DEC_BATCH, DEC_SEQ, D_MODEL),
            from_kernel(new_k), from_kernel(new_v), new_sf[:, None], new_sb[:, None])
```

```python
import functools
import itertools
from typing import NamedTuple

import jax
import jax.numpy as jnp
from jax import lax
from jax.experimental import pallas as pl
from jax.experimental.pallas import tpu as pltpu

F32 = jnp.float32
BF16 = jnp.bfloat16

D_MODEL = 1024
BATCH = 16
SEQ = 256
DEPTH = 4
DEC_BATCH = 2
DEC_SEQ = 2048
PAST_LEN = 512
GRID_W = 64
N_MIXERS = 3
ATTN_HEADS = 16
ATTN_KV_HEADS = 4
ATTN_HEAD_DIM = 64
ATTN_GROUP = 4
ATTN_WIDTH = 1024
ATTN_KV_WIDTH = 256
WINDOW = 128
ROPE_BASE = 10000.0
POOL_WINDOWS = (2, 4, 8, 16)
POOL_GROUP_DIM = 256
RET_HEADS = 4
RET_KEY_DIM = 256
RET_VAL_DIM = 512
RET_QK_WIDTH = 1024
RET_V_WIDTH = 2048
EPS = 1e-6
NEG_INF = -1e30
LOG2_E = 1.4426950408889634

N_PROMPT_TOK = BATCH * SEQ
N_SAMPLE_TOK = DEC_BATCH * DEC_SEQ
N_TOK = N_PROMPT_TOK + N_SAMPLE_TOK
N_COND = 8
LANES = 128
MXU_N = 256
Q_BLOCK = 128
Q_SUB = 4
RET_CHUNK = 256
RET_TAB_ROWS = 8
RET_CTX_SEQS = 2
CTX_ATTN_SEQS = 4
POOL_HALO = 8
POOL_SUB = SEQ
PROJ_SUB = 256
VMEM_LIMIT = 48 * 1024 * 1024
FUSED_VMEM_LIMIT = 58 * 1024 * 1024


class Tiling(NamedTuple):
    tm: int

    @property
    def n_tiles(self):
        return N_TOK // self.tm

    @property
    def n_prompt(self):
        return N_PROMPT_TOK // self.tm

    @property
    def per_dec_seq(self):
        return DEC_SEQ // self.tm

    def cond(self, i):
        return jnp.where(i < self.n_prompt, 0, 1 + (i - self.n_prompt) // self.per_dec_seq)

    def seq_tile(self, i):
        return jnp.where(i < self.n_prompt, 0, (i - self.n_prompt) % self.per_dec_seq)

    def rows(self, width):
        return pl.BlockSpec((self.tm, width), lambda i: (i, 0))

    def prompt_rows(self, width):
        return pl.BlockSpec((self.tm, width), lambda i: (jnp.minimum(i, self.n_prompt - 1), 0))

    def sample_rows(self, width):
        return pl.BlockSpec((self.tm, width), lambda i: (jnp.maximum(i - self.n_prompt, 0), 0))

    def mod(self, part):
        return pl.BlockSpec((None, 1, D_MODEL), lambda i: (self.cond(i), 0, part))


PROJ = Tiling(512)
FINAL = Tiling(1024)


def _silu(z):
    hz = 0.5 * z
    return hz + hz * jnp.tanh(hz)


def _dot(a, b):
    return jnp.dot(a, b, preferred_element_type=F32)


def _dot_nt(a, b):
    return lax.dot_general(a, b, (((1,), (1,)), ((), ())), preferred_element_type=F32)


def _params(*sem, vmem=None):
    return pltpu.CompilerParams(dimension_semantics=sem, vmem_limit_bytes=vmem)


def _const_spec(shape):
    nd = len(shape)
    return pl.BlockSpec(shape, lambda *_: (0,) * nd, pipeline_mode=pl.Buffered(1))


def _part_specs(t, parts, width):
    if len(parts) == 1:
        return [t.rows(width)]
    return [t.prompt_rows(width), t.sample_rows(width)]


def _read_parts(t, refs, rows=slice(None)):
    if len(refs) == 1:
        return refs[0][rows, :]
    return jnp.where(pl.program_id(0) < t.n_prompt, refs[0][rows, :], refs[1][rows, :])


def _ada_kernel(cond_ref, w_ref, b_ref, o_ref):
    s = _silu(cond_ref[...])
    w = w_ref[...]
    rows = [jnp.sum(s[:, c:c + 1] * w, axis=0, keepdims=True) + b_ref[...] for c in range(1 + DEC_BATCH)]
    rows.append(jnp.zeros((N_COND - len(rows), w.shape[1]), F32))
    o_ref[...] = jnp.concatenate(rows, axis=0)


def _ada_table(cond, ada_w, ada_b):
    tn = 3 * D_MODEL // 2
    return pl.pallas_call(
        _ada_kernel,
        out_shape=jax.ShapeDtypeStruct((DEPTH, N_COND, 3 * D_MODEL), F32),
        grid=(DEPTH, 3 * D_MODEL // tn),
        in_specs=[
            pl.BlockSpec((D_MODEL, N_COND), lambda l, n: (0, 0)),
            pl.BlockSpec((None, D_MODEL, tn), lambda l, n: (l, 0, n)),
            pl.BlockSpec((None, 1, tn), lambda l, n: (l, 0, n)),
        ],
        out_specs=pl.BlockSpec((None, N_COND, tn), lambda l, n: (l, 0, n)),
        compiler_params=_params("arbitrary", "arbitrary", vmem=VMEM_LIMIT),
        name="ada_table",
    )(cond, ada_w, ada_b.reshape(DEPTH, 1, 3 * D_MODEL))


def _rms(x, g):
    return x * lax.rsqrt(jnp.mean(x * x, axis=-1, keepdims=True) + EPS) * g


def _norm_mod(x, g_ref, sh_ref, sc_ref):
    return (_rms(x, g_ref[...]) * (1.0 + sc_ref[...]) + sh_ref[...]).astype(BF16)


def _layer_spec(w, j):
    nd = w.ndim - 1
    return pl.BlockSpec((None,) + w.shape[1:], lambda *_: (j,) + (0,) * nd, pipeline_mode=pl.Buffered(1))


def _cast_weight_once(w_ref, wb_ref):
    @pl.when(pl.program_id(0) == 0)
    def _():
        rows = wb_ref.shape[-2]
        for r in range(0, rows, MXU_N):
            wb_ref[..., r:r + MXU_N, :] = w_ref[..., r:r + MXU_N, :].astype(BF16)


class Mixed(NamedTuple):
    y_parts: tuple
    w_out: jax.Array
    j: int
    mod: jax.Array


def _residual_specs(t, mixed, x_parts):
    k = mixed.w_out.shape[1]
    return (_part_specs(t, mixed.y_parts, k) + [_layer_spec(mixed.w_out, mixed.j)]
            + _part_specs(t, x_parts, D_MODEL) + [t.mod(2)])


def _residual_args(mixed, x_parts):
    return (*mixed.y_parts, mixed.w_out, *x_parts, mixed.mod)


def _run_halves(t, x_refs, res, g_ref, sh_ref, sc_ref, xo_ref, wo_ref, h_ref, proj_steps):
    n_sub = t.tm // PROJ_SUB
    rows_of = lambda sub: slice(sub * PROJ_SUB, (sub + 1) * PROJ_SUB)

    if res is not None:
        y_refs, gate_ref = res
        for sub in range(n_sub):
            rows = rows_of(sub)
            xo_ref[rows, :] = (_read_parts(t, x_refs, rows)
                               + gate_ref[...] * _dot(_read_parts(t, y_refs, rows), wo_ref[...]))

    def normalise(sub):
        rows = rows_of(sub)
        x = xo_ref[rows, :] if res is not None else _read_parts(t, x_refs, rows)
        h_ref[rows, :] = _norm_mod(x, g_ref, sh_ref, sc_ref)

    normalise(0)
    for sub in range(n_sub):
        side = [functools.partial(normalise, sub + 1)] if sub + 1 < n_sub else []
        _run_interleaved(proj_steps(sub), side)


def _rep4(a, h):
    half = a[:, (h // 2) * LANES:(h // 2 + 1) * LANES]
    lane = lax.broadcasted_iota(jnp.int32, half.shape, 1)
    keep = (lane < ATTN_HEAD_DIM) if h % 2 == 0 else (lane >= ATTN_HEAD_DIM)
    m = jnp.where(keep, half, 0.0)
    s = m + pltpu.roll(m, ATTN_HEAD_DIM, 1)
    return jnp.concatenate([s, s], axis=1)


def _attn_in_kernel(*refs, j, n_y, n_x, n_alias):
    t = PROJ
    n_res = n_y + n_x + 2 if n_y else n_x
    res_refs, refs = refs[:n_res], refs[n_res:]
    (g_ref, sh_ref, sc_ref, wf_ref, cos_ref, sin_ref), refs = refs[:6], refs[6 + n_alias:]
    if n_y:
        xo_ref, q_ref, k4_ref, v4_ref, z_ref, kc_ref, vc_ref, wo_ref, w_ref, h_ref, kv_ref = refs
        y_refs, wof_ref, x_refs, gate_ref = (res_refs[:n_y], res_refs[n_y], res_refs[n_y + 1:n_y + 1 + n_x],
                                             res_refs[n_y + 1 + n_x])
        _cast_weight_once(wof_ref, wo_ref)
        res = (y_refs, gate_ref)
    else:
        q_ref, k4_ref, v4_ref, z_ref, kc_ref, vc_ref, w_ref, h_ref, kv_ref = refs
        x_refs, res, xo_ref, wo_ref = res_refs, None, None, None
    i = pl.program_id(0)
    _cast_weight_once(wf_ref, w_ref)
    lane = lax.broadcasted_iota(jnp.int32, (PROJ_SUB, LANES), 1)
    first = (lane % (ATTN_HEAD_DIM // 2)) < ATTN_HEAD_DIM // 4
    scale = ATTN_HEAD_DIM ** -0.5 * LOG2_E
    k_lo = ATTN_WIDTH
    v_lo = ATTN_WIDTH + ATTN_KV_WIDTH
    z_lo = ATTN_WIDTH + 2 * ATTN_KV_WIDTH

    def proj_steps(sub):
        rows = slice(sub * PROJ_SUB, (sub + 1) * PROJ_SUB)

        def rope(a):
            rot = jnp.where(first, pltpu.roll(a, LANES - ATTN_HEAD_DIM // 4, 1),
                            pltpu.roll(a, ATTN_HEAD_DIM // 4, 1))
            return a * cos_ref[rows, :] + rot * sin_ref[rows, :]

        def rope_wide(a):
            return jnp.concatenate(
                [rope(a[:, s * LANES:(s + 1) * LANES]) for s in range(MXU_N // LANES)], axis=1)

        def proj(lo):
            return _dot(h_ref[rows, :], w_ref[:, lo:lo + MXU_N])

        def q_chunk(c):
            cols = slice(c * MXU_N, (c + 1) * MXU_N)
            q_ref[rows, cols] = (rope_wide(proj(c * MXU_N)) * scale).astype(BF16)

        def keys():
            k = rope_wide(proj(k_lo))
            kv_ref[rows, :ATTN_KV_WIDTH] = k
            for hh in range(ATTN_KV_HEADS):
                k4_ref[rows, hh * MXU_N:(hh + 1) * MXU_N] = _rep4(k, hh).astype(BF16)

        def values():
            v = proj(v_lo)
            kv_ref[rows, ATTN_KV_WIDTH:] = v
            for hh in range(ATTN_KV_HEADS):
                v4_ref[rows, hh * MXU_N:(hh + 1) * MXU_N] = _rep4(v, hh).astype(BF16)

        def z_chunk(c):
            cols = slice(c * MXU_N, (c + 1) * MXU_N)
            z_ref[rows, cols] = _silu(proj(z_lo + c * MXU_N)).astype(BF16)

        n_chunks = ATTN_WIDTH // MXU_N
        return ([functools.partial(q_chunk, c) for c in range(n_chunks)] + [keys, values]
                + [functools.partial(z_chunk, c) for c in range(n_chunks)])

    _run_halves(t, x_refs, res, g_ref, sh_ref, sc_ref, xo_ref, wo_ref, h_ref, proj_steps)

    @pl.when(i < t.n_prompt)
    def _():
        for s in range(t.tm // SEQ):
            kt = kv_ref[s * SEQ:(s + 1) * SEQ, :ATTN_KV_WIDTH].T
            vt = kv_ref[s * SEQ:(s + 1) * SEQ, ATTN_KV_WIDTH:].T
            if n_alias:
                kc_ref[s] = kt
                vc_ref[s] = vt
            else:
                for l in range(kc_ref.shape[1]):
                    kc_ref[s, l] = kt if l == j else jnp.zeros_like(kt)
                    vc_ref[s, l] = vt if l == j else jnp.zeros_like(vt)


def _attn_in(mixed, x_parts, g, mod, w_all, j, cos_t, sin_t, caches):
    t = PROJ
    n_in = 2 * ATTN_WIDTH + 2 * ATTN_KV_WIDTH
    per = t.tm // SEQ
    rope_spec = pl.BlockSpec(
        (t.tm, LANES), lambda i: (jnp.where(i < t.n_prompt, 0, 1 + t.seq_tile(i)), 0))
    wide = jax.ShapeDtypeStruct((N_TOK, ATTN_WIDTH), BF16)
    n_attn = w_all.shape[0]
    cache = jax.ShapeDtypeStruct((BATCH, n_attn, ATTN_KV_WIDTH, SEQ), F32)
    if caches:
        cache_spec = pl.BlockSpec((per, None, ATTN_KV_WIDTH, SEQ),
                                  lambda i: (jnp.minimum(i, t.n_prompt - 1), j, 0, 0))
    else:
        cache_spec = pl.BlockSpec((per, n_attn, ATTN_KV_WIDTH, SEQ),
                                  lambda i: (jnp.minimum(i, t.n_prompt - 1), 0, 0, 0))
    n_x = len(x_parts)
    if mixed is None:
        n_y, res_specs, res_args, res_out, res_ospecs, res_scratch = 0, _part_specs(t, x_parts, D_MODEL), x_parts, (), (), []
    else:
        n_y = len(mixed.y_parts)
        res_specs, res_args = _residual_specs(t, mixed, x_parts), _residual_args(mixed, x_parts)
        res_out = (jax.ShapeDtypeStruct((N_TOK, D_MODEL), F32),)
        res_ospecs = (t.rows(D_MODEL),)
        res_scratch = [pltpu.VMEM(mixed.w_out.shape[1:], BF16)]
    n_front = len(res_specs) + 6
    return pl.pallas_call(
        functools.partial(_attn_in_kernel, j=j, n_y=n_y, n_x=n_x, n_alias=len(caches)),
        out_shape=res_out + (wide, wide, wide, wide, cache, cache),
        grid=(t.n_tiles,),
        in_specs=res_specs + [
            _const_spec((1, D_MODEL)), t.mod(0), t.mod(1), _layer_spec(w_all, j),
            rope_spec, rope_spec] + [pl.BlockSpec(memory_space=pl.ANY)] * len(caches),
        out_specs=res_ospecs + (t.rows(ATTN_WIDTH),) * 4 + (cache_spec,) * 2,
        scratch_shapes=res_scratch + [pltpu.VMEM((D_MODEL, n_in), BF16), pltpu.VMEM((t.tm, D_MODEL), BF16),
                                      pltpu.VMEM((t.tm, 2 * ATTN_KV_WIDTH), F32)],
        input_output_aliases={n_front + c: len(res_out) + 4 + c for c in range(len(caches))},
        compiler_params=_params("arbitrary", vmem=FUSED_VMEM_LIMIT),
        name="attn_in",
    )(*res_args, g, mod, mod, w_all, cos_t, sin_t, *caches)


def _stack_group_queries(q):
    qf = q.astype(F32)
    chunk = lax.broadcasted_iota(jnp.int32, qf.shape, 1) // ATTN_HEAD_DIM
    return jnp.concatenate(
        [jnp.where(chunk == g, qf, 0.0) for g in range(ATTN_GROUP)], axis=0).astype(BF16)


def _gather_group_outputs(o, rows):
    chunk = lax.broadcasted_iota(jnp.int32, (rows, MXU_N), 1) // ATTN_HEAD_DIM
    acc = jnp.zeros((rows, MXU_N), F32)
    for g in range(ATTN_GROUP):
        acc = acc + jnp.where(chunk == g, o[g * rows:(g + 1) * rows], 0.0)
    return acc


def _sink_column(sink_ref, h, rows):
    grp = lax.broadcasted_iota(jnp.int32, (ATTN_GROUP * rows, 1), 0) // rows
    col = jnp.zeros((ATTN_GROUP * rows, 1), F32)
    for g in range(ATTN_GROUP):
        col = jnp.where(grp == g, sink_ref[h * ATTN_GROUP + g] * LOG2_E, col)
    return col


def _chunk_rows(dtype):
    chunk = lax.broadcasted_iota(jnp.int32, (1, MXU_N), 1) // ATTN_HEAD_DIM
    return [(chunk == g).astype(F32).astype(dtype) for g in range(ATTN_GROUP)]


def _block_diag_rows(x4):
    return jnp.concatenate([x4 * m for m in _chunk_rows(x4.dtype)], axis=0)


def _ctx_attn_kernel(sink_ref, q_ref, k4_ref, v4_ref, z_ref, y_ref):
    chunk = lax.broadcasted_iota(jnp.int32, (SEQ, MXU_N), 1) // ATTN_HEAD_DIM
    for b, h in itertools.product(range(CTX_ATTN_SEQS), range(ATTN_KV_HEADS)):
        rows = slice(b * SEQ, (b + 1) * SEQ)
        cols = slice(h * MXU_N, (h + 1) * MXU_N)
        s = _dot_nt(q_ref[rows, cols], _block_diag_rows(k4_ref[rows, cols]))
        inv = jnp.zeros((SEQ, MXU_N), F32)
        probs = []
        for g in range(ATTN_GROUP):
            sg = s[:, g * SEQ:(g + 1) * SEQ]
            sk = sink_ref[h * ATTN_GROUP + g] * LOG2_E
            m = jnp.maximum(jnp.max(sg, axis=1, keepdims=True), sk)
            e = jnp.exp2(sg - m)
            den = jnp.sum(e, axis=1, keepdims=True) + jnp.exp2(sk - m)
            probs.append(e.astype(BF16))
            inv = jnp.where(chunk == g, 1.0 / den, inv)
        o = _dot(jnp.concatenate(probs, axis=1), _block_diag_rows(v4_ref[rows, cols]))
        y_ref[rows, cols] = (o * inv * z_ref[rows, cols].astype(F32)).astype(BF16)


def _ctx_attn(sink, q, k4, v4, z):
    spec = pl.BlockSpec((CTX_ATTN_SEQS * SEQ, ATTN_WIDTH), lambda b: (b, 0))
    return pl.pallas_call(
        _ctx_attn_kernel,
        out_shape=jax.ShapeDtypeStruct((N_PROMPT_TOK, ATTN_WIDTH), BF16),
        grid=(BATCH // CTX_ATTN_SEQS,),
        in_specs=[pl.BlockSpec(memory_space=pltpu.SMEM), spec, spec, spec, spec],
        out_specs=spec,
        compiler_params=_params("arbitrary"),
        name="ctx_attn",
    )(sink, q, k4, v4, z)


LAT_STEP = Q_SUB * Q_BLOCK
LAT_PER_SEQ = DEC_SEQ // LAT_STEP
LAT_BLOCKS = DEC_BATCH * ATTN_KV_HEADS * LAT_PER_SEQ
assert LAT_BLOCKS % 2 == 0


def _lat_block(blk):
    return (blk // (ATTN_KV_HEADS * LAT_PER_SEQ), (blk // LAT_PER_SEQ) % ATTN_KV_HEADS, blk % LAT_PER_SEQ)


def _band_rows(qb):
    return pl.ds(pl.multiple_of(qb * Q_BLOCK, Q_BLOCK), 3 * Q_BLOCK)


def _pad_sequence(dst, src):
    zeros = jnp.zeros((Q_BLOCK, MXU_N), BF16)
    dst[0:Q_BLOCK, :] = zeros
    dst[Q_BLOCK:Q_BLOCK + DEC_SEQ, :] = src[...]
    dst[Q_BLOCK + DEC_SEQ:, :] = zeros


def _lat_attn_kernel(sink_ref, q_ref, k4_ref, v4_ref, kc_ref, vc_ref, z_ref, y_ref,
                     kp, vp, k4c, v4c, sc_a, sl_a, m_a, sc_b, sl_b, m_b, e_scr):
    t = pl.program_id(0)
    _, _, n1 = _lat_block(jnp.minimum(t, LAT_BLOCKS - 1))
    _, h0, n0 = _lat_block(jnp.maximum(t - 1, 0))

    @pl.when(n1 == 0)
    def _():
        _pad_sequence(kp, k4_ref)
        k4c[...] = jnp.concatenate([kc_ref[...]] * ATTN_GROUP, axis=0).astype(BF16)

    @pl.when(n0 == 0)
    def _():
        _pad_sequence(vp, v4_ref)
        v4c[...] = jnp.concatenate([vc_ref[...]] * ATTN_GROUP, axis=0).T.astype(BF16)

    stages = functools.partial(_lat_stages, sink_ref, q_ref, z_ref, y_ref, kp, vp, k4c, v4c, e_scr, n1, h0, n0)
    even = t % 2 == 0
    a, b = (sc_a, sl_a, m_a), (sc_b, sl_b, m_b)
    pl.when(t == 0)(functools.partial(stages, *a, *b, finish_previous=False))
    pl.when(jnp.logical_and(even, jnp.logical_and(t > 0, t < LAT_BLOCKS)))(functools.partial(stages, *a, *b))
    pl.when(t == LAT_BLOCKS)(functools.partial(stages, *a, *b, score_next=False))
    pl.when(jnp.logical_not(even))(functools.partial(stages, *b, *a))


def _lat_stages(sink_ref, q_ref, z_ref, y_ref, kp, vp, k4c, v4c, e_scr, n1, h0, n0,
                sc_w, sl_w, m_w, sc_r, sl_r, m_r, score_next=True, finish_previous=True):
    n_blocks = DEC_SEQ // Q_BLOCK
    rows4 = ATTN_GROUP * Q_BLOCK
    r = lax.broadcasted_iota(jnp.int32, (rows4, Q_BLOCK), 0) % Q_BLOCK
    c = lax.broadcasted_iota(jnp.int32, (rows4, Q_BLOCK), 1)
    in_left = c >= r
    in_right = c <= r
    sk = _sink_column(sink_ref, h0, Q_BLOCK)
    lane_slabs = lambda a: [a[:, lo:lo + LANES] for lo in range(0, a.shape[1], LANES)]

    def scores(sub):
        qb = n1 * Q_SUB + sub
        qs = _stack_group_queries(q_ref[sub * Q_BLOCK:(sub + 1) * Q_BLOCK, :])
        s_ctx = _dot(qs, k4c[...])
        s_lat = _dot_nt(qs, kp[_band_rows(qb), :])
        band = [jnp.where(jnp.logical_and(in_left, qb > 0), s_lat[:, :Q_BLOCK], NEG_INF),
                s_lat[:, Q_BLOCK:2 * Q_BLOCK],
                jnp.where(jnp.logical_and(in_right, qb < n_blocks - 1), s_lat[:, 2 * Q_BLOCK:], NEG_INF)]
        sc_w[sub] = s_ctx
        for k, s in enumerate(band):
            sl_w[sub, :, k * Q_BLOCK:(k + 1) * Q_BLOCK] = s
        m_w[sub] = functools.reduce(jnp.maximum, lane_slabs(s_ctx) + band)

    def softmax(sub):
        m = jnp.maximum(sk, jnp.max(m_r[sub], axis=1, keepdims=True))
        es = [jnp.exp2(s - m) for s in lane_slabs(sc_r[sub]) + lane_slabs(sl_r[sub])]
        for c, e in enumerate(es):
            e_scr[sub, :, c * LANES:(c + 1) * LANES] = e.astype(BF16)
        return jnp.exp2(sk - m) + jnp.sum(functools.reduce(jnp.add, es), axis=1, keepdims=True)

    def finish(sub, den):
        qrows = slice(sub * Q_BLOCK, (sub + 1) * Q_BLOCK)
        o = (_dot(e_scr[sub, :, :PAST_LEN], v4c[...])
             + _dot(e_scr[sub, :, PAST_LEN:], vp[_band_rows(n0 * Q_SUB + sub), :])) * (1.0 / den)
        acc = _gather_group_outputs(o, Q_BLOCK)
        y_ref[qrows, :] = (acc * z_ref[qrows, :].astype(F32)).astype(BF16)

    for sub in range(Q_SUB):
        den = softmax(sub) if finish_previous else None
        if score_next:
            scores(sub)
        if finish_previous:
            finish(sub, den)


def _lat_attn(sink, q, k4, v4, kc, vc, j, z):
    padded = DEC_SEQ + 2 * Q_BLOCK
    rows4 = ATTN_GROUP * Q_BLOCK
    row0 = N_PROMPT_TOK // LAT_STEP
    seq0 = N_PROMPT_TOK // DEC_SEQ
    stage1 = lambda t: _lat_block(jnp.minimum(t, LAT_BLOCKS - 1))
    stage2 = lambda t: _lat_block(jnp.maximum(t - 1, 0))

    def spec(shape, stage, index):
        return pl.BlockSpec(shape, lambda t: index(*stage(t)))

    tile = (LAT_STEP, MXU_N)
    seq = (DEC_SEQ, MXU_N)
    ctx = (None, None, None, ATTN_HEAD_DIM, PAST_LEN)
    return pl.pallas_call(
        _lat_attn_kernel,
        out_shape=jax.ShapeDtypeStruct((N_SAMPLE_TOK, ATTN_WIDTH), BF16),
        grid=(LAT_BLOCKS + 1,),
        in_specs=[pl.BlockSpec(memory_space=pltpu.SMEM),
                  spec(tile, stage1, lambda b, h, n: (row0 + b * LAT_PER_SEQ + n, h)),
                  spec(seq, stage1, lambda b, h, n: (seq0 + b, h)),
                  spec(seq, stage2, lambda b, h, n: (seq0 + b, h)),
                  spec(ctx, stage1, lambda b, h, n: (b, j, h, 0, 0)),
                  spec(ctx, stage2, lambda b, h, n: (b, j, h, 0, 0)),
                  spec(tile, stage2, lambda b, h, n: (row0 + b * LAT_PER_SEQ + n, h))],
        out_specs=spec(tile, stage2, lambda b, h, n: (b * LAT_PER_SEQ + n, h)),
        scratch_shapes=[pltpu.VMEM((padded, MXU_N), BF16), pltpu.VMEM((padded, MXU_N), BF16),
                        pltpu.VMEM((MXU_N, PAST_LEN), BF16), pltpu.VMEM((PAST_LEN, MXU_N), BF16),
                        pltpu.VMEM((Q_SUB, rows4, PAST_LEN), F32),
                        pltpu.VMEM((Q_SUB, rows4, 3 * Q_BLOCK), F32),
                        pltpu.VMEM((Q_SUB, rows4, LANES), F32),
                        pltpu.VMEM((Q_SUB, rows4, PAST_LEN), F32),
                        pltpu.VMEM((Q_SUB, rows4, 3 * Q_BLOCK), F32),
                        pltpu.VMEM((Q_SUB, rows4, LANES), F32),
                        pltpu.VMEM((Q_SUB, rows4, PAST_LEN + 3 * Q_BLOCK), BF16)],
        compiler_params=_params("arbitrary", vmem=VMEM_LIMIT),
        name="lat_attn",
    )(sink, q, k4, v4, kc, vc, z)


def _out_final_kernel(yp_ref, ys_ref, wf_ref, x_ref, gate_ref, fg_ref, op_ref, os_ref, w_ref):
    t = FINAL
    i = pl.program_id(0)
    _cast_weight_once(wf_ref, w_ref)

    def body(y_ref, o_ref):
        for lo in range(0, t.tm, PROJ_SUB):
            rows = slice(lo, lo + PROJ_SUB)
            o_ref[rows, :] = _rms(x_ref[rows, :] + gate_ref[...] * _dot(y_ref[rows, :], w_ref[...]), fg_ref[...])

    pl.when(i < t.n_prompt)(functools.partial(body, yp_ref, op_ref))
    pl.when(i >= t.n_prompt)(functools.partial(body, ys_ref, os_ref))


def _out_proj_final(mixed, x, final_g):
    t = FINAL
    return pl.pallas_call(
        _out_final_kernel,
        out_shape=(jax.ShapeDtypeStruct((N_PROMPT_TOK, D_MODEL), F32),
                   jax.ShapeDtypeStruct((N_SAMPLE_TOK, D_MODEL), F32)),
        grid=(t.n_tiles,),
        in_specs=_residual_specs(t, mixed, (x,)) + [_const_spec((1, D_MODEL))],
        out_specs=(t.prompt_rows(D_MODEL), t.sample_rows(D_MODEL)),
        scratch_shapes=[pltpu.VMEM(mixed.w_out.shape[1:], BF16)],
        compiler_params=_params("arbitrary", vmem=VMEM_LIMIT),
        name="out_proj_final",
    )(*_residual_args(mixed, (x,)), final_g)


def _pool_in_kernel(*refs, n_y, n_x):
    t = PROJ
    n_res = n_y + n_x + 2
    res_refs, refs = refs[:n_res], refs[n_res:]
    g_ref, sh_ref, sc_ref, wf_ref, wn_ref, xo_ref, u_ref, z_ref, wnb_ref, wo_ref, w_ref, h_ref = refs
    y_refs, wof_ref, x_refs, gate_ref = (res_refs[:n_y], res_refs[n_y], res_refs[n_y + 1:n_y + 1 + n_x],
                                         res_refs[n_y + 1 + n_x])
    _cast_weight_once(wof_ref, wo_ref)
    _cast_weight_once(wf_ref, w_ref)
    wnb_ref[...] = wn_ref[...].astype(BF16)

    def proj_steps(sub):
        rows = slice(sub * PROJ_SUB, (sub + 1) * PROJ_SUB)

        def chunk(c):
            def step():
                cols = slice(c * MXU_N, (c + 1) * MXU_N)
                u_ref[rows, cols] = _dot(h_ref[rows, :], w_ref[:, cols])
            return step

        def gate_chunk(c):
            def step():
                a = _dot(h_ref[rows, :], w_ref[:, D_MODEL + c * MXU_N:D_MODEL + (c + 1) * MXU_N])
                z_ref[rows, c * MXU_N:(c + 1) * MXU_N] = _silu(a).astype(BF16)
            return step

        n = D_MODEL // MXU_N
        return [chunk(c) for c in range(n)] + [gate_chunk(c) for c in range(n)]

    _run_halves(t, x_refs, (y_refs, gate_ref), g_ref, sh_ref, sc_ref, xo_ref, wo_ref, h_ref, proj_steps)


def _pool_in(mixed, x_parts, g, mod, w_all, j, w_next_all, j_next):
    t = PROJ
    slab = jax.ShapeDtypeStruct((N_TOK, D_MODEL), F32)
    k_next, n_next = w_next_all.shape[1:]
    return pl.pallas_call(
        functools.partial(_pool_in_kernel, n_y=len(mixed.y_parts), n_x=len(x_parts)),
        out_shape=(slab, slab, jax.ShapeDtypeStruct((N_TOK, D_MODEL), BF16),
                   jax.ShapeDtypeStruct((k_next, n_next), BF16)),
        grid=(t.n_tiles,),
        in_specs=_residual_specs(t, mixed, x_parts) + [
            _const_spec((1, D_MODEL)), t.mod(0), t.mod(1), _layer_spec(w_all, j),
            pl.BlockSpec((None, k_next // t.n_tiles, n_next), lambda i: (j_next, i, 0))],
        out_specs=(t.rows(D_MODEL),) * 3 + (pl.BlockSpec((k_next // t.n_tiles, n_next), lambda i: (i, 0)),),
        scratch_shapes=[pltpu.VMEM(mixed.w_out.shape[1:], BF16), pltpu.VMEM((D_MODEL, 2 * D_MODEL), BF16),
                        pltpu.VMEM((t.tm, D_MODEL), BF16)],
        compiler_params=_params("arbitrary", vmem=FUSED_VMEM_LIMIT),
        name="pool_in",
    )(*_residual_args(mixed, x_parts), g, mod, mod, w_all, w_next_all)


def _split_bf16(a):
    hi = a.astype(BF16)
    return hi, (a - hi.astype(F32)).astype(BF16)


def _band_ones(shape, lo, hi):
    d = lax.broadcasted_iota(jnp.int32, shape, 1) - lax.broadcasted_iota(jnp.int32, shape, 0)
    return jnp.logical_and(d >= lo, d <= hi).astype(F32).astype(BF16)


def _pool_ret_kernel(u_ref, up_ref, un_ref, z_ref, wgf_ref, ps_ref, wof_ref, x_ref, gate_ref,
                     g_ref, sh_ref, sc_ref, w_ref,
                     xo_ref, q_ref, kt_ref, v_ref, zr_ref,
                     y_ref, wg_ref, wo_ref, band_ref, wkt_ref, h_ref):
    t = PROJ
    i = pl.program_id(0)
    _cast_weight_once(wgf_ref, wg_ref)
    _cast_weight_once(wof_ref, wo_ref)

    @pl.when(i == 0)
    def _():
        for g, w in enumerate(POOL_WINDOWS):
            band_ref[g] = _band_ones((POOL_SUB, POOL_SUB), -(w // 2), w - 1 - w // 2)
        for r in range(0, RET_QK_WIDTH, MXU_N):
            wk = w_ref[:, RET_QK_WIDTH + r:RET_QK_WIDTH + r + MXU_N]
            wkt_ref[r:r + MXU_N, :] = wk.astype(F32).T.astype(BF16)

    is_dec = i >= t.n_prompt
    st = t.seq_tile(i)
    seq_len = jnp.where(is_dec, DEC_SEQ, SEQ)
    h = POOL_HALO
    n_sub = t.tm // POOL_SUB
    def pool_steps(sub):
        rows = slice(sub * POOL_SUB, (sub + 1) * POOL_SUB)
        if sub == 0:
            before = jnp.where(jnp.logical_and(is_dec, st != 0), up_ref[...], 0.0)
        else:
            before = jnp.where(is_dec, u_ref[sub * POOL_SUB - h:sub * POOL_SUB, :], 0.0)
        if sub == n_sub - 1:
            after = jnp.where(jnp.logical_and(is_dec, st != t.per_dec_seq - 1), un_ref[...], 0.0)
        else:
            after = jnp.where(is_dec, u_ref[(sub + 1) * POOL_SUB:(sub + 1) * POOL_SUB + h, :], 0.0)
        pos0 = jnp.where(is_dec, st * t.tm + sub * POOL_SUB, 0)
        return _pool_core_steps(u_ref.at[rows], before, after, pos0, seq_len, z_ref.at[rows], ps_ref,
                                wg_ref, band_ref, y_ref.at[rows])

    def mix_steps(sub):
        rows = slice(sub * POOL_SUB, (sub + 1) * POOL_SUB)

        def residual():
            xo_ref[rows, :] = x_ref[rows, :] + gate_ref[...] * _dot(y_ref[rows, :], wo_ref[...])

        def normalise():
            h_ref[rows, :] = _norm_mod(xo_ref[rows, :], g_ref, sh_ref, sc_ref)

        return pool_steps(sub) + [residual, normalise]

    for step in mix_steps(0):
        step()
    for sub in range(n_sub):
        rows = slice(sub * POOL_SUB, (sub + 1) * POOL_SUB)
        proj = _ret_in_steps(h_ref.at[rows], w_ref, wkt_ref, q_ref.at[rows],
                             kt_ref.at[pl.ds(sub * POOL_SUB // RET_CHUNK, POOL_SUB // RET_CHUNK)],
                             v_ref.at[rows], zr_ref.at[rows])
        _run_interleaved(proj, mix_steps(sub + 1) if sub + 1 < n_sub else [])


def _pool_core_steps(u_ref, before, after, pos0, seq_len, z_ref, ps_ref, wg_ref, band_ref, y_ref):
    n = u_ref.shape[0]
    h = POOL_HALO
    halo_hi, halo_lo = _split_bf16(jnp.concatenate([before, after], axis=0))
    pos = pos0 + lax.broadcasted_iota(jnp.int32, (n, 1), 0)
    rr = lax.broadcasted_iota(jnp.int32, (2 * h, 2 * h), 0)
    cc = lax.broadcasted_iota(jnp.int32, (2 * h, 2 * h), 1)

    def pooled(g, w):
        def step():
            left = w // 2
            right = w - 1 - left
            cols = slice(g * POOL_GROUP_DIM, (g + 1) * POOL_GROUP_DIM)
            u = u_ref[:, cols]
            s = _dot(band_ref[g], u.astype(BF16))
            top = jnp.logical_and(jnp.logical_and(rr < h, cc < h), cc - h >= rr - left)
            bot = jnp.logical_and(jnp.logical_and(rr >= h, cc >= h), cc - h <= rr - 2 * h + right)
            edge = jnp.logical_or(top, bot).astype(F32).astype(BF16)
            se = _dot(edge, halo_hi[:, cols]) + _dot(edge, halo_lo[:, cols])
            s = jnp.concatenate([s[:h] + se[:h], s[h:n - h], s[n - h:] + se[h:]], axis=0)
            cnt = (jnp.minimum(pos + right + 1, seq_len) - jnp.maximum(pos - left, 0)).astype(F32)
            y_ref[:, cols] = (s / cnt - u).astype(BF16)
        return step

    def mixed(g):
        def step():
            cols = slice(g * POOL_GROUP_DIM, (g + 1) * POOL_GROUP_DIM)
            yg = _dot(y_ref[:, cols], wg_ref[g]) * ps_ref[:, cols] * z_ref[:, cols].astype(F32)
            y_ref[:, cols] = yg.astype(BF16)
        return step

    return ([pooled(g, w) for g, w in enumerate(POOL_WINDOWS)]
            + [mixed(g) for g in range(len(POOL_WINDOWS))])


def _pool_ret(u, z, wg_all, ps, wo_all, j, x, mod_pool, g, mod, w_ret):
    t = PROJ
    per = t.tm // POOL_HALO
    n_halo = N_TOK // POOL_HALO
    n_in = 2 * RET_QK_WIDTH + 2 * RET_V_WIDTH
    kt_per = t.tm // RET_CHUNK
    return pl.pallas_call(
        _pool_ret_kernel,
        out_shape=(jax.ShapeDtypeStruct((N_TOK, D_MODEL), F32),
                   jax.ShapeDtypeStruct((N_TOK, RET_QK_WIDTH), BF16),
                   jax.ShapeDtypeStruct((N_TOK // RET_CHUNK, RET_QK_WIDTH, RET_CHUNK), BF16),
                   jax.ShapeDtypeStruct((N_TOK, RET_V_WIDTH), BF16),
                   jax.ShapeDtypeStruct((N_TOK, RET_V_WIDTH), BF16)),
        grid=(t.n_tiles,),
        in_specs=[
            t.rows(D_MODEL),
            pl.BlockSpec((POOL_HALO, D_MODEL), lambda i: (jnp.maximum(i * per - 1, 0), 0)),
            pl.BlockSpec((POOL_HALO, D_MODEL), lambda i: (jnp.minimum((i + 1) * per, n_halo - 1), 0)),
            t.rows(D_MODEL),
            _layer_spec(wg_all, j),
            _const_spec((1, D_MODEL)),
            _layer_spec(wo_all, j),
            t.rows(D_MODEL),
            t.mod(2),
            _const_spec((1, D_MODEL)), t.mod(0), t.mod(1),
            pl.BlockSpec((D_MODEL, n_in), lambda i: (0, 0), pipeline_mode=pl.Buffered(1)),
        ],
        out_specs=(t.rows(D_MODEL), t.rows(RET_QK_WIDTH),
                   pl.BlockSpec((kt_per, RET_QK_WIDTH, RET_CHUNK), lambda i: (i, 0, 0)),
                   t.rows(RET_V_WIDTH), t.rows(RET_V_WIDTH)),
        scratch_shapes=[pltpu.VMEM((t.tm, D_MODEL), BF16),
                        pltpu.VMEM((len(POOL_WINDOWS), POOL_GROUP_DIM, POOL_GROUP_DIM), BF16),
                        pltpu.VMEM((D_MODEL, D_MODEL), BF16),
                        pltpu.VMEM((len(POOL_WINDOWS), POOL_SUB, POOL_SUB), BF16),
                        pltpu.VMEM((RET_QK_WIDTH, D_MODEL), BF16),
                        pltpu.VMEM((t.tm, D_MODEL), BF16)],
        compiler_params=_params("arbitrary", vmem=FUSED_VMEM_LIMIT),
        name="pool_ret",
    )(u, u, u, z, wg_all, ps, wo_all, x, mod_pool, g, mod, mod, w_ret)


def _ret_in_steps(h_ref, w_ref, wkt_ref, q_ref, kt_ref, v_ref, z_ref):
    n_rows = h_ref.shape[0]

    def chunk(ref, lo, c, post=lambda a: a):
        def step():
            cols = slice(c * MXU_N, (c + 1) * MXU_N)
            ref[:, cols] = post(_dot(h_ref[...], w_ref[:, lo + c * MXU_N:lo + (c + 1) * MXU_N])).astype(BF16)
        return step

    def key_chunk(c):
        def step():
            rows = slice(c * MXU_N, (c + 1) * MXU_N)
            kt = (_dot_nt(wkt_ref[rows, :], h_ref[...]) * RET_KEY_DIM ** -0.5).astype(BF16)
            for cc in range(n_rows // RET_CHUNK):
                kt_ref[cc, rows, :] = kt[:, cc * RET_CHUNK:(cc + 1) * RET_CHUNK]
        return step

    steps = [chunk(q_ref, 0, c) for c in range(RET_QK_WIDTH // MXU_N)]
    steps += [key_chunk(c) for c in range(RET_QK_WIDTH // MXU_N)]
    steps += [chunk(v_ref, 2 * RET_QK_WIDTH, c) for c in range(RET_V_WIDTH // MXU_N)]
    steps += [chunk(z_ref, 2 * RET_QK_WIDTH + RET_V_WIDTH, c, _silu) for c in range(RET_V_WIDTH // MXU_N)]
    return steps


def _run_interleaved(main, side):
    done = 0
    for k, step in enumerate(main):
        step()
        due = (k + 1) * len(side) // len(main)
        for s in side[done:due]:
            s()
        done = due


def _pos(shape, axis):
    return lax.broadcasted_iota(jnp.int32, shape, axis).astype(F32)


def _ret_tables_kernel(lgf_ref, lgb_ref, decay_ref, row_ref, col_ref, cdec_ref):
    h = pl.program_id(0)
    lg_f = lgf_ref[h]
    lg_b = lgb_ref[h]
    c = RET_CHUNK
    diff = _pos((c, c), 0) - _pos((c, c), 1)
    fwd = jnp.where(diff >= 0, jnp.exp(jnp.maximum(diff, 0.0) * lg_f), 0.0)
    bwd = jnp.where(diff <= 0, jnp.exp(jnp.maximum(-diff, 0.0) * lg_b), 0.0)
    decay_ref[...] = fwd + bwd
    j = _pos((RET_TAB_ROWS, c), 1)
    row_ref[0] = jnp.exp((c - 1.0 - j) * lg_f)
    row_ref[1] = jnp.exp(j * lg_b)
    i = _pos((c, LANES), 0)
    col_ref[0] = jnp.exp((i + 1.0) * lg_f)
    col_ref[1] = jnp.exp((c - i) * lg_b)
    full = jnp.full((RET_TAB_ROWS, RET_VAL_DIM), float(c), F32)
    cdec_ref[0] = jnp.exp(full * lg_f)
    cdec_ref[1] = jnp.exp(full * lg_b)


def _ret_tables(lg_f, lg_b):
    smem = pl.BlockSpec(memory_space=pltpu.SMEM)
    c = RET_CHUNK
    shapes = ((c, c), (2, RET_TAB_ROWS, c), (2, c, LANES), (2, RET_TAB_ROWS, RET_VAL_DIM))
    return pl.pallas_call(
        _ret_tables_kernel,
        out_shape=tuple(jax.ShapeDtypeStruct((RET_HEADS,) + s, F32) for s in shapes),
        grid=(RET_HEADS,),
        in_specs=[smem, smem],
        out_specs=tuple(pl.BlockSpec((None,) + s, lambda h, n=len(s): (h,) + (0,) * n) for s in shapes),
        compiler_params=_params("arbitrary"),
        name="ret_tables",
    )(lg_f, lg_b)


def _group_norm_gate(o, gn, z):
    mu = jnp.mean(o, axis=-1, keepdims=True)
    var = jnp.mean(jnp.square(o - mu), axis=-1, keepdims=True)
    on = (o - mu) * lax.rsqrt(var + EPS)
    return (on * gn * z.astype(F32)).astype(BF16)


def _ret_ctx_kernel(q_ref, kt_ref, v_ref, z_ref, gn_ref, decay_ref, row_ref, y_ref, sf_ref, sb_ref):
    for s in range(RET_CTX_SEQS):
        rows = slice(s * SEQ, (s + 1) * SEQ)
        for h in range(RET_HEADS):
            kc = slice(h * RET_KEY_DIM, (h + 1) * RET_KEY_DIM)
            vc = slice(h * RET_VAL_DIM, (h + 1) * RET_VAL_DIM)
            kt = kt_ref[s, kc, :]
            v = v_ref[rows, vc]
            att = (_dot(q_ref[rows, kc], kt) * decay_ref[h]).astype(BF16)
            y_ref[rows, vc] = _group_norm_gate(_dot(att, v), gn_ref[:, vc], z_ref[rows, vc])
            ktf = kt.astype(F32)
            sf_ref[s, h] = _dot((ktf * row_ref[h, 0, 0:1, :]).astype(BF16), v)
            sb_ref[s, h] = _dot((ktf * row_ref[h, 1, 0:1, :]).astype(BF16), v)


def _ret_ctx(q, kt, v, z, gn, decay, row):
    c = RET_CHUNK
    n = RET_CTX_SEQS
    wide = lambda w: pl.BlockSpec((n * SEQ, w), lambda b: (b, 0))
    st_spec = pl.BlockSpec((n, RET_HEADS, RET_KEY_DIM, RET_VAL_DIM), lambda b: (b, 0, 0, 0))
    st_shape = jax.ShapeDtypeStruct((BATCH, RET_HEADS, RET_KEY_DIM, RET_VAL_DIM), F32)
    return pl.pallas_call(
        _ret_ctx_kernel,
        out_shape=(jax.ShapeDtypeStruct((N_PROMPT_TOK, RET_V_WIDTH), BF16), st_shape, st_shape),
        grid=(BATCH // n,),
        in_specs=[wide(RET_QK_WIDTH), pl.BlockSpec((n, RET_QK_WIDTH, c), lambda b: (b, 0, 0)),
                  wide(RET_V_WIDTH), wide(RET_V_WIDTH), _const_spec((1, RET_V_WIDTH)),
                  _const_spec((RET_HEADS, c, c)), _const_spec((RET_HEADS, 2, RET_TAB_ROWS, c))],
        out_specs=(wide(RET_V_WIDTH), st_spec, st_spec),
        compiler_params=_params("arbitrary", vmem=VMEM_LIMIT),
        name="ret_ctx",
    )(q, kt, v, z, gn, decay, row)


def _ret_lat_kernel(q_ref, kt_ref, v_ref, z_ref, gn_ref, decay_ref, row_ref, col_ref, cdec_ref,
                    s0f_ref, s0b_ref, y_ref, sf_all, sb_all, sf_acc, sb_acc):
    c = RET_CHUNK
    n_chunks = DEC_SEQ // c
    rows_of = lambda ci: pl.ds(pl.multiple_of(ci * c, c), c)

    sf_acc[...] = s0f_ref[...]
    sb_acc[...] = s0b_ref[...]

    def scan_step(i, carry):
        cf = i
        cb = n_chunks - 1 - i
        sf_all[cf] = sf_acc[...].astype(BF16)
        sb_all[cb] = sb_acc[...].astype(BF16)
        uf = _dot((kt_ref[cf].astype(F32) * row_ref[0, 0:1, :]).astype(BF16), v_ref[rows_of(cf), :])
        ub = _dot((kt_ref[cb].astype(F32) * row_ref[1, 0:1, :]).astype(BF16), v_ref[rows_of(cb), :])
        sf_acc[...] = sf_acc[...] * cdec_ref[0, 0:1, :] + uf
        sb_acc[...] = sb_acc[...] * cdec_ref[1, 0:1, :] + ub
        return carry

    lax.fori_loop(0, n_chunks, scan_step, 0, unroll=4)

    def out_step(ci, carry):
        rows = rows_of(ci)
        q = q_ref[rows, :]
        qf = q.astype(F32)
        qdec_f = jnp.concatenate([col_ref[0]] * (RET_KEY_DIM // LANES), axis=1)
        qdec_b = jnp.concatenate([col_ref[1]] * (RET_KEY_DIM // LANES), axis=1)
        att = (_dot(q, kt_ref[ci]) * decay_ref[...]).astype(BF16)
        o = (_dot(att, v_ref[rows, :])
             + _dot((qf * qdec_f).astype(BF16), sf_all[ci])
             + _dot((qf * qdec_b).astype(BF16), sb_all[ci]))
        y_ref[rows, :] = _group_norm_gate(o, gn_ref[...], z_ref[rows, :])
        return carry

    lax.fori_loop(0, n_chunks, out_step, 0, unroll=8)


def _ret_lat(q, kt, v, z, gn, decay, row, col, cdec, s0f, s0b):
    c = RET_CHUNK
    n_chunks = DEC_SEQ // c
    row0 = N_PROMPT_TOK // DEC_SEQ
    qk_spec = pl.BlockSpec((DEC_SEQ, RET_KEY_DIM), lambda b, h: (row0 + b, h))
    v_spec = pl.BlockSpec((DEC_SEQ, RET_VAL_DIM), lambda b, h: (row0 + b, h))
    st_spec = pl.BlockSpec((None, None, RET_KEY_DIM, RET_VAL_DIM), lambda b, h: (b, h, 0, 0))
    tab = lambda *s: pl.BlockSpec((None,) + s, lambda b, h: (h,) + (0,) * len(s))
    states = pltpu.VMEM((n_chunks, RET_KEY_DIM, RET_VAL_DIM), BF16)
    acc = pltpu.VMEM((RET_KEY_DIM, RET_VAL_DIM), F32)
    return pl.pallas_call(
        _ret_lat_kernel,
        out_shape=jax.ShapeDtypeStruct((N_SAMPLE_TOK, RET_V_WIDTH), BF16),
        grid=(DEC_BATCH, RET_HEADS),
        in_specs=[qk_spec,
                  pl.BlockSpec((n_chunks, RET_KEY_DIM, c), lambda b, h: (row0 + b, h, 0)),
                  v_spec, v_spec, pl.BlockSpec((1, RET_VAL_DIM), lambda b, h: (0, h)),
                  tab(c, c), tab(2, RET_TAB_ROWS, c), tab(2, c, LANES), tab(2, RET_TAB_ROWS, RET_VAL_DIM),
                  st_spec, st_spec],
        out_specs=pl.BlockSpec((DEC_SEQ, RET_VAL_DIM), lambda b, h: (b, h)),
        scratch_shapes=[states, states, acc, acc],
        compiler_params=_params("arbitrary", "arbitrary", vmem=VMEM_LIMIT),
        name="ret_lat",
    )(q, kt, v, z, gn, decay, row, col, cdec, s0f, s0b)


def _rope_tables(tm):
    n_rows = DEC_SEQ // GRID_W
    rows = jnp.repeat(jnp.arange(n_rows), GRID_W).astype(F32)
    cols = jnp.tile(jnp.arange(GRID_W), n_rows).astype(F32)
    half = ATTN_HEAD_DIM // 4
    inv = ROPE_BASE ** (-jnp.arange(half, dtype=F32) / half)
    ang_r = rows[:, None] * inv[None, :]
    ang_c = cols[:, None] * inv[None, :]
    cos = jnp.concatenate([jnp.cos(ang_r), jnp.cos(ang_r), jnp.cos(ang_c), jnp.cos(ang_c)], axis=-1)
    sin = jnp.concatenate([-jnp.sin(ang_r), jnp.sin(ang_r), -jnp.sin(ang_c), jnp.sin(ang_c)], axis=-1)
    cos = jnp.concatenate([jnp.ones((tm, ATTN_HEAD_DIM), F32), cos], axis=0)
    sin = jnp.concatenate([jnp.zeros((tm, ATTN_HEAD_DIM), F32), sin], axis=0)
    return jnp.tile(cos, (1, 2)), jnp.tile(sin, (1, 2))


def kernel(x_prompt, x_sample, cache_k, cache_v, state_fwd, state_bwd, c, c_ctx, norm_g, ada_w, ada_b, attn_w_in, attn_w_out, attn_sink, pool_w_in, pool_w_grp, pool_scale, pool_w_out, ret_w_in, ret_decay_fwd, ret_decay_bwd, ret_gn_g, ret_w_out, final_g):
    x_parts = (x_prompt.reshape(N_PROMPT_TOK, D_MODEL), x_sample.reshape(N_SAMPLE_TOK, D_MODEL))
    cond = jnp.concatenate([c_ctx[None, :], c,
                            jnp.zeros((N_COND - 1 - DEC_BATCH, D_MODEL), F32)], axis=0)
    mods = _ada_table(cond.T, ada_w, ada_b).reshape(DEPTH, N_COND, 1, 3 * D_MODEL)
    cos_t, sin_t = _rope_tables(PROJ.tm)

    to_kernel = lambda a: jnp.transpose(a, (0, 1, 3, 4, 2))
    from_kernel = lambda a: jnp.transpose(
        a.reshape(a.shape[0], a.shape[1], ATTN_KV_HEADS, ATTN_HEAD_DIM, a.shape[3]), (0, 1, 4, 2, 3))
    ctx_k, ctx_v = to_kernel(cache_k), to_kernel(cache_v)

    assert DEPTH % N_MIXERS == 1, "the layer stack must end on an attention layer"
    caches = ()
    new_sf = new_sb = None
    mixed = None
    for i in range(DEPTH):
        kind, j = i % N_MIXERS, i // N_MIXERS
        g = norm_g[i].reshape(1, D_MODEL)
        mod = mods[i]
        if kind == 0:
            outs = _attn_in(mixed, x_parts, g, mod, attn_w_in, j, cos_t, sin_t, tuple(caches))
            if mixed is not None:
                x_parts, outs = (outs[0],), outs[1:]
            q, k4, v4, z, *caches = outs
            y_parts = (_ctx_attn(attn_sink[j], q, k4, v4, z),
                       _lat_attn(attn_sink[j], q, k4, v4, ctx_k, ctx_v, j, z))
            mixed = Mixed(y_parts, attn_w_out, j, mod)
        elif kind == 1:
            x, u, z, w_ret = _pool_in(mixed, x_parts, g, mod, pool_w_in, j, ret_w_in, j)
            x_parts, mixed, pooled = (x,), None, (u, z, j, mod, w_ret)
        else:
            u, z, jp, mod_pool, w_ret = pooled
            lg_f = jax.nn.log_sigmoid(ret_decay_fwd[j].astype(F32))
            lg_b = jax.nn.log_sigmoid(ret_decay_bwd[j].astype(F32))
            gn = ret_gn_g[j].reshape(1, RET_V_WIDTH)
            x, q, kt, v, z = _pool_ret(u, z, pool_w_grp, pool_scale[jp].reshape(1, D_MODEL), pool_w_out, jp,
                                       x_parts[0], mod_pool, g, mod, w_ret)
            x_parts = (x,)
            decay, row, col, cdec = _ret_tables(lg_f, lg_b)
            y_ctx, new_sf, new_sb = _ret_ctx(q, kt, v, z, gn, decay, row)
            y_parts = (y_ctx, _ret_lat(q, kt, v, z, gn, decay, row, col, cdec,
                                       state_fwd[:, j], state_bwd[:, j]))
            mixed = Mixed(y_parts, ret_w_out, j, mod)
    y_prompt, y_sample = _out_proj_final(mixed, x_parts[0], final_g.reshape(1, D_MODEL))
    new_k, new_v = caches
    return (y_prompt.reshape(BATCH, SEQ, D_MODEL), y_sample.reshape(DEC_BATCH, DEC_SEQ, D_MODEL),
            from_kernel(new_k), from_kernel(new_v), new_sf[:, None], new_sb[:, None])
```

```python
import functools
import itertools
from typing import NamedTuple

import jax
import jax.numpy as jnp
from jax import lax
from jax.experimental import pallas as pl
from jax.experimental.pallas import tpu as pltpu

F32 = jnp.float32
BF16 = jnp.bfloat16

D_MODEL = 1024
BATCH = 16
SEQ = 256
DEPTH = 4
DEC_BATCH = 2
DEC_SEQ = 2048
PAST_LEN = 512
GRID_W = 64
N_MIXERS = 3
ATTN_HEADS = 16
ATTN_KV_HEADS = 4
ATTN_HEAD_DIM = 64
ATTN_GROUP = 4
ATTN_WIDTH = 1024
ATTN_KV_WIDTH = 256
WINDOW = 128
ROPE_BASE = 10000.0
POOL_WINDOWS = (2, 4, 8, 16)
POOL_GROUP_DIM = 256
RET_HEADS = 4
RET_KEY_DIM = 256
RET_VAL_DIM = 512
RET_QK_WIDTH = 1024
RET_V_WIDTH = 2048
EPS = 1e-6
NEG_INF = -1e30
LOG2_E = 1.4426950408889634

N_PROMPT_TOK = BATCH * SEQ
N_SAMPLE_TOK = DEC_BATCH * DEC_SEQ
N_TOK = N_PROMPT_TOK + N_SAMPLE_TOK
N_COND = 8
LANES = 128
MXU_N = 256
Q_BLOCK = 128
Q_SUB = 4
RET_CHUNK = 256
RET_TAB_ROWS = 8
RET_CTX_SEQS = 2
CTX_ATTN_SEQS = 2
POOL_HALO = 8
POOL_SUB = SEQ
PROJ_SUB = 256
VMEM_LIMIT = 48 * 1024 * 1024
FUSED_VMEM_LIMIT = 58 * 1024 * 1024


class Tiling(NamedTuple):
    tm: int

    @property
    def n_tiles(self):
        return N_TOK // self.tm

    @property
    def n_prompt(self):
        return N_PROMPT_TOK // self.tm

    @property
    def per_dec_seq(self):
        return DEC_SEQ // self.tm

    def cond(self, i):
        return jnp.where(i < self.n_prompt, 0, 1 + (i - self.n_prompt) // self.per_dec_seq)

    def seq_tile(self, i):
        return jnp.where(i < self.n_prompt, 0, (i - self.n_prompt) % self.per_dec_seq)

    def rows(self, width):
        return pl.BlockSpec((self.tm, width), lambda i: (i, 0))

    def prompt_rows(self, width):
        return pl.BlockSpec((self.tm, width), lambda i: (jnp.minimum(i, self.n_prompt - 1), 0))

    def sample_rows(self, width):
        return pl.BlockSpec((self.tm, width), lambda i: (jnp.maximum(i - self.n_prompt, 0), 0))

    def mod(self, part):
        return pl.BlockSpec((None, 1, D_MODEL), lambda i: (self.cond(i), 0, part))


PROJ = Tiling(512)
FINAL = Tiling(1024)


def _silu(z):
    hz = 0.5 * z
    return hz + hz * jnp.tanh(hz)


def _dot(a, b):
    return jnp.dot(a, b, preferred_element_type=F32)


def _dot_nt(a, b):
    return lax.dot_general(a, b, (((1,), (1,)), ((), ())), preferred_element_type=F32)


def _params(*sem, vmem=None):
    return pltpu.CompilerParams(dimension_semantics=sem, vmem_limit_bytes=vmem)


def _const_spec(shape):
    nd = len(shape)
    return pl.BlockSpec(shape, lambda *_: (0,) * nd, pipeline_mode=pl.Buffered(1))


def _part_specs(t, parts, width):
    if len(parts) == 1:
        return [t.rows(width)]
    return [t.prompt_rows(width), t.sample_rows(width)]


def _read_parts(t, refs, rows=slice(None)):
    if len(refs) == 1:
        return refs[0][rows, :]
    return jnp.where(pl.program_id(0) < t.n_prompt, refs[0][rows, :], refs[1][rows, :])


def _ada_kernel(cond_ref, w_ref, b_ref, o_ref):
    s = _silu(cond_ref[...])
    w = w_ref[...]
    rows = [jnp.sum(s[:, c:c + 1] * w, axis=0, keepdims=True) + b_ref[...] for c in range(1 + DEC_BATCH)]
    rows.append(jnp.zeros((N_COND - len(rows), w.shape[1]), F32))
    o_ref[...] = jnp.concatenate(rows, axis=0)


def _ada_table(cond, ada_w, ada_b):
    tn = 3 * D_MODEL // 2
    return pl.pallas_call(
        _ada_kernel,
        out_shape=jax.ShapeDtypeStruct((DEPTH, N_COND, 3 * D_MODEL), F32),
        grid=(DEPTH, 3 * D_MODEL // tn),
        in_specs=[
            pl.BlockSpec((D_MODEL, N_COND), lambda l, n: (0, 0)),
            pl.BlockSpec((None, D_MODEL, tn), lambda l, n: (l, 0, n)),
            pl.BlockSpec((None, 1, tn), lambda l, n: (l, 0, n)),
        ],
        out_specs=pl.BlockSpec((None, N_COND, tn), lambda l, n: (l, 0, n)),
        compiler_params=_params("arbitrary", "arbitrary", vmem=VMEM_LIMIT),
        name="ada_table",
    )(cond, ada_w, ada_b.reshape(DEPTH, 1, 3 * D_MODEL))


def _rms(x, g):
    return x * lax.rsqrt(jnp.mean(x * x, axis=-1, keepdims=True) + EPS) * g


def _norm_mod(x, g_ref, sh_ref, sc_ref):
    return (_rms(x, g_ref[...]) * (1.0 + sc_ref[...]) + sh_ref[...]).astype(BF16)


def _layer_spec(w, j):
    nd = w.ndim - 1
    return pl.BlockSpec((None,) + w.shape[1:], lambda *_: (j,) + (0,) * nd, pipeline_mode=pl.Buffered(1))


def _cast_weight_once(w_ref, wb_ref):
    @pl.when(pl.program_id(0) == 0)
    def _():
        rows = wb_ref.shape[-2]
        for r in range(0, rows, MXU_N):
            wb_ref[..., r:r + MXU_N, :] = w_ref[..., r:r + MXU_N, :].astype(BF16)


class Mixed(NamedTuple):
    y_parts: tuple
    w_out: jax.Array
    j: int
    mod: jax.Array


def _residual_specs(t, mixed, x_parts):
    k = mixed.w_out.shape[1]
    return (_part_specs(t, mixed.y_parts, k) + [_layer_spec(mixed.w_out, mixed.j)]
            + _part_specs(t, x_parts, D_MODEL) + [t.mod(2)])


def _residual_args(mixed, x_parts):
    return (*mixed.y_parts, mixed.w_out, *x_parts, mixed.mod)


def _run_halves(t, x_refs, res, g_ref, sh_ref, sc_ref, xo_ref, wo_ref, h_ref, proj_steps):
    n_sub = t.tm // PROJ_SUB
    rows_of = lambda sub: slice(sub * PROJ_SUB, (sub + 1) * PROJ_SUB)

    if res is not None:
        y_refs, gate_ref = res
        for sub in range(n_sub):
            rows = rows_of(sub)
            xo_ref[rows, :] = (_read_parts(t, x_refs, rows)
                               + gate_ref[...] * _dot(_read_parts(t, y_refs, rows), wo_ref[...]))

    def normalise(sub):
        rows = rows_of(sub)
        x = xo_ref[rows, :] if res is not None else _read_parts(t, x_refs, rows)
        h_ref[rows, :] = _norm_mod(x, g_ref, sh_ref, sc_ref)

    normalise(0)
    for sub in range(n_sub):
        side = [functools.partial(normalise, sub + 1)] if sub + 1 < n_sub else []
        _run_interleaved(proj_steps(sub), side)


def _rep4(a, h):
    half = a[:, (h // 2) * LANES:(h // 2 + 1) * LANES]
    lane = lax.broadcasted_iota(jnp.int32, half.shape, 1)
    keep = (lane < ATTN_HEAD_DIM) if h % 2 == 0 else (lane >= ATTN_HEAD_DIM)
    m = jnp.where(keep, half, 0.0)
    s = m + pltpu.roll(m, ATTN_HEAD_DIM, 1)
    return jnp.concatenate([s, s], axis=1)


def _attn_in_kernel(*refs, j, n_y, n_x, n_alias):
    t = PROJ
    n_res = n_y + n_x + 2 if n_y else n_x
    res_refs, refs = refs[:n_res], refs[n_res:]
    (g_ref, sh_ref, sc_ref, wf_ref, cos_ref, sin_ref), refs = refs[:6], refs[6 + n_alias:]
    if n_y:
        xo_ref, q_ref, k4_ref, v4_ref, z_ref, kc_ref, vc_ref, wo_ref, w_ref, h_ref, kv_ref = refs
        y_refs, wof_ref, x_refs, gate_ref = (res_refs[:n_y], res_refs[n_y], res_refs[n_y + 1:n_y + 1 + n_x],
                                             res_refs[n_y + 1 + n_x])
        _cast_weight_once(wof_ref, wo_ref)
        res = (y_refs, gate_ref)
    else:
        q_ref, k4_ref, v4_ref, z_ref, kc_ref, vc_ref, w_ref, h_ref, kv_ref = refs
        x_refs, res, xo_ref, wo_ref = res_refs, None, None, None
    i = pl.program_id(0)
    _cast_weight_once(wf_ref, w_ref)
    lane = lax.broadcasted_iota(jnp.int32, (PROJ_SUB, LANES), 1)
    first = (lane % (ATTN_HEAD_DIM // 2)) < ATTN_HEAD_DIM // 4
    scale = ATTN_HEAD_DIM ** -0.5 * LOG2_E
    k_lo = ATTN_WIDTH
    v_lo = ATTN_WIDTH + ATTN_KV_WIDTH
    z_lo = ATTN_WIDTH + 2 * ATTN_KV_WIDTH

    def proj_steps(sub):
        rows = slice(sub * PROJ_SUB, (sub + 1) * PROJ_SUB)

        def rope(a):
            rot = jnp.where(first, pltpu.roll(a, LANES - ATTN_HEAD_DIM // 4, 1),
                            pltpu.roll(a, ATTN_HEAD_DIM // 4, 1))
            return a * cos_ref[rows, :] + rot * sin_ref[rows, :]

        def rope_wide(a):
            return jnp.concatenate(
                [rope(a[:, s * LANES:(s + 1) * LANES]) for s in range(MXU_N // LANES)], axis=1)

        def proj(lo):
            return _dot(h_ref[rows, :], w_ref[:, lo:lo + MXU_N])

        def q_chunk(c):
            cols = slice(c * MXU_N, (c + 1) * MXU_N)
            q_ref[rows, cols] = (rope_wide(proj(c * MXU_N)) * scale).astype(BF16)

        def keys():
            k = rope_wide(proj(k_lo))
            kv_ref[rows, :ATTN_KV_WIDTH] = k
            for hh in range(ATTN_KV_HEADS):
                k4_ref[rows, hh * MXU_N:(hh + 1) * MXU_N] = _rep4(k, hh).astype(BF16)

        def values():
            v = proj(v_lo)
            kv_ref[rows, ATTN_KV_WIDTH:] = v
            for hh in range(ATTN_KV_HEADS):
                v4_ref[rows, hh * MXU_N:(hh + 1) * MXU_N] = _rep4(v, hh).astype(BF16)

        def z_chunk(c):
            cols = slice(c * MXU_N, (c + 1) * MXU_N)
            z_ref[rows, cols] = _silu(proj(z_lo + c * MXU_N)).astype(BF16)

        n_chunks = ATTN_WIDTH // MXU_N
        return ([functools.partial(q_chunk, c) for c in range(n_chunks)] + [keys, values]
                + [functools.partial(z_chunk, c) for c in range(n_chunks)])

    _run_halves(t, x_refs, res, g_ref, sh_ref, sc_ref, xo_ref, wo_ref, h_ref, proj_steps)

    @pl.when(i < t.n_prompt)
    def _():
        for s in range(t.tm // SEQ):
            kt = kv_ref[s * SEQ:(s + 1) * SEQ, :ATTN_KV_WIDTH].T
            vt = kv_ref[s * SEQ:(s + 1) * SEQ, ATTN_KV_WIDTH:].T
            if n_alias:
                kc_ref[s] = kt
                vc_ref[s] = vt
            else:
                for l in range(kc_ref.shape[1]):
                    kc_ref[s, l] = kt if l == j else jnp.zeros_like(kt)
                    vc_ref[s, l] = vt if l == j else jnp.zeros_like(vt)


def _attn_in(mixed, x_parts, g, mod, w_all, j, cos_t, sin_t, caches):
    t = PROJ
    n_in = 2 * ATTN_WIDTH + 2 * ATTN_KV_WIDTH
    per = t.tm // SEQ
    rope_spec = pl.BlockSpec(
        (t.tm, LANES), lambda i: (jnp.where(i < t.n_prompt, 0, 1 + t.seq_tile(i)), 0))
    wide = jax.ShapeDtypeStruct((N_TOK, ATTN_WIDTH), BF16)
    n_attn = w_all.shape[0]
    cache = jax.ShapeDtypeStruct((BATCH, n_attn, ATTN_KV_WIDTH, SEQ), F32)
    if caches:
        cache_spec = pl.BlockSpec((per, None, ATTN_KV_WIDTH, SEQ),
                                  lambda i: (jnp.minimum(i, t.n_prompt - 1), j, 0, 0))
    else:
        cache_spec = pl.BlockSpec((per, n_attn, ATTN_KV_WIDTH, SEQ),
                                  lambda i: (jnp.minimum(i, t.n_prompt - 1), 0, 0, 0))
    n_x = len(x_parts)
    if mixed is None:
        n_y, res_specs, res_args, res_out, res_ospecs, res_scratch = 0, _part_specs(t, x_parts, D_MODEL), x_parts, (), (), []
    else:
        n_y = len(mixed.y_parts)
        res_specs, res_args = _residual_specs(t, mixed, x_parts), _residual_args(mixed, x_parts)
        res_out = (jax.ShapeDtypeStruct((N_TOK, D_MODEL), F32),)
        res_ospecs = (t.rows(D_MODEL),)
        res_scratch = [pltpu.VMEM(mixed.w_out.shape[1:], BF16)]
    n_front = len(res_specs) + 6
    return pl.pallas_call(
        functools.partial(_attn_in_kernel, j=j, n_y=n_y, n_x=n_x, n_alias=len(caches)),
        out_shape=res_out + (wide, wide, wide, wide, cache, cache),
        grid=(t.n_tiles,),
        in_specs=res_specs + [
            _const_spec((1, D_MODEL)), t.mod(0), t.mod(1), _layer_spec(w_all, j),
            rope_spec, rope_spec] + [pl.BlockSpec(memory_space=pl.ANY)] * len(caches),
        out_specs=res_ospecs + (t.rows(ATTN_WIDTH),) * 4 + (cache_spec,) * 2,
        scratch_shapes=res_scratch + [pltpu.VMEM((D_MODEL, n_in), BF16), pltpu.VMEM((t.tm, D_MODEL), BF16),
                                      pltpu.VMEM((t.tm, 2 * ATTN_KV_WIDTH), F32)],
        input_output_aliases={n_front + c: len(res_out) + 4 + c for c in range(len(caches))},
        compiler_params=_params("arbitrary", vmem=FUSED_VMEM_LIMIT),
        name="attn_in",
    )(*res_args, g, mod, mod, w_all, cos_t, sin_t, *caches)


def _stack_group_queries(q):
    qf = q.astype(F32)
    chunk = lax.broadcasted_iota(jnp.int32, qf.shape, 1) // ATTN_HEAD_DIM
    return jnp.concatenate(
        [jnp.where(chunk == g, qf, 0.0) for g in range(ATTN_GROUP)], axis=0).astype(BF16)


def _gather_group_outputs(o, rows):
    chunk = lax.broadcasted_iota(jnp.int32, (rows, MXU_N), 1) // ATTN_HEAD_DIM
    acc = jnp.zeros((rows, MXU_N), F32)
    for g in range(ATTN_GROUP):
        acc = acc + jnp.where(chunk == g, o[g * rows:(g + 1) * rows], 0.0)
    return acc


def _sink_column(sink_ref, h, rows):
    grp = lax.broadcasted_iota(jnp.int32, (ATTN_GROUP * rows, 1), 0) // rows
    col = jnp.zeros((ATTN_GROUP * rows, 1), F32)
    for g in range(ATTN_GROUP):
        col = jnp.where(grp == g, sink_ref[h * ATTN_GROUP + g] * LOG2_E, col)
    return col


def _chunk_rows(dtype):
    chunk = lax.broadcasted_iota(jnp.int32, (1, MXU_N), 1) // ATTN_HEAD_DIM
    return [(chunk == g).astype(F32).astype(dtype) for g in range(ATTN_GROUP)]


def _block_diag_rows(x4):
    return jnp.concatenate([x4 * m for m in _chunk_rows(x4.dtype)], axis=0)


def _ctx_attn_kernel(sink_ref, q_ref, k4_ref, v4_ref, z_ref, y_ref):
    chunk = lax.broadcasted_iota(jnp.int32, (SEQ, MXU_N), 1) // ATTN_HEAD_DIM
    for b, h in itertools.product(range(CTX_ATTN_SEQS), range(ATTN_KV_HEADS)):
        rows = slice(b * SEQ, (b + 1) * SEQ)
        cols = slice(h * MXU_N, (h + 1) * MXU_N)
        s = _dot_nt(q_ref[rows, cols], _block_diag_rows(k4_ref[rows, cols]))
        inv = jnp.zeros((SEQ, MXU_N), F32)
        probs = []
        for g in range(ATTN_GROUP):
            sg = s[:, g * SEQ:(g + 1) * SEQ]
            sk = sink_ref[h * ATTN_GROUP + g] * LOG2_E
            m = jnp.maximum(jnp.max(sg, axis=1, keepdims=True), sk)
            e = jnp.exp2(sg - m)
            den = jnp.sum(e, axis=1, keepdims=True) + jnp.exp2(sk - m)
            probs.append(e.astype(BF16))
            inv = jnp.where(chunk == g, 1.0 / den, inv)
        o = _dot(jnp.concatenate(probs, axis=1), _block_diag_rows(v4_ref[rows, cols]))
        y_ref[rows, cols] = (o * inv * z_ref[rows, cols].astype(F32)).astype(BF16)


def _ctx_attn(sink, q, k4, v4, z):
    spec = pl.BlockSpec((CTX_ATTN_SEQS * SEQ, ATTN_WIDTH), lambda b: (b, 0))
    return pl.pallas_call(
        _ctx_attn_kernel,
        out_shape=jax.ShapeDtypeStruct((N_PROMPT_TOK, ATTN_WIDTH), BF16),
        grid=(BATCH // CTX_ATTN_SEQS,),
        in_specs=[pl.BlockSpec(memory_space=pltpu.SMEM), spec, spec, spec, spec],
        out_specs=spec,
        compiler_params=_params("arbitrary"),
        name="ctx_attn",
    )(sink, q, k4, v4, z)


LAT_STEP = Q_SUB * Q_BLOCK
LAT_PER_SEQ = DEC_SEQ // LAT_STEP
LAT_BLOCKS = DEC_BATCH * ATTN_KV_HEADS * LAT_PER_SEQ
assert LAT_BLOCKS % 2 == 0


def _lat_block(blk):
    return (blk // (ATTN_KV_HEADS * LAT_PER_SEQ), (blk // LAT_PER_SEQ) % ATTN_KV_HEADS, blk % LAT_PER_SEQ)


def _band_rows(qb):
    return pl.ds(pl.multiple_of(qb * Q_BLOCK, Q_BLOCK), 3 * Q_BLOCK)


def _pad_sequence(dst, src):
    zeros = jnp.zeros((Q_BLOCK, MXU_N), BF16)
    dst[0:Q_BLOCK, :] = zeros
    dst[Q_BLOCK:Q_BLOCK + DEC_SEQ, :] = src[...]
    dst[Q_BLOCK + DEC_SEQ:, :] = zeros


def _lat_attn_kernel(sink_ref, q_ref, k4_ref, v4_ref, kc_ref, vc_ref, z_ref, y_ref,
                     kp, vp, k4c, v4c, sc_a, sl_a, m_a, sc_b, sl_b, m_b, e_scr):
    t = pl.program_id(0)
    _, _, n1 = _lat_block(jnp.minimum(t, LAT_BLOCKS - 1))
    _, h0, n0 = _lat_block(jnp.maximum(t - 1, 0))

    @pl.when(n1 == 0)
    def _():
        _pad_sequence(kp, k4_ref)
        k4c[...] = jnp.concatenate([kc_ref[...]] * ATTN_GROUP, axis=0).astype(BF16)

    @pl.when(n0 == 0)
    def _():
        _pad_sequence(vp, v4_ref)
        v4c[...] = jnp.concatenate([vc_ref[...]] * ATTN_GROUP, axis=0).T.astype(BF16)

    stages = functools.partial(_lat_stages, sink_ref, q_ref, z_ref, y_ref, kp, vp, k4c, v4c, e_scr, n1, h0, n0)
    even = t % 2 == 0
    a, b = (sc_a, sl_a, m_a), (sc_b, sl_b, m_b)
    pl.when(t == 0)(functools.partial(stages, *a, *b, finish_previous=False))
    pl.when(jnp.logical_and(even, jnp.logical_and(t > 0, t < LAT_BLOCKS)))(functools.partial(stages, *a, *b))
    pl.when(t == LAT_BLOCKS)(functools.partial(stages, *a, *b, score_next=False))
    pl.when(jnp.logical_not(even))(functools.partial(stages, *b, *a))


def _lat_stages(sink_ref, q_ref, z_ref, y_ref, kp, vp, k4c, v4c, e_scr, n1, h0, n0,
                sc_w, sl_w, m_w, sc_r, sl_r, m_r, score_next=True, finish_previous=True):
    n_blocks = DEC_SEQ // Q_BLOCK
    rows4 = ATTN_GROUP * Q_BLOCK
    r = lax.broadcasted_iota(jnp.int32, (rows4, Q_BLOCK), 0) % Q_BLOCK
    c = lax.broadcasted_iota(jnp.int32, (rows4, Q_BLOCK), 1)
    in_left = c >= r
    in_right = c <= r
    sk = _sink_column(sink_ref, h0, Q_BLOCK)
    lane_slabs = lambda a: [a[:, lo:lo + LANES] for lo in range(0, a.shape[1], LANES)]

    def scores(sub):
        qb = n1 * Q_SUB + sub
        qs = _stack_group_queries(q_ref[sub * Q_BLOCK:(sub + 1) * Q_BLOCK, :])
        s_ctx = _dot(qs, k4c[...])
        s_lat = _dot_nt(qs, kp[_band_rows(qb), :])
        band = [jnp.where(jnp.logical_and(in_left, qb > 0), s_lat[:, :Q_BLOCK], NEG_INF),
                s_lat[:, Q_BLOCK:2 * Q_BLOCK],
                jnp.where(jnp.logical_and(in_right, qb < n_blocks - 1), s_lat[:, 2 * Q_BLOCK:], NEG_INF)]
        sc_w[sub] = s_ctx
        for k, s in enumerate(band):
            sl_w[sub, :, k * Q_BLOCK:(k + 1) * Q_BLOCK] = s
        m_w[sub] = functools.reduce(jnp.maximum, lane_slabs(s_ctx) + band)

    def softmax(sub):
        m = jnp.maximum(sk, jnp.max(m_r[sub], axis=1, keepdims=True))
        es = [jnp.exp2(s - m) for s in lane_slabs(sc_r[sub]) + lane_slabs(sl_r[sub])]
        for c, e in enumerate(es):
            e_scr[sub, :, c * LANES:(c + 1) * LANES] = e.astype(BF16)
        return jnp.exp2(sk - m) + jnp.sum(functools.reduce(jnp.add, es), axis=1, keepdims=True)

    def finish(sub, den):
        qrows = slice(sub * Q_BLOCK, (sub + 1) * Q_BLOCK)
        o = (_dot(e_scr[sub, :, :PAST_LEN], v4c[...])
             + _dot(e_scr[sub, :, PAST_LEN:], vp[_band_rows(n0 * Q_SUB + sub), :])) * (1.0 / den)
        acc = _gather_group_outputs(o, Q_BLOCK)
        y_ref[qrows, :] = (acc * z_ref[qrows, :].astype(F32)).astype(BF16)

    for sub in range(Q_SUB):
        den = softmax(sub) if finish_previous else None
        if score_next:
            scores(sub)
        if finish_previous:
            finish(sub, den)


def _lat_attn(sink, q, k4, v4, kc, vc, j, z):
    padded = DEC_SEQ + 2 * Q_BLOCK
    rows4 = ATTN_GROUP * Q_BLOCK
    row0 = N_PROMPT_TOK // LAT_STEP
    seq0 = N_PROMPT_TOK // DEC_SEQ
    stage1 = lambda t: _lat_block(jnp.minimum(t, LAT_BLOCKS - 1))
    stage2 = lambda t: _lat_block(jnp.maximum(t - 1, 0))

    def spec(shape, stage, index):
        return pl.BlockSpec(shape, lambda t: index(*stage(t)))

    tile = (LAT_STEP, MXU_N)
    seq = (DEC_SEQ, MXU_N)
    ctx = (None, None, None, ATTN_HEAD_DIM, PAST_LEN)
    return pl.pallas_call(
        _lat_attn_kernel,
        out_shape=jax.ShapeDtypeStruct((N_SAMPLE_TOK, ATTN_WIDTH), BF16),
        grid=(LAT_BLOCKS + 1,),
        in_specs=[pl.BlockSpec(memory_space=pltpu.SMEM),
                  spec(tile, stage1, lambda b, h, n: (row0 + b * LAT_PER_SEQ + n, h)),
                  spec(seq, stage1, lambda b, h, n: (seq0 + b, h)),
                  spec(seq, stage2, lambda b, h, n: (seq0 + b, h)),
                  spec(ctx, stage1, lambda b, h, n: (b, j, h, 0, 0)),
                  spec(ctx, stage2, lambda b, h, n: (b, j, h, 0, 0)),
                  spec(tile, stage2, lambda b, h, n: (row0 + b * LAT_PER_SEQ + n, h))],
        out_specs=spec(tile, stage2, lambda b, h, n: (b * LAT_PER_SEQ + n, h)),
        scratch_shapes=[pltpu.VMEM((padded, MXU_N), BF16), pltpu.VMEM((padded, MXU_N), BF16),
                        pltpu.VMEM((MXU_N, PAST_LEN), BF16), pltpu.VMEM((PAST_LEN, MXU_N), BF16),
                        pltpu.VMEM((Q_SUB, rows4, PAST_LEN), F32),
                        pltpu.VMEM((Q_SUB, rows4, 3 * Q_BLOCK), F32),
                        pltpu.VMEM((Q_SUB, rows4, LANES), F32),
                        pltpu.VMEM((Q_SUB, rows4, PAST_LEN), F32),
                        pltpu.VMEM((Q_SUB, rows4, 3 * Q_BLOCK), F32),
                        pltpu.VMEM((Q_SUB, rows4, LANES), F32),
                        pltpu.VMEM((Q_SUB, rows4, PAST_LEN + 3 * Q_BLOCK), BF16)],
        compiler_params=_params("arbitrary", vmem=VMEM_LIMIT),
        name="lat_attn",
    )(sink, q, k4, v4, kc, vc, z)


def _out_final_kernel(yp_ref, ys_ref, wf_ref, x_ref, gate_ref, fg_ref, op_ref, os_ref, w_ref):
    t = FINAL
    i = pl.program_id(0)
    _cast_weight_once(wf_ref, w_ref)

    def body(y_ref, o_ref):
        for lo in range(0, t.tm, PROJ_SUB):
            rows = slice(lo, lo + PROJ_SUB)
            o_ref[rows, :] = _rms(x_ref[rows, :] + gate_ref[...] * _dot(y_ref[rows, :], w_ref[...]), fg_ref[...])

    pl.when(i < t.n_prompt)(functools.partial(body, yp_ref, op_ref))
    pl.when(i >= t.n_prompt)(functools.partial(body, ys_ref, os_ref))


def _out_proj_final(mixed, x, final_g):
    t = FINAL
    return pl.pallas_call(
        _out_final_kernel,
        out_shape=(jax.ShapeDtypeStruct((N_PROMPT_TOK, D_MODEL), F32),
                   jax.ShapeDtypeStruct((N_SAMPLE_TOK, D_MODEL), F32)),
        grid=(t.n_tiles,),
        in_specs=_residual_specs(t, mixed, (x,)) + [_const_spec((1, D_MODEL))],
        out_specs=(t.prompt_rows(D_MODEL), t.sample_rows(D_MODEL)),
        scratch_shapes=[pltpu.VMEM(mixed.w_out.shape[1:], BF16)],
        compiler_params=_params("arbitrary", vmem=VMEM_LIMIT),
        name="out_proj_final",
    )(*_residual_args(mixed, (x,)), final_g)


def _pool_in_kernel(*refs, n_y, n_x):
    t = PROJ
    n_res = n_y + n_x + 2
    res_refs, refs = refs[:n_res], refs[n_res:]
    g_ref, sh_ref, sc_ref, wf_ref, wn_ref, xo_ref, u_ref, z_ref, wnb_ref, wo_ref, w_ref, h_ref = refs
    y_refs, wof_ref, x_refs, gate_ref = (res_refs[:n_y], res_refs[n_y], res_refs[n_y + 1:n_y + 1 + n_x],
                                         res_refs[n_y + 1 + n_x])
    _cast_weight_once(wof_ref, wo_ref)
    _cast_weight_once(wf_ref, w_ref)
    wnb_ref[...] = wn_ref[...].astype(BF16)

    def proj_steps(sub):
        rows = slice(sub * PROJ_SUB, (sub + 1) * PROJ_SUB)

        def chunk(c):
            def step():
                cols = slice(c * MXU_N, (c + 1) * MXU_N)
                u_ref[rows, cols] = _dot(h_ref[rows, :], w_ref[:, cols])
            return step

        def gate_chunk(c):
            def step():
                a = _dot(h_ref[rows, :], w_ref[:, D_MODEL + c * MXU_N:D_MODEL + (c + 1) * MXU_N])
                z_ref[rows, c * MXU_N:(c + 1) * MXU_N] = _silu(a).astype(BF16)
            return step

        n = D_MODEL // MXU_N
        return [chunk(c) for c in range(n)] + [gate_chunk(c) for c in range(n)]

    _run_halves(t, x_refs, (y_refs, gate_ref), g_ref, sh_ref, sc_ref, xo_ref, wo_ref, h_ref, proj_steps)


def _pool_in(mixed, x_parts, g, mod, w_all, j, w_next_all, j_next):
    t = PROJ
    slab = jax.ShapeDtypeStruct((N_TOK, D_MODEL), F32)
    k_next, n_next = w_next_all.shape[1:]
    return pl.pallas_call(
        functools.partial(_pool_in_kernel, n_y=len(mixed.y_parts), n_x=len(x_parts)),
        out_shape=(slab, slab, jax.ShapeDtypeStruct((N_TOK, D_MODEL), BF16),
                   jax.ShapeDtypeStruct((k_next, n_next), BF16)),
        grid=(t.n_tiles,),
        in_specs=_residual_specs(t, mixed, x_parts) + [
            _const_spec((1, D_MODEL)), t.mod(0), t.mod(1), _layer_spec(w_all, j),
            pl.BlockSpec((None, k_next // t.n_tiles, n_next), lambda i: (j_next, i, 0))],
        out_specs=(t.rows(D_MODEL),) * 3 + (pl.BlockSpec((k_next // t.n_tiles, n_next), lambda i: (i, 0)),),
        scratch_shapes=[pltpu.VMEM(mixed.w_out.shape[1:], BF16), pltpu.VMEM((D_MODEL, 2 * D_MODEL), BF16),
                        pltpu.VMEM((t.tm, D_MODEL), BF16)],
        compiler_params=_params("arbitrary", vmem=FUSED_VMEM_LIMIT),
        name="pool_in",
    )(*_residual_args(mixed, x_parts), g, mod, mod, w_all, w_next_all)


def _split_bf16(a):
    hi = a.astype(BF16)
    return hi, (a - hi.astype(F32)).astype(BF16)


def _band_ones(shape, lo, hi):
    d = lax.broadcasted_iota(jnp.int32, shape, 1) - lax.broadcasted_iota(jnp.int32, shape, 0)
    return jnp.logical_and(d >= lo, d <= hi).astype(F32).astype(BF16)


def _pool_ret_kernel(u_ref, up_ref, un_ref, z_ref, wgf_ref, ps_ref, wof_ref, x_ref, gate_ref,
                     g_ref, sh_ref, sc_ref, w_ref,
                     xo_ref, q_ref, kt_ref, v_ref, zr_ref,
                     y_ref, wg_ref, wo_ref, band_ref, wkt_ref, h_ref):
    t = PROJ
    i = pl.program_id(0)
    _cast_weight_once(wgf_ref, wg_ref)
    _cast_weight_once(wof_ref, wo_ref)

    @pl.when(i == 0)
    def _():
        for g, w in enumerate(POOL_WINDOWS):
            band_ref[g] = _band_ones((POOL_SUB, POOL_SUB), -(w // 2), w - 1 - w // 2)
        for r in range(0, RET_QK_WIDTH, MXU_N):
            wk = w_ref[:, RET_QK_WIDTH + r:RET_QK_WIDTH + r + MXU_N]
            wkt_ref[r:r + MXU_N, :] = wk.astype(F32).T.astype(BF16)

    is_dec = i >= t.n_prompt
    st = t.seq_tile(i)
    seq_len = jnp.where(is_dec, DEC_SEQ, SEQ)
    h = POOL_HALO
    n_sub = t.tm // POOL_SUB
    def pool_steps(sub):
        rows = slice(sub * POOL_SUB, (sub + 1) * POOL_SUB)
        if sub == 0:
            before = jnp.where(jnp.logical_and(is_dec, st != 0), up_ref[...], 0.0)
        else:
            before = jnp.where(is_dec, u_ref[sub * POOL_SUB - h:sub * POOL_SUB, :], 0.0)
        if sub == n_sub - 1:
            after = jnp.where(jnp.logical_and(is_dec, st != t.per_dec_seq - 1), un_ref[...], 0.0)
        else:
            after = jnp.where(is_dec, u_ref[(sub + 1) * POOL_SUB:(sub + 1) * POOL_SUB + h, :], 0.0)
        pos0 = jnp.where(is_dec, st * t.tm + sub * POOL_SUB, 0)
        return _pool_core_steps(u_ref.at[rows], before, after, pos0, seq_len, z_ref.at[rows], ps_ref,
                                wg_ref, band_ref, y_ref.at[rows])

    def mix_steps(sub):
        rows = slice(sub * POOL_SUB, (sub + 1) * POOL_SUB)

        def residual():
            xo_ref[rows, :] = x_ref[rows, :] + gate_ref[...] * _dot(y_ref[rows, :], wo_ref[...])

        def normalise():
            h_ref[rows, :] = _norm_mod(xo_ref[rows, :], g_ref, sh_ref, sc_ref)

        return pool_steps(sub) + [residual, normalise]

    for step in mix_steps(0):
        step()
    for sub in range(n_sub):
        rows = slice(sub * POOL_SUB, (sub + 1) * POOL_SUB)
        proj = _ret_in_steps(h_ref.at[rows], w_ref, wkt_ref, q_ref.at[rows],
                             kt_ref.at[pl.ds(sub * POOL_SUB // RET_CHUNK, POOL_SUB // RET_CHUNK)],
                             v_ref.at[rows], zr_ref.at[rows])
        _run_interleaved(proj, mix_steps(sub + 1) if sub + 1 < n_sub else [])


def _pool_core_steps(u_ref, before, after, pos0, seq_len, z_ref, ps_ref, wg_ref, band_ref, y_ref):
    n = u_ref.shape[0]
    h = POOL_HALO
    halo_hi, halo_lo = _split_bf16(jnp.concatenate([before, after], axis=0))
    pos = pos0 + lax.broadcasted_iota(jnp.int32, (n, 1), 0)
    rr = lax.broadcasted_iota(jnp.int32, (2 * h, 2 * h), 0)
    cc = lax.broadcasted_iota(jnp.int32, (2 * h, 2 * h), 1)

    def pooled(g, w):
        def step():
            left = w // 2
            right = w - 1 - left
            cols = slice(g * POOL_GROUP_DIM, (g + 1) * POOL_GROUP_DIM)
            u = u_ref[:, cols]
            s = _dot(band_ref[g], u.astype(BF16))
            top = jnp.logical_and(jnp.logical_and(rr < h, cc < h), cc - h >= rr - left)
            bot = jnp.logical_and(jnp.logical_and(rr >= h, cc >= h), cc - h <= rr - 2 * h + right)
            edge = jnp.logical_or(top, bot).astype(F32).astype(BF16)
            se = _dot(edge, halo_hi[:, cols]) + _dot(edge, halo_lo[:, cols])
            s = jnp.concatenate([s[:h] + se[:h], s[h:n - h], s[n - h:] + se[h:]], axis=0)
            cnt = (jnp.minimum(pos + right + 1, seq_len) - jnp.maximum(pos - left, 0)).astype(F32)
            y_ref[:, cols] = (s / cnt - u).astype(BF16)
        return step

    def mixed(g):
        def step():
            cols = slice(g * POOL_GROUP_DIM, (g + 1) * POOL_GROUP_DIM)
            yg = _dot(y_ref[:, cols], wg_ref[g]) * ps_ref[:, cols] * z_ref[:, cols].astype(F32)
            y_ref[:, cols] = yg.astype(BF16)
        return step

    return ([pooled(g, w) for g, w in enumerate(POOL_WINDOWS)]
            + [mixed(g) for g in range(len(POOL_WINDOWS))])


def _pool_ret(u, z, wg_all, ps, wo_all, j, x, mod_pool, g, mod, w_ret):
    t = PROJ
    per = t.tm // POOL_HALO
    n_halo = N_TOK // POOL_HALO
    n_in = 2 * RET_QK_WIDTH + 2 * RET_V_WIDTH
    kt_per = t.tm // RET_CHUNK
    return pl.pallas_call(
        _pool_ret_kernel,
        out_shape=(jax.ShapeDtypeStruct((N_TOK, D_MODEL), F32),
                   jax.ShapeDtypeStruct((N_TOK, RET_QK_WIDTH), BF16),
                   jax.ShapeDtypeStruct((N_TOK // RET_CHUNK, RET_QK_WIDTH, RET_CHUNK), BF16),
                   jax.ShapeDtypeStruct((N_TOK, RET_V_WIDTH), BF16),
                   jax.ShapeDtypeStruct((N_TOK, RET_V_WIDTH), BF16)),
        grid=(t.n_tiles,),
        in_specs=[
            t.rows(D_MODEL),
            pl.BlockSpec((POOL_HALO, D_MODEL), lambda i: (jnp.maximum(i * per - 1, 0), 0)),
            pl.BlockSpec((POOL_HALO, D_MODEL), lambda i: (jnp.minimum((i + 1) * per, n_halo - 1), 0)),
            t.rows(D_MODEL),
            _layer_spec(wg_all, j),
            _const_spec((1, D_MODEL)),
            _layer_spec(wo_all, j),
            t.rows(D_MODEL),
            t.mod(2),
            _const_spec((1, D_MODEL)), t.mod(0), t.mod(1),
            pl.BlockSpec((D_MODEL, n_in), lambda i: (0, 0), pipeline_mode=pl.Buffered(1)),
        ],
        out_specs=(t.rows(D_MODEL), t.rows(RET_QK_WIDTH),
                   pl.BlockSpec((kt_per, RET_QK_WIDTH, RET_CHUNK), lambda i: (i, 0, 0)),
                   t.rows(RET_V_WIDTH), t.rows(RET_V_WIDTH)),
        scratch_shapes=[pltpu.VMEM((t.tm, D_MODEL), BF16),
                        pltpu.VMEM((len(POOL_WINDOWS), POOL_GROUP_DIM, POOL_GROUP_DIM), BF16),
                        pltpu.VMEM((D_MODEL, D_MODEL), BF16),
                        pltpu.VMEM((len(POOL_WINDOWS), POOL_SUB, POOL_SUB), BF16),
                        pltpu.VMEM((RET_QK_WIDTH, D_MODEL), BF16),
                        pltpu.VMEM((t.tm, D_MODEL), BF16)],
        compiler_params=_params("arbitrary", vmem=FUSED_VMEM_LIMIT),
        name="pool_ret",
    )(u, u, u, z, wg_all, ps, wo_all, x, mod_pool, g, mod, mod, w_ret)


def _ret_in_steps(h_ref, w_ref, wkt_ref, q_ref, kt_ref, v_ref, z_ref):
    n_rows = h_ref.shape[0]

    def chunk(ref, lo, c, post=lambda a: a):
        def step():
            cols = slice(c * MXU_N, (c + 1) * MXU_N)
            ref[:, cols] = post(_dot(h_ref[...], w_ref[:, lo + c * MXU_N:lo + (c + 1) * MXU_N])).astype(BF16)
        return step

    def key_chunk(c):
        def step():
            rows = slice(c * MXU_N, (c + 1) * MXU_N)
            kt = (_dot_nt(wkt_ref[rows, :], h_ref[...]) * RET_KEY_DIM ** -0.5).astype(BF16)
            for cc in range(n_rows // RET_CHUNK):
                kt_ref[cc, rows, :] = kt[:, cc * RET_CHUNK:(cc + 1) * RET_CHUNK]
        return step

    steps = [chunk(q_ref, 0, c) for c in range(RET_QK_WIDTH // MXU_N)]
    steps += [key_chunk(c) for c in range(RET_QK_WIDTH // MXU_N)]
    steps += [chunk(v_ref, 2 * RET_QK_WIDTH, c) for c in range(RET_V_WIDTH // MXU_N)]
    steps += [chunk(z_ref, 2 * RET_QK_WIDTH + RET_V_WIDTH, c, _silu) for c in range(RET_V_WIDTH // MXU_N)]
    return steps


def _run_interleaved(main, side):
    done = 0
    for k, step in enumerate(main):
        step()
        due = (k + 1) * len(side) // len(main)
        for s in side[done:due]:
            s()
        done = due


def _pos(shape, axis):
    return lax.broadcasted_iota(jnp.int32, shape, axis).astype(F32)


def _ret_tables_kernel(lgf_ref, lgb_ref, decay_ref, row_ref, col_ref, cdec_ref):
    h = pl.program_id(0)
    lg_f = lgf_ref[h]
    lg_b = lgb_ref[h]
    c = RET_CHUNK
    diff = _pos((c, c), 0) - _pos((c, c), 1)
    fwd = jnp.where(diff >= 0, jnp.exp(jnp.maximum(diff, 0.0) * lg_f), 0.0)
    bwd = jnp.where(diff <= 0, jnp.exp(jnp.maximum(-diff, 0.0) * lg_b), 0.0)
    decay_ref[...] = fwd + bwd
    j = _pos((RET_TAB_ROWS, c), 1)
    row_ref[0] = jnp.exp((c - 1.0 - j) * lg_f)
    row_ref[1] = jnp.exp(j * lg_b)
    i = _pos((c, LANES), 0)
    col_ref[0] = jnp.exp((i + 1.0) * lg_f)
    col_ref[1] = jnp.exp((c - i) * lg_b)
    full = jnp.full((RET_TAB_ROWS, RET_VAL_DIM), float(c), F32)
    cdec_ref[0] = jnp.exp(full * lg_f)
    cdec_ref[1] = jnp.exp(full * lg_b)


def _ret_tables(lg_f, lg_b):
    smem = pl.BlockSpec(memory_space=pltpu.SMEM)
    c = RET_CHUNK
    shapes = ((c, c), (2, RET_TAB_ROWS, c), (2, c, LANES), (2, RET_TAB_ROWS, RET_VAL_DIM))
    return pl.pallas_call(
        _ret_tables_kernel,
        out_shape=tuple(jax.ShapeDtypeStruct((RET_HEADS,) + s, F32) for s in shapes),
        grid=(RET_HEADS,),
        in_specs=[smem, smem],
        out_specs=tuple(pl.BlockSpec((None,) + s, lambda h, n=len(s): (h,) + (0,) * n) for s in shapes),
        compiler_params=_params("arbitrary"),
        name="ret_tables",
    )(lg_f, lg_b)


def _group_norm_gate(o, gn, z):
    mu = jnp.mean(o, axis=-1, keepdims=True)
    var = jnp.mean(jnp.square(o - mu), axis=-1, keepdims=True)
    on = (o - mu) * lax.rsqrt(var + EPS)
    return (on * gn * z.astype(F32)).astype(BF16)


def _ret_ctx_kernel(q_ref, kt_ref, v_ref, z_ref, gn_ref, decay_ref, row_ref, y_ref, sf_ref, sb_ref):
    for s in range(RET_CTX_SEQS):
        rows = slice(s * SEQ, (s + 1) * SEQ)
        for h in range(RET_HEADS):
            kc = slice(h * RET_KEY_DIM, (h + 1) * RET_KEY_DIM)
            vc = slice(h * RET_VAL_DIM, (h + 1) * RET_VAL_DIM)
            kt = kt_ref[s, kc, :]
            v = v_ref[rows, vc]
            att = (_dot(q_ref[rows, kc], kt) * decay_ref[h]).astype(BF16)
            y_ref[rows, vc] = _group_norm_gate(_dot(att, v), gn_ref[:, vc], z_ref[rows, vc])
            ktf = kt.astype(F32)
            sf_ref[s, h] = _dot((ktf * row_ref[h, 0, 0:1, :]).astype(BF16), v)
            sb_ref[s, h] = _dot((ktf * row_ref[h, 1, 0:1, :]).astype(BF16), v)


def _ret_ctx(q, kt, v, z, gn, decay, row):
    c = RET_CHUNK
    n = RET_CTX_SEQS
    wide = lambda w: pl.BlockSpec((n * SEQ, w), lambda b: (b, 0))
    st_spec = pl.BlockSpec((n, RET_HEADS, RET_KEY_DIM, RET_VAL_DIM), lambda b: (b, 0, 0, 0))
    st_shape = jax.ShapeDtypeStruct((BATCH, RET_HEADS, RET_KEY_DIM, RET_VAL_DIM), F32)
    return pl.pallas_call(
        _ret_ctx_kernel,
        out_shape=(jax.ShapeDtypeStruct((N_PROMPT_TOK, RET_V_WIDTH), BF16), st_shape, st_shape),
        grid=(BATCH // n,),
        in_specs=[wide(RET_QK_WIDTH), pl.BlockSpec((n, RET_QK_WIDTH, c), lambda b: (b, 0, 0)),
                  wide(RET_V_WIDTH), wide(RET_V_WIDTH), _const_spec((1, RET_V_WIDTH)),
                  _const_spec((RET_HEADS, c, c)), _const_spec((RET_HEADS, 2, RET_TAB_ROWS, c))],
        out_specs=(wide(RET_V_WIDTH), st_spec, st_spec),
        compiler_params=_params("arbitrary", vmem=VMEM_LIMIT),
        name="ret_ctx",
    )(q, kt, v, z, gn, decay, row)


def _ret_lat_kernel(q_ref, kt_ref, v_ref, z_ref, gn_ref, decay_ref, row_ref, col_ref, cdec_ref,
                    s0f_ref, s0b_ref, y_ref, sf_all, sb_all, sf_acc, sb_acc):
    c = RET_CHUNK
    n_chunks = DEC_SEQ // c
    rows_of = lambda ci: pl.ds(pl.multiple_of(ci * c, c), c)

    sf_acc[...] = s0f_ref[...]
    sb_acc[...] = s0b_ref[...]

    def scan_step(i, carry):
        cf = i
        cb = n_chunks - 1 - i
        sf_all[cf] = sf_acc[...].astype(BF16)
        sb_all[cb] = sb_acc[...].astype(BF16)
        uf = _dot((kt_ref[cf].astype(F32) * row_ref[0, 0:1, :]).astype(BF16), v_ref[rows_of(cf), :])
        ub = _dot((kt_ref[cb].astype(F32) * row_ref[1, 0:1, :]).astype(BF16), v_ref[rows_of(cb), :])
        sf_acc[...] = sf_acc[...] * cdec_ref[0, 0:1, :] + uf
        sb_acc[...] = sb_acc[...] * cdec_ref[1, 0:1, :] + ub
        return carry

    lax.fori_loop(0, n_chunks, scan_step, 0, unroll=4)

    def out_step(ci, carry):
        rows = rows_of(ci)
        q = q_ref[rows, :]
        qf = q.astype(F32)
        qdec_f = jnp.concatenate([col_ref[0]] * (RET_KEY_DIM // LANES), axis=1)
        qdec_b = jnp.concatenate([col_ref[1]] * (RET_KEY_DIM // LANES), axis=1)
        att = (_dot(q, kt_ref[ci]) * decay_ref[...]).astype(BF16)
        o = (_dot(att, v_ref[rows, :])
             + _dot((qf * qdec_f).astype(BF16), sf_all[ci])
             + _dot((qf * qdec_b).astype(BF16), sb_all[ci]))
        y_ref[rows, :] = _group_norm_gate(o, gn_ref[...], z_ref[rows, :])
        return carry

    lax.fori_loop(0, n_chunks, out_step, 0, unroll=8)


def _ret_lat(q, kt, v, z, gn, decay, row, col, cdec, s0f, s0b):
    c = RET_CHUNK
    n_chunks = DEC_SEQ // c
    row0 = N_PROMPT_TOK // DEC_SEQ
    qk_spec = pl.BlockSpec((DEC_SEQ, RET_KEY_DIM), lambda b, h: (row0 + b, h))
    v_spec = pl.BlockSpec((DEC_SEQ, RET_VAL_DIM), lambda b, h: (row0 + b, h))
    st_spec = pl.BlockSpec((None, None, RET_KEY_DIM, RET_VAL_DIM), lambda b, h: (b, h, 0, 0))
    tab = lambda *s: pl.BlockSpec((None,) + s, lambda b, h: (h,) + (0,) * len(s))
    states = pltpu.VMEM((n_chunks, RET_KEY_DIM, RET_VAL_DIM), BF16)
    acc = pltpu.VMEM((RET_KEY_DIM, RET_VAL_DIM), F32)
    return pl.pallas_call(
        _ret_lat_kernel,
        out_shape=jax.ShapeDtypeStruct((N_SAMPLE_TOK, RET_V_WIDTH), BF16),
        grid=(DEC_BATCH, RET_HEADS),
        in_specs=[qk_spec,
                  pl.BlockSpec((n_chunks, RET_KEY_DIM, c), lambda b, h: (row0 + b, h, 0)),
                  v_spec, v_spec, pl.BlockSpec((1, RET_VAL_DIM), lambda b, h: (0, h)),
                  tab(c, c), tab(2, RET_TAB_ROWS, c), tab(2, c, LANES), tab(2, RET_TAB_ROWS, RET_VAL_DIM),
                  st_spec, st_spec],
        out_specs=pl.BlockSpec((DEC_SEQ, RET_VAL_DIM), lambda b, h: (b, h)),
        scratch_shapes=[states, states, acc, acc],
        compiler_params=_params("arbitrary", "arbitrary", vmem=VMEM_LIMIT),
        name="ret_lat",
    )(q, kt, v, z, gn, decay, row, col, cdec, s0f, s0b)


def _rope_tables(tm):
    n_rows = DEC_SEQ // GRID_W
    rows = jnp.repeat(jnp.arange(n_rows), GRID_W).astype(F32)
    cols = jnp.tile(jnp.arange(GRID_W), n_rows).astype(F32)
    half = ATTN_HEAD_DIM // 4
    inv = ROPE_BASE ** (-jnp.arange(half, dtype=F32) / half)
    ang_r = rows[:, None] * inv[None, :]
    ang_c = cols[:, None] * inv[None, :]
    cos = jnp.concatenate([jnp.cos(ang_r), jnp.cos(ang_r), jnp.cos(ang_c), jnp.cos(ang_c)], axis=-1)
    sin = jnp.concatenate([-jnp.sin(ang_r), jnp.sin(ang_r), -jnp.sin(ang_c), jnp.sin(ang_c)], axis=-1)
    cos = jnp.concatenate([jnp.ones((tm, ATTN_HEAD_DIM), F32), cos], axis=0)
    sin = jnp.concatenate([jnp.zeros((tm, ATTN_HEAD_DIM), F32), sin], axis=0)
    return jnp.tile(cos, (1, 2)), jnp.tile(sin, (1, 2))


def kernel(x_prompt, x_sample, cache_k, cache_v, state_fwd, state_bwd, c, c_ctx, norm_g, ada_w, ada_b, attn_w_in, attn_w_out, attn_sink, pool_w_in, pool_w_grp, pool_scale, pool_w_out, ret_w_in, ret_decay_fwd, ret_decay_bwd, ret_gn_g, ret_w_out, final_g):
    x_parts = (x_prompt.reshape(N_PROMPT_TOK, D_MODEL), x_sample.reshape(N_SAMPLE_TOK, D_MODEL))
    cond = jnp.concatenate([c_ctx[None, :], c,
                            jnp.zeros((N_COND - 1 - DEC_BATCH, D_MODEL), F32)], axis=0)
    mods = _ada_table(cond.T, ada_w, ada_b).reshape(DEPTH, N_COND, 1, 3 * D_MODEL)
    cos_t, sin_t = _rope_tables(PROJ.tm)

    to_kernel = lambda a: jnp.transpose(a, (0, 1, 3, 4, 2))
    from_kernel = lambda a: jnp.transpose(
        a.reshape(a.shape[0], a.shape[1], ATTN_KV_HEADS, ATTN_HEAD_DIM, a.shape[3]), (0, 1, 4, 2, 3))
    ctx_k, ctx_v = to_kernel(cache_k), to_kernel(cache_v)

    assert DEPTH % N_MIXERS == 1, "the layer stack must end on an attention layer"
    caches = ()
    new_sf = new_sb = None
    mixed = None
    for i in range(DEPTH):
        kind, j = i % N_MIXERS, i // N_MIXERS
        g = norm_g[i].reshape(1, D_MODEL)
        mod = mods[i]
        if kind == 0:
            outs = _attn_in(mixed, x_parts, g, mod, attn_w_in, j, cos_t, sin_t, tuple(caches))
            if mixed is not None:
                x_parts, outs = (outs[0],), outs[1:]
            q, k4, v4, z, *caches = outs
            y_parts = (_ctx_attn(attn_sink[j], q, k4, v4, z),
                       _lat_attn(attn_sink[j], q, k4, v4, ctx_k, ctx_v, j, z))
            mixed = Mixed(y_parts, attn_w_out, j, mod)
        elif kind == 1:
            x, u, z, w_ret = _pool_in(mixed, x_parts, g, mod, pool_w_in, j, ret_w_in, j)
            x_parts, mixed, pooled = (x,), None, (u, z, j, mod, w_ret)
        else:
            u, z, jp, mod_pool, w_ret = pooled
            lg_f = jax.nn.log_sigmoid(ret_decay_fwd[j].astype(F32))
            lg_b = jax.nn.log_sigmoid(ret_decay_bwd[j].astype(F32))
            gn = ret_gn_g[j].reshape(1, RET_V_WIDTH)
            x, q, kt, v, z = _pool_ret(u, z, pool_w_grp, pool_scale[jp].reshape(1, D_MODEL), pool_w_out, jp,
                                       x_parts[0], mod_pool, g, mod, w_ret)
            x_parts = (x,)
            decay, row, col, cdec = _ret_tables(lg_f, lg_b)
            y_ctx, new_sf, new_sb = _ret_ctx(q, kt, v, z, gn, decay, row)
            y_parts = (y_ctx, _ret_lat(q, kt, v, z, gn, decay, row, col, cdec,
                                       state_fwd[:, j], state_bwd[:, j]))
            mixed = Mixed(y_parts, ret_w_out, j, mod)
    y_prompt, y_sample = _out_proj_final(mixed, x_parts[0], final_g.reshape(1, D_MODEL))
    new_k, new_v = caches
    return (y_prompt.reshape(BATCH, SEQ, D_MODEL), y_sample.reshape(DEC_BATCH, DEC_SEQ, D_MODEL),
            from_kernel(new_k), from_kernel(new_v), new_sf[:, None], new_sb[:, None])
```
